```python
import math
import jax, jax.numpy as jnp
from jax import lax
import numpy as np

D_MODEL = 1024
BATCH = 2
SEQ = 8192
DEPTH = 1

N_MEM = 256
D_MIX = D_MODEL
N_HEADS_A = 8
N_KV_A = 2
HEAD_DIM_A = 64
GQA_GROUP = N_HEADS_A // N_KV_A
WINDOW = 128
BLOCK = 128
NUM_BUCKETS = 32
MAX_EXACT = NUM_BUCKETS // 2
MAX_DISTANCE = 128
N_HEADS_B = 4
HEAD_DIM_B = 128
CHUNK = 128
CONV_WIDTH = 4
N_HEADS_X = 4
HEAD_DIM_X = D_MODEL // N_HEADS_X
N_GROUPS = 4
EXPERTS_PER_GROUP = 8
N_EXPERTS = N_GROUPS * EXPERTS_PER_GROUP
TOP_K = 2
D_EXPERT = 512
EPS = 1e-6
NEG_INF = -1e30

W_A_Q = N_HEADS_A * HEAD_DIM_A
W_A_KV = N_KV_A * HEAD_DIM_A
W_B = N_HEADS_B * HEAD_DIM_B
IN_WIDTHS = (W_A_Q, W_A_KV, W_A_KV, W_B, W_B, W_B, W_B, N_HEADS_B, N_HEADS_B)
D_IN = sum(IN_WIDTHS)
IN_SPLITS = tuple(int(c) for c in np.cumsum(IN_WIDTHS)[:-1])

kernel_name = 'hymba_swa_mlstm_hmoe_layer'


def rmsnorm(x, g):
    xf = x.astype(jnp.float32)
    y = xf * lax.rsqrt(jnp.mean(xf * xf, axis=-1, keepdims=True) + EPS)
    return (y * g.astype(jnp.float32)).astype(x.dtype)


def t5_band_bias(table):
    i = jnp.arange(BLOCK)[:, None]
    j = jnp.arange(2 * BLOCK)[None, :]
    n = jnp.maximum(i + BLOCK - j, 0)
    nf = jnp.maximum(n, 1).astype(jnp.float32)
    large = MAX_EXACT + (jnp.log(nf / MAX_EXACT) / math.log(MAX_DISTANCE / MAX_EXACT)
                         * (NUM_BUCKETS - MAX_EXACT)).astype(jnp.int32)
    large = jnp.minimum(large, NUM_BUCKETS - 1)
    bucket = jnp.where(n < MAX_EXACT, n, large)
    bias = table.astype(jnp.float32)[bucket]
    return jnp.transpose(bias, (2, 0, 1)).reshape(N_KV_A, GQA_GROUP, BLOCK, 2 * BLOCK)


def swa_sink_attention(q, k, v, sinks, rel_bias):
    B, S = q.shape[0], q.shape[1]
    nb = S // BLOCK
    qb = q.reshape(B, nb, BLOCK, N_KV_A, GQA_GROUP, HEAD_DIM_A)

    def band(t):
        tp = jnp.pad(t, ((0, 0), (BLOCK, 0), (0, 0), (0, 0)))
        tp = tp.reshape(B, nb + 1, BLOCK, N_KV_A, HEAD_DIM_A)
        return jnp.concatenate([tp[:, :-1], tp[:, 1:]], axis=2)

    kb, vb = band(k), band(v)
    s = jnp.einsum('bnqkgd,bnskd->bnkgqs', qb, kb).astype(jnp.float32) * (HEAD_DIM_A ** -0.5)
    s = s + rel_bias
    i = jnp.arange(BLOCK)[:, None]
    j = jnp.arange(2 * BLOCK)[None, :]
    d = i + BLOCK - j
    band_ok = (d >= 0) & (d < WINDOW)
    first_ok = (jnp.arange(nb)[:, None, None] > 0) | (j >= BLOCK)[None]
    mask = (band_ok[None] & first_ok)[None, :, None, None]
    s = jnp.where(mask, s, NEG_INF)
    sink = sinks.astype(jnp.float32).reshape(N_KV_A, GQA_GROUP, 1)
    m = jnp.maximum(s.max(axis=-1), sink)
    p = jnp.exp(s - m[..., None])
    denom = p.sum(axis=-1) + jnp.exp(sink - m)
    p = (p / denom[..., None]).astype(v.dtype)
    o = jnp.einsum('bnkgqs,bnskd->bnqkgd', p, vb)
    return o.reshape(B, S, W_A_Q)


def causal_depthwise_conv(u, w, b):
    y = lax.conv_general_dilated(u, w.astype(u.dtype), window_strides=(1,),
                                 padding=((CONV_WIDTH - 1, 0),),
                                 dimension_numbers=('NWC', 'WIO', 'NWC'),
                                 feature_group_count=u.shape[-1])
    return y + b.astype(u.dtype)


def mlstm_chunkwise(q, k, v, i_pre, logf):
    B, S = q.shape[0], q.shape[1]
    nc = S // CHUNK

    def to_chunks(t):
        t = t.reshape((B, nc, CHUNK) + t.shape[2:])
        return jnp.moveaxis(jnp.moveaxis(t, 1, 0), 3, 2)

    xs = (to_chunks(q * (HEAD_DIM_B ** -0.5)), to_chunks(k), to_chunks(v),
          to_chunks(i_pre), to_chunks(logf))
    tri = jnp.tril(jnp.ones((CHUNK, CHUNK), dtype=bool))

    def step(carry, xc):
        C, n, m = carry
        qc, kc, vc, ic, fc = xc
        b = jnp.cumsum(fc, axis=-1)
        logD = b[..., :, None] - b[..., None, :] + ic[..., None, :]
        logD = jnp.where(tri, logD, NEG_INF)
        inter = b + m[..., None]
        m_t = jnp.maximum(inter, logD.max(axis=-1))
        a_inter = jnp.exp(inter - m_t)
        sc = jnp.einsum('bhtd,bhsd->bhts', qc, kc) * jnp.exp(logD - m_t[..., None])
        num = jnp.einsum('bhts,bhsd->bhtd', sc, vc) + a_inter[..., None] * jnp.einsum('bhvk,bhtk->bhtv', C, qc)
        den = sc.sum(axis=-1) + a_inter * jnp.einsum('bhk,bhtk->bht', n, qc)
        h = num / jnp.maximum(jnp.abs(den), jnp.exp(-m_t))[..., None]
        bL = b[..., -1]
        logw = bL[..., None] - b + ic
        m_new = jnp.maximum(bL + m, logw.max(axis=-1))
        w = jnp.exp(logw - m_new[..., None])
        decay = jnp.exp(bL + m - m_new)
        C_new = decay[..., None, None] * C + jnp.einsum('bhs,bhsv,bhsk->bhvk', w, vc, kc)
        n_new = decay[..., None] * n + jnp.einsum('bhs,bhsk->bhk', w, kc)
        return (C_new, n_new, m_new), h

    H = N_HEADS_B
    init = (jnp.zeros((B, H, HEAD_DIM_B, HEAD_DIM_B), jnp.float32),
            jnp.zeros((B, H, HEAD_DIM_B), jnp.float32),
            jnp.zeros((B, H), jnp.float32))
    _, hs = lax.scan(step, init, xs)
    hs = jnp.moveaxis(jnp.moveaxis(hs, 0, 1), 3, 2)
    return hs.reshape(B, S, H, HEAD_DIM_B)


def setup_inputs(seed: int = 0) -> dict:
    key = jax.random.key(seed)
    ks = jax.random.split(key, 32)
    f32 = jnp.float32

    def nrm(k, shape, scale):
        return jax.random.normal(k, shape, f32) * scale

    L = DEPTH
    return {
        'x': nrm(ks[0], (BATCH, SEQ, D_MODEL), 1.0),
        'mem': nrm(ks[1], (BATCH, N_MEM, D_MODEL), 1.0),
        'rel_bias_table': nrm(ks[2], (NUM_BUCKETS, N_HEADS_A), 0.5),
        'norm_mix': 1.0 + nrm(ks[3], (L, D_MODEL), 0.02),
        'w_in': nrm(ks[4], (L, D_MODEL, D_IN), D_MODEL ** -0.5),
        'attn_sinks': nrm(ks[5], (L, N_HEADS_A), 0.5),
        'conv_w': nrm(ks[6], (L, CONV_WIDTH, 1, 2 * W_B), CONV_WIDTH ** -0.5),
        'conv_b': nrm(ks[7], (L, 2 * W_B), 0.02),
        'gate_bias_i': nrm(ks[8], (L, N_HEADS_B), 0.1),
        'gate_bias_f': jnp.linspace(3.0, 6.0, N_HEADS_B, dtype=f32)[None] + nrm(ks[9], (L, N_HEADS_B), 0.1),
        'mlstm_norm': 1.0 + nrm(ks[10], (L, W_B), 0.02),
        'w_out': nrm(ks[11], (L, D_MIX, D_MODEL), D_MIX ** -0.5),
        'norm_cross': 1.0 + nrm(ks[12], (L, D_MODEL), 0.02),
        'norm_mem': 1.0 + nrm(ks[13], (L, D_MODEL), 0.02),
        'w_cq': nrm(ks[14], (L, D_MODEL, D_MODEL), D_MODEL ** -0.5),
        'w_ck': nrm(ks[15], (L, D_MODEL, D_MODEL), D_MODEL ** -0.5),
        'w_cv': nrm(ks[16], (L, D_MODEL, D_MODEL), D_MODEL ** -0.5),
        'w_co': nrm(ks[17], (L, D_MODEL, D_MODEL), D_MODEL ** -0.5),
        'norm_moe': 1.0 + nrm(ks[18], (L, D_MODEL), 0.02),
        'w_router_group': nrm(ks[19], (L, D_MODEL, N_GROUPS), D_MODEL ** -0.5),
        'b_router_group': nrm(ks[20], (L, N_GROUPS), 0.01),
        'w_router_expert': nrm(ks[21], (L, D_MODEL, N_EXPERTS), D_MODEL ** -0.5),
        'b_router_expert': nrm(ks[22], (L, N_EXPERTS), 0.01),
        'w_exp_gate': nrm(ks[23], (L, N_EXPERTS, D_MODEL, D_EXPERT), D_MODEL ** -0.5),
        'w_exp_up': nrm(ks[24], (L, N_EXPERTS, D_MODEL, D_EXPERT), D_MODEL ** -0.5),
        'w_exp_down': nrm(ks[25], (L, N_EXPERTS, D_EXPERT, D_MODEL), D_EXPERT ** -0.5),
        'norm_final': 1.0 + nrm(ks[26], (D_MODEL,), 0.02),
    }


def reference(x, mem, rel_bias_table, norm_mix, w_in, attn_sinks, conv_w, conv_b,
              gate_bias_i, gate_bias_f, mlstm_norm, w_out, norm_cross, norm_mem,
              w_cq, w_ck, w_cv, w_co, norm_moe, w_router_group, b_router_group,
              w_router_expert, b_router_expert, w_exp_gate, w_exp_up, w_exp_down,
              norm_final):
    f32 = jnp.float32
    B, S = x.shape[0], x.shape[1]
    T = B * S
    rel_bias = t5_band_bias(rel_bias_table)

    for l in range(DEPTH):
        h = rmsnorm(x, norm_mix[l])
        proj = h @ w_in[l]
        qa, ka, va, qb, kb, vb, ob, ib, fb = jnp.split(proj, IN_SPLITS, axis=-1)
        out_a = swa_sink_attention(qa.reshape(B, S, N_HEADS_A, HEAD_DIM_A),
                                   ka.reshape(B, S, N_KV_A, HEAD_DIM_A),
                                   va.reshape(B, S, N_KV_A, HEAD_DIM_A),
                                   attn_sinks[l], rel_bias)
        qk_b = jax.nn.silu(causal_depthwise_conv(jnp.concatenate([qb, kb], axis=-1),
                                                 conv_w[l], conv_b[l]))
        qb_c, kb_c = jnp.split(qk_b, 2, axis=-1)
        i_pre = ib.astype(f32) + gate_bias_i[l].astype(f32)
        logf = jax.nn.log_sigmoid(fb.astype(f32) + gate_bias_f[l].astype(f32))
        hb = mlstm_chunkwise(qb_c.astype(f32).reshape(B, S, N_HEADS_B, HEAD_DIM_B),
                             kb_c.astype(f32).reshape(B, S, N_HEADS_B, HEAD_DIM_B),
                             vb.astype(f32).reshape(B, S, N_HEADS_B, HEAD_DIM_B),
                             i_pre, logf)
        hb = jax.nn.sigmoid(ob.astype(f32)).reshape(B, S, N_HEADS_B, HEAD_DIM_B) * hb
        hb = hb * lax.rsqrt(jnp.mean(hb * hb, axis=-1, keepdims=True) + EPS)
        hb = hb * mlstm_norm[l].astype(f32).reshape(N_HEADS_B, HEAD_DIM_B)
        mix = jnp.concatenate([out_a, hb.reshape(B, S, W_B).astype(x.dtype)], axis=-1)
        x = x + mix @ w_out[l]

        hc = rmsnorm(x, norm_cross[l])
        hm = rmsnorm(mem, norm_mem[l])
        cq = (hc @ w_cq[l]).reshape(B, S, N_HEADS_X, HEAD_DIM_X)
        ck = (hm @ w_ck[l]).reshape(B, N_MEM, N_HEADS_X, HEAD_DIM_X)
        cv = (hm @ w_cv[l]).reshape(B, N_MEM, N_HEADS_X, HEAD_DIM_X)
        cs = jnp.einsum('bqhd,bkhd->bhqk', cq, ck).astype(f32) * (HEAD_DIM_X ** -0.5)
        cp = jax.nn.softmax(cs, axis=-1).astype(cv.dtype)
        co = jnp.einsum('bhqk,bkhd->bqhd', cp, cv).reshape(B, S, D_MODEL)
        x = x + co @ w_co[l]

        hz = rmsnorm(x, norm_moe[l]).reshape(T, D_MODEL)
        g_logits = (hz @ w_router_group[l]).astype(f32) + b_router_group[l].astype(f32)
        p_g = jax.nn.softmax(g_logits, axis=-1)
        g_prob, g_idx = lax.top_k(p_g, 1)
        e_logits = ((hz @ w_router_expert[l]).astype(f32) + b_router_expert[l].astype(f32))
        e_logits = e_logits.reshape(T, N_GROUPS, EXPERTS_PER_GROUP)
        e_in = jnp.take_along_axis(e_logits, g_idx[:, :, None], axis=1)[:, 0]
        p_e = jax.nn.softmax(e_in, axis=-1)
        w_top, e_idx = lax.top_k(p_e, TOP_K)
        gates = g_prob * (w_top / jnp.sum(w_top, axis=-1, keepdims=True))
        expert_ids = g_idx * EXPERTS_PER_GROUP + e_idx
        comb = jnp.sum(jax.nn.one_hot(expert_ids, N_EXPERTS, dtype=f32) * gates[..., None],
                       axis=1).astype(hz.dtype)
        y = jnp.zeros_like(hz)
        for e in range(N_EXPERTS):
            he = jax.nn.silu(hz @ w_exp_gate[l, e]) * (hz @ w_exp_up[l, e])
            y = y + comb[:, e:e + 1] * (he @ w_exp_down[l, e])
        x = x + y.reshape(B, S, D_MODEL)

    return rmsnorm(x, norm_final)
```

```python
import functools
import math

import jax
import jax.numpy as jnp
import numpy as np
from jax import lax
from jax.experimental import pallas as pl
from jax.experimental.pallas import tpu as pltpu

F32 = jnp.float32
BF16 = jnp.bfloat16
U32 = jnp.uint32
I32 = jnp.int32

D_MODEL = 1024
N_MEM = 256
N_HEADS_A = 8
N_KV_A = 2
HEAD_DIM_A = 64
BLOCK = 128
WINDOW = 128
NUM_BUCKETS = 32
MAX_EXACT = NUM_BUCKETS // 2
MAX_DISTANCE = 128
N_HEADS_B = 4
HEAD_DIM_B = 128
CHUNK = 128
CONV_WIDTH = 4
N_HEADS_X = 4
HEAD_DIM_X = D_MODEL // N_HEADS_X
N_GROUPS = 4
EXPERTS_PER_GROUP = 8
N_EXPERTS = N_GROUPS * EXPERTS_PER_GROUP
TOP_K = 2
D_EXPERT = 512
EPS = 1e-6
NEG_INF = -1e30

W_A_Q = N_HEADS_A * HEAD_DIM_A
W_A_KV = N_KV_A * HEAD_DIM_A
W_B = N_HEADS_B * HEAD_DIM_B
C_QA = 0
C_KVA = C_QA + W_A_Q
C_QKB = C_KVA + 2 * W_A_KV
C_VB = C_QKB + 2 * W_B
C_OB = C_VB + W_B
C_GATE = C_OB + W_B
D_IN = C_GATE + 2 * N_HEADS_B

LANES = 128
GATE_ROWS = 8
HALF = D_MODEL // 2

TM_INPROJ = 512
TM_MID = 256
TM_FINAL = 512
TM_EXPERT = 256
ISSUE_UNROLL = 8

VMEM_LIMIT = 48 * 1024 * 1024


def _rms(xf, g):
    return xf * lax.rsqrt(jnp.mean(xf * xf, axis=-1, keepdims=True) + EPS) * g


def _pack_halves(v):
    b = pltpu.bitcast(v.astype(BF16).astype(F32), U32)
    return (b[:, :HALF] >> 16) | b[:, HALF:]


def _unpack_halves(p):
    lo = pltpu.bitcast(p << 16, F32)
    hi = pltpu.bitcast(p & jnp.uint32(0xFFFF0000), F32)
    return jnp.concatenate([lo, hi], axis=-1)


def _inproj_kernel(x_ref, g_ref, w_ref, wgt_ref, qa_ref, kva_ref, qkb_ref, vb_ref, ob_ref, gc_ref, gr_ref):
    h = _rms(x_ref[...], g_ref[...]).astype(BF16)

    def mm(lo, hi):
        return jnp.dot(h, w_ref[:, lo:hi], preferred_element_type=F32)

    qa_ref[...] = mm(C_QA, C_KVA).astype(BF16)
    kva_ref[...] = mm(C_KVA, C_QKB).astype(BF16)
    qkb_ref[...] = mm(C_QKB, C_VB).astype(BF16)
    vb_ref[...] = mm(C_VB, C_OB).astype(BF16)
    ob_ref[...] = mm(C_OB, C_GATE).astype(BF16)
    gc_ref[...] = mm(C_GATE, C_GATE + LANES)
    gr_ref[...] = lax.dot_general(wgt_ref[...], h, (((1,), (1,)), ((), ())), preferred_element_type=F32)


def _inproj(x2d, g, w_pad, wg_t):
    T = x2d.shape[0]
    tm = min(TM_INPROJ, T)
    row = lambda w: pl.BlockSpec((tm, w), lambda i: (i, 0))
    full = lambda a: pl.BlockSpec(a.shape, lambda i: (0,) * a.ndim)
    return pl.pallas_call(
        _inproj_kernel,
        grid=(T // tm,),
        in_specs=[row(D_MODEL), full(g), full(w_pad), full(wg_t)],
        out_specs=[row(W_A_Q), row(2 * W_A_KV), row(2 * W_B), row(W_B), row(W_B), row(LANES),
                   pl.BlockSpec((GATE_ROWS, tm), lambda i: (0, i))],
        out_shape=[jax.ShapeDtypeStruct((T, W_A_Q), BF16),
                   jax.ShapeDtypeStruct((T, 2 * W_A_KV), BF16),
                   jax.ShapeDtypeStruct((T, 2 * W_B), BF16),
                   jax.ShapeDtypeStruct((T, W_B), BF16),
                   jax.ShapeDtypeStruct((T, W_B), BF16),
                   jax.ShapeDtypeStruct((T, LANES), F32),
                   jax.ShapeDtypeStruct((GATE_ROWS, T), F32)],
        compiler_params=pltpu.CompilerParams(dimension_semantics=("parallel",), vmem_limit_bytes=VMEM_LIMIT),
        name="inproj",
    )(x2d, g, w_pad, wg_t)


def _swa_kernel(sink_ref, qa_ref, kvc_ref, kvp_ref, bias_ref, o_ref):
    n = pl.program_id(1)
    q = qa_ref[...]
    kvp = kvp_ref[...].astype(F32)
    kvc = kvc_ref[...].astype(F32)
    kband = jnp.concatenate([kvp[:, :W_A_KV], kvc[:, :W_A_KV]], axis=0)
    vband = jnp.concatenate([kvp[:, W_A_KV:], kvc[:, W_A_KV:]], axis=0)
    lane = lax.broadcasted_iota(I32, (2 * BLOCK, LANES), 1)
    lo = lane < HEAD_DIM_A

    def placements(band):
        swapped = pltpu.roll(band, HEAD_DIM_A, axis=1)
        z = jnp.zeros_like(band)
        return {(0, 0): jnp.where(lo, band, z).astype(BF16), (0, 1): jnp.where(lo, z, swapped).astype(BF16),
                (1, 0): jnp.where(lo, swapped, z).astype(BF16), (1, 1): jnp.where(lo, z, band).astype(BF16)}

    kpl = placements(kband)
    vpl = placements(vband)

    i = lax.broadcasted_iota(I32, (BLOCK, 2 * BLOCK), 0)
    j = lax.broadcasted_iota(I32, (BLOCK, 2 * BLOCK), 1)
    d = i + BLOCK - j
    ok = (d >= 0) & (d < WINDOW) & ((n > 0) | (j >= BLOCK))
    scale = HEAD_DIM_A ** -0.5
    group = N_HEADS_A // N_KV_A

    for pair in range(N_HEADS_A // 2):
        qt = q[:, pair * LANES:(pair + 1) * LANES]
        acc = None
        for half in range(2):
            h = 2 * pair + half
            g = h // group
            s = lax.dot_general(qt, kpl[(g, half)], (((1,), (1,)), ((), ())), preferred_element_type=F32)
            s = s * scale + bias_ref[h]
            s = jnp.where(ok, s, NEG_INF)
            sink = sink_ref[h]
            m = jnp.maximum(jnp.max(s, axis=-1, keepdims=True), sink)
            p = jnp.exp(s - m)
            denom = jnp.sum(p, axis=-1, keepdims=True) + jnp.exp(sink - m)
            o = jnp.dot(p.astype(BF16), vpl[(g, half)], preferred_element_type=F32) / denom
            acc = o if acc is None else acc + o
        o_ref[:, pair * LANES:(pair + 1) * LANES] = acc.astype(BF16)


def _swa(qa, kva, bias, sinks, B, S):
    nb = S // BLOCK
    T = B * S
    return pl.pallas_call(
        _swa_kernel,
        grid=(B, nb),
        in_specs=[pl.BlockSpec(memory_space=pltpu.SMEM),
                  pl.BlockSpec((BLOCK, W_A_Q), lambda b, n: (b * nb + n, 0)),
                  pl.BlockSpec((BLOCK, 2 * W_A_KV), lambda b, n: (b * nb + n, 0)),
                  pl.BlockSpec((BLOCK, 2 * W_A_KV), lambda b, n: (b * nb + jnp.maximum(n - 1, 0), 0)),
                  pl.BlockSpec(bias.shape, lambda b, n: (0, 0, 0))],
        out_specs=pl.BlockSpec((BLOCK, W_A_Q), lambda b, n: (b * nb + n, 0)),
        out_shape=jax.ShapeDtypeStruct((T, W_A_Q), BF16),
        compiler_params=pltpu.CompilerParams(dimension_semantics=("parallel", "parallel"),
                                             vmem_limit_bytes=VMEM_LIMIT),
        name="swa",
    )(sinks, qa, kva, kva, bias)


CONV_HALO = 16


def _mlstm_kernel(qkc_ref, qkp_ref, vb_ref, ob_ref, gc_ref, gr_ref, cw_ref, cb_ref, gbc_ref, gbr_ref, nrm_ref,
                  o_ref, ext_ref, state_ref, m_ref):
    c = pl.program_id(1)
    H, D, L = N_HEADS_B, HEAD_DIM_B, CHUNK

    @pl.when(c == 0)
    def _():
        state_ref[...] = jnp.zeros_like(state_ref)
        m_ref[...] = jnp.zeros_like(m_ref)

    prev = qkp_ref[...].astype(F32)
    ext_ref[0:CONV_HALO, :] = jnp.where(c > 0, prev, jnp.zeros_like(prev))
    ext_ref[CONV_HALO:CONV_HALO + L, :] = qkc_ref[...].astype(F32)
    y = cb_ref[...]
    for tap in range(CONV_WIDTH):
        off = CONV_HALO - (CONV_WIDTH - 1) + tap
        y = y + cw_ref[tap:tap + 1, :] * ext_ref[off:off + L, :]
    qk = y * jax.nn.sigmoid(y)

    def log_sigmoid(z):
        return jnp.minimum(z, 0.0) - jnp.log1p(jnp.exp(-jnp.abs(z)))

    gcol = gc_ref[...] + gbc_ref[...]
    grow = gr_ref[...] + gbr_ref[...]
    lane_c = lax.broadcasted_iota(I32, (L, LANES), 1)
    is_f_col = (lane_c >= H) & (lane_c < 2 * H)
    fcol = jnp.where(is_f_col, log_sigmoid(gcol), 0.0)
    row_r = lax.broadcasted_iota(I32, (GATE_ROWS, L), 0)
    frow = jnp.where(row_r >= H, log_sigmoid(grow), 0.0)
    ti = lax.broadcasted_iota(I32, (L, L), 0)
    si = lax.broadcasted_iota(I32, (L, L), 1)
    tri = si <= ti
    tril = jnp.where(tri, 1.0, 0.0).astype(F32)
    triu = jnp.where(si >= ti, 1.0, 0.0).astype(F32)
    bcol = jnp.dot(tril, fcol, preferred_element_type=F32, precision=lax.Precision.HIGHEST)
    brow = jnp.dot(frow, triu, preferred_element_type=F32, precision=lax.Precision.HIGHEST)

    lane_v = lax.broadcasted_iota(I32, (L, D), 1)
    ones_col = jnp.where(lane_v == 0, 1.0, 0.0).astype(F32)

    for h in range(H):
        qh = (qk[:, h * D:(h + 1) * D] * (D ** -0.5)).astype(BF16)
        kh = qk[:, W_B + h * D:W_B + (h + 1) * D].astype(BF16)
        vh = vb_ref[:, h * D:(h + 1) * D].astype(F32)
        v1 = jnp.concatenate([vh, ones_col], axis=-1)
        b_c = bcol[:, H + h:H + h + 1]
        i_c = gcol[:, h:h + 1]
        b_r = brow[H + h:H + h + 1, :]
        i_r = grow[h:h + 1, :]
        m_prev = m_ref[h:h + 1, 0:1]

        logd = jnp.where(tri, b_c - b_r + i_r, NEG_INF)
        inter = b_c + m_prev
        m_t = jnp.maximum(inter, jnp.max(logd, axis=-1, keepdims=True))
        a_inter = jnp.exp(inter - m_t)
        sc = lax.dot_general(qh, kh, (((1,), (1,)), ((), ())), preferred_element_type=F32) * jnp.exp(logd - m_t)
        state = state_ref[h]
        tot = (jnp.dot(sc.astype(BF16), v1.astype(BF16), preferred_element_type=F32)
               + a_inter * jnp.dot(qh, state.astype(BF16), preferred_element_type=F32))
        num = tot[:, :D]
        den = tot[:, D:D + 1]
        hh = num / jnp.maximum(jnp.abs(den), jnp.exp(-m_t))

        b_last = b_c[L - 1:L, :]
        logw_r = b_last - b_r + i_r
        m_new = jnp.maximum(b_last + m_prev, jnp.max(logw_r, axis=-1, keepdims=True))
        w_c = jnp.exp(b_last - b_c + i_c - m_new)
        decay = jnp.exp(b_last + m_prev - m_new)
        upd = lax.dot_general(kh, (w_c * v1).astype(BF16), (((0,), (0,)), ((), ())), preferred_element_type=F32)
        state_ref[h] = decay * state + upd
        m_ref[h:h + 1, :] = jnp.broadcast_to(m_new, (1, LANES))

        og = jax.nn.sigmoid(ob_ref[:, h * D:(h + 1) * D].astype(F32))
        hb = og * hh
        hb = hb * lax.rsqrt(jnp.mean(hb * hb, axis=-1, keepdims=True) + EPS)
        o_ref[:, h * D:(h + 1) * D] = (hb * nrm_ref[:, h * D:(h + 1) * D]).astype(BF16)


def _mlstm(qkb, vb, ob, gc, gr, conv_w, conv_b, gbias_col, gbias_row, nrm, B, S):
    nc = S // CHUNK
    T = B * S
    halo_per_chunk = CHUNK // CONV_HALO
    blk = lambda w: pl.BlockSpec((CHUNK, w), lambda b, c: (b * nc + c, 0))
    full = lambda a: pl.BlockSpec(a.shape, lambda b, c: (0,) * a.ndim)
    return pl.pallas_call(
        _mlstm_kernel,
        grid=(B, nc),
        in_specs=[blk(2 * W_B),
                  pl.BlockSpec((CONV_HALO, 2 * W_B),
                               lambda b, c: (jnp.maximum((b * nc + c) * halo_per_chunk - 1, 0), 0)),
                  blk(W_B), blk(W_B), blk(LANES),
                  pl.BlockSpec((GATE_ROWS, CHUNK), lambda b, c: (0, b * nc + c)),
                  full(conv_w), full(conv_b), full(gbias_col), full(gbias_row), full(nrm)],
        out_specs=blk(W_B),
        out_shape=jax.ShapeDtypeStruct((T, W_B), BF16),
        scratch_shapes=[pltpu.VMEM((CONV_HALO + CHUNK, 2 * W_B), F32),
                        pltpu.VMEM((N_HEADS_B, HEAD_DIM_B, 2 * HEAD_DIM_B), F32),
                        pltpu.VMEM((GATE_ROWS, LANES), F32)],
        compiler_params=pltpu.CompilerParams(dimension_semantics=("arbitrary", "arbitrary"),
                                             vmem_limit_bytes=VMEM_LIMIT),
        name="mlstm",
    )(qkb, qkb, vb, ob, gc, gr, conv_w, conv_b, gbias_col, gbias_row, nrm)


def _memkv_kernel(mem_ref, g_ref, wk_ref, wv_ref, k_ref, v_ref):
    hm = _rms(mem_ref[...], g_ref[...]).astype(BF16)
    k_ref[...] = jnp.dot(hm, wk_ref[...], preferred_element_type=F32).astype(BF16)
    v_ref[...] = jnp.dot(hm, wv_ref[...], preferred_element_type=F32).astype(BF16)


def _memkv(mem2d, g, wk, wv, B):
    full = lambda a: pl.BlockSpec(a.shape, lambda b: (0,) * a.ndim)
    blk = pl.BlockSpec((N_MEM, D_MODEL), lambda b: (b, 0))
    return pl.pallas_call(
        _memkv_kernel,
        grid=(B,),
        in_specs=[blk, full(g), full(wk), full(wv)],
        out_specs=[blk, blk],
        out_shape=[jax.ShapeDtypeStruct((B * N_MEM, D_MODEL), BF16)] * 2,
        compiler_params=pltpu.CompilerParams(dimension_semantics=("parallel",), vmem_limit_bytes=VMEM_LIMIT),
        name="memkv",
    )(mem2d, g, wk, wv)


ROUTE_E1, ROUTE_E2, ROUTE_G1, ROUTE_G2 = 0, 1, 2, 3
ROUTER_GROUP_COL = N_EXPERTS


def _mid_kernel(x_ref, oa_ref, hb_ref, wo_ref, gx_ref, wq_ref, ck_ref, cv_ref, wco_ref, gz_ref, wr_ref, br_ref,
                x2_ref, hz_ref, route_ref):
    x1 = (x_ref[...]
          + jnp.dot(oa_ref[...], wo_ref[0:W_A_Q, :], preferred_element_type=F32)
          + jnp.dot(hb_ref[...], wo_ref[W_A_Q:, :], preferred_element_type=F32))

    hc = _rms(x1, gx_ref[...]).astype(BF16)
    cq = jnp.dot(hc, wq_ref[...], preferred_element_type=F32).astype(BF16)
    scale = HEAD_DIM_X ** -0.5
    heads = []
    for h in range(N_HEADS_X):
        sl = slice(h * HEAD_DIM_X, (h + 1) * HEAD_DIM_X)
        s = lax.dot_general(cq[:, sl], ck_ref[:, sl], (((1,), (1,)), ((), ())), preferred_element_type=F32) * scale
        p = jnp.exp(s - jnp.max(s, axis=-1, keepdims=True))
        co = jnp.dot(p.astype(BF16), cv_ref[:, sl], preferred_element_type=F32) / jnp.sum(p, axis=-1, keepdims=True)
        heads.append(co.astype(BF16))
    x2 = x1 + jnp.dot(jnp.concatenate(heads, axis=-1), wco_ref[...], preferred_element_type=F32)
    x2_ref[...] = x2

    hz = _rms(x2, gz_ref[...])
    hz_ref[...] = _pack_halves(hz)
    lg = jnp.dot(hz.astype(BF16), wr_ref[...], preferred_element_type=F32) + br_ref[...]
    col = lax.broadcasted_iota(I32, lg.shape, 1)
    big = jnp.int32(LANES)
    is_g = (col >= ROUTER_GROUP_COL) & (col < ROUTER_GROUP_COL + N_GROUPS)
    gl = jnp.where(is_g, lg, NEG_INF)
    gmax = jnp.max(gl, axis=-1, keepdims=True)
    gsum = jnp.sum(jnp.exp(gl - gmax), axis=-1, keepdims=True)
    g_prob = 1.0 / gsum
    g_idx = jnp.min(jnp.where(gl == gmax, col - ROUTER_GROUP_COL, big), axis=-1, keepdims=True)
    sel = (col < N_EXPERTS) & ((col // EXPERTS_PER_GROUP) == g_idx)
    el = jnp.where(sel, lg, NEG_INF)
    m1 = jnp.max(el, axis=-1, keepdims=True)
    i1 = jnp.min(jnp.where(el == m1, col, big), axis=-1, keepdims=True)
    el2 = jnp.where(col == i1, NEG_INF, el)
    m2 = jnp.max(el2, axis=-1, keepdims=True)
    i2 = jnp.min(jnp.where(el2 == m2, col, big), axis=-1, keepdims=True)
    z = jnp.sum(jnp.exp(el - m1), axis=-1, keepdims=True)
    p1 = 1.0 / z
    p2 = jnp.exp(m2 - m1) / z
    g1 = g_prob * (p1 / (p1 + p2))
    g2 = g_prob * (p2 / (p1 + p2))
    route = jnp.where(col == ROUTE_E1, i1.astype(F32),
                      jnp.where(col == ROUTE_E2, i2.astype(F32),
                                jnp.where(col == ROUTE_G1, g1, jnp.where(col == ROUTE_G2, g2, 0.0))))
    route_ref[...] = route


def _mid(x2d, oa, hb, wo, gx, wq, ck, cv, wco, gz, wr, br, B, S):
    T = B * S
    tm = min(TM_MID, S)
    per_b = S // tm
    row = lambda w: pl.BlockSpec((tm, w), lambda i: (i, 0))
    full = lambda a: pl.BlockSpec(a.shape, lambda i: (0,) * a.ndim)
    kvspec = pl.BlockSpec((N_MEM, D_MODEL), lambda i: (i // per_b, 0))
    return pl.pallas_call(
        _mid_kernel,
        grid=(T // tm,),
        in_specs=[row(D_MODEL), row(W_A_Q), row(W_B), full(wo), full(gx), full(wq), kvspec, kvspec,
                  full(wco), full(gz), full(wr), full(br)],
        out_specs=[row(D_MODEL), row(HALF), row(LANES)],
        out_shape=[jax.ShapeDtypeStruct((T, D_MODEL), F32),
                   jax.ShapeDtypeStruct((T, HALF), U32),
                   jax.ShapeDtypeStruct((T, LANES), F32)],
        compiler_params=pltpu.CompilerParams(dimension_semantics=("parallel",), vmem_limit_bytes=VMEM_LIMIT),
        name="mid",
    )(x2d, oa, hb, wo, gx, wq, ck, cv, wco, gz, wr, br)


def _expert_kernel(te_ref, nt_ref, stok_ref, sdst_ref,
                   hz_hbm, wg_ref, wu_ref, wd_ref,
                   y_hbm,
                   xbuf, ybuf, wgb, wub, wdb, gsem, ssem):
    i = pl.program_id(0)
    nt = nt_ref[0]
    tmx = xbuf.shape[1]

    def row_copies(tile, slot, start_gather):
        base = tile * tmx

        def body(r, carry):
            if start_gather:
                pltpu.make_async_copy(hz_hbm.at[pl.ds(stok_ref[base + r], 1)],
                                      xbuf.at[slot, pl.ds(r, 1)], gsem.at[slot]).start()
            else:
                pltpu.make_async_copy(ybuf.at[slot, pl.ds(r, 1)],
                                      y_hbm.at[pl.ds(sdst_ref[base + r], 1)], ssem.at[slot]).start()
            return carry

        lax.fori_loop(0, tmx, body, 0, unroll=ISSUE_UNROLL)

    def gather_wait(slot):
        pltpu.make_async_copy(hz_hbm.at[pl.ds(0, tmx)], xbuf.at[slot], gsem.at[slot]).wait()

    def scatter_wait(slot):
        pltpu.make_async_copy(ybuf.at[slot], y_hbm.at[pl.ds(0, tmx)], ssem.at[slot]).wait()

    @pl.when(i < nt)
    def _():
        slot = i % 2

        @pl.when(i == 0)
        def _():
            row_copies(0, 0, True)
            ybuf[0] = jnp.zeros((tmx, HALF), U32)
            n_real = y_hbm.shape[0] - 2 * tmx
            for part in range(2):
                fill = pltpu.make_async_copy(ybuf.at[0], y_hbm.at[pl.ds(n_real + part * tmx, tmx)], ssem.at[0])
                fill.start()
                fill.wait()

        @pl.when(i + 1 < nt)
        def _():
            row_copies(i + 1, 1 - slot, True)

        gather_wait(slot)

        @pl.when((i == 0) | (te_ref[i] != te_ref[jnp.maximum(i - 1, 0)]))
        def _():
            wgb[...] = wg_ref[0].astype(BF16)
            wub[...] = wu_ref[0].astype(BF16)
            wdb[...] = wd_ref[0].astype(BF16)

        x = _unpack_halves(xbuf[slot]).astype(BF16)
        hg = jnp.dot(x, wgb[...], preferred_element_type=F32)
        hu = jnp.dot(x, wub[...], preferred_element_type=F32)
        a = (hg * jax.nn.sigmoid(hg) * hu).astype(BF16)
        y = jnp.dot(a, wdb[...], preferred_element_type=F32)

        @pl.when(i >= 2)
        def _():
            scatter_wait(slot)

        ybuf[slot] = _pack_halves(y)
        row_copies(i, slot, False)

        @pl.when(i == nt - 1)
        def _():
            scatter_wait(slot)

            @pl.when(i >= 1)
            def _():
                scatter_wait(1 - slot)


def _experts(hz_packed, w_gate, w_up, w_down, tile_expert, ntiles, slot_tok, slot_dst, n_out_rows, tmx):
    n_tiles_max = tile_expert.shape[0]
    wspec = lambda a: pl.BlockSpec((1,) + a.shape[1:], lambda i, te, nt, st, sd: (te[i], 0, 0))
    grid_spec = pltpu.PrefetchScalarGridSpec(
        num_scalar_prefetch=4,
        grid=(n_tiles_max,),
        in_specs=[pl.BlockSpec(memory_space=pl.ANY), wspec(w_gate), wspec(w_up), wspec(w_down)],
        out_specs=pl.BlockSpec(memory_space=pl.ANY),
        scratch_shapes=[pltpu.VMEM((2, tmx, HALF), U32),
                        pltpu.VMEM((2, tmx, HALF), U32),
                        pltpu.VMEM((D_MODEL, D_EXPERT), BF16),
                        pltpu.VMEM((D_MODEL, D_EXPERT), BF16),
                        pltpu.VMEM((D_EXPERT, D_MODEL), BF16),
                        pltpu.SemaphoreType.DMA((2,)),
                        pltpu.SemaphoreType.DMA((2,))],
    )
    return pl.pallas_call(
        _expert_kernel,
        grid_spec=grid_spec,
        out_shape=jax.ShapeDtypeStruct((n_out_rows, HALF), U32),
        compiler_params=pltpu.CompilerParams(dimension_semantics=("arbitrary",), vmem_limit_bytes=VMEM_LIMIT),
        name="experts",
    )(tile_expert, ntiles, slot_tok, slot_dst, hz_packed, w_gate, w_up, w_down)


def _final_kernel(x2_ref, ya_ref, yb_ref, route_ref, g_ref, o_ref):
    r = route_ref[...]
    g1 = r[:, ROUTE_G1:ROUTE_G1 + 1]
    g2 = r[:, ROUTE_G2:ROUTE_G2 + 1]
    xo = x2_ref[...] + g1 * _unpack_halves(ya_ref[...]) + g2 * _unpack_halves(yb_ref[...])
    o_ref[...] = _rms(xo, g_ref[...])


def _final(x2, y_packed, route, g, T):
    tm = min(TM_FINAL, T)
    nblk = T // tm
    row = lambda w: pl.BlockSpec((tm, w), lambda i: (i, 0))
    return pl.pallas_call(
        _final_kernel,
        grid=(nblk,),
        in_specs=[row(D_MODEL), row(HALF), pl.BlockSpec((tm, HALF), lambda i: (i + nblk, 0)), row(LANES),
                  pl.BlockSpec(g.shape, lambda i: (0, 0))],
        out_specs=row(D_MODEL),
        out_shape=jax.ShapeDtypeStruct((T, D_MODEL), F32),
        compiler_params=pltpu.CompilerParams(dimension_semantics=("parallel",), vmem_limit_bytes=VMEM_LIMIT),
        name="final",
    )(x2, y_packed, y_packed, route, g)


def _band_bias(table):
    i = jnp.arange(BLOCK)[:, None]
    j = jnp.arange(2 * BLOCK)[None, :]
    n = jnp.maximum(i + BLOCK - j, 0)
    nf = jnp.maximum(n, 1).astype(F32)
    large = MAX_EXACT + (jnp.log(nf / MAX_EXACT) / math.log(MAX_DISTANCE / MAX_EXACT)
                         * (NUM_BUCKETS - MAX_EXACT)).astype(I32)
    large = jnp.minimum(large, NUM_BUCKETS - 1)
    bucket = jnp.where(n < MAX_EXACT, n, large)
    return jnp.transpose(table.astype(F32)[bucket], (2, 0, 1))


def _dispatch_plan(eid, T, tmx, n_tiles_max):
    n_pairs = T * TOP_K
    flat_e = eid.reshape(-1)
    order = jnp.argsort(flat_e, stable=True).astype(I32)
    counts = jnp.sum((flat_e[:, None] == jnp.arange(N_EXPERTS, dtype=I32)[None, :]).astype(I32), axis=0)
    ptiles = (counts + tmx - 1) // tmx
    tile_end = jnp.cumsum(ptiles)
    nt = tile_end[-1]
    tile_ids = jnp.arange(n_tiles_max, dtype=I32)
    te = jnp.searchsorted(tile_end, tile_ids, side="right").astype(I32)
    te = jnp.where(tile_ids < nt, te, te[jnp.maximum(nt - 1, 0)])
    te = jnp.minimum(te, N_EXPERTS - 1)
    start = jnp.cumsum(counts) - counts
    tstart = tile_end - ptiles
    s = jnp.arange(n_tiles_max * tmx, dtype=I32)
    ti = s // tmx
    e = te[ti]
    jj = s - tstart[e] * tmx
    valid = (ti < nt) & (jj < counts[e])
    pair = order[jnp.clip(start[e] + jj, 0, n_pairs - 1)]
    tok = pair // TOP_K
    k = pair % TOP_K
    slot_tok = jnp.where(valid, tok, 0).astype(I32)
    slot_dst = jnp.where(valid, k * T + tok, n_pairs + (ti % 2) * tmx + (s % tmx)).astype(I32)
    return te, nt.reshape(1).astype(I32), slot_tok, slot_dst


def kernel(x, mem, rel_bias_table, norm_mix, w_in, attn_sinks, conv_w, conv_b, gate_bias_i, gate_bias_f, mlstm_norm, w_out, norm_cross, norm_mem, w_cq, w_ck, w_cv, w_co, norm_moe, w_router_group, b_router_group, w_router_expert, b_router_expert, w_exp_gate, w_exp_up, w_exp_down, norm_final):
    B, S, _ = x.shape
    T = B * S
    depth = w_in.shape[0]
    x2d = x.reshape(T, D_MODEL)
    mem2d = mem.reshape(B * N_MEM, D_MODEL)
    bias = _band_bias(rel_bias_table)

    tmx = min(TM_EXPERT, T)
    n_tiles_max = (T * TOP_K) // tmx + N_EXPERTS
    n_out_rows = T * TOP_K + 2 * tmx

    assert depth == 1, "the final combine is fused with the final norm: single layer only"
    l = 0
    w_pad = jnp.pad(w_in[l], ((0, 0), (0, C_GATE + LANES - D_IN))).astype(BF16)
    wg_t = w_in[l][:, C_GATE:].T.astype(BF16)
    qa, kva, qkb, vb, ob, gc, gr = _inproj(x2d, norm_mix[l][None, :], w_pad, wg_t)

    out_a = _swa(qa, kva, bias, attn_sinks[l].astype(F32), B, S)

    gb = jnp.concatenate([gate_bias_i[l], gate_bias_f[l]]).astype(F32)
    gbias_col = jnp.pad(gb, (0, LANES - GATE_ROWS))[None, :]
    gbias_row = jnp.broadcast_to(gb[:, None], (GATE_ROWS, CHUNK))
    hb = _mlstm(qkb, vb, ob, gc, gr, conv_w[l][:, 0, :].astype(F32), conv_b[l][None, :].astype(F32),
                gbias_col, gbias_row, mlstm_norm[l][None, :].astype(F32), B, S)

    ck, cv = _memkv(mem2d, norm_mem[l][None, :], w_ck[l].astype(BF16), w_cv[l].astype(BF16), B)

    wr = jnp.pad(jnp.concatenate([w_router_expert[l], w_router_group[l]], axis=1),
                 ((0, 0), (0, LANES - N_EXPERTS - N_GROUPS))).astype(BF16)
    br = jnp.pad(jnp.concatenate([b_router_expert[l], b_router_group[l]]),
                 (0, LANES - N_EXPERTS - N_GROUPS)).astype(F32)[None, :]
    x2, hz_packed, route = _mid(x2d, out_a, hb, w_out[l].astype(BF16), norm_cross[l][None, :],
                                w_cq[l].astype(BF16), ck, cv, w_co[l].astype(BF16), norm_moe[l][None, :],
                                wr, br, B, S)

    eid = route[:, ROUTE_E1:ROUTE_E2 + 1].astype(I32)
    te, nt, slot_tok, slot_dst = _dispatch_plan(eid, T, tmx, n_tiles_max)
    y_packed = _experts(hz_packed, w_exp_gate[l], w_exp_up[l], w_exp_down[l], te, nt, slot_tok, slot_dst,
                        n_out_rows, tmx)

    out = _final(x2, y_packed, route, norm_final[None, :], T)
    return out.reshape(B, S, D_MODEL)
```

```python
import math

import jax
import jax.numpy as jnp
from jax import lax
from jax.experimental import pallas as pl
from jax.experimental.pallas import tpu as pltpu

F32 = jnp.float32
BF16 = jnp.bfloat16
U32 = jnp.uint32
I32 = jnp.int32

D_MODEL = 1024
N_MEM = 256
N_HEADS_A = 8
N_KV_A = 2
HEAD_DIM_A = 64
BLOCK = 128
WINDOW = 128
NUM_BUCKETS = 32
MAX_EXACT = NUM_BUCKETS // 2
MAX_DISTANCE = 128
N_HEADS_B = 4
HEAD_DIM_B = 128
CHUNK = 128
CONV_WIDTH = 4
N_HEADS_X = 4
HEAD_DIM_X = D_MODEL // N_HEADS_X
N_GROUPS = 4
EXPERTS_PER_GROUP = 8
N_EXPERTS = N_GROUPS * EXPERTS_PER_GROUP
TOP_K = 2
D_EXPERT = 512
EPS = 1e-6
NEG_INF = -1e30

W_A_Q = N_HEADS_A * HEAD_DIM_A
W_A_KV = N_KV_A * HEAD_DIM_A
W_B = N_HEADS_B * HEAD_DIM_B
C_QA = 0
C_KVA = C_QA + W_A_Q
C_QKB = C_KVA + 2 * W_A_KV
C_VB = C_QKB + 2 * W_B
C_OB = C_VB + W_B
C_GATE = C_OB + W_B
D_IN = C_GATE + 2 * N_HEADS_B

LANES = 128
SUBLANES = 8
GATE_ROWS = 8
HALF = D_MODEL // 2

TM_INPROJ = 512
TM_MID = 256
TM_ROWDMA = 256
TM_EXPERT = 256

VMEM_LIMIT = 48 * 1024 * 1024


def _rms(xf, g):
    return xf * lax.rsqrt(jnp.mean(xf * xf, axis=-1, keepdims=True) + EPS) * g


def _pack_halves(v):
    b = pltpu.bitcast(v.astype(BF16).astype(F32), U32)
    return (b[:, :HALF] >> 16) | b[:, HALF:]


def _unpack_halves(p):
    lo = pltpu.bitcast(p << 16, F32)
    hi = pltpu.bitcast(p & jnp.uint32(0xFFFF0000), F32)
    return jnp.concatenate([lo, hi], axis=-1)


def _inproj_kernel(x_ref, g_ref, w_ref, wgt_ref, qa_ref, kva_ref, qkb_ref, vb_ref, ob_ref, gc_ref, gr_ref):
    h = _rms(x_ref[...], g_ref[...]).astype(BF16)

    def mm(lo, hi):
        return jnp.dot(h, w_ref[:, lo:hi], preferred_element_type=F32)

    qa_ref[...] = mm(C_QA, C_KVA).astype(BF16)
    kva_ref[...] = mm(C_KVA, C_QKB).astype(BF16)
    qkb_ref[...] = mm(C_QKB, C_VB).astype(BF16)
    vb_ref[...] = mm(C_VB, C_OB).astype(BF16)
    ob_ref[...] = mm(C_OB, C_GATE).astype(BF16)
    gc_ref[...] = mm(C_GATE, C_GATE + LANES)
    gr_ref[...] = lax.dot_general(wgt_ref[...], h, (((1,), (1,)), ((), ())), preferred_element_type=F32)


def _inproj(x2d, g, w_pad, wg_t):
    T = x2d.shape[0]
    tm = min(TM_INPROJ, T)
    row = lambda w: pl.BlockSpec((tm, w), lambda i: (i, 0))
    full = lambda a: pl.BlockSpec(a.shape, lambda i: (0,) * a.ndim)
    return pl.pallas_call(
        _inproj_kernel,
        grid=(T // tm,),
        in_specs=[row(D_MODEL), full(g), full(w_pad), full(wg_t)],
        out_specs=[row(W_A_Q), row(2 * W_A_KV), row(2 * W_B), row(W_B), row(W_B), row(LANES),
                   pl.BlockSpec((GATE_ROWS, tm), lambda i: (0, i))],
        out_shape=[jax.ShapeDtypeStruct((T, W_A_Q), BF16),
                   jax.ShapeDtypeStruct((T, 2 * W_A_KV), BF16),
                   jax.ShapeDtypeStruct((T, 2 * W_B), BF16),
                   jax.ShapeDtypeStruct((T, W_B), BF16),
                   jax.ShapeDtypeStruct((T, W_B), BF16),
                   jax.ShapeDtypeStruct((T, LANES), F32),
                   jax.ShapeDtypeStruct((GATE_ROWS, T), F32)],
        compiler_params=pltpu.CompilerParams(dimension_semantics=("parallel",), vmem_limit_bytes=VMEM_LIMIT),
        name="inproj",
    )(x2d, g, w_pad, wg_t)


def _swa_kernel(sink_ref, qa_ref, kvc_ref, kvp_ref, bias_ref, o_ref):
    n = pl.program_id(1)
    q = qa_ref[...]
    kvp = kvp_ref[...].astype(F32)
    kvc = kvc_ref[...].astype(F32)
    kband = jnp.concatenate([kvp[:, :W_A_KV], kvc[:, :W_A_KV]], axis=0)
    vband = jnp.concatenate([kvp[:, W_A_KV:], kvc[:, W_A_KV:]], axis=0)
    lane = lax.broadcasted_iota(I32, (2 * BLOCK, LANES), 1)
    lo = lane < HEAD_DIM_A

    def placements(band):
        swapped = pltpu.roll(band, HEAD_DIM_A, axis=1)
        z = jnp.zeros_like(band)
        return {(0, 0): jnp.where(lo, band, z).astype(BF16), (0, 1): jnp.where(lo, z, swapped).astype(BF16),
                (1, 0): jnp.where(lo, swapped, z).astype(BF16), (1, 1): jnp.where(lo, z, band).astype(BF16)}

    kpl = placements(kband)
    vpl = placements(vband)

    i = lax.broadcasted_iota(I32, (BLOCK, 2 * BLOCK), 0)
    j = lax.broadcasted_iota(I32, (BLOCK, 2 * BLOCK), 1)
    d = i + BLOCK - j
    ok = (d >= 0) & (d < WINDOW) & ((n > 0) | (j >= BLOCK))
    scale = HEAD_DIM_A ** -0.5
    group = N_HEADS_A // N_KV_A

    for pair in range(N_HEADS_A // 2):
        qt = q[:, pair * LANES:(pair + 1) * LANES]
        acc = None
        for half in range(2):
            h = 2 * pair + half
            g = h // group
            s = lax.dot_general(qt, kpl[(g, half)], (((1,), (1,)), ((), ())), preferred_element_type=F32)
            s = s * scale + bias_ref[h]
            s = jnp.where(ok, s, NEG_INF)
            sink = sink_ref[h]
            m = jnp.maximum(jnp.max(s, axis=-1, keepdims=True), sink)
            p = jnp.exp(s - m)
            denom = jnp.sum(p, axis=-1, keepdims=True) + jnp.exp(sink - m)
            o = jnp.dot(p.astype(BF16), vpl[(g, half)], preferred_element_type=F32) / denom
            acc = o if acc is None else acc + o
        o_ref[:, pair * LANES:(pair + 1) * LANES] = acc.astype(BF16)


def _swa(qa, kva, bias, sinks, B, S):
    nb = S // BLOCK
    T = B * S
    return pl.pallas_call(
        _swa_kernel,
        grid=(B, nb),
        in_specs=[pl.BlockSpec(memory_space=pltpu.SMEM),
                  pl.BlockSpec((BLOCK, W_A_Q), lambda b, n: (b * nb + n, 0)),
                  pl.BlockSpec((BLOCK, 2 * W_A_KV), lambda b, n: (b * nb + n, 0)),
                  pl.BlockSpec((BLOCK, 2 * W_A_KV), lambda b, n: (b * nb + jnp.maximum(n - 1, 0), 0)),
                  pl.BlockSpec(bias.shape, lambda b, n: (0, 0, 0))],
        out_specs=pl.BlockSpec((BLOCK, W_A_Q), lambda b, n: (b * nb + n, 0)),
        out_shape=jax.ShapeDtypeStruct((T, W_A_Q), BF16),
        compiler_params=pltpu.CompilerParams(dimension_semantics=("parallel", "parallel"),
                                             vmem_limit_bytes=VMEM_LIMIT),
        name="swa",
    )(sinks, qa, kva, kva, bias)


CONV_HALO = 16


def _mlstm_kernel(qkc_ref, qkp_ref, vb_ref, ob_ref, gc_ref, gr_ref, cw_ref, cb_ref, gbc_ref, gbr_ref, nrm_ref,
                  o_ref, ext_ref, state_ref, m_ref):
    c = pl.program_id(1)
    H, D, L = N_HEADS_B, HEAD_DIM_B, CHUNK

    @pl.when(c == 0)
    def _():
        state_ref[...] = jnp.zeros_like(state_ref)
        m_ref[...] = jnp.zeros_like(m_ref)

    prev = qkp_ref[...].astype(F32)
    ext_ref[0:CONV_HALO, :] = jnp.where(c > 0, prev, jnp.zeros_like(prev))
    ext_ref[CONV_HALO:CONV_HALO + L, :] = qkc_ref[...].astype(F32)
    y = cb_ref[...]
    for tap in range(CONV_WIDTH):
        off = CONV_HALO - (CONV_WIDTH - 1) + tap
        y = y + cw_ref[tap:tap + 1, :] * ext_ref[off:off + L, :]
    qk = y * jax.nn.sigmoid(y)

    def log_sigmoid(z):
        return jnp.minimum(z, 0.0) - jnp.log1p(jnp.exp(-jnp.abs(z)))

    gcol = gc_ref[...] + gbc_ref[...]
    grow = gr_ref[...] + gbr_ref[...]
    lane_c = lax.broadcasted_iota(I32, (L, LANES), 1)
    is_f_col = (lane_c >= H) & (lane_c < 2 * H)
    fcol = jnp.where(is_f_col, log_sigmoid(gcol), 0.0)
    row_r = lax.broadcasted_iota(I32, (GATE_ROWS, L), 0)
    frow = jnp.where(row_r >= H, log_sigmoid(grow), 0.0)
    ti = lax.broadcasted_iota(I32, (L, L), 0)
    si = lax.broadcasted_iota(I32, (L, L), 1)
    tri = si <= ti
    tril = jnp.where(tri, 1.0, 0.0).astype(F32)
    triu = jnp.where(si >= ti, 1.0, 0.0).astype(F32)
    bcol = jnp.dot(tril, fcol, preferred_element_type=F32, precision=lax.Precision.HIGHEST)
    brow = jnp.dot(frow, triu, preferred_element_type=F32, precision=lax.Precision.HIGHEST)

    lane_v = lax.broadcasted_iota(I32, (L, D), 1)
    ones_col = jnp.where(lane_v == 0, 1.0, 0.0).astype(F32)

    for h in range(H):
        qh = (qk[:, h * D:(h + 1) * D] * (D ** -0.5)).astype(BF16)
        kh = qk[:, W_B + h * D:W_B + (h + 1) * D].astype(BF16)
        vh = vb_ref[:, h * D:(h + 1) * D].astype(F32)
        v1 = jnp.concatenate([vh, ones_col], axis=-1)
        b_c = bcol[:, H + h:H + h + 1]
        i_c = gcol[:, h:h + 1]
        b_r = brow[H + h:H + h + 1, :]
        i_r = grow[h:h + 1, :]
        m_prev = m_ref[h:h + 1, 0:1]

        logd = jnp.where(tri, b_c - b_r + i_r, NEG_INF)
        inter = b_c + m_prev
        m_t = jnp.maximum(inter, jnp.max(logd, axis=-1, keepdims=True))
        a_inter = jnp.exp(inter - m_t)
        sc = lax.dot_general(qh, kh, (((1,), (1,)), ((), ())), preferred_element_type=F32) * jnp.exp(logd - m_t)
        state = state_ref[h]
        tot = (jnp.dot(sc.astype(BF16), v1.astype(BF16), preferred_element_type=F32)
               + a_inter * jnp.dot(qh, state.astype(BF16), preferred_element_type=F32))
        num = tot[:, :D]
        den = tot[:, D:D + 1]
        hh = num / jnp.maximum(jnp.abs(den), jnp.exp(-m_t))

        b_last = b_c[L - 1:L, :]
        logw_r = b_last - b_r + i_r
        m_new = jnp.maximum(b_last + m_prev, jnp.max(logw_r, axis=-1, keepdims=True))
        w_c = jnp.exp(b_last - b_c + i_c - m_new)
        decay = jnp.exp(b_last + m_prev - m_new)
        upd = lax.dot_general(kh, (w_c * v1).astype(BF16), (((0,), (0,)), ((), ())), preferred_element_type=F32)
        state_ref[h] = decay * state + upd
        m_ref[h:h + 1, :] = jnp.broadcast_to(m_new, (1, LANES))

        og = jax.nn.sigmoid(ob_ref[:, h * D:(h + 1) * D].astype(F32))
        hb = og * hh
        hb = hb * lax.rsqrt(jnp.mean(hb * hb, axis=-1, keepdims=True) + EPS)
        o_ref[:, h * D:(h + 1) * D] = (hb * nrm_ref[:, h * D:(h + 1) * D]).astype(BF16)


def _mlstm(qkb, vb, ob, gc, gr, conv_w, conv_b, gbias_col, gbias_row, nrm, B, S):
    nc = S // CHUNK
    T = B * S
    halo_per_chunk = CHUNK // CONV_HALO
    blk = lambda w: pl.BlockSpec((CHUNK, w), lambda b, c: (b * nc + c, 0))
    full = lambda a: pl.BlockSpec(a.shape, lambda b, c: (0,) * a.ndim)
    return pl.pallas_call(
        _mlstm_kernel,
        grid=(B, nc),
        in_specs=[blk(2 * W_B),
                  pl.BlockSpec((CONV_HALO, 2 * W_B),
                               lambda b, c: (jnp.maximum((b * nc + c) * halo_per_chunk - 1, 0), 0)),
                  blk(W_B), blk(W_B), blk(LANES),
                  pl.BlockSpec((GATE_ROWS, CHUNK), lambda b, c: (0, b * nc + c)),
                  full(conv_w), full(conv_b), full(gbias_col), full(gbias_row), full(nrm)],
        out_specs=blk(W_B),
        out_shape=jax.ShapeDtypeStruct((T, W_B), BF16),
        scratch_shapes=[pltpu.VMEM((CONV_HALO + CHUNK, 2 * W_B), F32),
                        pltpu.VMEM((N_HEADS_B, HEAD_DIM_B, 2 * HEAD_DIM_B), F32),
                        pltpu.VMEM((GATE_ROWS, LANES), F32)],
        compiler_params=pltpu.CompilerParams(dimension_semantics=("arbitrary", "arbitrary"),
                                             vmem_limit_bytes=VMEM_LIMIT),
        name="mlstm",
    )(qkb, qkb, vb, ob, gc, gr, conv_w, conv_b, gbias_col, gbias_row, nrm)


def _memkv_kernel(mem_ref, g_ref, wk_ref, wv_ref, k_ref, v_ref):
    hm = _rms(mem_ref[...], g_ref[...]).astype(BF16)
    k_ref[...] = jnp.dot(hm, wk_ref[...], preferred_element_type=F32).astype(BF16)
    v_ref[...] = jnp.dot(hm, wv_ref[...], preferred_element_type=F32).astype(BF16)


def _memkv(mem2d, g, wk, wv, B):
    full = lambda a: pl.BlockSpec(a.shape, lambda b: (0,) * a.ndim)
    blk = pl.BlockSpec((N_MEM, D_MODEL), lambda b: (b, 0))
    return pl.pallas_call(
        _memkv_kernel,
        grid=(B,),
        in_specs=[blk, full(g), full(wk), full(wv)],
        out_specs=[blk, blk],
        out_shape=[jax.ShapeDtypeStruct((B * N_MEM, D_MODEL), BF16)] * 2,
        compiler_params=pltpu.CompilerParams(dimension_semantics=("parallel",), vmem_limit_bytes=VMEM_LIMIT),
        name="memkv",
    )(mem2d, g, wk, wv)


ROUTE_E1, ROUTE_E2, ROUTE_G1, ROUTE_G2, ROUTE_R1, ROUTE_R2 = 0, 1, 2, 3, 4, 5
ROUTER_GROUP_COL = N_EXPERTS


def _mid_kernel(x_ref, oa_ref, hb_ref, wo_ref, gx_ref, wq_ref, ck_ref, cv_ref, wco_ref, gz_ref, wr_ref, br_ref,
                x2_ref, hz_ref, route_ref, counts_ref, cnt_ref):
    @pl.when(pl.program_id(0) == 0)
    def _():
        cnt_ref[...] = jnp.zeros_like(cnt_ref)

    x1 = (x_ref[...]
          + jnp.dot(oa_ref[...], wo_ref[0:W_A_Q, :], preferred_element_type=F32)
          + jnp.dot(hb_ref[...], wo_ref[W_A_Q:, :], preferred_element_type=F32))

    hc = _rms(x1, gx_ref[...]).astype(BF16)
    cq = jnp.dot(hc, wq_ref[...], preferred_element_type=F32).astype(BF16)
    scale = HEAD_DIM_X ** -0.5
    heads = []
    for h in range(N_HEADS_X):
        sl = slice(h * HEAD_DIM_X, (h + 1) * HEAD_DIM_X)
        s = lax.dot_general(cq[:, sl], ck_ref[:, sl], (((1,), (1,)), ((), ())), preferred_element_type=F32) * scale
        p = jnp.exp(s - jnp.max(s, axis=-1, keepdims=True))
        co = jnp.dot(p.astype(BF16), cv_ref[:, sl], preferred_element_type=F32) / jnp.sum(p, axis=-1, keepdims=True)
        heads.append(co.astype(BF16))
    x2 = x1 + jnp.dot(jnp.concatenate(heads, axis=-1), wco_ref[...], preferred_element_type=F32)
    x2_ref[...] = x2

    hz = _rms(x2, gz_ref[...])
    hz_ref[...] = _pack_halves(hz)
    lg = jnp.dot(hz.astype(BF16), wr_ref[...], preferred_element_type=F32) + br_ref[...]
    col = lax.broadcasted_iota(I32, lg.shape, 1)
    big = jnp.int32(LANES)
    is_g = (col >= ROUTER_GROUP_COL) & (col < ROUTER_GROUP_COL + N_GROUPS)
    gl = jnp.where(is_g, lg, NEG_INF)
    gmax = jnp.max(gl, axis=-1, keepdims=True)
    gsum = jnp.sum(jnp.exp(gl - gmax), axis=-1, keepdims=True)
    g_prob = 1.0 / gsum
    g_idx = jnp.min(jnp.where(gl == gmax, col - ROUTER_GROUP_COL, big), axis=-1, keepdims=True)
    sel = (col < N_EXPERTS) & ((col // EXPERTS_PER_GROUP) == g_idx)
    el = jnp.where(sel, lg, NEG_INF)
    m1 = jnp.max(el, axis=-1, keepdims=True)
    i1 = jnp.min(jnp.where(el == m1, col, big), axis=-1, keepdims=True)
    el2 = jnp.where(col == i1, NEG_INF, el)
    m2 = jnp.max(el2, axis=-1, keepdims=True)
    i2 = jnp.min(jnp.where(el2 == m2, col, big), axis=-1, keepdims=True)
    z = jnp.sum(jnp.exp(el - m1), axis=-1, keepdims=True)
    p1 = 1.0 / z
    p2 = jnp.exp(m2 - m1) / z
    g1 = g_prob * (p1 / (p1 + p2))
    g2 = g_prob * (p2 / (p1 + p2))

    tm = lg.shape[0]
    used = jnp.where((col == i1) | (col == i2), 1.0, 0.0)
    tt = lax.broadcasted_iota(I32, (tm, tm), 0)
    ss = lax.broadcasted_iota(I32, (tm, tm), 1)
    earlier = jnp.where(ss < tt, 1.0, 0.0).astype(BF16)
    before = jnp.dot(earlier, used.astype(BF16), preferred_element_type=F32) + cnt_ref[...]
    r1 = jnp.sum(jnp.where(col == i1, before, 0.0), axis=-1, keepdims=True)
    r2 = jnp.sum(jnp.where(col == i2, before, 0.0), axis=-1, keepdims=True)
    cnt_ref[...] = cnt_ref[...] + jnp.sum(used, axis=0, keepdims=True)
    counts_ref[...] = jnp.broadcast_to(cnt_ref[...], counts_ref.shape)

    route = jnp.zeros_like(lg)
    for c, v in ((ROUTE_E1, i1.astype(F32)), (ROUTE_E2, i2.astype(F32)), (ROUTE_G1, g1), (ROUTE_G2, g2),
                 (ROUTE_R1, r1), (ROUTE_R2, r2)):
        route = jnp.where(col == c, v, route)
    route_ref[...] = route


def _mid(x2d, oa, hb, wo, gx, wq, ck, cv, wco, gz, wr, br, B, S):
    T = B * S
    tm = min(TM_MID, S)
    per_b = S // tm
    row = lambda w: pl.BlockSpec((tm, w), lambda i: (i, 0))
    full = lambda a: pl.BlockSpec(a.shape, lambda i: (0,) * a.ndim)
    kvspec = pl.BlockSpec((N_MEM, D_MODEL), lambda i: (i // per_b, 0))
    return pl.pallas_call(
        _mid_kernel,
        grid=(T // tm,),
        in_specs=[row(D_MODEL), row(W_A_Q), row(W_B), full(wo), full(gx), full(wq), kvspec, kvspec,
                  full(wco), full(gz), full(wr), full(br)],
        out_specs=[row(D_MODEL), row(HALF), row(LANES), pl.BlockSpec((SUBLANES, LANES), lambda i: (0, 0))],
        out_shape=[jax.ShapeDtypeStruct((T, D_MODEL), F32),
                   jax.ShapeDtypeStruct((T, HALF), U32),
                   jax.ShapeDtypeStruct((T, LANES), F32),
                   jax.ShapeDtypeStruct((SUBLANES, LANES), F32)],
        scratch_shapes=[pltpu.VMEM((1, LANES), F32)],
        compiler_params=pltpu.CompilerParams(dimension_semantics=("arbitrary",), vmem_limit_bytes=VMEM_LIMIT),
        name="mid",
    )(x2d, oa, hb, wo, gx, wq, ck, cv, wco, gz, wr, br)


def _dispatch_kernel(zf_ref, pos_ref, hz_ref, xs_hbm, zbuf, sem, zsem):
    i = pl.program_id(0)
    tm = hz_ref.shape[0]
    tmx = zbuf.shape[0]

    @pl.when(i == 0)
    def _():
        zbuf[...] = jnp.zeros_like(zbuf)

        def fill(t):
            return pltpu.make_async_copy(zbuf, xs_hbm.at[pl.ds(t * tmx, tmx)], zsem)

        def start(t, carry):
            @pl.when(zf_ref[t] != 0)
            def _():
                fill(t).start()
            return carry

        def wait(t, carry):
            @pl.when(zf_ref[t] != 0)
            def _():
                fill(t).wait()
            return carry

        lax.fori_loop(0, zf_ref.shape[0], start, 0)
        lax.fori_loop(0, zf_ref.shape[0], wait, 0)

    for r in range(tm):
        for k in range(TOP_K):
            pltpu.make_async_copy(hz_ref.at[pl.ds(r, 1)], xs_hbm.at[pl.ds(pos_ref[0, 0, TOP_K * r + k], 1)],
                                  sem).start()
    for k in range(TOP_K):
        pltpu.make_async_copy(hz_ref, xs_hbm.at[pl.ds(0, tm)], sem).wait()


def _dispatch(hz_packed, pos, zfill, n_slots, tmx):
    T = hz_packed.shape[0]
    tm = pos.shape[2] // TOP_K
    grid_spec = pltpu.PrefetchScalarGridSpec(
        num_scalar_prefetch=1,
        grid=(T // tm,),
        in_specs=[pl.BlockSpec((1, 1, TOP_K * tm), lambda i, zf: (i, 0, 0), memory_space=pltpu.SMEM),
                  pl.BlockSpec((tm, HALF), lambda i, zf: (i, 0))],
        out_specs=pl.BlockSpec(memory_space=pl.ANY),
        scratch_shapes=[pltpu.VMEM((tmx, HALF), U32), pltpu.SemaphoreType.DMA(()), pltpu.SemaphoreType.DMA(())],
    )
    return pl.pallas_call(
        _dispatch_kernel,
        grid_spec=grid_spec,
        out_shape=jax.ShapeDtypeStruct((n_slots, HALF), U32),
        compiler_params=pltpu.CompilerParams(dimension_semantics=("arbitrary",), vmem_limit_bytes=VMEM_LIMIT),
        name="dispatch",
    )(zfill, pos, hz_packed)


def _expert_kernel(te_ref, nt_ref, xs_ref, wg_ref, wu_ref, wd_ref, ys_ref, wgb, wub, wdb):
    i = pl.program_id(0)
    nt = nt_ref[0]

    @pl.when(i < nt)
    def _():
        @pl.when((i == 0) | (te_ref[i] != te_ref[jnp.maximum(i - 1, 0)]))
        def _():
            wgb[...] = wg_ref[0].astype(BF16)
            wub[...] = wu_ref[0].astype(BF16)
            wdb[...] = wd_ref[0].astype(BF16)

        x = _unpack_halves(xs_ref[...]).astype(BF16)
        hg = jnp.dot(x, wgb[...], preferred_element_type=F32)
        hu = jnp.dot(x, wub[...], preferred_element_type=F32)
        a = (hg * jax.nn.sigmoid(hg) * hu).astype(BF16)
        ys_ref[...] = _pack_halves(jnp.dot(a, wdb[...], preferred_element_type=F32))

    @pl.when(i >= nt)
    def _():
        ys_ref[...] = jnp.zeros_like(ys_ref)


def _experts(xs, w_gate, w_up, w_down, tile_expert, ntiles, tmx):
    n_tiles_max = tile_expert.shape[0]
    wspec = lambda a: pl.BlockSpec((1,) + a.shape[1:], lambda i, te, nt: (te[i], 0, 0))
    grid_spec = pltpu.PrefetchScalarGridSpec(
        num_scalar_prefetch=2,
        grid=(n_tiles_max,),
        in_specs=[pl.BlockSpec((tmx, HALF), lambda i, te, nt: (jnp.minimum(i, nt[0] - 1), 0)),
                  wspec(w_gate), wspec(w_up), wspec(w_down)],
        out_specs=pl.BlockSpec((tmx, HALF), lambda i, te, nt: (i, 0)),
        scratch_shapes=[pltpu.VMEM((D_MODEL, D_EXPERT), BF16),
                        pltpu.VMEM((D_MODEL, D_EXPERT), BF16),
                        pltpu.VMEM((D_EXPERT, D_MODEL), BF16)],
    )
    return pl.pallas_call(
        _expert_kernel,
        grid_spec=grid_spec,
        out_shape=jax.ShapeDtypeStruct(xs.shape, U32),
        compiler_params=pltpu.CompilerParams(dimension_semantics=("arbitrary",), vmem_limit_bytes=VMEM_LIMIT),
        name="experts",
    )(tile_expert, ntiles, xs, w_gate, w_up, w_down)


def _final_kernel(posc_ref, posn_ref, x2_ref, route_ref, g_ref, ys_hbm, o_ref, ybuf, sem):
    i = pl.program_id(0)
    n = pl.num_programs(0)
    tm = x2_ref.shape[0]
    slot = i % 2

    def issue(pos_ref, s):
        for r in range(tm):
            for k in range(TOP_K):
                pltpu.make_async_copy(ys_hbm.at[pl.ds(pos_ref[0, 0, TOP_K * r + k], 1)],
                                      ybuf.at[s, k, pl.ds(r, 1)], sem.at[s]).start()

    @pl.when(i == 0)
    def _():
        issue(posc_ref, 0)

    @pl.when(i + 1 < n)
    def _():
        issue(posn_ref, 1 - slot)

    for k in range(TOP_K):
        pltpu.make_async_copy(ys_hbm.at[pl.ds(0, tm)], ybuf.at[slot, k], sem.at[slot]).wait()

    r = route_ref[...]
    g1 = r[:, ROUTE_G1:ROUTE_G1 + 1]
    g2 = r[:, ROUTE_G2:ROUTE_G2 + 1]
    xo = x2_ref[...] + g1 * _unpack_halves(ybuf[slot, 0]) + g2 * _unpack_halves(ybuf[slot, 1])
    o_ref[...] = _rms(xo, g_ref[...])


def _final(x2, ys, pos, route, g):
    T = x2.shape[0]
    nblk = pos.shape[0]
    tm = T // nblk
    row = lambda w: pl.BlockSpec((tm, w), lambda i: (i, 0))
    return pl.pallas_call(
        _final_kernel,
        grid=(nblk,),
        in_specs=[pl.BlockSpec((1, 1, TOP_K * tm), lambda i: (i, 0, 0), memory_space=pltpu.SMEM),
                  pl.BlockSpec((1, 1, TOP_K * tm), lambda i: (jnp.minimum(i + 1, nblk - 1), 0, 0),
                               memory_space=pltpu.SMEM),
                  row(D_MODEL), row(LANES), pl.BlockSpec(g.shape, lambda i: (0, 0)),
                  pl.BlockSpec(memory_space=pl.ANY)],
        out_specs=row(D_MODEL),
        out_shape=jax.ShapeDtypeStruct((T, D_MODEL), F32),
        scratch_shapes=[pltpu.VMEM((2, TOP_K, tm, HALF), U32), pltpu.SemaphoreType.DMA((2,))],
        compiler_params=pltpu.CompilerParams(dimension_semantics=("arbitrary",), vmem_limit_bytes=VMEM_LIMIT),
        name="final",
    )(pos, pos, x2, route, g, ys)


def _band_bias(table):
    i = jnp.arange(BLOCK)[:, None]
    j = jnp.arange(2 * BLOCK)[None, :]
    n = jnp.maximum(i + BLOCK - j, 0)
    nf = jnp.maximum(n, 1).astype(F32)
    large = MAX_EXACT + (jnp.log(nf / MAX_EXACT) / math.log(MAX_DISTANCE / MAX_EXACT)
                         * (NUM_BUCKETS - MAX_EXACT)).astype(I32)
    large = jnp.minimum(large, NUM_BUCKETS - 1)
    bucket = jnp.where(n < MAX_EXACT, n, large)
    return jnp.transpose(table.astype(F32)[bucket], (2, 0, 1))


def _dispatch_plan(route, counts_f, tmx, n_tiles_max, tm_rows):
    T = route.shape[0]
    experts = jnp.arange(N_EXPERTS, dtype=I32)
    counts = counts_f[0, :N_EXPERTS].astype(I32)
    ptiles = (counts + tmx - 1) // tmx
    tile_end = jnp.cumsum(ptiles)
    nt = tile_end[-1]
    row_off = (tile_end - ptiles) * tmx

    def slot(e_col, r_col):
        e = route[:, e_col].astype(I32)
        off = jnp.sum(jnp.where(e[:, None] == experts[None, :], row_off[None, :], 0), axis=1)
        return off + route[:, r_col].astype(I32)

    pos = jnp.stack([slot(ROUTE_E1, ROUTE_R1), slot(ROUTE_E2, ROUTE_R2)], axis=1)
    pos = pos.reshape(T // tm_rows, 1, tm_rows * TOP_K)

    tile_ids = jnp.arange(n_tiles_max, dtype=I32)
    expert_of = lambda t: jnp.sum((tile_end[None, :] <= t[:, None]).astype(I32), axis=1)
    te = expert_of(jnp.minimum(tile_ids, nt - 1))
    partial = jnp.any((tile_ids[:, None] == (tile_end - 1)[None, :]) & (counts % tmx != 0)[None, :], axis=1)
    zfill = (partial | (tile_ids >= nt)).astype(I32)
    return pos, te, nt.reshape(1), zfill


def kernel(x, mem, rel_bias_table, norm_mix, w_in, attn_sinks, conv_w, conv_b, gate_bias_i, gate_bias_f, mlstm_norm, w_out, norm_cross, norm_mem, w_cq, w_ck, w_cv, w_co, norm_moe, w_router_group, b_router_group, w_router_expert, b_router_expert, w_exp_gate, w_exp_up, w_exp_down, norm_final):
    B, S, _ = x.shape
    T = B * S
    depth = w_in.shape[0]
    x2d = x.reshape(T, D_MODEL)
    mem2d = mem.reshape(B * N_MEM, D_MODEL)
    bias = _band_bias(rel_bias_table)

    tmx = min(TM_EXPERT, T)
    n_tiles_max = (T * TOP_K) // tmx + N_EXPERTS
    tm_rows = min(TM_ROWDMA, T)

    assert depth == 1, "the final combine is fused with the final norm: single layer only"
    l = 0
    w_pad = jnp.pad(w_in[l], ((0, 0), (0, C_GATE + LANES - D_IN))).astype(BF16)
    wg_t = w_in[l][:, C_GATE:].T.astype(BF16)
    qa, kva, qkb, vb, ob, gc, gr = _inproj(x2d, norm_mix[l][None, :], w_pad, wg_t)

    out_a = _swa(qa, kva, bias, attn_sinks[l].astype(F32), B, S)

    gb = jnp.concatenate([gate_bias_i[l], gate_bias_f[l]]).astype(F32)
    gbias_col = jnp.pad(gb, (0, LANES - GATE_ROWS))[None, :]
    gbias_row = jnp.broadcast_to(gb[:, None], (GATE_ROWS, CHUNK))
    hb = _mlstm(qkb, vb, ob, gc, gr, conv_w[l][:, 0, :].astype(F32), conv_b[l][None, :].astype(F32),
                gbias_col, gbias_row, mlstm_norm[l][None, :].astype(F32), B, S)

    ck, cv = _memkv(mem2d, norm_mem[l][None, :], w_ck[l].astype(BF16), w_cv[l].astype(BF16), B)

    wr = jnp.pad(jnp.concatenate([w_router_expert[l], w_router_group[l]], axis=1),
                 ((0, 0), (0, LANES - N_EXPERTS - N_GROUPS))).astype(BF16)
    br = jnp.pad(jnp.concatenate([b_router_expert[l], b_router_group[l]]),
                 (0, LANES - N_EXPERTS - N_GROUPS)).astype(F32)[None, :]
    x2, hz_packed, route, counts = _mid(x2d, out_a, hb, w_out[l].astype(BF16), norm_cross[l][None, :],
                                        w_cq[l].astype(BF16), ck, cv, w_co[l].astype(BF16), norm_moe[l][None, :],
                                        wr, br, B, S)

    pos, te, nt, zfill = _dispatch_plan(route, counts, tmx, n_tiles_max, tm_rows)
    xs = _dispatch(hz_packed, pos, zfill, n_tiles_max * tmx, tmx)
    ys = _experts(xs, w_exp_gate[l], w_exp_up[l], w_exp_down[l], te, nt, tmx)
    out = _final(x2, ys, pos, route, norm_final[None, :])
    return out.reshape(B, S, D_MODEL)
```

```python
import functools
import math

import jax
import jax.numpy as jnp
from jax import lax
from jax.experimental import pallas as pl
from jax.experimental.pallas import tpu as pltpu

F32 = jnp.float32
BF16 = jnp.bfloat16
U32 = jnp.uint32
I32 = jnp.int32

D_MODEL = 1024
N_MEM = 256
N_HEADS_A = 8
N_KV_A = 2
HEAD_DIM_A = 64
BLOCK = 128
WINDOW = 128
NUM_BUCKETS = 32
MAX_EXACT = NUM_BUCKETS // 2
MAX_DISTANCE = 128
N_HEADS_B = 4
HEAD_DIM_B = 128
CHUNK = 128
CONV_WIDTH = 4
N_HEADS_X = 4
HEAD_DIM_X = D_MODEL // N_HEADS_X
N_GROUPS = 4
EXPERTS_PER_GROUP = 8
N_EXPERTS = N_GROUPS * EXPERTS_PER_GROUP
TOP_K = 2
D_EXPERT = 512
EPS = 1e-6
NEG_INF = -1e30

W_A_Q = N_HEADS_A * HEAD_DIM_A
W_A_KV = N_KV_A * HEAD_DIM_A
W_B = N_HEADS_B * HEAD_DIM_B
C_QA = 0
C_KVA = C_QA + W_A_Q
C_QKB = C_KVA + 2 * W_A_KV
C_VB = C_QKB + 2 * W_B
C_OB = C_VB + W_B
C_GATE = C_OB + W_B
D_IN = C_GATE + 2 * N_HEADS_B

LANES = 128
SUBLANES = 8
GATE_ROWS = 8
HALF = D_MODEL // 2

TM_INPROJ = 512
TM_MID = 256
TM_ROWDMA = 256
TM_EXPERT = 256

VMEM_LIMIT = 48 * 1024 * 1024


def _rms(xf, g):
    return xf * lax.rsqrt(jnp.mean(xf * xf, axis=-1, keepdims=True) + EPS) * g


def _pack_halves(v):
    b = pltpu.bitcast(v.astype(BF16).astype(F32), U32)
    return (b[:, :HALF] >> 16) | b[:, HALF:]


def _unpack_halves(p):
    lo = pltpu.bitcast(p << 16, F32)
    hi = pltpu.bitcast(p & jnp.uint32(0xFFFF0000), F32)
    return jnp.concatenate([lo, hi], axis=-1)


def _log_sigmoid(z):
    return jnp.minimum(z, 0.0) - jnp.log1p(jnp.exp(-jnp.abs(z)))


def _inproj_kernel(x_ref, g_ref, w_ref, wgt_ref, gbc_ref, gbr_ref,
                   qa_ref, kva_ref, qkb_ref, vb_ref, ob_ref, gc_ref, gr_ref):
    tm = x_ref.shape[0]
    h = _rms(x_ref[...], g_ref[...]).astype(BF16)

    def mm(lo, hi):
        return jnp.dot(h, w_ref[:, lo:hi], preferred_element_type=F32)

    qa_ref[...] = mm(C_QA, C_KVA).astype(BF16)
    kva_ref[...] = mm(C_KVA, C_QKB).astype(BF16)
    qkb_ref[...] = mm(C_QKB, C_VB).astype(BF16)
    vb_ref[...] = mm(C_VB, C_OB).astype(BF16)
    ob_ref[...] = mm(C_OB, C_GATE).astype(BF16)

    H, L = N_HEADS_B, CHUNK
    gcol = mm(C_GATE, C_GATE + LANES) + gbc_ref[...]
    grow = lax.dot_general(wgt_ref[...], h, (((1,), (1,)), ((), ())), preferred_element_type=F32) + gbr_ref[...]
    lane_c = lax.broadcasted_iota(I32, (L, LANES), 1)
    is_f_col = (lane_c >= H) & (lane_c < 2 * H)
    is_f_row = lax.broadcasted_iota(I32, (GATE_ROWS, L), 0) >= H
    ti = lax.broadcasted_iota(I32, (L, L), 0)
    si = lax.broadcasted_iota(I32, (L, L), 1)
    tril = jnp.where(si <= ti, 1.0, 0.0).astype(F32)
    triu = jnp.where(si >= ti, 1.0, 0.0).astype(F32)
    for c in range(tm // L):
        rows = slice(c * L, (c + 1) * L)
        gcol_c = gcol[rows, :]
        fcol = jnp.where(is_f_col, _log_sigmoid(gcol_c), 0.0)
        bcol = jnp.dot(tril, fcol, preferred_element_type=F32, precision=lax.Precision.HIGHEST)
        gc_ref[rows, :] = jnp.where(is_f_col, bcol, gcol_c)
        grow_c = grow[:, rows]
        frow = jnp.where(is_f_row, _log_sigmoid(grow_c), 0.0)
        brow = jnp.dot(frow, triu, preferred_element_type=F32, precision=lax.Precision.HIGHEST)
        gr_ref[:, rows] = jnp.where(is_f_row, brow, grow_c)


def _inproj(x2d, g, w_pad, wg_t, gbias_col, gbias_row):
    T = x2d.shape[0]
    tm = gbias_row.shape[1]
    row = lambda w: pl.BlockSpec((tm, w), lambda i: (i, 0))
    full = lambda a: pl.BlockSpec(a.shape, lambda i: (0,) * a.ndim)
    return pl.pallas_call(
        _inproj_kernel,
        grid=(T // tm,),
        in_specs=[row(D_MODEL), full(g), full(w_pad), full(wg_t), full(gbias_col), full(gbias_row)],
        out_specs=[row(W_A_Q), row(2 * W_A_KV), row(2 * W_B), row(W_B), row(W_B), row(LANES),
                   pl.BlockSpec((GATE_ROWS, tm), lambda i: (0, i))],
        out_shape=[jax.ShapeDtypeStruct((T, W_A_Q), BF16),
                   jax.ShapeDtypeStruct((T, 2 * W_A_KV), BF16),
                   jax.ShapeDtypeStruct((T, 2 * W_B), BF16),
                   jax.ShapeDtypeStruct((T, W_B), BF16),
                   jax.ShapeDtypeStruct((T, W_B), BF16),
                   jax.ShapeDtypeStruct((T, LANES), F32),
                   jax.ShapeDtypeStruct((GATE_ROWS, T), F32)],
        compiler_params=pltpu.CompilerParams(dimension_semantics=("parallel",), vmem_limit_bytes=VMEM_LIMIT),
        name="inproj",
    )(x2d, g, w_pad, wg_t, gbias_col, gbias_row)


def _swa_kernel(sink_ref, qa_ref, kvc_ref, kvp_ref, bias_ref, o_ref):
    n = pl.program_id(1)
    q = qa_ref[...]
    kvp = kvp_ref[...].astype(F32)
    kvc = kvc_ref[...].astype(F32)
    kband = jnp.concatenate([kvp[:, :W_A_KV], kvc[:, :W_A_KV]], axis=0)
    vband = jnp.concatenate([kvp[:, W_A_KV:], kvc[:, W_A_KV:]], axis=0)
    lane = lax.broadcasted_iota(I32, (2 * BLOCK, LANES), 1)
    lo = lane < HEAD_DIM_A

    def placements(band):
        swapped = pltpu.roll(band, HEAD_DIM_A, axis=1)
        z = jnp.zeros_like(band)
        return {(0, 0): jnp.where(lo, band, z).astype(BF16), (0, 1): jnp.where(lo, z, swapped).astype(BF16),
                (1, 0): jnp.where(lo, swapped, z).astype(BF16), (1, 1): jnp.where(lo, z, band).astype(BF16)}

    kpl = placements(kband)
    vpl = placements(vband)

    i = lax.broadcasted_iota(I32, (BLOCK, 2 * BLOCK), 0)
    j = lax.broadcasted_iota(I32, (BLOCK, 2 * BLOCK), 1)
    d = i + BLOCK - j
    ok = (d >= 0) & (d < WINDOW) & ((n > 0) | (j >= BLOCK))
    scale = HEAD_DIM_A ** -0.5
    group = N_HEADS_A // N_KV_A

    for pair in range(N_HEADS_A // 2):
        qt = q[:, pair * LANES:(pair + 1) * LANES]
        acc = None
        for half in range(2):
            h = 2 * pair + half
            g = h // group
            s = lax.dot_general(qt, kpl[(g, half)], (((1,), (1,)), ((), ())), preferred_element_type=F32)
            s = s * scale + bias_ref[h]
            s = jnp.where(ok, s, NEG_INF)
            sink = sink_ref[h]
            m = jnp.maximum(jnp.max(s, axis=-1, keepdims=True), sink)
            p = jnp.exp(s - m)
            denom = jnp.sum(p, axis=-1, keepdims=True) + jnp.exp(sink - m)
            o = jnp.dot(p.astype(BF16), vpl[(g, half)], preferred_element_type=F32) / denom
            acc = o if acc is None else acc + o
        o_ref[:, pair * LANES:(pair + 1) * LANES] = acc.astype(BF16)


def _swa(qa, kva, bias, sinks, B, S):
    nb = S // BLOCK
    T = B * S
    return pl.pallas_call(
        _swa_kernel,
        grid=(B, nb),
        in_specs=[pl.BlockSpec(memory_space=pltpu.SMEM),
                  pl.BlockSpec((BLOCK, W_A_Q), lambda b, n: (b * nb + n, 0)),
                  pl.BlockSpec((BLOCK, 2 * W_A_KV), lambda b, n: (b * nb + n, 0)),
                  pl.BlockSpec((BLOCK, 2 * W_A_KV), lambda b, n: (b * nb + jnp.maximum(n - 1, 0), 0)),
                  pl.BlockSpec(bias.shape, lambda b, n: (0, 0, 0))],
        out_specs=pl.BlockSpec((BLOCK, W_A_Q), lambda b, n: (b * nb + n, 0)),
        out_shape=jax.ShapeDtypeStruct((T, W_A_Q), BF16),
        compiler_params=pltpu.CompilerParams(dimension_semantics=("parallel", "parallel"),
                                             vmem_limit_bytes=VMEM_LIMIT),
        name="swa",
    )(sinks, qa, kva, kva, bias)


CONV_HALO = 16


def _mlstm_kernel(qkc_ref, qkp_ref, vb_ref, ob_ref, gc_ref, gr_ref, cw_ref, cb_ref, nrm_ref,
                  o_ref, state_ref, m_ref):
    c = pl.program_id(1)
    H, D, L = N_HEADS_B, HEAD_DIM_B, CHUNK

    @pl.when(c == 0)
    def _():
        state_ref[...] = jnp.zeros_like(state_ref)
        m_ref[...] = jnp.zeros_like(m_ref)

    prev = qkp_ref[...]
    prev = jnp.where(c > 0, prev, jnp.zeros_like(prev))
    cur = qkc_ref[...]
    ext = jnp.concatenate([prev, cur], axis=0)
    rr = lax.broadcasted_iota(I32, (L, CONV_HALO + L), 0)
    cc = lax.broadcasted_iota(I32, (L, CONV_HALO + L), 1)
    y = cb_ref[...] + cw_ref[CONV_WIDTH - 1:CONV_WIDTH, :] * cur.astype(F32)
    for delay in range(1, CONV_WIDTH):
        shift = jnp.where(cc == rr + (CONV_HALO - delay), 1.0, 0.0).astype(BF16)
        tap = CONV_WIDTH - 1 - delay
        y = y + cw_ref[tap:tap + 1, :] * jnp.dot(shift, ext, preferred_element_type=F32)
    qk = y * jax.nn.sigmoid(y)

    gcol = gc_ref[...]
    grow = gr_ref[...]
    ti = lax.broadcasted_iota(I32, (L, L), 0)
    si = lax.broadcasted_iota(I32, (L, L), 1)
    tri = si <= ti
    ones_blk = jnp.ones((L, D), BF16)

    states = [state_ref[h] for h in range(H)]
    m_all = m_ref[...]
    new_states, new_m, outs = [], [], []

    for h in range(H):
        qh = (qk[:, h * D:(h + 1) * D] * (D ** -0.5)).astype(BF16)
        k_t = qk[:, W_B + h * D:W_B + (h + 1) * D].T
        v1 = jnp.concatenate([vb_ref[:, h * D:(h + 1) * D], ones_blk], axis=-1)
        b_r = grow[H + h:H + h + 1, :]
        g_r = grow[h:h + 1, :] - b_r
        b_c = gcol[:, H + h:H + h + 1]
        m_prev = m_all[h:h + 1, 0:1]
        state = states[h]

        gmat = jnp.where(tri, g_r, NEG_INF)
        m_c = jnp.maximum(jnp.max(gmat, axis=-1, keepdims=True), m_prev)
        a_inter = jnp.exp(m_prev - m_c)
        sc = jnp.dot(qh, k_t.astype(BF16), preferred_element_type=F32) * jnp.exp(gmat - m_c)
        tot = (jnp.dot(sc.astype(BF16), v1, preferred_element_type=F32)
               + a_inter * jnp.dot(qh, state.astype(BF16), preferred_element_type=F32))
        num = tot[:, :D]
        den = tot[:, D:]
        hh = num / jnp.maximum(jnp.abs(den), jnp.exp(-(b_c + m_c)))

        b_last = b_r[:, L - 1:L]
        m_new = jnp.maximum(b_last + m_prev, b_last + jnp.max(g_r, axis=-1, keepdims=True))
        w_r = jnp.exp(g_r + (b_last - m_new))
        decay = jnp.exp(b_last + m_prev - m_new)
        upd = jnp.dot((k_t * w_r).astype(BF16), v1, preferred_element_type=F32)
        new_states.append(decay * state + upd)
        new_m.append(jnp.broadcast_to(m_new, (1, LANES)))

        og = jax.nn.sigmoid(ob_ref[:, h * D:(h + 1) * D].astype(F32))
        hb = og * hh
        hb = hb * lax.rsqrt(jnp.mean(hb * hb, axis=-1, keepdims=True) + EPS)
        outs.append((hb * nrm_ref[:, h * D:(h + 1) * D]).astype(BF16))

    for h in range(H):
        state_ref[h] = new_states[h]
        m_ref[h:h + 1, :] = new_m[h]
        o_ref[:, h * D:(h + 1) * D] = outs[h]


def _mlstm(qkb, vb, ob, gc, gr, conv_w, conv_b, nrm, B, S):
    nc = S // CHUNK
    T = B * S
    halo_per_chunk = CHUNK // CONV_HALO
    blk = lambda w: pl.BlockSpec((CHUNK, w), lambda b, c: (b * nc + c, 0))
    full = lambda a: pl.BlockSpec(a.shape, lambda b, c: (0,) * a.ndim)
    return pl.pallas_call(
        _mlstm_kernel,
        grid=(B, nc),
        in_specs=[blk(2 * W_B),
                  pl.BlockSpec((CONV_HALO, 2 * W_B),
                               lambda b, c: (jnp.maximum((b * nc + c) * halo_per_chunk - 1, 0), 0)),
                  blk(W_B), blk(W_B), blk(LANES),
                  pl.BlockSpec((GATE_ROWS, CHUNK), lambda b, c: (0, b * nc + c)),
                  full(conv_w), full(conv_b), full(nrm)],
        out_specs=blk(W_B),
        out_shape=jax.ShapeDtypeStruct((T, W_B), BF16),
        scratch_shapes=[pltpu.VMEM((N_HEADS_B, HEAD_DIM_B, 2 * HEAD_DIM_B), F32),
                        pltpu.VMEM((GATE_ROWS, LANES), F32)],
        compiler_params=pltpu.CompilerParams(dimension_semantics=("arbitrary", "arbitrary"),
                                             vmem_limit_bytes=VMEM_LIMIT),
        name="mlstm",
    )(qkb, qkb, vb, ob, gc, gr, conv_w, conv_b, nrm)


def _memkv_kernel(mem_ref, g_ref, wk_ref, wv_ref, k_ref, v_ref):
    hm = _rms(mem_ref[...], g_ref[...]).astype(BF16)
    k_ref[...] = jnp.dot(hm, wk_ref[...], preferred_element_type=F32).astype(BF16)
    v_ref[...] = jnp.dot(hm, wv_ref[...], preferred_element_type=F32).astype(BF16)


def _memkv(mem2d, g, wk, wv, B):
    full = lambda a: pl.BlockSpec(a.shape, lambda b: (0,) * a.ndim)
    blk = pl.BlockSpec((N_MEM, D_MODEL), lambda b: (b, 0))
    return pl.pallas_call(
        _memkv_kernel,
        grid=(B,),
        in_specs=[blk, full(g), full(wk), full(wv)],
        out_specs=[blk, blk],
        out_shape=[jax.ShapeDtypeStruct((B * N_MEM, D_MODEL), BF16)] * 2,
        compiler_params=pltpu.CompilerParams(dimension_semantics=("parallel",), vmem_limit_bytes=VMEM_LIMIT),
        name="memkv",
    )(mem2d, g, wk, wv)


ROUTE_E1, ROUTE_E2, ROUTE_G1, ROUTE_G2, ROUTE_R1, ROUTE_R2 = 0, 1, 2, 3, 4, 5
ROUTER_GROUP_COL = N_EXPERTS


def _mid_kernel(x_ref, oa_ref, hb_ref, wo_ref, gx_ref, wq_ref, ck_ref, cv_ref, wco_ref, gz_ref, wr_ref, br_ref,
                x2_ref, hz_ref, route_ref, counts_ref, cnt_ref):
    @pl.when(pl.program_id(0) == 0)
    def _():
        cnt_ref[...] = jnp.zeros_like(cnt_ref)

    x1 = (x_ref[...]
          + jnp.dot(oa_ref[...], wo_ref[0:W_A_Q, :], preferred_element_type=F32)
          + jnp.dot(hb_ref[...], wo_ref[W_A_Q:, :], preferred_element_type=F32))

    hc = _rms(x1, gx_ref[...]).astype(BF16)
    cq = jnp.dot(hc, wq_ref[...], preferred_element_type=F32).astype(BF16)
    scale = HEAD_DIM_X ** -0.5
    heads = []
    for h in range(N_HEADS_X):
        sl = slice(h * HEAD_DIM_X, (h + 1) * HEAD_DIM_X)
        s = lax.dot_general(cq[:, sl], ck_ref[:, sl], (((1,), (1,)), ((), ())), preferred_element_type=F32) * scale
        p = jnp.exp(s - jnp.max(s, axis=-1, keepdims=True))
        co = jnp.dot(p.astype(BF16), cv_ref[:, sl], preferred_element_type=F32) / jnp.sum(p, axis=-1, keepdims=True)
        heads.append(co.astype(BF16))
    x2 = x1 + jnp.dot(jnp.concatenate(heads, axis=-1), wco_ref[...], preferred_element_type=F32)
    x2_ref[...] = x2

    hz = _rms(x2, gz_ref[...])
    hz_ref[...] = _pack_halves(hz)
    lg = jnp.dot(hz.astype(BF16), wr_ref[...], preferred_element_type=F32) + br_ref[...]
    col = lax.broadcasted_iota(I32, lg.shape, 1)
    big = jnp.int32(LANES)
    is_g = (col >= ROUTER_GROUP_COL) & (col < ROUTER_GROUP_COL + N_GROUPS)
    gl = jnp.where(is_g, lg, NEG_INF)
    gmax = jnp.max(gl, axis=-1, keepdims=True)
    gsum = jnp.sum(jnp.exp(gl - gmax), axis=-1, keepdims=True)
    g_prob = 1.0 / gsum
    g_idx = jnp.min(jnp.where(gl == gmax, col - ROUTER_GROUP_COL, big), axis=-1, keepdims=True)
    sel = (col < N_EXPERTS) & ((col // EXPERTS_PER_GROUP) == g_idx)
    el = jnp.where(sel, lg, NEG_INF)
    m1 = jnp.max(el, axis=-1, keepdims=True)
    i1 = jnp.min(jnp.where(el == m1, col, big), axis=-1, keepdims=True)
    el2 = jnp.where(col == i1, NEG_INF, el)
    m2 = jnp.max(el2, axis=-1, keepdims=True)
    i2 = jnp.min(jnp.where(el2 == m2, col, big), axis=-1, keepdims=True)
    z = jnp.sum(jnp.exp(el - m1), axis=-1, keepdims=True)
    p1 = 1.0 / z
    p2 = jnp.exp(m2 - m1) / z
    g1 = g_prob * (p1 / (p1 + p2))
    g2 = g_prob * (p2 / (p1 + p2))

    tm = lg.shape[0]
    used = jnp.where((col == i1) | (col == i2), 1.0, 0.0)
    tt = lax.broadcasted_iota(I32, (tm, tm), 0)
    ss = lax.broadcasted_iota(I32, (tm, tm), 1)
    earlier = jnp.where(ss < tt, 1.0, 0.0).astype(BF16)
    before = jnp.dot(earlier, used.astype(BF16), preferred_element_type=F32) + cnt_ref[...]
    r1 = jnp.sum(jnp.where(col == i1, before, 0.0), axis=-1, keepdims=True)
    r2 = jnp.sum(jnp.where(col == i2, before, 0.0), axis=-1, keepdims=True)
    cnt_ref[...] = cnt_ref[...] + jnp.sum(used, axis=0, keepdims=True)
    counts_ref[...] = jnp.broadcast_to(cnt_ref[...], counts_ref.shape)

    route = jnp.zeros_like(lg)
    for c, v in ((ROUTE_E1, i1.astype(F32)), (ROUTE_E2, i2.astype(F32)), (ROUTE_G1, g1), (ROUTE_G2, g2),
                 (ROUTE_R1, r1), (ROUTE_R2, r2)):
        route = jnp.where(col == c, v, route)
    route_ref[...] = route


def _mid(x2d, oa, hb, wo, gx, wq, ck, cv, wco, gz, wr, br, B, S):
    T = B * S
    tm = min(TM_MID, S)
    per_b = S // tm
    row = lambda w: pl.BlockSpec((tm, w), lambda i: (i, 0))
    full = lambda a: pl.BlockSpec(a.shape, lambda i: (0,) * a.ndim)
    kvspec = pl.BlockSpec((N_MEM, D_MODEL), lambda i: (i // per_b, 0))
    return pl.pallas_call(
        _mid_kernel,
        grid=(T // tm,),
        in_specs=[row(D_MODEL), row(W_A_Q), row(W_B), full(wo), full(gx), full(wq), kvspec, kvspec,
                  full(wco), full(gz), full(wr), full(br)],
        out_specs=[row(D_MODEL), row(HALF), row(LANES), pl.BlockSpec((SUBLANES, LANES), lambda i: (0, 0))],
        out_shape=[jax.ShapeDtypeStruct((T, D_MODEL), F32),
                   jax.ShapeDtypeStruct((T, HALF), U32),
                   jax.ShapeDtypeStruct((T, LANES), F32),
                   jax.ShapeDtypeStruct((SUBLANES, LANES), F32)],
        scratch_shapes=[pltpu.VMEM((1, LANES), F32)],
        compiler_params=pltpu.CompilerParams(dimension_semantics=("arbitrary",), vmem_limit_bytes=VMEM_LIMIT),
        name="mid",
    )(x2d, oa, hb, wo, gx, wq, ck, cv, wco, gz, wr, br)


def _dispatch_kernel(zf_ref, pos_ref, hz_ref, xs_hbm, zbuf, sem, zsem):
    i = pl.program_id(0)
    tm = hz_ref.shape[0]
    tmx = zbuf.shape[0]

    @pl.when(i == 0)
    def _():
        zbuf[...] = jnp.zeros_like(zbuf)

        def fill(t):
            return pltpu.make_async_copy(zbuf, xs_hbm.at[pl.ds(t * tmx, tmx)], zsem)

        def start(t, carry):
            @pl.when(zf_ref[t] != 0)
            def _():
                fill(t).start()
            return carry

        def wait(t, carry):
            @pl.when(zf_ref[t] != 0)
            def _():
                fill(t).wait()
            return carry

        lax.fori_loop(0, zf_ref.shape[0], start, 0)
        lax.fori_loop(0, zf_ref.shape[0], wait, 0)

    for r in range(tm):
        for k in range(TOP_K):
            pltpu.make_async_copy(hz_ref.at[pl.ds(r, 1)], xs_hbm.at[pl.ds(pos_ref[0, 0, TOP_K * r + k], 1)],
                                  sem).start(priority=k % 2)
    for k in range(TOP_K):
        pltpu.make_async_copy(hz_ref, xs_hbm.at[pl.ds(0, tm)], sem).wait()


def _dispatch(hz_packed, pos, zfill, n_slots, tmx):
    T = hz_packed.shape[0]
    tm = pos.shape[2] // TOP_K
    grid_spec = pltpu.PrefetchScalarGridSpec(
        num_scalar_prefetch=1,
        grid=(T // tm,),
        in_specs=[pl.BlockSpec((1, 1, TOP_K * tm), lambda i, zf: (i, 0, 0), memory_space=pltpu.SMEM),
                  pl.BlockSpec((tm, HALF), lambda i, zf: (i, 0))],
        out_specs=pl.BlockSpec(memory_space=pl.ANY),
        scratch_shapes=[pltpu.VMEM((tmx, HALF), U32), pltpu.SemaphoreType.DMA(()), pltpu.SemaphoreType.DMA(())],
    )
    return pl.pallas_call(
        _dispatch_kernel,
        grid_spec=grid_spec,
        out_shape=jax.ShapeDtypeStruct((n_slots, HALF), U32),
        compiler_params=pltpu.CompilerParams(dimension_semantics=("arbitrary",), vmem_limit_bytes=VMEM_LIMIT),
        name="dispatch",
    )(zfill, pos, hz_packed)


def _expert_kernel(te_ref, nt_ref, xs_ref, wg_ref, wu_ref, wd_ref, ys_ref, wgb, wub, wdb):
    i = pl.program_id(0)
    nt = nt_ref[0]

    @pl.when(i < nt)
    def _():
        @pl.when((i == 0) | (te_ref[i] != te_ref[jnp.maximum(i - 1, 0)]))
        def _():
            wgb[...] = wg_ref[0].astype(BF16)
            wub[...] = wu_ref[0].astype(BF16)
            wdb[...] = wd_ref[0].astype(BF16)

        x = _unpack_halves(xs_ref[...]).astype(BF16)
        hg = jnp.dot(x, wgb[...], preferred_element_type=F32)
        hu = jnp.dot(x, wub[...], preferred_element_type=F32)
        a = (hg * jax.nn.sigmoid(hg) * hu).astype(BF16)
        ys_ref[...] = _pack_halves(jnp.dot(a, wdb[...], preferred_element_type=F32))

    @pl.when(i >= nt)
    def _():
        ys_ref[...] = jnp.zeros_like(ys_ref)


def _experts(xs, w_gate, w_up, w_down, tile_expert, ntiles, tmx):
    n_tiles_max = tile_expert.shape[0]
    wspec = lambda a: pl.BlockSpec((1,) + a.shape[1:], lambda i, te, nt: (te[i], 0, 0))
    grid_spec = pltpu.PrefetchScalarGridSpec(
        num_scalar_prefetch=2,
        grid=(n_tiles_max,),
        in_specs=[pl.BlockSpec((tmx, HALF), lambda i, te, nt: (jnp.minimum(i, nt[0] - 1), 0)),
                  wspec(w_gate), wspec(w_up), wspec(w_down)],
        out_specs=pl.BlockSpec((tmx, HALF), lambda i, te, nt: (i, 0)),
        scratch_shapes=[pltpu.VMEM((D_MODEL, D_EXPERT), BF16),
                        pltpu.VMEM((D_MODEL, D_EXPERT), BF16),
                        pltpu.VMEM((D_EXPERT, D_MODEL), BF16)],
    )
    return pl.pallas_call(
        _expert_kernel,
        grid_spec=grid_spec,
        out_shape=jax.ShapeDtypeStruct(xs.shape, U32),
        compiler_params=pltpu.CompilerParams(dimension_semantics=("arbitrary",), vmem_limit_bytes=VMEM_LIMIT),
        name="experts",
    )(tile_expert, ntiles, xs, w_gate, w_up, w_down)


def _final_kernel(posc_ref, posn_ref, x2_ref, route_ref, g_ref, ys_hbm, o_ref, ybuf, sem):
    i = pl.program_id(0)
    n = pl.num_programs(0)
    tm = x2_ref.shape[0]
    slot = i % 2

    def issue(pos_ref, s):
        for r in range(tm):
            for k in range(TOP_K):
                pltpu.make_async_copy(ys_hbm.at[pl.ds(pos_ref[0, 0, TOP_K * r + k], 1)],
                                      ybuf.at[s, k, pl.ds(r, 1)], sem.at[s]).start(priority=k % 2)

    @pl.when(i == 0)
    def _():
        issue(posc_ref, 0)

    @pl.when(i + 1 < n)
    def _():
        issue(posn_ref, 1 - slot)

    for k in range(TOP_K):
        pltpu.make_async_copy(ys_hbm.at[pl.ds(0, tm)], ybuf.at[slot, k], sem.at[slot]).wait()

    r = route_ref[...]
    g1 = r[:, ROUTE_G1:ROUTE_G1 + 1]
    g2 = r[:, ROUTE_G2:ROUTE_G2 + 1]
    xo = x2_ref[...] + g1 * _unpack_halves(ybuf[slot, 0]) + g2 * _unpack_halves(ybuf[slot, 1])
    o_ref[...] = _rms(xo, g_ref[...])


def _final(x2, ys, pos, route, g):
    T = x2.shape[0]
    nblk = pos.shape[0]
    tm = T // nblk
    row = lambda w: pl.BlockSpec((tm, w), lambda i: (i, 0))
    return pl.pallas_call(
        _final_kernel,
        grid=(nblk,),
        in_specs=[pl.BlockSpec((1, 1, TOP_K * tm), lambda i: (i, 0, 0), memory_space=pltpu.SMEM),
                  pl.BlockSpec((1, 1, TOP_K * tm), lambda i: (jnp.minimum(i + 1, nblk - 1), 0, 0),
                               memory_space=pltpu.SMEM),
                  row(D_MODEL), row(LANES), pl.BlockSpec(g.shape, lambda i: (0, 0)),
                  pl.BlockSpec(memory_space=pl.ANY)],
        out_specs=row(D_MODEL),
        out_shape=jax.ShapeDtypeStruct((T, D_MODEL), F32),
        scratch_shapes=[pltpu.VMEM((2, TOP_K, tm, HALF), U32), pltpu.SemaphoreType.DMA((2,))],
        compiler_params=pltpu.CompilerParams(dimension_semantics=("arbitrary",), vmem_limit_bytes=VMEM_LIMIT),
        name="final",
    )(pos, pos, x2, route, g, ys)


def _band_bias(table):
    i = jnp.arange(BLOCK)[:, None]
    j = jnp.arange(2 * BLOCK)[None, :]
    n = jnp.maximum(i + BLOCK - j, 0)
    nf = jnp.maximum(n, 1).astype(F32)
    large = MAX_EXACT + (jnp.log(nf / MAX_EXACT) / math.log(MAX_DISTANCE / MAX_EXACT)
                         * (NUM_BUCKETS - MAX_EXACT)).astype(I32)
    large = jnp.minimum(large, NUM_BUCKETS - 1)
    bucket = jnp.where(n < MAX_EXACT, n, large)
    onehot = (bucket[:, :, None] == jnp.arange(NUM_BUCKETS)[None, None, :]).astype(F32)
    return jnp.einsum("ijb,bh->hij", onehot, table.astype(F32), precision=lax.Precision.HIGHEST)


def _dispatch_plan(route, counts_f, tmx, n_tiles_max, tm_rows):
    T = route.shape[0]
    experts = jnp.arange(N_EXPERTS, dtype=I32)
    counts = counts_f[0, :N_EXPERTS].astype(I32)
    ptiles = (counts + tmx - 1) // tmx
    tile_end = jnp.cumsum(ptiles)
    nt = tile_end[-1]
    row_off = (tile_end - ptiles) * tmx

    def slot(e_col, r_col):
        e = route[:, e_col].astype(I32)
        off = jnp.sum(jnp.where(e[:, None] == experts[None, :], row_off[None, :], 0), axis=1)
        return off + route[:, r_col].astype(I32)

    pos = jnp.stack([slot(ROUTE_E1, ROUTE_R1), slot(ROUTE_E2, ROUTE_R2)], axis=1)
    pos = pos.reshape(T // tm_rows, 1, tm_rows * TOP_K)

    tile_ids = jnp.arange(n_tiles_max, dtype=I32)
    expert_of = lambda t: jnp.sum((tile_end[None, :] <= t[:, None]).astype(I32), axis=1)
    te = expert_of(jnp.minimum(tile_ids, nt - 1))
    partial = jnp.any((tile_ids[:, None] == (tile_end - 1)[None, :]) & (counts % tmx != 0)[None, :], axis=1)
    zfill = (partial | (tile_ids >= nt)).astype(I32)
    return pos, te, nt.reshape(1), zfill


def kernel(x, mem, rel_bias_table, norm_mix, w_in, attn_sinks, conv_w, conv_b, gate_bias_i, gate_bias_f, mlstm_norm, w_out, norm_cross, norm_mem, w_cq, w_ck, w_cv, w_co, norm_moe, w_router_group, b_router_group, w_router_expert, b_router_expert, w_exp_gate, w_exp_up, w_exp_down, norm_final):
    B, S, _ = x.shape
    T = B * S
    depth = w_in.shape[0]
    x2d = x.reshape(T, D_MODEL)
    mem2d = mem.reshape(B * N_MEM, D_MODEL)
    bias = _band_bias(rel_bias_table)

    tmx = min(TM_EXPERT, T)
    n_tiles_max = (T * TOP_K) // tmx + N_EXPERTS
    tm_rows = min(TM_ROWDMA, T)

    assert depth == 1, "the final combine is fused with the final norm: single layer only"
    l = 0
    w_pad = jnp.pad(w_in[l], ((0, 0), (0, C_GATE + LANES - D_IN))).astype(BF16)
    wg_t = w_in[l][:, C_GATE:].T.astype(BF16)
    gb = jnp.concatenate([gate_bias_i[l], gate_bias_f[l]]).astype(F32)
    gbias_col = jnp.pad(gb, (0, LANES - GATE_ROWS))[None, :]
    gbias_row = jnp.broadcast_to(gb[:, None], (GATE_ROWS, min(TM_INPROJ, S)))
    qa, kva, qkb, vb, ob, gc, gr = _inproj(x2d, norm_mix[l][None, :], w_pad, wg_t, gbias_col, gbias_row)

    out_a = _swa(qa, kva, bias, attn_sinks[l].astype(F32), B, S)
    hb = _mlstm(qkb, vb, ob, gc, gr, conv_w[l][:, 0, :].astype(F32), conv_b[l][None, :].astype(F32),
                mlstm_norm[l][None, :].astype(F32), B, S)

    ck, cv = _memkv(mem2d, norm_mem[l][None, :], w_ck[l].astype(BF16), w_cv[l].astype(BF16), B)

    wr = jnp.pad(jnp.concatenate([w_router_expert[l], w_router_group[l]], axis=1),
                 ((0, 0), (0, LANES - N_EXPERTS - N_GROUPS))).astype(BF16)
    br = jnp.pad(jnp.concatenate([b_router_expert[l], b_router_group[l]]),
                 (0, LANES - N_EXPERTS - N_GROUPS)).astype(F32)[None, :]
    x2, hz_packed, route, counts = _mid(x2d, out_a, hb, w_out[l].astype(BF16), norm_cross[l][None, :],
                                        w_cq[l].astype(BF16), ck, cv, w_co[l].astype(BF16), norm_moe[l][None, :],
                                        wr, br, B, S)

    pos, te, nt, zfill = _dispatch_plan(route, counts, tmx, n_tiles_max, tm_rows)
    xs = _dispatch(hz_packed, pos, zfill, n_tiles_max * tmx, tmx)
    ys = _experts(xs, w_exp_gate[l], w_exp_up[l], w_exp_down[l], te, nt, tmx)
    out = _final(x2, ys, pos, route, norm_final[None, :])
    return out.reshape(B, S, D_MODEL)
```

```python
import functools
import math

import jax
import jax.numpy as jnp
from jax import lax
from jax.experimental import pallas as pl
from jax.experimental.pallas import tpu as pltpu

F32 = jnp.float32
BF16 = jnp.bfloat16
U32 = jnp.uint32
I32 = jnp.int32

D_MODEL = 1024
N_MEM = 256
N_HEADS_A = 8
N_KV_A = 2
HEAD_DIM_A = 64
BLOCK = 128
WINDOW = 128
NUM_BUCKETS = 32
MAX_EXACT = NUM_BUCKETS // 2
MAX_DISTANCE = 128
N_HEADS_B = 4
HEAD_DIM_B = 128
CHUNK = 128
CONV_WIDTH = 4
N_HEADS_X = 4
HEAD_DIM_X = D_MODEL // N_HEADS_X
N_GROUPS = 4
EXPERTS_PER_GROUP = 8
N_EXPERTS = N_GROUPS * EXPERTS_PER_GROUP
TOP_K = 2
D_EXPERT = 512
EPS = 1e-6
NEG_INF = -1e30

W_A_Q = N_HEADS_A * HEAD_DIM_A
W_A_KV = N_KV_A * HEAD_DIM_A
W_B = N_HEADS_B * HEAD_DIM_B
C_QA = 0
C_KVA = C_QA + W_A_Q
C_QKB = C_KVA + 2 * W_A_KV
C_VB = C_QKB + 2 * W_B
C_OB = C_VB + W_B
C_GATE = C_OB + W_B
D_IN = C_GATE + 2 * N_HEADS_B

LANES = 128
SUBLANES = 8
GATE_ROWS = 8
HALF = D_MODEL // 2

TM_INPROJ = 512
TM_MID = 512
TM_ROWDMA = 512
TM_EXPERT = 256

VMEM_LIMIT = 48 * 1024 * 1024


def _rms(xf, g):
    return xf * lax.rsqrt(jnp.mean(xf * xf, axis=-1, keepdims=True) + EPS) * g


def _pack_halves(v):
    b = pltpu.bitcast(v.astype(BF16).astype(F32), U32)
    return (b[:, :HALF] >> 16) | b[:, HALF:]


def _unpack_halves(p):
    lo = pltpu.bitcast(p << 16, F32)
    hi = pltpu.bitcast(p & jnp.uint32(0xFFFF0000), F32)
    return jnp.concatenate([lo, hi], axis=-1)


def _log_sigmoid(z):
    return jnp.minimum(z, 0.0) - jnp.log1p(jnp.exp(-jnp.abs(z)))


def _inproj_kernel(x_ref, g_ref, w_ref, wgt_ref, gbc_ref, gbr_ref,
                   qa_ref, kva_ref, qkb_ref, vb_ref, ob_ref, gc_ref, gr_ref):
    tm = x_ref.shape[0]
    h = _rms(x_ref[...], g_ref[...]).astype(BF16)

    def mm(lo, hi):
        return jnp.dot(h, w_ref[:, lo:hi], preferred_element_type=F32)

    qa_ref[...] = mm(C_QA, C_KVA).astype(BF16)
    kva_ref[...] = mm(C_KVA, C_QKB).astype(BF16)
    qkb_ref[...] = mm(C_QKB, C_VB).astype(BF16)
    vb_ref[...] = mm(C_VB, C_OB).astype(BF16)
    ob_ref[...] = mm(C_OB, C_GATE).astype(BF16)

    H, L = N_HEADS_B, CHUNK
    gcol = mm(C_GATE, C_GATE + LANES) + gbc_ref[...]
    grow = lax.dot_general(wgt_ref[...], h, (((1,), (1,)), ((), ())), preferred_element_type=F32) + gbr_ref[...]
    lane_c = lax.broadcasted_iota(I32, (L, LANES), 1)
    is_f_col = (lane_c >= H) & (lane_c < 2 * H)
    is_f_row = lax.broadcasted_iota(I32, (GATE_ROWS, L), 0) >= H
    ti = lax.broadcasted_iota(I32, (L, L), 0)
    si = lax.broadcasted_iota(I32, (L, L), 1)
    tril = jnp.where(si <= ti, 1.0, 0.0).astype(F32)
    triu = jnp.where(si >= ti, 1.0, 0.0).astype(F32)
    for c in range(tm // L):
        rows = slice(c * L, (c + 1) * L)
        gcol_c = gcol[rows, :]
        fcol = jnp.where(is_f_col, _log_sigmoid(gcol_c), 0.0)
        bcol = jnp.dot(tril, fcol, preferred_element_type=F32, precision=lax.Precision.HIGHEST)
        gc_ref[rows, :] = jnp.where(is_f_col, bcol, gcol_c)
        grow_c = grow[:, rows]
        frow = jnp.where(is_f_row, _log_sigmoid(grow_c), 0.0)
        brow = jnp.dot(frow, triu, preferred_element_type=F32, precision=lax.Precision.HIGHEST)
        gr_ref[:, rows] = jnp.where(is_f_row, brow, grow_c)


def _inproj(x2d, g, w_pad, wg_t, gbias_col, gbias_row, B, S):
    T = x2d.shape[0]
    tm = gbias_row.shape[1]
    tiles_per_seq = S // tm
    row = lambda w: pl.BlockSpec((tm, w), lambda i: (i, 0))
    full = lambda a: pl.BlockSpec(a.shape, lambda i: (0,) * a.ndim)
    return pl.pallas_call(
        _inproj_kernel,
        grid=(T // tm,),
        in_specs=[row(D_MODEL), full(g), full(w_pad), full(wg_t), full(gbias_col), full(gbias_row)],
        out_specs=[row(W_A_Q), row(2 * W_A_KV), row(2 * W_B), row(W_B), row(W_B), row(LANES),
                   pl.BlockSpec((None, GATE_ROWS, tm), lambda i: (i // tiles_per_seq, 0, i % tiles_per_seq))],
        out_shape=[jax.ShapeDtypeStruct((T, W_A_Q), BF16),
                   jax.ShapeDtypeStruct((T, 2 * W_A_KV), BF16),
                   jax.ShapeDtypeStruct((T, 2 * W_B), BF16),
                   jax.ShapeDtypeStruct((T, W_B), BF16),
                   jax.ShapeDtypeStruct((T, W_B), BF16),
                   jax.ShapeDtypeStruct((T, LANES), F32),
                   jax.ShapeDtypeStruct((B, GATE_ROWS, S), F32)],
        compiler_params=pltpu.CompilerParams(dimension_semantics=("parallel",), vmem_limit_bytes=VMEM_LIMIT),
        name="inproj",
    )(x2d, g, w_pad, wg_t, gbias_col, gbias_row)


def _swa_kernel(sink_ref, qa_ref, kvc_ref, kvp_ref, bias_ref, o_ref):
    q = qa_ref[...]
    kvp = kvp_ref[...].astype(F32)
    kvc = kvc_ref[...].astype(F32)
    kband = jnp.concatenate([kvp[:, :W_A_KV], kvc[:, :W_A_KV]], axis=0)
    vband = jnp.concatenate([kvp[:, W_A_KV:], kvc[:, W_A_KV:]], axis=0)
    lane = lax.broadcasted_iota(I32, (2 * BLOCK, LANES), 1)
    lo = lane < HEAD_DIM_A

    def placements(band):
        swapped = pltpu.roll(band, HEAD_DIM_A, axis=1)
        z = jnp.zeros_like(band)
        return {(0, 0): jnp.where(lo, band, z).astype(BF16), (0, 1): jnp.where(lo, z, swapped).astype(BF16),
                (1, 0): jnp.where(lo, swapped, z).astype(BF16), (1, 1): jnp.where(lo, z, band).astype(BF16)}

    kpl = placements(kband)
    vpl = placements(vband)

    scale = HEAD_DIM_A ** -0.5
    group = N_HEADS_A // N_KV_A

    for pair in range(N_HEADS_A // 2):
        qt = q[:, pair * LANES:(pair + 1) * LANES]
        acc = None
        for half in range(2):
            h = 2 * pair + half
            g = h // group
            s = lax.dot_general(qt, kpl[(g, half)], (((1,), (1,)), ((), ())), preferred_element_type=F32)
            s = s * scale + bias_ref[h]
            sink = sink_ref[h]
            m = jnp.maximum(jnp.max(s, axis=-1, keepdims=True), sink)
            p = jnp.exp(s - m)
            denom = jnp.sum(p, axis=-1, keepdims=True) + jnp.exp(sink - m)
            o = jnp.dot(p.astype(BF16), vpl[(g, half)], preferred_element_type=F32) / denom
            acc = o if acc is None else acc + o
        o_ref[:, pair * LANES:(pair + 1) * LANES] = acc.astype(BF16)


def _swa(qa, kva, bias, sinks, B, S):
    nb = S // BLOCK
    T = B * S
    return pl.pallas_call(
        _swa_kernel,
        grid=(B, nb),
        in_specs=[pl.BlockSpec(memory_space=pltpu.SMEM),
                  pl.BlockSpec((BLOCK, W_A_Q), lambda b, n: (b * nb + n, 0)),
                  pl.BlockSpec((BLOCK, 2 * W_A_KV), lambda b, n: (b * nb + n, 0)),
                  pl.BlockSpec((BLOCK, 2 * W_A_KV), lambda b, n: (b * nb + jnp.maximum(n - 1, 0), 0)),
                  pl.BlockSpec((None,) + bias.shape[1:], lambda b, n: (jnp.minimum(n, 1), 0, 0, 0))],
        out_specs=pl.BlockSpec((BLOCK, W_A_Q), lambda b, n: (b * nb + n, 0)),
        out_shape=jax.ShapeDtypeStruct((T, W_A_Q), BF16),
        compiler_params=pltpu.CompilerParams(dimension_semantics=("parallel", "parallel"),
                                             vmem_limit_bytes=VMEM_LIMIT),
        name="swa",
    )(sinks, qa, kva, kva, bias)


CONV_HALO = 16


def _mlstm_kernel(qkc_ref, qkp_ref, vb_ref, ob_ref, gc_ref, gr_ref, cw_ref, cb_ref, nrm_ref,
                  o_ref, state_ref, m_ref):
    c = pl.program_id(0)
    B = qkc_ref.shape[0]
    H, D, L = N_HEADS_B, HEAD_DIM_B, CHUNK

    @pl.when(c == 0)
    def _():
        state_ref[...] = jnp.zeros_like(state_ref)
        m_ref[...] = jnp.zeros_like(m_ref)

    rr = lax.broadcasted_iota(I32, (L, CONV_HALO + L), 0)
    cc = lax.broadcasted_iota(I32, (L, CONV_HALO + L), 1)
    shifts = {delay: jnp.where(cc == rr + (CONV_HALO - delay), 1.0, 0.0).astype(BF16)
              for delay in range(1, CONV_WIDTH)}
    ti = lax.broadcasted_iota(I32, (L, L), 0)
    si = lax.broadcasted_iota(I32, (L, L), 1)
    tri = si <= ti
    ones_blk = jnp.ones((L, D), BF16)

    def conv_silu(b):
        prev = qkp_ref[b]
        prev = jnp.where(c > 0, prev, jnp.zeros_like(prev))
        cur = qkc_ref[b]
        ext = jnp.concatenate([prev, cur], axis=0)
        y = cb_ref[...] + cw_ref[CONV_WIDTH - 1:CONV_WIDTH, :] * cur.astype(F32)
        for delay in range(1, CONV_WIDTH):
            tap = CONV_WIDTH - 1 - delay
            y = y + cw_ref[tap:tap + 1, :] * jnp.dot(shifts[delay], ext, preferred_element_type=F32)
        return y * jax.nn.sigmoid(y)

    states = [[state_ref[b, h] for h in range(H)] for b in range(B)]
    m_alls = [m_ref[b] for b in range(B)]
    new_states, new_m, outs = {}, {}, {}

    for b, h in [(b, h) for b in range(B) for h in range(H)]:
        if h == 0:
            qk = conv_silu(b)
            gcol = gc_ref[b]
            grow = gr_ref[b]
        qh = (qk[:, h * D:(h + 1) * D] * (D ** -0.5)).astype(BF16)
        k_t = qk[:, W_B + h * D:W_B + (h + 1) * D].T
        v1 = jnp.concatenate([vb_ref[b, :, h * D:(h + 1) * D], ones_blk], axis=-1)
        b_r = grow[H + h:H + h + 1, :]
        g_r = grow[h:h + 1, :] - b_r
        b_c = gcol[:, H + h:H + h + 1]
        m_prev = m_alls[b][h:h + 1, 0:1]
        state = states[b][h]

        gmat = jnp.where(tri, g_r, NEG_INF)
        m_c = jnp.maximum(jnp.max(gmat, axis=-1, keepdims=True), m_prev)
        a_inter = jnp.exp(m_prev - m_c)
        sc = jnp.dot(qh, k_t.astype(BF16), preferred_element_type=F32) * jnp.exp(gmat - m_c)
        tot = (jnp.dot(sc.astype(BF16), v1, preferred_element_type=F32)
               + a_inter * jnp.dot(qh, state.astype(BF16), preferred_element_type=F32))
        num = tot[:, :D]
        den = tot[:, D:]
        hh = num / jnp.maximum(jnp.abs(den), jnp.exp(-(b_c + m_c)))

        b_last = b_r[:, L - 1:L]
        m_new = jnp.maximum(b_last + m_prev, b_last + jnp.max(g_r, axis=-1, keepdims=True))
        w_r = jnp.exp(g_r + (b_last - m_new))
        decay = jnp.exp(b_last + m_prev - m_new)
        upd = jnp.dot((k_t * w_r).astype(BF16), v1, preferred_element_type=F32)
        new_states[b, h] = decay * state + upd
        new_m[b, h] = jnp.broadcast_to(m_new, (1, LANES))

        og = jax.nn.sigmoid(ob_ref[b, :, h * D:(h + 1) * D].astype(F32))
        hb = og * hh
        hb = hb * lax.rsqrt(jnp.mean(hb * hb, axis=-1, keepdims=True) + EPS)
        outs[b, h] = (hb * nrm_ref[:, h * D:(h + 1) * D]).astype(BF16)

    for b, h in new_states:
        state_ref[b, h] = new_states[b, h]
        m_ref[b, h:h + 1, :] = new_m[b, h]
        o_ref[b, :, h * D:(h + 1) * D] = outs[b, h]


def _mlstm(qkb, vb, ob, gc, gr, conv_w, conv_b, nrm):
    B, S, _ = qkb.shape
    nc = S // CHUNK
    halo_per_chunk = CHUNK // CONV_HALO
    blk = lambda w: pl.BlockSpec((B, CHUNK, w), lambda c: (0, c, 0))
    full = lambda a: pl.BlockSpec(a.shape, lambda c: (0,) * a.ndim)
    return pl.pallas_call(
        _mlstm_kernel,
        grid=(nc,),
        in_specs=[blk(2 * W_B),
                  pl.BlockSpec((B, CONV_HALO, 2 * W_B), lambda c: (0, jnp.maximum(c * halo_per_chunk - 1, 0), 0)),
                  blk(W_B), blk(W_B), blk(LANES),
                  pl.BlockSpec((B, GATE_ROWS, CHUNK), lambda c: (0, 0, c)),
                  full(conv_w), full(conv_b), full(nrm)],
        out_specs=blk(W_B),
        out_shape=jax.ShapeDtypeStruct((B, S, W_B), BF16),
        scratch_shapes=[pltpu.VMEM((B, N_HEADS_B, HEAD_DIM_B, 2 * HEAD_DIM_B), F32),
                        pltpu.VMEM((B, GATE_ROWS, LANES), F32)],
        compiler_params=pltpu.CompilerParams(dimension_semantics=("arbitrary",), vmem_limit_bytes=VMEM_LIMIT),
        name="mlstm",
    )(qkb, qkb, vb, ob, gc, gr, conv_w, conv_b, nrm)


def _memkv_kernel(mem_ref, g_ref, wk_ref, wv_ref, k_ref, v_ref):
    hm = _rms(mem_ref[...], g_ref[...]).astype(BF16)
    k_ref[...] = jnp.dot(hm, wk_ref[...], preferred_element_type=F32).astype(BF16)
    v_ref[...] = jnp.dot(hm, wv_ref[...], preferred_element_type=F32).astype(BF16)


def _memkv(mem2d, g, wk, wv, B):
    full = lambda a: pl.BlockSpec(a.shape, lambda b: (0,) * a.ndim)
    blk = pl.BlockSpec((N_MEM, D_MODEL), lambda b: (b, 0))
    return pl.pallas_call(
        _memkv_kernel,
        grid=(B,),
        in_specs=[blk, full(g), full(wk), full(wv)],
        out_specs=[blk, blk],
        out_shape=[jax.ShapeDtypeStruct((B * N_MEM, D_MODEL), BF16)] * 2,
        compiler_params=pltpu.CompilerParams(dimension_semantics=("parallel",), vmem_limit_bytes=VMEM_LIMIT),
        name="memkv",
    )(mem2d, g, wk, wv)


ROUTE_E1, ROUTE_E2, ROUTE_G1, ROUTE_G2, ROUTE_R1, ROUTE_R2 = 0, 1, 2, 3, 4, 5
ROUTER_GROUP_COL = N_EXPERTS


def _mid_kernel(x_ref, oa_ref, hb_ref, wo_ref, gx_ref, wq_ref, ck_ref, cv_ref, wco_ref, gz_ref, wr_ref, br_ref,
                x2_ref, hz_ref, route_ref, counts_ref, cnt_ref):
    @pl.when(pl.program_id(0) == 0)
    def _():
        cnt_ref[...] = jnp.zeros_like(cnt_ref)

    x1 = (x_ref[...]
          + jnp.dot(oa_ref[...], wo_ref[0:W_A_Q, :], preferred_element_type=F32)
          + jnp.dot(hb_ref[...], wo_ref[W_A_Q:, :], preferred_element_type=F32))

    hc = _rms(x1, gx_ref[...]).astype(BF16)
    cq = jnp.dot(hc, wq_ref[...], preferred_element_type=F32).astype(BF16)
    scale = HEAD_DIM_X ** -0.5
    heads = []
    for h in range(N_HEADS_X):
        sl = slice(h * HEAD_DIM_X, (h + 1) * HEAD_DIM_X)
        s = lax.dot_general(cq[:, sl], ck_ref[:, sl], (((1,), (1,)), ((), ())), preferred_element_type=F32) * scale
        p = jnp.exp(s - jnp.max(s, axis=-1, keepdims=True))
        co = jnp.dot(p.astype(BF16), cv_ref[:, sl], preferred_element_type=F32) / jnp.sum(p, axis=-1, keepdims=True)
        heads.append(co.astype(BF16))
    x2 = x1 + jnp.dot(jnp.concatenate(heads, axis=-1), wco_ref[...], preferred_element_type=F32)
    x2_ref[...] = x2

    hz = _rms(x2, gz_ref[...])
    hz_ref[...] = _pack_halves(hz)
    lg = jnp.dot(hz.astype(BF16), wr_ref[...], preferred_element_type=F32) + br_ref[...]
    col = lax.broadcasted_iota(I32, lg.shape, 1)
    big = jnp.int32(LANES)
    is_g = (col >= ROUTER_GROUP_COL) & (col < ROUTER_GROUP_COL + N_GROUPS)
    gl = jnp.where(is_g, lg, NEG_INF)
    gmax = jnp.max(gl, axis=-1, keepdims=True)
    gsum = jnp.sum(jnp.exp(gl - gmax), axis=-1, keepdims=True)
    g_prob = 1.0 / gsum
    g_idx = jnp.min(jnp.where(gl == gmax, col - ROUTER_GROUP_COL, big), axis=-1, keepdims=True)
    sel = (col < N_EXPERTS) & ((col // EXPERTS_PER_GROUP) == g_idx)
    el = jnp.where(sel, lg, NEG_INF)
    m1 = jnp.max(el, axis=-1, keepdims=True)
    i1 = jnp.min(jnp.where(el == m1, col, big), axis=-1, keepdims=True)
    el2 = jnp.where(col == i1, NEG_INF, el)
    m2 = jnp.max(el2, axis=-1, keepdims=True)
    i2 = jnp.min(jnp.where(el2 == m2, col, big), axis=-1, keepdims=True)
    z = jnp.sum(jnp.exp(el - m1), axis=-1, keepdims=True)
    p1 = 1.0 / z
    p2 = jnp.exp(m2 - m1) / z
    g1 = g_prob * (p1 / (p1 + p2))
    g2 = g_prob * (p2 / (p1 + p2))

    tm = lg.shape[0]
    used = jnp.where((col == i1) | (col == i2), 1.0, 0.0)
    tt = lax.broadcasted_iota(I32, (tm, tm), 0)
    ss = lax.broadcasted_iota(I32, (tm, tm), 1)
    earlier = jnp.where(ss < tt, 1.0, 0.0).astype(BF16)
    before = jnp.dot(earlier, used.astype(BF16), preferred_element_type=F32) + cnt_ref[...]
    r1 = jnp.sum(jnp.where(col == i1, before, 0.0), axis=-1, keepdims=True)
    r2 = jnp.sum(jnp.where(col == i2, before, 0.0), axis=-1, keepdims=True)
    cnt_ref[...] = cnt_ref[...] + jnp.sum(used, axis=0, keepdims=True)
    counts_ref[...] = jnp.broadcast_to(cnt_ref[...], counts_ref.shape)

    route = jnp.zeros_like(lg)
    for c, v in ((ROUTE_E1, i1.astype(F32)), (ROUTE_E2, i2.astype(F32)), (ROUTE_G1, g1), (ROUTE_G2, g2),
                 (ROUTE_R1, r1), (ROUTE_R2, r2)):
        route = jnp.where(col == c, v, route)
    route_ref[...] = route


def _mid(x2d, oa, hb, wo, gx, wq, ck, cv, wco, gz, wr, br, B, S):
    T = B * S
    tm = min(TM_MID, S)
    per_b = S // tm
    row = lambda w: pl.BlockSpec((tm, w), lambda i: (i, 0))
    full = lambda a: pl.BlockSpec(a.shape, lambda i: (0,) * a.ndim)
    kvspec = pl.BlockSpec((N_MEM, D_MODEL), lambda i: (i // per_b, 0))
    return pl.pallas_call(
        _mid_kernel,
        grid=(T // tm,),
        in_specs=[row(D_MODEL), row(W_A_Q), row(W_B), full(wo), full(gx), full(wq), kvspec, kvspec,
                  full(wco), full(gz), full(wr), full(br)],
        out_specs=[row(D_MODEL), row(HALF), row(LANES), pl.BlockSpec((SUBLANES, LANES), lambda i: (0, 0))],
        out_shape=[jax.ShapeDtypeStruct((T, D_MODEL), F32),
                   jax.ShapeDtypeStruct((T, HALF), U32),
                   jax.ShapeDtypeStruct((T, LANES), F32),
                   jax.ShapeDtypeStruct((SUBLANES, LANES), F32)],
        scratch_shapes=[pltpu.VMEM((1, LANES), F32)],
        compiler_params=pltpu.CompilerParams(dimension_semantics=("arbitrary",), vmem_limit_bytes=VMEM_LIMIT),
        name="mid",
    )(x2d, oa, hb, wo, gx, wq, ck, cv, wco, gz, wr, br)


def _dispatch_kernel(zf_ref, pos_ref, hz_ref, xs_hbm, zbuf, sem, zsem):
    i = pl.program_id(0)
    tm = hz_ref.shape[0]
    tmx = zbuf.shape[0]

    @pl.when(i == 0)
    def _():
        zbuf[...] = jnp.zeros_like(zbuf)

        def fill(t):
            return pltpu.make_async_copy(zbuf, xs_hbm.at[pl.ds(t * tmx, tmx)], zsem)

        def start(t, carry):
            @pl.when(zf_ref[t] != 0)
            def _():
                fill(t).start()
            return carry

        def wait(t, carry):
            @pl.when(zf_ref[t] != 0)
            def _():
                fill(t).wait()
            return carry

        lax.fori_loop(0, zf_ref.shape[0], start, 0)
        lax.fori_loop(0, zf_ref.shape[0], wait, 0)

    for r in range(tm):
        for k in range(TOP_K):
            pltpu.make_async_copy(hz_ref.at[pl.ds(r, 1)], xs_hbm.at[pl.ds(pos_ref[0, 0, TOP_K * r + k], 1)],
                                  sem).start(priority=k % 2)
    for k in range(TOP_K):
        pltpu.make_async_copy(hz_ref, xs_hbm.at[pl.ds(0, tm)], sem).wait()


def _dispatch(hz_packed, pos, zfill, n_slots, tmx):
    T = hz_packed.shape[0]
    tm = pos.shape[2] // TOP_K
    grid_spec = pltpu.PrefetchScalarGridSpec(
        num_scalar_prefetch=1,
        grid=(T // tm,),
        in_specs=[pl.BlockSpec((1, 1, TOP_K * tm), lambda i, zf: (i, 0, 0), memory_space=pltpu.SMEM),
                  pl.BlockSpec((tm, HALF), lambda i, zf: (i, 0))],
        out_specs=pl.BlockSpec(memory_space=pl.ANY),
        scratch_shapes=[pltpu.VMEM((tmx, HALF), U32), pltpu.SemaphoreType.DMA(()), pltpu.SemaphoreType.DMA(())],
    )
    return pl.pallas_call(
        _dispatch_kernel,
        grid_spec=grid_spec,
        out_shape=jax.ShapeDtypeStruct((n_slots, HALF), U32),
        compiler_params=pltpu.CompilerParams(dimension_semantics=("arbitrary",), vmem_limit_bytes=VMEM_LIMIT),
        name="dispatch",
    )(zfill, pos, hz_packed)


def _expert_kernel(te_ref, nt_ref, xs_ref, wg_ref, wu_ref, wd_ref, ys_ref, wgb, wub, wdb):
    i = pl.program_id(0)
    nt = nt_ref[0]

    @pl.when(i < nt)
    def _():
        @pl.when((i == 0) | (te_ref[i] != te_ref[jnp.maximum(i - 1, 0)]))
        def _():
            wgb[...] = wg_ref[0].astype(BF16)
            wub[...] = wu_ref[0].astype(BF16)
            wdb[...] = wd_ref[0].astype(BF16)

        x = _unpack_halves(xs_ref[...]).astype(BF16)
        hg = jnp.dot(x, wgb[...], preferred_element_type=F32)
        hu = jnp.dot(x, wub[...], preferred_element_type=F32)
        a = (hg * jax.nn.sigmoid(hg) * hu).astype(BF16)
        ys_ref[...] = _pack_halves(jnp.dot(a, wdb[...], preferred_element_type=F32))

    @pl.when(i >= nt)
    def _():
        ys_ref[...] = jnp.zeros_like(ys_ref)


def _experts(xs, w_gate, w_up, w_down, tile_expert, ntiles, tmx):
    n_tiles_max = tile_expert.shape[0]
    wspec = lambda a: pl.BlockSpec((1,) + a.shape[1:], lambda i, te, nt: (te[i], 0, 0))
    grid_spec = pltpu.PrefetchScalarGridSpec(
        num_scalar_prefetch=2,
        grid=(n_tiles_max,),
        in_specs=[pl.BlockSpec((tmx, HALF), lambda i, te, nt: (jnp.minimum(i, nt[0] - 1), 0)),
                  wspec(w_gate), wspec(w_up), wspec(w_down)],
        out_specs=pl.BlockSpec((tmx, HALF), lambda i, te, nt: (i, 0)),
        scratch_shapes=[pltpu.VMEM((D_MODEL, D_EXPERT), BF16),
                        pltpu.VMEM((D_MODEL, D_EXPERT), BF16),
                        pltpu.VMEM((D_EXPERT, D_MODEL), BF16)],
    )
    return pl.pallas_call(
        _expert_kernel,
        grid_spec=grid_spec,
        out_shape=jax.ShapeDtypeStruct(xs.shape, U32),
        compiler_params=pltpu.CompilerParams(dimension_semantics=("arbitrary",), vmem_limit_bytes=VMEM_LIMIT),
        name="experts",
    )(tile_expert, ntiles, xs, w_gate, w_up, w_down)


def _final_kernel(posc_ref, posn_ref, x2_ref, route_ref, g_ref, ys_hbm, o_ref, ybuf, sem):
    i = pl.program_id(0)
    n = pl.num_programs(0)
    tm = x2_ref.shape[0]
    slot = i % 2

    def issue(pos_ref, s):
        for r in range(tm):
            for k in range(TOP_K):
                pltpu.make_async_copy(ys_hbm.at[pl.ds(pos_ref[0, 0, TOP_K * r + k], 1)],
                                      ybuf.at[s, k, pl.ds(r, 1)], sem.at[s]).start(priority=k % 2)

    @pl.when(i == 0)
    def _():
        issue(posc_ref, 0)

    @pl.when(i + 1 < n)
    def _():
        issue(posn_ref, 1 - slot)

    for k in range(TOP_K):
        pltpu.make_async_copy(ys_hbm.at[pl.ds(0, tm)], ybuf.at[slot, k], sem.at[slot]).wait()

    r = route_ref[...]
    g1 = r[:, ROUTE_G1:ROUTE_G1 + 1]
    g2 = r[:, ROUTE_G2:ROUTE_G2 + 1]
    xo = x2_ref[...] + g1 * _unpack_halves(ybuf[slot, 0]) + g2 * _unpack_halves(ybuf[slot, 1])
    o_ref[...] = _rms(xo, g_ref[...])


def _final(x2, ys, pos, route, g):
    T = x2.shape[0]
    nblk = pos.shape[0]
    tm = T // nblk
    row = lambda w: pl.BlockSpec((tm, w), lambda i: (i, 0))
    return pl.pallas_call(
        _final_kernel,
        grid=(nblk,),
        in_specs=[pl.BlockSpec((1, 1, TOP_K * tm), lambda i: (i, 0, 0), memory_space=pltpu.SMEM),
                  pl.BlockSpec((1, 1, TOP_K * tm), lambda i: (jnp.minimum(i + 1, nblk - 1), 0, 0),
                               memory_space=pltpu.SMEM),
                  row(D_MODEL), row(LANES), pl.BlockSpec(g.shape, lambda i: (0, 0)),
                  pl.BlockSpec(memory_space=pl.ANY)],
        out_specs=row(D_MODEL),
        out_shape=jax.ShapeDtypeStruct((T, D_MODEL), F32),
        scratch_shapes=[pltpu.VMEM((2, TOP_K, tm, HALF), U32), pltpu.SemaphoreType.DMA((2,))],
        compiler_params=pltpu.CompilerParams(dimension_semantics=("arbitrary",), vmem_limit_bytes=VMEM_LIMIT),
        name="final",
    )(pos, pos, x2, route, g, ys)


def _band_bias(table):
    i = jnp.arange(BLOCK)[:, None]
    j = jnp.arange(2 * BLOCK)[None, :]
    n = jnp.maximum(i + BLOCK - j, 0)
    nf = jnp.maximum(n, 1).astype(F32)
    large = MAX_EXACT + (jnp.log(nf / MAX_EXACT) / math.log(MAX_DISTANCE / MAX_EXACT)
                         * (NUM_BUCKETS - MAX_EXACT)).astype(I32)
    large = jnp.minimum(large, NUM_BUCKETS - 1)
    bucket = jnp.where(n < MAX_EXACT, n, large)
    onehot = (bucket[:, :, None] == jnp.arange(NUM_BUCKETS)[None, None, :]).astype(F32)
    bias = jnp.einsum("ijb,bh->hij", onehot, table.astype(F32), precision=lax.Precision.HIGHEST)
    d = i + BLOCK - j
    band_ok = (d >= 0) & (d < WINDOW)
    first_ok = band_ok & (j >= BLOCK)
    return jnp.stack([jnp.where(first_ok[None], bias, NEG_INF), jnp.where(band_ok[None], bias, NEG_INF)])


def _dispatch_plan(route, counts_f, tmx, n_tiles_max, tm_rows):
    T = route.shape[0]
    experts = jnp.arange(N_EXPERTS, dtype=I32)
    counts = counts_f[0, :N_EXPERTS].astype(I32)
    ptiles = (counts + tmx - 1) // tmx
    tile_end = jnp.cumsum(ptiles)
    nt = tile_end[-1]
    row_off = (tile_end - ptiles) * tmx

    def slot(e_col, r_col):
        e = route[:, e_col].astype(I32)
        off = jnp.sum(jnp.where(e[:, None] == experts[None, :], row_off[None, :], 0), axis=1)
        return off + route[:, r_col].astype(I32)

    pos = jnp.stack([slot(ROUTE_E1, ROUTE_R1), slot(ROUTE_E2, ROUTE_R2)], axis=1)
    pos = pos.reshape(T // tm_rows, 1, tm_rows * TOP_K)

    tile_ids = jnp.arange(n_tiles_max, dtype=I32)
    expert_of = lambda t: jnp.sum((tile_end[None, :] <= t[:, None]).astype(I32), axis=1)
    te = expert_of(jnp.minimum(tile_ids, nt - 1))
    partial = jnp.any((tile_ids[:, None] == (tile_end - 1)[None, :]) & (counts % tmx != 0)[None, :], axis=1)
    zfill = (partial | (tile_ids >= nt)).astype(I32)
    return pos, te, nt.reshape(1), zfill


def kernel(x, mem, rel_bias_table, norm_mix, w_in, attn_sinks, conv_w, conv_b, gate_bias_i, gate_bias_f, mlstm_norm, w_out, norm_cross, norm_mem, w_cq, w_ck, w_cv, w_co, norm_moe, w_router_group, b_router_group, w_router_expert, b_router_expert, w_exp_gate, w_exp_up, w_exp_down, norm_final):
    B, S, _ = x.shape
    T = B * S
    depth = w_in.shape[0]
    x2d = x.reshape(T, D_MODEL)
    mem2d = mem.reshape(B * N_MEM, D_MODEL)
    bias = _band_bias(rel_bias_table)

    tmx = min(TM_EXPERT, T)
    n_tiles_max = (T * TOP_K) // tmx + N_EXPERTS
    tm_rows = min(TM_ROWDMA, T)

    assert depth == 1, "the final combine is fused with the final norm: single layer only"
    l = 0
    w_pad = jnp.pad(w_in[l], ((0, 0), (0, C_GATE + LANES - D_IN))).astype(BF16)
    wg_t = w_in[l][:, C_GATE:].T.astype(BF16)
    gb = jnp.concatenate([gate_bias_i[l], gate_bias_f[l]]).astype(F32)
    gbias_col = jnp.pad(gb, (0, LANES - GATE_ROWS))[None, :]
    gbias_row = jnp.broadcast_to(gb[:, None], (GATE_ROWS, min(TM_INPROJ, S)))
    qa, kva, qkb, vb, ob, gc, gr = _inproj(x2d, norm_mix[l][None, :], w_pad, wg_t, gbias_col, gbias_row, B, S)

    out_a = _swa(qa, kva, bias, attn_sinks[l].astype(F32), B, S)
    per_seq = lambda a: a.reshape(B, S, a.shape[-1])
    hb = _mlstm(per_seq(qkb), per_seq(vb), per_seq(ob), per_seq(gc), gr, conv_w[l][:, 0, :].astype(F32),
                conv_b[l][None, :].astype(F32), mlstm_norm[l][None, :].astype(F32)).reshape(T, W_B)

    ck, cv = _memkv(mem2d, norm_mem[l][None, :], w_ck[l].astype(BF16), w_cv[l].astype(BF16), B)

    wr = jnp.pad(jnp.concatenate([w_router_expert[l], w_router_group[l]], axis=1),
                 ((0, 0), (0, LANES - N_EXPERTS - N_GROUPS))).astype(BF16)
    br = jnp.pad(jnp.concatenate([b_router_expert[l], b_router_group[l]]),
                 (0, LANES - N_EXPERTS - N_GROUPS)).astype(F32)[None, :]
    x2, hz_packed, route, counts = _mid(x2d, out_a, hb, w_out[l].astype(BF16), norm_cross[l][None, :],
                                        w_cq[l].astype(BF16), ck, cv, w_co[l].astype(BF16), norm_moe[l][None, :],
                                        wr, br, B, S)

    pos, te, nt, zfill = _dispatch_plan(route, counts, tmx, n_tiles_max, tm_rows)
    xs = _dispatch(hz_packed, pos, zfill, n_tiles_max * tmx, tmx)
    ys = _experts(xs, w_exp_gate[l], w_exp_up[l], w_exp_down[l], te, nt, tmx)
    out = _final(x2, ys, pos, route, norm_final[None, :])
    return out.reshape(B, S, D_MODEL)
```

```python
import functools
import math

import jax
import jax.numpy as jnp
from jax import lax
from jax.experimental import pallas as pl
from jax.experimental.pallas import tpu as pltpu

F32 = jnp.float32
BF16 = jnp.bfloat16
U32 = jnp.uint32
I32 = jnp.int32

D_MODEL = 1024
N_MEM = 256
N_HEADS_A = 8
N_KV_A = 2
HEAD_DIM_A = 64
BLOCK = 128
WINDOW = 128
NUM_BUCKETS = 32
MAX_EXACT = NUM_BUCKETS // 2
MAX_DISTANCE = 128
N_HEADS_B = 4
HEAD_DIM_B = 128
CHUNK = 128
CONV_WIDTH = 4
N_HEADS_X = 4
HEAD_DIM_X = D_MODEL // N_HEADS_X
N_GROUPS = 4
EXPERTS_PER_GROUP = 8
N_EXPERTS = N_GROUPS * EXPERTS_PER_GROUP
TOP_K = 2
D_EXPERT = 512
EPS = 1e-6
NEG_INF = -1e30

W_A_Q = N_HEADS_A * HEAD_DIM_A
W_A_KV = N_KV_A * HEAD_DIM_A
W_B = N_HEADS_B * HEAD_DIM_B
C_QA = 0
C_KVA = C_QA + W_A_Q
C_QKB = C_KVA + 2 * W_A_KV
C_VB = C_QKB + 2 * W_B
C_OB = C_VB + W_B
C_GATE = C_OB + W_B
D_IN = C_GATE + 2 * N_HEADS_B

LANES = 128
SUBLANES = 8
GATE_ROWS = 8
HALF = D_MODEL // 2

TM_INPROJ = 512
TM_MID = 512
TM_ROWDMA = 512
TM_EXPERT = 256

VMEM_LIMIT = 48 * 1024 * 1024


def _rms(xf, g):
    return xf * lax.rsqrt(jnp.mean(xf * xf, axis=-1, keepdims=True) + EPS) * g


def _pack_halves(v):
    b = pltpu.bitcast(v.astype(BF16).astype(F32), U32)
    return (b[:, :HALF] >> 16) | b[:, HALF:]


def _unpack_halves(p):
    lo = pltpu.bitcast(p << 16, F32)
    hi = pltpu.bitcast(p & jnp.uint32(0xFFFF0000), F32)
    return jnp.concatenate([lo, hi], axis=-1)


def _log_sigmoid(z):
    return jnp.minimum(z, 0.0) - jnp.log1p(jnp.exp(-jnp.abs(z)))


def _inproj_kernel(x_ref, g_ref, w_ref, wgt_ref, gbc_ref, gbr_ref,
                   qa_ref, kva_ref, qkb_ref, vb_ref, ob_ref, gc_ref, gr_ref):
    tm = x_ref.shape[0]
    h = _rms(x_ref[...], g_ref[...]).astype(BF16)

    def mm(lo, hi):
        return jnp.dot(h, w_ref[:, lo:hi], preferred_element_type=F32)

    qa_ref[...] = mm(C_QA, C_KVA).astype(BF16)
    kva_ref[...] = mm(C_KVA, C_QKB).astype(BF16)
    qkb_ref[...] = mm(C_QKB, C_VB).astype(BF16)
    vb_ref[...] = mm(C_VB, C_OB).astype(BF16)
    ob_ref[...] = mm(C_OB, C_GATE).astype(BF16)

    H, L = N_HEADS_B, CHUNK
    gcol = mm(C_GATE, C_GATE + LANES) + gbc_ref[...]
    grow = lax.dot_general(wgt_ref[...], h, (((1,), (1,)), ((), ())), preferred_element_type=F32) + gbr_ref[...]
    lane_c = lax.broadcasted_iota(I32, (L, LANES), 1)
    is_f_col = (lane_c >= H) & (lane_c < 2 * H)
    is_f_row = lax.broadcasted_iota(I32, (GATE_ROWS, L), 0) >= H
    ti = lax.broadcasted_iota(I32, (L, L), 0)
    si = lax.broadcasted_iota(I32, (L, L), 1)
    tril = jnp.where(si <= ti, 1.0, 0.0).astype(F32)
    triu = jnp.where(si >= ti, 1.0, 0.0).astype(F32)
    for c in range(tm // L):
        rows = slice(c * L, (c + 1) * L)
        gcol_c = gcol[rows, :]
        fcol = jnp.where(is_f_col, _log_sigmoid(gcol_c), 0.0)
        bcol = jnp.dot(tril, fcol, preferred_element_type=F32, precision=lax.Precision.HIGHEST)
        gc_ref[rows, :] = jnp.where(is_f_col, bcol, gcol_c)
        grow_c = grow[:, rows]
        frow = jnp.where(is_f_row, _log_sigmoid(grow_c), 0.0)
        brow = jnp.dot(frow, triu, preferred_element_type=F32, precision=lax.Precision.HIGHEST)
        gr_ref[:, rows] = jnp.where(is_f_row, brow, grow_c)


def _inproj(x2d, g, w_pad, wg_t, gbias_col, gbias_row, B, S):
    T = x2d.shape[0]
    tm = gbias_row.shape[1]
    tiles_per_seq = S // tm
    row = lambda w: pl.BlockSpec((tm, w), lambda i: (i, 0))
    full = lambda a: pl.BlockSpec(a.shape, lambda i: (0,) * a.ndim)
    return pl.pallas_call(
        _inproj_kernel,
        grid=(T // tm,),
        in_specs=[row(D_MODEL), full(g), full(w_pad), full(wg_t), full(gbias_col), full(gbias_row)],
        out_specs=[row(W_A_Q), row(2 * W_A_KV), row(2 * W_B), row(W_B), row(W_B), row(LANES),
                   pl.BlockSpec((None, GATE_ROWS, tm), lambda i: (i // tiles_per_seq, 0, i % tiles_per_seq))],
        out_shape=[jax.ShapeDtypeStruct((T, W_A_Q), BF16),
                   jax.ShapeDtypeStruct((T, 2 * W_A_KV), BF16),
                   jax.ShapeDtypeStruct((T, 2 * W_B), BF16),
                   jax.ShapeDtypeStruct((T, W_B), BF16),
                   jax.ShapeDtypeStruct((T, W_B), BF16),
                   jax.ShapeDtypeStruct((T, LANES), F32),
                   jax.ShapeDtypeStruct((B, GATE_ROWS, S), F32)],
        compiler_params=pltpu.CompilerParams(dimension_semantics=("parallel",), vmem_limit_bytes=VMEM_LIMIT),
        name="inproj",
    )(x2d, g, w_pad, wg_t, gbias_col, gbias_row)


def _swa_kernel(sink_ref, qa_ref, kvc_ref, kvp_ref, bias_ref, o_ref):
    q = qa_ref[...]
    kvp = kvp_ref[...].astype(F32)
    kvc = kvc_ref[...].astype(F32)
    kband = jnp.concatenate([kvp[:, :W_A_KV], kvc[:, :W_A_KV]], axis=0)
    vband = jnp.concatenate([kvp[:, W_A_KV:], kvc[:, W_A_KV:]], axis=0)
    lane = lax.broadcasted_iota(I32, (2 * BLOCK, LANES), 1)
    lo = lane < HEAD_DIM_A

    def placements(band):
        swapped = pltpu.roll(band, HEAD_DIM_A, axis=1)
        z = jnp.zeros_like(band)
        return {(0, 0): jnp.where(lo, band, z).astype(BF16), (0, 1): jnp.where(lo, z, swapped).astype(BF16),
                (1, 0): jnp.where(lo, swapped, z).astype(BF16), (1, 1): jnp.where(lo, z, band).astype(BF16)}

    kpl = placements(kband)
    vpl = placements(vband)

    scale = HEAD_DIM_A ** -0.5
    group = N_HEADS_A // N_KV_A

    for pair in range(N_HEADS_A // 2):
        qt = q[:, pair * LANES:(pair + 1) * LANES]
        acc = None
        for half in range(2):
            h = 2 * pair + half
            g = h // group
            s = lax.dot_general(qt, kpl[(g, half)], (((1,), (1,)), ((), ())), preferred_element_type=F32)
            s = s * scale + bias_ref[h]
            sink = sink_ref[h]
            m = jnp.maximum(jnp.max(s, axis=-1, keepdims=True), sink)
            p = jnp.exp(s - m)
            denom = jnp.sum(p, axis=-1, keepdims=True) + jnp.exp(sink - m)
            o = jnp.dot(p.astype(BF16), vpl[(g, half)], preferred_element_type=F32) / denom
            acc = o if acc is None else acc + o
        o_ref[:, pair * LANES:(pair + 1) * LANES] = acc.astype(BF16)


def _swa(qa, kva, bias, sinks, B, S):
    nb = S // BLOCK
    T = B * S
    return pl.pallas_call(
        _swa_kernel,
        grid=(B, nb),
        in_specs=[pl.BlockSpec(memory_space=pltpu.SMEM),
                  pl.BlockSpec((BLOCK, W_A_Q), lambda b, n: (b * nb + n, 0)),
                  pl.BlockSpec((BLOCK, 2 * W_A_KV), lambda b, n: (b * nb + n, 0)),
                  pl.BlockSpec((BLOCK, 2 * W_A_KV), lambda b, n: (b * nb + jnp.maximum(n - 1, 0), 0)),
                  pl.BlockSpec((None,) + bias.shape[1:], lambda b, n: (jnp.minimum(n, 1), 0, 0, 0))],
        out_specs=pl.BlockSpec((BLOCK, W_A_Q), lambda b, n: (b * nb + n, 0)),
        out_shape=jax.ShapeDtypeStruct((T, W_A_Q), BF16),
        compiler_params=pltpu.CompilerParams(dimension_semantics=("parallel", "parallel"),
                                             vmem_limit_bytes=VMEM_LIMIT),
        name="swa",
    )(sinks, qa, kva, kva, bias)


CONV_HALO = 16


def _mlstm_kernel(qkc_ref, qkp_ref, vb_ref, ob_ref, gc_ref, gr_ref, cw_ref, cb_ref, nrm_ref,
                  o_ref, state_ref, m_ref):
    c = pl.program_id(0)
    B = qkc_ref.shape[0]
    H, D, L = N_HEADS_B, HEAD_DIM_B, CHUNK

    @pl.when(c == 0)
    def _():
        state_ref[...] = jnp.zeros_like(state_ref)
        m_ref[...] = jnp.zeros_like(m_ref)

    rr = lax.broadcasted_iota(I32, (L, CONV_HALO + L), 0)
    cc = lax.broadcasted_iota(I32, (L, CONV_HALO + L), 1)
    shifts = {delay: jnp.where(cc == rr + (CONV_HALO - delay), 1.0, 0.0).astype(BF16)
              for delay in range(1, CONV_WIDTH)}
    ti = lax.broadcasted_iota(I32, (L, L), 0)
    si = lax.broadcasted_iota(I32, (L, L), 1)
    tri = si <= ti
    ones_blk = jnp.ones((L, D), BF16)

    def conv_silu(b):
        prev = qkp_ref[b]
        prev = jnp.where(c > 0, prev, jnp.zeros_like(prev))
        cur = qkc_ref[b]
        ext = jnp.concatenate([prev, cur], axis=0)
        y = cb_ref[...] + cw_ref[CONV_WIDTH - 1:CONV_WIDTH, :] * cur.astype(F32)
        for delay in range(1, CONV_WIDTH):
            tap = CONV_WIDTH - 1 - delay
            y = y + cw_ref[tap:tap + 1, :] * jnp.dot(shifts[delay], ext, preferred_element_type=F32)
        return y * jax.nn.sigmoid(y)

    states = [[state_ref[b, h] for h in range(H)] for b in range(B)]
    m_alls = [m_ref[b] for b in range(B)]
    new_states, new_m, outs = {}, {}, {}

    for b, h in [(b, h) for b in range(B) for h in range(H)]:
        if h == 0:
            qk = conv_silu(b)
            gcol = gc_ref[b]
            grow = gr_ref[b]
        qh = (qk[:, h * D:(h + 1) * D] * (D ** -0.5)).astype(BF16)
        k_t = qk[:, W_B + h * D:W_B + (h + 1) * D].T
        v1 = jnp.concatenate([vb_ref[b, :, h * D:(h + 1) * D], ones_blk], axis=-1)
        b_r = grow[H + h:H + h + 1, :]
        g_r = grow[h:h + 1, :] - b_r
        b_c = gcol[:, H + h:H + h + 1]
        m_prev = m_alls[b][h:h + 1, 0:1]
        state = states[b][h]

        gmat = jnp.where(tri, g_r, NEG_INF)
        m_c = jnp.maximum(jnp.max(gmat, axis=-1, keepdims=True), m_prev)
        a_inter = jnp.exp(m_prev - m_c)
        sc = jnp.dot(qh, k_t.astype(BF16), preferred_element_type=F32) * jnp.exp(gmat - m_c)
        tot = (jnp.dot(sc.astype(BF16), v1, preferred_element_type=F32)
               + a_inter * jnp.dot(qh, state.astype(BF16), preferred_element_type=F32))
        num = tot[:, :D]
        den = tot[:, D:]
        hh = num / jnp.maximum(jnp.abs(den), jnp.exp(-(b_c + m_c)))

        b_last = b_r[:, L - 1:L]
        m_new = jnp.maximum(b_last + m_prev, b_last + jnp.max(g_r, axis=-1, keepdims=True))
        w_r = jnp.exp(g_r + (b_last - m_new))
        decay = jnp.exp(b_last + m_prev - m_new)
        upd = jnp.dot((k_t * w_r).astype(BF16), v1, preferred_element_type=F32)
        new_states[b, h] = decay * state + upd
        new_m[b, h] = jnp.broadcast_to(m_new, (1, LANES))

        og = jax.nn.sigmoid(ob_ref[b, :, h * D:(h + 1) * D].astype(F32))
        hb = og * hh
        hb = hb * lax.rsqrt(jnp.mean(hb * hb, axis=-1, keepdims=True) + EPS)
        outs[b, h] = (hb * nrm_ref[:, h * D:(h + 1) * D]).astype(BF16)

    for b, h in new_states:
        state_ref[b, h] = new_states[b, h]
        m_ref[b, h:h + 1, :] = new_m[b, h]
        o_ref[b, :, h * D:(h + 1) * D] = outs[b, h]


def _mlstm(qkb, vb, ob, gc, gr, conv_w, conv_b, nrm):
    B, S, _ = qkb.shape
    nc = S // CHUNK
    halo_per_chunk = CHUNK // CONV_HALO
    blk = lambda w: pl.BlockSpec((B, CHUNK, w), lambda c: (0, c, 0))
    full = lambda a: pl.BlockSpec(a.shape, lambda c: (0,) * a.ndim)
    return pl.pallas_call(
        _mlstm_kernel,
        grid=(nc,),
        in_specs=[blk(2 * W_B),
                  pl.BlockSpec((B, CONV_HALO, 2 * W_B), lambda c: (0, jnp.maximum(c * halo_per_chunk - 1, 0), 0)),
                  blk(W_B), blk(W_B), blk(LANES),
                  pl.BlockSpec((B, GATE_ROWS, CHUNK), lambda c: (0, 0, c)),
                  full(conv_w), full(conv_b), full(nrm)],
        out_specs=blk(W_B),
        out_shape=jax.ShapeDtypeStruct((B, S, W_B), BF16),
        scratch_shapes=[pltpu.VMEM((B, N_HEADS_B, HEAD_DIM_B, 2 * HEAD_DIM_B), F32),
                        pltpu.VMEM((B, GATE_ROWS, LANES), F32)],
        compiler_params=pltpu.CompilerParams(dimension_semantics=("arbitrary",), vmem_limit_bytes=VMEM_LIMIT),
        name="mlstm",
    )(qkb, qkb, vb, ob, gc, gr, conv_w, conv_b, nrm)


def _memkv_kernel(mem_ref, g_ref, wk_ref, wv_ref, k_ref, v_ref):
    hm = _rms(mem_ref[...], g_ref[...]).astype(BF16)
    k_ref[...] = jnp.dot(hm, wk_ref[...], preferred_element_type=F32).astype(BF16)
    v_ref[...] = jnp.dot(hm, wv_ref[...], preferred_element_type=F32).astype(BF16)


def _memkv(mem2d, g, wk, wv, B):
    full = lambda a: pl.BlockSpec(a.shape, lambda b: (0,) * a.ndim)
    blk = pl.BlockSpec((N_MEM, D_MODEL), lambda b: (b, 0))
    return pl.pallas_call(
        _memkv_kernel,
        grid=(B,),
        in_specs=[blk, full(g), full(wk), full(wv)],
        out_specs=[blk, blk],
        out_shape=[jax.ShapeDtypeStruct((B * N_MEM, D_MODEL), BF16)] * 2,
        compiler_params=pltpu.CompilerParams(dimension_semantics=("parallel",), vmem_limit_bytes=VMEM_LIMIT),
        name="memkv",
    )(mem2d, g, wk, wv)


ROUTE_E1, ROUTE_E2, ROUTE_G1, ROUTE_G2, ROUTE_R1, ROUTE_R2 = 0, 1, 2, 3, 4, 5
ROUTER_GROUP_COL = N_EXPERTS


def _mid_kernel(x_ref, oa_ref, hb_ref, wo_ref, gx_ref, wq_ref, ck_ref, cv_ref, wco_ref, gz_ref, wr_ref, br_ref,
                x2_ref, hz_ref, route_ref, counts_ref, cnt_ref):
    @pl.when(pl.program_id(0) == 0)
    def _():
        cnt_ref[...] = jnp.zeros_like(cnt_ref)

    x1 = (x_ref[...]
          + jnp.dot(oa_ref[...], wo_ref[0:W_A_Q, :], preferred_element_type=F32)
          + jnp.dot(hb_ref[...], wo_ref[W_A_Q:, :], preferred_element_type=F32))

    hc = _rms(x1, gx_ref[...]).astype(BF16)
    cq = jnp.dot(hc, wq_ref[...], preferred_element_type=F32).astype(BF16)
    scale = HEAD_DIM_X ** -0.5
    heads = []
    for h in range(N_HEADS_X):
        sl = slice(h * HEAD_DIM_X, (h + 1) * HEAD_DIM_X)
        s = lax.dot_general(cq[:, sl], ck_ref[:, sl], (((1,), (1,)), ((), ())), preferred_element_type=F32) * scale
        p = jnp.exp(s - jnp.max(s, axis=-1, keepdims=True))
        co = jnp.dot(p.astype(BF16), cv_ref[:, sl], preferred_element_type=F32) / jnp.sum(p, axis=-1, keepdims=True)
        heads.append(co.astype(BF16))
    x2 = x1 + jnp.dot(jnp.concatenate(heads, axis=-1), wco_ref[...], preferred_element_type=F32)
    x2_ref[...] = x2

    hz = _rms(x2, gz_ref[...])
    hz_ref[...] = _pack_halves(hz)
    lg = jnp.dot(hz.astype(BF16), wr_ref[...], preferred_element_type=F32) + br_ref[...]
    col = lax.broadcasted_iota(I32, lg.shape, 1)
    big = jnp.int32(LANES)
    is_g = (col >= ROUTER_GROUP_COL) & (col < ROUTER_GROUP_COL + N_GROUPS)
    gl = jnp.where(is_g, lg, NEG_INF)
    gmax = jnp.max(gl, axis=-1, keepdims=True)
    gsum = jnp.sum(jnp.exp(gl - gmax), axis=-1, keepdims=True)
    g_prob = 1.0 / gsum
    g_idx = jnp.min(jnp.where(gl == gmax, col - ROUTER_GROUP_COL, big), axis=-1, keepdims=True)
    sel = (col < N_EXPERTS) & ((col // EXPERTS_PER_GROUP) == g_idx)
    el = jnp.where(sel, lg, NEG_INF)
    m1 = jnp.max(el, axis=-1, keepdims=True)
    i1 = jnp.min(jnp.where(el == m1, col, big), axis=-1, keepdims=True)
    el2 = jnp.where(col == i1, NEG_INF, el)
    m2 = jnp.max(el2, axis=-1, keepdims=True)
    i2 = jnp.min(jnp.where(el2 == m2, col, big), axis=-1, keepdims=True)
    z = jnp.sum(jnp.exp(el - m1), axis=-1, keepdims=True)
    p1 = 1.0 / z
    p2 = jnp.exp(m2 - m1) / z
    g1 = g_prob * (p1 / (p1 + p2))
    g2 = g_prob * (p2 / (p1 + p2))

    tm = lg.shape[0]
    used = jnp.where((col == i1) | (col == i2), 1.0, 0.0)
    tt = lax.broadcasted_iota(I32, (tm, tm), 0)
    ss = lax.broadcasted_iota(I32, (tm, tm), 1)
    earlier = jnp.where(ss < tt, 1.0, 0.0).astype(BF16)
    before = jnp.dot(earlier, used.astype(BF16), preferred_element_type=F32) + cnt_ref[...]
    r1 = jnp.sum(jnp.where(col == i1, before, 0.0), axis=-1, keepdims=True)
    r2 = jnp.sum(jnp.where(col == i2, before, 0.0), axis=-1, keepdims=True)
    cnt_ref[...] = cnt_ref[...] + jnp.sum(used, axis=0, keepdims=True)
    counts_ref[...] = jnp.broadcast_to(cnt_ref[...], counts_ref.shape)

    route = jnp.zeros_like(lg)
    for c, v in ((ROUTE_E1, i1.astype(F32)), (ROUTE_E2, i2.astype(F32)), (ROUTE_G1, g1), (ROUTE_G2, g2),
                 (ROUTE_R1, r1), (ROUTE_R2, r2)):
        route = jnp.where(col == c, v, route)
    route_ref[...] = route


def _mid(x2d, oa, hb, wo, gx, wq, ck, cv, wco, gz, wr, br, B, S):
    T = B * S
    tm = min(TM_MID, S)
    per_b = S // tm
    row = lambda w: pl.BlockSpec((tm, w), lambda i: (i, 0))
    full = lambda a: pl.BlockSpec(a.shape, lambda i: (0,) * a.ndim)
    kvspec = pl.BlockSpec((N_MEM, D_MODEL), lambda i: (i // per_b, 0))
    return pl.pallas_call(
        _mid_kernel,
        grid=(T // tm,),
        in_specs=[row(D_MODEL), row(W_A_Q), row(W_B), full(wo), full(gx), full(wq), kvspec, kvspec,
                  full(wco), full(gz), full(wr), full(br)],
        out_specs=[row(D_MODEL), row(HALF), row(LANES), pl.BlockSpec((SUBLANES, LANES), lambda i: (0, 0))],
        out_shape=[jax.ShapeDtypeStruct((T, D_MODEL), F32),
                   jax.ShapeDtypeStruct((T, HALF), U32),
                   jax.ShapeDtypeStruct((T, LANES), F32),
                   jax.ShapeDtypeStruct((SUBLANES, LANES), F32)],
        scratch_shapes=[pltpu.VMEM((1, LANES), F32)],
        compiler_params=pltpu.CompilerParams(dimension_semantics=("arbitrary",), vmem_limit_bytes=VMEM_LIMIT),
        name="mid",
    )(x2d, oa, hb, wo, gx, wq, ck, cv, wco, gz, wr, br)


def _dispatch_kernel(zf_ref, pos_ref, hz_ref, xs_hbm, zbuf, sem, zsem):
    i = pl.program_id(0)
    tm = hz_ref.shape[0]
    tmx = zbuf.shape[0]

    @pl.when(i == 0)
    def _():
        zbuf[...] = jnp.zeros_like(zbuf)

        def fill(t):
            return pltpu.make_async_copy(zbuf, xs_hbm.at[pl.ds(t * tmx, tmx)], zsem)

        def start(t, carry):
            @pl.when(zf_ref[t] != 0)
            def _():
                fill(t).start()
            return carry

        def wait(t, carry):
            @pl.when(zf_ref[t] != 0)
            def _():
                fill(t).wait()
            return carry

        lax.fori_loop(0, zf_ref.shape[0], start, 0)
        lax.fori_loop(0, zf_ref.shape[0], wait, 0)

    for r in range(tm):
        for k in range(TOP_K):
            pltpu.make_async_copy(hz_ref.at[pl.ds(r, 1)], xs_hbm.at[pl.ds(pos_ref[0, 0, TOP_K * r + k], 1)],
                                  sem).start(priority=k % 2)
    for k in range(TOP_K):
        pltpu.make_async_copy(hz_ref, xs_hbm.at[pl.ds(0, tm)], sem).wait()


def _dispatch(hz_packed, pos, zfill, n_slots, tmx):
    T = hz_packed.shape[0]
    tm = pos.shape[2] // TOP_K
    grid_spec = pltpu.PrefetchScalarGridSpec(
        num_scalar_prefetch=1,
        grid=(T // tm,),
        in_specs=[pl.BlockSpec((1, 1, TOP_K * tm), lambda i, zf: (i, 0, 0), memory_space=pltpu.SMEM),
                  pl.BlockSpec((tm, HALF), lambda i, zf: (i, 0))],
        out_specs=pl.BlockSpec(memory_space=pl.ANY),
        scratch_shapes=[pltpu.VMEM((tmx, HALF), U32), pltpu.SemaphoreType.DMA(()), pltpu.SemaphoreType.DMA(())],
    )
    return pl.pallas_call(
        _dispatch_kernel,
        grid_spec=grid_spec,
        out_shape=jax.ShapeDtypeStruct((n_slots, HALF), U32),
        compiler_params=pltpu.CompilerParams(dimension_semantics=("arbitrary",), vmem_limit_bytes=VMEM_LIMIT),
        name="dispatch",
    )(zfill, pos, hz_packed)


def _expert_kernel(te_ref, nt_ref, first_ref, slot_ref, next_ref, xs_ref, wg_hbm, wu_hbm, wd_hbm, ys_ref,
                   wg32, wu32, wd32, wgb, wub, wdb, wsem):
    i = pl.program_id(0)
    nt = nt_ref[0]

    def fetch(e, s):
        return [pltpu.make_async_copy(src.at[e], dst.at[s], wsem.at[s])
                for src, dst in ((wg_hbm, wg32), (wu_hbm, wu32), (wd_hbm, wd32))]

    @pl.when(i < nt)
    def _():
        @pl.when(i == 0)
        def _():
            for cp in fetch(te_ref[0], 0):
                cp.start()

        @pl.when(first_ref[i] != 0)
        def _():
            s = slot_ref[i]
            for cp in fetch(te_ref[i], s):
                cp.wait()

            @pl.when(next_ref[i] >= 0)
            def _():
                for cp in fetch(next_ref[i], 1 - s):
                    cp.start()

            wgb[...] = wg32[s].astype(BF16)
            wub[...] = wu32[s].astype(BF16)
            wdb[...] = wd32[s].astype(BF16)

        x = _unpack_halves(xs_ref[...]).astype(BF16)
        hg = jnp.dot(x, wgb[...], preferred_element_type=F32)
        hu = jnp.dot(x, wub[...], preferred_element_type=F32)
        a = (hg * jax.nn.sigmoid(hg) * hu).astype(BF16)
        ys_ref[...] = _pack_halves(jnp.dot(a, wdb[...], preferred_element_type=F32))

    @pl.when(i >= nt)
    def _():
        ys_ref[...] = jnp.zeros_like(ys_ref)


def _experts(xs, w_gate, w_up, w_down, tile_expert, ntiles, run_first, run_slot, run_next, tmx):
    n_tiles_max = tile_expert.shape[0]
    hbm = pl.BlockSpec(memory_space=pl.ANY)
    grid_spec = pltpu.PrefetchScalarGridSpec(
        num_scalar_prefetch=5,
        grid=(n_tiles_max,),
        in_specs=[pl.BlockSpec((tmx, HALF), lambda i, te, nt, *_: (jnp.minimum(i, nt[0] - 1), 0)), hbm, hbm, hbm],
        out_specs=pl.BlockSpec((tmx, HALF), lambda i, *_: (i, 0)),
        scratch_shapes=[pltpu.VMEM((2, D_MODEL, D_EXPERT), F32),
                        pltpu.VMEM((2, D_MODEL, D_EXPERT), F32),
                        pltpu.VMEM((2, D_EXPERT, D_MODEL), F32),
                        pltpu.VMEM((D_MODEL, D_EXPERT), BF16),
                        pltpu.VMEM((D_MODEL, D_EXPERT), BF16),
                        pltpu.VMEM((D_EXPERT, D_MODEL), BF16),
                        pltpu.SemaphoreType.DMA((2,))],
    )
    return pl.pallas_call(
        _expert_kernel,
        grid_spec=grid_spec,
        out_shape=jax.ShapeDtypeStruct(xs.shape, U32),
        compiler_params=pltpu.CompilerParams(dimension_semantics=("arbitrary",), vmem_limit_bytes=VMEM_LIMIT),
        name="experts",
    )(tile_expert, ntiles, run_first, run_slot, run_next, xs, w_gate, w_up, w_down)


def _final_kernel(posc_ref, posn_ref, x2_ref, route_ref, g_ref, ys_hbm, o_ref, ybuf, sem):
    i = pl.program_id(0)
    n = pl.num_programs(0)
    tm = x2_ref.shape[0]
    slot = i % 2

    def issue(pos_ref, s):
        for r in range(tm):
            for k in range(TOP_K):
                pltpu.make_async_copy(ys_hbm.at[pl.ds(pos_ref[0, 0, TOP_K * r + k], 1)],
                                      ybuf.at[s, k, pl.ds(r, 1)], sem.at[s]).start(priority=k % 2)

    @pl.when(i == 0)
    def _():
        issue(posc_ref, 0)

    @pl.when(i + 1 < n)
    def _():
        issue(posn_ref, 1 - slot)

    for k in range(TOP_K):
        pltpu.make_async_copy(ys_hbm.at[pl.ds(0, tm)], ybuf.at[slot, k], sem.at[slot]).wait()

    r = route_ref[...]
    g1 = r[:, ROUTE_G1:ROUTE_G1 + 1]
    g2 = r[:, ROUTE_G2:ROUTE_G2 + 1]
    xo = x2_ref[...] + g1 * _unpack_halves(ybuf[slot, 0]) + g2 * _unpack_halves(ybuf[slot, 1])
    o_ref[...] = _rms(xo, g_ref[...])


def _final(x2, ys, pos, route, g):
    T = x2.shape[0]
    nblk = pos.shape[0]
    tm = T // nblk
    row = lambda w: pl.BlockSpec((tm, w), lambda i: (i, 0))
    return pl.pallas_call(
        _final_kernel,
        grid=(nblk,),
        in_specs=[pl.BlockSpec((1, 1, TOP_K * tm), lambda i: (i, 0, 0), memory_space=pltpu.SMEM),
                  pl.BlockSpec((1, 1, TOP_K * tm), lambda i: (jnp.minimum(i + 1, nblk - 1), 0, 0),
                               memory_space=pltpu.SMEM),
                  row(D_MODEL), row(LANES), pl.BlockSpec(g.shape, lambda i: (0, 0)),
                  pl.BlockSpec(memory_space=pl.ANY)],
        out_specs=row(D_MODEL),
        out_shape=jax.ShapeDtypeStruct((T, D_MODEL), F32),
        scratch_shapes=[pltpu.VMEM((2, TOP_K, tm, HALF), U32), pltpu.SemaphoreType.DMA((2,))],
        compiler_params=pltpu.CompilerParams(dimension_semantics=("arbitrary",), vmem_limit_bytes=VMEM_LIMIT),
        name="final",
    )(pos, pos, x2, route, g, ys)


def _band_bias(table):
    i = jnp.arange(BLOCK)[:, None]
    j = jnp.arange(2 * BLOCK)[None, :]
    n = jnp.maximum(i + BLOCK - j, 0)
    nf = jnp.maximum(n, 1).astype(F32)
    large = MAX_EXACT + (jnp.log(nf / MAX_EXACT) / math.log(MAX_DISTANCE / MAX_EXACT)
                         * (NUM_BUCKETS - MAX_EXACT)).astype(I32)
    large = jnp.minimum(large, NUM_BUCKETS - 1)
    bucket = jnp.where(n < MAX_EXACT, n, large)
    onehot = (bucket[:, :, None] == jnp.arange(NUM_BUCKETS)[None, None, :]).astype(F32)
    bias = jnp.einsum("ijb,bh->hij", onehot, table.astype(F32), precision=lax.Precision.HIGHEST)
    d = i + BLOCK - j
    band_ok = (d >= 0) & (d < WINDOW)
    first_ok = band_ok & (j >= BLOCK)
    return jnp.stack([jnp.where(first_ok[None], bias, NEG_INF), jnp.where(band_ok[None], bias, NEG_INF)])


def _dispatch_plan(route, counts_f, tmx, n_tiles_max, tm_rows):
    T = route.shape[0]
    experts = jnp.arange(N_EXPERTS, dtype=I32)
    counts = counts_f[0, :N_EXPERTS].astype(I32)
    ptiles = (counts + tmx - 1) // tmx
    tile_end = jnp.cumsum(ptiles)
    nt = tile_end[-1]
    row_off = (tile_end - ptiles) * tmx

    def slot(e_col, r_col):
        e = route[:, e_col].astype(I32)
        off = jnp.sum(jnp.where(e[:, None] == experts[None, :], row_off[None, :], 0), axis=1)
        return off + route[:, r_col].astype(I32)

    pos = jnp.stack([slot(ROUTE_E1, ROUTE_R1), slot(ROUTE_E2, ROUTE_R2)], axis=1)
    pos = pos.reshape(T // tm_rows, 1, tm_rows * TOP_K)

    tile_ids = jnp.arange(n_tiles_max, dtype=I32)
    expert_of = lambda t: jnp.sum((tile_end[None, :] <= t[:, None]).astype(I32), axis=1)
    te = expert_of(jnp.minimum(tile_ids, nt - 1))
    partial = jnp.any((tile_ids[:, None] == (tile_end - 1)[None, :]) & (counts % tmx != 0)[None, :], axis=1)
    zfill = (partial | (tile_ids >= nt)).astype(I32)

    used = ptiles > 0
    run_first = (jnp.any((tile_ids[:, None] == (tile_end - ptiles)[None, :]) & used[None, :], axis=1)
                 & (tile_ids < nt)).astype(I32)
    run_slot = (jnp.cumsum(run_first) - 1) % 2
    later_used = used[None, :] & (experts[None, :] > experts[:, None])
    next_of = jnp.min(jnp.where(later_used, experts[None, :], N_EXPERTS), axis=1)
    next_of = jnp.where(next_of < N_EXPERTS, next_of, -1)
    run_next = jnp.sum(jnp.where(te[:, None] == experts[None, :], next_of[None, :], 0), axis=1)
    return pos, te, nt.reshape(1), zfill, run_first, run_slot.astype(I32), run_next.astype(I32)


def kernel(x, mem, rel_bias_table, norm_mix, w_in, attn_sinks, conv_w, conv_b, gate_bias_i, gate_bias_f, mlstm_norm, w_out, norm_cross, norm_mem, w_cq, w_ck, w_cv, w_co, norm_moe, w_router_group, b_router_group, w_router_expert, b_router_expert, w_exp_gate, w_exp_up, w_exp_down, norm_final):
    B, S, _ = x.shape
    T = B * S
    depth = w_in.shape[0]
    x2d = x.reshape(T, D_MODEL)
    mem2d = mem.reshape(B * N_MEM, D_MODEL)
    bias = _band_bias(rel_bias_table)

    tmx = min(TM_EXPERT, T)
    n_tiles_max = (T * TOP_K) // tmx + N_EXPERTS
    tm_rows = min(TM_ROWDMA, T)

    assert depth == 1, "the final combine is fused with the final norm: single layer only"
    l = 0
    w_pad = jnp.pad(w_in[l], ((0, 0), (0, C_GATE + LANES - D_IN))).astype(BF16)
    wg_t = w_in[l][:, C_GATE:].T.astype(BF16)
    gb = jnp.concatenate([gate_bias_i[l], gate_bias_f[l]]).astype(F32)
    gbias_col = jnp.pad(gb, (0, LANES - GATE_ROWS))[None, :]
    gbias_row = jnp.broadcast_to(gb[:, None], (GATE_ROWS, min(TM_INPROJ, S)))
    qa, kva, qkb, vb, ob, gc, gr = _inproj(x2d, norm_mix[l][None, :], w_pad, wg_t, gbias_col, gbias_row, B, S)

    out_a = _swa(qa, kva, bias, attn_sinks[l].astype(F32), B, S)
    per_seq = lambda a: a.reshape(B, S, a.shape[-1])
    hb = _mlstm(per_seq(qkb), per_seq(vb), per_seq(ob), per_seq(gc), gr, conv_w[l][:, 0, :].astype(F32),
                conv_b[l][None, :].astype(F32), mlstm_norm[l][None, :].astype(F32)).reshape(T, W_B)

    ck, cv = _memkv(mem2d, norm_mem[l][None, :], w_ck[l].astype(BF16), w_cv[l].astype(BF16), B)

    wr = jnp.pad(jnp.concatenate([w_router_expert[l], w_router_group[l]], axis=1),
                 ((0, 0), (0, LANES - N_EXPERTS - N_GROUPS))).astype(BF16)
    br = jnp.pad(jnp.concatenate([b_router_expert[l], b_router_group[l]]),
                 (0, LANES - N_EXPERTS - N_GROUPS)).astype(F32)[None, :]
    x2, hz_packed, route, counts = _mid(x2d, out_a, hb, w_out[l].astype(BF16), norm_cross[l][None, :],
                                        w_cq[l].astype(BF16), ck, cv, w_co[l].astype(BF16), norm_moe[l][None, :],
                                        wr, br, B, S)

    pos, te, nt, zfill, run_first, run_slot, run_next = _dispatch_plan(route, counts, tmx, n_tiles_max, tm_rows)
    xs = _dispatch(hz_packed, pos, zfill, n_tiles_max * tmx, tmx)
    ys = _experts(xs, w_exp_gate[l], w_exp_up[l], w_exp_down[l], te, nt, run_first, run_slot, run_next, tmx)
    out = _final(x2, ys, pos, route, norm_final[None, :])
    return out.reshape(B, S, D_MODEL)
```

```python
import functools
import math

import jax
import jax.numpy as jnp
from jax import lax
from jax.experimental import pallas as pl
from jax.experimental.pallas import tpu as pltpu

F32 = jnp.float32
BF16 = jnp.bfloat16
U32 = jnp.uint32
I32 = jnp.int32

D_MODEL = 1024
N_MEM = 256
N_HEADS_A = 8
N_KV_A = 2
HEAD_DIM_A = 64
BLOCK = 128
WINDOW = 128
NUM_BUCKETS = 32
MAX_EXACT = NUM_BUCKETS // 2
MAX_DISTANCE = 128
N_HEADS_B = 4
HEAD_DIM_B = 128
CHUNK = 128
CONV_WIDTH = 4
N_HEADS_X = 4
HEAD_DIM_X = D_MODEL // N_HEADS_X
N_GROUPS = 4
EXPERTS_PER_GROUP = 8
N_EXPERTS = N_GROUPS * EXPERTS_PER_GROUP
TOP_K = 2
D_EXPERT = 512
EPS = 1e-6
NEG_INF = -1e30

W_A_Q = N_HEADS_A * HEAD_DIM_A
W_A_KV = N_KV_A * HEAD_DIM_A
W_B = N_HEADS_B * HEAD_DIM_B
C_QA = 0
C_KVA = C_QA + W_A_Q
C_QKB = C_KVA + 2 * W_A_KV
C_VB = C_QKB + 2 * W_B
C_OB = C_VB + W_B
C_GATE = C_OB + W_B
D_IN = C_GATE + 2 * N_HEADS_B

LANES = 128
SUBLANES = 8
GATE_ROWS = 8
HALF = D_MODEL // 2

TM_INPROJ = 512
TM_MID = 512
TM_ROWDMA = 512
TM_EXPERT = 256

VMEM_LIMIT = 48 * 1024 * 1024


def _rms(xf, g):
    return xf * lax.rsqrt(jnp.mean(xf * xf, axis=-1, keepdims=True) + EPS) * g


def _pack_halves(v):
    b = pltpu.bitcast(v.astype(BF16).astype(F32), U32)
    return (b[:, :HALF] >> 16) | b[:, HALF:]


def _unpack_halves(p):
    lo = pltpu.bitcast(p << 16, F32)
    hi = pltpu.bitcast(p & jnp.uint32(0xFFFF0000), F32)
    return jnp.concatenate([lo, hi], axis=-1)


def _log_sigmoid(z):
    return jnp.minimum(z, 0.0) - jnp.log1p(jnp.exp(-jnp.abs(z)))


def _inproj_kernel(x_ref, g_ref, w_ref, gbc_ref, qa_ref, kva_ref, qkb_ref, vb_ref, ob_ref, gc_ref, gr_ref):
    tm = x_ref.shape[0]
    h = _rms(x_ref[...], g_ref[...]).astype(BF16)

    def mm(lo, hi):
        return jnp.dot(h, w_ref[:, lo:hi], preferred_element_type=F32)

    qa_ref[...] = mm(C_QA, C_KVA).astype(BF16)
    kva_ref[...] = mm(C_KVA, C_QKB).astype(BF16)
    qkb_ref[...] = mm(C_QKB, C_VB).astype(BF16)
    vb_ref[...] = mm(C_VB, C_OB).astype(BF16)
    ob_ref[...] = mm(C_OB, C_GATE).astype(BF16)

    H, L = N_HEADS_B, CHUNK
    gcol = mm(C_GATE, C_GATE + LANES) + gbc_ref[...]
    grow = jnp.transpose(gcol)[0:GATE_ROWS, :]
    lane_c = lax.broadcasted_iota(I32, (L, LANES), 1)
    is_f_col = (lane_c >= H) & (lane_c < 2 * H)
    is_f_row = lax.broadcasted_iota(I32, (GATE_ROWS, L), 0) >= H
    ti = lax.broadcasted_iota(I32, (L, L), 0)
    si = lax.broadcasted_iota(I32, (L, L), 1)
    tril = jnp.where(si <= ti, 1.0, 0.0).astype(F32)
    triu = jnp.where(si >= ti, 1.0, 0.0).astype(F32)
    for c in range(tm // L):
        rows = slice(c * L, (c + 1) * L)
        gcol_c = gcol[rows, :]
        fcol = jnp.where(is_f_col, _log_sigmoid(gcol_c), 0.0)
        bcol = jnp.dot(tril, fcol, preferred_element_type=F32, precision=lax.Precision.HIGHEST)
        gc_ref[rows, :] = jnp.where(is_f_col, bcol, gcol_c)
        grow_c = grow[:, rows]
        frow = jnp.where(is_f_row, _log_sigmoid(grow_c), 0.0)
        brow = jnp.dot(frow, triu, preferred_element_type=F32, precision=lax.Precision.HIGHEST)
        gr_ref[:, rows] = jnp.where(is_f_row, brow, grow_c)


def _inproj(x2d, g, w_pad, gbias_col, B, S):
    T = x2d.shape[0]
    tm = min(TM_INPROJ, S)
    tiles_per_seq = S // tm
    row = lambda w: pl.BlockSpec((tm, w), lambda i: (i, 0))
    full = lambda a: pl.BlockSpec(a.shape, lambda i: (0,) * a.ndim)
    return pl.pallas_call(
        _inproj_kernel,
        grid=(T // tm,),
        in_specs=[row(D_MODEL), full(g), full(w_pad), full(gbias_col)],
        out_specs=[row(W_A_Q), row(2 * W_A_KV), row(2 * W_B), row(W_B), row(W_B), row(LANES),
                   pl.BlockSpec((None, GATE_ROWS, tm), lambda i: (i // tiles_per_seq, 0, i % tiles_per_seq))],
        out_shape=[jax.ShapeDtypeStruct((T, W_A_Q), BF16),
                   jax.ShapeDtypeStruct((T, 2 * W_A_KV), BF16),
                   jax.ShapeDtypeStruct((T, 2 * W_B), BF16),
                   jax.ShapeDtypeStruct((T, W_B), BF16),
                   jax.ShapeDtypeStruct((T, W_B), BF16),
                   jax.ShapeDtypeStruct((T, LANES), F32),
                   jax.ShapeDtypeStruct((B, GATE_ROWS, S), F32)],
        compiler_params=pltpu.CompilerParams(dimension_semantics=("parallel",), vmem_limit_bytes=VMEM_LIMIT),
        name="inproj",
    )(x2d, g, w_pad, gbias_col)


def _swa_kernel(sink_ref, qa_ref, kvc_ref, kvp_ref, bias_ref, o_ref):
    q = qa_ref[...]
    kvp = kvp_ref[...].astype(F32)
    kvc = kvc_ref[...].astype(F32)
    kband = jnp.concatenate([kvp[:, :W_A_KV], kvc[:, :W_A_KV]], axis=0)
    vband = jnp.concatenate([kvp[:, W_A_KV:], kvc[:, W_A_KV:]], axis=0)
    lane = lax.broadcasted_iota(I32, (2 * BLOCK, LANES), 1)
    lo = lane < HEAD_DIM_A

    def placements(band):
        swapped = pltpu.roll(band, HEAD_DIM_A, axis=1)
        z = jnp.zeros_like(band)
        return {(0, 0): jnp.where(lo, band, z).astype(BF16), (0, 1): jnp.where(lo, z, swapped).astype(BF16),
                (1, 0): jnp.where(lo, swapped, z).astype(BF16), (1, 1): jnp.where(lo, z, band).astype(BF16)}

    kpl = placements(kband)
    vpl = placements(vband)

    scale = HEAD_DIM_A ** -0.5
    group = N_HEADS_A // N_KV_A

    for pair in range(N_HEADS_A // 2):
        qt = q[:, pair * LANES:(pair + 1) * LANES]
        acc = None
        for half in range(2):
            h = 2 * pair + half
            g = h // group
            s = lax.dot_general(qt, kpl[(g, half)], (((1,), (1,)), ((), ())), preferred_element_type=F32)
            s = s * scale + bias_ref[h]
            sink = sink_ref[h]
            m = jnp.maximum(jnp.max(s, axis=-1, keepdims=True), sink)
            p = jnp.exp(s - m)
            denom = jnp.sum(p, axis=-1, keepdims=True) + jnp.exp(sink - m)
            o = jnp.dot(p.astype(BF16), vpl[(g, half)], preferred_element_type=F32) / denom
            acc = o if acc is None else acc + o
        o_ref[:, pair * LANES:(pair + 1) * LANES] = acc.astype(BF16)


def _swa(qa, kva, bias, sinks, B, S):
    nb = S // BLOCK
    T = B * S
    return pl.pallas_call(
        _swa_kernel,
        grid=(B, nb),
        in_specs=[pl.BlockSpec(memory_space=pltpu.SMEM),
                  pl.BlockSpec((BLOCK, W_A_Q), lambda b, n: (b * nb + n, 0)),
                  pl.BlockSpec((BLOCK, 2 * W_A_KV), lambda b, n: (b * nb + n, 0)),
                  pl.BlockSpec((BLOCK, 2 * W_A_KV), lambda b, n: (b * nb + jnp.maximum(n - 1, 0), 0)),
                  pl.BlockSpec((None,) + bias.shape[1:], lambda b, n: (jnp.minimum(n, 1), 0, 0, 0))],
        out_specs=pl.BlockSpec((BLOCK, W_A_Q), lambda b, n: (b * nb + n, 0)),
        out_shape=jax.ShapeDtypeStruct((T, W_A_Q), BF16),
        compiler_params=pltpu.CompilerParams(dimension_semantics=("parallel", "parallel"),
                                             vmem_limit_bytes=VMEM_LIMIT),
        name="swa",
    )(sinks, qa, kva, kva, bias)


CONV_HALO = 16


def _mlstm_kernel(qkc_ref, qkp_ref, vb_ref, ob_ref, gc_ref, gr_ref, cw_ref, cb_ref, nrm_ref,
                  o_ref, state_ref, m_ref):
    c = pl.program_id(0)
    B = qkc_ref.shape[0]
    H, D, L = N_HEADS_B, HEAD_DIM_B, CHUNK

    @pl.when(c == 0)
    def _():
        state_ref[...] = jnp.zeros_like(state_ref)
        m_ref[...] = jnp.zeros_like(m_ref)

    rr = lax.broadcasted_iota(I32, (L, CONV_HALO + L), 0)
    cc = lax.broadcasted_iota(I32, (L, CONV_HALO + L), 1)
    shifts = {delay: jnp.where(cc == rr + (CONV_HALO - delay), 1.0, 0.0).astype(BF16)
              for delay in range(1, CONV_WIDTH)}
    ti = lax.broadcasted_iota(I32, (L, L), 0)
    si = lax.broadcasted_iota(I32, (L, L), 1)
    tri = si <= ti
    ones_blk = jnp.ones((L, D), BF16)

    def conv_silu(b):
        prev = qkp_ref[b]
        prev = jnp.where(c > 0, prev, jnp.zeros_like(prev))
        cur = qkc_ref[b]
        ext = jnp.concatenate([prev, cur], axis=0)
        y = cb_ref[...] + cw_ref[CONV_WIDTH - 1:CONV_WIDTH, :] * cur.astype(F32)
        for delay in range(1, CONV_WIDTH):
            tap = CONV_WIDTH - 1 - delay
            y = y + cw_ref[tap:tap + 1, :] * jnp.dot(shifts[delay], ext, preferred_element_type=F32)
        return y * jax.nn.sigmoid(y)

    states = [[state_ref[b, h] for h in range(H)] for b in range(B)]
    m_alls = [m_ref[b] for b in range(B)]
    new_states, new_m, outs = {}, {}, {}

    for b, h in [(b, h) for b in range(B) for h in range(H)]:
        if h == 0:
            qk = conv_silu(b)
            gcol = gc_ref[b]
            grow = gr_ref[b]
        qh = (qk[:, h * D:(h + 1) * D] * (D ** -0.5)).astype(BF16)
        k_t = qk[:, W_B + h * D:W_B + (h + 1) * D].T
        v1 = jnp.concatenate([vb_ref[b, :, h * D:(h + 1) * D], ones_blk], axis=-1)
        b_r = grow[H + h:H + h + 1, :]
        g_r = grow[h:h + 1, :] - b_r
        b_c = gcol[:, H + h:H + h + 1]
        m_prev = m_alls[b][h:h + 1, 0:1]
        state = states[b][h]

        gmat = jnp.where(tri, g_r, NEG_INF)
        m_c = jnp.maximum(jnp.max(gmat, axis=-1, keepdims=True), m_prev)
        a_inter = jnp.exp(m_prev - m_c)
        sc = jnp.dot(qh, k_t.astype(BF16), preferred_element_type=F32) * jnp.exp(gmat - m_c)
        tot = (jnp.dot(sc.astype(BF16), v1, preferred_element_type=F32)
               + a_inter * jnp.dot(qh, state.astype(BF16), preferred_element_type=F32))
        num = tot[:, :D]
        den = tot[:, D:]
        hh = num / jnp.maximum(jnp.abs(den), jnp.exp(-(b_c + m_c)))

        b_last = b_r[:, L - 1:L]
        m_new = jnp.maximum(b_last + m_prev, b_last + jnp.max(g_r, axis=-1, keepdims=True))
        w_r = jnp.exp(g_r + (b_last - m_new))
        decay = jnp.exp(b_last + m_prev - m_new)
        upd = jnp.dot((k_t * w_r).astype(BF16), v1, preferred_element_type=F32)
        new_states[b, h] = decay * state + upd
        new_m[b, h] = jnp.broadcast_to(m_new, (1, LANES))

        og = jax.nn.sigmoid(ob_ref[b, :, h * D:(h + 1) * D].astype(F32))
        hb = og * hh
        hb = hb * lax.rsqrt(jnp.mean(hb * hb, axis=-1, keepdims=True) + EPS)
        outs[b, h] = (hb * nrm_ref[:, h * D:(h + 1) * D]).astype(BF16)

    for b, h in new_states:
        state_ref[b, h] = new_states[b, h]
        m_ref[b, h:h + 1, :] = new_m[b, h]
        o_ref[b, :, h * D:(h + 1) * D] = outs[b, h]


def _mlstm(qkb, vb, ob, gc, gr, conv_w, conv_b, nrm):
    B, S, _ = qkb.shape
    nc = S // CHUNK
    halo_per_chunk = CHUNK // CONV_HALO
    blk = lambda w: pl.BlockSpec((B, CHUNK, w), lambda c: (0, c, 0))
    full = lambda a: pl.BlockSpec(a.shape, lambda c: (0,) * a.ndim)
    return pl.pallas_call(
        _mlstm_kernel,
        grid=(nc,),
        in_specs=[blk(2 * W_B),
                  pl.BlockSpec((B, CONV_HALO, 2 * W_B), lambda c: (0, jnp.maximum(c * halo_per_chunk - 1, 0), 0)),
                  blk(W_B), blk(W_B), blk(LANES),
                  pl.BlockSpec((B, GATE_ROWS, CHUNK), lambda c: (0, 0, c)),
                  full(conv_w), full(conv_b), full(nrm)],
        out_specs=blk(W_B),
        out_shape=jax.ShapeDtypeStruct((B, S, W_B), BF16),
        scratch_shapes=[pltpu.VMEM((B, N_HEADS_B, HEAD_DIM_B, 2 * HEAD_DIM_B), F32),
                        pltpu.VMEM((B, GATE_ROWS, LANES), F32)],
        compiler_params=pltpu.CompilerParams(dimension_semantics=("arbitrary",), vmem_limit_bytes=VMEM_LIMIT),
        name="mlstm",
    )(qkb, qkb, vb, ob, gc, gr, conv_w, conv_b, nrm)


def _memkv_kernel(mem_ref, g_ref, wk_ref, wv_ref, k_ref, v_ref):
    hm = _rms(mem_ref[...], g_ref[...]).astype(BF16)
    k_ref[...] = jnp.dot(hm, wk_ref[...], preferred_element_type=F32).astype(BF16)
    v_ref[...] = jnp.dot(hm, wv_ref[...], preferred_element_type=F32).astype(BF16)


def _memkv(mem2d, g, wk, wv, B):
    full = lambda a: pl.BlockSpec(a.shape, lambda b: (0,) * a.ndim)
    blk = pl.BlockSpec((N_MEM, D_MODEL), lambda b: (b, 0))
    return pl.pallas_call(
        _memkv_kernel,
        grid=(B,),
        in_specs=[blk, full(g), full(wk), full(wv)],
        out_specs=[blk, blk],
        out_shape=[jax.ShapeDtypeStruct((B * N_MEM, D_MODEL), BF16)] * 2,
        compiler_params=pltpu.CompilerParams(dimension_semantics=("parallel",), vmem_limit_bytes=VMEM_LIMIT),
        name="memkv",
    )(mem2d, g, wk, wv)


ROUTE_E1, ROUTE_E2, ROUTE_G1, ROUTE_G2, ROUTE_R1, ROUTE_R2 = 0, 1, 2, 3, 4, 5
ROUTER_GROUP_COL = N_EXPERTS


def _mid_kernel(x_ref, oa_ref, hb_ref, wo_ref, gx_ref, wq_ref, ck_ref, cv_ref, wco_ref, gz_ref, wr_ref, br_ref,
                x2_ref, hz_ref, route_ref, rrows_ref, counts_ref, cnt_ref):
    @pl.when(pl.program_id(0) == 0)
    def _():
        cnt_ref[...] = jnp.zeros_like(cnt_ref)

    x1 = (x_ref[...]
          + jnp.dot(oa_ref[...], wo_ref[0:W_A_Q, :], preferred_element_type=F32)
          + jnp.dot(hb_ref[...], wo_ref[W_A_Q:, :], preferred_element_type=F32))

    hc = _rms(x1, gx_ref[...]).astype(BF16)
    cq = jnp.dot(hc, wq_ref[...], preferred_element_type=F32).astype(BF16)
    scale = HEAD_DIM_X ** -0.5
    heads = []
    for h in range(N_HEADS_X):
        sl = slice(h * HEAD_DIM_X, (h + 1) * HEAD_DIM_X)
        s = lax.dot_general(cq[:, sl], ck_ref[:, sl], (((1,), (1,)), ((), ())), preferred_element_type=F32) * scale
        p = jnp.exp(s - jnp.max(s, axis=-1, keepdims=True))
        co = jnp.dot(p.astype(BF16), cv_ref[:, sl], preferred_element_type=F32) / jnp.sum(p, axis=-1, keepdims=True)
        heads.append(co.astype(BF16))
    x2 = x1 + jnp.dot(jnp.concatenate(heads, axis=-1), wco_ref[...], preferred_element_type=F32)
    x2_ref[...] = x2

    hz = _rms(x2, gz_ref[...])
    hz_ref[...] = _pack_halves(hz)
    lg = jnp.dot(hz.astype(BF16), wr_ref[...], preferred_element_type=F32) + br_ref[...]
    col = lax.broadcasted_iota(I32, lg.shape, 1)
    big = jnp.int32(LANES)
    is_g = (col >= ROUTER_GROUP_COL) & (col < ROUTER_GROUP_COL + N_GROUPS)
    gl = jnp.where(is_g, lg, NEG_INF)
    gmax = jnp.max(gl, axis=-1, keepdims=True)
    gsum = jnp.sum(jnp.exp(gl - gmax), axis=-1, keepdims=True)
    g_prob = 1.0 / gsum
    g_idx = jnp.min(jnp.where(gl == gmax, col - ROUTER_GROUP_COL, big), axis=-1, keepdims=True)
    sel = (col < N_EXPERTS) & ((col // EXPERTS_PER_GROUP) == g_idx)
    el = jnp.where(sel, lg, NEG_INF)
    m1 = jnp.max(el, axis=-1, keepdims=True)
    i1 = jnp.min(jnp.where(el == m1, col, big), axis=-1, keepdims=True)
    el2 = jnp.where(col == i1, NEG_INF, el)
    m2 = jnp.max(el2, axis=-1, keepdims=True)
    i2 = jnp.min(jnp.where(el2 == m2, col, big), axis=-1, keepdims=True)
    z = jnp.sum(jnp.exp(el - m1), axis=-1, keepdims=True)
    p1 = 1.0 / z
    p2 = jnp.exp(m2 - m1) / z
    g1 = g_prob * (p1 / (p1 + p2))
    g2 = g_prob * (p2 / (p1 + p2))

    tm = lg.shape[0]
    used = jnp.where((col == i1) | (col == i2), 1.0, 0.0)
    tt = lax.broadcasted_iota(I32, (tm, tm), 0)
    ss = lax.broadcasted_iota(I32, (tm, tm), 1)
    earlier = jnp.where(ss < tt, 1.0, 0.0).astype(BF16)
    before = jnp.dot(earlier, used.astype(BF16), preferred_element_type=F32) + cnt_ref[...]
    r1 = jnp.sum(jnp.where(col == i1, before, 0.0), axis=-1, keepdims=True)
    r2 = jnp.sum(jnp.where(col == i2, before, 0.0), axis=-1, keepdims=True)
    cnt_ref[...] = cnt_ref[...] + jnp.sum(used, axis=0, keepdims=True)
    counts_ref[...] = jnp.broadcast_to(cnt_ref[...], counts_ref.shape)

    route = jnp.zeros_like(lg)
    for c, v in ((ROUTE_E1, i1.astype(F32)), (ROUTE_E2, i2.astype(F32)), (ROUTE_G1, g1), (ROUTE_G2, g2),
                 (ROUTE_R1, r1), (ROUTE_R2, r2)):
        route = jnp.where(col == c, v, route)
    route_ref[...] = route
    rrows_ref[...] = jnp.transpose(route)[0:SUBLANES, :]


def _mid(x2d, oa, hb, wo, gx, wq, ck, cv, wco, gz, wr, br, B, S):
    T = B * S
    tm = min(TM_MID, S)
    per_b = S // tm
    row = lambda w: pl.BlockSpec((tm, w), lambda i: (i, 0))
    full = lambda a: pl.BlockSpec(a.shape, lambda i: (0,) * a.ndim)
    kvspec = pl.BlockSpec((N_MEM, D_MODEL), lambda i: (i // per_b, 0))
    return pl.pallas_call(
        _mid_kernel,
        grid=(T // tm,),
        in_specs=[row(D_MODEL), row(W_A_Q), row(W_B), full(wo), full(gx), full(wq), kvspec, kvspec,
                  full(wco), full(gz), full(wr), full(br)],
        out_specs=[row(D_MODEL), row(HALF), row(LANES), pl.BlockSpec((SUBLANES, tm), lambda i: (0, i)),
                   pl.BlockSpec((SUBLANES, LANES), lambda i: (0, 0))],
        out_shape=[jax.ShapeDtypeStruct((T, D_MODEL), F32),
                   jax.ShapeDtypeStruct((T, HALF), U32),
                   jax.ShapeDtypeStruct((T, LANES), F32),
                   jax.ShapeDtypeStruct((SUBLANES, T), F32),
                   jax.ShapeDtypeStruct((SUBLANES, LANES), F32)],
        scratch_shapes=[pltpu.VMEM((1, LANES), F32)],
        compiler_params=pltpu.CompilerParams(dimension_semantics=("arbitrary",), vmem_limit_bytes=VMEM_LIMIT),
        name="mid",
    )(x2d, oa, hb, wo, gx, wq, ck, cv, wco, gz, wr, br)


def _dispatch_kernel(zf_ref, pos_ref, hz_ref, xs_hbm, zbuf, sem, zsem):
    i = pl.program_id(0)
    tm = hz_ref.shape[0]
    zrows = zbuf.shape[0]

    @pl.when(i == 0)
    def _():
        zbuf[...] = jnp.zeros_like(zbuf)

        def fill(t):
            return pltpu.make_async_copy(zbuf, xs_hbm.at[pl.ds(t * zrows, zrows)], zsem)

        def start(t, carry):
            @pl.when(zf_ref[t] != 0)
            def _():
                fill(t).start()
            return carry

        def wait(t, carry):
            @pl.when(zf_ref[t] != 0)
            def _():
                fill(t).wait()
            return carry

        lax.fori_loop(0, zf_ref.shape[0], start, 0)
        lax.fori_loop(0, zf_ref.shape[0], wait, 0)

    for r in range(tm):
        for k in range(TOP_K):
            pltpu.make_async_copy(hz_ref.at[pl.ds(r, 1)], xs_hbm.at[pl.ds(pos_ref[0, 0, k * tm + r], 1)],
                                  sem).start(priority=k % 2)
    for k in range(TOP_K):
        pltpu.make_async_copy(hz_ref, xs_hbm.at[pl.ds(0, tm)], sem).wait()


def _dispatch(hz_packed, pos, zfill, n_slots, tmx):
    T = hz_packed.shape[0]
    tm = pos.shape[2] // TOP_K
    grid_spec = pltpu.PrefetchScalarGridSpec(
        num_scalar_prefetch=1,
        grid=(T // tm,),
        in_specs=[pl.BlockSpec((1, 1, TOP_K * tm), lambda i, zf: (i, 0, 0), memory_space=pltpu.SMEM),
                  pl.BlockSpec((tm, HALF), lambda i, zf: (i, 0))],
        out_specs=pl.BlockSpec(memory_space=pl.ANY),
        scratch_shapes=[pltpu.VMEM((tmx, HALF), U32), pltpu.SemaphoreType.DMA(()), pltpu.SemaphoreType.DMA(())],
    )
    return pl.pallas_call(
        _dispatch_kernel,
        grid_spec=grid_spec,
        out_shape=jax.ShapeDtypeStruct((n_slots, HALF), U32),
        compiler_params=pltpu.CompilerParams(dimension_semantics=("arbitrary",), vmem_limit_bytes=VMEM_LIMIT),
        name="dispatch",
    )(zfill, pos, hz_packed)


def _expert_kernel(te_ref, nt_ref, first_ref, slot_ref, next_ref, xs_ref, wg_hbm, wu_hbm, wd_hbm, ys_ref,
                   wg32, wu32, wd32, wgb, wub, wdb, wsem):
    i = pl.program_id(0)
    nt = nt_ref[0]

    def fetch(e, s):
        return [pltpu.make_async_copy(src.at[e], dst.at[s], wsem.at[s])
                for src, dst in ((wg_hbm, wg32), (wu_hbm, wu32), (wd_hbm, wd32))]

    @pl.when(i < nt)
    def _():
        @pl.when(i == 0)
        def _():
            for cp in fetch(te_ref[0], 0):
                cp.start()

        @pl.when(first_ref[i] != 0)
        def _():
            s = slot_ref[i]
            for cp in fetch(te_ref[i], s):
                cp.wait()

            @pl.when(next_ref[i] >= 0)
            def _():
                for cp in fetch(next_ref[i], 1 - s):
                    cp.start()

            wgb[...] = wg32[s].astype(BF16)
            wub[...] = wu32[s].astype(BF16)
            wdb[...] = wd32[s].astype(BF16)

        x = _unpack_halves(xs_ref[...]).astype(BF16)
        hg = jnp.dot(x, wgb[...], preferred_element_type=F32)
        hu = jnp.dot(x, wub[...], preferred_element_type=F32)
        a = (hg * jax.nn.sigmoid(hg) * hu).astype(BF16)
        ys_ref[...] = _pack_halves(jnp.dot(a, wdb[...], preferred_element_type=F32))

    @pl.when(i >= nt)
    def _():
        ys_ref[...] = jnp.zeros_like(ys_ref)


def _experts(xs, w_gate, w_up, w_down, tile_expert, ntiles, run_first, run_slot, run_next, tmx):
    n_tiles_max = tile_expert.shape[0]
    hbm = pl.BlockSpec(memory_space=pl.ANY)
    grid_spec = pltpu.PrefetchScalarGridSpec(
        num_scalar_prefetch=5,
        grid=(n_tiles_max,),
        in_specs=[pl.BlockSpec((tmx, HALF), lambda i, te, nt, *_: (jnp.minimum(i, nt[0] - 1), 0)), hbm, hbm, hbm],
        out_specs=pl.BlockSpec((tmx, HALF), lambda i, *_: (i, 0)),
        scratch_shapes=[pltpu.VMEM((2, D_MODEL, D_EXPERT), F32),
                        pltpu.VMEM((2, D_MODEL, D_EXPERT), F32),
                        pltpu.VMEM((2, D_EXPERT, D_MODEL), F32),
                        pltpu.VMEM((D_MODEL, D_EXPERT), BF16),
                        pltpu.VMEM((D_MODEL, D_EXPERT), BF16),
                        pltpu.VMEM((D_EXPERT, D_MODEL), BF16),
                        pltpu.SemaphoreType.DMA((2,))],
    )
    return pl.pallas_call(
        _expert_kernel,
        grid_spec=grid_spec,
        out_shape=jax.ShapeDtypeStruct(xs.shape, U32),
        compiler_params=pltpu.CompilerParams(dimension_semantics=("arbitrary",), vmem_limit_bytes=VMEM_LIMIT),
        name="experts",
    )(tile_expert, ntiles, run_first, run_slot, run_next, xs, w_gate, w_up, w_down)


def _final_kernel(posc_ref, posn_ref, x2_ref, route_ref, g_ref, ys_hbm, o_ref, ybuf, sem):
    i = pl.program_id(0)
    n = pl.num_programs(0)
    tm = x2_ref.shape[0]
    slot = i % 2

    def issue(pos_ref, s):
        for r in range(tm):
            for k in range(TOP_K):
                pltpu.make_async_copy(ys_hbm.at[pl.ds(pos_ref[0, 0, k * tm + r], 1)],
                                      ybuf.at[s, k, pl.ds(r, 1)], sem.at[s]).start(priority=k % 2)

    def wait(s):
        for k in range(TOP_K):
            pltpu.make_async_copy(ys_hbm.at[pl.ds(0, tm)], ybuf.at[s, k], sem.at[s]).wait()

    @pl.when(i == 0)
    def _():
        issue(posc_ref, 0)

    wait(slot)

    for s in range(2):
        @pl.when(slot == s)
        def _():
            issue(posn_ref, 1 - s)

    r = route_ref[...]
    g1 = r[:, ROUTE_G1:ROUTE_G1 + 1]
    g2 = r[:, ROUTE_G2:ROUTE_G2 + 1]
    xo = x2_ref[...] + g1 * _unpack_halves(ybuf[slot, 0]) + g2 * _unpack_halves(ybuf[slot, 1])
    o_ref[...] = _rms(xo, g_ref[...])

    @pl.when(i == n - 1)
    def _():
        wait(1 - slot)


def _final(x2, ys, pos, route, g):
    T = x2.shape[0]
    nblk = pos.shape[0]
    tm = T // nblk
    row = lambda w: pl.BlockSpec((tm, w), lambda i: (i, 0))
    return pl.pallas_call(
        _final_kernel,
        grid=(nblk,),
        in_specs=[pl.BlockSpec((1, 1, TOP_K * tm), lambda i: (i, 0, 0), memory_space=pltpu.SMEM),
                  pl.BlockSpec((1, 1, TOP_K * tm), lambda i: (jnp.minimum(i + 1, nblk - 1), 0, 0),
                               memory_space=pltpu.SMEM),
                  row(D_MODEL), row(LANES), pl.BlockSpec(g.shape, lambda i: (0, 0)),
                  pl.BlockSpec(memory_space=pl.ANY)],
        out_specs=row(D_MODEL),
        out_shape=jax.ShapeDtypeStruct((T, D_MODEL), F32),
        scratch_shapes=[pltpu.VMEM((2, TOP_K, tm, HALF), U32), pltpu.SemaphoreType.DMA((2,))],
        compiler_params=pltpu.CompilerParams(dimension_semantics=("arbitrary",), vmem_limit_bytes=VMEM_LIMIT),
        name="final",
    )(pos, pos, x2, route, g, ys)


def _band_bias(table):
    i = jnp.arange(BLOCK)[:, None]
    j = jnp.arange(2 * BLOCK)[None, :]
    n = jnp.maximum(i + BLOCK - j, 0)
    nf = jnp.maximum(n, 1).astype(F32)
    large = MAX_EXACT + (jnp.log(nf / MAX_EXACT) / math.log(MAX_DISTANCE / MAX_EXACT)
                         * (NUM_BUCKETS - MAX_EXACT)).astype(I32)
    large = jnp.minimum(large, NUM_BUCKETS - 1)
    bucket = jnp.where(n < MAX_EXACT, n, large)
    onehot = (bucket[:, :, None] == jnp.arange(NUM_BUCKETS)[None, None, :]).astype(F32)
    bias = jnp.einsum("ijb,bh->hij", onehot, table.astype(F32), precision=lax.Precision.HIGHEST)
    d = i + BLOCK - j
    band_ok = (d >= 0) & (d < WINDOW)
    first_ok = band_ok & (j >= BLOCK)
    return jnp.stack([jnp.where(first_ok[None], bias, NEG_INF), jnp.where(band_ok[None], bias, NEG_INF)])


def _dispatch_plan(route_rows, counts_f, tmx, n_tiles_max, tm_rows):
    T = route_rows.shape[1]
    experts = jnp.arange(N_EXPERTS, dtype=I32)
    counts = counts_f[0, :N_EXPERTS].astype(I32)
    ptiles = (counts + tmx - 1) // tmx
    tile_end = jnp.cumsum(ptiles)
    nt = tile_end[-1]
    row_off = (tile_end - ptiles) * tmx

    def slot(e_row, r_row):
        e = route_rows[e_row].astype(I32)
        off = jnp.sum(jnp.where(e[None, :] == experts[:, None], row_off[:, None], 0), axis=0)
        return (off + route_rows[r_row].astype(I32)).reshape(T // tm_rows, 1, tm_rows)

    pos = jnp.concatenate([slot(ROUTE_E1, ROUTE_R1), slot(ROUTE_E2, ROUTE_R2)], axis=2)

    tile_ids = jnp.arange(n_tiles_max, dtype=I32)
    expert_of = lambda t: jnp.sum((tile_end[None, :] <= t[:, None]).astype(I32), axis=1)
    te = expert_of(jnp.minimum(tile_ids, nt - 1))
    partial = jnp.any((tile_ids[:, None] == (tile_end - 1)[None, :]) & (counts % tmx != 0)[None, :], axis=1)
    zfill = (partial | (tile_ids >= nt)).astype(I32)

    used = ptiles > 0
    run_first = (jnp.any((tile_ids[:, None] == (tile_end - ptiles)[None, :]) & used[None, :], axis=1)
                 & (tile_ids < nt)).astype(I32)
    run_slot = (jnp.cumsum(run_first) - 1) % 2
    later_used = used[None, :] & (experts[None, :] > experts[:, None])
    next_of = jnp.min(jnp.where(later_used, experts[None, :], N_EXPERTS), axis=1)
    next_of = jnp.where(next_of < N_EXPERTS, next_of, -1)
    run_next = jnp.sum(jnp.where(te[:, None] == experts[None, :], next_of[None, :], 0), axis=1)
    return pos, te, nt.reshape(1), zfill, run_first, run_slot.astype(I32), run_next.astype(I32)


def kernel(x, mem, rel_bias_table, norm_mix, w_in, attn_sinks, conv_w, conv_b, gate_bias_i, gate_bias_f, mlstm_norm, w_out, norm_cross, norm_mem, w_cq, w_ck, w_cv, w_co, norm_moe, w_router_group, b_router_group, w_router_expert, b_router_expert, w_exp_gate, w_exp_up, w_exp_down, norm_final):
    B, S, _ = x.shape
    T = B * S
    depth = w_in.shape[0]
    x2d = x.reshape(T, D_MODEL)
    mem2d = mem.reshape(B * N_MEM, D_MODEL)
    bias = _band_bias(rel_bias_table)

    tmx = min(TM_EXPERT, T)
    n_tiles_max = (T * TOP_K) // tmx + N_EXPERTS
    tm_rows = min(TM_ROWDMA, T)

    assert depth == 1, "the final combine is fused with the final norm: single layer only"
    l = 0
    w_pad = jnp.pad(w_in[l], ((0, 0), (0, C_GATE + LANES - D_IN))).astype(BF16)
    gb = jnp.concatenate([gate_bias_i[l], gate_bias_f[l]]).astype(F32)
    gbias_col = jnp.pad(gb, (0, LANES - GATE_ROWS))[None, :]
    qa, kva, qkb, vb, ob, gc, gr = _inproj(x2d, norm_mix[l][None, :], w_pad, gbias_col, B, S)

    out_a = _swa(qa, kva, bias, attn_sinks[l].astype(F32), B, S)
    per_seq = lambda a: a.reshape(B, S, a.shape[-1])
    hb = _mlstm(per_seq(qkb), per_seq(vb), per_seq(ob), per_seq(gc), gr, conv_w[l][:, 0, :].astype(F32),
                conv_b[l][None, :].astype(F32), mlstm_norm[l][None, :].astype(F32)).reshape(T, W_B)

    ck, cv = _memkv(mem2d, norm_mem[l][None, :], w_ck[l].astype(BF16), w_cv[l].astype(BF16), B)

    wr = jnp.pad(jnp.concatenate([w_router_expert[l], w_router_group[l]], axis=1),
                 ((0, 0), (0, LANES - N_EXPERTS - N_GROUPS))).astype(BF16)
    br = jnp.pad(jnp.concatenate([b_router_expert[l], b_router_group[l]]),
                 (0, LANES - N_EXPERTS - N_GROUPS)).astype(F32)[None, :]
    x2, hz_packed, route, route_rows, counts = _mid(
        x2d, out_a, hb, w_out[l].astype(BF16), norm_cross[l][None, :], w_cq[l].astype(BF16), ck, cv,
        w_co[l].astype(BF16), norm_moe[l][None, :], wr, br, B, S)

    pos, te, nt, zfill, run_first, run_slot, run_next = _dispatch_plan(route_rows, counts, tmx, n_tiles_max,
                                                                       tm_rows)
    xs = _dispatch(hz_packed, pos, zfill, n_tiles_max * tmx, tmx)
    ys = _experts(xs, w_exp_gate[l], w_exp_up[l], w_exp_down[l], te, nt, run_first, run_slot, run_next, tmx)
    out = _final(x2, ys, pos, route, norm_final[None, :])
    return out.reshape(B, S, D_MODEL)
```

```python
import functools
import math

import jax
import jax.numpy as jnp
from jax import lax
from jax.experimental import pallas as pl
from jax.experimental.pallas import tpu as pltpu

F32 = jnp.float32
BF16 = jnp.bfloat16
U32 = jnp.uint32
I32 = jnp.int32

D_MODEL = 1024
N_MEM = 256
N_HEADS_A = 8
N_KV_A = 2
HEAD_DIM_A = 64
BLOCK = 128
WINDOW = 128
NUM_BUCKETS = 32
MAX_EXACT = NUM_BUCKETS // 2
MAX_DISTANCE = 128
N_HEADS_B = 4
HEAD_DIM_B = 128
CHUNK = 128
CONV_WIDTH = 4
N_HEADS_X = 4
HEAD_DIM_X = D_MODEL // N_HEADS_X
N_GROUPS = 4
EXPERTS_PER_GROUP = 8
N_EXPERTS = N_GROUPS * EXPERTS_PER_GROUP
TOP_K = 2
D_EXPERT = 512
EPS = 1e-6
NEG_INF = -1e30

W_A_Q = N_HEADS_A * HEAD_DIM_A
W_A_KV = N_KV_A * HEAD_DIM_A
W_B = N_HEADS_B * HEAD_DIM_B
C_QA = 0
C_KVA = C_QA + W_A_Q
C_QKB = C_KVA + 2 * W_A_KV
C_VB = C_QKB + 2 * W_B
C_OB = C_VB + W_B
C_GATE = C_OB + W_B
D_IN = C_GATE + 2 * N_HEADS_B

LANES = 128
SUBLANES = 8
GATE_ROWS = 8
HALF = D_MODEL // 2

TM_INPROJ = 512
TM_MID = 512
TM_ROWDMA = 512
TM_EXPERT = 256

VMEM_LIMIT = 48 * 1024 * 1024


def _rms(xf, g):
    return xf * lax.rsqrt(jnp.mean(xf * xf, axis=-1, keepdims=True) + EPS) * g


def _pack_halves(v):
    b = pltpu.bitcast(v.astype(BF16).astype(F32), U32)
    return (b[:, :HALF] >> 16) | b[:, HALF:]


def _unpack_halves(p):
    lo = pltpu.bitcast(p << 16, F32)
    hi = pltpu.bitcast(p & jnp.uint32(0xFFFF0000), F32)
    return jnp.concatenate([lo, hi], axis=-1)


def _log_sigmoid(z):
    return jnp.minimum(z, 0.0) - jnp.log1p(jnp.exp(-jnp.abs(z)))


def _inproj_kernel(x_ref, g_ref, w_ref, gbc_ref, qa_ref, kva_ref, qkb_ref, vb_ref, ob_ref, gc_ref, gr_ref):
    tm = x_ref.shape[0]
    h = _rms(x_ref[...], g_ref[...]).astype(BF16)

    def mm(lo, hi):
        return jnp.dot(h, w_ref[:, lo:hi], preferred_element_type=F32)

    qa_ref[...] = mm(C_QA, C_KVA).astype(BF16)
    kva_ref[...] = mm(C_KVA, C_QKB).astype(BF16)
    qkb_ref[...] = mm(C_QKB, C_VB).astype(BF16)
    vb_ref[...] = mm(C_VB, C_OB).astype(BF16)
    ob_ref[...] = mm(C_OB, C_GATE).astype(BF16)

    H, L = N_HEADS_B, CHUNK
    gcol = mm(C_GATE, C_GATE + LANES) + gbc_ref[...]
    grow = jnp.transpose(gcol)[0:GATE_ROWS, :]
    lane_c = lax.broadcasted_iota(I32, (L, LANES), 1)
    is_f_col = (lane_c >= H) & (lane_c < 2 * H)
    is_f_row = lax.broadcasted_iota(I32, (GATE_ROWS, L), 0) >= H
    ti = lax.broadcasted_iota(I32, (L, L), 0)
    si = lax.broadcasted_iota(I32, (L, L), 1)
    tril = jnp.where(si <= ti, 1.0, 0.0).astype(F32)
    triu = jnp.where(si >= ti, 1.0, 0.0).astype(F32)
    for c in range(tm // L):
        rows = slice(c * L, (c + 1) * L)
        gcol_c = gcol[rows, :]
        fcol = jnp.where(is_f_col, _log_sigmoid(gcol_c), 0.0)
        bcol = jnp.dot(tril, fcol, preferred_element_type=F32, precision=lax.Precision.HIGHEST)
        gc_ref[rows, :] = jnp.where(is_f_col, bcol, gcol_c)
        grow_c = grow[:, rows]
        frow = jnp.where(is_f_row, _log_sigmoid(grow_c), 0.0)
        brow = jnp.dot(frow, triu, preferred_element_type=F32, precision=lax.Precision.HIGHEST)
        gr_ref[:, rows] = jnp.where(is_f_row, brow, grow_c)


def _inproj(x2d, g, w_pad, gbias_col, B, S):
    T = x2d.shape[0]
    tm = min(TM_INPROJ, S)
    tiles_per_seq = S // tm
    row = lambda w: pl.BlockSpec((tm, w), lambda i: (i, 0))
    full = lambda a: pl.BlockSpec(a.shape, lambda i: (0,) * a.ndim)
    return pl.pallas_call(
        _inproj_kernel,
        grid=(T // tm,),
        in_specs=[row(D_MODEL), full(g), full(w_pad), full(gbias_col)],
        out_specs=[row(W_A_Q), row(2 * W_A_KV), row(2 * W_B), row(W_B), row(W_B), row(LANES),
                   pl.BlockSpec((None, GATE_ROWS, tm), lambda i: (i // tiles_per_seq, 0, i % tiles_per_seq))],
        out_shape=[jax.ShapeDtypeStruct((T, W_A_Q), BF16),
                   jax.ShapeDtypeStruct((T, 2 * W_A_KV), BF16),
                   jax.ShapeDtypeStruct((T, 2 * W_B), BF16),
                   jax.ShapeDtypeStruct((T, W_B), BF16),
                   jax.ShapeDtypeStruct((T, W_B), BF16),
                   jax.ShapeDtypeStruct((T, LANES), F32),
                   jax.ShapeDtypeStruct((B, GATE_ROWS, S), F32)],
        compiler_params=pltpu.CompilerParams(dimension_semantics=("parallel",), vmem_limit_bytes=VMEM_LIMIT),
        name="inproj",
    )(x2d, g, w_pad, gbias_col)


def _swa_block(q, kvc, kvp, bias_ref, sink_ref):
    kvp = kvp.astype(F32)
    kvc = kvc.astype(F32)
    kband = jnp.concatenate([kvp[:, :W_A_KV], kvc[:, :W_A_KV]], axis=0)
    vband = jnp.concatenate([kvp[:, W_A_KV:], kvc[:, W_A_KV:]], axis=0)
    lane = lax.broadcasted_iota(I32, (2 * BLOCK, LANES), 1)
    lo = lane < HEAD_DIM_A

    def placements(band):
        swapped = pltpu.roll(band, HEAD_DIM_A, axis=1)
        z = jnp.zeros_like(band)
        return {(0, 0): jnp.where(lo, band, z).astype(BF16), (0, 1): jnp.where(lo, z, swapped).astype(BF16),
                (1, 0): jnp.where(lo, swapped, z).astype(BF16), (1, 1): jnp.where(lo, z, band).astype(BF16)}

    kpl = placements(kband)
    vpl = placements(vband)

    scale = HEAD_DIM_A ** -0.5
    group = N_HEADS_A // N_KV_A
    tiles = []

    for pair in range(N_HEADS_A // 2):
        qt = q[:, pair * LANES:(pair + 1) * LANES]
        acc = None
        for half in range(2):
            h = 2 * pair + half
            g = h // group
            s = lax.dot_general(qt, kpl[(g, half)], (((1,), (1,)), ((), ())), preferred_element_type=F32)
            s = s * scale + bias_ref[h]
            sink = sink_ref[h]
            m = jnp.maximum(jnp.max(s, axis=-1, keepdims=True), sink)
            p = jnp.exp(s - m)
            denom = jnp.sum(p, axis=-1, keepdims=True) + jnp.exp(sink - m)
            o = jnp.dot(p.astype(BF16), vpl[(g, half)], preferred_element_type=F32) / denom
            acc = o if acc is None else acc + o
        tiles.append(acc.astype(BF16))
    return tiles


CONV_HALO = 16


def _seqmix_kernel(sink_ref, qkc_ref, qkp_ref, vb_ref, ob_ref, gc_ref, gr_ref, qa_ref, kvc_ref, kvp_ref,
                   cw_ref, cb_ref, nrm_ref, bias_ref, o_ref, state_ref, m_ref):
    c = pl.program_id(0)
    B = qkc_ref.shape[0]
    H, D, L = N_HEADS_B, HEAD_DIM_B, CHUNK

    @pl.when(c == 0)
    def _():
        state_ref[...] = jnp.zeros_like(state_ref)
        m_ref[...] = jnp.zeros_like(m_ref)

    rr = lax.broadcasted_iota(I32, (L, CONV_HALO + L), 0)
    cc = lax.broadcasted_iota(I32, (L, CONV_HALO + L), 1)
    shifts = {delay: jnp.where(cc == rr + (CONV_HALO - delay), 1.0, 0.0).astype(BF16)
              for delay in range(1, CONV_WIDTH)}
    ti = lax.broadcasted_iota(I32, (L, L), 0)
    si = lax.broadcasted_iota(I32, (L, L), 1)
    tri = si <= ti
    ones_blk = jnp.ones((L, D), BF16)

    def conv_silu(b):
        prev = qkp_ref[b]
        prev = jnp.where(c > 0, prev, jnp.zeros_like(prev))
        cur = qkc_ref[b]
        ext = jnp.concatenate([prev, cur], axis=0)
        y = cb_ref[...] + cw_ref[CONV_WIDTH - 1:CONV_WIDTH, :] * cur.astype(F32)
        for delay in range(1, CONV_WIDTH):
            tap = CONV_WIDTH - 1 - delay
            y = y + cw_ref[tap:tap + 1, :] * jnp.dot(shifts[delay], ext, preferred_element_type=F32)
        return y * jax.nn.sigmoid(y)

    states = [[state_ref[b, h] for h in range(H)] for b in range(B)]
    m_alls = [m_ref[b] for b in range(B)]
    new_states, new_m, outs = {}, {}, {}

    for b, h in [(b, h) for b in range(B) for h in range(H)]:
        if h == 0:
            for pair, tile in enumerate(_swa_block(qa_ref[b], kvc_ref[b], kvp_ref[b], bias_ref, sink_ref)):
                o_ref[b, :, pair * LANES:(pair + 1) * LANES] = tile
            qk = conv_silu(b)
            gcol = gc_ref[b]
            grow = gr_ref[b]
        qh = (qk[:, h * D:(h + 1) * D] * (D ** -0.5)).astype(BF16)
        k_t = qk[:, W_B + h * D:W_B + (h + 1) * D].T
        v1 = jnp.concatenate([vb_ref[b, :, h * D:(h + 1) * D], ones_blk], axis=-1)
        b_r = grow[H + h:H + h + 1, :]
        g_r = grow[h:h + 1, :] - b_r
        b_c = gcol[:, H + h:H + h + 1]
        m_prev = m_alls[b][h:h + 1, 0:1]
        state = states[b][h]

        gmat = jnp.where(tri, g_r, NEG_INF)
        m_c = jnp.maximum(jnp.max(gmat, axis=-1, keepdims=True), m_prev)
        a_inter = jnp.exp(m_prev - m_c)
        sc = jnp.dot(qh, k_t.astype(BF16), preferred_element_type=F32) * jnp.exp(gmat - m_c)
        tot = (jnp.dot(sc.astype(BF16), v1, preferred_element_type=F32)
               + a_inter * jnp.dot(qh, state.astype(BF16), preferred_element_type=F32))
        num = tot[:, :D]
        den = tot[:, D:]
        hh = num / jnp.maximum(jnp.abs(den), jnp.exp(-(b_c + m_c)))

        b_last = b_r[:, L - 1:L]
        m_new = jnp.maximum(b_last + m_prev, b_last + jnp.max(g_r, axis=-1, keepdims=True))
        w_r = jnp.exp(g_r + (b_last - m_new))
        decay = jnp.exp(b_last + m_prev - m_new)
        upd = jnp.dot((k_t * w_r).astype(BF16), v1, preferred_element_type=F32)
        new_states[b, h] = decay * state + upd
        new_m[b, h] = jnp.broadcast_to(m_new, (1, LANES))

        og = jax.nn.sigmoid(ob_ref[b, :, h * D:(h + 1) * D].astype(F32))
        hb = og * hh
        hb = hb * lax.rsqrt(jnp.mean(hb * hb, axis=-1, keepdims=True) + EPS)
        outs[b, h] = (hb * nrm_ref[:, h * D:(h + 1) * D]).astype(BF16)

    for b, h in new_states:
        state_ref[b, h] = new_states[b, h]
        m_ref[b, h:h + 1, :] = new_m[b, h]
        o_ref[b, :, W_A_Q + h * D:W_A_Q + (h + 1) * D] = outs[b, h]


def _seqmix(qkb, vb, ob, gc, gr, qa, kva, conv_w, conv_b, nrm, bias, sinks):
    assert CHUNK == BLOCK
    B, S, _ = qkb.shape
    nc = S // CHUNK
    halo_per_chunk = CHUNK // CONV_HALO
    blk = lambda w: pl.BlockSpec((B, CHUNK, w), lambda c: (0, c, 0))
    full = lambda a: pl.BlockSpec(a.shape, lambda c: (0,) * a.ndim)
    return pl.pallas_call(
        _seqmix_kernel,
        grid=(nc,),
        in_specs=[pl.BlockSpec(memory_space=pltpu.SMEM),
                  blk(2 * W_B),
                  pl.BlockSpec((B, CONV_HALO, 2 * W_B), lambda c: (0, jnp.maximum(c * halo_per_chunk - 1, 0), 0)),
                  blk(W_B), blk(W_B), blk(LANES),
                  pl.BlockSpec((B, GATE_ROWS, CHUNK), lambda c: (0, 0, c)),
                  blk(W_A_Q), blk(2 * W_A_KV),
                  pl.BlockSpec((B, BLOCK, 2 * W_A_KV), lambda c: (0, jnp.maximum(c - 1, 0), 0)),
                  full(conv_w), full(conv_b), full(nrm),
                  pl.BlockSpec((None,) + bias.shape[1:], lambda c: (jnp.minimum(c, 1), 0, 0, 0))],
        out_specs=blk(W_A_Q + W_B),
        out_shape=jax.ShapeDtypeStruct((B, S, W_A_Q + W_B), BF16),
        scratch_shapes=[pltpu.VMEM((B, N_HEADS_B, HEAD_DIM_B, 2 * HEAD_DIM_B), F32),
                        pltpu.VMEM((B, GATE_ROWS, LANES), F32)],
        compiler_params=pltpu.CompilerParams(dimension_semantics=("arbitrary",), vmem_limit_bytes=VMEM_LIMIT),
        name="seqmix",
    )(sinks, qkb, qkb, vb, ob, gc, gr, qa, kva, kva, conv_w, conv_b, nrm, bias)


def _memkv_kernel(mem_ref, g_ref, wk_ref, wv_ref, k_ref, v_ref):
    hm = _rms(mem_ref[...], g_ref[...]).astype(BF16)
    k_ref[...] = jnp.dot(hm, wk_ref[...], preferred_element_type=F32).astype(BF16)
    v_ref[...] = jnp.dot(hm, wv_ref[...], preferred_element_type=F32).astype(BF16)


def _memkv(mem2d, g, wk, wv, B):
    full = lambda a: pl.BlockSpec(a.shape, lambda b: (0,) * a.ndim)
    blk = pl.BlockSpec((N_MEM, D_MODEL), lambda b: (b, 0))
    return pl.pallas_call(
        _memkv_kernel,
        grid=(B,),
        in_specs=[blk, full(g), full(wk), full(wv)],
        out_specs=[blk, blk],
        out_shape=[jax.ShapeDtypeStruct((B * N_MEM, D_MODEL), BF16)] * 2,
        compiler_params=pltpu.CompilerParams(dimension_semantics=("parallel",), vmem_limit_bytes=VMEM_LIMIT),
        name="memkv",
    )(mem2d, g, wk, wv)


ROUTE_E1, ROUTE_E2, ROUTE_G1, ROUTE_G2, ROUTE_R1, ROUTE_R2 = 0, 1, 2, 3, 4, 5
ROUTER_GROUP_COL = N_EXPERTS


def _mid_kernel(x_ref, mix_ref, wo_ref, gx_ref, wq_ref, ck_ref, cv_ref, wco_ref, gz_ref, wr_ref, br_ref,
                x2_ref, hz_ref, route_ref, rrows_ref, counts_ref, cnt_ref):
    @pl.when(pl.program_id(0) == 0)
    def _():
        cnt_ref[...] = jnp.zeros_like(cnt_ref)

    x1 = x_ref[...] + jnp.dot(mix_ref[...], wo_ref[...], preferred_element_type=F32)

    hc = _rms(x1, gx_ref[...]).astype(BF16)
    cq = jnp.dot(hc, wq_ref[...], preferred_element_type=F32).astype(BF16)
    scale = HEAD_DIM_X ** -0.5
    heads = []
    for h in range(N_HEADS_X):
        sl = slice(h * HEAD_DIM_X, (h + 1) * HEAD_DIM_X)
        s = lax.dot_general(cq[:, sl], ck_ref[:, sl], (((1,), (1,)), ((), ())), preferred_element_type=F32) * scale
        p = jnp.exp(s - jnp.max(s, axis=-1, keepdims=True))
        co = jnp.dot(p.astype(BF16), cv_ref[:, sl], preferred_element_type=F32) / jnp.sum(p, axis=-1, keepdims=True)
        heads.append(co.astype(BF16))
    x2 = x1 + jnp.dot(jnp.concatenate(heads, axis=-1), wco_ref[...], preferred_element_type=F32)
    x2_ref[...] = x2

    hz = _rms(x2, gz_ref[...])
    hz_ref[...] = _pack_halves(hz)
    lg = jnp.dot(hz.astype(BF16), wr_ref[...], preferred_element_type=F32) + br_ref[...]
    col = lax.broadcasted_iota(I32, lg.shape, 1)
    big = jnp.int32(LANES)
    is_g = (col >= ROUTER_GROUP_COL) & (col < ROUTER_GROUP_COL + N_GROUPS)
    gl = jnp.where(is_g, lg, NEG_INF)
    gmax = jnp.max(gl, axis=-1, keepdims=True)
    gsum = jnp.sum(jnp.exp(gl - gmax), axis=-1, keepdims=True)
    g_prob = 1.0 / gsum
    g_idx = jnp.min(jnp.where(gl == gmax, col - ROUTER_GROUP_COL, big), axis=-1, keepdims=True)
    sel = (col < N_EXPERTS) & ((col // EXPERTS_PER_GROUP) == g_idx)
    el = jnp.where(sel, lg, NEG_INF)
    m1 = jnp.max(el, axis=-1, keepdims=True)
    i1 = jnp.min(jnp.where(el == m1, col, big), axis=-1, keepdims=True)
    el2 = jnp.where(col == i1, NEG_INF, el)
    m2 = jnp.max(el2, axis=-1, keepdims=True)
    i2 = jnp.min(jnp.where(el2 == m2, col, big), axis=-1, keepdims=True)
    z = jnp.sum(jnp.exp(el - m1), axis=-1, keepdims=True)
    p1 = 1.0 / z
    p2 = jnp.exp(m2 - m1) / z
    g1 = g_prob * (p1 / (p1 + p2))
    g2 = g_prob * (p2 / (p1 + p2))

    tm = lg.shape[0]
    used = jnp.where((col == i1) | (col == i2), 1.0, 0.0)
    tt = lax.broadcasted_iota(I32, (tm, tm), 0)
    ss = lax.broadcasted_iota(I32, (tm, tm), 1)
    earlier = jnp.where(ss < tt, 1.0, 0.0).astype(BF16)
    before = jnp.dot(earlier, used.astype(BF16), preferred_element_type=F32) + cnt_ref[...]
    r1 = jnp.sum(jnp.where(col == i1, before, 0.0), axis=-1, keepdims=True)
    r2 = jnp.sum(jnp.where(col == i2, before, 0.0), axis=-1, keepdims=True)
    cnt_ref[...] = cnt_ref[...] + jnp.sum(used, axis=0, keepdims=True)
    counts_ref[...] = jnp.broadcast_to(cnt_ref[...], counts_ref.shape)

    route = jnp.zeros_like(lg)
    for c, v in ((ROUTE_E1, i1.astype(F32)), (ROUTE_E2, i2.astype(F32)), (ROUTE_G1, g1), (ROUTE_G2, g2),
                 (ROUTE_R1, r1), (ROUTE_R2, r2)):
        route = jnp.where(col == c, v, route)
    route_ref[...] = route
    rrows_ref[...] = jnp.transpose(route)[0:SUBLANES, :]


def _mid(x2d, mix, wo, gx, wq, ck, cv, wco, gz, wr, br, B, S):
    T = B * S
    tm = min(TM_MID, S)
    per_b = S // tm
    row = lambda w: pl.BlockSpec((tm, w), lambda i: (i, 0))
    full = lambda a: pl.BlockSpec(a.shape, lambda i: (0,) * a.ndim)
    kvspec = pl.BlockSpec((N_MEM, D_MODEL), lambda i: (i // per_b, 0))
    return pl.pallas_call(
        _mid_kernel,
        grid=(T // tm,),
        in_specs=[row(D_MODEL), row(W_A_Q + W_B), full(wo), full(gx), full(wq), kvspec, kvspec,
                  full(wco), full(gz), full(wr), full(br)],
        out_specs=[row(D_MODEL), row(HALF), row(LANES), pl.BlockSpec((SUBLANES, tm), lambda i: (0, i)),
                   pl.BlockSpec((SUBLANES, LANES), lambda i: (0, 0))],
        out_shape=[jax.ShapeDtypeStruct((T, D_MODEL), F32),
                   jax.ShapeDtypeStruct((T, HALF), U32),
                   jax.ShapeDtypeStruct((T, LANES), F32),
                   jax.ShapeDtypeStruct((SUBLANES, T), F32),
                   jax.ShapeDtypeStruct((SUBLANES, LANES), F32)],
        scratch_shapes=[pltpu.VMEM((1, LANES), F32)],
        compiler_params=pltpu.CompilerParams(dimension_semantics=("arbitrary",), vmem_limit_bytes=VMEM_LIMIT),
        name="mid",
    )(x2d, mix, wo, gx, wq, ck, cv, wco, gz, wr, br)


def _dispatch_kernel(zf_ref, pos_ref, hz_ref, xs_hbm, zbuf, sem, zsem):
    i = pl.program_id(0)
    tm = hz_ref.shape[0]
    zrows = zbuf.shape[0]

    @pl.when(i == 0)
    def _():
        zbuf[...] = jnp.zeros_like(zbuf)

        def fill(t):
            return pltpu.make_async_copy(zbuf, xs_hbm.at[pl.ds(t * zrows, zrows)], zsem)

        def start(t, carry):
            @pl.when(zf_ref[t] != 0)
            def _():
                fill(t).start()
            return carry

        def wait(t, carry):
            @pl.when(zf_ref[t] != 0)
            def _():
                fill(t).wait()
            return carry

        lax.fori_loop(0, zf_ref.shape[0], start, 0)
        lax.fori_loop(0, zf_ref.shape[0], wait, 0)

    for r in range(tm):
        for k in range(TOP_K):
            pltpu.make_async_copy(hz_ref.at[pl.ds(r, 1)], xs_hbm.at[pl.ds(pos_ref[0, 0, k * tm + r], 1)],
                                  sem).start(priority=k % 2)
    for k in range(TOP_K):
        pltpu.make_async_copy(hz_ref, xs_hbm.at[pl.ds(0, tm)], sem).wait()


def _dispatch(hz_packed, pos, zfill, n_slots, tmx):
    T = hz_packed.shape[0]
    tm = pos.shape[2] // TOP_K
    grid_spec = pltpu.PrefetchScalarGridSpec(
        num_scalar_prefetch=1,
        grid=(T // tm,),
        in_specs=[pl.BlockSpec((1, 1, TOP_K * tm), lambda i, zf: (i, 0, 0), memory_space=pltpu.SMEM),
                  pl.BlockSpec((tm, HALF), lambda i, zf: (i, 0))],
        out_specs=pl.BlockSpec(memory_space=pl.ANY),
        scratch_shapes=[pltpu.VMEM((tmx, HALF), U32), pltpu.SemaphoreType.DMA(()), pltpu.SemaphoreType.DMA(())],
    )
    return pl.pallas_call(
        _dispatch_kernel,
        grid_spec=grid_spec,
        out_shape=jax.ShapeDtypeStruct((n_slots, HALF), U32),
        compiler_params=pltpu.CompilerParams(dimension_semantics=("arbitrary",), vmem_limit_bytes=VMEM_LIMIT),
        name="dispatch",
    )(zfill, pos, hz_packed)


def _expert_kernel(te_ref, nt_ref, first_ref, slot_ref, next_ref, xs_ref, wg_hbm, wu_hbm, wd_hbm, ys_ref,
                   wg32, wu32, wd32, wgb, wub, wdb, wsem):
    i = pl.program_id(0)
    nt = nt_ref[0]

    def fetch(e, s):
        return [pltpu.make_async_copy(src.at[e], dst.at[s], wsem.at[s])
                for src, dst in ((wg_hbm, wg32), (wu_hbm, wu32), (wd_hbm, wd32))]

    @pl.when(i < nt)
    def _():
        @pl.when(i == 0)
        def _():
            for cp in fetch(te_ref[0], 0):
                cp.start()

        @pl.when(first_ref[i] != 0)
        def _():
            s = slot_ref[i]
            for cp in fetch(te_ref[i], s):
                cp.wait()

            @pl.when(next_ref[i] >= 0)
            def _():
                for cp in fetch(next_ref[i], 1 - s):
                    cp.start()

            wgb[...] = wg32[s].astype(BF16)
            wub[...] = wu32[s].astype(BF16)
            wdb[...] = wd32[s].astype(BF16)

        x = _unpack_halves(xs_ref[...]).astype(BF16)
        hg = jnp.dot(x, wgb[...], preferred_element_type=F32)
        hu = jnp.dot(x, wub[...], preferred_element_type=F32)
        a = (hg * jax.nn.sigmoid(hg) * hu).astype(BF16)
        ys_ref[...] = _pack_halves(jnp.dot(a, wdb[...], preferred_element_type=F32))

    @pl.when(i >= nt)
    def _():
        ys_ref[...] = jnp.zeros_like(ys_ref)


def _experts(xs, w_gate, w_up, w_down, tile_expert, ntiles, run_first, run_slot, run_next, tmx):
    n_tiles_max = tile_expert.shape[0]
    hbm = pl.BlockSpec(memory_space=pl.ANY)
    grid_spec = pltpu.PrefetchScalarGridSpec(
        num_scalar_prefetch=5,
        grid=(n_tiles_max,),
        in_specs=[pl.BlockSpec((tmx, HALF), lambda i, te, nt, *_: (jnp.minimum(i, nt[0] - 1), 0)), hbm, hbm, hbm],
        out_specs=pl.BlockSpec((tmx, HALF), lambda i, *_: (i, 0)),
        scratch_shapes=[pltpu.VMEM((2, D_MODEL, D_EXPERT), F32),
                        pltpu.VMEM((2, D_MODEL, D_EXPERT), F32),
                        pltpu.VMEM((2, D_EXPERT, D_MODEL), F32),
                        pltpu.VMEM((D_MODEL, D_EXPERT), BF16),
                        pltpu.VMEM((D_MODEL, D_EXPERT), BF16),
                        pltpu.VMEM((D_EXPERT, D_MODEL), BF16),
                        pltpu.SemaphoreType.DMA((2,))],
    )
    return pl.pallas_call(
        _expert_kernel,
        grid_spec=grid_spec,
        out_shape=jax.ShapeDtypeStruct(xs.shape, U32),
        compiler_params=pltpu.CompilerParams(dimension_semantics=("arbitrary",), vmem_limit_bytes=VMEM_LIMIT),
        name="experts",
    )(tile_expert, ntiles, run_first, run_slot, run_next, xs, w_gate, w_up, w_down)


def _final_kernel(posc_ref, posn_ref, x2_ref, route_ref, g_ref, ys_hbm, o_ref, ybuf, sem):
    i = pl.program_id(0)
    n = pl.num_programs(0)
    tm = x2_ref.shape[0]
    slot = i % 2

    def issue(pos_ref, s):
        for r in range(tm):
            for k in range(TOP_K):
                pltpu.make_async_copy(ys_hbm.at[pl.ds(pos_ref[0, 0, k * tm + r], 1)],
                                      ybuf.at[s, k, pl.ds(r, 1)], sem.at[s]).start(priority=k % 2)

    def wait(s):
        for k in range(TOP_K):
            pltpu.make_async_copy(ys_hbm.at[pl.ds(0, tm)], ybuf.at[s, k], sem.at[s]).wait()

    @pl.when(i == 0)
    def _():
        issue(posc_ref, 0)

    wait(slot)

    for s in range(2):
        @pl.when(slot == s)
        def _():
            issue(posn_ref, 1 - s)

    r = route_ref[...]
    g1 = r[:, ROUTE_G1:ROUTE_G1 + 1]
    g2 = r[:, ROUTE_G2:ROUTE_G2 + 1]
    xo = x2_ref[...] + g1 * _unpack_halves(ybuf[slot, 0]) + g2 * _unpack_halves(ybuf[slot, 1])
    o_ref[...] = _rms(xo, g_ref[...])

    @pl.when(i == n - 1)
    def _():
        wait(1 - slot)


def _final(x2, ys, pos, route, g):
    T = x2.shape[0]
    nblk = pos.shape[0]
    tm = T // nblk
    row = lambda w: pl.BlockSpec((tm, w), lambda i: (i, 0))
    return pl.pallas_call(
        _final_kernel,
        grid=(nblk,),
        in_specs=[pl.BlockSpec((1, 1, TOP_K * tm), lambda i: (i, 0, 0), memory_space=pltpu.SMEM),
                  pl.BlockSpec((1, 1, TOP_K * tm), lambda i: (jnp.minimum(i + 1, nblk - 1), 0, 0),
                               memory_space=pltpu.SMEM),
                  row(D_MODEL), row(LANES), pl.BlockSpec(g.shape, lambda i: (0, 0)),
                  pl.BlockSpec(memory_space=pl.ANY)],
        out_specs=row(D_MODEL),
        out_shape=jax.ShapeDtypeStruct((T, D_MODEL), F32),
        scratch_shapes=[pltpu.VMEM((2, TOP_K, tm, HALF), U32), pltpu.SemaphoreType.DMA((2,))],
        compiler_params=pltpu.CompilerParams(dimension_semantics=("arbitrary",), vmem_limit_bytes=VMEM_LIMIT),
        name="final",
    )(pos, pos, x2, route, g, ys)


def _band_bias(table):
    i = jnp.arange(BLOCK)[:, None]
    j = jnp.arange(2 * BLOCK)[None, :]
    n = jnp.maximum(i + BLOCK - j, 0)
    nf = jnp.maximum(n, 1).astype(F32)
    large = MAX_EXACT + (jnp.log(nf / MAX_EXACT) / math.log(MAX_DISTANCE / MAX_EXACT)
                         * (NUM_BUCKETS - MAX_EXACT)).astype(I32)
    large = jnp.minimum(large, NUM_BUCKETS - 1)
    bucket = jnp.where(n < MAX_EXACT, n, large)
    onehot = (bucket[:, :, None] == jnp.arange(NUM_BUCKETS)[None, None, :]).astype(F32)
    bias = jnp.einsum("ijb,bh->hij", onehot, table.astype(F32), precision=lax.Precision.HIGHEST)
    d = i + BLOCK - j
    band_ok = (d >= 0) & (d < WINDOW)
    first_ok = band_ok & (j >= BLOCK)
    return jnp.stack([jnp.where(first_ok[None], bias, NEG_INF), jnp.where(band_ok[None], bias, NEG_INF)])


def _dispatch_plan(route_rows, counts_f, tmx, n_tiles_max, tm_rows):
    T = route_rows.shape[1]
    experts = jnp.arange(N_EXPERTS, dtype=I32)
    counts = counts_f[0, :N_EXPERTS].astype(I32)
    ptiles = (counts + tmx - 1) // tmx
    tile_end = jnp.cumsum(ptiles)
    nt = tile_end[-1]
    row_off = (tile_end - ptiles) * tmx

    def slot(e_row, r_row):
        e = route_rows[e_row].astype(I32)
        off = jnp.sum(jnp.where(e[None, :] == experts[:, None], row_off[:, None], 0), axis=0)
        return (off + route_rows[r_row].astype(I32)).reshape(T // tm_rows, 1, tm_rows)

    pos = jnp.concatenate([slot(ROUTE_E1, ROUTE_R1), slot(ROUTE_E2, ROUTE_R2)], axis=2)

    tile_ids = jnp.arange(n_tiles_max, dtype=I32)
    expert_of = lambda t: jnp.sum((tile_end[None, :] <= t[:, None]).astype(I32), axis=1)
    te = expert_of(jnp.minimum(tile_ids, nt - 1))
    partial = jnp.any((tile_ids[:, None] == (tile_end - 1)[None, :]) & (counts % tmx != 0)[None, :], axis=1)
    zfill = (partial | (tile_ids >= nt)).astype(I32)

    used = ptiles > 0
    run_first = (jnp.any((tile_ids[:, None] == (tile_end - ptiles)[None, :]) & used[None, :], axis=1)
                 & (tile_ids < nt)).astype(I32)
    run_slot = (jnp.cumsum(run_first) - 1) % 2
    later_used = used[None, :] & (experts[None, :] > experts[:, None])
    next_of = jnp.min(jnp.where(later_used, experts[None, :], N_EXPERTS), axis=1)
    next_of = jnp.where(next_of < N_EXPERTS, next_of, -1)
    run_next = jnp.sum(jnp.where(te[:, None] == experts[None, :], next_of[None, :], 0), axis=1)
    return pos, te, nt.reshape(1), zfill, run_first, run_slot.astype(I32), run_next.astype(I32)


def kernel(x, mem, rel_bias_table, norm_mix, w_in, attn_sinks, conv_w, conv_b, gate_bias_i, gate_bias_f, mlstm_norm, w_out, norm_cross, norm_mem, w_cq, w_ck, w_cv, w_co, norm_moe, w_router_group, b_router_group, w_router_expert, b_router_expert, w_exp_gate, w_exp_up, w_exp_down, norm_final):
    B, S, _ = x.shape
    T = B * S
    depth = w_in.shape[0]
    x2d = x.reshape(T, D_MODEL)
    mem2d = mem.reshape(B * N_MEM, D_MODEL)
    bias = _band_bias(rel_bias_table)

    tmx = min(TM_EXPERT, T)
    n_tiles_max = (T * TOP_K) // tmx + N_EXPERTS
    tm_rows = min(TM_ROWDMA, T)

    assert depth == 1, "the final combine is fused with the final norm: single layer only"
    l = 0
    w_pad = jnp.pad(w_in[l], ((0, 0), (0, C_GATE + LANES - D_IN))).astype(BF16)
    gb = jnp.concatenate([gate_bias_i[l], gate_bias_f[l]]).astype(F32)
    gbias_col = jnp.pad(gb, (0, LANES - GATE_ROWS))[None, :]
    qa, kva, qkb, vb, ob, gc, gr = _inproj(x2d, norm_mix[l][None, :], w_pad, gbias_col, B, S)

    per_seq = lambda a: a.reshape(B, S, a.shape[-1])
    mix = _seqmix(per_seq(qkb), per_seq(vb), per_seq(ob), per_seq(gc), gr, per_seq(qa), per_seq(kva),
                  conv_w[l][:, 0, :].astype(F32), conv_b[l][None, :].astype(F32),
                  mlstm_norm[l][None, :].astype(F32), bias, attn_sinks[l].astype(F32)).reshape(T, W_A_Q + W_B)

    ck, cv = _memkv(mem2d, norm_mem[l][None, :], w_ck[l].astype(BF16), w_cv[l].astype(BF16), B)

    wr = jnp.pad(jnp.concatenate([w_router_expert[l], w_router_group[l]], axis=1),
                 ((0, 0), (0, LANES - N_EXPERTS - N_GROUPS))).astype(BF16)
    br = jnp.pad(jnp.concatenate([b_router_expert[l], b_router_group[l]]),
                 (0, LANES - N_EXPERTS - N_GROUPS)).astype(F32)[None, :]
    x2, hz_packed, route, route_rows, counts = _mid(
        x2d, mix, w_out[l].astype(BF16), norm_cross[l][None, :], w_cq[l].astype(BF16), ck, cv,
        w_co[l].astype(BF16), norm_moe[l][None, :], wr, br, B, S)

    pos, te, nt, zfill, run_first, run_slot, run_next = _dispatch_plan(route_rows, counts, tmx, n_tiles_max,
                                                                       tm_rows)
    xs = _dispatch(hz_packed, pos, zfill, n_tiles_max * tmx, tmx)
    ys = _experts(xs, w_exp_gate[l], w_exp_up[l], w_exp_down[l], te, nt, run_first, run_slot, run_next, tmx)
    out = _final(x2, ys, pos, route, norm_final[None, :])
    return out.reshape(B, S, D_MODEL)
```

```python
import math

import jax
import jax.numpy as jnp
from jax import lax
from jax.experimental import pallas as pl
from jax.experimental.pallas import tpu as pltpu

F32 = jnp.float32
BF16 = jnp.bfloat16
U32 = jnp.uint32
I32 = jnp.int32

D_MODEL = 1024
N_MEM = 256
N_HEADS_A = 8
N_KV_A = 2
HEAD_DIM_A = 64
BLOCK = 128
WINDOW = 128
NUM_BUCKETS = 32
MAX_EXACT = NUM_BUCKETS // 2
MAX_DISTANCE = 128
N_HEADS_B = 4
HEAD_DIM_B = 128
CHUNK = 128
CONV_WIDTH = 4
N_HEADS_X = 4
HEAD_DIM_X = D_MODEL // N_HEADS_X
N_GROUPS = 4
EXPERTS_PER_GROUP = 8
N_EXPERTS = N_GROUPS * EXPERTS_PER_GROUP
TOP_K = 2
D_EXPERT = 512
EPS = 1e-6
NEG_INF = -1e30

W_A_Q = N_HEADS_A * HEAD_DIM_A
W_A_KV = N_KV_A * HEAD_DIM_A
W_B = N_HEADS_B * HEAD_DIM_B
C_QA = 0
C_KVA = C_QA + W_A_Q
C_QKB = C_KVA + 2 * W_A_KV
C_VB = C_QKB + 2 * W_B
C_OB = C_VB + W_B
C_GATE = C_OB + W_B
D_IN = C_GATE + 2 * N_HEADS_B

LANES = 128
SUBLANES = 8
GATE_ROWS = 8
HALF = D_MODEL // 2

TM_INPROJ = 512
TM_MID = 512
TM_ROWDMA = 512
TM_EXPERT = 256

VMEM_LIMIT = 48 * 1024 * 1024


def _rms(xf, g):
    return xf * lax.rsqrt(jnp.mean(xf * xf, axis=-1, keepdims=True) + EPS) * g


def _pack_halves(v):
    b = pltpu.bitcast(v.astype(BF16).astype(F32), U32)
    return (b[:, :HALF] >> 16) | b[:, HALF:]


def _unpack_halves(p):
    lo = pltpu.bitcast(p << 16, F32)
    hi = pltpu.bitcast(p & jnp.uint32(0xFFFF0000), F32)
    return jnp.concatenate([lo, hi], axis=-1)


def _log_sigmoid(z):
    return jnp.minimum(z, 0.0) - jnp.log1p(jnp.exp(-jnp.abs(z)))


def _split3(v):
    hi = v.astype(BF16).astype(F32)
    rest = v - hi
    mid = rest.astype(BF16).astype(F32)
    return hi, mid, (rest - mid).astype(BF16).astype(F32)


def _inproj_kernel(x_ref, g_ref, w_ref, gbc_ref, qa_ref, kva_ref, qkb_ref, vb_ref, ob_ref, gc_ref, gr_ref):
    tm = x_ref.shape[0]
    h = _rms(x_ref[...], g_ref[...]).astype(BF16)

    def mm(lo, hi):
        return jnp.dot(h, w_ref[:, lo:hi], preferred_element_type=F32)

    qa_ref[...] = mm(C_QA, C_KVA).astype(BF16)
    kva_ref[...] = mm(C_KVA, C_QKB).astype(BF16)
    qkb_ref[...] = mm(C_QKB, C_VB).astype(BF16)
    vb_ref[...] = mm(C_VB, C_OB).astype(BF16)
    ob_ref[...] = mm(C_OB, C_GATE).astype(BF16)

    H, L = N_HEADS_B, CHUNK
    gcol = mm(C_GATE, C_GATE + LANES) + gbc_ref[...]
    grow = jnp.transpose(gcol)[0:GATE_ROWS, :]
    lane_c = lax.broadcasted_iota(I32, (L, LANES), 1)
    is_f_col = (lane_c >= H) & (lane_c < 2 * H)
    is_f_row = lax.broadcasted_iota(I32, (GATE_ROWS, L), 0) >= H
    ti = lax.broadcasted_iota(I32, (L, L), 0)
    si = lax.broadcasted_iota(I32, (L, L), 1)
    tril = jnp.where(si <= ti, 1.0, 0.0).astype(BF16)
    triu = jnp.where(si >= ti, 1.0, 0.0).astype(BF16)
    for c in range(tm // L):
        rows = slice(c * L, (c + 1) * L)
        gcol_c = gcol[rows, :]
        fcol = jnp.where(is_f_col, _log_sigmoid(gcol_c), 0.0)
        parts = jnp.dot(tril, jnp.concatenate(_split3(fcol), axis=1).astype(BF16), preferred_element_type=F32)
        bcol = parts[:, :LANES] + parts[:, LANES:2 * LANES] + parts[:, 2 * LANES:]
        gc_ref[rows, :] = jnp.where(is_f_col, bcol, gcol_c)
        grow_c = grow[:, rows]
        frow = jnp.where(is_f_row, _log_sigmoid(grow_c), 0.0)
        parts = jnp.dot(jnp.concatenate(_split3(frow), axis=0).astype(BF16), triu, preferred_element_type=F32)
        brow = parts[:GATE_ROWS] + parts[GATE_ROWS:2 * GATE_ROWS] + parts[2 * GATE_ROWS:]
        gr_ref[:, rows] = jnp.where(is_f_row, brow, grow_c)


def _inproj(x2d, g, w_pad, gbias_col, B, S):
    T = x2d.shape[0]
    tm = min(TM_INPROJ, S)
    tiles_per_seq = S // tm
    row = lambda w: pl.BlockSpec((tm, w), lambda i: (i, 0))
    full = lambda a: pl.BlockSpec(a.shape, lambda i: (0,) * a.ndim)
    return pl.pallas_call(
        _inproj_kernel,
        grid=(T // tm,),
        in_specs=[row(D_MODEL), full(g), full(w_pad), full(gbias_col)],
        out_specs=[row(W_A_Q), row(2 * W_A_KV), row(2 * W_B), row(W_B), row(W_B), row(LANES),
                   pl.BlockSpec((None, GATE_ROWS, tm), lambda i: (i // tiles_per_seq, 0, i % tiles_per_seq))],
        out_shape=[jax.ShapeDtypeStruct((T, W_A_Q), BF16),
                   jax.ShapeDtypeStruct((T, 2 * W_A_KV), BF16),
                   jax.ShapeDtypeStruct((T, 2 * W_B), BF16),
                   jax.ShapeDtypeStruct((T, W_B), BF16),
                   jax.ShapeDtypeStruct((T, W_B), BF16),
                   jax.ShapeDtypeStruct((T, LANES), F32),
                   jax.ShapeDtypeStruct((B, GATE_ROWS, S), F32)],
        compiler_params=pltpu.CompilerParams(dimension_semantics=("parallel",), vmem_limit_bytes=VMEM_LIMIT),
        name="inproj",
    )(x2d, g, w_pad, gbias_col)


def _swa_block(q, kvc, kvp, bias_ref, sink_ref):
    kvp = kvp.astype(F32)
    kvc = kvc.astype(F32)
    kband = jnp.concatenate([kvp[:, :W_A_KV], kvc[:, :W_A_KV]], axis=0)
    vband = jnp.concatenate([kvp[:, W_A_KV:], kvc[:, W_A_KV:]], axis=0)
    lane = lax.broadcasted_iota(I32, (2 * BLOCK, LANES), 1)
    lo = lane < HEAD_DIM_A

    def placements(band):
        swapped = pltpu.roll(band, HEAD_DIM_A, axis=1)
        z = jnp.zeros_like(band)
        return {(0, 0): jnp.where(lo, band, z).astype(BF16), (0, 1): jnp.where(lo, z, swapped).astype(BF16),
                (1, 0): jnp.where(lo, swapped, z).astype(BF16), (1, 1): jnp.where(lo, z, band).astype(BF16)}

    kpl = placements(kband)
    vpl = placements(vband)

    scale = HEAD_DIM_A ** -0.5
    group = N_HEADS_A // N_KV_A
    tiles = []

    for pair in range(N_HEADS_A // 2):
        qt = q[:, pair * LANES:(pair + 1) * LANES]
        acc = None
        for half in range(2):
            h = 2 * pair + half
            g = h // group
            s = lax.dot_general(qt, kpl[(g, half)], (((1,), (1,)), ((), ())), preferred_element_type=F32)
            s = s * scale + bias_ref[h]
            sink = sink_ref[h]
            m = jnp.maximum(jnp.max(s, axis=-1, keepdims=True), sink)
            p = jnp.exp(s - m)
            denom = jnp.sum(p, axis=-1, keepdims=True) + jnp.exp(sink - m)
            o = jnp.dot(p.astype(BF16), vpl[(g, half)], preferred_element_type=F32) / denom
            acc = o if acc is None else acc + o
        tiles.append(acc.astype(BF16))
    return tiles


CONV_HALO = 16


def _seqmix_kernel(sink_ref, qkc_ref, qkp_ref, vb_ref, ob_ref, gc_ref, gr_ref, qa_ref, kvc_ref, kvp_ref,
                   cw_ref, cb_ref, nrm_ref, bias_ref, o_ref, state_ref, m_ref):
    c = pl.program_id(0)
    B = qkc_ref.shape[0]
    H, D, L = N_HEADS_B, HEAD_DIM_B, CHUNK

    @pl.when(c == 0)
    def _():
        state_ref[...] = jnp.zeros_like(state_ref)
        m_ref[...] = jnp.zeros_like(m_ref)

    rr = lax.broadcasted_iota(I32, (L, CONV_HALO + L), 0)
    cc = lax.broadcasted_iota(I32, (L, CONV_HALO + L), 1)
    shifts = {delay: jnp.where(cc == rr + (CONV_HALO - delay), 1.0, 0.0).astype(BF16)
              for delay in range(1, CONV_WIDTH)}
    ti = lax.broadcasted_iota(I32, (L, L), 0)
    si = lax.broadcasted_iota(I32, (L, L), 1)
    tri = si <= ti
    ones_blk = jnp.ones((L, D), BF16)

    def conv_silu(b):
        prev = qkp_ref[b]
        prev = jnp.where(c > 0, prev, jnp.zeros_like(prev))
        cur = qkc_ref[b]
        ext = jnp.concatenate([prev, cur], axis=0)
        y = cb_ref[...] + cw_ref[CONV_WIDTH - 1:CONV_WIDTH, :] * cur.astype(F32)
        for delay in range(1, CONV_WIDTH):
            tap = CONV_WIDTH - 1 - delay
            y = y + cw_ref[tap:tap + 1, :] * jnp.dot(shifts[delay], ext, preferred_element_type=F32)
        return y * jax.nn.sigmoid(y)

    states = [[state_ref[b, h] for h in range(H)] for b in range(B)]
    m_alls = [m_ref[b] for b in range(B)]
    new_states, new_m, outs = {}, {}, {}

    for b, h in [(b, h) for b in range(B) for h in range(H)]:
        if h == 0:
            for pair, tile in enumerate(_swa_block(qa_ref[b], kvc_ref[b], kvp_ref[b], bias_ref, sink_ref)):
                o_ref[b, :, pair * LANES:(pair + 1) * LANES] = tile
            qk = conv_silu(b)
            gcol = gc_ref[b]
            grow = gr_ref[b]
        qh = (qk[:, h * D:(h + 1) * D] * (D ** -0.5)).astype(BF16)
        k_t = qk[:, W_B + h * D:W_B + (h + 1) * D].T
        v1 = jnp.concatenate([vb_ref[b, :, h * D:(h + 1) * D], ones_blk], axis=-1)
        b_r = grow[H + h:H + h + 1, :]
        g_r = grow[h:h + 1, :] - b_r
        b_c = gcol[:, H + h:H + h + 1]
        m_prev = m_alls[b][h:h + 1, 0:1]
        state = states[b][h]

        gmat = jnp.where(tri, g_r, NEG_INF)
        m_c = jnp.maximum(jnp.max(gmat, axis=-1, keepdims=True), m_prev)
        a_inter = jnp.exp(m_prev - m_c)
        sc = jnp.dot(qh, k_t.astype(BF16), preferred_element_type=F32) * jnp.exp(gmat - m_c)
        tot = (jnp.dot(sc.astype(BF16), v1, preferred_element_type=F32)
               + a_inter * jnp.dot(qh, state.astype(BF16), preferred_element_type=F32))
        num = tot[:, :D]
        den = tot[:, D:]
        hh = num / jnp.maximum(jnp.abs(den), jnp.exp(-(b_c + m_c)))

        b_last = b_r[:, L - 1:L]
        m_new = jnp.maximum(b_last + m_prev, b_last + jnp.max(g_r, axis=-1, keepdims=True))
        w_r = jnp.exp(g_r + (b_last - m_new))
        decay = jnp.exp(b_last + m_prev - m_new)
        upd = jnp.dot((k_t * w_r).astype(BF16), v1, preferred_element_type=F32)
        new_states[b, h] = decay * state + upd
        new_m[b, h] = jnp.broadcast_to(m_new, (1, LANES))

        og = jax.nn.sigmoid(ob_ref[b, :, h * D:(h + 1) * D].astype(F32))
        hb = og * hh
        hb = hb * lax.rsqrt(jnp.mean(hb * hb, axis=-1, keepdims=True) + EPS)
        outs[b, h] = (hb * nrm_ref[:, h * D:(h + 1) * D]).astype(BF16)

    for b, h in new_states:
        state_ref[b, h] = new_states[b, h]
        m_ref[b, h:h + 1, :] = new_m[b, h]
        o_ref[b, :, W_A_Q + h * D:W_A_Q + (h + 1) * D] = outs[b, h]


def _seqmix(qkb, vb, ob, gc, gr, qa, kva, conv_w, conv_b, nrm, bias, sinks):
    assert CHUNK == BLOCK
    B, S, _ = qkb.shape
    nc = S // CHUNK
    halo_per_chunk = CHUNK // CONV_HALO
    blk = lambda w: pl.BlockSpec((B, CHUNK, w), lambda c: (0, c, 0))
    full = lambda a: pl.BlockSpec(a.shape, lambda c: (0,) * a.ndim)
    return pl.pallas_call(
        _seqmix_kernel,
        grid=(nc,),
        in_specs=[pl.BlockSpec(memory_space=pltpu.SMEM),
                  blk(2 * W_B),
                  pl.BlockSpec((B, CONV_HALO, 2 * W_B), lambda c: (0, jnp.maximum(c * halo_per_chunk - 1, 0), 0)),
                  blk(W_B), blk(W_B), blk(LANES),
                  pl.BlockSpec((B, GATE_ROWS, CHUNK), lambda c: (0, 0, c)),
                  blk(W_A_Q), blk(2 * W_A_KV),
                  pl.BlockSpec((B, BLOCK, 2 * W_A_KV), lambda c: (0, jnp.maximum(c - 1, 0), 0)),
                  full(conv_w), full(conv_b), full(nrm),
                  pl.BlockSpec((None,) + bias.shape[1:], lambda c: (jnp.minimum(c, 1), 0, 0, 0))],
        out_specs=blk(W_A_Q + W_B),
        out_shape=jax.ShapeDtypeStruct((B, S, W_A_Q + W_B), BF16),
        scratch_shapes=[pltpu.VMEM((B, N_HEADS_B, HEAD_DIM_B, 2 * HEAD_DIM_B), F32),
                        pltpu.VMEM((B, GATE_ROWS, LANES), F32)],
        compiler_params=pltpu.CompilerParams(dimension_semantics=("arbitrary",), vmem_limit_bytes=VMEM_LIMIT),
        name="seqmix",
    )(sinks, qkb, qkb, vb, ob, gc, gr, qa, kva, kva, conv_w, conv_b, nrm, bias)


def _memkv_kernel(mem_ref, g_ref, wk_ref, wv_ref, k_ref, v_ref):
    hm = _rms(mem_ref[...], g_ref[...]).astype(BF16)
    k_ref[...] = jnp.dot(hm, wk_ref[...], preferred_element_type=F32).astype(BF16)
    v_ref[...] = jnp.dot(hm, wv_ref[...], preferred_element_type=F32).astype(BF16)


def _memkv(mem2d, g, wk, wv, B):
    full = lambda a: pl.BlockSpec(a.shape, lambda b: (0,) * a.ndim)
    blk = pl.BlockSpec((N_MEM, D_MODEL), lambda b: (b, 0))
    return pl.pallas_call(
        _memkv_kernel,
        grid=(B,),
        in_specs=[blk, full(g), full(wk), full(wv)],
        out_specs=[blk, blk],
        out_shape=[jax.ShapeDtypeStruct((B * N_MEM, D_MODEL), BF16)] * 2,
        compiler_params=pltpu.CompilerParams(dimension_semantics=("parallel",), vmem_limit_bytes=VMEM_LIMIT),
        name="memkv",
    )(mem2d, g, wk, wv)


ROUTE_E1, ROUTE_E2, ROUTE_G1, ROUTE_G2, ROUTE_R1, ROUTE_R2 = 0, 1, 2, 3, 4, 5
ROUTER_GROUP_COL = N_EXPERTS
ROUTER_ROWS = -(-(N_EXPERTS + N_GROUPS) // SUBLANES) * SUBLANES


def _mid_kernel(x_ref, mix_ref, wo_ref, gx_ref, wq_ref, ck_ref, cv_ref, wco_ref, gz_ref, wr_ref, br_ref,
                x2_ref, hz_ref, route_ref, rrows_ref, counts_ref, cnt_ref):
    @pl.when(pl.program_id(0) == 0)
    def _():
        cnt_ref[...] = jnp.zeros_like(cnt_ref)

    x1 = x_ref[...] + jnp.dot(mix_ref[...], wo_ref[...], preferred_element_type=F32)

    hc = _rms(x1, gx_ref[...]).astype(BF16)
    cq = jnp.dot(hc, wq_ref[...], preferred_element_type=F32).astype(BF16)
    scale = HEAD_DIM_X ** -0.5
    heads = []
    for h in range(N_HEADS_X):
        sl = slice(h * HEAD_DIM_X, (h + 1) * HEAD_DIM_X)
        s = lax.dot_general(cq[:, sl], ck_ref[:, sl], (((1,), (1,)), ((), ())), preferred_element_type=F32) * scale
        p = jnp.exp(s - jnp.max(s, axis=-1, keepdims=True))
        co = jnp.dot(p.astype(BF16), cv_ref[:, sl], preferred_element_type=F32) / jnp.sum(p, axis=-1, keepdims=True)
        heads.append(co.astype(BF16))
    x2 = x1 + jnp.dot(jnp.concatenate(heads, axis=-1), wco_ref[...], preferred_element_type=F32)
    x2_ref[...] = x2

    hz = _rms(x2, gz_ref[...])
    hz_ref[...] = _pack_halves(hz)
    lg = jnp.dot(hz.astype(BF16), wr_ref[...], preferred_element_type=F32) + br_ref[...]
    tm = lg.shape[0]
    lt = jnp.transpose(lg)[0:ROUTER_ROWS, :]
    row = lax.broadcasted_iota(I32, lt.shape, 0)
    big = jnp.int32(ROUTER_ROWS)
    is_g = (row >= ROUTER_GROUP_COL) & (row < ROUTER_GROUP_COL + N_GROUPS)
    gl = jnp.where(is_g, lt, NEG_INF)
    gmax = jnp.max(gl, axis=0, keepdims=True)
    gsum = jnp.sum(jnp.exp(gl - gmax), axis=0, keepdims=True)
    g_prob = 1.0 / gsum
    g_idx = jnp.min(jnp.where(gl == gmax, row - ROUTER_GROUP_COL, big), axis=0, keepdims=True)
    sel = (row < N_EXPERTS) & ((row // EXPERTS_PER_GROUP) == g_idx)
    el = jnp.where(sel, lt, NEG_INF)
    m1 = jnp.max(el, axis=0, keepdims=True)
    i1 = jnp.min(jnp.where(el == m1, row, big), axis=0, keepdims=True)
    el2 = jnp.where(row == i1, NEG_INF, el)
    m2 = jnp.max(el2, axis=0, keepdims=True)
    i2 = jnp.min(jnp.where(el2 == m2, row, big), axis=0, keepdims=True)
    z = jnp.sum(jnp.exp(el - m1), axis=0, keepdims=True)
    p1 = 1.0 / z
    p2 = jnp.exp(m2 - m1) / z
    g1 = g_prob * (p1 / (p1 + p2))
    g2 = g_prob * (p2 / (p1 + p2))

    used = jnp.where((row == i1) | (row == i2), 1.0, 0.0)
    t_from = lax.broadcasted_iota(I32, (tm, tm), 0)
    t_to = lax.broadcasted_iota(I32, (tm, tm), 1)
    earlier = jnp.where(t_from < t_to, 1.0, 0.0).astype(BF16)
    before = jnp.dot(used.astype(BF16), earlier, preferred_element_type=F32) + cnt_ref[:, 0:1]
    r1 = jnp.sum(jnp.where(row == i1, before, 0.0), axis=0, keepdims=True)
    r2 = jnp.sum(jnp.where(row == i2, before, 0.0), axis=0, keepdims=True)
    cnt_ref[...] = cnt_ref[...] + jnp.sum(used, axis=1, keepdims=True)
    counts_ref[...] = cnt_ref[...]

    rec_row = lax.broadcasted_iota(I32, (SUBLANES, tm), 0)
    rec = jnp.zeros((SUBLANES, tm), F32)
    for c, v in ((ROUTE_E1, i1.astype(F32)), (ROUTE_E2, i2.astype(F32)), (ROUTE_G1, g1), (ROUTE_G2, g2),
                 (ROUTE_R1, r1), (ROUTE_R2, r2)):
        rec = jnp.where(rec_row == c, v, rec)
    rrows_ref[...] = rec
    route_ref[...] = jnp.transpose(jnp.concatenate([rec, jnp.zeros((LANES - SUBLANES, tm), F32)], axis=0))


def _mid(x2d, mix, wo, gx, wq, ck, cv, wco, gz, wr, br, B, S):
    T = B * S
    tm = min(TM_MID, S)
    per_b = S // tm
    row = lambda w: pl.BlockSpec((tm, w), lambda i: (i, 0))
    full = lambda a: pl.BlockSpec(a.shape, lambda i: (0,) * a.ndim)
    kvspec = pl.BlockSpec((N_MEM, D_MODEL), lambda i: (i // per_b, 0))
    return pl.pallas_call(
        _mid_kernel,
        grid=(T // tm,),
        in_specs=[row(D_MODEL), row(W_A_Q + W_B), full(wo), full(gx), full(wq), kvspec, kvspec,
                  full(wco), full(gz), full(wr), full(br)],
        out_specs=[row(D_MODEL), row(HALF), row(LANES), pl.BlockSpec((SUBLANES, tm), lambda i: (0, i)),
                   pl.BlockSpec((ROUTER_ROWS, LANES), lambda i: (0, 0))],
        out_shape=[jax.ShapeDtypeStruct((T, D_MODEL), F32),
                   jax.ShapeDtypeStruct((T, HALF), U32),
                   jax.ShapeDtypeStruct((T, LANES), F32),
                   jax.ShapeDtypeStruct((SUBLANES, T), F32),
                   jax.ShapeDtypeStruct((ROUTER_ROWS, LANES), F32)],
        scratch_shapes=[pltpu.VMEM((ROUTER_ROWS, LANES), F32)],
        compiler_params=pltpu.CompilerParams(dimension_semantics=("arbitrary",), vmem_limit_bytes=VMEM_LIMIT),
        name="mid",
    )(x2d, mix, wo, gx, wq, ck, cv, wco, gz, wr, br)


def _dispatch_kernel(zf_ref, pos_ref, hz_ref, xs_hbm, zbuf, sem, zsem):
    i = pl.program_id(0)
    tm = hz_ref.shape[0]
    zrows = zbuf.shape[0]

    @pl.when(i == 0)
    def _():
        zbuf[...] = jnp.zeros_like(zbuf)

        def fill(t):
            return pltpu.make_async_copy(zbuf, xs_hbm.at[pl.ds(t * zrows, zrows)], zsem)

        def start(t, carry):
            @pl.when(zf_ref[t] != 0)
            def _():
                fill(t).start()
            return carry

        def wait(t, carry):
            @pl.when(zf_ref[t] != 0)
            def _():
                fill(t).wait()
            return carry

        lax.fori_loop(0, zf_ref.shape[0], start, 0)
        lax.fori_loop(0, zf_ref.shape[0], wait, 0)

    for r in range(tm):
        for k in range(TOP_K):
            pltpu.make_async_copy(hz_ref.at[pl.ds(r, 1)], xs_hbm.at[pl.ds(pos_ref[0, 0, k * tm + r], 1)],
                                  sem).start(priority=k % 2)
    for k in range(TOP_K):
        pltpu.make_async_copy(hz_ref, xs_hbm.at[pl.ds(0, tm)], sem).wait()


def _dispatch(hz_packed, pos, zfill, n_slots, tmx):
    T = hz_packed.shape[0]
    tm = pos.shape[2] // TOP_K
    grid_spec = pltpu.PrefetchScalarGridSpec(
        num_scalar_prefetch=1,
        grid=(T // tm,),
        in_specs=[pl.BlockSpec((1, 1, TOP_K * tm), lambda i, zf: (i, 0, 0), memory_space=pltpu.SMEM),
                  pl.BlockSpec((tm, HALF), lambda i, zf: (i, 0))],
        out_specs=pl.BlockSpec(memory_space=pl.ANY),
        scratch_shapes=[pltpu.VMEM((tmx, HALF), U32), pltpu.SemaphoreType.DMA(()), pltpu.SemaphoreType.DMA(())],
    )
    return pl.pallas_call(
        _dispatch_kernel,
        grid_spec=grid_spec,
        out_shape=jax.ShapeDtypeStruct((n_slots, HALF), U32),
        compiler_params=pltpu.CompilerParams(dimension_semantics=("arbitrary",), vmem_limit_bytes=VMEM_LIMIT),
        name="dispatch",
    )(zfill, pos, hz_packed)


def _expert_kernel(te_ref, nt_ref, first_ref, slot_ref, next_ref, xs_ref, wg_hbm, wu_hbm, wd_hbm, ys_ref,
                   wg32, wu32, wd32, wgb, wub, wdb, wsem):
    i = pl.program_id(0)
    nt = nt_ref[0]

    def fetch(e, s):
        return [pltpu.make_async_copy(src.at[e], dst.at[s], wsem.at[s])
                for src, dst in ((wg_hbm, wg32), (wu_hbm, wu32), (wd_hbm, wd32))]

    @pl.when(i < nt)
    def _():
        @pl.when(i == 0)
        def _():
            for cp in fetch(te_ref[0], 0):
                cp.start()

        @pl.when(first_ref[i] != 0)
        def _():
            s = slot_ref[i]
            for cp in fetch(te_ref[i], s):
                cp.wait()

            @pl.when(next_ref[i] >= 0)
            def _():
                for cp in fetch(next_ref[i], 1 - s):
                    cp.start()

            wgb[...] = wg32[s].astype(BF16)
            wub[...] = wu32[s].astype(BF16)
            wdb[...] = wd32[s].astype(BF16)

        x = _unpack_halves(xs_ref[...]).astype(BF16)
        hg = jnp.dot(x, wgb[...], preferred_element_type=F32)
        hu = jnp.dot(x, wub[...], preferred_element_type=F32)
        a = (hg * jax.nn.sigmoid(hg) * hu).astype(BF16)
        ys_ref[...] = _pack_halves(jnp.dot(a, wdb[...], preferred_element_type=F32))

    @pl.when(i >= nt)
    def _():
        ys_ref[...] = jnp.zeros_like(ys_ref)


def _experts(xs, w_gate, w_up, w_down, tile_expert, ntiles, run_first, run_slot, run_next, tmx):
    n_tiles_max = tile_expert.shape[0]
    hbm = pl.BlockSpec(memory_space=pl.ANY)
    grid_spec = pltpu.PrefetchScalarGridSpec(
        num_scalar_prefetch=5,
        grid=(n_tiles_max,),
        in_specs=[pl.BlockSpec((tmx, HALF), lambda i, te, nt, *_: (jnp.minimum(i, nt[0] - 1), 0)), hbm, hbm, hbm],
        out_specs=pl.BlockSpec((tmx, HALF), lambda i, *_: (i, 0)),
        scratch_shapes=[pltpu.VMEM((2, D_MODEL, D_EXPERT), F32),
                        pltpu.VMEM((2, D_MODEL, D_EXPERT), F32),
                        pltpu.VMEM((2, D_EXPERT, D_MODEL), F32),
                        pltpu.VMEM((D_MODEL, D_EXPERT), BF16),
                        pltpu.VMEM((D_MODEL, D_EXPERT), BF16),
                        pltpu.VMEM((D_EXPERT, D_MODEL), BF16),
                        pltpu.SemaphoreType.DMA((2,))],
    )
    return pl.pallas_call(
        _expert_kernel,
        grid_spec=grid_spec,
        out_shape=jax.ShapeDtypeStruct(xs.shape, U32),
        compiler_params=pltpu.CompilerParams(dimension_semantics=("arbitrary",), vmem_limit_bytes=VMEM_LIMIT),
        name="experts",
    )(tile_expert, ntiles, run_first, run_slot, run_next, xs, w_gate, w_up, w_down)


def _final_kernel(posc_ref, posn_ref, x2_ref, route_ref, g_ref, ys_hbm, o_ref, ybuf, sem):
    i = pl.program_id(0)
    n = pl.num_programs(0)
    tm = x2_ref.shape[0]
    slot = i % 2

    def issue(pos_ref, s):
        for r in range(tm):
            for k in range(TOP_K):
                pltpu.make_async_copy(ys_hbm.at[pl.ds(pos_ref[0, 0, k * tm + r], 1)],
                                      ybuf.at[s, k, pl.ds(r, 1)], sem.at[s]).start(priority=k % 2)

    def wait(s):
        for k in range(TOP_K):
            pltpu.make_async_copy(ys_hbm.at[pl.ds(0, tm)], ybuf.at[s, k], sem.at[s]).wait()

    @pl.when(i == 0)
    def _():
        issue(posc_ref, 0)

    wait(slot)

    for s in range(2):
        @pl.when(slot == s)
        def _():
            issue(posn_ref, 1 - s)

    r = route_ref[...]
    g1 = r[:, ROUTE_G1:ROUTE_G1 + 1]
    g2 = r[:, ROUTE_G2:ROUTE_G2 + 1]
    xo = x2_ref[...] + g1 * _unpack_halves(ybuf[slot, 0]) + g2 * _unpack_halves(ybuf[slot, 1])
    o_ref[...] = _rms(xo, g_ref[...])

    @pl.when(i == n - 1)
    def _():
        wait(1 - slot)


def _final(x2, ys, pos, route, g):
    T = x2.shape[0]
    nblk = pos.shape[0]
    tm = T // nblk
    row = lambda w: pl.BlockSpec((tm, w), lambda i: (i, 0))
    return pl.pallas_call(
        _final_kernel,
        grid=(nblk,),
        in_specs=[pl.BlockSpec((1, 1, TOP_K * tm), lambda i: (i, 0, 0), memory_space=pltpu.SMEM),
                  pl.BlockSpec((1, 1, TOP_K * tm), lambda i: (jnp.minimum(i + 1, nblk - 1), 0, 0),
                               memory_space=pltpu.SMEM),
                  row(D_MODEL), row(LANES), pl.BlockSpec(g.shape, lambda i: (0, 0)),
                  pl.BlockSpec(memory_space=pl.ANY)],
        out_specs=row(D_MODEL),
        out_shape=jax.ShapeDtypeStruct((T, D_MODEL), F32),
        scratch_shapes=[pltpu.VMEM((2, TOP_K, tm, HALF), U32), pltpu.SemaphoreType.DMA((2,))],
        compiler_params=pltpu.CompilerParams(dimension_semantics=("arbitrary",), vmem_limit_bytes=VMEM_LIMIT),
        name="final",
    )(pos, pos, x2, route, g, ys)


def _band_bias(table):
    i = jnp.arange(BLOCK)[:, None]
    j = jnp.arange(2 * BLOCK)[None, :]
    n = jnp.maximum(i + BLOCK - j, 0)
    nf = jnp.maximum(n, 1).astype(F32)
    large = MAX_EXACT + (jnp.log(nf / MAX_EXACT) / math.log(MAX_DISTANCE / MAX_EXACT)
                         * (NUM_BUCKETS - MAX_EXACT)).astype(I32)
    large = jnp.minimum(large, NUM_BUCKETS - 1)
    bucket = jnp.where(n < MAX_EXACT, n, large)
    onehot = (bucket[:, :, None] == jnp.arange(NUM_BUCKETS)[None, None, :]).astype(F32)
    bias = jnp.einsum("ijb,bh->hij", onehot, table.astype(F32), precision=lax.Precision.HIGHEST)
    d = i + BLOCK - j
    band_ok = (d >= 0) & (d < WINDOW)
    first_ok = band_ok & (j >= BLOCK)
    return jnp.stack([jnp.where(first_ok[None], bias, NEG_INF), jnp.where(band_ok[None], bias, NEG_INF)])


def _dispatch_plan(route_rows, counts_f, tmx, n_tiles_max, tm_rows):
    T = route_rows.shape[1]
    experts = jnp.arange(N_EXPERTS, dtype=I32)
    counts = counts_f[:N_EXPERTS, 0].astype(I32)
    ptiles = (counts + tmx - 1) // tmx
    tile_end = jnp.cumsum(ptiles)
    nt = tile_end[-1]
    row_off = (tile_end - ptiles) * tmx

    def slot(e_row, r_row):
        e = route_rows[e_row].astype(I32)
        off = jnp.sum(jnp.where(e[None, :] == experts[:, None], row_off[:, None], 0), axis=0)
        return (off + route_rows[r_row].astype(I32)).reshape(T // tm_rows, 1, tm_rows)

    pos = jnp.concatenate([slot(ROUTE_E1, ROUTE_R1), slot(ROUTE_E2, ROUTE_R2)], axis=2)

    tile_ids = jnp.arange(n_tiles_max, dtype=I32)
    expert_of = lambda t: jnp.sum((tile_end[None, :] <= t[:, None]).astype(I32), axis=1)
    te = expert_of(jnp.minimum(tile_ids, nt - 1))
    partial = jnp.any((tile_ids[:, None] == (tile_end - 1)[None, :]) & (counts % tmx != 0)[None, :], axis=1)
    zfill = (partial | (tile_ids >= nt)).astype(I32)

    used = ptiles > 0
    run_first = (jnp.any((tile_ids[:, None] == (tile_end - ptiles)[None, :]) & used[None, :], axis=1)
                 & (tile_ids < nt)).astype(I32)
    run_slot = (jnp.cumsum(run_first) - 1) % 2
    later_used = used[None, :] & (experts[None, :] > experts[:, None])
    next_of = jnp.min(jnp.where(later_used, experts[None, :], N_EXPERTS), axis=1)
    next_of = jnp.where(next_of < N_EXPERTS, next_of, -1)
    run_next = jnp.sum(jnp.where(te[:, None] == experts[None, :], next_of[None, :], 0), axis=1)
    return pos, te, nt.reshape(1), zfill, run_first, run_slot.astype(I32), run_next.astype(I32)


def kernel(x, mem, rel_bias_table, norm_mix, w_in, attn_sinks, conv_w, conv_b, gate_bias_i, gate_bias_f, mlstm_norm, w_out, norm_cross, norm_mem, w_cq, w_ck, w_cv, w_co, norm_moe, w_router_group, b_router_group, w_router_expert, b_router_expert, w_exp_gate, w_exp_up, w_exp_down, norm_final):
    B, S, _ = x.shape
    T = B * S
    depth = w_in.shape[0]
    x2d = x.reshape(T, D_MODEL)
    mem2d = mem.reshape(B * N_MEM, D_MODEL)
    bias = _band_bias(rel_bias_table)

    tmx = min(TM_EXPERT, T)
    n_tiles_max = (T * TOP_K) // tmx + N_EXPERTS
    tm_rows = min(TM_ROWDMA, T)

    assert depth == 1, "the final combine is fused with the final norm: single layer only"
    l = 0
    w_pad = jnp.pad(w_in[l], ((0, 0), (0, C_GATE + LANES - D_IN))).astype(BF16)
    gb = jnp.concatenate([gate_bias_i[l], gate_bias_f[l]]).astype(F32)
    gbias_col = jnp.pad(gb, (0, LANES - GATE_ROWS))[None, :]
    qa, kva, qkb, vb, ob, gc, gr = _inproj(x2d, norm_mix[l][None, :], w_pad, gbias_col, B, S)

    per_seq = lambda a: a.reshape(B, S, a.shape[-1])
    mix = _seqmix(per_seq(qkb), per_seq(vb), per_seq(ob), per_seq(gc), gr, per_seq(qa), per_seq(kva),
                  conv_w[l][:, 0, :].astype(F32), conv_b[l][None, :].astype(F32),
                  mlstm_norm[l][None, :].astype(F32), bias, attn_sinks[l].astype(F32)).reshape(T, W_A_Q + W_B)

    ck, cv = _memkv(mem2d, norm_mem[l][None, :], w_ck[l].astype(BF16), w_cv[l].astype(BF16), B)

    wr = jnp.pad(jnp.concatenate([w_router_expert[l], w_router_group[l]], axis=1),
                 ((0, 0), (0, LANES - N_EXPERTS - N_GROUPS))).astype(BF16)
    br = jnp.pad(jnp.concatenate([b_router_expert[l], b_router_group[l]]),
                 (0, LANES - N_EXPERTS - N_GROUPS)).astype(F32)[None, :]
    x2, hz_packed, route, route_rows, counts = _mid(
        x2d, mix, w_out[l].astype(BF16), norm_cross[l][None, :], w_cq[l].astype(BF16), ck, cv,
        w_co[l].astype(BF16), norm_moe[l][None, :], wr, br, B, S)

    pos, te, nt, zfill, run_first, run_slot, run_next = _dispatch_plan(route_rows, counts, tmx, n_tiles_max,
                                                                       tm_rows)
    xs = _dispatch(hz_packed, pos, zfill, n_tiles_max * tmx, tmx)
    ys = _experts(xs, w_exp_gate[l], w_exp_up[l], w_exp_down[l], te, nt, run_first, run_slot, run_next, tmx)
    out = _final(x2, ys, pos, route, norm_final[None, :])
    return out.reshape(B, S, D_MODEL)
```

```python
import math

import jax
import jax.numpy as jnp
from jax import lax
from jax.experimental import pallas as pl
from jax.experimental.pallas import tpu as pltpu

F32 = jnp.float32
BF16 = jnp.bfloat16
U32 = jnp.uint32
I32 = jnp.int32

D_MODEL = 1024
N_MEM = 256
N_HEADS_A = 8
N_KV_A = 2
HEAD_DIM_A = 64
BLOCK = 128
WINDOW = 128
NUM_BUCKETS = 32
MAX_EXACT = NUM_BUCKETS // 2
MAX_DISTANCE = 128
N_HEADS_B = 4
HEAD_DIM_B = 128
CHUNK = 128
CONV_WIDTH = 4
N_HEADS_X = 4
HEAD_DIM_X = D_MODEL // N_HEADS_X
N_GROUPS = 4
EXPERTS_PER_GROUP = 8
N_EXPERTS = N_GROUPS * EXPERTS_PER_GROUP
TOP_K = 2
D_EXPERT = 512
EPS = 1e-6
NEG_INF = -1e30

W_A_Q = N_HEADS_A * HEAD_DIM_A
W_A_KV = N_KV_A * HEAD_DIM_A
W_B = N_HEADS_B * HEAD_DIM_B
C_QA = 0
C_KVA = C_QA + W_A_Q
C_QKB = C_KVA + 2 * W_A_KV
C_VB = C_QKB + 2 * W_B
C_OB = C_VB + W_B
C_GATE = C_OB + W_B
D_IN = C_GATE + 2 * N_HEADS_B

LANES = 128
SUBLANES = 8
GATE_ROWS = 8
HALF = D_MODEL // 2

TM_INPROJ = 512
TM_MID = 512
TM_ROWDMA = 512
TM_EXPERT = 512

VMEM_LIMIT = 48 * 1024 * 1024


def _rms(xf, g):
    return xf * lax.rsqrt(jnp.mean(xf * xf, axis=-1, keepdims=True) + EPS) * g


def _pack_halves(v):
    b = pltpu.bitcast(v.astype(BF16).astype(F32), U32)
    return (b[:, :HALF] >> 16) | b[:, HALF:]


def _unpack_halves(p):
    lo = pltpu.bitcast(p << 16, F32)
    hi = pltpu.bitcast(p & jnp.uint32(0xFFFF0000), F32)
    return jnp.concatenate([lo, hi], axis=-1)


def _log_sigmoid(z):
    return jnp.minimum(z, 0.0) - jnp.log1p(jnp.exp(-jnp.abs(z)))


def _split3(v):
    hi = v.astype(BF16).astype(F32)
    rest = v - hi
    mid = rest.astype(BF16).astype(F32)
    return hi, mid, (rest - mid).astype(BF16).astype(F32)


def _inproj_kernel(x_ref, g_ref, w_ref, gbc_ref, qa_ref, kva_ref, qkb_ref, vb_ref, ob_ref, gc_ref, gr_ref):
    tm = x_ref.shape[0]
    h = _rms(x_ref[...], g_ref[...]).astype(BF16)

    def mm(lo, hi):
        return jnp.dot(h, w_ref[:, lo:hi], preferred_element_type=F32)

    qa_ref[...] = mm(C_QA, C_KVA).astype(BF16)
    kva_ref[...] = mm(C_KVA, C_QKB).astype(BF16)
    qkb_ref[...] = mm(C_QKB, C_VB).astype(BF16)
    vb_ref[...] = mm(C_VB, C_OB).astype(BF16)
    ob_ref[...] = mm(C_OB, C_GATE).astype(BF16)

    H, L = N_HEADS_B, CHUNK
    gcol = mm(C_GATE, C_GATE + LANES) + gbc_ref[...]
    grow = jnp.transpose(gcol)[0:GATE_ROWS, :]
    lane_c = lax.broadcasted_iota(I32, (L, LANES), 1)
    is_f_col = (lane_c >= H) & (lane_c < 2 * H)
    is_f_row = lax.broadcasted_iota(I32, (GATE_ROWS, L), 0) >= H
    ti = lax.broadcasted_iota(I32, (L, L), 0)
    si = lax.broadcasted_iota(I32, (L, L), 1)
    tril = jnp.where(si <= ti, 1.0, 0.0).astype(BF16)
    triu = jnp.where(si >= ti, 1.0, 0.0).astype(BF16)
    for c in range(tm // L):
        rows = slice(c * L, (c + 1) * L)
        gcol_c = gcol[rows, :]
        fcol = jnp.where(is_f_col, _log_sigmoid(gcol_c), 0.0)
        parts = jnp.dot(tril, jnp.concatenate(_split3(fcol), axis=1).astype(BF16), preferred_element_type=F32)
        bcol = parts[:, :LANES] + parts[:, LANES:2 * LANES] + parts[:, 2 * LANES:]
        gc_ref[rows, :] = jnp.where(is_f_col, bcol, gcol_c)
        grow_c = grow[:, rows]
        frow = jnp.where(is_f_row, _log_sigmoid(grow_c), 0.0)
        parts = jnp.dot(jnp.concatenate(_split3(frow), axis=0).astype(BF16), triu, preferred_element_type=F32)
        brow = parts[:GATE_ROWS] + parts[GATE_ROWS:2 * GATE_ROWS] + parts[2 * GATE_ROWS:]
        gr_ref[:, rows] = jnp.where(is_f_row, brow, grow_c)


def _inproj(x2d, g, w_pad, gbias_col, B, S):
    T = x2d.shape[0]
    tm = min(TM_INPROJ, S)
    tiles_per_seq = S // tm
    row = lambda w: pl.BlockSpec((tm, w), lambda i: (i, 0))
    full = lambda a: pl.BlockSpec(a.shape, lambda i: (0,) * a.ndim)
    return pl.pallas_call(
        _inproj_kernel,
        grid=(T // tm,),
        in_specs=[row(D_MODEL), full(g), full(w_pad), full(gbias_col)],
        out_specs=[row(W_A_Q), row(2 * W_A_KV), row(2 * W_B), row(W_B), row(W_B), row(LANES),
                   pl.BlockSpec((None, GATE_ROWS, tm), lambda i: (i // tiles_per_seq, 0, i % tiles_per_seq))],
        out_shape=[jax.ShapeDtypeStruct((T, W_A_Q), BF16),
                   jax.ShapeDtypeStruct((T, 2 * W_A_KV), BF16),
                   jax.ShapeDtypeStruct((T, 2 * W_B), BF16),
                   jax.ShapeDtypeStruct((T, W_B), BF16),
                   jax.ShapeDtypeStruct((T, W_B), BF16),
                   jax.ShapeDtypeStruct((T, LANES), F32),
                   jax.ShapeDtypeStruct((B, GATE_ROWS, S), F32)],
        compiler_params=pltpu.CompilerParams(dimension_semantics=("parallel",), vmem_limit_bytes=VMEM_LIMIT),
        name="inproj",
    )(x2d, g, w_pad, gbias_col)


def _swa_block(q, kvc, kvp, bias_ref, sink_ref):
    kvp = kvp.astype(F32)
    kvc = kvc.astype(F32)
    kband = jnp.concatenate([kvp[:, :W_A_KV], kvc[:, :W_A_KV]], axis=0)
    vband = jnp.concatenate([kvp[:, W_A_KV:], kvc[:, W_A_KV:]], axis=0)
    lane = lax.broadcasted_iota(I32, (2 * BLOCK, LANES), 1)
    lo = lane < HEAD_DIM_A

    def placements(band):
        swapped = pltpu.roll(band, HEAD_DIM_A, axis=1)
        z = jnp.zeros_like(band)
        return {(0, 0): jnp.where(lo, band, z).astype(BF16), (0, 1): jnp.where(lo, z, swapped).astype(BF16),
                (1, 0): jnp.where(lo, swapped, z).astype(BF16), (1, 1): jnp.where(lo, z, band).astype(BF16)}

    kpl = placements(kband)
    vpl = placements(vband)

    scale = HEAD_DIM_A ** -0.5
    group = N_HEADS_A // N_KV_A
    tiles = []
    from_prev = (lax.broadcasted_iota(I32, (BLOCK, BLOCK), 1) > lax.broadcasted_iota(I32, (BLOCK, BLOCK), 0))

    for pair in range(N_HEADS_A // 2):
        qt = q[:, pair * LANES:(pair + 1) * LANES]
        acc = None
        for half in range(2):
            h = 2 * pair + half
            g = h // group
            s2 = lax.dot_general(qt, kpl[(g, half)], (((1,), (1,)), ((), ())), preferred_element_type=F32)
            s = jnp.where(from_prev, s2[:, :BLOCK], s2[:, BLOCK:])
            s = s * scale + bias_ref[h]
            sink = sink_ref[h]
            m = jnp.maximum(jnp.max(s, axis=-1, keepdims=True), sink)
            p = jnp.exp(s - m)
            denom = jnp.sum(p, axis=-1, keepdims=True) + jnp.exp(sink - m)
            p2 = jnp.concatenate([jnp.where(from_prev, p, 0.0), jnp.where(from_prev, 0.0, p)], axis=1)
            o = jnp.dot(p2.astype(BF16), vpl[(g, half)], preferred_element_type=F32) / denom
            acc = o if acc is None else acc + o
        tiles.append(acc.astype(BF16))
    return tiles


CONV_HALO = 16


def _seqmix_kernel(sink_ref, qkc_ref, qkp_ref, vb_ref, ob_ref, gc_ref, gr_ref, qa_ref, kvc_ref, kvp_ref,
                   cw_ref, cb_ref, nrm_ref, bias_ref, o_ref, state_ref, m_ref):
    c = pl.program_id(0)
    B = qkc_ref.shape[0]
    H, D, L = N_HEADS_B, HEAD_DIM_B, CHUNK

    @pl.when(c == 0)
    def _():
        state_ref[...] = jnp.zeros_like(state_ref)
        m_ref[...] = jnp.zeros_like(m_ref)

    rr = lax.broadcasted_iota(I32, (L, CONV_HALO + L), 0)
    cc = lax.broadcasted_iota(I32, (L, CONV_HALO + L), 1)
    shifts = {delay: jnp.where(cc == rr + (CONV_HALO - delay), 1.0, 0.0).astype(BF16)
              for delay in range(1, CONV_WIDTH)}
    ti = lax.broadcasted_iota(I32, (L, L), 0)
    si = lax.broadcasted_iota(I32, (L, L), 1)
    tri = si <= ti
    ones_blk = jnp.ones((L, D), BF16)

    def conv_silu(b):
        prev = qkp_ref[b]
        prev = jnp.where(c > 0, prev, jnp.zeros_like(prev))
        cur = qkc_ref[b]
        ext = jnp.concatenate([prev, cur], axis=0)
        y = cb_ref[...] + cw_ref[CONV_WIDTH - 1:CONV_WIDTH, :] * cur.astype(F32)
        for delay in range(1, CONV_WIDTH):
            tap = CONV_WIDTH - 1 - delay
            y = y + cw_ref[tap:tap + 1, :] * jnp.dot(shifts[delay], ext, preferred_element_type=F32)
        return y * jax.nn.sigmoid(y)

    states = [[state_ref[b, h] for h in range(H)] for b in range(B)]
    m_alls = [m_ref[b] for b in range(B)]
    new_states, new_m, outs = {}, {}, {}

    for b, h in [(b, h) for b in range(B) for h in range(H)]:
        if h == 0:
            for pair, tile in enumerate(_swa_block(qa_ref[b], kvc_ref[b], kvp_ref[b], bias_ref, sink_ref)):
                o_ref[b, :, pair * LANES:(pair + 1) * LANES] = tile
            qk = conv_silu(b)
            gcol = gc_ref[b]
            grow = gr_ref[b]
        qh = (qk[:, h * D:(h + 1) * D] * (D ** -0.5)).astype(BF16)
        k_t = qk[:, W_B + h * D:W_B + (h + 1) * D].T
        v1 = jnp.concatenate([vb_ref[b, :, h * D:(h + 1) * D], ones_blk], axis=-1)
        b_r = grow[H + h:H + h + 1, :]
        g_r = grow[h:h + 1, :] - b_r
        b_c = gcol[:, H + h:H + h + 1]
        m_prev = m_alls[b][h:h + 1, 0:1]
        state = states[b][h]

        gmat = jnp.where(tri, g_r, NEG_INF)
        m_c = jnp.maximum(jnp.max(gmat, axis=-1, keepdims=True), m_prev)
        a_inter = jnp.exp(m_prev - m_c)
        sc = jnp.dot(qh, k_t.astype(BF16), preferred_element_type=F32) * jnp.exp(gmat - m_c)
        tot = (jnp.dot(sc.astype(BF16), v1, preferred_element_type=F32)
               + a_inter * jnp.dot(qh, state.astype(BF16), preferred_element_type=F32))
        num = tot[:, :D]
        den = tot[:, D:]
        hh = num / jnp.maximum(jnp.abs(den), jnp.exp(-(b_c + m_c)))

        b_last = b_r[:, L - 1:L]
        m_new = jnp.maximum(b_last + m_prev, b_last + jnp.max(g_r, axis=-1, keepdims=True))
        w_r = jnp.exp(g_r + (b_last - m_new))
        decay = jnp.exp(b_last + m_prev - m_new)
        upd = jnp.dot((k_t * w_r).astype(BF16), v1, preferred_element_type=F32)
        new_states[b, h] = decay * state + upd
        new_m[b, h] = jnp.broadcast_to(m_new, (1, LANES))

        og = jax.nn.sigmoid(ob_ref[b, :, h * D:(h + 1) * D].astype(F32))
        hb = og * hh
        hb = hb * lax.rsqrt(jnp.mean(hb * hb, axis=-1, keepdims=True) + EPS)
        outs[b, h] = (hb * nrm_ref[:, h * D:(h + 1) * D]).astype(BF16)

    for b, h in new_states:
        state_ref[b, h] = new_states[b, h]
        m_ref[b, h:h + 1, :] = new_m[b, h]
        o_ref[b, :, W_A_Q + h * D:W_A_Q + (h + 1) * D] = outs[b, h]


def _seqmix(qkb, vb, ob, gc, gr, qa, kva, conv_w, conv_b, nrm, bias, sinks):
    assert CHUNK == BLOCK
    B, S, _ = qkb.shape
    nc = S // CHUNK
    halo_per_chunk = CHUNK // CONV_HALO
    blk = lambda w: pl.BlockSpec((B, CHUNK, w), lambda c: (0, c, 0))
    full = lambda a: pl.BlockSpec(a.shape, lambda c: (0,) * a.ndim)
    return pl.pallas_call(
        _seqmix_kernel,
        grid=(nc,),
        in_specs=[pl.BlockSpec(memory_space=pltpu.SMEM),
                  blk(2 * W_B),
                  pl.BlockSpec((B, CONV_HALO, 2 * W_B), lambda c: (0, jnp.maximum(c * halo_per_chunk - 1, 0), 0)),
                  blk(W_B), blk(W_B), blk(LANES),
                  pl.BlockSpec((B, GATE_ROWS, CHUNK), lambda c: (0, 0, c)),
                  blk(W_A_Q), blk(2 * W_A_KV),
                  pl.BlockSpec((B, BLOCK, 2 * W_A_KV), lambda c: (0, jnp.maximum(c - 1, 0), 0)),
                  full(conv_w), full(conv_b), full(nrm),
                  pl.BlockSpec((None,) + bias.shape[1:], lambda c: (jnp.minimum(c, 1), 0, 0, 0))],
        out_specs=blk(W_A_Q + W_B),
        out_shape=jax.ShapeDtypeStruct((B, S, W_A_Q + W_B), BF16),
        scratch_shapes=[pltpu.VMEM((B, N_HEADS_B, HEAD_DIM_B, 2 * HEAD_DIM_B), F32),
                        pltpu.VMEM((B, GATE_ROWS, LANES), F32)],
        compiler_params=pltpu.CompilerParams(dimension_semantics=("arbitrary",), vmem_limit_bytes=VMEM_LIMIT),
        name="seqmix",
    )(sinks, qkb, qkb, vb, ob, gc, gr, qa, kva, kva, conv_w, conv_b, nrm, bias)


def _memkv_kernel(mem_ref, g_ref, wk_ref, wv_ref, k_ref, v_ref):
    hm = _rms(mem_ref[...], g_ref[...]).astype(BF16)
    k_ref[...] = jnp.dot(hm, wk_ref[...], preferred_element_type=F32).astype(BF16)
    v_ref[...] = jnp.dot(hm, wv_ref[...], preferred_element_type=F32).astype(BF16)


def _memkv(mem2d, g, wk, wv, B):
    full = lambda a: pl.BlockSpec(a.shape, lambda b: (0,) * a.ndim)
    blk = pl.BlockSpec((N_MEM, D_MODEL), lambda b: (b, 0))
    return pl.pallas_call(
        _memkv_kernel,
        grid=(B,),
        in_specs=[blk, full(g), full(wk), full(wv)],
        out_specs=[blk, blk],
        out_shape=[jax.ShapeDtypeStruct((B * N_MEM, D_MODEL), BF16)] * 2,
        compiler_params=pltpu.CompilerParams(dimension_semantics=("parallel",), vmem_limit_bytes=VMEM_LIMIT),
        name="memkv",
    )(mem2d, g, wk, wv)


ROUTE_E1, ROUTE_E2, ROUTE_G1, ROUTE_G2, ROUTE_R1, ROUTE_R2 = 0, 1, 2, 3, 4, 5
ROUTER_GROUP_COL = N_EXPERTS
ROUTER_ROWS = -(-(N_EXPERTS + N_GROUPS) // SUBLANES) * SUBLANES


def _mid_kernel(x_ref, mix_ref, wo_ref, gx_ref, wq_ref, ck_ref, cv_ref, wco_ref, gz_ref, wr_ref, br_ref,
                x2_ref, hz_ref, route_ref, rrows_ref, counts_ref, cnt_ref):
    @pl.when(pl.program_id(0) == 0)
    def _():
        cnt_ref[...] = jnp.zeros_like(cnt_ref)

    x1 = x_ref[...] + jnp.dot(mix_ref[...], wo_ref[...], preferred_element_type=F32)

    hc = _rms(x1, gx_ref[...]).astype(BF16)
    cq = jnp.dot(hc, wq_ref[...], preferred_element_type=F32).astype(BF16)
    scale = HEAD_DIM_X ** -0.5
    heads = []
    for h in range(N_HEADS_X):
        sl = slice(h * HEAD_DIM_X, (h + 1) * HEAD_DIM_X)
        s = lax.dot_general(cq[:, sl], ck_ref[:, sl], (((1,), (1,)), ((), ())), preferred_element_type=F32) * scale
        p = jnp.exp(s - jnp.max(s, axis=-1, keepdims=True))
        co = jnp.dot(p.astype(BF16), cv_ref[:, sl], preferred_element_type=F32) / jnp.sum(p, axis=-1, keepdims=True)
        heads.append(co.astype(BF16))
    x2 = x1 + jnp.dot(jnp.concatenate(heads, axis=-1), wco_ref[...], preferred_element_type=F32)
    x2_ref[...] = x2

    hz = _rms(x2, gz_ref[...])
    hz_ref[...] = _pack_halves(hz)
    lg = jnp.dot(hz.astype(BF16), wr_ref[...], preferred_element_type=F32) + br_ref[...]
    tm = lg.shape[0]
    lt = jnp.transpose(lg)[0:ROUTER_ROWS, :]
    row = lax.broadcasted_iota(I32, lt.shape, 0)
    big = jnp.int32(ROUTER_ROWS)
    is_g = (row >= ROUTER_GROUP_COL) & (row < ROUTER_GROUP_COL + N_GROUPS)
    gl = jnp.where(is_g, lt, NEG_INF)
    gmax = jnp.max(gl, axis=0, keepdims=True)
    gsum = jnp.sum(jnp.exp(gl - gmax), axis=0, keepdims=True)
    g_prob = 1.0 / gsum
    g_idx = jnp.min(jnp.where(gl == gmax, row - ROUTER_GROUP_COL, big), axis=0, keepdims=True)
    sel = (row < N_EXPERTS) & ((row // EXPERTS_PER_GROUP) == g_idx)
    el = jnp.where(sel, lt, NEG_INF)
    m1 = jnp.max(el, axis=0, keepdims=True)
    i1 = jnp.min(jnp.where(el == m1, row, big), axis=0, keepdims=True)
    el2 = jnp.where(row == i1, NEG_INF, el)
    m2 = jnp.max(el2, axis=0, keepdims=True)
    i2 = jnp.min(jnp.where(el2 == m2, row, big), axis=0, keepdims=True)
    z = jnp.sum(jnp.exp(el - m1), axis=0, keepdims=True)
    p1 = 1.0 / z
    p2 = jnp.exp(m2 - m1) / z
    g1 = g_prob * (p1 / (p1 + p2))
    g2 = g_prob * (p2 / (p1 + p2))

    used = jnp.where((row == i1) | (row == i2), 1.0, 0.0)
    t_from = lax.broadcasted_iota(I32, (tm, tm), 0)
    t_to = lax.broadcasted_iota(I32, (tm, tm), 1)
    earlier = jnp.where(t_from < t_to, 1.0, 0.0).astype(BF16)
    before = jnp.dot(used.astype(BF16), earlier, preferred_element_type=F32) + cnt_ref[:, 0:1]
    r1 = jnp.sum(jnp.where(row == i1, before, 0.0), axis=0, keepdims=True)
    r2 = jnp.sum(jnp.where(row == i2, before, 0.0), axis=0, keepdims=True)
    cnt_ref[...] = cnt_ref[...] + jnp.sum(used, axis=1, keepdims=True)
    counts_ref[...] = cnt_ref[...]

    rec_row = lax.broadcasted_iota(I32, (SUBLANES, tm), 0)
    rec = jnp.zeros((SUBLANES, tm), F32)
    for c, v in ((ROUTE_E1, i1.astype(F32)), (ROUTE_E2, i2.astype(F32)), (ROUTE_G1, g1), (ROUTE_G2, g2),
                 (ROUTE_R1, r1), (ROUTE_R2, r2)):
        rec = jnp.where(rec_row == c, v, rec)
    rrows_ref[...] = rec
    route_ref[...] = jnp.transpose(jnp.concatenate([rec, jnp.zeros((LANES - SUBLANES, tm), F32)], axis=0))


def _mid(x2d, mix, wo, gx, wq, ck, cv, wco, gz, wr, br, B, S):
    T = B * S
    tm = min(TM_MID, S)
    per_b = S // tm
    row = lambda w: pl.BlockSpec((tm, w), lambda i: (i, 0))
    full = lambda a: pl.BlockSpec(a.shape, lambda i: (0,) * a.ndim)
    kvspec = pl.BlockSpec((N_MEM, D_MODEL), lambda i: (i // per_b, 0))
    return pl.pallas_call(
        _mid_kernel,
        grid=(T // tm,),
        in_specs=[row(D_MODEL), row(W_A_Q + W_B), full(wo), full(gx), full(wq), kvspec, kvspec,
                  full(wco), full(gz), full(wr), full(br)],
        out_specs=[row(D_MODEL), row(HALF), row(LANES), pl.BlockSpec((SUBLANES, tm), lambda i: (0, i)),
                   pl.BlockSpec((ROUTER_ROWS, LANES), lambda i: (0, 0))],
        out_shape=[jax.ShapeDtypeStruct((T, D_MODEL), F32),
                   jax.ShapeDtypeStruct((T, HALF), U32),
                   jax.ShapeDtypeStruct((T, LANES), F32),
                   jax.ShapeDtypeStruct((SUBLANES, T), F32),
                   jax.ShapeDtypeStruct((ROUTER_ROWS, LANES), F32)],
        scratch_shapes=[pltpu.VMEM((ROUTER_ROWS, LANES), F32)],
        compiler_params=pltpu.CompilerParams(dimension_semantics=("arbitrary",), vmem_limit_bytes=VMEM_LIMIT),
        name="mid",
    )(x2d, mix, wo, gx, wq, ck, cv, wco, gz, wr, br)


def _dispatch_kernel(zf_ref, pos_ref, hz_ref, xs_hbm, zbuf, sem, zsem):
    i = pl.program_id(0)
    tm = hz_ref.shape[0]
    zrows = zbuf.shape[0]

    @pl.when(i == 0)
    def _():
        zbuf[...] = jnp.zeros_like(zbuf)

        def fill(t):
            return pltpu.make_async_copy(zbuf, xs_hbm.at[pl.ds(t * zrows, zrows)], zsem)

        def start(t, carry):
            @pl.when(zf_ref[t] != 0)
            def _():
                fill(t).start()
            return carry

        def wait(t, carry):
            @pl.when(zf_ref[t] != 0)
            def _():
                fill(t).wait()
            return carry

        lax.fori_loop(0, zf_ref.shape[0], start, 0)
        lax.fori_loop(0, zf_ref.shape[0], wait, 0)

    for r in range(tm):
        for k in range(TOP_K):
            pltpu.make_async_copy(hz_ref.at[pl.ds(r, 1)], xs_hbm.at[pl.ds(pos_ref[0, 0, k * tm + r], 1)],
                                  sem).start(priority=k % 2)
    for k in range(TOP_K):
        pltpu.make_async_copy(hz_ref, xs_hbm.at[pl.ds(0, tm)], sem).wait()


def _dispatch(hz_packed, pos, zfill, n_slots, tmx):
    T = hz_packed.shape[0]
    tm = pos.shape[2] // TOP_K
    grid_spec = pltpu.PrefetchScalarGridSpec(
        num_scalar_prefetch=1,
        grid=(T // tm,),
        in_specs=[pl.BlockSpec((1, 1, TOP_K * tm), lambda i, zf: (i, 0, 0), memory_space=pltpu.SMEM),
                  pl.BlockSpec((tm, HALF), lambda i, zf: (i, 0))],
        out_specs=pl.BlockSpec(memory_space=pl.ANY),
        scratch_shapes=[pltpu.VMEM((tmx, HALF), U32), pltpu.SemaphoreType.DMA(()), pltpu.SemaphoreType.DMA(())],
    )
    return pl.pallas_call(
        _dispatch_kernel,
        grid_spec=grid_spec,
        out_shape=jax.ShapeDtypeStruct((n_slots, HALF), U32),
        compiler_params=pltpu.CompilerParams(dimension_semantics=("arbitrary",), vmem_limit_bytes=VMEM_LIMIT),
        name="dispatch",
    )(zfill, pos, hz_packed)


def _expert_kernel(te_ref, nt_ref, first_ref, slot_ref, next_ref, xs_ref, wg_hbm, wu_hbm, wd_hbm, ys_ref,
                   wg32, wu32, wd32, wgb, wub, wdb, wsem):
    i = pl.program_id(0)
    nt = nt_ref[0]

    def fetch(e, s):
        return [pltpu.make_async_copy(src.at[e], dst.at[s], wsem.at[s])
                for src, dst in ((wg_hbm, wg32), (wu_hbm, wu32), (wd_hbm, wd32))]

    @pl.when(i < nt)
    def _():
        @pl.when(i == 0)
        def _():
            for cp in fetch(te_ref[0], 0):
                cp.start()

        @pl.when(first_ref[i] != 0)
        def _():
            s = slot_ref[i]
            for cp in fetch(te_ref[i], s):
                cp.wait()

            @pl.when(next_ref[i] >= 0)
            def _():
                for cp in fetch(next_ref[i], 1 - s):
                    cp.start()

            wgb[...] = wg32[s].astype(BF16)
            wub[...] = wu32[s].astype(BF16)
            wdb[...] = wd32[s].astype(BF16)

        x = _unpack_halves(xs_ref[...]).astype(BF16)
        hg = jnp.dot(x, wgb[...], preferred_element_type=F32)
        hu = jnp.dot(x, wub[...], preferred_element_type=F32)
        a = (hg * jax.nn.sigmoid(hg) * hu).astype(BF16)
        ys_ref[...] = _pack_halves(jnp.dot(a, wdb[...], preferred_element_type=F32))

    @pl.when(i >= nt)
    def _():
        ys_ref[...] = jnp.zeros_like(ys_ref)


def _experts(xs, w_gate, w_up, w_down, tile_expert, ntiles, run_first, run_slot, run_next, tmx):
    n_tiles_max = tile_expert.shape[0]
    hbm = pl.BlockSpec(memory_space=pl.ANY)
    grid_spec = pltpu.PrefetchScalarGridSpec(
        num_scalar_prefetch=5,
        grid=(n_tiles_max,),
        in_specs=[pl.BlockSpec((tmx, HALF), lambda i, te, nt, *_: (jnp.minimum(i, nt[0] - 1), 0)), hbm, hbm, hbm],
        out_specs=pl.BlockSpec((tmx, HALF), lambda i, *_: (i, 0)),
        scratch_shapes=[pltpu.VMEM((2, D_MODEL, D_EXPERT), F32),
                        pltpu.VMEM((2, D_MODEL, D_EXPERT), F32),
                        pltpu.VMEM((2, D_EXPERT, D_MODEL), F32),
                        pltpu.VMEM((D_MODEL, D_EXPERT), BF16),
                        pltpu.VMEM((D_MODEL, D_EXPERT), BF16),
                        pltpu.VMEM((D_EXPERT, D_MODEL), BF16),
                        pltpu.SemaphoreType.DMA((2,))],
    )
    return pl.pallas_call(
        _expert_kernel,
        grid_spec=grid_spec,
        out_shape=jax.ShapeDtypeStruct(xs.shape, U32),
        compiler_params=pltpu.CompilerParams(dimension_semantics=("arbitrary",), vmem_limit_bytes=VMEM_LIMIT),
        name="experts",
    )(tile_expert, ntiles, run_first, run_slot, run_next, xs, w_gate, w_up, w_down)


def _final_kernel(posc_ref, posn_ref, x2_ref, route_ref, g_ref, ys_hbm, o_ref, ybuf, sem):
    i = pl.program_id(0)
    n = pl.num_programs(0)
    tm = x2_ref.shape[0]
    slot = i % 2

    def issue(pos_ref, s):
        for r in range(tm):
            for k in range(TOP_K):
                pltpu.make_async_copy(ys_hbm.at[pl.ds(pos_ref[0, 0, k * tm + r], 1)],
                                      ybuf.at[s, k, pl.ds(r, 1)], sem.at[s]).start(priority=k % 2)

    def wait(s):
        for k in range(TOP_K):
            pltpu.make_async_copy(ys_hbm.at[pl.ds(0, tm)], ybuf.at[s, k], sem.at[s]).wait()

    @pl.when(i == 0)
    def _():
        issue(posc_ref, 0)

    wait(slot)

    for s in range(2):
        @pl.when(slot == s)
        def _():
            issue(posn_ref, 1 - s)

    r = route_ref[...]
    g1 = r[:, ROUTE_G1:ROUTE_G1 + 1]
    g2 = r[:, ROUTE_G2:ROUTE_G2 + 1]
    xo = x2_ref[...] + g1 * _unpack_halves(ybuf[slot, 0]) + g2 * _unpack_halves(ybuf[slot, 1])
    o_ref[...] = _rms(xo, g_ref[...])

    @pl.when(i == n - 1)
    def _():
        wait(1 - slot)


def _final(x2, ys, pos, route, g):
    T = x2.shape[0]
    nblk = pos.shape[0]
    tm = T // nblk
    row = lambda w: pl.BlockSpec((tm, w), lambda i: (i, 0))
    return pl.pallas_call(
        _final_kernel,
        grid=(nblk,),
        in_specs=[pl.BlockSpec((1, 1, TOP_K * tm), lambda i: (i, 0, 0), memory_space=pltpu.SMEM),
                  pl.BlockSpec((1, 1, TOP_K * tm), lambda i: (jnp.minimum(i + 1, nblk - 1), 0, 0),
                               memory_space=pltpu.SMEM),
                  row(D_MODEL), row(LANES), pl.BlockSpec(g.shape, lambda i: (0, 0)),
                  pl.BlockSpec(memory_space=pl.ANY)],
        out_specs=row(D_MODEL),
        out_shape=jax.ShapeDtypeStruct((T, D_MODEL), F32),
        scratch_shapes=[pltpu.VMEM((2, TOP_K, tm, HALF), U32), pltpu.SemaphoreType.DMA((2,))],
        compiler_params=pltpu.CompilerParams(dimension_semantics=("arbitrary",), vmem_limit_bytes=VMEM_LIMIT),
        name="final",
    )(pos, pos, x2, route, g, ys)


def _band_bias(table):
    assert WINDOW == BLOCK
    i = jnp.arange(BLOCK)[:, None]
    j = jnp.arange(2 * BLOCK)[None, :]
    n = jnp.maximum(i + BLOCK - j, 0)
    nf = jnp.maximum(n, 1).astype(F32)
    large = MAX_EXACT + (jnp.log(nf / MAX_EXACT) / math.log(MAX_DISTANCE / MAX_EXACT)
                         * (NUM_BUCKETS - MAX_EXACT)).astype(I32)
    large = jnp.minimum(large, NUM_BUCKETS - 1)
    bucket = jnp.where(n < MAX_EXACT, n, large)
    onehot = (bucket[:, :, None] == jnp.arange(NUM_BUCKETS)[None, None, :]).astype(F32)
    bias = jnp.einsum("ijb,bh->hij", onehot, table.astype(F32), precision=lax.Precision.HIGHEST)
    from_prev = (jnp.arange(BLOCK)[None, :] > i)[None]
    prev, cur = bias[:, :, :BLOCK], bias[:, :, BLOCK:]
    return jnp.stack([jnp.where(from_prev, NEG_INF, cur), jnp.where(from_prev, prev, cur)])


def _dispatch_plan(route_rows, counts_f, tmx, n_tiles_max, tm_rows):
    T = route_rows.shape[1]
    experts = jnp.arange(N_EXPERTS, dtype=I32)
    counts = counts_f[:N_EXPERTS, 0].astype(I32)
    ptiles = (counts + tmx - 1) // tmx
    tile_end = jnp.cumsum(ptiles)
    nt = tile_end[-1]
    row_off = (tile_end - ptiles) * tmx

    def slot(e_row, r_row):
        e = route_rows[e_row].astype(I32)
        off = jnp.sum(jnp.where(e[None, :] == experts[:, None], row_off[:, None], 0), axis=0)
        return (off + route_rows[r_row].astype(I32)).reshape(T // tm_rows, 1, tm_rows)

    pos = jnp.concatenate([slot(ROUTE_E1, ROUTE_R1), slot(ROUTE_E2, ROUTE_R2)], axis=2)

    tile_ids = jnp.arange(n_tiles_max, dtype=I32)
    expert_of = lambda t: jnp.sum((tile_end[None, :] <= t[:, None]).astype(I32), axis=1)
    te = expert_of(jnp.minimum(tile_ids, nt - 1))
    partial = jnp.any((tile_ids[:, None] == (tile_end - 1)[None, :]) & (counts % tmx != 0)[None, :], axis=1)
    zfill = (partial | (tile_ids >= nt)).astype(I32)

    used = ptiles > 0
    run_first = (jnp.any((tile_ids[:, None] == (tile_end - ptiles)[None, :]) & used[None, :], axis=1)
                 & (tile_ids < nt)).astype(I32)
    run_slot = (jnp.cumsum(run_first) - 1) % 2
    later_used = used[None, :] & (experts[None, :] > experts[:, None])
    next_of = jnp.min(jnp.where(later_used, experts[None, :], N_EXPERTS), axis=1)
    next_of = jnp.where(next_of < N_EXPERTS, next_of, -1)
    run_next = jnp.sum(jnp.where(te[:, None] == experts[None, :], next_of[None, :], 0), axis=1)
    return pos, te, nt.reshape(1), zfill, run_first, run_slot.astype(I32), run_next.astype(I32)


def kernel(x, mem, rel_bias_table, norm_mix, w_in, attn_sinks, conv_w, conv_b, gate_bias_i, gate_bias_f, mlstm_norm, w_out, norm_cross, norm_mem, w_cq, w_ck, w_cv, w_co, norm_moe, w_router_group, b_router_group, w_router_expert, b_router_expert, w_exp_gate, w_exp_up, w_exp_down, norm_final):
    B, S, _ = x.shape
    T = B * S
    depth = w_in.shape[0]
    x2d = x.reshape(T, D_MODEL)
    mem2d = mem.reshape(B * N_MEM, D_MODEL)
    bias = _band_bias(rel_bias_table)

    tmx = min(TM_EXPERT, T)
    n_tiles_max = (T * TOP_K) // tmx + N_EXPERTS
    tm_rows = min(TM_ROWDMA, T)

    assert depth == 1, "the final combine is fused with the final norm: single layer only"
    l = 0
    w_pad = jnp.pad(w_in[l], ((0, 0), (0, C_GATE + LANES - D_IN))).astype(BF16)
    gb = jnp.concatenate([gate_bias_i[l], gate_bias_f[l]]).astype(F32)
    gbias_col = jnp.pad(gb, (0, LANES - GATE_ROWS))[None, :]
    qa, kva, qkb, vb, ob, gc, gr = _inproj(x2d, norm_mix[l][None, :], w_pad, gbias_col, B, S)

    per_seq = lambda a: a.reshape(B, S, a.shape[-1])
    mix = _seqmix(per_seq(qkb), per_seq(vb), per_seq(ob), per_seq(gc), gr, per_seq(qa), per_seq(kva),
                  conv_w[l][:, 0, :].astype(F32), conv_b[l][None, :].astype(F32),
                  mlstm_norm[l][None, :].astype(F32), bias, attn_sinks[l].astype(F32)).reshape(T, W_A_Q + W_B)

    ck, cv = _memkv(mem2d, norm_mem[l][None, :], w_ck[l].astype(BF16), w_cv[l].astype(BF16), B)

    wr = jnp.pad(jnp.concatenate([w_router_expert[l], w_router_group[l]], axis=1),
                 ((0, 0), (0, LANES - N_EXPERTS - N_GROUPS))).astype(BF16)
    br = jnp.pad(jnp.concatenate([b_router_expert[l], b_router_group[l]]),
                 (0, LANES - N_EXPERTS - N_GROUPS)).astype(F32)[None, :]
    x2, hz_packed, route, route_rows, counts = _mid(
        x2d, mix, w_out[l].astype(BF16), norm_cross[l][None, :], w_cq[l].astype(BF16), ck, cv,
        w_co[l].astype(BF16), norm_moe[l][None, :], wr, br, B, S)

    pos, te, nt, zfill, run_first, run_slot, run_next = _dispatch_plan(route_rows, counts, tmx, n_tiles_max,
                                                                       tm_rows)
    xs = _dispatch(hz_packed, pos, zfill, n_tiles_max * tmx, tmx)
    ys = _experts(xs, w_exp_gate[l], w_exp_up[l], w_exp_down[l], te, nt, run_first, run_slot, run_next, tmx)
    out = _final(x2, ys, pos, route, norm_final[None, :])
    return out.reshape(B, S, D_MODEL)
```

```python
import math

import jax
import jax.numpy as jnp
from jax import lax
from jax.experimental import pallas as pl
from jax.experimental.pallas import tpu as pltpu

F32 = jnp.float32
BF16 = jnp.bfloat16
U32 = jnp.uint32
I32 = jnp.int32

D_MODEL = 1024
N_MEM = 256
N_HEADS_A = 8
N_KV_A = 2
HEAD_DIM_A = 64
BLOCK = 128
WINDOW = 128
NUM_BUCKETS = 32
MAX_EXACT = NUM_BUCKETS // 2
MAX_DISTANCE = 128
N_HEADS_B = 4
HEAD_DIM_B = 128
CHUNK = 128
CONV_WIDTH = 4
N_HEADS_X = 4
HEAD_DIM_X = D_MODEL // N_HEADS_X
N_GROUPS = 4
EXPERTS_PER_GROUP = 8
N_EXPERTS = N_GROUPS * EXPERTS_PER_GROUP
TOP_K = 2
D_EXPERT = 512
EPS = 1e-6
NEG_INF = -1e30

W_A_Q = N_HEADS_A * HEAD_DIM_A
W_A_KV = N_KV_A * HEAD_DIM_A
W_B = N_HEADS_B * HEAD_DIM_B
C_QA = 0
C_KVA = C_QA + W_A_Q
C_QKB = C_KVA + 2 * W_A_KV
C_VB = C_QKB + 2 * W_B
C_OB = C_VB + W_B
C_GATE = C_OB + W_B
D_IN = C_GATE + 2 * N_HEADS_B

LANES = 128
SUBLANES = 8
GATE_ROWS = 8
HALF = D_MODEL // 2

TM_INPROJ = 1024
TM_MID = 1024
TM_ROWDMA = 512
TM_EXPERT = 512

VMEM_LIMIT = 48 * 1024 * 1024


def _rms(xf, g):
    return xf * lax.rsqrt(jnp.mean(xf * xf, axis=-1, keepdims=True) + EPS) * g


def _pack_halves(v):
    b = pltpu.bitcast(v.astype(BF16).astype(F32), U32)
    return (b[:, :HALF] >> 16) | b[:, HALF:]


def _unpack_halves(p):
    lo = pltpu.bitcast(p << 16, F32)
    hi = pltpu.bitcast(p & jnp.uint32(0xFFFF0000), F32)
    return jnp.concatenate([lo, hi], axis=-1)


def _log_sigmoid(z):
    return jnp.minimum(z, 0.0) - jnp.log1p(jnp.exp(-jnp.abs(z)))


def _split3(v):
    hi = v.astype(BF16).astype(F32)
    rest = v - hi
    mid = rest.astype(BF16).astype(F32)
    return hi, mid, (rest - mid).astype(BF16).astype(F32)


def _inproj_kernel(x_ref, g_ref, w_ref, gbc_ref, qa_ref, kva_ref, qkb_ref, vb_ref, ob_ref, gc_ref, gr_ref):
    tm = x_ref.shape[0]
    h = _rms(x_ref[...], g_ref[...]).astype(BF16)

    def mm(lo, hi):
        return jnp.dot(h, w_ref[:, lo:hi], preferred_element_type=F32)

    qa_ref[...] = mm(C_QA, C_KVA).astype(BF16)
    kva_ref[...] = mm(C_KVA, C_QKB).astype(BF16)
    qkb_ref[...] = mm(C_QKB, C_VB).astype(BF16)
    vb_ref[...] = mm(C_VB, C_OB).astype(BF16)
    ob_ref[...] = mm(C_OB, C_GATE).astype(BF16)

    H, L = N_HEADS_B, CHUNK
    gcol = mm(C_GATE, C_GATE + LANES) + gbc_ref[...]
    grow = jnp.transpose(gcol)[0:GATE_ROWS, :]
    lane_c = lax.broadcasted_iota(I32, (L, LANES), 1)
    is_f_col = (lane_c >= H) & (lane_c < 2 * H)
    is_f_row = lax.broadcasted_iota(I32, (GATE_ROWS, L), 0) >= H
    ti = lax.broadcasted_iota(I32, (L, L), 0)
    si = lax.broadcasted_iota(I32, (L, L), 1)
    tril = jnp.where(si <= ti, 1.0, 0.0).astype(BF16)
    triu = jnp.where(si >= ti, 1.0, 0.0).astype(BF16)
    for c in range(tm // L):
        rows = slice(c * L, (c + 1) * L)
        gcol_c = gcol[rows, :]
        fcol = jnp.where(is_f_col, _log_sigmoid(gcol_c), 0.0)
        parts = jnp.dot(tril, jnp.concatenate(_split3(fcol), axis=1).astype(BF16), preferred_element_type=F32)
        bcol = parts[:, :LANES] + parts[:, LANES:2 * LANES] + parts[:, 2 * LANES:]
        gc_ref[rows, :] = jnp.where(is_f_col, bcol, gcol_c)
        grow_c = grow[:, rows]
        frow = jnp.where(is_f_row, _log_sigmoid(grow_c), 0.0)
        parts = jnp.dot(jnp.concatenate(_split3(frow), axis=0).astype(BF16), triu, preferred_element_type=F32)
        brow = parts[:GATE_ROWS] + parts[GATE_ROWS:2 * GATE_ROWS] + parts[2 * GATE_ROWS:]
        gr_ref[:, rows] = jnp.where(is_f_row, brow, grow_c)


def _inproj(x2d, g, w_pad, gbias_col, B, S):
    T = x2d.shape[0]
    tm = min(TM_INPROJ, S)
    tiles_per_seq = S // tm
    row = lambda w: pl.BlockSpec((tm, w), lambda i: (i, 0))
    full = lambda a: pl.BlockSpec(a.shape, lambda i: (0,) * a.ndim)
    return pl.pallas_call(
        _inproj_kernel,
        grid=(T // tm,),
        in_specs=[row(D_MODEL), full(g), full(w_pad), full(gbias_col)],
        out_specs=[row(W_A_Q), row(2 * W_A_KV), row(2 * W_B), row(W_B), row(W_B), row(LANES),
                   pl.BlockSpec((None, GATE_ROWS, tm), lambda i: (i // tiles_per_seq, 0, i % tiles_per_seq))],
        out_shape=[jax.ShapeDtypeStruct((T, W_A_Q), BF16),
                   jax.ShapeDtypeStruct((T, 2 * W_A_KV), BF16),
                   jax.ShapeDtypeStruct((T, 2 * W_B), BF16),
                   jax.ShapeDtypeStruct((T, W_B), BF16),
                   jax.ShapeDtypeStruct((T, W_B), BF16),
                   jax.ShapeDtypeStruct((T, LANES), F32),
                   jax.ShapeDtypeStruct((B, GATE_ROWS, S), F32)],
        compiler_params=pltpu.CompilerParams(dimension_semantics=("parallel",), vmem_limit_bytes=VMEM_LIMIT),
        name="inproj",
    )(x2d, g, w_pad, gbias_col)


def _swa_block(q, kvc, kvp, bias_ref, sink_ref):
    kvp = kvp.astype(F32)
    kvc = kvc.astype(F32)
    kband = jnp.concatenate([kvp[:, :W_A_KV], kvc[:, :W_A_KV]], axis=0)
    vband = jnp.concatenate([kvp[:, W_A_KV:], kvc[:, W_A_KV:]], axis=0)
    lane = lax.broadcasted_iota(I32, (2 * BLOCK, LANES), 1)
    lo = lane < HEAD_DIM_A

    def placements(band):
        swapped = pltpu.roll(band, HEAD_DIM_A, axis=1)
        z = jnp.zeros_like(band)
        return {(0, 0): jnp.where(lo, band, z).astype(BF16), (0, 1): jnp.where(lo, z, swapped).astype(BF16),
                (1, 0): jnp.where(lo, swapped, z).astype(BF16), (1, 1): jnp.where(lo, z, band).astype(BF16)}

    kpl = placements(kband)
    vpl = placements(vband)

    scale = HEAD_DIM_A ** -0.5
    group = N_HEADS_A // N_KV_A
    tiles = []
    from_prev = (lax.broadcasted_iota(I32, (BLOCK, BLOCK), 1) > lax.broadcasted_iota(I32, (BLOCK, BLOCK), 0))

    for pair in range(N_HEADS_A // 2):
        qt = q[:, pair * LANES:(pair + 1) * LANES]
        acc = None
        for half in range(2):
            h = 2 * pair + half
            g = h // group
            s2 = lax.dot_general(qt, kpl[(g, half)], (((1,), (1,)), ((), ())), preferred_element_type=F32)
            s = jnp.where(from_prev, s2[:, :BLOCK], s2[:, BLOCK:])
            s = s * scale + bias_ref[h]
            sink = sink_ref[h]
            m = jnp.maximum(jnp.max(s, axis=-1, keepdims=True), sink)
            p = jnp.exp(s - m)
            denom = jnp.sum(p, axis=-1, keepdims=True) + jnp.exp(sink - m)
            p2 = jnp.concatenate([jnp.where(from_prev, p, 0.0), jnp.where(from_prev, 0.0, p)], axis=1)
            o = jnp.dot(p2.astype(BF16), vpl[(g, half)], preferred_element_type=F32) / denom
            acc = o if acc is None else acc + o
        tiles.append(acc.astype(BF16))
    return tiles


CONV_HALO = 16


def _seqmix_kernel(sink_ref, qkc_ref, qkp_ref, vb_ref, ob_ref, gc_ref, gr_ref, qa_ref, kvc_ref, kvp_ref,
                   cw_ref, cb_ref, nrm_ref, bias_ref, o_ref, state_ref, m_ref):
    c = pl.program_id(0)
    B = qkc_ref.shape[0]
    H, D, L = N_HEADS_B, HEAD_DIM_B, CHUNK

    @pl.when(c == 0)
    def _():
        state_ref[...] = jnp.zeros_like(state_ref)
        m_ref[...] = jnp.zeros_like(m_ref)

    rr = lax.broadcasted_iota(I32, (L, CONV_HALO + L), 0)
    cc = lax.broadcasted_iota(I32, (L, CONV_HALO + L), 1)
    shifts = {delay: jnp.where(cc == rr + (CONV_HALO - delay), 1.0, 0.0).astype(BF16)
              for delay in range(1, CONV_WIDTH)}
    ti = lax.broadcasted_iota(I32, (L, L), 0)
    si = lax.broadcasted_iota(I32, (L, L), 1)
    tri = si <= ti
    ones_blk = jnp.ones((L, D), BF16)

    def conv_silu(b):
        prev = qkp_ref[b]
        prev = jnp.where(c > 0, prev, jnp.zeros_like(prev))
        cur = qkc_ref[b]
        ext = jnp.concatenate([prev, cur], axis=0)
        y = cb_ref[...] + cw_ref[CONV_WIDTH - 1:CONV_WIDTH, :] * cur.astype(F32)
        for delay in range(1, CONV_WIDTH):
            tap = CONV_WIDTH - 1 - delay
            y = y + cw_ref[tap:tap + 1, :] * jnp.dot(shifts[delay], ext, preferred_element_type=F32)
        return y * jax.nn.sigmoid(y)

    states = [[state_ref[b, h] for h in range(H)] for b in range(B)]
    m_alls = [m_ref[b] for b in range(B)]
    new_states, new_m, outs = {}, {}, {}

    for b, h in [(b, h) for b in range(B) for h in range(H)]:
        if h == 0:
            for pair, tile in enumerate(_swa_block(qa_ref[b], kvc_ref[b], kvp_ref[b], bias_ref, sink_ref)):
                o_ref[b, :, pair * LANES:(pair + 1) * LANES] = tile
            qk = conv_silu(b)
            gcol = gc_ref[b]
            grow = gr_ref[b]
        qh = (qk[:, h * D:(h + 1) * D] * (D ** -0.5)).astype(BF16)
        k_t = qk[:, W_B + h * D:W_B + (h + 1) * D].T
        v1 = jnp.concatenate([vb_ref[b, :, h * D:(h + 1) * D], ones_blk], axis=-1)
        b_r = grow[H + h:H + h + 1, :]
        g_r = grow[h:h + 1, :] - b_r
        b_c = gcol[:, H + h:H + h + 1]
        m_prev = m_alls[b][h:h + 1, 0:1]
        state = states[b][h]

        gmat = jnp.where(tri, g_r, NEG_INF)
        m_c = jnp.maximum(jnp.max(gmat, axis=-1, keepdims=True), m_prev)
        a_inter = jnp.exp(m_prev - m_c)
        sc = jnp.dot(qh, k_t.astype(BF16), preferred_element_type=F32) * jnp.exp(gmat - m_c)
        tot = (jnp.dot(sc.astype(BF16), v1, preferred_element_type=F32)
               + a_inter * jnp.dot(qh, state.astype(BF16), preferred_element_type=F32))
        num = tot[:, :D]
        den = tot[:, D:]
        hh = num / jnp.maximum(jnp.abs(den), jnp.exp(-(b_c + m_c)))

        b_last = b_r[:, L - 1:L]
        m_new = jnp.maximum(b_last + m_prev, b_last + jnp.max(g_r, axis=-1, keepdims=True))
        w_r = jnp.exp(g_r + (b_last - m_new))
        decay = jnp.exp(b_last + m_prev - m_new)
        upd = jnp.dot((k_t * w_r).astype(BF16), v1, preferred_element_type=F32)
        new_states[b, h] = decay * state + upd
        new_m[b, h] = jnp.broadcast_to(m_new, (1, LANES))

        og = jax.nn.sigmoid(ob_ref[b, :, h * D:(h + 1) * D].astype(F32))
        hb = og * hh
        hb = hb * lax.rsqrt(jnp.mean(hb * hb, axis=-1, keepdims=True) + EPS)
        outs[b, h] = (hb * nrm_ref[:, h * D:(h + 1) * D]).astype(BF16)

    for b, h in new_states:
        state_ref[b, h] = new_states[b, h]
        m_ref[b, h:h + 1, :] = new_m[b, h]
        o_ref[b, :, W_A_Q + h * D:W_A_Q + (h + 1) * D] = outs[b, h]


def _seqmix(qkb, vb, ob, gc, gr, qa, kva, conv_w, conv_b, nrm, bias, sinks):
    assert CHUNK == BLOCK
    B, S, _ = qkb.shape
    nc = S // CHUNK
    halo_per_chunk = CHUNK // CONV_HALO
    blk = lambda w: pl.BlockSpec((B, CHUNK, w), lambda c: (0, c, 0))
    full = lambda a: pl.BlockSpec(a.shape, lambda c: (0,) * a.ndim)
    return pl.pallas_call(
        _seqmix_kernel,
        grid=(nc,),
        in_specs=[pl.BlockSpec(memory_space=pltpu.SMEM),
                  blk(2 * W_B),
                  pl.BlockSpec((B, CONV_HALO, 2 * W_B), lambda c: (0, jnp.maximum(c * halo_per_chunk - 1, 0), 0)),
                  blk(W_B), blk(W_B), blk(LANES),
                  pl.BlockSpec((B, GATE_ROWS, CHUNK), lambda c: (0, 0, c)),
                  blk(W_A_Q), blk(2 * W_A_KV),
                  pl.BlockSpec((B, BLOCK, 2 * W_A_KV), lambda c: (0, jnp.maximum(c - 1, 0), 0)),
                  full(conv_w), full(conv_b), full(nrm),
                  pl.BlockSpec((None,) + bias.shape[1:], lambda c: (jnp.minimum(c, 1), 0, 0, 0))],
        out_specs=blk(W_A_Q + W_B),
        out_shape=jax.ShapeDtypeStruct((B, S, W_A_Q + W_B), BF16),
        scratch_shapes=[pltpu.VMEM((B, N_HEADS_B, HEAD_DIM_B, 2 * HEAD_DIM_B), F32),
                        pltpu.VMEM((B, GATE_ROWS, LANES), F32)],
        compiler_params=pltpu.CompilerParams(dimension_semantics=("arbitrary",), vmem_limit_bytes=VMEM_LIMIT),
        name="seqmix",
    )(sinks, qkb, qkb, vb, ob, gc, gr, qa, kva, kva, conv_w, conv_b, nrm, bias)


def _memkv_kernel(mem_ref, g_ref, wk_ref, wv_ref, k_ref, v_ref):
    hm = _rms(mem_ref[...], g_ref[...]).astype(BF16)
    k_ref[...] = jnp.dot(hm, wk_ref[...], preferred_element_type=F32).astype(BF16)
    v_ref[...] = jnp.dot(hm, wv_ref[...], preferred_element_type=F32).astype(BF16)


def _memkv(mem2d, g, wk, wv, B):
    full = lambda a: pl.BlockSpec(a.shape, lambda b: (0,) * a.ndim)
    blk = pl.BlockSpec((N_MEM, D_MODEL), lambda b: (b, 0))
    return pl.pallas_call(
        _memkv_kernel,
        grid=(B,),
        in_specs=[blk, full(g), full(wk), full(wv)],
        out_specs=[blk, blk],
        out_shape=[jax.ShapeDtypeStruct((B * N_MEM, D_MODEL), BF16)] * 2,
        compiler_params=pltpu.CompilerParams(dimension_semantics=("parallel",), vmem_limit_bytes=VMEM_LIMIT),
        name="memkv",
    )(mem2d, g, wk, wv)


ROUTE_E1, ROUTE_E2, ROUTE_G1, ROUTE_G2, ROUTE_R1, ROUTE_R2 = 0, 1, 2, 3, 4, 5
ROUTER_GROUP_COL = N_EXPERTS
ROUTER_ROWS = -(-(N_EXPERTS + N_GROUPS) // SUBLANES) * SUBLANES


def _mid_kernel(x_ref, mix_ref, wo_ref, gx_ref, wq_ref, ck_ref, cv_ref, wco_ref, gz_ref, wr_ref, br_ref,
                x2_ref, hz_ref, route_ref, rrows_ref, counts_ref, cnt_ref):
    @pl.when(pl.program_id(0) == 0)
    def _():
        cnt_ref[...] = jnp.zeros_like(cnt_ref)

    x1 = x_ref[...] + jnp.dot(mix_ref[...], wo_ref[...], preferred_element_type=F32)

    hc = _rms(x1, gx_ref[...]).astype(BF16)
    cq = jnp.dot(hc, wq_ref[...], preferred_element_type=F32).astype(BF16)
    scale = HEAD_DIM_X ** -0.5
    heads = []
    for h in range(N_HEADS_X):
        sl = slice(h * HEAD_DIM_X, (h + 1) * HEAD_DIM_X)
        s = lax.dot_general(cq[:, sl], ck_ref[:, sl], (((1,), (1,)), ((), ())), preferred_element_type=F32) * scale
        p = jnp.exp(s - jnp.max(s, axis=-1, keepdims=True))
        co = jnp.dot(p.astype(BF16), cv_ref[:, sl], preferred_element_type=F32) / jnp.sum(p, axis=-1, keepdims=True)
        heads.append(co.astype(BF16))
    x2 = x1 + jnp.dot(jnp.concatenate(heads, axis=-1), wco_ref[...], preferred_element_type=F32)
    x2_ref[...] = x2

    hz = _rms(x2, gz_ref[...])
    hz_ref[...] = _pack_halves(hz)
    lg = jnp.dot(hz.astype(BF16), wr_ref[...], preferred_element_type=F32) + br_ref[...]
    tm = lg.shape[0]
    lt = jnp.transpose(lg)[0:ROUTER_ROWS, :]
    row = lax.broadcasted_iota(I32, lt.shape, 0)
    big = jnp.int32(ROUTER_ROWS)
    is_g = (row >= ROUTER_GROUP_COL) & (row < ROUTER_GROUP_COL + N_GROUPS)
    gl = jnp.where(is_g, lt, NEG_INF)
    gmax = jnp.max(gl, axis=0, keepdims=True)
    gsum = jnp.sum(jnp.exp(gl - gmax), axis=0, keepdims=True)
    g_prob = 1.0 / gsum
    g_idx = jnp.min(jnp.where(gl == gmax, row - ROUTER_GROUP_COL, big), axis=0, keepdims=True)
    sel = (row < N_EXPERTS) & ((row // EXPERTS_PER_GROUP) == g_idx)
    el = jnp.where(sel, lt, NEG_INF)
    m1 = jnp.max(el, axis=0, keepdims=True)
    i1 = jnp.min(jnp.where(el == m1, row, big), axis=0, keepdims=True)
    el2 = jnp.where(row == i1, NEG_INF, el)
    m2 = jnp.max(el2, axis=0, keepdims=True)
    i2 = jnp.min(jnp.where(el2 == m2, row, big), axis=0, keepdims=True)
    z = jnp.sum(jnp.exp(el - m1), axis=0, keepdims=True)
    p1 = 1.0 / z
    p2 = jnp.exp(m2 - m1) / z
    g1 = g_prob * (p1 / (p1 + p2))
    g2 = g_prob * (p2 / (p1 + p2))

    used = jnp.where((row == i1) | (row == i2), 1.0, 0.0)
    t_from = lax.broadcasted_iota(I32, (tm, tm), 0)
    t_to = lax.broadcasted_iota(I32, (tm, tm), 1)
    earlier = jnp.where(t_from < t_to, 1.0, 0.0).astype(BF16)
    before = jnp.dot(used.astype(BF16), earlier, preferred_element_type=F32) + cnt_ref[:, 0:1]
    r1 = jnp.sum(jnp.where(row == i1, before, 0.0), axis=0, keepdims=True)
    r2 = jnp.sum(jnp.where(row == i2, before, 0.0), axis=0, keepdims=True)
    cnt_ref[...] = cnt_ref[...] + jnp.sum(used, axis=1, keepdims=True)
    counts_ref[...] = cnt_ref[...]

    rec_row = lax.broadcasted_iota(I32, (SUBLANES, tm), 0)
    rec = jnp.zeros((SUBLANES, tm), F32)
    for c, v in ((ROUTE_E1, i1.astype(F32)), (ROUTE_E2, i2.astype(F32)), (ROUTE_G1, g1), (ROUTE_G2, g2),
                 (ROUTE_R1, r1), (ROUTE_R2, r2)):
        rec = jnp.where(rec_row == c, v, rec)
    rrows_ref[...] = rec
    route_ref[...] = jnp.transpose(jnp.concatenate([rec, jnp.zeros((LANES - SUBLANES, tm), F32)], axis=0))


def _mid(x2d, mix, wo, gx, wq, ck, cv, wco, gz, wr, br, B, S):
    T = B * S
    tm = min(TM_MID, S)
    per_b = S // tm
    row = lambda w: pl.BlockSpec((tm, w), lambda i: (i, 0))
    full = lambda a: pl.BlockSpec(a.shape, lambda i: (0,) * a.ndim)
    kvspec = pl.BlockSpec((N_MEM, D_MODEL), lambda i: (i // per_b, 0))
    return pl.pallas_call(
        _mid_kernel,
        grid=(T // tm,),
        in_specs=[row(D_MODEL), row(W_A_Q + W_B), full(wo), full(gx), full(wq), kvspec, kvspec,
                  full(wco), full(gz), full(wr), full(br)],
        out_specs=[row(D_MODEL), row(HALF), row(LANES), pl.BlockSpec((SUBLANES, tm), lambda i: (0, i)),
                   pl.BlockSpec((ROUTER_ROWS, LANES), lambda i: (0, 0))],
        out_shape=[jax.ShapeDtypeStruct((T, D_MODEL), F32),
                   jax.ShapeDtypeStruct((T, HALF), U32),
                   jax.ShapeDtypeStruct((T, LANES), F32),
                   jax.ShapeDtypeStruct((SUBLANES, T), F32),
                   jax.ShapeDtypeStruct((ROUTER_ROWS, LANES), F32)],
        scratch_shapes=[pltpu.VMEM((ROUTER_ROWS, LANES), F32)],
        compiler_params=pltpu.CompilerParams(dimension_semantics=("arbitrary",), vmem_limit_bytes=VMEM_LIMIT),
        name="mid",
    )(x2d, mix, wo, gx, wq, ck, cv, wco, gz, wr, br)


def _dispatch_kernel(zf_ref, pos_ref, hz_ref, xs_hbm, zbuf, sem, zsem):
    i = pl.program_id(0)
    tm = hz_ref.shape[0]
    zrows = zbuf.shape[0]

    @pl.when(i == 0)
    def _():
        zbuf[...] = jnp.zeros_like(zbuf)

        def fill(t):
            return pltpu.make_async_copy(zbuf, xs_hbm.at[pl.ds(t * zrows, zrows)], zsem)

        def start(t, carry):
            @pl.when(zf_ref[t] != 0)
            def _():
                fill(t).start()
            return carry

        def wait(t, carry):
            @pl.when(zf_ref[t] != 0)
            def _():
                fill(t).wait()
            return carry

        lax.fori_loop(0, zf_ref.shape[0], start, 0)
        lax.fori_loop(0, zf_ref.shape[0], wait, 0)

    for r in range(tm):
        for k in range(TOP_K):
            pltpu.make_async_copy(hz_ref.at[pl.ds(r, 1)], xs_hbm.at[pl.ds(pos_ref[0, 0, k * tm + r], 1)],
                                  sem).start(priority=k % 2)
    for k in range(TOP_K):
        pltpu.make_async_copy(hz_ref, xs_hbm.at[pl.ds(0, tm)], sem).wait()


def _dispatch(hz_packed, pos, zfill, n_slots, tmx):
    T = hz_packed.shape[0]
    tm = pos.shape[2] // TOP_K
    grid_spec = pltpu.PrefetchScalarGridSpec(
        num_scalar_prefetch=1,
        grid=(T // tm,),
        in_specs=[pl.BlockSpec((1, 1, TOP_K * tm), lambda i, zf: (i, 0, 0), memory_space=pltpu.SMEM),
                  pl.BlockSpec((tm, HALF), lambda i, zf: (i, 0))],
        out_specs=pl.BlockSpec(memory_space=pl.ANY),
        scratch_shapes=[pltpu.VMEM((tmx, HALF), U32), pltpu.SemaphoreType.DMA(()), pltpu.SemaphoreType.DMA(())],
    )
    return pl.pallas_call(
        _dispatch_kernel,
        grid_spec=grid_spec,
        out_shape=jax.ShapeDtypeStruct((n_slots, HALF), U32),
        compiler_params=pltpu.CompilerParams(dimension_semantics=("arbitrary",), vmem_limit_bytes=VMEM_LIMIT),
        name="dispatch",
    )(zfill, pos, hz_packed)


def _expert_kernel(te_ref, nt_ref, first_ref, slot_ref, next_ref, xs_ref, wg_hbm, wu_hbm, wd_hbm, ys_ref,
                   wg32, wu32, wd32, wgb, wub, wdb, wsem):
    i = pl.program_id(0)
    nt = nt_ref[0]

    def fetch(e, s):
        return [pltpu.make_async_copy(src.at[e], dst.at[s], wsem.at[s])
                for src, dst in ((wg_hbm, wg32), (wu_hbm, wu32), (wd_hbm, wd32))]

    @pl.when(i < nt)
    def _():
        @pl.when(i == 0)
        def _():
            for cp in fetch(te_ref[0], 0):
                cp.start()

        @pl.when(first_ref[i] != 0)
        def _():
            s = slot_ref[i]
            for cp in fetch(te_ref[i], s):
                cp.wait()

            @pl.when(next_ref[i] >= 0)
            def _():
                for cp in fetch(next_ref[i], 1 - s):
                    cp.start()

            wgb[...] = wg32[s].astype(BF16)
            wub[...] = wu32[s].astype(BF16)
            wdb[...] = wd32[s].astype(BF16)

        x = _unpack_halves(xs_ref[...]).astype(BF16)
        hg = jnp.dot(x, wgb[...], preferred_element_type=F32)
        hu = jnp.dot(x, wub[...], preferred_element_type=F32)
        a = (hg * jax.nn.sigmoid(hg) * hu).astype(BF16)
        ys_ref[...] = _pack_halves(jnp.dot(a, wdb[...], preferred_element_type=F32))

    @pl.when(i >= nt)
    def _():
        ys_ref[...] = jnp.zeros_like(ys_ref)


def _experts(xs, w_gate, w_up, w_down, tile_expert, ntiles, run_first, run_slot, run_next, tmx):
    n_tiles_max = tile_expert.shape[0]
    hbm = pl.BlockSpec(memory_space=pl.ANY)
    grid_spec = pltpu.PrefetchScalarGridSpec(
        num_scalar_prefetch=5,
        grid=(n_tiles_max,),
        in_specs=[pl.BlockSpec((tmx, HALF), lambda i, te, nt, *_: (jnp.minimum(i, nt[0] - 1), 0)), hbm, hbm, hbm],
        out_specs=pl.BlockSpec((tmx, HALF), lambda i, *_: (i, 0)),
        scratch_shapes=[pltpu.VMEM((2, D_MODEL, D_EXPERT), F32),
                        pltpu.VMEM((2, D_MODEL, D_EXPERT), F32),
                        pltpu.VMEM((2, D_EXPERT, D_MODEL), F32),
                        pltpu.VMEM((D_MODEL, D_EXPERT), BF16),
                        pltpu.VMEM((D_MODEL, D_EXPERT), BF16),
                        pltpu.VMEM((D_EXPERT, D_MODEL), BF16),
                        pltpu.SemaphoreType.DMA((2,))],
    )
    return pl.pallas_call(
        _expert_kernel,
        grid_spec=grid_spec,
        out_shape=jax.ShapeDtypeStruct(xs.shape, U32),
        compiler_params=pltpu.CompilerParams(dimension_semantics=("arbitrary",), vmem_limit_bytes=VMEM_LIMIT),
        name="experts",
    )(tile_expert, ntiles, run_first, run_slot, run_next, xs, w_gate, w_up, w_down)


def _final_kernel(posc_ref, posn_ref, x2_ref, route_ref, g_ref, ys_hbm, o_ref, ybuf, sem):
    i = pl.program_id(0)
    n = pl.num_programs(0)
    tm = x2_ref.shape[0]
    slot = i % 2

    def issue(pos_ref, s):
        for r in range(tm):
            for k in range(TOP_K):
                pltpu.make_async_copy(ys_hbm.at[pl.ds(pos_ref[0, 0, k * tm + r], 1)],
                                      ybuf.at[s, k, pl.ds(r, 1)], sem.at[s]).start(priority=k % 2)

    def wait(s):
        for k in range(TOP_K):
            pltpu.make_async_copy(ys_hbm.at[pl.ds(0, tm)], ybuf.at[s, k], sem.at[s]).wait()

    @pl.when(i == 0)
    def _():
        issue(posc_ref, 0)

    wait(slot)

    for s in range(2):
        @pl.when(slot == s)
        def _():
            issue(posn_ref, 1 - s)

    r = route_ref[...]
    g1 = r[:, ROUTE_G1:ROUTE_G1 + 1]
    g2 = r[:, ROUTE_G2:ROUTE_G2 + 1]
    xo = x2_ref[...] + g1 * _unpack_halves(ybuf[slot, 0]) + g2 * _unpack_halves(ybuf[slot, 1])
    o_ref[...] = _rms(xo, g_ref[...])

    @pl.when(i == n - 1)
    def _():
        wait(1 - slot)


def _final(x2, ys, pos, route, g):
    T = x2.shape[0]
    nblk = pos.shape[0]
    tm = T // nblk
    row = lambda w: pl.BlockSpec((tm, w), lambda i: (i, 0))
    return pl.pallas_call(
        _final_kernel,
        grid=(nblk,),
        in_specs=[pl.BlockSpec((1, 1, TOP_K * tm), lambda i: (i, 0, 0), memory_space=pltpu.SMEM),
                  pl.BlockSpec((1, 1, TOP_K * tm), lambda i: (jnp.minimum(i + 1, nblk - 1), 0, 0),
                               memory_space=pltpu.SMEM),
                  row(D_MODEL), row(LANES), pl.BlockSpec(g.shape, lambda i: (0, 0)),
                  pl.BlockSpec(memory_space=pl.ANY)],
        out_specs=row(D_MODEL),
        out_shape=jax.ShapeDtypeStruct((T, D_MODEL), F32),
        scratch_shapes=[pltpu.VMEM((2, TOP_K, tm, HALF), U32), pltpu.SemaphoreType.DMA((2,))],
        compiler_params=pltpu.CompilerParams(dimension_semantics=("arbitrary",), vmem_limit_bytes=VMEM_LIMIT),
        name="final",
    )(pos, pos, x2, route, g, ys)


def _band_bias(table):
    assert WINDOW == BLOCK
    i = jnp.arange(BLOCK)[:, None]
    j = jnp.arange(2 * BLOCK)[None, :]
    n = jnp.maximum(i + BLOCK - j, 0)
    nf = jnp.maximum(n, 1).astype(F32)
    large = MAX_EXACT + (jnp.log(nf / MAX_EXACT) / math.log(MAX_DISTANCE / MAX_EXACT)
                         * (NUM_BUCKETS - MAX_EXACT)).astype(I32)
    large = jnp.minimum(large, NUM_BUCKETS - 1)
    bucket = jnp.where(n < MAX_EXACT, n, large)
    onehot = (bucket[:, :, None] == jnp.arange(NUM_BUCKETS)[None, None, :]).astype(F32)
    bias = jnp.einsum("ijb,bh->hij", onehot, table.astype(F32), precision=lax.Precision.HIGHEST)
    from_prev = (jnp.arange(BLOCK)[None, :] > i)[None]
    prev, cur = bias[:, :, :BLOCK], bias[:, :, BLOCK:]
    return jnp.stack([jnp.where(from_prev, NEG_INF, cur), jnp.where(from_prev, prev, cur)])


def _dispatch_plan(route_rows, counts_f, tmx, n_tiles_max, tm_rows):
    T = route_rows.shape[1]
    experts = jnp.arange(N_EXPERTS, dtype=I32)
    counts = counts_f[:N_EXPERTS, 0].astype(I32)
    ptiles = (counts + tmx - 1) // tmx
    tile_end = jnp.cumsum(ptiles)
    nt = tile_end[-1]
    row_off = (tile_end - ptiles) * tmx

    def slot(e_row, r_row):
        e = route_rows[e_row].astype(I32)
        off = jnp.sum(jnp.where(e[None, :] == experts[:, None], row_off[:, None], 0), axis=0)
        return (off + route_rows[r_row].astype(I32)).reshape(T // tm_rows, 1, tm_rows)

    pos = jnp.concatenate([slot(ROUTE_E1, ROUTE_R1), slot(ROUTE_E2, ROUTE_R2)], axis=2)

    tile_ids = jnp.arange(n_tiles_max, dtype=I32)
    expert_of = lambda t: jnp.sum((tile_end[None, :] <= t[:, None]).astype(I32), axis=1)
    te = expert_of(jnp.minimum(tile_ids, nt - 1))
    partial = jnp.any((tile_ids[:, None] == (tile_end - 1)[None, :]) & (counts % tmx != 0)[None, :], axis=1)
    zfill = (partial | (tile_ids >= nt)).astype(I32)

    used = ptiles > 0
    run_first = (jnp.any((tile_ids[:, None] == (tile_end - ptiles)[None, :]) & used[None, :], axis=1)
                 & (tile_ids < nt)).astype(I32)
    run_slot = (jnp.cumsum(run_first) - 1) % 2
    later_used = used[None, :] & (experts[None, :] > experts[:, None])
    next_of = jnp.min(jnp.where(later_used, experts[None, :], N_EXPERTS), axis=1)
    next_of = jnp.where(next_of < N_EXPERTS, next_of, -1)
    run_next = jnp.sum(jnp.where(te[:, None] == experts[None, :], next_of[None, :], 0), axis=1)
    return pos, te, nt.reshape(1), zfill, run_first, run_slot.astype(I32), run_next.astype(I32)


def kernel(x, mem, rel_bias_table, norm_mix, w_in, attn_sinks, conv_w, conv_b, gate_bias_i, gate_bias_f, mlstm_norm, w_out, norm_cross, norm_mem, w_cq, w_ck, w_cv, w_co, norm_moe, w_router_group, b_router_group, w_router_expert, b_router_expert, w_exp_gate, w_exp_up, w_exp_down, norm_final):
    B, S, _ = x.shape
    T = B * S
    depth = w_in.shape[0]
    x2d = x.reshape(T, D_MODEL)
    mem2d = mem.reshape(B * N_MEM, D_MODEL)
    bias = _band_bias(rel_bias_table)

    tmx = min(TM_EXPERT, T)
    n_tiles_max = (T * TOP_K) // tmx + N_EXPERTS
    tm_rows = min(TM_ROWDMA, T)

    assert depth == 1, "the final combine is fused with the final norm: single layer only"
    l = 0
    w_pad = jnp.pad(w_in[l], ((0, 0), (0, C_GATE + LANES - D_IN))).astype(BF16)
    gb = jnp.concatenate([gate_bias_i[l], gate_bias_f[l]]).astype(F32)
    gbias_col = jnp.pad(gb, (0, LANES - GATE_ROWS))[None, :]
    qa, kva, qkb, vb, ob, gc, gr = _inproj(x2d, norm_mix[l][None, :], w_pad, gbias_col, B, S)

    per_seq = lambda a: a.reshape(B, S, a.shape[-1])
    mix = _seqmix(per_seq(qkb), per_seq(vb), per_seq(ob), per_seq(gc), gr, per_seq(qa), per_seq(kva),
                  conv_w[l][:, 0, :].astype(F32), conv_b[l][None, :].astype(F32),
                  mlstm_norm[l][None, :].astype(F32), bias, attn_sinks[l].astype(F32)).reshape(T, W_A_Q + W_B)

    ck, cv = _memkv(mem2d, norm_mem[l][None, :], w_ck[l].astype(BF16), w_cv[l].astype(BF16), B)

    wr = jnp.pad(jnp.concatenate([w_router_expert[l], w_router_group[l]], axis=1),
                 ((0, 0), (0, LANES - N_EXPERTS - N_GROUPS))).astype(BF16)
    br = jnp.pad(jnp.concatenate([b_router_expert[l], b_router_group[l]]),
                 (0, LANES - N_EXPERTS - N_GROUPS)).astype(F32)[None, :]
    x2, hz_packed, route, route_rows, counts = _mid(
        x2d, mix, w_out[l].astype(BF16), norm_cross[l][None, :], w_cq[l].astype(BF16), ck, cv,
        w_co[l].astype(BF16), norm_moe[l][None, :], wr, br, B, S)

    pos, te, nt, zfill, run_first, run_slot, run_next = _dispatch_plan(route_rows, counts, tmx, n_tiles_max,
                                                                       tm_rows)
    xs = _dispatch(hz_packed, pos, zfill, n_tiles_max * tmx, tmx)
    ys = _experts(xs, w_exp_gate[l], w_exp_up[l], w_exp_down[l], te, nt, run_first, run_slot, run_next, tmx)
    out = _final(x2, ys, pos, route, norm_final[None, :])
    return out.reshape(B, S, D_MODEL)
```

```python
import math

import jax
import jax.numpy as jnp
from jax import lax
from jax.experimental import pallas as pl
from jax.experimental.pallas import tpu as pltpu

F32 = jnp.float32
BF16 = jnp.bfloat16
U32 = jnp.uint32
I32 = jnp.int32

D_MODEL = 1024
N_MEM = 256
N_HEADS_A = 8
N_KV_A = 2
HEAD_DIM_A = 64
BLOCK = 128
WINDOW = 128
NUM_BUCKETS = 32
MAX_EXACT = NUM_BUCKETS // 2
MAX_DISTANCE = 128
N_HEADS_B = 4
HEAD_DIM_B = 128
CHUNK = 128
CONV_WIDTH = 4
N_HEADS_X = 4
HEAD_DIM_X = D_MODEL // N_HEADS_X
N_GROUPS = 4
EXPERTS_PER_GROUP = 8
N_EXPERTS = N_GROUPS * EXPERTS_PER_GROUP
TOP_K = 2
D_EXPERT = 512
EPS = 1e-6
NEG_INF = -1e30

W_A_Q = N_HEADS_A * HEAD_DIM_A
W_A_KV = N_KV_A * HEAD_DIM_A
W_B = N_HEADS_B * HEAD_DIM_B
C_QA = 0
C_KVA = C_QA + W_A_Q
C_QKB = C_KVA + 2 * W_A_KV
C_VB = C_QKB + 2 * W_B
C_OB = C_VB + W_B
C_GATE = C_OB + W_B
D_IN = C_GATE + 2 * N_HEADS_B

LANES = 128
SUBLANES = 8
GATE_ROWS = 8
HALF = D_MODEL // 2

TM_INPROJ = 1024
TM_MID = 1024
TM_ROWDMA = 512
FINAL_GROUPS = 16
TM_EXPERT = 512

VMEM_LIMIT = 48 * 1024 * 1024


def _rms(xf, g):
    return xf * lax.rsqrt(jnp.mean(xf * xf, axis=-1, keepdims=True) + EPS) * g


def _pack_halves(v):
    b = pltpu.bitcast(v.astype(BF16).astype(F32), U32)
    return (b[:, :HALF] >> 16) | b[:, HALF:]


def _unpack_halves(p):
    lo = pltpu.bitcast(p << 16, F32)
    hi = pltpu.bitcast(p & jnp.uint32(0xFFFF0000), F32)
    return jnp.concatenate([lo, hi], axis=-1)


def _log_sigmoid(z):
    return jnp.minimum(z, 0.0) - jnp.log1p(jnp.exp(-jnp.abs(z)))


def _split3(v):
    hi = v.astype(BF16).astype(F32)
    rest = v - hi
    mid = rest.astype(BF16).astype(F32)
    return hi, mid, (rest - mid).astype(BF16).astype(F32)


def _inproj_kernel(x_ref, g_ref, w_ref, gbc_ref, qa_ref, kva_ref, qkb_ref, vb_ref, ob_ref, gc_ref, gr_ref):
    tm = x_ref.shape[0]
    h = _rms(x_ref[...], g_ref[...]).astype(BF16)

    def mm(lo, hi):
        return jnp.dot(h, w_ref[:, lo:hi], preferred_element_type=F32)

    qa_ref[...] = mm(C_QA, C_KVA).astype(BF16)
    kva_ref[...] = mm(C_KVA, C_QKB).astype(BF16)
    qkb_ref[...] = mm(C_QKB, C_VB).astype(BF16)
    vb_ref[...] = mm(C_VB, C_OB).astype(BF16)
    ob_ref[...] = mm(C_OB, C_GATE).astype(BF16)

    H, L = N_HEADS_B, CHUNK
    gcol = mm(C_GATE, C_GATE + LANES) + gbc_ref[...]
    grow = jnp.transpose(gcol)[0:GATE_ROWS, :]
    lane_c = lax.broadcasted_iota(I32, (L, LANES), 1)
    is_f_col = (lane_c >= H) & (lane_c < 2 * H)
    is_f_row = lax.broadcasted_iota(I32, (GATE_ROWS, L), 0) >= H
    ti = lax.broadcasted_iota(I32, (L, L), 0)
    si = lax.broadcasted_iota(I32, (L, L), 1)
    tril = jnp.where(si <= ti, 1.0, 0.0).astype(BF16)
    triu = jnp.where(si >= ti, 1.0, 0.0).astype(BF16)
    for c in range(tm // L):
        rows = slice(c * L, (c + 1) * L)
        gcol_c = gcol[rows, :]
        fcol = jnp.where(is_f_col, _log_sigmoid(gcol_c), 0.0)
        parts = jnp.dot(tril, jnp.concatenate(_split3(fcol), axis=1).astype(BF16), preferred_element_type=F32)
        bcol = parts[:, :LANES] + parts[:, LANES:2 * LANES] + parts[:, 2 * LANES:]
        gc_ref[rows, :] = jnp.where(is_f_col, bcol, gcol_c)
        grow_c = grow[:, rows]
        frow = jnp.where(is_f_row, _log_sigmoid(grow_c), 0.0)
        parts = jnp.dot(jnp.concatenate(_split3(frow), axis=0).astype(BF16), triu, preferred_element_type=F32)
        brow = parts[:GATE_ROWS] + parts[GATE_ROWS:2 * GATE_ROWS] + parts[2 * GATE_ROWS:]
        gr_ref[:, rows] = jnp.where(is_f_row, brow, grow_c)


def _inproj(x2d, g, w_pad, gbias_col, B, S):
    T = x2d.shape[0]
    tm = min(TM_INPROJ, S)
    tiles_per_seq = S // tm
    row = lambda w: pl.BlockSpec((tm, w), lambda i: (i, 0))
    full = lambda a: pl.BlockSpec(a.shape, lambda i: (0,) * a.ndim)
    return pl.pallas_call(
        _inproj_kernel,
        grid=(T // tm,),
        in_specs=[row(D_MODEL), full(g), full(w_pad), full(gbias_col)],
        out_specs=[row(W_A_Q), row(2 * W_A_KV), row(2 * W_B), row(W_B), row(W_B), row(LANES),
                   pl.BlockSpec((None, GATE_ROWS, tm), lambda i: (i // tiles_per_seq, 0, i % tiles_per_seq))],
        out_shape=[jax.ShapeDtypeStruct((T, W_A_Q), BF16),
                   jax.ShapeDtypeStruct((T, 2 * W_A_KV), BF16),
                   jax.ShapeDtypeStruct((T, 2 * W_B), BF16),
                   jax.ShapeDtypeStruct((T, W_B), BF16),
                   jax.ShapeDtypeStruct((T, W_B), BF16),
                   jax.ShapeDtypeStruct((T, LANES), F32),
                   jax.ShapeDtypeStruct((B, GATE_ROWS, S), F32)],
        compiler_params=pltpu.CompilerParams(dimension_semantics=("parallel",), vmem_limit_bytes=VMEM_LIMIT),
        name="inproj",
    )(x2d, g, w_pad, gbias_col)


def _swa_block(q, kvc, kvp, bias_ref, sink_ref):
    kvp = kvp.astype(F32)
    kvc = kvc.astype(F32)
    kband = jnp.concatenate([kvp[:, :W_A_KV], kvc[:, :W_A_KV]], axis=0)
    vband = jnp.concatenate([kvp[:, W_A_KV:], kvc[:, W_A_KV:]], axis=0)
    lane = lax.broadcasted_iota(I32, (2 * BLOCK, LANES), 1)
    lo = lane < HEAD_DIM_A

    def placements(band):
        swapped = pltpu.roll(band, HEAD_DIM_A, axis=1)
        z = jnp.zeros_like(band)
        return {(0, 0): jnp.where(lo, band, z).astype(BF16), (0, 1): jnp.where(lo, z, swapped).astype(BF16),
                (1, 0): jnp.where(lo, swapped, z).astype(BF16), (1, 1): jnp.where(lo, z, band).astype(BF16)}

    kpl = placements(kband)
    vpl = placements(vband)

    scale = HEAD_DIM_A ** -0.5
    group = N_HEADS_A // N_KV_A
    tiles = []
    from_prev = (lax.broadcasted_iota(I32, (BLOCK, BLOCK), 1) > lax.broadcasted_iota(I32, (BLOCK, BLOCK), 0))

    for pair in range(N_HEADS_A // 2):
        qt = q[:, pair * LANES:(pair + 1) * LANES]
        acc = None
        for half in range(2):
            h = 2 * pair + half
            g = h // group
            s2 = lax.dot_general(qt, kpl[(g, half)], (((1,), (1,)), ((), ())), preferred_element_type=F32)
            s = jnp.where(from_prev, s2[:, :BLOCK], s2[:, BLOCK:])
            s = s * scale + bias_ref[h]
            sink = sink_ref[h]
            m = jnp.maximum(jnp.max(s, axis=-1, keepdims=True), sink)
            p = jnp.exp(s - m)
            denom = jnp.sum(p, axis=-1, keepdims=True) + jnp.exp(sink - m)
            p2 = jnp.concatenate([jnp.where(from_prev, p, 0.0), jnp.where(from_prev, 0.0, p)], axis=1)
            o = jnp.dot(p2.astype(BF16), vpl[(g, half)], preferred_element_type=F32) / denom
            acc = o if acc is None else acc + o
        tiles.append(acc.astype(BF16))
    return tiles


CONV_HALO = 16


def _seqmix_kernel(sink_ref, qkc_ref, qkp_ref, vb_ref, ob_ref, gc_ref, gr_ref, qa_ref, kvc_ref, kvp_ref,
                   cw_ref, cb_ref, nrm_ref, bias_ref, o_ref, state_ref, m_ref):
    c = pl.program_id(0)
    B = qkc_ref.shape[0]
    H, D, L = N_HEADS_B, HEAD_DIM_B, CHUNK

    @pl.when(c == 0)
    def _():
        state_ref[...] = jnp.zeros_like(state_ref)
        m_ref[...] = jnp.zeros_like(m_ref)

    rr = lax.broadcasted_iota(I32, (L, CONV_HALO + L), 0)
    cc = lax.broadcasted_iota(I32, (L, CONV_HALO + L), 1)
    shifts = {delay: jnp.where(cc == rr + (CONV_HALO - delay), 1.0, 0.0).astype(BF16)
              for delay in range(1, CONV_WIDTH)}
    ti = lax.broadcasted_iota(I32, (L, L), 0)
    si = lax.broadcasted_iota(I32, (L, L), 1)
    tri = si <= ti
    ones_blk = jnp.ones((L, D), BF16)

    def conv_silu(b):
        prev = qkp_ref[b]
        prev = jnp.where(c > 0, prev, jnp.zeros_like(prev))
        cur = qkc_ref[b]
        ext = jnp.concatenate([prev, cur], axis=0)
        y = cb_ref[...] + cw_ref[CONV_WIDTH - 1:CONV_WIDTH, :] * cur.astype(F32)
        for delay in range(1, CONV_WIDTH):
            tap = CONV_WIDTH - 1 - delay
            y = y + cw_ref[tap:tap + 1, :] * jnp.dot(shifts[delay], ext, preferred_element_type=F32)
        return y * jax.nn.sigmoid(y)

    states = [[state_ref[b, h] for h in range(H)] for b in range(B)]
    m_alls = [m_ref[b] for b in range(B)]
    new_states, new_m, outs = {}, {}, {}

    for b, h in [(b, h) for b in range(B) for h in range(H)]:
        if h == 0:
            for pair, tile in enumerate(_swa_block(qa_ref[b], kvc_ref[b], kvp_ref[b], bias_ref, sink_ref)):
                o_ref[b, :, pair * LANES:(pair + 1) * LANES] = tile
            qk = conv_silu(b)
            gcol = gc_ref[b]
            grow = gr_ref[b]
        qh = (qk[:, h * D:(h + 1) * D] * (D ** -0.5)).astype(BF16)
        k_t = qk[:, W_B + h * D:W_B + (h + 1) * D].T
        v1 = jnp.concatenate([vb_ref[b, :, h * D:(h + 1) * D], ones_blk], axis=-1)
        b_r = grow[H + h:H + h + 1, :]
        g_r = grow[h:h + 1, :] - b_r
        b_c = gcol[:, H + h:H + h + 1]
        m_prev = m_alls[b][h:h + 1, 0:1]
        state = states[b][h]

        gmat = jnp.where(tri, g_r, NEG_INF)
        m_c = jnp.maximum(jnp.max(gmat, axis=-1, keepdims=True), m_prev)
        a_inter = jnp.exp(m_prev - m_c)
        sc = jnp.dot(qh, k_t.astype(BF16), preferred_element_type=F32) * jnp.exp(gmat - m_c)
        tot = (jnp.dot(sc.astype(BF16), v1, preferred_element_type=F32)
               + a_inter * jnp.dot(qh, state.astype(BF16), preferred_element_type=F32))
        num = tot[:, :D]
        den = tot[:, D:]
        hh = num / jnp.maximum(jnp.abs(den), jnp.exp(-(b_c + m_c)))

        b_last = b_r[:, L - 1:L]
        m_new = jnp.maximum(b_last + m_prev, b_last + jnp.max(g_r, axis=-1, keepdims=True))
        w_r = jnp.exp(g_r + (b_last - m_new))
        decay = jnp.exp(b_last + m_prev - m_new)
        upd = jnp.dot((k_t * w_r).astype(BF16), v1, preferred_element_type=F32)
        new_states[b, h] = decay * state + upd
        new_m[b, h] = jnp.broadcast_to(m_new, (1, LANES))

        og = jax.nn.sigmoid(ob_ref[b, :, h * D:(h + 1) * D].astype(F32))
        hb = og * hh
        hb = hb * lax.rsqrt(jnp.mean(hb * hb, axis=-1, keepdims=True) + EPS)
        outs[b, h] = (hb * nrm_ref[:, h * D:(h + 1) * D]).astype(BF16)

    for b, h in new_states:
        state_ref[b, h] = new_states[b, h]
        m_ref[b, h:h + 1, :] = new_m[b, h]
        o_ref[b, :, W_A_Q + h * D:W_A_Q + (h + 1) * D] = outs[b, h]


def _seqmix(qkb, vb, ob, gc, gr, qa, kva, conv_w, conv_b, nrm, bias, sinks):
    assert CHUNK == BLOCK
    B, S, _ = qkb.shape
    nc = S // CHUNK
    halo_per_chunk = CHUNK // CONV_HALO
    blk = lambda w: pl.BlockSpec((B, CHUNK, w), lambda c: (0, c, 0))
    full = lambda a: pl.BlockSpec(a.shape, lambda c: (0,) * a.ndim)
    return pl.pallas_call(
        _seqmix_kernel,
        grid=(nc,),
        in_specs=[pl.BlockSpec(memory_space=pltpu.SMEM),
                  blk(2 * W_B),
                  pl.BlockSpec((B, CONV_HALO, 2 * W_B), lambda c: (0, jnp.maximum(c * halo_per_chunk - 1, 0), 0)),
                  blk(W_B), blk(W_B), blk(LANES),
                  pl.BlockSpec((B, GATE_ROWS, CHUNK), lambda c: (0, 0, c)),
                  blk(W_A_Q), blk(2 * W_A_KV),
                  pl.BlockSpec((B, BLOCK, 2 * W_A_KV), lambda c: (0, jnp.maximum(c - 1, 0), 0)),
                  full(conv_w), full(conv_b), full(nrm),
                  pl.BlockSpec((None,) + bias.shape[1:], lambda c: (jnp.minimum(c, 1), 0, 0, 0))],
        out_specs=blk(W_A_Q + W_B),
        out_shape=jax.ShapeDtypeStruct((B, S, W_A_Q + W_B), BF16),
        scratch_shapes=[pltpu.VMEM((B, N_HEADS_B, HEAD_DIM_B, 2 * HEAD_DIM_B), F32),
                        pltpu.VMEM((B, GATE_ROWS, LANES), F32)],
        compiler_params=pltpu.CompilerParams(dimension_semantics=("arbitrary",), vmem_limit_bytes=VMEM_LIMIT),
        name="seqmix",
    )(sinks, qkb, qkb, vb, ob, gc, gr, qa, kva, kva, conv_w, conv_b, nrm, bias)


def _memkv_kernel(mem_ref, g_ref, wk_ref, wv_ref, k_ref, v_ref):
    hm = _rms(mem_ref[...], g_ref[...]).astype(BF16)
    k_ref[...] = jnp.dot(hm, wk_ref[...], preferred_element_type=F32).astype(BF16)
    v_ref[...] = jnp.dot(hm, wv_ref[...], preferred_element_type=F32).astype(BF16)


def _memkv(mem2d, g, wk, wv, B):
    full = lambda a: pl.BlockSpec(a.shape, lambda b: (0,) * a.ndim)
    blk = pl.BlockSpec((N_MEM, D_MODEL), lambda b: (b, 0))
    return pl.pallas_call(
        _memkv_kernel,
        grid=(B,),
        in_specs=[blk, full(g), full(wk), full(wv)],
        out_specs=[blk, blk],
        out_shape=[jax.ShapeDtypeStruct((B * N_MEM, D_MODEL), BF16)] * 2,
        compiler_params=pltpu.CompilerParams(dimension_semantics=("parallel",), vmem_limit_bytes=VMEM_LIMIT),
        name="memkv",
    )(mem2d, g, wk, wv)


ROUTE_E1, ROUTE_E2, ROUTE_G1, ROUTE_G2, ROUTE_R1, ROUTE_R2 = 0, 1, 2, 3, 4, 5
ROUTER_GROUP_COL = N_EXPERTS
ROUTER_ROWS = -(-(N_EXPERTS + N_GROUPS) // SUBLANES) * SUBLANES


def _mid_kernel(x_ref, mix_ref, wo_ref, gx_ref, wq_ref, ck_ref, cv_ref, wco_ref, gz_ref, wr_ref, br_ref,
                x2_ref, hz_ref, route_ref, rrows_ref, counts_ref, cnt_ref):
    @pl.when(pl.program_id(0) == 0)
    def _():
        cnt_ref[...] = jnp.zeros_like(cnt_ref)

    x1 = x_ref[...] + jnp.dot(mix_ref[...], wo_ref[...], preferred_element_type=F32)

    hc = _rms(x1, gx_ref[...]).astype(BF16)
    cq = jnp.dot(hc, wq_ref[...], preferred_element_type=F32).astype(BF16)
    scale = HEAD_DIM_X ** -0.5
    heads = []
    for h in range(N_HEADS_X):
        sl = slice(h * HEAD_DIM_X, (h + 1) * HEAD_DIM_X)
        s = lax.dot_general(cq[:, sl], ck_ref[:, sl], (((1,), (1,)), ((), ())), preferred_element_type=F32) * scale
        p = jnp.exp(s - jnp.max(s, axis=-1, keepdims=True))
        co = jnp.dot(p.astype(BF16), cv_ref[:, sl], preferred_element_type=F32) / jnp.sum(p, axis=-1, keepdims=True)
        heads.append(co.astype(BF16))
    x2 = x1 + jnp.dot(jnp.concatenate(heads, axis=-1), wco_ref[...], preferred_element_type=F32)
    x2_ref[...] = x2

    hz = _rms(x2, gz_ref[...])
    hz_ref[...] = _pack_halves(hz)
    lg = jnp.dot(hz.astype(BF16), wr_ref[...], preferred_element_type=F32) + br_ref[...]
    tm = lg.shape[0]
    lt = jnp.transpose(lg)[0:ROUTER_ROWS, :]
    row = lax.broadcasted_iota(I32, lt.shape, 0)
    big = jnp.int32(ROUTER_ROWS)
    is_g = (row >= ROUTER_GROUP_COL) & (row < ROUTER_GROUP_COL + N_GROUPS)
    gl = jnp.where(is_g, lt, NEG_INF)
    gmax = jnp.max(gl, axis=0, keepdims=True)
    gsum = jnp.sum(jnp.exp(gl - gmax), axis=0, keepdims=True)
    g_prob = 1.0 / gsum
    g_idx = jnp.min(jnp.where(gl == gmax, row - ROUTER_GROUP_COL, big), axis=0, keepdims=True)
    sel = (row < N_EXPERTS) & ((row // EXPERTS_PER_GROUP) == g_idx)
    el = jnp.where(sel, lt, NEG_INF)
    m1 = jnp.max(el, axis=0, keepdims=True)
    i1 = jnp.min(jnp.where(el == m1, row, big), axis=0, keepdims=True)
    el2 = jnp.where(row == i1, NEG_INF, el)
    m2 = jnp.max(el2, axis=0, keepdims=True)
    i2 = jnp.min(jnp.where(el2 == m2, row, big), axis=0, keepdims=True)
    z = jnp.sum(jnp.exp(el - m1), axis=0, keepdims=True)
    p1 = 1.0 / z
    p2 = jnp.exp(m2 - m1) / z
    g1 = g_prob * (p1 / (p1 + p2))
    g2 = g_prob * (p2 / (p1 + p2))

    used = jnp.where((row == i1) | (row == i2), 1.0, 0.0)
    t_from = lax.broadcasted_iota(I32, (tm, tm), 0)
    t_to = lax.broadcasted_iota(I32, (tm, tm), 1)
    earlier = jnp.where(t_from < t_to, 1.0, 0.0).astype(BF16)
    before = jnp.dot(used.astype(BF16), earlier, preferred_element_type=F32) + cnt_ref[:, 0:1]
    r1 = jnp.sum(jnp.where(row == i1, before, 0.0), axis=0, keepdims=True)
    r2 = jnp.sum(jnp.where(row == i2, before, 0.0), axis=0, keepdims=True)
    cnt_ref[...] = cnt_ref[...] + jnp.sum(used, axis=1, keepdims=True)
    counts_ref[...] = cnt_ref[...]

    rec_row = lax.broadcasted_iota(I32, (SUBLANES, tm), 0)
    rec = jnp.zeros((SUBLANES, tm), F32)
    for c, v in ((ROUTE_E1, i1.astype(F32)), (ROUTE_E2, i2.astype(F32)), (ROUTE_G1, g1), (ROUTE_G2, g2),
                 (ROUTE_R1, r1), (ROUTE_R2, r2)):
        rec = jnp.where(rec_row == c, v, rec)
    rrows_ref[...] = rec
    route_ref[...] = jnp.transpose(jnp.concatenate([rec, jnp.zeros((LANES - SUBLANES, tm), F32)], axis=0))


def _mid(x2d, mix, wo, gx, wq, ck, cv, wco, gz, wr, br, B, S):
    T = B * S
    tm = min(TM_MID, S)
    per_b = S // tm
    row = lambda w: pl.BlockSpec((tm, w), lambda i: (i, 0))
    full = lambda a: pl.BlockSpec(a.shape, lambda i: (0,) * a.ndim)
    kvspec = pl.BlockSpec((N_MEM, D_MODEL), lambda i: (i // per_b, 0))
    return pl.pallas_call(
        _mid_kernel,
        grid=(T // tm,),
        in_specs=[row(D_MODEL), row(W_A_Q + W_B), full(wo), full(gx), full(wq), kvspec, kvspec,
                  full(wco), full(gz), full(wr), full(br)],
        out_specs=[row(D_MODEL), row(HALF), row(LANES), pl.BlockSpec((SUBLANES, tm), lambda i: (0, i)),
                   pl.BlockSpec((ROUTER_ROWS, LANES), lambda i: (0, 0))],
        out_shape=[jax.ShapeDtypeStruct((T, D_MODEL), F32),
                   jax.ShapeDtypeStruct((T, HALF), U32),
                   jax.ShapeDtypeStruct((T, LANES), F32),
                   jax.ShapeDtypeStruct((SUBLANES, T), F32),
                   jax.ShapeDtypeStruct((ROUTER_ROWS, LANES), F32)],
        scratch_shapes=[pltpu.VMEM((ROUTER_ROWS, LANES), F32)],
        compiler_params=pltpu.CompilerParams(dimension_semantics=("arbitrary",), vmem_limit_bytes=VMEM_LIMIT),
        name="mid",
    )(x2d, mix, wo, gx, wq, ck, cv, wco, gz, wr, br)


def _dispatch_kernel(zf_ref, pos_ref, hz_ref, xs_hbm, zbuf, sem, zsem):
    i = pl.program_id(0)
    tm = hz_ref.shape[0]
    zrows = zbuf.shape[0]

    @pl.when(i == 0)
    def _():
        zbuf[...] = jnp.zeros_like(zbuf)

        def fill(t):
            return pltpu.make_async_copy(zbuf, xs_hbm.at[pl.ds(t * zrows, zrows)], zsem)

        def start(t, carry):
            @pl.when(zf_ref[t] != 0)
            def _():
                fill(t).start()
            return carry

        def wait(t, carry):
            @pl.when(zf_ref[t] != 0)
            def _():
                fill(t).wait()
            return carry

        lax.fori_loop(0, zf_ref.shape[0], start, 0)
        lax.fori_loop(0, zf_ref.shape[0], wait, 0)

    for r in range(tm):
        for k in range(TOP_K):
            pltpu.make_async_copy(hz_ref.at[pl.ds(r, 1)], xs_hbm.at[pl.ds(pos_ref[0, 0, k * tm + r], 1)],
                                  sem).start(priority=k % 2)
    for k in range(TOP_K):
        pltpu.make_async_copy(hz_ref, xs_hbm.at[pl.ds(0, tm)], sem).wait()


def _dispatch(hz_packed, pos, zfill, n_slots, tmx):
    T = hz_packed.shape[0]
    tm = pos.shape[2] // TOP_K
    grid_spec = pltpu.PrefetchScalarGridSpec(
        num_scalar_prefetch=1,
        grid=(T // tm,),
        in_specs=[pl.BlockSpec((1, 1, TOP_K * tm), lambda i, zf: (i, 0, 0), memory_space=pltpu.SMEM),
                  pl.BlockSpec((tm, HALF), lambda i, zf: (i, 0))],
        out_specs=pl.BlockSpec(memory_space=pl.ANY),
        scratch_shapes=[pltpu.VMEM((tmx, HALF), U32), pltpu.SemaphoreType.DMA(()), pltpu.SemaphoreType.DMA(())],
    )
    return pl.pallas_call(
        _dispatch_kernel,
        grid_spec=grid_spec,
        out_shape=jax.ShapeDtypeStruct((n_slots, HALF), U32),
        compiler_params=pltpu.CompilerParams(dimension_semantics=("arbitrary",), vmem_limit_bytes=VMEM_LIMIT),
        name="dispatch",
    )(zfill, pos, hz_packed)


def _expert_kernel(te_ref, nt_ref, first_ref, slot_ref, next_ref, xs_ref, wg_hbm, wu_hbm, wd_hbm, ys_ref,
                   wg32, wu32, wd32, wgb, wub, wdb, wsem):
    i = pl.program_id(0)
    nt = nt_ref[0]

    def fetch(e, s):
        return [pltpu.make_async_copy(src.at[e], dst.at[s], wsem.at[s])
                for src, dst in ((wg_hbm, wg32), (wu_hbm, wu32), (wd_hbm, wd32))]

    @pl.when(i < nt)
    def _():
        @pl.when(i == 0)
        def _():
            for cp in fetch(te_ref[0], 0):
                cp.start()

        @pl.when(first_ref[i] != 0)
        def _():
            s = slot_ref[i]
            for cp in fetch(te_ref[i], s):
                cp.wait()

            @pl.when(next_ref[i] >= 0)
            def _():
                for cp in fetch(next_ref[i], 1 - s):
                    cp.start()

            wgb[...] = wg32[s].astype(BF16)
            wub[...] = wu32[s].astype(BF16)
            wdb[...] = wd32[s].astype(BF16)

        x = _unpack_halves(xs_ref[...]).astype(BF16)
        hg = jnp.dot(x, wgb[...], preferred_element_type=F32)
        hu = jnp.dot(x, wub[...], preferred_element_type=F32)
        a = (hg * jax.nn.sigmoid(hg) * hu).astype(BF16)
        ys_ref[...] = _pack_halves(jnp.dot(a, wdb[...], preferred_element_type=F32))

    @pl.when(i >= nt)
    def _():
        ys_ref[...] = jnp.zeros_like(ys_ref)


def _experts(xs, w_gate, w_up, w_down, tile_expert, ntiles, run_first, run_slot, run_next, tmx):
    n_tiles_max = tile_expert.shape[0]
    hbm = pl.BlockSpec(memory_space=pl.ANY)
    grid_spec = pltpu.PrefetchScalarGridSpec(
        num_scalar_prefetch=5,
        grid=(n_tiles_max,),
        in_specs=[pl.BlockSpec((tmx, HALF), lambda i, te, nt, *_: (jnp.minimum(i, nt[0] - 1), 0)), hbm, hbm, hbm],
        out_specs=pl.BlockSpec((tmx, HALF), lambda i, *_: (i, 0)),
        scratch_shapes=[pltpu.VMEM((2, D_MODEL, D_EXPERT), F32),
                        pltpu.VMEM((2, D_MODEL, D_EXPERT), F32),
                        pltpu.VMEM((2, D_EXPERT, D_MODEL), F32),
                        pltpu.VMEM((D_MODEL, D_EXPERT), BF16),
                        pltpu.VMEM((D_MODEL, D_EXPERT), BF16),
                        pltpu.VMEM((D_EXPERT, D_MODEL), BF16),
                        pltpu.SemaphoreType.DMA((2,))],
    )
    return pl.pallas_call(
        _expert_kernel,
        grid_spec=grid_spec,
        out_shape=jax.ShapeDtypeStruct(xs.shape, U32),
        compiler_params=pltpu.CompilerParams(dimension_semantics=("arbitrary",), vmem_limit_bytes=VMEM_LIMIT),
        name="experts",
    )(tile_expert, ntiles, run_first, run_slot, run_next, xs, w_gate, w_up, w_down)


def _final_kernel(posc_ref, posn_ref, x2_ref, route_ref, g_ref, ys_hbm, o_ref, ybuf, sem):
    i = pl.program_id(0)
    n = pl.num_programs(0)
    tm = x2_ref.shape[0]
    slot = i % 2

    def issue(pos_ref, s, rows):
        for r in rows:
            for k in range(TOP_K):
                pltpu.make_async_copy(ys_hbm.at[pl.ds(pos_ref[0, 0, k * tm + r], 1)],
                                      ybuf.at[s, k, pl.ds(r, 1)], sem.at[s]).start(priority=k % 2)

    def wait(s):
        for k in range(TOP_K):
            pltpu.make_async_copy(ys_hbm.at[pl.ds(0, tm)], ybuf.at[s, k], sem.at[s]).wait()

    def combine(s, rows):
        r = route_ref[rows, :]
        g1 = r[:, ROUTE_G1:ROUTE_G1 + 1]
        g2 = r[:, ROUTE_G2:ROUTE_G2 + 1]
        xo = x2_ref[rows, :] + g1 * _unpack_halves(ybuf[s, 0, rows, :]) + g2 * _unpack_halves(ybuf[s, 1, rows, :])
        o_ref[rows, :] = _rms(xo, g_ref[...])

    @pl.when(i == 0)
    def _():
        issue(posc_ref, 0, range(tm))

    wait(slot)

    group = tm // FINAL_GROUPS
    for s in range(2):
        @pl.when(slot == s)
        def _():
            for j in range(FINAL_GROUPS):
                combine(s, pl.ds(j * group, group))
                issue(posn_ref, 1 - s, range(j * group, (j + 1) * group))

    @pl.when(i == n - 1)
    def _():
        wait(1 - slot)


def _final(x2, ys, pos, route, g):
    T = x2.shape[0]
    nblk = pos.shape[0]
    tm = T // nblk
    row = lambda w: pl.BlockSpec((tm, w), lambda i: (i, 0))
    return pl.pallas_call(
        _final_kernel,
        grid=(nblk,),
        in_specs=[pl.BlockSpec((1, 1, TOP_K * tm), lambda i: (i, 0, 0), memory_space=pltpu.SMEM),
                  pl.BlockSpec((1, 1, TOP_K * tm), lambda i: (jnp.minimum(i + 1, nblk - 1), 0, 0),
                               memory_space=pltpu.SMEM),
                  row(D_MODEL), row(LANES), pl.BlockSpec(g.shape, lambda i: (0, 0)),
                  pl.BlockSpec(memory_space=pl.ANY)],
        out_specs=row(D_MODEL),
        out_shape=jax.ShapeDtypeStruct((T, D_MODEL), F32),
        scratch_shapes=[pltpu.VMEM((2, TOP_K, tm, HALF), U32), pltpu.SemaphoreType.DMA((2,))],
        compiler_params=pltpu.CompilerParams(dimension_semantics=("arbitrary",), vmem_limit_bytes=VMEM_LIMIT),
        name="final",
    )(pos, pos, x2, route, g, ys)


def _band_bias(table):
    assert WINDOW == BLOCK
    i = jnp.arange(BLOCK)[:, None]
    j = jnp.arange(2 * BLOCK)[None, :]
    n = jnp.maximum(i + BLOCK - j, 0)
    nf = jnp.maximum(n, 1).astype(F32)
    large = MAX_EXACT + (jnp.log(nf / MAX_EXACT) / math.log(MAX_DISTANCE / MAX_EXACT)
                         * (NUM_BUCKETS - MAX_EXACT)).astype(I32)
    large = jnp.minimum(large, NUM_BUCKETS - 1)
    bucket = jnp.where(n < MAX_EXACT, n, large)
    onehot = (bucket[:, :, None] == jnp.arange(NUM_BUCKETS)[None, None, :]).astype(F32)
    bias = jnp.einsum("ijb,bh->hij", onehot, table.astype(F32), precision=lax.Precision.HIGHEST)
    from_prev = (jnp.arange(BLOCK)[None, :] > i)[None]
    prev, cur = bias[:, :, :BLOCK], bias[:, :, BLOCK:]
    return jnp.stack([jnp.where(from_prev, NEG_INF, cur), jnp.where(from_prev, prev, cur)])


def _dispatch_plan(route_rows, counts_f, tmx, n_tiles_max, tm_rows):
    T = route_rows.shape[1]
    experts = jnp.arange(N_EXPERTS, dtype=I32)
    counts = counts_f[:N_EXPERTS, 0].astype(I32)
    ptiles = (counts + tmx - 1) // tmx
    tile_end = jnp.cumsum(ptiles)
    nt = tile_end[-1]
    row_off = (tile_end - ptiles) * tmx

    def slot(e_row, r_row):
        e = route_rows[e_row].astype(I32)
        off = jnp.sum(jnp.where(e[None, :] == experts[:, None], row_off[:, None], 0), axis=0)
        return (off + route_rows[r_row].astype(I32)).reshape(T // tm_rows, 1, tm_rows)

    pos = jnp.concatenate([slot(ROUTE_E1, ROUTE_R1), slot(ROUTE_E2, ROUTE_R2)], axis=2)

    tile_ids = jnp.arange(n_tiles_max, dtype=I32)
    expert_of = lambda t: jnp.sum((tile_end[None, :] <= t[:, None]).astype(I32), axis=1)
    te = expert_of(jnp.minimum(tile_ids, nt - 1))
    partial = jnp.any((tile_ids[:, None] == (tile_end - 1)[None, :]) & (counts % tmx != 0)[None, :], axis=1)
    zfill = (partial | (tile_ids >= nt)).astype(I32)

    used = ptiles > 0
    run_first = (jnp.any((tile_ids[:, None] == (tile_end - ptiles)[None, :]) & used[None, :], axis=1)
                 & (tile_ids < nt)).astype(I32)
    run_slot = (jnp.cumsum(run_first) - 1) % 2
    later_used = used[None, :] & (experts[None, :] > experts[:, None])
    next_of = jnp.min(jnp.where(later_used, experts[None, :], N_EXPERTS), axis=1)
    next_of = jnp.where(next_of < N_EXPERTS, next_of, -1)
    run_next = jnp.sum(jnp.where(te[:, None] == experts[None, :], next_of[None, :], 0), axis=1)
    return pos, te, nt.reshape(1), zfill, run_first, run_slot.astype(I32), run_next.astype(I32)


def kernel(x, mem, rel_bias_table, norm_mix, w_in, attn_sinks, conv_w, conv_b, gate_bias_i, gate_bias_f, mlstm_norm, w_out, norm_cross, norm_mem, w_cq, w_ck, w_cv, w_co, norm_moe, w_router_group, b_router_group, w_router_expert, b_router_expert, w_exp_gate, w_exp_up, w_exp_down, norm_final):
    B, S, _ = x.shape
    T = B * S
    depth = w_in.shape[0]
    x2d = x.reshape(T, D_MODEL)
    mem2d = mem.reshape(B * N_MEM, D_MODEL)
    bias = _band_bias(rel_bias_table)

    tmx = min(TM_EXPERT, T)
    n_tiles_max = (T * TOP_K) // tmx + N_EXPERTS
    tm_rows = min(TM_ROWDMA, T)

    assert depth == 1, "the final combine is fused with the final norm: single layer only"
    l = 0
    w_pad = jnp.pad(w_in[l], ((0, 0), (0, C_GATE + LANES - D_IN))).astype(BF16)
    gb = jnp.concatenate([gate_bias_i[l], gate_bias_f[l]]).astype(F32)
    gbias_col = jnp.pad(gb, (0, LANES - GATE_ROWS))[None, :]
    qa, kva, qkb, vb, ob, gc, gr = _inproj(x2d, norm_mix[l][None, :], w_pad, gbias_col, B, S)

    per_seq = lambda a: a.reshape(B, S, a.shape[-1])
    mix = _seqmix(per_seq(qkb), per_seq(vb), per_seq(ob), per_seq(gc), gr, per_seq(qa), per_seq(kva),
                  conv_w[l][:, 0, :].astype(F32), conv_b[l][None, :].astype(F32),
                  mlstm_norm[l][None, :].astype(F32), bias, attn_sinks[l].astype(F32)).reshape(T, W_A_Q + W_B)

    ck, cv = _memkv(mem2d, norm_mem[l][None, :], w_ck[l].astype(BF16), w_cv[l].astype(BF16), B)

    wr = jnp.pad(jnp.concatenate([w_router_expert[l], w_router_group[l]], axis=1),
                 ((0, 0), (0, LANES - N_EXPERTS - N_GROUPS))).astype(BF16)
    br = jnp.pad(jnp.concatenate([b_router_expert[l], b_router_group[l]]),
                 (0, LANES - N_EXPERTS - N_GROUPS)).astype(F32)[None, :]
    x2, hz_packed, route, route_rows, counts = _mid(
        x2d, mix, w_out[l].astype(BF16), norm_cross[l][None, :], w_cq[l].astype(BF16), ck, cv,
        w_co[l].astype(BF16), norm_moe[l][None, :], wr, br, B, S)

    pos, te, nt, zfill, run_first, run_slot, run_next = _dispatch_plan(route_rows, counts, tmx, n_tiles_max,
                                                                       tm_rows)
    xs = _dispatch(hz_packed, pos, zfill, n_tiles_max * tmx, tmx)
    ys = _experts(xs, w_exp_gate[l], w_exp_up[l], w_exp_down[l], te, nt, run_first, run_slot, run_next, tmx)
    out = _final(x2, ys, pos, route, norm_final[None, :])
    return out.reshape(B, S, D_MODEL)
```

```python
import math

import jax
import jax.numpy as jnp
from jax import lax
from jax.experimental import pallas as pl
from jax.experimental.pallas import tpu as pltpu

F32 = jnp.float32
BF16 = jnp.bfloat16
U32 = jnp.uint32
I32 = jnp.int32

D_MODEL = 1024
N_MEM = 256
N_HEADS_A = 8
N_KV_A = 2
HEAD_DIM_A = 64
BLOCK = 128
WINDOW = 128
NUM_BUCKETS = 32
MAX_EXACT = NUM_BUCKETS // 2
MAX_DISTANCE = 128
N_HEADS_B = 4
HEAD_DIM_B = 128
CHUNK = 128
CONV_WIDTH = 4
N_HEADS_X = 4
HEAD_DIM_X = D_MODEL // N_HEADS_X
N_GROUPS = 4
EXPERTS_PER_GROUP = 8
N_EXPERTS = N_GROUPS * EXPERTS_PER_GROUP
TOP_K = 2
D_EXPERT = 512
EPS = 1e-6
NEG_INF = -1e30

W_A_Q = N_HEADS_A * HEAD_DIM_A
W_A_KV = N_KV_A * HEAD_DIM_A
W_B = N_HEADS_B * HEAD_DIM_B
C_QA = 0
C_KVA = C_QA + W_A_Q
C_QKB = C_KVA + 2 * W_A_KV
C_VB = C_QKB + 2 * W_B
C_OB = C_VB + W_B
C_GATE = C_OB + W_B
D_IN = C_GATE + 2 * N_HEADS_B

LANES = 128
SUBLANES = 8
GATE_ROWS = 8
HALF = D_MODEL // 2

TM_INPROJ = 1024
TM_MID = 1024
TM_ROWDMA = 512
TM_EXPERT = 512

VMEM_LIMIT = 48 * 1024 * 1024


def _rms(xf, g):
    return xf * lax.rsqrt(jnp.mean(xf * xf, axis=-1, keepdims=True) + EPS) * g


def _pack_halves(v):
    b = pltpu.bitcast(v.astype(BF16).astype(F32), U32)
    return (b[:, :HALF] >> 16) | b[:, HALF:]


def _unpack_halves(p):
    lo = pltpu.bitcast(p << 16, F32)
    hi = pltpu.bitcast(p & jnp.uint32(0xFFFF0000), F32)
    return jnp.concatenate([lo, hi], axis=-1)


def _log_sigmoid(z):
    return jnp.minimum(z, 0.0) - jnp.log1p(jnp.exp(-jnp.abs(z)))


def _split3(v):
    hi = v.astype(BF16).astype(F32)
    rest = v - hi
    mid = rest.astype(BF16).astype(F32)
    return hi, mid, (rest - mid).astype(BF16).astype(F32)


def _inproj_kernel(x_ref, g_ref, w_ref, gbc_ref, qa_ref, kva_ref, qkb_ref, vb_ref, ob_ref, gc_ref, gr_ref):
    tm = x_ref.shape[0]
    h = _rms(x_ref[...], g_ref[...]).astype(BF16)

    def mm(lo, hi):
        return jnp.dot(h, w_ref[:, lo:hi], preferred_element_type=F32)

    qa_ref[...] = mm(C_QA, C_KVA).astype(BF16)
    kva_ref[...] = mm(C_KVA, C_QKB).astype(BF16)
    qkb_ref[...] = mm(C_QKB, C_VB).astype(BF16)
    vb_ref[...] = mm(C_VB, C_OB).astype(BF16)
    ob_ref[...] = mm(C_OB, C_GATE).astype(BF16)

    H, L = N_HEADS_B, CHUNK
    gcol = mm(C_GATE, C_GATE + LANES) + gbc_ref[...]
    grow = jnp.transpose(gcol)[0:GATE_ROWS, :]
    lane_c = lax.broadcasted_iota(I32, (L, LANES), 1)
    is_f_col = (lane_c >= H) & (lane_c < 2 * H)
    is_f_row = lax.broadcasted_iota(I32, (GATE_ROWS, L), 0) >= H
    ti = lax.broadcasted_iota(I32, (L, L), 0)
    si = lax.broadcasted_iota(I32, (L, L), 1)
    tril = jnp.where(si <= ti, 1.0, 0.0).astype(BF16)
    triu = jnp.where(si >= ti, 1.0, 0.0).astype(BF16)
    for c in range(tm // L):
        rows = slice(c * L, (c + 1) * L)
        gcol_c = gcol[rows, :]
        fcol = jnp.where(is_f_col, _log_sigmoid(gcol_c), 0.0)
        parts = jnp.dot(tril, jnp.concatenate(_split3(fcol), axis=1).astype(BF16), preferred_element_type=F32)
        bcol = parts[:, :LANES] + parts[:, LANES:2 * LANES] + parts[:, 2 * LANES:]
        gc_ref[rows, :] = jnp.where(is_f_col, bcol, gcol_c)
        grow_c = grow[:, rows]
        frow = jnp.where(is_f_row, _log_sigmoid(grow_c), 0.0)
        parts = jnp.dot(jnp.concatenate(_split3(frow), axis=0).astype(BF16), triu, preferred_element_type=F32)
        brow = parts[:GATE_ROWS] + parts[GATE_ROWS:2 * GATE_ROWS] + parts[2 * GATE_ROWS:]
        gr_ref[:, rows] = jnp.where(is_f_row, brow, grow_c)


def _inproj(x2d, g, w_pad, gbias_col, B, S):
    T = x2d.shape[0]
    tm = min(TM_INPROJ, S)
    tiles_per_seq = S // tm
    row = lambda w: pl.BlockSpec((tm, w), lambda i: (i, 0))
    full = lambda a: pl.BlockSpec(a.shape, lambda i: (0,) * a.ndim)
    return pl.pallas_call(
        _inproj_kernel,
        grid=(T // tm,),
        in_specs=[row(D_MODEL), full(g), full(w_pad), full(gbias_col)],
        out_specs=[row(W_A_Q), row(2 * W_A_KV), row(2 * W_B), row(W_B), row(W_B), row(LANES),
                   pl.BlockSpec((None, GATE_ROWS, tm), lambda i: (i // tiles_per_seq, 0, i % tiles_per_seq))],
        out_shape=[jax.ShapeDtypeStruct((T, W_A_Q), BF16),
                   jax.ShapeDtypeStruct((T, 2 * W_A_KV), BF16),
                   jax.ShapeDtypeStruct((T, 2 * W_B), BF16),
                   jax.ShapeDtypeStruct((T, W_B), BF16),
                   jax.ShapeDtypeStruct((T, W_B), BF16),
                   jax.ShapeDtypeStruct((T, LANES), F32),
                   jax.ShapeDtypeStruct((B, GATE_ROWS, S), F32)],
        compiler_params=pltpu.CompilerParams(dimension_semantics=("parallel",), vmem_limit_bytes=VMEM_LIMIT),
        name="inproj",
    )(x2d, g, w_pad, gbias_col)


def _swa_block(q, kvc, kvp, bias_ref, sink_ref):
    kvp = kvp.astype(F32)
    kvc = kvc.astype(F32)
    kband = jnp.concatenate([kvp[:, :W_A_KV], kvc[:, :W_A_KV]], axis=0)
    vband = jnp.concatenate([kvp[:, W_A_KV:], kvc[:, W_A_KV:]], axis=0)
    lane = lax.broadcasted_iota(I32, (2 * BLOCK, LANES), 1)
    lo = lane < HEAD_DIM_A

    def placements(band):
        swapped = pltpu.roll(band, HEAD_DIM_A, axis=1)
        z = jnp.zeros_like(band)
        return {(0, 0): jnp.where(lo, band, z).astype(BF16), (0, 1): jnp.where(lo, z, swapped).astype(BF16),
                (1, 0): jnp.where(lo, swapped, z).astype(BF16), (1, 1): jnp.where(lo, z, band).astype(BF16)}

    kpl = placements(kband)
    vpl = placements(vband)

    scale = HEAD_DIM_A ** -0.5
    group = N_HEADS_A // N_KV_A
    tiles = []
    from_prev = (lax.broadcasted_iota(I32, (BLOCK, BLOCK), 1) > lax.broadcasted_iota(I32, (BLOCK, BLOCK), 0))

    for pair in range(N_HEADS_A // 2):
        qt = q[:, pair * LANES:(pair + 1) * LANES]
        acc = None
        for half in range(2):
            h = 2 * pair + half
            g = h // group
            s2 = lax.dot_general(qt, kpl[(g, half)], (((1,), (1,)), ((), ())), preferred_element_type=F32)
            s = jnp.where(from_prev, s2[:, :BLOCK], s2[:, BLOCK:])
            s = s * scale + bias_ref[h]
            sink = sink_ref[h]
            m = jnp.maximum(jnp.max(s, axis=-1, keepdims=True), sink)
            p = jnp.exp(s - m)
            denom = jnp.sum(p, axis=-1, keepdims=True) + jnp.exp(sink - m)
            p2 = jnp.concatenate([jnp.where(from_prev, p, 0.0), jnp.where(from_prev, 0.0, p)], axis=1)
            o = jnp.dot(p2.astype(BF16), vpl[(g, half)], preferred_element_type=F32) / denom
            acc = o if acc is None else acc + o
        tiles.append(acc.astype(BF16))
    return tiles


CONV_HALO = 16


def _seqmix_kernel(sink_ref, qkc_ref, qkp_ref, vb_ref, ob_ref, gc_ref, gr_ref, qa_ref, kvc_ref, kvp_ref,
                   cw_ref, cb_ref, nrm_ref, bias_ref, o_ref, state_ref, m_ref):
    c = pl.program_id(0)
    B = qkc_ref.shape[0]
    H, D, L = N_HEADS_B, HEAD_DIM_B, CHUNK

    @pl.when(c == 0)
    def _():
        state_ref[...] = jnp.zeros_like(state_ref)
        m_ref[...] = jnp.zeros_like(m_ref)

    rr = lax.broadcasted_iota(I32, (L, CONV_HALO + L), 0)
    cc = lax.broadcasted_iota(I32, (L, CONV_HALO + L), 1)
    shifts = {delay: jnp.where(cc == rr + (CONV_HALO - delay), 1.0, 0.0).astype(BF16)
              for delay in range(1, CONV_WIDTH)}
    ti = lax.broadcasted_iota(I32, (L, L), 0)
    si = lax.broadcasted_iota(I32, (L, L), 1)
    tri = si <= ti
    ones_blk = jnp.ones((L, D), BF16)

    def conv_silu(b):
        prev = qkp_ref[b]
        prev = jnp.where(c > 0, prev, jnp.zeros_like(prev))
        cur = qkc_ref[b]
        ext = jnp.concatenate([prev, cur], axis=0)
        y = cb_ref[...] + cw_ref[CONV_WIDTH - 1:CONV_WIDTH, :] * cur.astype(F32)
        for delay in range(1, CONV_WIDTH):
            tap = CONV_WIDTH - 1 - delay
            y = y + cw_ref[tap:tap + 1, :] * jnp.dot(shifts[delay], ext, preferred_element_type=F32)
        return y * jax.nn.sigmoid(y)

    states = [[state_ref[b, h] for h in range(H)] for b in range(B)]
    m_alls = [m_ref[b] for b in range(B)]
    new_states, new_m, outs = {}, {}, {}

    for b, h in [(b, h) for b in range(B) for h in range(H)]:
        if h == 0:
            for pair, tile in enumerate(_swa_block(qa_ref[b], kvc_ref[b], kvp_ref[b], bias_ref, sink_ref)):
                o_ref[b, :, pair * LANES:(pair + 1) * LANES] = tile
            qk = conv_silu(b)
            gcol = gc_ref[b]
            grow = gr_ref[b]
        qh = (qk[:, h * D:(h + 1) * D] * (D ** -0.5)).astype(BF16)
        k_t = qk[:, W_B + h * D:W_B + (h + 1) * D].T
        v1 = jnp.concatenate([vb_ref[b, :, h * D:(h + 1) * D], ones_blk], axis=-1)
        b_r = grow[H + h:H + h + 1, :]
        g_r = grow[h:h + 1, :] - b_r
        b_c = gcol[:, H + h:H + h + 1]
        m_prev = m_alls[b][h:h + 1, 0:1]
        state = states[b][h]

        gmat = jnp.where(tri, g_r, NEG_INF)
        m_c = jnp.maximum(jnp.max(gmat, axis=-1, keepdims=True), m_prev)
        a_inter = jnp.exp(m_prev - m_c)
        sc = jnp.dot(qh, k_t.astype(BF16), preferred_element_type=F32) * jnp.exp(gmat - m_c)
        tot = (jnp.dot(sc.astype(BF16), v1, preferred_element_type=F32)
               + a_inter * jnp.dot(qh, state.astype(BF16), preferred_element_type=F32))
        num = tot[:, :D]
        den = tot[:, D:]
        hh = num / jnp.maximum(jnp.abs(den), jnp.exp(-(b_c + m_c)))

        b_last = b_r[:, L - 1:L]
        m_new = jnp.maximum(b_last + m_prev, b_last + jnp.max(g_r, axis=-1, keepdims=True))
        w_r = jnp.exp(g_r + (b_last - m_new))
        decay = jnp.exp(b_last + m_prev - m_new)
        upd = jnp.dot((k_t * w_r).astype(BF16), v1, preferred_element_type=F32)
        new_states[b, h] = decay * state + upd
        new_m[b, h] = jnp.broadcast_to(m_new, (1, LANES))

        og = jax.nn.sigmoid(ob_ref[b, :, h * D:(h + 1) * D].astype(F32))
        hb = og * hh
        hb = hb * lax.rsqrt(jnp.mean(hb * hb, axis=-1, keepdims=True) + EPS)
        outs[b, h] = (hb * nrm_ref[:, h * D:(h + 1) * D]).astype(BF16)

    for b, h in new_states:
        state_ref[b, h] = new_states[b, h]
        m_ref[b, h:h + 1, :] = new_m[b, h]
        o_ref[b, :, W_A_Q + h * D:W_A_Q + (h + 1) * D] = outs[b, h]


def _seqmix(qkb, vb, ob, gc, gr, qa, kva, conv_w, conv_b, nrm, bias, sinks):
    assert CHUNK == BLOCK
    B, S, _ = qkb.shape
    nc = S // CHUNK
    halo_per_chunk = CHUNK // CONV_HALO
    blk = lambda w: pl.BlockSpec((B, CHUNK, w), lambda c: (0, c, 0))
    full = lambda a: pl.BlockSpec(a.shape, lambda c: (0,) * a.ndim)
    return pl.pallas_call(
        _seqmix_kernel,
        grid=(nc,),
        in_specs=[pl.BlockSpec(memory_space=pltpu.SMEM),
                  blk(2 * W_B),
                  pl.BlockSpec((B, CONV_HALO, 2 * W_B), lambda c: (0, jnp.maximum(c * halo_per_chunk - 1, 0), 0)),
                  blk(W_B), blk(W_B), blk(LANES),
                  pl.BlockSpec((B, GATE_ROWS, CHUNK), lambda c: (0, 0, c)),
                  blk(W_A_Q), blk(2 * W_A_KV),
                  pl.BlockSpec((B, BLOCK, 2 * W_A_KV), lambda c: (0, jnp.maximum(c - 1, 0), 0)),
                  full(conv_w), full(conv_b), full(nrm),
                  pl.BlockSpec((None,) + bias.shape[1:], lambda c: (jnp.minimum(c, 1), 0, 0, 0))],
        out_specs=blk(W_A_Q + W_B),
        out_shape=jax.ShapeDtypeStruct((B, S, W_A_Q + W_B), BF16),
        scratch_shapes=[pltpu.VMEM((B, N_HEADS_B, HEAD_DIM_B, 2 * HEAD_DIM_B), F32),
                        pltpu.VMEM((B, GATE_ROWS, LANES), F32)],
        compiler_params=pltpu.CompilerParams(dimension_semantics=("arbitrary",), vmem_limit_bytes=VMEM_LIMIT),
        name="seqmix",
    )(sinks, qkb, qkb, vb, ob, gc, gr, qa, kva, kva, conv_w, conv_b, nrm, bias)


def _memkv_kernel(mem_ref, g_ref, wk_ref, wv_ref, k_ref, v_ref):
    hm = _rms(mem_ref[...], g_ref[...]).astype(BF16)
    k_ref[...] = jnp.dot(hm, wk_ref[...], preferred_element_type=F32).astype(BF16)
    v_ref[...] = jnp.dot(hm, wv_ref[...], preferred_element_type=F32).astype(BF16)


def _memkv(mem2d, g, wk, wv, B):
    full = lambda a: pl.BlockSpec(a.shape, lambda b: (0,) * a.ndim)
    blk = pl.BlockSpec((N_MEM, D_MODEL), lambda b: (b, 0))
    return pl.pallas_call(
        _memkv_kernel,
        grid=(B,),
        in_specs=[blk, full(g), full(wk), full(wv)],
        out_specs=[blk, blk],
        out_shape=[jax.ShapeDtypeStruct((B * N_MEM, D_MODEL), BF16)] * 2,
        compiler_params=pltpu.CompilerParams(dimension_semantics=("parallel",), vmem_limit_bytes=VMEM_LIMIT),
        name="memkv",
    )(mem2d, g, wk, wv)


ROUTE_E1, ROUTE_E2, ROUTE_G1, ROUTE_G2, ROUTE_R1, ROUTE_R2 = 0, 1, 2, 3, 4, 5
ROUTER_GROUP_COL = N_EXPERTS
ROUTER_ROWS = -(-(N_EXPERTS + N_GROUPS) // SUBLANES) * SUBLANES


def _mid_kernel(x_ref, mix_ref, wo_ref, gx_ref, wq_ref, ck_ref, cv_ref, wco_ref, gz_ref, wr_ref, br_ref,
                x2_ref, hz_ref, route_ref, rrows_ref, counts_ref, cnt_ref):
    @pl.when(pl.program_id(0) == 0)
    def _():
        cnt_ref[...] = jnp.zeros_like(cnt_ref)

    x1 = x_ref[...] + jnp.dot(mix_ref[...], wo_ref[...], preferred_element_type=F32)

    hc = _rms(x1, gx_ref[...]).astype(BF16)
    cq = jnp.dot(hc, wq_ref[...], preferred_element_type=F32).astype(BF16)
    scale = HEAD_DIM_X ** -0.5
    heads = []
    for h in range(N_HEADS_X):
        sl = slice(h * HEAD_DIM_X, (h + 1) * HEAD_DIM_X)
        s = lax.dot_general(cq[:, sl], ck_ref[:, sl], (((1,), (1,)), ((), ())), preferred_element_type=F32) * scale
        p = jnp.exp(s - jnp.max(s, axis=-1, keepdims=True))
        co = jnp.dot(p.astype(BF16), cv_ref[:, sl], preferred_element_type=F32) / jnp.sum(p, axis=-1, keepdims=True)
        heads.append(co.astype(BF16))
    x2 = x1 + jnp.dot(jnp.concatenate(heads, axis=-1), wco_ref[...], preferred_element_type=F32)
    x2_ref[...] = x2

    hz = _rms(x2, gz_ref[...])
    hz_ref[...] = _pack_halves(hz)
    lg = jnp.dot(hz.astype(BF16), wr_ref[...], preferred_element_type=F32) + br_ref[...]
    tm = lg.shape[0]
    lt = jnp.transpose(lg)[0:ROUTER_ROWS, :]
    row = lax.broadcasted_iota(I32, lt.shape, 0)
    big = jnp.int32(ROUTER_ROWS)
    is_g = (row >= ROUTER_GROUP_COL) & (row < ROUTER_GROUP_COL + N_GROUPS)
    gl = jnp.where(is_g, lt, NEG_INF)
    gmax = jnp.max(gl, axis=0, keepdims=True)
    gsum = jnp.sum(jnp.exp(gl - gmax), axis=0, keepdims=True)
    g_prob = 1.0 / gsum
    g_idx = jnp.min(jnp.where(gl == gmax, row - ROUTER_GROUP_COL, big), axis=0, keepdims=True)
    sel = (row < N_EXPERTS) & ((row // EXPERTS_PER_GROUP) == g_idx)
    el = jnp.where(sel, lt, NEG_INF)
    m1 = jnp.max(el, axis=0, keepdims=True)
    i1 = jnp.min(jnp.where(el == m1, row, big), axis=0, keepdims=True)
    el2 = jnp.where(row == i1, NEG_INF, el)
    m2 = jnp.max(el2, axis=0, keepdims=True)
    i2 = jnp.min(jnp.where(el2 == m2, row, big), axis=0, keepdims=True)
    z = jnp.sum(jnp.exp(el - m1), axis=0, keepdims=True)
    p1 = 1.0 / z
    p2 = jnp.exp(m2 - m1) / z
    g1 = g_prob * (p1 / (p1 + p2))
    g2 = g_prob * (p2 / (p1 + p2))

    used = jnp.where((row == i1) | (row == i2), 1.0, 0.0)
    t_from = lax.broadcasted_iota(I32, (tm, tm), 0)
    t_to = lax.broadcasted_iota(I32, (tm, tm), 1)
    earlier = jnp.where(t_from < t_to, 1.0, 0.0).astype(BF16)
    before = jnp.dot(used.astype(BF16), earlier, preferred_element_type=F32) + cnt_ref[:, 0:1]
    r1 = jnp.sum(jnp.where(row == i1, before, 0.0), axis=0, keepdims=True)
    r2 = jnp.sum(jnp.where(row == i2, before, 0.0), axis=0, keepdims=True)
    cnt_ref[...] = cnt_ref[...] + jnp.sum(used, axis=1, keepdims=True)
    counts_ref[...] = cnt_ref[...]

    rec_row = lax.broadcasted_iota(I32, (SUBLANES, tm), 0)
    rec = jnp.zeros((SUBLANES, tm), F32)
    for c, v in ((ROUTE_E1, i1.astype(F32)), (ROUTE_E2, i2.astype(F32)), (ROUTE_G1, g1), (ROUTE_G2, g2),
                 (ROUTE_R1, r1), (ROUTE_R2, r2)):
        rec = jnp.where(rec_row == c, v, rec)
    rrows_ref[...] = rec
    route_ref[...] = jnp.transpose(jnp.concatenate([rec, jnp.zeros((LANES - SUBLANES, tm), F32)], axis=0))


def _mid(x2d, mix, wo, gx, wq, ck, cv, wco, gz, wr, br, B, S):
    T = B * S
    tm = min(TM_MID, S)
    per_b = S // tm
    row = lambda w: pl.BlockSpec((tm, w), lambda i: (i, 0))
    full = lambda a: pl.BlockSpec(a.shape, lambda i: (0,) * a.ndim)
    kvspec = pl.BlockSpec((N_MEM, D_MODEL), lambda i: (i // per_b, 0))
    return pl.pallas_call(
        _mid_kernel,
        grid=(T // tm,),
        in_specs=[row(D_MODEL), row(W_A_Q + W_B), full(wo), full(gx), full(wq), kvspec, kvspec,
                  full(wco), full(gz), full(wr), full(br)],
        out_specs=[row(D_MODEL), row(HALF), row(LANES), pl.BlockSpec((SUBLANES, tm), lambda i: (0, i)),
                   pl.BlockSpec((ROUTER_ROWS, LANES), lambda i: (0, 0))],
        out_shape=[jax.ShapeDtypeStruct((T, D_MODEL), F32),
                   jax.ShapeDtypeStruct((T, HALF), U32),
                   jax.ShapeDtypeStruct((T, LANES), F32),
                   jax.ShapeDtypeStruct((SUBLANES, T), F32),
                   jax.ShapeDtypeStruct((ROUTER_ROWS, LANES), F32)],
        scratch_shapes=[pltpu.VMEM((ROUTER_ROWS, LANES), F32)],
        compiler_params=pltpu.CompilerParams(dimension_semantics=("arbitrary",), vmem_limit_bytes=VMEM_LIMIT),
        name="mid",
    )(x2d, mix, wo, gx, wq, ck, cv, wco, gz, wr, br)


def _dispatch_kernel(zf_ref, pos_ref, hz_ref, xs_hbm, zbuf, sem, zsem):
    i = pl.program_id(0)
    tm = hz_ref.shape[0]
    zrows = zbuf.shape[0]

    @pl.when(i == 0)
    def _():
        zbuf[...] = jnp.zeros_like(zbuf)

        def fill(t):
            return pltpu.make_async_copy(zbuf, xs_hbm.at[pl.ds(t * zrows, zrows)], zsem)

        def start(t, carry):
            @pl.when(zf_ref[t] != 0)
            def _():
                fill(t).start()
            return carry

        def wait(t, carry):
            @pl.when(zf_ref[t] != 0)
            def _():
                fill(t).wait()
            return carry

        lax.fori_loop(0, zf_ref.shape[0], start, 0)
        lax.fori_loop(0, zf_ref.shape[0], wait, 0)

    for r in range(tm):
        for k in range(TOP_K):
            pltpu.make_async_copy(hz_ref.at[pl.ds(r, 1)], xs_hbm.at[pl.ds(pos_ref[0, 0, k * tm + r], 1)],
                                  sem).start(priority=1)
    for k in range(TOP_K):
        pltpu.make_async_copy(hz_ref, xs_hbm.at[pl.ds(0, tm)], sem).wait()


def _dispatch(hz_packed, pos, zfill, n_slots, tmx):
    T = hz_packed.shape[0]
    tm = pos.shape[2] // TOP_K
    grid_spec = pltpu.PrefetchScalarGridSpec(
        num_scalar_prefetch=1,
        grid=(T // tm,),
        in_specs=[pl.BlockSpec((1, 1, TOP_K * tm), lambda i, zf: (i, 0, 0), memory_space=pltpu.SMEM),
                  pl.BlockSpec((tm, HALF), lambda i, zf: (i, 0))],
        out_specs=pl.BlockSpec(memory_space=pl.ANY),
        scratch_shapes=[pltpu.VMEM((tmx, HALF), U32), pltpu.SemaphoreType.DMA(()), pltpu.SemaphoreType.DMA(())],
    )
    return pl.pallas_call(
        _dispatch_kernel,
        grid_spec=grid_spec,
        out_shape=jax.ShapeDtypeStruct((n_slots, HALF), U32),
        compiler_params=pltpu.CompilerParams(dimension_semantics=("arbitrary",), vmem_limit_bytes=VMEM_LIMIT),
        name="dispatch",
    )(zfill, pos, hz_packed)


def _expert_kernel(te_ref, nt_ref, first_ref, slot_ref, next_ref, xs_ref, wg_hbm, wu_hbm, wd_hbm, ys_ref,
                   wg32, wu32, wd32, wgb, wub, wdb, wsem):
    i = pl.program_id(0)
    nt = nt_ref[0]

    def fetch(e, s):
        return [pltpu.make_async_copy(src.at[e], dst.at[s], wsem.at[s])
                for src, dst in ((wg_hbm, wg32), (wu_hbm, wu32), (wd_hbm, wd32))]

    @pl.when(i < nt)
    def _():
        @pl.when(i == 0)
        def _():
            for cp in fetch(te_ref[0], 0):
                cp.start()

        @pl.when(first_ref[i] != 0)
        def _():
            s = slot_ref[i]
            for cp in fetch(te_ref[i], s):
                cp.wait()

            @pl.when(next_ref[i] >= 0)
            def _():
                for cp in fetch(next_ref[i], 1 - s):
                    cp.start()

            wgb[...] = wg32[s].astype(BF16)
            wub[...] = wu32[s].astype(BF16)
            wdb[...] = wd32[s].astype(BF16)

        x = _unpack_halves(xs_ref[...]).astype(BF16)
        hg = jnp.dot(x, wgb[...], preferred_element_type=F32)
        hu = jnp.dot(x, wub[...], preferred_element_type=F32)
        a = (hg * jax.nn.sigmoid(hg) * hu).astype(BF16)
        ys_ref[...] = _pack_halves(jnp.dot(a, wdb[...], preferred_element_type=F32))

    @pl.when(i >= nt)
    def _():
        ys_ref[...] = jnp.zeros_like(ys_ref)


def _experts(xs, w_gate, w_up, w_down, tile_expert, ntiles, run_first, run_slot, run_next, tmx):
    n_tiles_max = tile_expert.shape[0]
    hbm = pl.BlockSpec(memory_space=pl.ANY)
    grid_spec = pltpu.PrefetchScalarGridSpec(
        num_scalar_prefetch=5,
        grid=(n_tiles_max,),
        in_specs=[pl.BlockSpec((tmx, HALF), lambda i, te, nt, *_: (jnp.minimum(i, nt[0] - 1), 0)), hbm, hbm, hbm],
        out_specs=pl.BlockSpec((tmx, HALF), lambda i, *_: (i, 0)),
        scratch_shapes=[pltpu.VMEM((2, D_MODEL, D_EXPERT), F32),
                        pltpu.VMEM((2, D_MODEL, D_EXPERT), F32),
                        pltpu.VMEM((2, D_EXPERT, D_MODEL), F32),
                        pltpu.VMEM((D_MODEL, D_EXPERT), BF16),
                        pltpu.VMEM((D_MODEL, D_EXPERT), BF16),
                        pltpu.VMEM((D_EXPERT, D_MODEL), BF16),
                        pltpu.SemaphoreType.DMA((2,))],
    )
    return pl.pallas_call(
        _expert_kernel,
        grid_spec=grid_spec,
        out_shape=jax.ShapeDtypeStruct(xs.shape, U32),
        compiler_params=pltpu.CompilerParams(dimension_semantics=("arbitrary",), vmem_limit_bytes=VMEM_LIMIT),
        name="experts",
    )(tile_expert, ntiles, run_first, run_slot, run_next, xs, w_gate, w_up, w_down)


def _final_kernel(posc_ref, posn_ref, x2_ref, route_ref, g_ref, ys_hbm, o_ref, ybuf, sem):
    i = pl.program_id(0)
    n = pl.num_programs(0)
    tm = x2_ref.shape[0]
    slot = i % 2

    def issue(pos_ref, s):
        for r in range(tm):
            for k in range(TOP_K):
                pltpu.make_async_copy(ys_hbm.at[pl.ds(pos_ref[0, 0, k * tm + r], 1)],
                                      ybuf.at[s, k, pl.ds(r, 1)], sem.at[s]).start(priority=1)

    def wait(s):
        for k in range(TOP_K):
            pltpu.make_async_copy(ys_hbm.at[pl.ds(0, tm)], ybuf.at[s, k], sem.at[s]).wait()

    @pl.when(i == 0)
    def _():
        issue(posc_ref, 0)

    wait(slot)

    for s in range(2):
        @pl.when(slot == s)
        def _():
            issue(posn_ref, 1 - s)

    r = route_ref[...]
    g1 = r[:, ROUTE_G1:ROUTE_G1 + 1]
    g2 = r[:, ROUTE_G2:ROUTE_G2 + 1]
    xo = x2_ref[...] + g1 * _unpack_halves(ybuf[slot, 0]) + g2 * _unpack_halves(ybuf[slot, 1])
    o_ref[...] = _rms(xo, g_ref[...])

    @pl.when(i == n - 1)
    def _():
        wait(1 - slot)


def _final(x2, ys, pos, route, g):
    T = x2.shape[0]
    nblk = pos.shape[0]
    tm = T // nblk
    row = lambda w: pl.BlockSpec((tm, w), lambda i: (i, 0))
    return pl.pallas_call(
        _final_kernel,
        grid=(nblk,),
        in_specs=[pl.BlockSpec((1, 1, TOP_K * tm), lambda i: (i, 0, 0), memory_space=pltpu.SMEM),
                  pl.BlockSpec((1, 1, TOP_K * tm), lambda i: (jnp.minimum(i + 1, nblk - 1), 0, 0),
                               memory_space=pltpu.SMEM),
                  row(D_MODEL), row(LANES), pl.BlockSpec(g.shape, lambda i: (0, 0)),
                  pl.BlockSpec(memory_space=pl.ANY)],
        out_specs=row(D_MODEL),
        out_shape=jax.ShapeDtypeStruct((T, D_MODEL), F32),
        scratch_shapes=[pltpu.VMEM((2, TOP_K, tm, HALF), U32), pltpu.SemaphoreType.DMA((2,))],
        compiler_params=pltpu.CompilerParams(dimension_semantics=("arbitrary",), vmem_limit_bytes=VMEM_LIMIT),
        name="final",
    )(pos, pos, x2, route, g, ys)


def _band_bias(table):
    assert WINDOW == BLOCK
    i = jnp.arange(BLOCK)[:, None]
    j = jnp.arange(2 * BLOCK)[None, :]
    n = jnp.maximum(i + BLOCK - j, 0)
    nf = jnp.maximum(n, 1).astype(F32)
    large = MAX_EXACT + (jnp.log(nf / MAX_EXACT) / math.log(MAX_DISTANCE / MAX_EXACT)
                         * (NUM_BUCKETS - MAX_EXACT)).astype(I32)
    large = jnp.minimum(large, NUM_BUCKETS - 1)
    bucket = jnp.where(n < MAX_EXACT, n, large)
    onehot = (bucket[:, :, None] == jnp.arange(NUM_BUCKETS)[None, None, :]).astype(F32)
    bias = jnp.einsum("ijb,bh->hij", onehot, table.astype(F32), precision=lax.Precision.HIGHEST)
    from_prev = (jnp.arange(BLOCK)[None, :] > i)[None]
    prev, cur = bias[:, :, :BLOCK], bias[:, :, BLOCK:]
    return jnp.stack([jnp.where(from_prev, NEG_INF, cur), jnp.where(from_prev, prev, cur)])


def _dispatch_plan(route_rows, counts_f, tmx, n_tiles_max, tm_rows):
    T = route_rows.shape[1]
    experts = jnp.arange(N_EXPERTS, dtype=I32)
    counts = counts_f[:N_EXPERTS, 0].astype(I32)
    ptiles = (counts + tmx - 1) // tmx
    tile_end = jnp.cumsum(ptiles)
    nt = tile_end[-1]
    row_off = (tile_end - ptiles) * tmx

    def slot(e_row, r_row):
        e = route_rows[e_row].astype(I32)
        off = jnp.sum(jnp.where(e[None, :] == experts[:, None], row_off[:, None], 0), axis=0)
        return (off + route_rows[r_row].astype(I32)).reshape(T // tm_rows, 1, tm_rows)

    pos = jnp.concatenate([slot(ROUTE_E1, ROUTE_R1), slot(ROUTE_E2, ROUTE_R2)], axis=2)

    tile_ids = jnp.arange(n_tiles_max, dtype=I32)
    expert_of = lambda t: jnp.sum((tile_end[None, :] <= t[:, None]).astype(I32), axis=1)
    te = expert_of(jnp.minimum(tile_ids, nt - 1))
    partial = jnp.any((tile_ids[:, None] == (tile_end - 1)[None, :]) & (counts % tmx != 0)[None, :], axis=1)
    zfill = (partial | (tile_ids >= nt)).astype(I32)

    used = ptiles > 0
    run_first = (jnp.any((tile_ids[:, None] == (tile_end - ptiles)[None, :]) & used[None, :], axis=1)
                 & (tile_ids < nt)).astype(I32)
    run_slot = (jnp.cumsum(run_first) - 1) % 2
    later_used = used[None, :] & (experts[None, :] > experts[:, None])
    next_of = jnp.min(jnp.where(later_used, experts[None, :], N_EXPERTS), axis=1)
    next_of = jnp.where(next_of < N_EXPERTS, next_of, -1)
    run_next = jnp.sum(jnp.where(te[:, None] == experts[None, :], next_of[None, :], 0), axis=1)
    return pos, te, nt.reshape(1), zfill, run_first, run_slot.astype(I32), run_next.astype(I32)


def kernel(x, mem, rel_bias_table, norm_mix, w_in, attn_sinks, conv_w, conv_b, gate_bias_i, gate_bias_f, mlstm_norm, w_out, norm_cross, norm_mem, w_cq, w_ck, w_cv, w_co, norm_moe, w_router_group, b_router_group, w_router_expert, b_router_expert, w_exp_gate, w_exp_up, w_exp_down, norm_final):
    B, S, _ = x.shape
    T = B * S
    depth = w_in.shape[0]
    x2d = x.reshape(T, D_MODEL)
    mem2d = mem.reshape(B * N_MEM, D_MODEL)
    bias = _band_bias(rel_bias_table)

    tmx = min(TM_EXPERT, T)
    n_tiles_max = (T * TOP_K) // tmx + N_EXPERTS
    tm_rows = min(TM_ROWDMA, T)

    assert depth == 1, "the final combine is fused with the final norm: single layer only"
    l = 0
    w_pad = jnp.pad(w_in[l], ((0, 0), (0, C_GATE + LANES - D_IN))).astype(BF16)
    gb = jnp.concatenate([gate_bias_i[l], gate_bias_f[l]]).astype(F32)
    gbias_col = jnp.pad(gb, (0, LANES - GATE_ROWS))[None, :]
    qa, kva, qkb, vb, ob, gc, gr = _inproj(x2d, norm_mix[l][None, :], w_pad, gbias_col, B, S)

    per_seq = lambda a: a.reshape(B, S, a.shape[-1])
    mix = _seqmix(per_seq(qkb), per_seq(vb), per_seq(ob), per_seq(gc), gr, per_seq(qa), per_seq(kva),
                  conv_w[l][:, 0, :].astype(F32), conv_b[l][None, :].astype(F32),
                  mlstm_norm[l][None, :].astype(F32), bias, attn_sinks[l].astype(F32)).reshape(T, W_A_Q + W_B)

    ck, cv = _memkv(mem2d, norm_mem[l][None, :], w_ck[l].astype(BF16), w_cv[l].astype(BF16), B)

    wr = jnp.pad(jnp.concatenate([w_router_expert[l], w_router_group[l]], axis=1),
                 ((0, 0), (0, LANES - N_EXPERTS - N_GROUPS))).astype(BF16)
    br = jnp.pad(jnp.concatenate([b_router_expert[l], b_router_group[l]]),
                 (0, LANES - N_EXPERTS - N_GROUPS)).astype(F32)[None, :]
    x2, hz_packed, route, route_rows, counts = _mid(
        x2d, mix, w_out[l].astype(BF16), norm_cross[l][None, :], w_cq[l].astype(BF16), ck, cv,
        w_co[l].astype(BF16), norm_moe[l][None, :], wr, br, B, S)

    pos, te, nt, zfill, run_first, run_slot, run_next = _dispatch_plan(route_rows, counts, tmx, n_tiles_max,
                                                                       tm_rows)
    xs = _dispatch(hz_packed, pos, zfill, n_tiles_max * tmx, tmx)
    ys = _experts(xs, w_exp_gate[l], w_exp_up[l], w_exp_down[l], te, nt, run_first, run_slot, run_next, tmx)
    out = _final(x2, ys, pos, route, norm_final[None, :])
    return out.reshape(B, S, D_MODEL)
```

```python
import math

import jax
import jax.numpy as jnp
from jax import lax
from jax.experimental import pallas as pl
from jax.experimental.pallas import tpu as pltpu

F32 = jnp.float32
BF16 = jnp.bfloat16
U32 = jnp.uint32
I32 = jnp.int32

D_MODEL = 1024
N_MEM = 256
N_HEADS_A = 8
N_KV_A = 2
HEAD_DIM_A = 64
BLOCK = 128
WINDOW = 128
NUM_BUCKETS = 32
MAX_EXACT = NUM_BUCKETS // 2
MAX_DISTANCE = 128
N_HEADS_B = 4
HEAD_DIM_B = 128
CHUNK = 128
CONV_WIDTH = 4
N_HEADS_X = 4
HEAD_DIM_X = D_MODEL // N_HEADS_X
N_GROUPS = 4
EXPERTS_PER_GROUP = 8
N_EXPERTS = N_GROUPS * EXPERTS_PER_GROUP
TOP_K = 2
D_EXPERT = 512
EPS = 1e-6
NEG_INF = -1e30

W_A_Q = N_HEADS_A * HEAD_DIM_A
W_A_KV = N_KV_A * HEAD_DIM_A
W_B = N_HEADS_B * HEAD_DIM_B
C_QA = 0
C_KVA = C_QA + W_A_Q
C_QKB = C_KVA + 2 * W_A_KV
C_VB = C_QKB + 2 * W_B
C_OB = C_VB + W_B
C_GATE = C_OB + W_B
D_IN = C_GATE + 2 * N_HEADS_B

LANES = 128
SUBLANES = 8
GATE_ROWS = 8
HALF = D_MODEL // 2

TM_INPROJ = 1024
TM_MID = 1024
TM_ROWDMA = 512
TM_EXPERT = 512

V7X_VMEM_BYTES = 64 * 1024 * 1024
VMEM_LIMIT = V7X_VMEM_BYTES * 3 // 4


def _rms(xf, g):
    return xf * lax.rsqrt(jnp.mean(xf * xf, axis=-1, keepdims=True) + EPS) * g


def _pack_halves(v):
    b = pltpu.bitcast(v.astype(BF16).astype(F32), U32)
    return (b[:, :HALF] >> 16) | b[:, HALF:]


def _unpack_halves(p):
    lo = pltpu.bitcast(p << 16, F32)
    hi = pltpu.bitcast(p & jnp.uint32(0xFFFF0000), F32)
    return jnp.concatenate([lo, hi], axis=-1)


def _log_sigmoid(z):
    return jnp.minimum(z, 0.0) - jnp.log1p(jnp.exp(-jnp.abs(z)))


def _split3(v):
    hi = v.astype(BF16).astype(F32)
    rest = v - hi
    mid = rest.astype(BF16).astype(F32)
    return hi, mid, (rest - mid).astype(BF16).astype(F32)


def _inproj_kernel(x_ref, g_ref, w_ref, gbc_ref, qa_ref, kva_ref, qkb_ref, vb_ref, ob_ref, gc_ref, gr_ref):
    tm = x_ref.shape[0]
    h = _rms(x_ref[...], g_ref[...]).astype(BF16)

    def mm(lo, hi):
        return jnp.dot(h, w_ref[:, lo:hi], preferred_element_type=F32)

    qa_ref[...] = mm(C_QA, C_KVA).astype(BF16)
    kva_ref[...] = mm(C_KVA, C_QKB).astype(BF16)
    qkb_ref[...] = mm(C_QKB, C_VB).astype(BF16)
    vb_ref[...] = mm(C_VB, C_OB).astype(BF16)
    ob_ref[...] = mm(C_OB, C_GATE).astype(BF16)

    H, L = N_HEADS_B, CHUNK
    gcol = mm(C_GATE, C_GATE + LANES) + gbc_ref[...]
    grow = jnp.transpose(gcol)[0:GATE_ROWS, :]
    lane_c = lax.broadcasted_iota(I32, (L, LANES), 1)
    is_f_col = (lane_c >= H) & (lane_c < 2 * H)
    is_f_row = lax.broadcasted_iota(I32, (GATE_ROWS, L), 0) >= H
    ti = lax.broadcasted_iota(I32, (L, L), 0)
    si = lax.broadcasted_iota(I32, (L, L), 1)
    tril = jnp.where(si <= ti, 1.0, 0.0).astype(BF16)
    triu = jnp.where(si >= ti, 1.0, 0.0).astype(BF16)
    for c in range(tm // L):
        rows = slice(c * L, (c + 1) * L)
        gcol_c = gcol[rows, :]
        fcol = jnp.where(is_f_col, _log_sigmoid(gcol_c), 0.0)
        parts = jnp.dot(tril, jnp.concatenate(_split3(fcol), axis=1).astype(BF16), preferred_element_type=F32)
        bcol = parts[:, :LANES] + parts[:, LANES:2 * LANES] + parts[:, 2 * LANES:]
        gc_ref[rows, :] = jnp.where(is_f_col, bcol, gcol_c)
        grow_c = grow[:, rows]
        frow = jnp.where(is_f_row, _log_sigmoid(grow_c), 0.0)
        parts = jnp.dot(jnp.concatenate(_split3(frow), axis=0).astype(BF16), triu, preferred_element_type=F32)
        brow = parts[:GATE_ROWS] + parts[GATE_ROWS:2 * GATE_ROWS] + parts[2 * GATE_ROWS:]
        gr_ref[:, rows] = jnp.where(is_f_row, brow, grow_c)


def _inproj(x2d, g, w_pad, gbias_col, B, S):
    T = x2d.shape[0]
    tm = min(TM_INPROJ, S)
    tiles_per_seq = S // tm
    row = lambda w: pl.BlockSpec((tm, w), lambda i: (i, 0))
    full = lambda a: pl.BlockSpec(a.shape, lambda i: (0,) * a.ndim)
    return pl.pallas_call(
        _inproj_kernel,
        grid=(T // tm,),
        in_specs=[row(D_MODEL), full(g), full(w_pad), full(gbias_col)],
        out_specs=[row(W_A_Q), row(2 * W_A_KV), row(2 * W_B), row(W_B), row(W_B), row(LANES),
                   pl.BlockSpec((None, GATE_ROWS, tm), lambda i: (i // tiles_per_seq, 0, i % tiles_per_seq))],
        out_shape=[jax.ShapeDtypeStruct((T, W_A_Q), BF16),
                   jax.ShapeDtypeStruct((T, 2 * W_A_KV), BF16),
                   jax.ShapeDtypeStruct((T, 2 * W_B), BF16),
                   jax.ShapeDtypeStruct((T, W_B), BF16),
                   jax.ShapeDtypeStruct((T, W_B), BF16),
                   jax.ShapeDtypeStruct((T, LANES), F32),
                   jax.ShapeDtypeStruct((B, GATE_ROWS, S), F32)],
        compiler_params=pltpu.CompilerParams(dimension_semantics=("parallel",), vmem_limit_bytes=VMEM_LIMIT),
        name="inproj",
    )(x2d, g, w_pad, gbias_col)


def _swa_block(q, kvc, kvp, bias_ref, sink_ref):
    kvp = kvp.astype(F32)
    kvc = kvc.astype(F32)
    kband = jnp.concatenate([kvp[:, :W_A_KV], kvc[:, :W_A_KV]], axis=0)
    vband = jnp.concatenate([kvp[:, W_A_KV:], kvc[:, W_A_KV:]], axis=0)
    lane = lax.broadcasted_iota(I32, (2 * BLOCK, LANES), 1)
    lo = lane < HEAD_DIM_A

    def placements(band):
        swapped = pltpu.roll(band, HEAD_DIM_A, axis=1)
        z = jnp.zeros_like(band)
        return {(0, 0): jnp.where(lo, band, z).astype(BF16), (0, 1): jnp.where(lo, z, swapped).astype(BF16),
                (1, 0): jnp.where(lo, swapped, z).astype(BF16), (1, 1): jnp.where(lo, z, band).astype(BF16)}

    kpl = placements(kband)
    vpl = placements(vband)

    scale = HEAD_DIM_A ** -0.5
    group = N_HEADS_A // N_KV_A
    tiles = []
    from_prev = (lax.broadcasted_iota(I32, (BLOCK, BLOCK), 1) > lax.broadcasted_iota(I32, (BLOCK, BLOCK), 0))

    for pair in range(N_HEADS_A // 2):
        qt = q[:, pair * LANES:(pair + 1) * LANES]
        acc = None
        for half in range(2):
            h = 2 * pair + half
            g = h // group
            s2 = lax.dot_general(qt, kpl[(g, half)], (((1,), (1,)), ((), ())), preferred_element_type=F32)
            s = jnp.where(from_prev, s2[:, :BLOCK], s2[:, BLOCK:])
            s = s * scale + bias_ref[h]
            sink = sink_ref[h]
            m = jnp.maximum(jnp.max(s, axis=-1, keepdims=True), sink)
            p = jnp.exp(s - m)
            denom = jnp.sum(p, axis=-1, keepdims=True) + jnp.exp(sink - m)
            p2 = jnp.concatenate([jnp.where(from_prev, p, 0.0), jnp.where(from_prev, 0.0, p)], axis=1)
            o = jnp.dot(p2.astype(BF16), vpl[(g, half)], preferred_element_type=F32) / denom
            acc = o if acc is None else acc + o
        tiles.append(acc.astype(BF16))
    return tiles


CONV_HALO = 16


def _seqmix_kernel(sink_ref, qkc_ref, qkp_ref, vb_ref, ob_ref, gc_ref, gr_ref, qa_ref, kvc_ref, kvp_ref,
                   cw_ref, cb_ref, nrm_ref, bias_ref, o_ref, state_ref, m_ref):
    c = pl.program_id(0)
    B = qkc_ref.shape[0]
    H, D, L = N_HEADS_B, HEAD_DIM_B, CHUNK

    @pl.when(c == 0)
    def _():
        state_ref[...] = jnp.zeros_like(state_ref)
        m_ref[...] = jnp.zeros_like(m_ref)

    rr = lax.broadcasted_iota(I32, (L, CONV_HALO + L), 0)
    cc = lax.broadcasted_iota(I32, (L, CONV_HALO + L), 1)
    shifts = {delay: jnp.where(cc == rr + (CONV_HALO - delay), 1.0, 0.0).astype(BF16)
              for delay in range(1, CONV_WIDTH)}
    ti = lax.broadcasted_iota(I32, (L, L), 0)
    si = lax.broadcasted_iota(I32, (L, L), 1)
    tri = si <= ti
    ones_blk = jnp.ones((L, D), BF16)

    def conv_silu(b):
        prev = qkp_ref[b]
        prev = jnp.where(c > 0, prev, jnp.zeros_like(prev))
        cur = qkc_ref[b]
        ext = jnp.concatenate([prev, cur], axis=0)
        y = cb_ref[...] + cw_ref[CONV_WIDTH - 1:CONV_WIDTH, :] * cur.astype(F32)
        for delay in range(1, CONV_WIDTH):
            tap = CONV_WIDTH - 1 - delay
            y = y + cw_ref[tap:tap + 1, :] * jnp.dot(shifts[delay], ext, preferred_element_type=F32)
        return y * jax.nn.sigmoid(y)

    states = [[state_ref[b, h] for h in range(H)] for b in range(B)]
    m_alls = [m_ref[b] for b in range(B)]
    new_states, new_m, outs = {}, {}, {}

    for b, h in [(b, h) for b in range(B) for h in range(H)]:
        if h == 0:
            for pair, tile in enumerate(_swa_block(qa_ref[b], kvc_ref[b], kvp_ref[b], bias_ref, sink_ref)):
                o_ref[b, :, pair * LANES:(pair + 1) * LANES] = tile
            qk = conv_silu(b)
            gcol = gc_ref[b]
            grow = gr_ref[b]
        qh = (qk[:, h * D:(h + 1) * D] * (D ** -0.5)).astype(BF16)
        k_t = qk[:, W_B + h * D:W_B + (h + 1) * D].T
        v1 = jnp.concatenate([vb_ref[b, :, h * D:(h + 1) * D], ones_blk], axis=-1)
        b_r = grow[H + h:H + h + 1, :]
        g_r = grow[h:h + 1, :] - b_r
        b_c = gcol[:, H + h:H + h + 1]
        m_prev = m_alls[b][h:h + 1, 0:1]
        state = states[b][h]

        gmat = jnp.where(tri, g_r, NEG_INF)
        m_c = jnp.maximum(jnp.max(gmat, axis=-1, keepdims=True), m_prev)
        a_inter = jnp.exp(m_prev - m_c)
        sc = jnp.dot(qh, k_t.astype(BF16), preferred_element_type=F32) * jnp.exp(gmat - m_c)
        tot = (jnp.dot(sc.astype(BF16), v1, preferred_element_type=F32)
               + a_inter * jnp.dot(qh, state.astype(BF16), preferred_element_type=F32))
        num = tot[:, :D]
        den = tot[:, D:]
        hh = num / jnp.maximum(jnp.abs(den), jnp.exp(-(b_c + m_c)))

        b_last = b_r[:, L - 1:L]
        m_new = jnp.maximum(b_last + m_prev, b_last + jnp.max(g_r, axis=-1, keepdims=True))
        w_r = jnp.exp(g_r + (b_last - m_new))
        decay = jnp.exp(b_last + m_prev - m_new)
        upd = jnp.dot((k_t * w_r).astype(BF16), v1, preferred_element_type=F32)
        new_states[b, h] = decay * state + upd
        new_m[b, h] = jnp.broadcast_to(m_new, (1, LANES))

        og = jax.nn.sigmoid(ob_ref[b, :, h * D:(h + 1) * D].astype(F32))
        hb = og * hh
        hb = hb * lax.rsqrt(jnp.mean(hb * hb, axis=-1, keepdims=True) + EPS)
        outs[b, h] = (hb * nrm_ref[:, h * D:(h + 1) * D]).astype(BF16)

    for b, h in new_states:
        state_ref[b, h] = new_states[b, h]
        m_ref[b, h:h + 1, :] = new_m[b, h]
        o_ref[b, :, W_A_Q + h * D:W_A_Q + (h + 1) * D] = outs[b, h]


def _seqmix(qkb, vb, ob, gc, gr, qa, kva, conv_w, conv_b, nrm, bias, sinks):
    assert CHUNK == BLOCK
    B, S, _ = qkb.shape
    nc = S // CHUNK
    halo_per_chunk = CHUNK // CONV_HALO
    blk = lambda w: pl.BlockSpec((B, CHUNK, w), lambda c: (0, c, 0))
    full = lambda a: pl.BlockSpec(a.shape, lambda c: (0,) * a.ndim)
    return pl.pallas_call(
        _seqmix_kernel,
        grid=(nc,),
        in_specs=[pl.BlockSpec(memory_space=pltpu.SMEM),
                  blk(2 * W_B),
                  pl.BlockSpec((B, CONV_HALO, 2 * W_B), lambda c: (0, jnp.maximum(c * halo_per_chunk - 1, 0), 0)),
                  blk(W_B), blk(W_B), blk(LANES),
                  pl.BlockSpec((B, GATE_ROWS, CHUNK), lambda c: (0, 0, c)),
                  blk(W_A_Q), blk(2 * W_A_KV),
                  pl.BlockSpec((B, BLOCK, 2 * W_A_KV), lambda c: (0, jnp.maximum(c - 1, 0), 0)),
                  full(conv_w), full(conv_b), full(nrm),
                  pl.BlockSpec((None,) + bias.shape[1:], lambda c: (jnp.minimum(c, 1), 0, 0, 0))],
        out_specs=blk(W_A_Q + W_B),
        out_shape=jax.ShapeDtypeStruct((B, S, W_A_Q + W_B), BF16),
        scratch_shapes=[pltpu.VMEM((B, N_HEADS_B, HEAD_DIM_B, 2 * HEAD_DIM_B), F32),
                        pltpu.VMEM((B, GATE_ROWS, LANES), F32)],
        compiler_params=pltpu.CompilerParams(dimension_semantics=("arbitrary",), vmem_limit_bytes=VMEM_LIMIT),
        name="seqmix",
    )(sinks, qkb, qkb, vb, ob, gc, gr, qa, kva, kva, conv_w, conv_b, nrm, bias)


def _memkv_kernel(mem_ref, g_ref, wk_ref, wv_ref, k_ref, v_ref):
    hm = _rms(mem_ref[...], g_ref[...]).astype(BF16)
    k_ref[...] = jnp.dot(hm, wk_ref[...], preferred_element_type=F32).astype(BF16)
    v_ref[...] = jnp.dot(hm, wv_ref[...], preferred_element_type=F32).astype(BF16)


def _memkv(mem2d, g, wk, wv, B):
    full = lambda a: pl.BlockSpec(a.shape, lambda b: (0,) * a.ndim)
    blk = pl.BlockSpec((N_MEM, D_MODEL), lambda b: (b, 0))
    return pl.pallas_call(
        _memkv_kernel,
        grid=(B,),
        in_specs=[blk, full(g), full(wk), full(wv)],
        out_specs=[blk, blk],
        out_shape=[jax.ShapeDtypeStruct((B * N_MEM, D_MODEL), BF16)] * 2,
        compiler_params=pltpu.CompilerParams(dimension_semantics=("parallel",), vmem_limit_bytes=VMEM_LIMIT),
        name="memkv",
    )(mem2d, g, wk, wv)


ROUTE_E1, ROUTE_E2, ROUTE_G1, ROUTE_G2, ROUTE_R1, ROUTE_R2 = 0, 1, 2, 3, 4, 5
ROUTER_GROUP_COL = N_EXPERTS
ROUTER_ROWS = -(-(N_EXPERTS + N_GROUPS) // SUBLANES) * SUBLANES


def _mid_kernel(x_ref, mix_ref, wo_ref, gx_ref, wq_ref, ck_ref, cv_ref, wco_ref, gz_ref, wr_ref, br_ref,
                x2_ref, hz_ref, route_ref, rrows_ref, counts_ref, cnt_ref):
    @pl.when(pl.program_id(0) == 0)
    def _():
        cnt_ref[...] = jnp.zeros_like(cnt_ref)

    x1 = x_ref[...] + jnp.dot(mix_ref[...], wo_ref[...], preferred_element_type=F32)

    hc = _rms(x1, gx_ref[...]).astype(BF16)
    cq = jnp.dot(hc, wq_ref[...], preferred_element_type=F32).astype(BF16)
    scale = HEAD_DIM_X ** -0.5
    heads = []
    for h in range(N_HEADS_X):
        sl = slice(h * HEAD_DIM_X, (h + 1) * HEAD_DIM_X)
        s = lax.dot_general(cq[:, sl], ck_ref[:, sl], (((1,), (1,)), ((), ())), preferred_element_type=F32) * scale
        p = jnp.exp(s - jnp.max(s, axis=-1, keepdims=True))
        co = jnp.dot(p.astype(BF16), cv_ref[:, sl], preferred_element_type=F32) / jnp.sum(p, axis=-1, keepdims=True)
        heads.append(co.astype(BF16))
    x2 = x1 + jnp.dot(jnp.concatenate(heads, axis=-1), wco_ref[...], preferred_element_type=F32)
    x2_ref[...] = x2

    hz = _rms(x2, gz_ref[...])
    hz_ref[...] = _pack_halves(hz)
    lg = jnp.dot(hz.astype(BF16), wr_ref[...], preferred_element_type=F32) + br_ref[...]
    tm = lg.shape[0]
    lt = jnp.transpose(lg)[0:ROUTER_ROWS, :]
    row = lax.broadcasted_iota(I32, lt.shape, 0)
    big = jnp.int32(ROUTER_ROWS)
    is_g = (row >= ROUTER_GROUP_COL) & (row < ROUTER_GROUP_COL + N_GROUPS)
    gl = jnp.where(is_g, lt, NEG_INF)
    gmax = jnp.max(gl, axis=0, keepdims=True)
    gsum = jnp.sum(jnp.exp(gl - gmax), axis=0, keepdims=True)
    g_prob = 1.0 / gsum
    g_idx = jnp.min(jnp.where(gl == gmax, row - ROUTER_GROUP_COL, big), axis=0, keepdims=True)
    sel = (row < N_EXPERTS) & ((row // EXPERTS_PER_GROUP) == g_idx)
    el = jnp.where(sel, lt, NEG_INF)
    m1 = jnp.max(el, axis=0, keepdims=True)
    i1 = jnp.min(jnp.where(el == m1, row, big), axis=0, keepdims=True)
    el2 = jnp.where(row == i1, NEG_INF, el)
    m2 = jnp.max(el2, axis=0, keepdims=True)
    i2 = jnp.min(jnp.where(el2 == m2, row, big), axis=0, keepdims=True)
    z = jnp.sum(jnp.exp(el - m1), axis=0, keepdims=True)
    p1 = 1.0 / z
    p2 = jnp.exp(m2 - m1) / z
    g1 = g_prob * (p1 / (p1 + p2))
    g2 = g_prob * (p2 / (p1 + p2))

    used = jnp.where((row == i1) | (row == i2), 1.0, 0.0)
    t_from = lax.broadcasted_iota(I32, (tm, tm), 0)
    t_to = lax.broadcasted_iota(I32, (tm, tm), 1)
    earlier = jnp.where(t_from < t_to, 1.0, 0.0).astype(BF16)
    before = jnp.dot(used.astype(BF16), earlier, preferred_element_type=F32) + cnt_ref[:, 0:1]
    r1 = jnp.sum(jnp.where(row == i1, before, 0.0), axis=0, keepdims=True)
    r2 = jnp.sum(jnp.where(row == i2, before, 0.0), axis=0, keepdims=True)
    cnt_ref[...] = cnt_ref[...] + jnp.sum(used, axis=1, keepdims=True)
    counts_ref[...] = cnt_ref[...]

    rec_row = lax.broadcasted_iota(I32, (SUBLANES, tm), 0)
    rec = jnp.zeros((SUBLANES, tm), F32)
    for c, v in ((ROUTE_E1, i1.astype(F32)), (ROUTE_E2, i2.astype(F32)), (ROUTE_G1, g1), (ROUTE_G2, g2),
                 (ROUTE_R1, r1), (ROUTE_R2, r2)):
        rec = jnp.where(rec_row == c, v, rec)
    rrows_ref[...] = rec
    route_ref[...] = jnp.transpose(jnp.concatenate([rec, jnp.zeros((LANES - SUBLANES, tm), F32)], axis=0))


def _mid(x2d, mix, wo, gx, wq, ck, cv, wco, gz, wr, br, B, S):
    T = B * S
    tm = min(TM_MID, S)
    per_b = S // tm
    row = lambda w: pl.BlockSpec((tm, w), lambda i: (i, 0))
    full = lambda a: pl.BlockSpec(a.shape, lambda i: (0,) * a.ndim)
    kvspec = pl.BlockSpec((N_MEM, D_MODEL), lambda i: (i // per_b, 0))
    return pl.pallas_call(
        _mid_kernel,
        grid=(T // tm,),
        in_specs=[row(D_MODEL), row(W_A_Q + W_B), full(wo), full(gx), full(wq), kvspec, kvspec,
                  full(wco), full(gz), full(wr), full(br)],
        out_specs=[row(D_MODEL), row(HALF), row(LANES), pl.BlockSpec((SUBLANES, tm), lambda i: (0, i)),
                   pl.BlockSpec((ROUTER_ROWS, LANES), lambda i: (0, 0))],
        out_shape=[jax.ShapeDtypeStruct((T, D_MODEL), F32),
                   jax.ShapeDtypeStruct((T, HALF), U32),
                   jax.ShapeDtypeStruct((T, LANES), F32),
                   jax.ShapeDtypeStruct((SUBLANES, T), F32),
                   jax.ShapeDtypeStruct((ROUTER_ROWS, LANES), F32)],
        scratch_shapes=[pltpu.VMEM((ROUTER_ROWS, LANES), F32)],
        compiler_params=pltpu.CompilerParams(dimension_semantics=("arbitrary",), vmem_limit_bytes=VMEM_LIMIT),
        name="mid",
    )(x2d, mix, wo, gx, wq, ck, cv, wco, gz, wr, br)


def _dispatch_kernel(zf_ref, pos_ref, hz_ref, xs_hbm, zbuf, sem, zsem):
    i = pl.program_id(0)
    tm = hz_ref.shape[0]
    zrows = zbuf.shape[0]

    @pl.when(i == 0)
    def _():
        zbuf[...] = jnp.zeros_like(zbuf)

        def fill(t):
            return pltpu.make_async_copy(zbuf, xs_hbm.at[pl.ds(t * zrows, zrows)], zsem)

        def start(t, carry):
            @pl.when(zf_ref[t] != 0)
            def _():
                fill(t).start()
            return carry

        def wait(t, carry):
            @pl.when(zf_ref[t] != 0)
            def _():
                fill(t).wait()
            return carry

        lax.fori_loop(0, zf_ref.shape[0], start, 0)
        lax.fori_loop(0, zf_ref.shape[0], wait, 0)

    for r in range(tm):
        for k in range(TOP_K):
            pltpu.make_async_copy(hz_ref.at[pl.ds(r, 1)], xs_hbm.at[pl.ds(pos_ref[0, 0, k * tm + r], 1)],
                                  sem).start(priority=k % 2)
    for k in range(TOP_K):
        pltpu.make_async_copy(hz_ref, xs_hbm.at[pl.ds(0, tm)], sem).wait()


def _dispatch(hz_packed, pos, zfill, n_slots, tmx):
    T = hz_packed.shape[0]
    tm = pos.shape[2] // TOP_K
    grid_spec = pltpu.PrefetchScalarGridSpec(
        num_scalar_prefetch=1,
        grid=(T // tm,),
        in_specs=[pl.BlockSpec((1, 1, TOP_K * tm), lambda i, zf: (i, 0, 0), memory_space=pltpu.SMEM),
                  pl.BlockSpec((tm, HALF), lambda i, zf: (i, 0))],
        out_specs=pl.BlockSpec(memory_space=pl.ANY),
        scratch_shapes=[pltpu.VMEM((tmx, HALF), U32), pltpu.SemaphoreType.DMA(()), pltpu.SemaphoreType.DMA(())],
    )
    return pl.pallas_call(
        _dispatch_kernel,
        grid_spec=grid_spec,
        out_shape=jax.ShapeDtypeStruct((n_slots, HALF), U32),
        compiler_params=pltpu.CompilerParams(dimension_semantics=("arbitrary",), vmem_limit_bytes=VMEM_LIMIT),
        name="dispatch",
    )(zfill, pos, hz_packed)


def _expert_kernel(te_ref, nt_ref, first_ref, slot_ref, next_ref, xs_ref, wg_hbm, wu_hbm, wd_hbm, ys_ref,
                   wg32, wu32, wd32, wgb, wub, wdb, wsem):
    i = pl.program_id(0)
    nt = nt_ref[0]

    def fetch(e, s):
        return [pltpu.make_async_copy(src.at[e], dst.at[s], wsem.at[s])
                for src, dst in ((wg_hbm, wg32), (wu_hbm, wu32), (wd_hbm, wd32))]

    @pl.when(i < nt)
    def _():
        @pl.when(i == 0)
        def _():
            for cp in fetch(te_ref[0], 0):
                cp.start()

        @pl.when(first_ref[i] != 0)
        def _():
            s = slot_ref[i]
            for cp in fetch(te_ref[i], s):
                cp.wait()

            @pl.when(next_ref[i] >= 0)
            def _():
                for cp in fetch(next_ref[i], 1 - s):
                    cp.start()

            wgb[...] = wg32[s].astype(BF16)
            wub[...] = wu32[s].astype(BF16)
            wdb[...] = wd32[s].astype(BF16)

        x = _unpack_halves(xs_ref[...]).astype(BF16)
        hg = jnp.dot(x, wgb[...], preferred_element_type=F32)
        hu = jnp.dot(x, wub[...], preferred_element_type=F32)
        a = (hg * jax.nn.sigmoid(hg) * hu).astype(BF16)
        ys_ref[...] = _pack_halves(jnp.dot(a, wdb[...], preferred_element_type=F32))

    @pl.when(i >= nt)
    def _():
        ys_ref[...] = jnp.zeros_like(ys_ref)


def _experts(xs, w_gate, w_up, w_down, tile_expert, ntiles, run_first, run_slot, run_next, tmx):
    n_tiles_max = tile_expert.shape[0]
    hbm = pl.BlockSpec(memory_space=pl.ANY)
    grid_spec = pltpu.PrefetchScalarGridSpec(
        num_scalar_prefetch=5,
        grid=(n_tiles_max,),
        in_specs=[pl.BlockSpec((tmx, HALF), lambda i, te, nt, *_: (jnp.minimum(i, nt[0] - 1), 0)), hbm, hbm, hbm],
        out_specs=pl.BlockSpec((tmx, HALF), lambda i, *_: (i, 0)),
        scratch_shapes=[pltpu.VMEM((2, D_MODEL, D_EXPERT), F32),
                        pltpu.VMEM((2, D_MODEL, D_EXPERT), F32),
                        pltpu.VMEM((2, D_EXPERT, D_MODEL), F32),
                        pltpu.VMEM((D_MODEL, D_EXPERT), BF16),
                        pltpu.VMEM((D_MODEL, D_EXPERT), BF16),
                        pltpu.VMEM((D_EXPERT, D_MODEL), BF16),
                        pltpu.SemaphoreType.DMA((2,))],
    )
    return pl.pallas_call(
        _expert_kernel,
        grid_spec=grid_spec,
        out_shape=jax.ShapeDtypeStruct(xs.shape, U32),
        compiler_params=pltpu.CompilerParams(dimension_semantics=("arbitrary",), vmem_limit_bytes=VMEM_LIMIT),
        name="experts",
    )(tile_expert, ntiles, run_first, run_slot, run_next, xs, w_gate, w_up, w_down)


def _final_kernel(posc_ref, posn_ref, x2_ref, route_ref, g_ref, ys_hbm, o_ref, ybuf, sem):
    i = pl.program_id(0)
    n = pl.num_programs(0)
    tm = x2_ref.shape[0]
    slot = i % 2

    def issue(pos_ref, s):
        for r in range(tm):
            for k in range(TOP_K):
                pltpu.make_async_copy(ys_hbm.at[pl.ds(pos_ref[0, 0, k * tm + r], 1)],
                                      ybuf.at[s, k, pl.ds(r, 1)], sem.at[s]).start(priority=k % 2)

    def wait(s):
        for k in range(TOP_K):
            pltpu.make_async_copy(ys_hbm.at[pl.ds(0, tm)], ybuf.at[s, k], sem.at[s]).wait()

    @pl.when(i == 0)
    def _():
        issue(posc_ref, 0)

    wait(slot)

    for s in range(2):
        @pl.when(slot == s)
        def _():
            issue(posn_ref, 1 - s)

    r = route_ref[...]
    g1 = r[:, ROUTE_G1:ROUTE_G1 + 1]
    g2 = r[:, ROUTE_G2:ROUTE_G2 + 1]
    xo = x2_ref[...] + g1 * _unpack_halves(ybuf[slot, 0]) + g2 * _unpack_halves(ybuf[slot, 1])
    o_ref[...] = _rms(xo, g_ref[...])

    @pl.when(i == n - 1)
    def _():
        wait(1 - slot)


def _final(x2, ys, pos, route, g):
    T = x2.shape[0]
    nblk = pos.shape[0]
    tm = T // nblk
    row = lambda w: pl.BlockSpec((tm, w), lambda i: (i, 0))
    return pl.pallas_call(
        _final_kernel,
        grid=(nblk,),
        in_specs=[pl.BlockSpec((1, 1, TOP_K * tm), lambda i: (i, 0, 0), memory_space=pltpu.SMEM),
                  pl.BlockSpec((1, 1, TOP_K * tm), lambda i: (jnp.minimum(i + 1, nblk - 1), 0, 0),
                               memory_space=pltpu.SMEM),
                  row(D_MODEL), row(LANES), pl.BlockSpec(g.shape, lambda i: (0, 0)),
                  pl.BlockSpec(memory_space=pl.ANY)],
        out_specs=row(D_MODEL),
        out_shape=jax.ShapeDtypeStruct((T, D_MODEL), F32),
        scratch_shapes=[pltpu.VMEM((2, TOP_K, tm, HALF), U32), pltpu.SemaphoreType.DMA((2,))],
        compiler_params=pltpu.CompilerParams(dimension_semantics=("arbitrary",), vmem_limit_bytes=VMEM_LIMIT),
        name="final",
    )(pos, pos, x2, route, g, ys)


def _band_bias(table):
    assert WINDOW == BLOCK
    i = jnp.arange(BLOCK)[:, None]
    j = jnp.arange(2 * BLOCK)[None, :]
    n = jnp.maximum(i + BLOCK - j, 0)
    nf = jnp.maximum(n, 1).astype(F32)
    large = MAX_EXACT + (jnp.log(nf / MAX_EXACT) / math.log(MAX_DISTANCE / MAX_EXACT)
                         * (NUM_BUCKETS - MAX_EXACT)).astype(I32)
    large = jnp.minimum(large, NUM_BUCKETS - 1)
    bucket = jnp.where(n < MAX_EXACT, n, large)
    onehot = (bucket[:, :, None] == jnp.arange(NUM_BUCKETS)[None, None, :]).astype(F32)
    bias = jnp.einsum("ijb,bh->hij", onehot, table.astype(F32), precision=lax.Precision.HIGHEST)
    from_prev = (jnp.arange(BLOCK)[None, :] > i)[None]
    prev, cur = bias[:, :, :BLOCK], bias[:, :, BLOCK:]
    return jnp.stack([jnp.where(from_prev, NEG_INF, cur), jnp.where(from_prev, prev, cur)])


def _dispatch_plan(route_rows, counts_f, tmx, n_tiles_max, tm_rows):
    T = route_rows.shape[1]
    experts = jnp.arange(N_EXPERTS, dtype=I32)
    counts = counts_f[:N_EXPERTS, 0].astype(I32)
    ptiles = (counts + tmx - 1) // tmx
    tile_end = jnp.cumsum(ptiles)
    nt = tile_end[-1]
    row_off = (tile_end - ptiles) * tmx

    def slot(e_row, r_row):
        e = route_rows[e_row].astype(I32)
        off = jnp.sum(jnp.where(e[None, :] == experts[:, None], row_off[:, None], 0), axis=0)
        return (off + route_rows[r_row].astype(I32)).reshape(T // tm_rows, 1, tm_rows)

    pos = jnp.concatenate([slot(ROUTE_E1, ROUTE_R1), slot(ROUTE_E2, ROUTE_R2)], axis=2)

    tile_ids = jnp.arange(n_tiles_max, dtype=I32)
    expert_of = lambda t: jnp.sum((tile_end[None, :] <= t[:, None]).astype(I32), axis=1)
    te = expert_of(jnp.minimum(tile_ids, nt - 1))
    partial = jnp.any((tile_ids[:, None] == (tile_end - 1)[None, :]) & (counts % tmx != 0)[None, :], axis=1)
    zfill = (partial | (tile_ids >= nt)).astype(I32)

    used = ptiles > 0
    run_first = (jnp.any((tile_ids[:, None] == (tile_end - ptiles)[None, :]) & used[None, :], axis=1)
                 & (tile_ids < nt)).astype(I32)
    run_slot = (jnp.cumsum(run_first) - 1) % 2
    later_used = used[None, :] & (experts[None, :] > experts[:, None])
    next_of = jnp.min(jnp.where(later_used, experts[None, :], N_EXPERTS), axis=1)
    next_of = jnp.where(next_of < N_EXPERTS, next_of, -1)
    run_next = jnp.sum(jnp.where(te[:, None] == experts[None, :], next_of[None, :], 0), axis=1)
    return pos, te, nt.reshape(1), zfill, run_first, run_slot.astype(I32), run_next.astype(I32)


def kernel(x, mem, rel_bias_table, norm_mix, w_in, attn_sinks, conv_w, conv_b, gate_bias_i, gate_bias_f, mlstm_norm, w_out, norm_cross, norm_mem, w_cq, w_ck, w_cv, w_co, norm_moe, w_router_group, b_router_group, w_router_expert, b_router_expert, w_exp_gate, w_exp_up, w_exp_down, norm_final):
    B, S, _ = x.shape
    T = B * S
    depth = w_in.shape[0]
    x2d = x.reshape(T, D_MODEL)
    mem2d = mem.reshape(B * N_MEM, D_MODEL)
    bias = _band_bias(rel_bias_table)

    tmx = min(TM_EXPERT, T)
    n_tiles_max = (T * TOP_K) // tmx + N_EXPERTS
    tm_rows = min(TM_ROWDMA, T)

    assert depth == 1, "the final combine is fused with the final norm: single layer only"
    l = 0
    w_pad = jnp.pad(w_in[l], ((0, 0), (0, C_GATE + LANES - D_IN))).astype(BF16)
    gb = jnp.concatenate([gate_bias_i[l], gate_bias_f[l]]).astype(F32)
    gbias_col = jnp.pad(gb, (0, LANES - GATE_ROWS))[None, :]
    qa, kva, qkb, vb, ob, gc, gr = _inproj(x2d, norm_mix[l][None, :], w_pad, gbias_col, B, S)

    per_seq = lambda a: a.reshape(B, S, a.shape[-1])
    mix = _seqmix(per_seq(qkb), per_seq(vb), per_seq(ob), per_seq(gc), gr, per_seq(qa), per_seq(kva),
                  conv_w[l][:, 0, :].astype(F32), conv_b[l][None, :].astype(F32),
                  mlstm_norm[l][None, :].astype(F32), bias, attn_sinks[l].astype(F32)).reshape(T, W_A_Q + W_B)

    ck, cv = _memkv(mem2d, norm_mem[l][None, :], w_ck[l].astype(BF16), w_cv[l].astype(BF16), B)

    wr = jnp.pad(jnp.concatenate([w_router_expert[l], w_router_group[l]], axis=1),
                 ((0, 0), (0, LANES - N_EXPERTS - N_GROUPS))).astype(BF16)
    br = jnp.pad(jnp.concatenate([b_router_expert[l], b_router_group[l]]),
                 (0, LANES - N_EXPERTS - N_GROUPS)).astype(F32)[None, :]
    x2, hz_packed, route, route_rows, counts = _mid(
        x2d, mix, w_out[l].astype(BF16), norm_cross[l][None, :], w_cq[l].astype(BF16), ck, cv,
        w_co[l].astype(BF16), norm_moe[l][None, :], wr, br, B, S)

    pos, te, nt, zfill, run_first, run_slot, run_next = _dispatch_plan(route_rows, counts, tmx, n_tiles_max,
                                                                       tm_rows)
    xs = _dispatch(hz_packed, pos, zfill, n_tiles_max * tmx, tmx)
    ys = _experts(xs, w_exp_gate[l], w_exp_up[l], w_exp_down[l], te, nt, run_first, run_slot, run_next, tmx)
    out = _final(x2, ys, pos, route, norm_final[None, :])
    return out.reshape(B, S, D_MODEL)
```

```python
import math

import jax
import jax.numpy as jnp
from jax import lax
from jax.experimental import pallas as pl
from jax.experimental.pallas import tpu as pltpu

F32 = jnp.float32
BF16 = jnp.bfloat16
U32 = jnp.uint32
I32 = jnp.int32

D_MODEL = 1024
N_MEM = 256
N_HEADS_A = 8
N_KV_A = 2
HEAD_DIM_A = 64
BLOCK = 128
WINDOW = 128
NUM_BUCKETS = 32
MAX_EXACT = NUM_BUCKETS // 2
MAX_DISTANCE = 128
N_HEADS_B = 4
HEAD_DIM_B = 128
CHUNK = 128
CONV_WIDTH = 4
N_HEADS_X = 4
HEAD_DIM_X = D_MODEL // N_HEADS_X
N_GROUPS = 4
EXPERTS_PER_GROUP = 8
N_EXPERTS = N_GROUPS * EXPERTS_PER_GROUP
TOP_K = 2
D_EXPERT = 512
EPS = 1e-6
NEG_INF = -1e30

W_A_Q = N_HEADS_A * HEAD_DIM_A
W_A_KV = N_KV_A * HEAD_DIM_A
W_B = N_HEADS_B * HEAD_DIM_B
C_QA = 0
C_KVA = C_QA + W_A_Q
C_QKB = C_KVA + 2 * W_A_KV
C_VB = C_QKB + 2 * W_B
C_OB = C_VB + W_B
C_GATE = C_OB + W_B
D_IN = C_GATE + 2 * N_HEADS_B

LANES = 128
SUBLANES = 8
GATE_ROWS = 8
HALF = D_MODEL // 2

TM_INPROJ = 1024
TM_MID = 1024
TM_ROWDMA = 512
TM_EXPERT = 512

V7X_VMEM_BYTES = 64 * 1024 * 1024
VMEM_LIMIT = V7X_VMEM_BYTES * 3 // 4


def _rms(xf, g):
    return xf * lax.rsqrt(jnp.mean(xf * xf, axis=-1, keepdims=True) + EPS) * g


def _pack_halves(v):
    b = pltpu.bitcast(v.astype(BF16).astype(F32), U32)
    return (b[:, :HALF] >> 16) | b[:, HALF:]


def _unpack_halves(p):
    lo = pltpu.bitcast(p << 16, F32)
    hi = pltpu.bitcast(p & jnp.uint32(0xFFFF0000), F32)
    return jnp.concatenate([lo, hi], axis=-1)


def _log_sigmoid(z):
    return jnp.minimum(z, 0.0) - jnp.log1p(jnp.exp(-jnp.abs(z)))


def _split3(v):
    hi = v.astype(BF16).astype(F32)
    rest = v - hi
    mid = rest.astype(BF16).astype(F32)
    return hi, mid, (rest - mid).astype(BF16).astype(F32)


def _inproj_kernel(x_ref, g_ref, w_ref, gbc_ref, qa_ref, kva_ref, qkb_ref, vb_ref, ob_ref, gc_ref, gr_ref):
    tm = x_ref.shape[0]
    h = _rms(x_ref[...], g_ref[...]).astype(BF16)

    def mm(lo, hi):
        return jnp.dot(h, w_ref[:, lo:hi], preferred_element_type=F32)

    qa_ref[...] = mm(C_QA, C_KVA).astype(BF16)
    kva_ref[...] = mm(C_KVA, C_QKB).astype(BF16)
    qkb_ref[...] = mm(C_QKB, C_VB).astype(BF16)
    vb_ref[...] = mm(C_VB, C_OB).astype(BF16)
    ob_ref[...] = mm(C_OB, C_GATE).astype(BF16)

    H, L = N_HEADS_B, CHUNK
    gcol = mm(C_GATE, C_GATE + LANES) + gbc_ref[...]
    grow = jnp.transpose(gcol)[0:GATE_ROWS, :]
    lane_c = lax.broadcasted_iota(I32, (L, LANES), 1)
    is_f_col = (lane_c >= H) & (lane_c < 2 * H)
    is_f_row = lax.broadcasted_iota(I32, (GATE_ROWS, L), 0) >= H
    ti = lax.broadcasted_iota(I32, (L, L), 0)
    si = lax.broadcasted_iota(I32, (L, L), 1)
    tril = jnp.where(si <= ti, 1.0, 0.0).astype(BF16)
    triu = jnp.where(si >= ti, 1.0, 0.0).astype(BF16)
    for c in range(tm // L):
        rows = slice(c * L, (c + 1) * L)
        gcol_c = gcol[rows, :]
        fcol = jnp.where(is_f_col, _log_sigmoid(gcol_c), 0.0)
        parts = jnp.dot(tril, jnp.concatenate(_split3(fcol), axis=1).astype(BF16), preferred_element_type=F32)
        bcol = parts[:, :LANES] + parts[:, LANES:2 * LANES] + parts[:, 2 * LANES:]
        gc_ref[rows, :] = jnp.where(is_f_col, bcol, gcol_c)
        grow_c = grow[:, rows]
        frow = jnp.where(is_f_row, _log_sigmoid(grow_c), 0.0)
        parts = jnp.dot(jnp.concatenate(_split3(frow), axis=0).astype(BF16), triu, preferred_element_type=F32)
        brow = parts[:GATE_ROWS] + parts[GATE_ROWS:2 * GATE_ROWS] + parts[2 * GATE_ROWS:]
        gr_ref[:, rows] = jnp.where(is_f_row, brow, grow_c)


def _inproj(x2d, g, w_pad, gbias_col, B, S):
    T = x2d.shape[0]
    tm = min(TM_INPROJ, S)
    tiles_per_seq = S // tm
    row = lambda w: pl.BlockSpec((tm, w), lambda i: (i, 0))
    full = lambda a: pl.BlockSpec(a.shape, lambda i: (0,) * a.ndim)
    return pl.pallas_call(
        _inproj_kernel,
        grid=(T // tm,),
        in_specs=[row(D_MODEL), full(g), full(w_pad), full(gbias_col)],
        out_specs=[row(W_A_Q), row(2 * W_A_KV), row(2 * W_B), row(W_B), row(W_B), row(LANES),
                   pl.BlockSpec((None, GATE_ROWS, tm), lambda i: (i // tiles_per_seq, 0, i % tiles_per_seq))],
        out_shape=[jax.ShapeDtypeStruct((T, W_A_Q), BF16),
                   jax.ShapeDtypeStruct((T, 2 * W_A_KV), BF16),
                   jax.ShapeDtypeStruct((T, 2 * W_B), BF16),
                   jax.ShapeDtypeStruct((T, W_B), BF16),
                   jax.ShapeDtypeStruct((T, W_B), BF16),
                   jax.ShapeDtypeStruct((T, LANES), F32),
                   jax.ShapeDtypeStruct((B, GATE_ROWS, S), F32)],
        compiler_params=pltpu.CompilerParams(dimension_semantics=("parallel",), vmem_limit_bytes=VMEM_LIMIT),
        name="inproj",
    )(x2d, g, w_pad, gbias_col)


def _swa_block(q, kvc, kvp, bias_ref, sink_ref):
    kvp = kvp.astype(F32)
    kvc = kvc.astype(F32)
    kband = jnp.concatenate([kvp[:, :W_A_KV], kvc[:, :W_A_KV]], axis=0)
    vband = jnp.concatenate([kvp[:, W_A_KV:], kvc[:, W_A_KV:]], axis=0)
    lane = lax.broadcasted_iota(I32, (2 * BLOCK, LANES), 1)
    lo = lane < HEAD_DIM_A

    def placements(band):
        swapped = pltpu.roll(band, HEAD_DIM_A, axis=1)
        z = jnp.zeros_like(band)
        return {(0, 0): jnp.where(lo, band, z).astype(BF16), (0, 1): jnp.where(lo, z, swapped).astype(BF16),
                (1, 0): jnp.where(lo, swapped, z).astype(BF16), (1, 1): jnp.where(lo, z, band).astype(BF16)}

    kpl = placements(kband)
    vpl = placements(vband)

    scale = HEAD_DIM_A ** -0.5
    group = N_HEADS_A // N_KV_A
    tiles = []
    from_prev = (lax.broadcasted_iota(I32, (BLOCK, BLOCK), 1) > lax.broadcasted_iota(I32, (BLOCK, BLOCK), 0))

    for pair in range(N_HEADS_A // 2):
        qt = q[:, pair * LANES:(pair + 1) * LANES]
        acc = None
        for half in range(2):
            h = 2 * pair + half
            g = h // group
            s2 = lax.dot_general(qt, kpl[(g, half)], (((1,), (1,)), ((), ())), preferred_element_type=F32)
            s = jnp.where(from_prev, s2[:, :BLOCK], s2[:, BLOCK:])
            s = s * scale + bias_ref[h]
            sink = sink_ref[h]
            m = jnp.maximum(jnp.max(s, axis=-1, keepdims=True), sink)
            p = jnp.exp(s - m)
            denom = jnp.sum(p, axis=-1, keepdims=True) + jnp.exp(sink - m)
            p2 = jnp.concatenate([jnp.where(from_prev, p, 0.0), jnp.where(from_prev, 0.0, p)], axis=1)
            o = jnp.dot(p2.astype(BF16), vpl[(g, half)], preferred_element_type=F32) / denom
            acc = o if acc is None else acc + o
        tiles.append(acc.astype(BF16))
    return tiles


CONV_HALO = 16
SEQ_SUB = 2


def _seqmix_kernel(sink_ref, qkc_ref, qkp_ref, vb_ref, ob_ref, gc_ref, gr_ref, qa_ref, kvc_ref, kvp_ref,
                   cw_ref, cb_ref, nrm_ref, bias0_ref, bias_ref, o_ref, state_ref, m_ref):
    c = pl.program_id(0)
    B = qkc_ref.shape[0]
    H, D, L = N_HEADS_B, HEAD_DIM_B, CHUNK

    @pl.when(c == 0)
    def _():
        state_ref[...] = jnp.zeros_like(state_ref)
        m_ref[...] = jnp.zeros_like(m_ref)

    rr = lax.broadcasted_iota(I32, (L, CONV_HALO + L), 0)
    cc = lax.broadcasted_iota(I32, (L, CONV_HALO + L), 1)
    shifts = {delay: jnp.where(cc == rr + (CONV_HALO - delay), 1.0, 0.0).astype(BF16)
              for delay in range(1, CONV_WIDTH)}
    ti = lax.broadcasted_iota(I32, (L, L), 0)
    si = lax.broadcasted_iota(I32, (L, L), 1)
    tri = si <= ti
    ones_blk = jnp.ones((L, D), BF16)

    def conv_silu(b, u):
        if u == 0:
            prev = qkp_ref[b]
            prev = jnp.where(c > 0, prev, jnp.zeros_like(prev))
        else:
            prev = qkc_ref[b, u * L - CONV_HALO:u * L, :]
        cur = qkc_ref[b, u * L:(u + 1) * L, :]
        ext = jnp.concatenate([prev, cur], axis=0)
        y = cb_ref[...] + cw_ref[CONV_WIDTH - 1:CONV_WIDTH, :] * cur.astype(F32)
        for delay in range(1, CONV_WIDTH):
            tap = CONV_WIDTH - 1 - delay
            y = y + cw_ref[tap:tap + 1, :] * jnp.dot(shifts[delay], ext, preferred_element_type=F32)
        return y * jax.nn.sigmoid(y)

    states = {(b, h): state_ref[b, h] for b in range(B) for h in range(H)}
    ms = {(b, h): m_ref[b, h:h + 1, 0:1] for b in range(B) for h in range(H)}

    n_sub = qkc_ref.shape[1] // L
    for u, b, h in [(u, b, h) for u in range(n_sub) for b in range(B) for h in range(H)]:
        rows = slice(u * L, (u + 1) * L)
        if h == 0:
            kvp = kvp_ref[b] if u == 0 else kvc_ref[b, (u - 1) * L:u * L, :]
            tiles = _swa_block(qa_ref[b, rows, :], kvc_ref[b, rows, :], kvp, bias0_ref if u == 0 else bias_ref,
                               sink_ref)
            for pair, tile in enumerate(tiles):
                o_ref[b, rows, pair * LANES:(pair + 1) * LANES] = tile
            qk = conv_silu(b, u)
            gcol = gc_ref[b, rows, :]
            grow = gr_ref[b, :, rows]
        qh = (qk[:, h * D:(h + 1) * D] * (D ** -0.5)).astype(BF16)
        k_t = qk[:, W_B + h * D:W_B + (h + 1) * D].T
        v1 = jnp.concatenate([vb_ref[b, rows, h * D:(h + 1) * D], ones_blk], axis=-1)
        b_r = grow[H + h:H + h + 1, :]
        g_r = grow[h:h + 1, :] - b_r
        b_c = gcol[:, H + h:H + h + 1]
        m_prev = ms[b, h]
        state = states[b, h]

        gmat = jnp.where(tri, g_r, NEG_INF)
        m_c = jnp.maximum(jnp.max(gmat, axis=-1, keepdims=True), m_prev)
        a_inter = jnp.exp(m_prev - m_c)
        sc = jnp.dot(qh, k_t.astype(BF16), preferred_element_type=F32) * jnp.exp(gmat - m_c)
        tot = (jnp.dot(sc.astype(BF16), v1, preferred_element_type=F32)
               + a_inter * jnp.dot(qh, state.astype(BF16), preferred_element_type=F32))
        num = tot[:, :D]
        den = tot[:, D:]
        hh = num / jnp.maximum(jnp.abs(den), jnp.exp(-(b_c + m_c)))

        b_last = b_r[:, L - 1:L]
        m_new = jnp.maximum(b_last + m_prev, b_last + jnp.max(g_r, axis=-1, keepdims=True))
        w_r = jnp.exp(g_r + (b_last - m_new))
        decay = jnp.exp(b_last + m_prev - m_new)
        upd = jnp.dot((k_t * w_r).astype(BF16), v1, preferred_element_type=F32)
        states[b, h] = decay * state + upd
        ms[b, h] = m_new

        og = jax.nn.sigmoid(ob_ref[b, rows, h * D:(h + 1) * D].astype(F32))
        hb = og * hh
        hb = hb * lax.rsqrt(jnp.mean(hb * hb, axis=-1, keepdims=True) + EPS)
        o_ref[b, rows, W_A_Q + h * D:W_A_Q + (h + 1) * D] = (hb * nrm_ref[:, h * D:(h + 1) * D]).astype(BF16)

    for b, h in states:
        state_ref[b, h] = states[b, h]
        m_ref[b, h:h + 1, :] = jnp.broadcast_to(ms[b, h], (1, LANES))


def _seqmix(qkb, vb, ob, gc, gr, qa, kva, conv_w, conv_b, nrm, bias, sinks):
    assert CHUNK == BLOCK
    B, S, _ = qkb.shape
    rows = min(SEQ_SUB * CHUNK, S)
    halo_per_step = rows // CONV_HALO
    blk = lambda w: pl.BlockSpec((B, rows, w), lambda c: (0, c, 0))
    full = lambda a: pl.BlockSpec(a.shape, lambda c: (0,) * a.ndim)
    bias_variant = lambda pick: pl.BlockSpec((None,) + bias.shape[1:], lambda c: (pick(c), 0, 0, 0))
    return pl.pallas_call(
        _seqmix_kernel,
        grid=(S // rows,),
        in_specs=[pl.BlockSpec(memory_space=pltpu.SMEM),
                  blk(2 * W_B),
                  pl.BlockSpec((B, CONV_HALO, 2 * W_B), lambda c: (0, jnp.maximum(c * halo_per_step - 1, 0), 0)),
                  blk(W_B), blk(W_B), blk(LANES),
                  pl.BlockSpec((B, GATE_ROWS, rows), lambda c: (0, 0, c)),
                  blk(W_A_Q), blk(2 * W_A_KV),
                  pl.BlockSpec((B, BLOCK, 2 * W_A_KV), lambda c: (0, jnp.maximum(c * (rows // BLOCK) - 1, 0), 0)),
                  full(conv_w), full(conv_b), full(nrm),
                  bias_variant(lambda c: jnp.minimum(c, 1)), bias_variant(lambda c: 1)],
        out_specs=blk(W_A_Q + W_B),
        out_shape=jax.ShapeDtypeStruct((B, S, W_A_Q + W_B), BF16),
        scratch_shapes=[pltpu.VMEM((B, N_HEADS_B, HEAD_DIM_B, 2 * HEAD_DIM_B), F32),
                        pltpu.VMEM((B, GATE_ROWS, LANES), F32)],
        compiler_params=pltpu.CompilerParams(dimension_semantics=("arbitrary",), vmem_limit_bytes=VMEM_LIMIT),
        name="seqmix",
    )(sinks, qkb, qkb, vb, ob, gc, gr, qa, kva, kva, conv_w, conv_b, nrm, bias, bias)


def _memkv_kernel(mem_ref, g_ref, wk_ref, wv_ref, k_ref, v_ref):
    hm = _rms(mem_ref[...], g_ref[...]).astype(BF16)
    k_ref[...] = jnp.dot(hm, wk_ref[...], preferred_element_type=F32).astype(BF16)
    v_ref[...] = jnp.dot(hm, wv_ref[...], preferred_element_type=F32).astype(BF16)


def _memkv(mem2d, g, wk, wv, B):
    full = lambda a: pl.BlockSpec(a.shape, lambda b: (0,) * a.ndim)
    blk = pl.BlockSpec((N_MEM, D_MODEL), lambda b: (b, 0))
    return pl.pallas_call(
        _memkv_kernel,
        grid=(B,),
        in_specs=[blk, full(g), full(wk), full(wv)],
        out_specs=[blk, blk],
        out_shape=[jax.ShapeDtypeStruct((B * N_MEM, D_MODEL), BF16)] * 2,
        compiler_params=pltpu.CompilerParams(dimension_semantics=("parallel",), vmem_limit_bytes=VMEM_LIMIT),
        name="memkv",
    )(mem2d, g, wk, wv)


ROUTE_E1, ROUTE_E2, ROUTE_G1, ROUTE_G2, ROUTE_R1, ROUTE_R2 = 0, 1, 2, 3, 4, 5
ROUTER_GROUP_COL = N_EXPERTS
ROUTER_ROWS = -(-(N_EXPERTS + N_GROUPS) // SUBLANES) * SUBLANES


def _mid_kernel(x_ref, mix_ref, wo_ref, gx_ref, wq_ref, ck_ref, cv_ref, wco_ref, gz_ref, wr_ref, br_ref,
                x2_ref, hz_ref, route_ref, rrows_ref, counts_ref, cnt_ref):
    @pl.when(pl.program_id(0) == 0)
    def _():
        cnt_ref[...] = jnp.zeros_like(cnt_ref)

    x1 = x_ref[...] + jnp.dot(mix_ref[...], wo_ref[...], preferred_element_type=F32)

    hc = _rms(x1, gx_ref[...]).astype(BF16)
    cq = jnp.dot(hc, wq_ref[...], preferred_element_type=F32).astype(BF16)
    scale = HEAD_DIM_X ** -0.5
    heads = []
    for h in range(N_HEADS_X):
        sl = slice(h * HEAD_DIM_X, (h + 1) * HEAD_DIM_X)
        s = lax.dot_general(cq[:, sl], ck_ref[:, sl], (((1,), (1,)), ((), ())), preferred_element_type=F32) * scale
        p = jnp.exp(s - jnp.max(s, axis=-1, keepdims=True))
        co = jnp.dot(p.astype(BF16), cv_ref[:, sl], preferred_element_type=F32) / jnp.sum(p, axis=-1, keepdims=True)
        heads.append(co.astype(BF16))
    x2 = x1 + jnp.dot(jnp.concatenate(heads, axis=-1), wco_ref[...], preferred_element_type=F32)
    x2_ref[...] = x2

    hz = _rms(x2, gz_ref[...])
    hz_ref[...] = _pack_halves(hz)
    lg = jnp.dot(hz.astype(BF16), wr_ref[...], preferred_element_type=F32) + br_ref[...]
    tm = lg.shape[0]
    lt = jnp.transpose(lg)[0:ROUTER_ROWS, :]
    row = lax.broadcasted_iota(I32, lt.shape, 0)
    big = jnp.int32(ROUTER_ROWS)
    is_g = (row >= ROUTER_GROUP_COL) & (row < ROUTER_GROUP_COL + N_GROUPS)
    gl = jnp.where(is_g, lt, NEG_INF)
    gmax = jnp.max(gl, axis=0, keepdims=True)
    gsum = jnp.sum(jnp.exp(gl - gmax), axis=0, keepdims=True)
    g_prob = 1.0 / gsum
    g_idx = jnp.min(jnp.where(gl == gmax, row - ROUTER_GROUP_COL, big), axis=0, keepdims=True)
    sel = (row < N_EXPERTS) & ((row // EXPERTS_PER_GROUP) == g_idx)
    el = jnp.where(sel, lt, NEG_INF)
    m1 = jnp.max(el, axis=0, keepdims=True)
    i1 = jnp.min(jnp.where(el == m1, row, big), axis=0, keepdims=True)
    el2 = jnp.where(row == i1, NEG_INF, el)
    m2 = jnp.max(el2, axis=0, keepdims=True)
    i2 = jnp.min(jnp.where(el2 == m2, row, big), axis=0, keepdims=True)
    z = jnp.sum(jnp.exp(el - m1), axis=0, keepdims=True)
    p1 = 1.0 / z
    p2 = jnp.exp(m2 - m1) / z
    g1 = g_prob * (p1 / (p1 + p2))
    g2 = g_prob * (p2 / (p1 + p2))

    used = jnp.where((row == i1) | (row == i2), 1.0, 0.0)
    t_from = lax.broadcasted_iota(I32, (tm, tm), 0)
    t_to = lax.broadcasted_iota(I32, (tm, tm), 1)
    earlier = jnp.where(t_from < t_to, 1.0, 0.0).astype(BF16)
    before = jnp.dot(used.astype(BF16), earlier, preferred_element_type=F32) + cnt_ref[:, 0:1]
    r1 = jnp.sum(jnp.where(row == i1, before, 0.0), axis=0, keepdims=True)
    r2 = jnp.sum(jnp.where(row == i2, before, 0.0), axis=0, keepdims=True)
    cnt_ref[...] = cnt_ref[...] + jnp.sum(used, axis=1, keepdims=True)
    counts_ref[...] = cnt_ref[...]

    rec_row = lax.broadcasted_iota(I32, (SUBLANES, tm), 0)
    rec = jnp.zeros((SUBLANES, tm), F32)
    for c, v in ((ROUTE_E1, i1.astype(F32)), (ROUTE_E2, i2.astype(F32)), (ROUTE_G1, g1), (ROUTE_G2, g2),
                 (ROUTE_R1, r1), (ROUTE_R2, r2)):
        rec = jnp.where(rec_row == c, v, rec)
    rrows_ref[...] = rec
    route_ref[...] = jnp.transpose(jnp.concatenate([rec, jnp.zeros((LANES - SUBLANES, tm), F32)], axis=0))


def _mid(x2d, mix, wo, gx, wq, ck, cv, wco, gz, wr, br, B, S):
    T = B * S
    tm = min(TM_MID, S)
    per_b = S // tm
    row = lambda w: pl.BlockSpec((tm, w), lambda i: (i, 0))
    full = lambda a: pl.BlockSpec(a.shape, lambda i: (0,) * a.ndim)
    kvspec = pl.BlockSpec((N_MEM, D_MODEL), lambda i: (i // per_b, 0))
    return pl.pallas_call(
        _mid_kernel,
        grid=(T // tm,),
        in_specs=[row(D_MODEL), row(W_A_Q + W_B), full(wo), full(gx), full(wq), kvspec, kvspec,
                  full(wco), full(gz), full(wr), full(br)],
        out_specs=[row(D_MODEL), row(HALF), row(LANES), pl.BlockSpec((SUBLANES, tm), lambda i: (0, i)),
                   pl.BlockSpec((ROUTER_ROWS, LANES), lambda i: (0, 0))],
        out_shape=[jax.ShapeDtypeStruct((T, D_MODEL), F32),
                   jax.ShapeDtypeStruct((T, HALF), U32),
                   jax.ShapeDtypeStruct((T, LANES), F32),
                   jax.ShapeDtypeStruct((SUBLANES, T), F32),
                   jax.ShapeDtypeStruct((ROUTER_ROWS, LANES), F32)],
        scratch_shapes=[pltpu.VMEM((ROUTER_ROWS, LANES), F32)],
        compiler_params=pltpu.CompilerParams(dimension_semantics=("arbitrary",), vmem_limit_bytes=VMEM_LIMIT),
        name="mid",
    )(x2d, mix, wo, gx, wq, ck, cv, wco, gz, wr, br)


def _dispatch_kernel(zf_ref, pos_ref, hz_ref, xs_hbm, zbuf, sem, zsem):
    i = pl.program_id(0)
    tm = hz_ref.shape[0]
    zrows = zbuf.shape[0]

    @pl.when(i == 0)
    def _():
        zbuf[...] = jnp.zeros_like(zbuf)

        def fill(t):
            return pltpu.make_async_copy(zbuf, xs_hbm.at[pl.ds(t * zrows, zrows)], zsem)

        def start(t, carry):
            @pl.when(zf_ref[t] != 0)
            def _():
                fill(t).start()
            return carry

        def wait(t, carry):
            @pl.when(zf_ref[t] != 0)
            def _():
                fill(t).wait()
            return carry

        lax.fori_loop(0, zf_ref.shape[0], start, 0)
        lax.fori_loop(0, zf_ref.shape[0], wait, 0)

    for r in range(tm):
        for k in range(TOP_K):
            pltpu.make_async_copy(hz_ref.at[pl.ds(r, 1)], xs_hbm.at[pl.ds(pos_ref[0, 0, k * tm + r], 1)],
                                  sem).start(priority=k % 2)
    for k in range(TOP_K):
        pltpu.make_async_copy(hz_ref, xs_hbm.at[pl.ds(0, tm)], sem).wait()


def _dispatch(hz_packed, pos, zfill, n_slots, tmx):
    T = hz_packed.shape[0]
    tm = pos.shape[2] // TOP_K
    grid_spec = pltpu.PrefetchScalarGridSpec(
        num_scalar_prefetch=1,
        grid=(T // tm,),
        in_specs=[pl.BlockSpec((1, 1, TOP_K * tm), lambda i, zf: (i, 0, 0), memory_space=pltpu.SMEM),
                  pl.BlockSpec((tm, HALF), lambda i, zf: (i, 0))],
        out_specs=pl.BlockSpec(memory_space=pl.ANY),
        scratch_shapes=[pltpu.VMEM((tmx, HALF), U32), pltpu.SemaphoreType.DMA(()), pltpu.SemaphoreType.DMA(())],
    )
    return pl.pallas_call(
        _dispatch_kernel,
        grid_spec=grid_spec,
        out_shape=jax.ShapeDtypeStruct((n_slots, HALF), U32),
        compiler_params=pltpu.CompilerParams(dimension_semantics=("arbitrary",), vmem_limit_bytes=VMEM_LIMIT),
        name="dispatch",
    )(zfill, pos, hz_packed)


def _expert_kernel(te_ref, nt_ref, first_ref, slot_ref, next_ref, xs_ref, wg_hbm, wu_hbm, wd_hbm, ys_ref,
                   wg32, wu32, wd32, wgb, wub, wdb, wsem):
    i = pl.program_id(0)
    nt = nt_ref[0]

    def fetch(e, s):
        return [pltpu.make_async_copy(src.at[e], dst.at[s], wsem.at[s])
                for src, dst in ((wg_hbm, wg32), (wu_hbm, wu32), (wd_hbm, wd32))]

    @pl.when(i < nt)
    def _():
        @pl.when(i == 0)
        def _():
            for cp in fetch(te_ref[0], 0):
                cp.start()

        @pl.when(first_ref[i] != 0)
        def _():
            s = slot_ref[i]
            for cp in fetch(te_ref[i], s):
                cp.wait()

            @pl.when(next_ref[i] >= 0)
            def _():
                for cp in fetch(next_ref[i], 1 - s):
                    cp.start()

            wgb[...] = wg32[s].astype(BF16)
            wub[...] = wu32[s].astype(BF16)
            wdb[...] = wd32[s].astype(BF16)

        x = _unpack_halves(xs_ref[...]).astype(BF16)
        hg = jnp.dot(x, wgb[...], preferred_element_type=F32)
        hu = jnp.dot(x, wub[...], preferred_element_type=F32)
        a = (hg * jax.nn.sigmoid(hg) * hu).astype(BF16)
        ys_ref[...] = _pack_halves(jnp.dot(a, wdb[...], preferred_element_type=F32))

    @pl.when(i >= nt)
    def _():
        ys_ref[...] = jnp.zeros_like(ys_ref)


def _experts(xs, w_gate, w_up, w_down, tile_expert, ntiles, run_first, run_slot, run_next, tmx):
    n_tiles_max = tile_expert.shape[0]
    hbm = pl.BlockSpec(memory_space=pl.ANY)
    grid_spec = pltpu.PrefetchScalarGridSpec(
        num_scalar_prefetch=5,
        grid=(n_tiles_max,),
        in_specs=[pl.BlockSpec((tmx, HALF), lambda i, te, nt, *_: (jnp.minimum(i, nt[0] - 1), 0)), hbm, hbm, hbm],
        out_specs=pl.BlockSpec((tmx, HALF), lambda i, *_: (i, 0)),
        scratch_shapes=[pltpu.VMEM((2, D_MODEL, D_EXPERT), F32),
                        pltpu.VMEM((2, D_MODEL, D_EXPERT), F32),
                        pltpu.VMEM((2, D_EXPERT, D_MODEL), F32),
                        pltpu.VMEM((D_MODEL, D_EXPERT), BF16),
                        pltpu.VMEM((D_MODEL, D_EXPERT), BF16),
                        pltpu.VMEM((D_EXPERT, D_MODEL), BF16),
                        pltpu.SemaphoreType.DMA((2,))],
    )
    return pl.pallas_call(
        _expert_kernel,
        grid_spec=grid_spec,
        out_shape=jax.ShapeDtypeStruct(xs.shape, U32),
        compiler_params=pltpu.CompilerParams(dimension_semantics=("arbitrary",), vmem_limit_bytes=VMEM_LIMIT),
        name="experts",
    )(tile_expert, ntiles, run_first, run_slot, run_next, xs, w_gate, w_up, w_down)


def _final_kernel(posc_ref, posn_ref, x2_ref, route_ref, g_ref, ys_hbm, o_ref, ybuf, sem):
    i = pl.program_id(0)
    n = pl.num_programs(0)
    tm = x2_ref.shape[0]
    slot = i % 2

    def issue(pos_ref, s):
        for r in range(tm):
            for k in range(TOP_K):
                pltpu.make_async_copy(ys_hbm.at[pl.ds(pos_ref[0, 0, k * tm + r], 1)],
                                      ybuf.at[s, k, pl.ds(r, 1)], sem.at[s]).start(priority=k % 2)

    def wait(s):
        for k in range(TOP_K):
            pltpu.make_async_copy(ys_hbm.at[pl.ds(0, tm)], ybuf.at[s, k], sem.at[s]).wait()

    @pl.when(i == 0)
    def _():
        issue(posc_ref, 0)

    wait(slot)

    for s in range(2):
        @pl.when(slot == s)
        def _():
            issue(posn_ref, 1 - s)

    r = route_ref[...]
    g1 = r[:, ROUTE_G1:ROUTE_G1 + 1]
    g2 = r[:, ROUTE_G2:ROUTE_G2 + 1]
    xo = x2_ref[...] + g1 * _unpack_halves(ybuf[slot, 0]) + g2 * _unpack_halves(ybuf[slot, 1])
    o_ref[...] = _rms(xo, g_ref[...])

    @pl.when(i == n - 1)
    def _():
        wait(1 - slot)


def _final(x2, ys, pos, route, g):
    T = x2.shape[0]
    nblk = pos.shape[0]
    tm = T // nblk
    row = lambda w: pl.BlockSpec((tm, w), lambda i: (i, 0))
    return pl.pallas_call(
        _final_kernel,
        grid=(nblk,),
        in_specs=[pl.BlockSpec((1, 1, TOP_K * tm), lambda i: (i, 0, 0), memory_space=pltpu.SMEM),
                  pl.BlockSpec((1, 1, TOP_K * tm), lambda i: (jnp.minimum(i + 1, nblk - 1), 0, 0),
                               memory_space=pltpu.SMEM),
                  row(D_MODEL), row(LANES), pl.BlockSpec(g.shape, lambda i: (0, 0)),
                  pl.BlockSpec(memory_space=pl.ANY)],
        out_specs=row(D_MODEL),
        out_shape=jax.ShapeDtypeStruct((T, D_MODEL), F32),
        scratch_shapes=[pltpu.VMEM((2, TOP_K, tm, HALF), U32), pltpu.SemaphoreType.DMA((2,))],
        compiler_params=pltpu.CompilerParams(dimension_semantics=("arbitrary",), vmem_limit_bytes=VMEM_LIMIT),
        name="final",
    )(pos, pos, x2, route, g, ys)


def _band_bias(table):
    assert WINDOW == BLOCK
    i = jnp.arange(BLOCK)[:, None]
    j = jnp.arange(2 * BLOCK)[None, :]
    n = jnp.maximum(i + BLOCK - j, 0)
    nf = jnp.maximum(n, 1).astype(F32)
    large = MAX_EXACT + (jnp.log(nf / MAX_EXACT) / math.log(MAX_DISTANCE / MAX_EXACT)
                         * (NUM_BUCKETS - MAX_EXACT)).astype(I32)
    large = jnp.minimum(large, NUM_BUCKETS - 1)
    bucket = jnp.where(n < MAX_EXACT, n, large)
    onehot = (bucket[:, :, None] == jnp.arange(NUM_BUCKETS)[None, None, :]).astype(F32)
    bias = jnp.einsum("ijb,bh->hij", onehot, table.astype(F32), precision=lax.Precision.HIGHEST)
    from_prev = (jnp.arange(BLOCK)[None, :] > i)[None]
    prev, cur = bias[:, :, :BLOCK], bias[:, :, BLOCK:]
    return jnp.stack([jnp.where(from_prev, NEG_INF, cur), jnp.where(from_prev, prev, cur)])


def _dispatch_plan(route_rows, counts_f, tmx, n_tiles_max, tm_rows):
    T = route_rows.shape[1]
    experts = jnp.arange(N_EXPERTS, dtype=I32)
    counts = counts_f[:N_EXPERTS, 0].astype(I32)
    ptiles = (counts + tmx - 1) // tmx
    tile_end = jnp.cumsum(ptiles)
    nt = tile_end[-1]
    row_off = (tile_end - ptiles) * tmx

    def slot(e_row, r_row):
        e = route_rows[e_row].astype(I32)
        off = jnp.sum(jnp.where(e[None, :] == experts[:, None], row_off[:, None], 0), axis=0)
        return (off + route_rows[r_row].astype(I32)).reshape(T // tm_rows, 1, tm_rows)

    pos = jnp.concatenate([slot(ROUTE_E1, ROUTE_R1), slot(ROUTE_E2, ROUTE_R2)], axis=2)

    tile_ids = jnp.arange(n_tiles_max, dtype=I32)
    expert_of = lambda t: jnp.sum((tile_end[None, :] <= t[:, None]).astype(I32), axis=1)
    te = expert_of(jnp.minimum(tile_ids, nt - 1))
    partial = jnp.any((tile_ids[:, None] == (tile_end - 1)[None, :]) & (counts % tmx != 0)[None, :], axis=1)
    zfill = (partial | (tile_ids >= nt)).astype(I32)

    used = ptiles > 0
    run_first = (jnp.any((tile_ids[:, None] == (tile_end - ptiles)[None, :]) & used[None, :], axis=1)
                 & (tile_ids < nt)).astype(I32)
    run_slot = (jnp.cumsum(run_first) - 1) % 2
    later_used = used[None, :] & (experts[None, :] > experts[:, None])
    next_of = jnp.min(jnp.where(later_used, experts[None, :], N_EXPERTS), axis=1)
    next_of = jnp.where(next_of < N_EXPERTS, next_of, -1)
    run_next = jnp.sum(jnp.where(te[:, None] == experts[None, :], next_of[None, :], 0), axis=1)
    return pos, te, nt.reshape(1), zfill, run_first, run_slot.astype(I32), run_next.astype(I32)


def kernel(x, mem, rel_bias_table, norm_mix, w_in, attn_sinks, conv_w, conv_b, gate_bias_i, gate_bias_f, mlstm_norm, w_out, norm_cross, norm_mem, w_cq, w_ck, w_cv, w_co, norm_moe, w_router_group, b_router_group, w_router_expert, b_router_expert, w_exp_gate, w_exp_up, w_exp_down, norm_final):
    B, S, _ = x.shape
    T = B * S
    depth = w_in.shape[0]
    x2d = x.reshape(T, D_MODEL)
    mem2d = mem.reshape(B * N_MEM, D_MODEL)
    bias = _band_bias(rel_bias_table)

    tmx = min(TM_EXPERT, T)
    n_tiles_max = (T * TOP_K) // tmx + N_EXPERTS
    tm_rows = min(TM_ROWDMA, T)

    assert depth == 1, "the final combine is fused with the final norm: single layer only"
    l = 0
    w_pad = jnp.pad(w_in[l], ((0, 0), (0, C_GATE + LANES - D_IN))).astype(BF16)
    gb = jnp.concatenate([gate_bias_i[l], gate_bias_f[l]]).astype(F32)
    gbias_col = jnp.pad(gb, (0, LANES - GATE_ROWS))[None, :]
    qa, kva, qkb, vb, ob, gc, gr = _inproj(x2d, norm_mix[l][None, :], w_pad, gbias_col, B, S)

    per_seq = lambda a: a.reshape(B, S, a.shape[-1])
    mix = _seqmix(per_seq(qkb), per_seq(vb), per_seq(ob), per_seq(gc), gr, per_seq(qa), per_seq(kva),
                  conv_w[l][:, 0, :].astype(F32), conv_b[l][None, :].astype(F32),
                  mlstm_norm[l][None, :].astype(F32), bias, attn_sinks[l].astype(F32)).reshape(T, W_A_Q + W_B)

    ck, cv = _memkv(mem2d, norm_mem[l][None, :], w_ck[l].astype(BF16), w_cv[l].astype(BF16), B)

    wr = jnp.pad(jnp.concatenate([w_router_expert[l], w_router_group[l]], axis=1),
                 ((0, 0), (0, LANES - N_EXPERTS - N_GROUPS))).astype(BF16)
    br = jnp.pad(jnp.concatenate([b_router_expert[l], b_router_group[l]]),
                 (0, LANES - N_EXPERTS - N_GROUPS)).astype(F32)[None, :]
    x2, hz_packed, route, route_rows, counts = _mid(
        x2d, mix, w_out[l].astype(BF16), norm_cross[l][None, :], w_cq[l].astype(BF16), ck, cv,
        w_co[l].astype(BF16), norm_moe[l][None, :], wr, br, B, S)

    pos, te, nt, zfill, run_first, run_slot, run_next = _dispatch_plan(route_rows, counts, tmx, n_tiles_max,
                                                                       tm_rows)
    xs = _dispatch(hz_packed, pos, zfill, n_tiles_max * tmx, tmx)
    ys = _experts(xs, w_exp_gate[l], w_exp_up[l], w_exp_down[l], te, nt, run_first, run_slot, run_next, tmx)
    out = _final(x2, ys, pos, route, norm_final[None, :])
    return out.reshape(B, S, D_MODEL)
```

```python
import math

import jax
import jax.numpy as jnp
from jax import lax
from jax.experimental import pallas as pl
from jax.experimental.pallas import tpu as pltpu

F32 = jnp.float32
BF16 = jnp.bfloat16
U32 = jnp.uint32
I32 = jnp.int32

D_MODEL = 1024
N_MEM = 256
N_HEADS_A = 8
N_KV_A = 2
HEAD_DIM_A = 64
BLOCK = 128
WINDOW = 128
NUM_BUCKETS = 32
MAX_EXACT = NUM_BUCKETS // 2
MAX_DISTANCE = 128
N_HEADS_B = 4
HEAD_DIM_B = 128
CHUNK = 128
CONV_WIDTH = 4
N_HEADS_X = 4
HEAD_DIM_X = D_MODEL // N_HEADS_X
N_GROUPS = 4
EXPERTS_PER_GROUP = 8
N_EXPERTS = N_GROUPS * EXPERTS_PER_GROUP
TOP_K = 2
D_EXPERT = 512
EPS = 1e-6
NEG_INF = -1e30

W_A_Q = N_HEADS_A * HEAD_DIM_A
W_A_KV = N_KV_A * HEAD_DIM_A
W_B = N_HEADS_B * HEAD_DIM_B
C_QA = 0
C_KVA = C_QA + W_A_Q
C_QKB = C_KVA + 2 * W_A_KV
C_VB = C_QKB + 2 * W_B
C_OB = C_VB + W_B
C_GATE = C_OB + W_B
D_IN = C_GATE + 2 * N_HEADS_B

LANES = 128
SUBLANES = 8
GATE_ROWS = 8
HALF = D_MODEL // 2

TM_INPROJ = 1024
TM_MID = 1024
TM_ROWDMA = 512
TM_EXPERT = 512

V7X_VMEM_BYTES = 64 * 1024 * 1024
VMEM_LIMIT = V7X_VMEM_BYTES * 3 // 4


def _rms(xf, g):
    return xf * lax.rsqrt(jnp.mean(xf * xf, axis=-1, keepdims=True) + EPS) * g


def _pack_halves(v):
    b = pltpu.bitcast(v.astype(BF16).astype(F32), U32)
    return (b[:, :HALF] >> 16) | b[:, HALF:]


def _unpack_halves(p):
    lo = pltpu.bitcast(p << 16, F32)
    hi = pltpu.bitcast(p & jnp.uint32(0xFFFF0000), F32)
    return jnp.concatenate([lo, hi], axis=-1)


def _log_sigmoid(z):
    return jnp.minimum(z, 0.0) - jnp.log1p(jnp.exp(-jnp.abs(z)))


def _split3(v):
    hi = v.astype(BF16).astype(F32)
    rest = v - hi
    mid = rest.astype(BF16).astype(F32)
    return hi, mid, (rest - mid).astype(BF16).astype(F32)


def _inproj_kernel(x_ref, g_ref, w_ref, gbc_ref, qa_ref, kva_ref, qkb_ref, vb_ref, ob_ref, gc_ref, gr_ref):
    tm = x_ref.shape[0]
    h = _rms(x_ref[...], g_ref[...]).astype(BF16)

    def mm(lo, hi):
        return jnp.dot(h, w_ref[:, lo:hi], preferred_element_type=F32)

    qa_ref[...] = mm(C_QA, C_KVA).astype(BF16)
    kva_ref[...] = mm(C_KVA, C_QKB).astype(BF16)
    qkb_ref[...] = mm(C_QKB, C_VB).astype(BF16)
    vb_ref[...] = mm(C_VB, C_OB).astype(BF16)
    ob_ref[...] = mm(C_OB, C_GATE).astype(BF16)

    H, L = N_HEADS_B, CHUNK
    gcol = mm(C_GATE, C_GATE + LANES) + gbc_ref[...]
    grow = jnp.transpose(gcol)[0:GATE_ROWS, :]
    lane_c = lax.broadcasted_iota(I32, (L, LANES), 1)
    is_f_col = (lane_c >= H) & (lane_c < 2 * H)
    is_f_row = lax.broadcasted_iota(I32, (GATE_ROWS, L), 0) >= H
    ti = lax.broadcasted_iota(I32, (L, L), 0)
    si = lax.broadcasted_iota(I32, (L, L), 1)
    tril = jnp.where(si <= ti, 1.0, 0.0).astype(BF16)
    triu = jnp.where(si >= ti, 1.0, 0.0).astype(BF16)
    for c in range(tm // L):
        rows = slice(c * L, (c + 1) * L)
        gcol_c = gcol[rows, :]
        fcol = jnp.where(is_f_col, _log_sigmoid(gcol_c), 0.0)
        parts = jnp.dot(tril, jnp.concatenate(_split3(fcol), axis=1).astype(BF16), preferred_element_type=F32)
        bcol = parts[:, :LANES] + parts[:, LANES:2 * LANES] + parts[:, 2 * LANES:]
        gc_ref[rows, :] = jnp.where(is_f_col, bcol, gcol_c)
        grow_c = grow[:, rows]
        frow = jnp.where(is_f_row, _log_sigmoid(grow_c), 0.0)
        parts = jnp.dot(jnp.concatenate(_split3(frow), axis=0).astype(BF16), triu, preferred_element_type=F32)
        brow = parts[:GATE_ROWS] + parts[GATE_ROWS:2 * GATE_ROWS] + parts[2 * GATE_ROWS:]
        gr_ref[:, rows] = jnp.where(is_f_row, brow, grow_c)


def _inproj(x2d, g, w_pad, gbias_col, B, S):
    T = x2d.shape[0]
    tm = min(TM_INPROJ, S)
    tiles_per_seq = S // tm
    row = lambda w: pl.BlockSpec((tm, w), lambda i: (i, 0))
    full = lambda a: pl.BlockSpec(a.shape, lambda i: (0,) * a.ndim)
    return pl.pallas_call(
        _inproj_kernel,
        grid=(T // tm,),
        in_specs=[row(D_MODEL), full(g), full(w_pad), full(gbias_col)],
        out_specs=[row(W_A_Q), row(2 * W_A_KV), row(2 * W_B), row(W_B), row(W_B), row(LANES),
                   pl.BlockSpec((None, GATE_ROWS, tm), lambda i: (i // tiles_per_seq, 0, i % tiles_per_seq))],
        out_shape=[jax.ShapeDtypeStruct((T, W_A_Q), BF16),
                   jax.ShapeDtypeStruct((T, 2 * W_A_KV), BF16),
                   jax.ShapeDtypeStruct((T, 2 * W_B), BF16),
                   jax.ShapeDtypeStruct((T, W_B), BF16),
                   jax.ShapeDtypeStruct((T, W_B), BF16),
                   jax.ShapeDtypeStruct((T, LANES), F32),
                   jax.ShapeDtypeStruct((B, GATE_ROWS, S), F32)],
        compiler_params=pltpu.CompilerParams(dimension_semantics=("parallel",), vmem_limit_bytes=VMEM_LIMIT),
        name="inproj",
    )(x2d, g, w_pad, gbias_col)


def _swa_block(q, kvc, kvp, bias_ref, sink_ref):
    kvp = kvp.astype(F32)
    kvc = kvc.astype(F32)
    kband = jnp.concatenate([kvp[:, :W_A_KV], kvc[:, :W_A_KV]], axis=0)
    vband = jnp.concatenate([kvp[:, W_A_KV:], kvc[:, W_A_KV:]], axis=0)
    lane = lax.broadcasted_iota(I32, (2 * BLOCK, LANES), 1)
    lo = lane < HEAD_DIM_A

    def placements(band):
        swapped = pltpu.roll(band, HEAD_DIM_A, axis=1)
        z = jnp.zeros_like(band)
        return {(0, 0): jnp.where(lo, band, z).astype(BF16), (0, 1): jnp.where(lo, z, swapped).astype(BF16),
                (1, 0): jnp.where(lo, swapped, z).astype(BF16), (1, 1): jnp.where(lo, z, band).astype(BF16)}

    kpl = placements(kband)
    vpl = placements(vband)

    scale = HEAD_DIM_A ** -0.5
    group = N_HEADS_A // N_KV_A
    tiles = []
    from_prev = (lax.broadcasted_iota(I32, (BLOCK, BLOCK), 1) > lax.broadcasted_iota(I32, (BLOCK, BLOCK), 0))

    for pair in range(N_HEADS_A // 2):
        qt = q[:, pair * LANES:(pair + 1) * LANES]
        acc = None
        for half in range(2):
            h = 2 * pair + half
            g = h // group
            s2 = lax.dot_general(qt, kpl[(g, half)], (((1,), (1,)), ((), ())), preferred_element_type=F32)
            s = jnp.where(from_prev, s2[:, :BLOCK], s2[:, BLOCK:])
            s = s * scale + bias_ref[h]
            sink = sink_ref[h]
            m = jnp.maximum(jnp.max(s, axis=-1, keepdims=True), sink)
            p = jnp.exp(s - m)
            denom = jnp.sum(p, axis=-1, keepdims=True) + jnp.exp(sink - m)
            p2 = jnp.concatenate([jnp.where(from_prev, p, 0.0), jnp.where(from_prev, 0.0, p)], axis=1)
            o = jnp.dot(p2.astype(BF16), vpl[(g, half)], preferred_element_type=F32) / denom
            acc = o if acc is None else acc + o
        tiles.append(acc.astype(BF16))
    return tiles


CONV_HALO = 16
SEQ_SUB = 4


def _seqmix_kernel(sink_ref, qkc_ref, qkp_ref, vb_ref, ob_ref, gc_ref, gr_ref, qa_ref, kvc_ref, kvp_ref,
                   cw_ref, cb_ref, nrm_ref, bias0_ref, bias_ref, o_ref, state_ref, m_ref):
    c = pl.program_id(0)
    B = qkc_ref.shape[0]
    H, D, L = N_HEADS_B, HEAD_DIM_B, CHUNK

    @pl.when(c == 0)
    def _():
        state_ref[...] = jnp.zeros_like(state_ref)
        m_ref[...] = jnp.zeros_like(m_ref)

    rr = lax.broadcasted_iota(I32, (L, CONV_HALO + L), 0)
    cc = lax.broadcasted_iota(I32, (L, CONV_HALO + L), 1)
    shifts = {delay: jnp.where(cc == rr + (CONV_HALO - delay), 1.0, 0.0).astype(BF16)
              for delay in range(1, CONV_WIDTH)}
    ti = lax.broadcasted_iota(I32, (L, L), 0)
    si = lax.broadcasted_iota(I32, (L, L), 1)
    tri = si <= ti
    ones_blk = jnp.ones((L, D), BF16)

    def conv_silu(b, u):
        if u == 0:
            prev = qkp_ref[b]
            prev = jnp.where(c > 0, prev, jnp.zeros_like(prev))
        else:
            prev = qkc_ref[b, u * L - CONV_HALO:u * L, :]
        cur = qkc_ref[b, u * L:(u + 1) * L, :]
        ext = jnp.concatenate([prev, cur], axis=0)
        y = cb_ref[...] + cw_ref[CONV_WIDTH - 1:CONV_WIDTH, :] * cur.astype(F32)
        for delay in range(1, CONV_WIDTH):
            tap = CONV_WIDTH - 1 - delay
            y = y + cw_ref[tap:tap + 1, :] * jnp.dot(shifts[delay], ext, preferred_element_type=F32)
        return y * jax.nn.sigmoid(y)

    states = {(b, h): state_ref[b, h] for b in range(B) for h in range(H)}
    ms = {(b, h): m_ref[b, h:h + 1, 0:1] for b in range(B) for h in range(H)}

    n_sub = qkc_ref.shape[1] // L
    for u, b, h in [(u, b, h) for u in range(n_sub) for b in range(B) for h in range(H)]:
        rows = slice(u * L, (u + 1) * L)
        if h == 0:
            kvp = kvp_ref[b] if u == 0 else kvc_ref[b, (u - 1) * L:u * L, :]
            tiles = _swa_block(qa_ref[b, rows, :], kvc_ref[b, rows, :], kvp, bias0_ref if u == 0 else bias_ref,
                               sink_ref)
            for pair, tile in enumerate(tiles):
                o_ref[b, rows, pair * LANES:(pair + 1) * LANES] = tile
            qk = conv_silu(b, u)
            gcol = gc_ref[b, rows, :]
            grow = gr_ref[b, :, rows]
        qh = (qk[:, h * D:(h + 1) * D] * (D ** -0.5)).astype(BF16)
        k_t = qk[:, W_B + h * D:W_B + (h + 1) * D].T
        v1 = jnp.concatenate([vb_ref[b, rows, h * D:(h + 1) * D], ones_blk], axis=-1)
        b_r = grow[H + h:H + h + 1, :]
        g_r = grow[h:h + 1, :] - b_r
        b_c = gcol[:, H + h:H + h + 1]
        m_prev = ms[b, h]
        state = states[b, h]

        gmat = jnp.where(tri, g_r, NEG_INF)
        m_c = jnp.maximum(jnp.max(gmat, axis=-1, keepdims=True), m_prev)
        a_inter = jnp.exp(m_prev - m_c)
        sc = jnp.dot(qh, k_t.astype(BF16), preferred_element_type=F32) * jnp.exp(gmat - m_c)
        tot = (jnp.dot(sc.astype(BF16), v1, preferred_element_type=F32)
               + a_inter * jnp.dot(qh, state.astype(BF16), preferred_element_type=F32))
        num = tot[:, :D]
        den = tot[:, D:]
        hh = num / jnp.maximum(jnp.abs(den), jnp.exp(-(b_c + m_c)))

        b_last = b_r[:, L - 1:L]
        m_new = jnp.maximum(b_last + m_prev, b_last + jnp.max(g_r, axis=-1, keepdims=True))
        w_r = jnp.exp(g_r + (b_last - m_new))
        decay = jnp.exp(b_last + m_prev - m_new)
        upd = jnp.dot((k_t * w_r).astype(BF16), v1, preferred_element_type=F32)
        states[b, h] = decay * state + upd
        ms[b, h] = m_new

        og = jax.nn.sigmoid(ob_ref[b, rows, h * D:(h + 1) * D].astype(F32))
        hb = og * hh
        hb = hb * lax.rsqrt(jnp.mean(hb * hb, axis=-1, keepdims=True) + EPS)
        o_ref[b, rows, W_A_Q + h * D:W_A_Q + (h + 1) * D] = (hb * nrm_ref[:, h * D:(h + 1) * D]).astype(BF16)

    for b, h in states:
        state_ref[b, h] = states[b, h]
        m_ref[b, h:h + 1, :] = jnp.broadcast_to(ms[b, h], (1, LANES))


def _seqmix(qkb, vb, ob, gc, gr, qa, kva, conv_w, conv_b, nrm, bias, sinks):
    assert CHUNK == BLOCK
    B, S, _ = qkb.shape
    rows = min(SEQ_SUB * CHUNK, S)
    halo_per_step = rows // CONV_HALO
    blk = lambda w: pl.BlockSpec((B, rows, w), lambda c: (0, c, 0))
    full = lambda a: pl.BlockSpec(a.shape, lambda c: (0,) * a.ndim)
    bias_variant = lambda pick: pl.BlockSpec((None,) + bias.shape[1:], lambda c: (pick(c), 0, 0, 0))
    return pl.pallas_call(
        _seqmix_kernel,
        grid=(S // rows,),
        in_specs=[pl.BlockSpec(memory_space=pltpu.SMEM),
                  blk(2 * W_B),
                  pl.BlockSpec((B, CONV_HALO, 2 * W_B), lambda c: (0, jnp.maximum(c * halo_per_step - 1, 0), 0)),
                  blk(W_B), blk(W_B), blk(LANES),
                  pl.BlockSpec((B, GATE_ROWS, rows), lambda c: (0, 0, c)),
                  blk(W_A_Q), blk(2 * W_A_KV),
                  pl.BlockSpec((B, BLOCK, 2 * W_A_KV), lambda c: (0, jnp.maximum(c * (rows // BLOCK) - 1, 0), 0)),
                  full(conv_w), full(conv_b), full(nrm),
                  bias_variant(lambda c: jnp.minimum(c, 1)), bias_variant(lambda c: 1)],
        out_specs=blk(W_A_Q + W_B),
        out_shape=jax.ShapeDtypeStruct((B, S, W_A_Q + W_B), BF16),
        scratch_shapes=[pltpu.VMEM((B, N_HEADS_B, HEAD_DIM_B, 2 * HEAD_DIM_B), F32),
                        pltpu.VMEM((B, GATE_ROWS, LANES), F32)],
        compiler_params=pltpu.CompilerParams(dimension_semantics=("arbitrary",), vmem_limit_bytes=VMEM_LIMIT),
        name="seqmix",
    )(sinks, qkb, qkb, vb, ob, gc, gr, qa, kva, kva, conv_w, conv_b, nrm, bias, bias)


def _memkv_kernel(mem_ref, g_ref, wk_ref, wv_ref, k_ref, v_ref):
    hm = _rms(mem_ref[...], g_ref[...]).astype(BF16)
    k_ref[...] = jnp.dot(hm, wk_ref[...], preferred_element_type=F32).astype(BF16)
    v_ref[...] = jnp.dot(hm, wv_ref[...], preferred_element_type=F32).astype(BF16)


def _memkv(mem2d, g, wk, wv, B):
    full = lambda a: pl.BlockSpec(a.shape, lambda b: (0,) * a.ndim)
    blk = pl.BlockSpec((N_MEM, D_MODEL), lambda b: (b, 0))
    return pl.pallas_call(
        _memkv_kernel,
        grid=(B,),
        in_specs=[blk, full(g), full(wk), full(wv)],
        out_specs=[blk, blk],
        out_shape=[jax.ShapeDtypeStruct((B * N_MEM, D_MODEL), BF16)] * 2,
        compiler_params=pltpu.CompilerParams(dimension_semantics=("parallel",), vmem_limit_bytes=VMEM_LIMIT),
        name="memkv",
    )(mem2d, g, wk, wv)


ROUTE_E1, ROUTE_E2, ROUTE_G1, ROUTE_G2, ROUTE_R1, ROUTE_R2 = 0, 1, 2, 3, 4, 5
ROUTER_GROUP_COL = N_EXPERTS
ROUTER_ROWS = -(-(N_EXPERTS + N_GROUPS) // SUBLANES) * SUBLANES


def _mid_kernel(x_ref, mix_ref, wo_ref, gx_ref, wq_ref, ck_ref, cv_ref, wco_ref, gz_ref, wr_ref, br_ref,
                x2_ref, hz_ref, route_ref, rrows_ref, counts_ref, cnt_ref):
    @pl.when(pl.program_id(0) == 0)
    def _():
        cnt_ref[...] = jnp.zeros_like(cnt_ref)

    x1 = x_ref[...] + jnp.dot(mix_ref[...], wo_ref[...], preferred_element_type=F32)

    hc = _rms(x1, gx_ref[...]).astype(BF16)
    cq = jnp.dot(hc, wq_ref[...], preferred_element_type=F32).astype(BF16)
    scale = HEAD_DIM_X ** -0.5
    heads = []
    for h in range(N_HEADS_X):
        sl = slice(h * HEAD_DIM_X, (h + 1) * HEAD_DIM_X)
        s = lax.dot_general(cq[:, sl], ck_ref[:, sl], (((1,), (1,)), ((), ())), preferred_element_type=F32) * scale
        p = jnp.exp(s - jnp.max(s, axis=-1, keepdims=True))
        co = jnp.dot(p.astype(BF16), cv_ref[:, sl], preferred_element_type=F32) / jnp.sum(p, axis=-1, keepdims=True)
        heads.append(co.astype(BF16))
    x2 = x1 + jnp.dot(jnp.concatenate(heads, axis=-1), wco_ref[...], preferred_element_type=F32)
    x2_ref[...] = x2

    hz = _rms(x2, gz_ref[...])
    hz_ref[...] = _pack_halves(hz)
    lg = jnp.dot(hz.astype(BF16), wr_ref[...], preferred_element_type=F32) + br_ref[...]
    tm = lg.shape[0]
    lt = jnp.transpose(lg)[0:ROUTER_ROWS, :]
    row = lax.broadcasted_iota(I32, lt.shape, 0)
    big = jnp.int32(ROUTER_ROWS)
    is_g = (row >= ROUTER_GROUP_COL) & (row < ROUTER_GROUP_COL + N_GROUPS)
    gl = jnp.where(is_g, lt, NEG_INF)
    gmax = jnp.max(gl, axis=0, keepdims=True)
    gsum = jnp.sum(jnp.exp(gl - gmax), axis=0, keepdims=True)
    g_prob = 1.0 / gsum
    g_idx = jnp.min(jnp.where(gl == gmax, row - ROUTER_GROUP_COL, big), axis=0, keepdims=True)
    sel = (row < N_EXPERTS) & ((row // EXPERTS_PER_GROUP) == g_idx)
    el = jnp.where(sel, lt, NEG_INF)
    m1 = jnp.max(el, axis=0, keepdims=True)
    i1 = jnp.min(jnp.where(el == m1, row, big), axis=0, keepdims=True)
    el2 = jnp.where(row == i1, NEG_INF, el)
    m2 = jnp.max(el2, axis=0, keepdims=True)
    i2 = jnp.min(jnp.where(el2 == m2, row, big), axis=0, keepdims=True)
    z = jnp.sum(jnp.exp(el - m1), axis=0, keepdims=True)
    p1 = 1.0 / z
    p2 = jnp.exp(m2 - m1) / z
    g1 = g_prob * (p1 / (p1 + p2))
    g2 = g_prob * (p2 / (p1 + p2))

    used = jnp.where((row == i1) | (row == i2), 1.0, 0.0)
    t_from = lax.broadcasted_iota(I32, (tm, tm), 0)
    t_to = lax.broadcasted_iota(I32, (tm, tm), 1)
    earlier = jnp.where(t_from < t_to, 1.0, 0.0).astype(BF16)
    before = jnp.dot(used.astype(BF16), earlier, preferred_element_type=F32) + cnt_ref[:, 0:1]
    r1 = jnp.sum(jnp.where(row == i1, before, 0.0), axis=0, keepdims=True)
    r2 = jnp.sum(jnp.where(row == i2, before, 0.0), axis=0, keepdims=True)
    cnt_ref[...] = cnt_ref[...] + jnp.sum(used, axis=1, keepdims=True)
    counts_ref[...] = cnt_ref[...]

    rec_row = lax.broadcasted_iota(I32, (SUBLANES, tm), 0)
    rec = jnp.zeros((SUBLANES, tm), F32)
    for c, v in ((ROUTE_E1, i1.astype(F32)), (ROUTE_E2, i2.astype(F32)), (ROUTE_G1, g1), (ROUTE_G2, g2),
                 (ROUTE_R1, r1), (ROUTE_R2, r2)):
        rec = jnp.where(rec_row == c, v, rec)
    rrows_ref[...] = rec
    route_ref[...] = jnp.transpose(jnp.concatenate([rec, jnp.zeros((LANES - SUBLANES, tm), F32)], axis=0))


def _mid(x2d, mix, wo, gx, wq, ck, cv, wco, gz, wr, br, B, S):
    T = B * S
    tm = min(TM_MID, S)
    per_b = S // tm
    row = lambda w: pl.BlockSpec((tm, w), lambda i: (i, 0))
    full = lambda a: pl.BlockSpec(a.shape, lambda i: (0,) * a.ndim)
    kvspec = pl.BlockSpec((N_MEM, D_MODEL), lambda i: (i // per_b, 0))
    return pl.pallas_call(
        _mid_kernel,
        grid=(T // tm,),
        in_specs=[row(D_MODEL), row(W_A_Q + W_B), full(wo), full(gx), full(wq), kvspec, kvspec,
                  full(wco), full(gz), full(wr), full(br)],
        out_specs=[row(D_MODEL), row(HALF), row(LANES), pl.BlockSpec((SUBLANES, tm), lambda i: (0, i)),
                   pl.BlockSpec((ROUTER_ROWS, LANES), lambda i: (0, 0))],
        out_shape=[jax.ShapeDtypeStruct((T, D_MODEL), F32),
                   jax.ShapeDtypeStruct((T, HALF), U32),
                   jax.ShapeDtypeStruct((T, LANES), F32),
                   jax.ShapeDtypeStruct((SUBLANES, T), F32),
                   jax.ShapeDtypeStruct((ROUTER_ROWS, LANES), F32)],
        scratch_shapes=[pltpu.VMEM((ROUTER_ROWS, LANES), F32)],
        compiler_params=pltpu.CompilerParams(dimension_semantics=("arbitrary",), vmem_limit_bytes=VMEM_LIMIT),
        name="mid",
    )(x2d, mix, wo, gx, wq, ck, cv, wco, gz, wr, br)


def _dispatch_kernel(zf_ref, pos_ref, hz_ref, xs_hbm, zbuf, sem, zsem):
    i = pl.program_id(0)
    tm = hz_ref.shape[0]
    zrows = zbuf.shape[0]

    @pl.when(i == 0)
    def _():
        zbuf[...] = jnp.zeros_like(zbuf)

        def fill(t):
            return pltpu.make_async_copy(zbuf, xs_hbm.at[pl.ds(t * zrows, zrows)], zsem)

        def start(t, carry):
            @pl.when(zf_ref[t] != 0)
            def _():
                fill(t).start()
            return carry

        def wait(t, carry):
            @pl.when(zf_ref[t] != 0)
            def _():
                fill(t).wait()
            return carry

        lax.fori_loop(0, zf_ref.shape[0], start, 0)
        lax.fori_loop(0, zf_ref.shape[0], wait, 0)

    for r in range(tm):
        for k in range(TOP_K):
            pltpu.make_async_copy(hz_ref.at[pl.ds(r, 1)], xs_hbm.at[pl.ds(pos_ref[0, 0, k * tm + r], 1)],
                                  sem).start(priority=k % 2)
    for k in range(TOP_K):
        pltpu.make_async_copy(hz_ref, xs_hbm.at[pl.ds(0, tm)], sem).wait()


def _dispatch(hz_packed, pos, zfill, n_slots, tmx):
    T = hz_packed.shape[0]
    tm = pos.shape[2] // TOP_K
    grid_spec = pltpu.PrefetchScalarGridSpec(
        num_scalar_prefetch=1,
        grid=(T // tm,),
        in_specs=[pl.BlockSpec((1, 1, TOP_K * tm), lambda i, zf: (i, 0, 0), memory_space=pltpu.SMEM),
                  pl.BlockSpec((tm, HALF), lambda i, zf: (i, 0))],
        out_specs=pl.BlockSpec(memory_space=pl.ANY),
        scratch_shapes=[pltpu.VMEM((tmx, HALF), U32), pltpu.SemaphoreType.DMA(()), pltpu.SemaphoreType.DMA(())],
    )
    return pl.pallas_call(
        _dispatch_kernel,
        grid_spec=grid_spec,
        out_shape=jax.ShapeDtypeStruct((n_slots, HALF), U32),
        compiler_params=pltpu.CompilerParams(dimension_semantics=("arbitrary",), vmem_limit_bytes=VMEM_LIMIT),
        name="dispatch",
    )(zfill, pos, hz_packed)


def _expert_kernel(te_ref, nt_ref, first_ref, slot_ref, next_ref, xs_ref, wg_hbm, wu_hbm, wd_hbm, ys_ref,
                   wg32, wu32, wd32, wgb, wub, wdb, wsem):
    i = pl.program_id(0)
    nt = nt_ref[0]

    def fetch(e, s):
        return [pltpu.make_async_copy(src.at[e], dst.at[s], wsem.at[s])
                for src, dst in ((wg_hbm, wg32), (wu_hbm, wu32), (wd_hbm, wd32))]

    @pl.when(i < nt)
    def _():
        @pl.when(i == 0)
        def _():
            for cp in fetch(te_ref[0], 0):
                cp.start()

        @pl.when(first_ref[i] != 0)
        def _():
            s = slot_ref[i]
            for cp in fetch(te_ref[i], s):
                cp.wait()

            @pl.when(next_ref[i] >= 0)
            def _():
                for cp in fetch(next_ref[i], 1 - s):
                    cp.start()

            wgb[...] = wg32[s].astype(BF16)
            wub[...] = wu32[s].astype(BF16)
            wdb[...] = wd32[s].astype(BF16)

        x = _unpack_halves(xs_ref[...]).astype(BF16)
        hg = jnp.dot(x, wgb[...], preferred_element_type=F32)
        hu = jnp.dot(x, wub[...], preferred_element_type=F32)
        a = (hg * jax.nn.sigmoid(hg) * hu).astype(BF16)
        ys_ref[...] = _pack_halves(jnp.dot(a, wdb[...], preferred_element_type=F32))

    @pl.when(i >= nt)
    def _():
        ys_ref[...] = jnp.zeros_like(ys_ref)


def _experts(xs, w_gate, w_up, w_down, tile_expert, ntiles, run_first, run_slot, run_next, tmx):
    n_tiles_max = tile_expert.shape[0]
    hbm = pl.BlockSpec(memory_space=pl.ANY)
    grid_spec = pltpu.PrefetchScalarGridSpec(
        num_scalar_prefetch=5,
        grid=(n_tiles_max,),
        in_specs=[pl.BlockSpec((tmx, HALF), lambda i, te, nt, *_: (jnp.minimum(i, nt[0] - 1), 0)), hbm, hbm, hbm],
        out_specs=pl.BlockSpec((tmx, HALF), lambda i, *_: (i, 0)),
        scratch_shapes=[pltpu.VMEM((2, D_MODEL, D_EXPERT), F32),
                        pltpu.VMEM((2, D_MODEL, D_EXPERT), F32),
                        pltpu.VMEM((2, D_EXPERT, D_MODEL), F32),
                        pltpu.VMEM((D_MODEL, D_EXPERT), BF16),
                        pltpu.VMEM((D_MODEL, D_EXPERT), BF16),
                        pltpu.VMEM((D_EXPERT, D_MODEL), BF16),
                        pltpu.SemaphoreType.DMA((2,))],
    )
    return pl.pallas_call(
        _expert_kernel,
        grid_spec=grid_spec,
        out_shape=jax.ShapeDtypeStruct(xs.shape, U32),
        compiler_params=pltpu.CompilerParams(dimension_semantics=("arbitrary",), vmem_limit_bytes=VMEM_LIMIT),
        name="experts",
    )(tile_expert, ntiles, run_first, run_slot, run_next, xs, w_gate, w_up, w_down)


def _final_kernel(posc_ref, posn_ref, x2_ref, route_ref, g_ref, ys_hbm, o_ref, ybuf, sem):
    i = pl.program_id(0)
    n = pl.num_programs(0)
    tm = x2_ref.shape[0]
    slot = i % 2

    def issue(pos_ref, s):
        for r in range(tm):
            for k in range(TOP_K):
                pltpu.make_async_copy(ys_hbm.at[pl.ds(pos_ref[0, 0, k * tm + r], 1)],
                                      ybuf.at[s, k, pl.ds(r, 1)], sem.at[s]).start(priority=k % 2)

    def wait(s):
        for k in range(TOP_K):
            pltpu.make_async_copy(ys_hbm.at[pl.ds(0, tm)], ybuf.at[s, k], sem.at[s]).wait()

    @pl.when(i == 0)
    def _():
        issue(posc_ref, 0)

    wait(slot)

    for s in range(2):
        @pl.when(slot == s)
        def _():
            issue(posn_ref, 1 - s)

    r = route_ref[...]
    g1 = r[:, ROUTE_G1:ROUTE_G1 + 1]
    g2 = r[:, ROUTE_G2:ROUTE_G2 + 1]
    xo = x2_ref[...] + g1 * _unpack_halves(ybuf[slot, 0]) + g2 * _unpack_halves(ybuf[slot, 1])
    o_ref[...] = _rms(xo, g_ref[...])

    @pl.when(i == n - 1)
    def _():
        wait(1 - slot)


def _final(x2, ys, pos, route, g):
    T = x2.shape[0]
    nblk = pos.shape[0]
    tm = T // nblk
    row = lambda w: pl.BlockSpec((tm, w), lambda i: (i, 0))
    return pl.pallas_call(
        _final_kernel,
        grid=(nblk,),
        in_specs=[pl.BlockSpec((1, 1, TOP_K * tm), lambda i: (i, 0, 0), memory_space=pltpu.SMEM),
                  pl.BlockSpec((1, 1, TOP_K * tm), lambda i: (jnp.minimum(i + 1, nblk - 1), 0, 0),
                               memory_space=pltpu.SMEM),
                  row(D_MODEL), row(LANES), pl.BlockSpec(g.shape, lambda i: (0, 0)),
                  pl.BlockSpec(memory_space=pl.ANY)],
        out_specs=row(D_MODEL),
        out_shape=jax.ShapeDtypeStruct((T, D_MODEL), F32),
        scratch_shapes=[pltpu.VMEM((2, TOP_K, tm, HALF), U32), pltpu.SemaphoreType.DMA((2,))],
        compiler_params=pltpu.CompilerParams(dimension_semantics=("arbitrary",), vmem_limit_bytes=VMEM_LIMIT),
        name="final",
    )(pos, pos, x2, route, g, ys)


def _band_bias(table):
    assert WINDOW == BLOCK
    i = jnp.arange(BLOCK)[:, None]
    j = jnp.arange(2 * BLOCK)[None, :]
    n = jnp.maximum(i + BLOCK - j, 0)
    nf = jnp.maximum(n, 1).astype(F32)
    large = MAX_EXACT + (jnp.log(nf / MAX_EXACT) / math.log(MAX_DISTANCE / MAX_EXACT)
                         * (NUM_BUCKETS - MAX_EXACT)).astype(I32)
    large = jnp.minimum(large, NUM_BUCKETS - 1)
    bucket = jnp.where(n < MAX_EXACT, n, large)
    onehot = (bucket[:, :, None] == jnp.arange(NUM_BUCKETS)[None, None, :]).astype(F32)
    bias = jnp.einsum("ijb,bh->hij", onehot, table.astype(F32), precision=lax.Precision.HIGHEST)
    from_prev = (jnp.arange(BLOCK)[None, :] > i)[None]
    prev, cur = bias[:, :, :BLOCK], bias[:, :, BLOCK:]
    return jnp.stack([jnp.where(from_prev, NEG_INF, cur), jnp.where(from_prev, prev, cur)])


def _dispatch_plan(route_rows, counts_f, tmx, n_tiles_max, tm_rows):
    T = route_rows.shape[1]
    experts = jnp.arange(N_EXPERTS, dtype=I32)
    counts = counts_f[:N_EXPERTS, 0].astype(I32)
    ptiles = (counts + tmx - 1) // tmx
    tile_end = jnp.cumsum(ptiles)
    nt = tile_end[-1]
    row_off = (tile_end - ptiles) * tmx

    def slot(e_row, r_row):
        e = route_rows[e_row].astype(I32)
        off = jnp.sum(jnp.where(e[None, :] == experts[:, None], row_off[:, None], 0), axis=0)
        return (off + route_rows[r_row].astype(I32)).reshape(T // tm_rows, 1, tm_rows)

    pos = jnp.concatenate([slot(ROUTE_E1, ROUTE_R1), slot(ROUTE_E2, ROUTE_R2)], axis=2)

    tile_ids = jnp.arange(n_tiles_max, dtype=I32)
    expert_of = lambda t: jnp.sum((tile_end[None, :] <= t[:, None]).astype(I32), axis=1)
    te = expert_of(jnp.minimum(tile_ids, nt - 1))
    partial = jnp.any((tile_ids[:, None] == (tile_end - 1)[None, :]) & (counts % tmx != 0)[None, :], axis=1)
    zfill = (partial | (tile_ids >= nt)).astype(I32)

    used = ptiles > 0
    run_first = (jnp.any((tile_ids[:, None] == (tile_end - ptiles)[None, :]) & used[None, :], axis=1)
                 & (tile_ids < nt)).astype(I32)
    run_slot = (jnp.cumsum(run_first) - 1) % 2
    later_used = used[None, :] & (experts[None, :] > experts[:, None])
    next_of = jnp.min(jnp.where(later_used, experts[None, :], N_EXPERTS), axis=1)
    next_of = jnp.where(next_of < N_EXPERTS, next_of, -1)
    run_next = jnp.sum(jnp.where(te[:, None] == experts[None, :], next_of[None, :], 0), axis=1)
    return pos, te, nt.reshape(1), zfill, run_first, run_slot.astype(I32), run_next.astype(I32)


def kernel(x, mem, rel_bias_table, norm_mix, w_in, attn_sinks, conv_w, conv_b, gate_bias_i, gate_bias_f, mlstm_norm, w_out, norm_cross, norm_mem, w_cq, w_ck, w_cv, w_co, norm_moe, w_router_group, b_router_group, w_router_expert, b_router_expert, w_exp_gate, w_exp_up, w_exp_down, norm_final):
    B, S, _ = x.shape
    T = B * S
    depth = w_in.shape[0]
    x2d = x.reshape(T, D_MODEL)
    mem2d = mem.reshape(B * N_MEM, D_MODEL)
    bias = _band_bias(rel_bias_table)

    tmx = min(TM_EXPERT, T)
    n_tiles_max = (T * TOP_K) // tmx + N_EXPERTS
    tm_rows = min(TM_ROWDMA, T)

    assert depth == 1, "the final combine is fused with the final norm: single layer only"
    l = 0
    w_pad = jnp.pad(w_in[l], ((0, 0), (0, C_GATE + LANES - D_IN))).astype(BF16)
    gb = jnp.concatenate([gate_bias_i[l], gate_bias_f[l]]).astype(F32)
    gbias_col = jnp.pad(gb, (0, LANES - GATE_ROWS))[None, :]
    qa, kva, qkb, vb, ob, gc, gr = _inproj(x2d, norm_mix[l][None, :], w_pad, gbias_col, B, S)

    per_seq = lambda a: a.reshape(B, S, a.shape[-1])
    mix = _seqmix(per_seq(qkb), per_seq(vb), per_seq(ob), per_seq(gc), gr, per_seq(qa), per_seq(kva),
                  conv_w[l][:, 0, :].astype(F32), conv_b[l][None, :].astype(F32),
                  mlstm_norm[l][None, :].astype(F32), bias, attn_sinks[l].astype(F32)).reshape(T, W_A_Q + W_B)

    ck, cv = _memkv(mem2d, norm_mem[l][None, :], w_ck[l].astype(BF16), w_cv[l].astype(BF16), B)

    wr = jnp.pad(jnp.concatenate([w_router_expert[l], w_router_group[l]], axis=1),
                 ((0, 0), (0, LANES - N_EXPERTS - N_GROUPS))).astype(BF16)
    br = jnp.pad(jnp.concatenate([b_router_expert[l], b_router_group[l]]),
                 (0, LANES - N_EXPERTS - N_GROUPS)).astype(F32)[None, :]
    x2, hz_packed, route, route_rows, counts = _mid(
        x2d, mix, w_out[l].astype(BF16), norm_cross[l][None, :], w_cq[l].astype(BF16), ck, cv,
        w_co[l].astype(BF16), norm_moe[l][None, :], wr, br, B, S)

    pos, te, nt, zfill, run_first, run_slot, run_next = _dispatch_plan(route_rows, counts, tmx, n_tiles_max,
                                                                       tm_rows)
    xs = _dispatch(hz_packed, pos, zfill, n_tiles_max * tmx, tmx)
    ys = _experts(xs, w_exp_gate[l], w_exp_up[l], w_exp_down[l], te, nt, run_first, run_slot, run_next, tmx)
    out = _final(x2, ys, pos, route, norm_final[None, :])
    return out.reshape(B, S, D_MODEL)
```

```python
import math

import jax
import jax.numpy as jnp
from jax import lax
from jax.experimental import pallas as pl
from jax.experimental.pallas import tpu as pltpu

F32 = jnp.float32
BF16 = jnp.bfloat16
U32 = jnp.uint32
I32 = jnp.int32

D_MODEL = 1024
N_MEM = 256
N_HEADS_A = 8
N_KV_A = 2
HEAD_DIM_A = 64
BLOCK = 128
WINDOW = 128
NUM_BUCKETS = 32
MAX_EXACT = NUM_BUCKETS // 2
MAX_DISTANCE = 128
N_HEADS_B = 4
HEAD_DIM_B = 128
CHUNK = 128
CONV_WIDTH = 4
N_HEADS_X = 4
HEAD_DIM_X = D_MODEL // N_HEADS_X
N_GROUPS = 4
EXPERTS_PER_GROUP = 8
N_EXPERTS = N_GROUPS * EXPERTS_PER_GROUP
TOP_K = 2
D_EXPERT = 512
EPS = 1e-6
NEG_INF = -1e30

W_A_Q = N_HEADS_A * HEAD_DIM_A
W_A_KV = N_KV_A * HEAD_DIM_A
W_B = N_HEADS_B * HEAD_DIM_B
C_QA = 0
C_KVA = C_QA + W_A_Q
C_QKB = C_KVA + 2 * W_A_KV
C_VB = C_QKB + 2 * W_B
C_OB = C_VB + W_B
C_GATE = C_OB + W_B
D_IN = C_GATE + 2 * N_HEADS_B

LANES = 128
SUBLANES = 8
GATE_ROWS = 8
HALF = D_MODEL // 2

TM_INPROJ = 1024
TM_MID = 1024
TM_ROWDMA = 512
TM_EXPERT = 512

V7X_VMEM_BYTES = 64 * 1024 * 1024
VMEM_LIMIT = V7X_VMEM_BYTES * 3 // 4


def _rms(xf, g):
    return xf * lax.rsqrt(jnp.mean(xf * xf, axis=-1, keepdims=True) + EPS) * g


def _pack_halves(v):
    b = pltpu.bitcast(v.astype(BF16).astype(F32), U32)
    return (b[:, :HALF] >> 16) | b[:, HALF:]


def _unpack_halves(p):
    lo = pltpu.bitcast(p << 16, F32)
    hi = pltpu.bitcast(p & jnp.uint32(0xFFFF0000), F32)
    return jnp.concatenate([lo, hi], axis=-1)


def _log_sigmoid(z):
    return jnp.minimum(z, 0.0) - jnp.log1p(jnp.exp(-jnp.abs(z)))


def _split3(v):
    hi = v.astype(BF16).astype(F32)
    rest = v - hi
    mid = rest.astype(BF16).astype(F32)
    return hi, mid, (rest - mid).astype(BF16).astype(F32)


def _inproj_kernel(x_ref, g_ref, w32_ref, gbc_ref, qa_ref, kva_ref, qkb_ref, vb_ref, ob_ref, gc_ref, gr_ref,
                   w_ref):
    @pl.when(pl.program_id(0) == 0)
    def _():
        w_ref[:, C_GATE:] = jnp.zeros((D_MODEL, LANES), BF16)
        w_ref[:, :D_IN] = w32_ref[...].astype(BF16)

    tm = x_ref.shape[0]
    h = _rms(x_ref[...], g_ref[...]).astype(BF16)

    def mm(lo, hi):
        return jnp.dot(h, w_ref[:, lo:hi], preferred_element_type=F32)

    qa_ref[...] = mm(C_QA, C_KVA).astype(BF16)
    kva_ref[...] = mm(C_KVA, C_QKB).astype(BF16)
    qkb_ref[...] = mm(C_QKB, C_VB).astype(BF16)
    vb_ref[...] = mm(C_VB, C_OB).astype(BF16)
    ob_ref[...] = mm(C_OB, C_GATE).astype(BF16)

    H, L = N_HEADS_B, CHUNK
    gcol = mm(C_GATE, C_GATE + LANES) + gbc_ref[...]
    grow = jnp.transpose(gcol)[0:GATE_ROWS, :]
    lane_c = lax.broadcasted_iota(I32, (L, LANES), 1)
    is_f_col = (lane_c >= H) & (lane_c < 2 * H)
    is_f_row = lax.broadcasted_iota(I32, (GATE_ROWS, L), 0) >= H
    ti = lax.broadcasted_iota(I32, (L, L), 0)
    si = lax.broadcasted_iota(I32, (L, L), 1)
    tril = jnp.where(si <= ti, 1.0, 0.0).astype(BF16)
    triu = jnp.where(si >= ti, 1.0, 0.0).astype(BF16)
    for c in range(tm // L):
        rows = slice(c * L, (c + 1) * L)
        gcol_c = gcol[rows, :]
        fcol = jnp.where(is_f_col, _log_sigmoid(gcol_c), 0.0)
        parts = jnp.dot(tril, jnp.concatenate(_split3(fcol), axis=1).astype(BF16), preferred_element_type=F32)
        bcol = parts[:, :LANES] + parts[:, LANES:2 * LANES] + parts[:, 2 * LANES:]
        gc_ref[rows, :] = jnp.where(is_f_col, bcol, gcol_c)
        grow_c = grow[:, rows]
        frow = jnp.where(is_f_row, _log_sigmoid(grow_c), 0.0)
        parts = jnp.dot(jnp.concatenate(_split3(frow), axis=0).astype(BF16), triu, preferred_element_type=F32)
        brow = parts[:GATE_ROWS] + parts[GATE_ROWS:2 * GATE_ROWS] + parts[2 * GATE_ROWS:]
        gr_ref[:, rows] = jnp.where(is_f_row, brow, grow_c)


def _inproj(x2d, g, w_in, gbias_col, B, S):
    T = x2d.shape[0]
    tm = min(TM_INPROJ, S)
    tiles_per_seq = S // tm
    row = lambda w: pl.BlockSpec((tm, w), lambda i: (i, 0))
    full = lambda a: pl.BlockSpec(a.shape, lambda i: (0,) * a.ndim)
    return pl.pallas_call(
        _inproj_kernel,
        grid=(T // tm,),
        in_specs=[row(D_MODEL), full(g),
                  pl.BlockSpec(w_in.shape, lambda i: (0, 0), pipeline_mode=pl.Buffered(1)), full(gbias_col)],
        out_specs=[row(W_A_Q), row(2 * W_A_KV), row(2 * W_B), row(W_B), row(W_B), row(LANES),
                   pl.BlockSpec((None, GATE_ROWS, tm), lambda i: (i // tiles_per_seq, 0, i % tiles_per_seq))],
        out_shape=[jax.ShapeDtypeStruct((T, W_A_Q), BF16),
                   jax.ShapeDtypeStruct((T, 2 * W_A_KV), BF16),
                   jax.ShapeDtypeStruct((T, 2 * W_B), BF16),
                   jax.ShapeDtypeStruct((T, W_B), BF16),
                   jax.ShapeDtypeStruct((T, W_B), BF16),
                   jax.ShapeDtypeStruct((T, LANES), F32),
                   jax.ShapeDtypeStruct((B, GATE_ROWS, S), F32)],
        scratch_shapes=[pltpu.VMEM((D_MODEL, C_GATE + LANES), BF16)],
        compiler_params=pltpu.CompilerParams(dimension_semantics=("arbitrary",), vmem_limit_bytes=VMEM_LIMIT),
        name="inproj",
    )(x2d, g, w_in, gbias_col)


def _swa_block(q, kvc, kvp, bias_ref, sink_ref):
    kvp = kvp.astype(F32)
    kvc = kvc.astype(F32)
    kband = jnp.concatenate([kvp[:, :W_A_KV], kvc[:, :W_A_KV]], axis=0)
    vband = jnp.concatenate([kvp[:, W_A_KV:], kvc[:, W_A_KV:]], axis=0)
    lane = lax.broadcasted_iota(I32, (2 * BLOCK, LANES), 1)
    lo = lane < HEAD_DIM_A

    def placements(band):
        swapped = pltpu.roll(band, HEAD_DIM_A, axis=1)
        z = jnp.zeros_like(band)
        return {(0, 0): jnp.where(lo, band, z).astype(BF16), (0, 1): jnp.where(lo, z, swapped).astype(BF16),
                (1, 0): jnp.where(lo, swapped, z).astype(BF16), (1, 1): jnp.where(lo, z, band).astype(BF16)}

    kpl = placements(kband)
    vpl = placements(vband)

    scale = HEAD_DIM_A ** -0.5
    group = N_HEADS_A // N_KV_A
    tiles = []
    from_prev = (lax.broadcasted_iota(I32, (BLOCK, BLOCK), 1) > lax.broadcasted_iota(I32, (BLOCK, BLOCK), 0))

    for pair in range(N_HEADS_A // 2):
        qt = q[:, pair * LANES:(pair + 1) * LANES]
        acc = None
        for half in range(2):
            h = 2 * pair + half
            g = h // group
            s2 = lax.dot_general(qt, kpl[(g, half)], (((1,), (1,)), ((), ())), preferred_element_type=F32)
            s = jnp.where(from_prev, s2[:, :BLOCK], s2[:, BLOCK:])
            s = s * scale + bias_ref[h]
            sink = sink_ref[h]
            m = jnp.maximum(jnp.max(s, axis=-1, keepdims=True), sink)
            p = jnp.exp(s - m)
            denom = jnp.sum(p, axis=-1, keepdims=True) + jnp.exp(sink - m)
            p2 = jnp.concatenate([jnp.where(from_prev, p, 0.0), jnp.where(from_prev, 0.0, p)], axis=1)
            o = jnp.dot(p2.astype(BF16), vpl[(g, half)], preferred_element_type=F32) / denom
            acc = o if acc is None else acc + o
        tiles.append(acc.astype(BF16))
    return tiles


CONV_HALO = 16
SEQ_SUB = 2


def _seqmix_kernel(sink_ref, qkc_ref, qkp_ref, vb_ref, ob_ref, gc_ref, gr_ref, qa_ref, kvc_ref, kvp_ref,
                   cw_ref, cb_ref, nrm_ref, bias0_ref, bias_ref, o_ref, state_ref, m_ref):
    c = pl.program_id(0)
    B = qkc_ref.shape[0]
    H, D, L = N_HEADS_B, HEAD_DIM_B, CHUNK

    @pl.when(c == 0)
    def _():
        state_ref[...] = jnp.zeros_like(state_ref)
        m_ref[...] = jnp.zeros_like(m_ref)

    rr = lax.broadcasted_iota(I32, (L, CONV_HALO + L), 0)
    cc = lax.broadcasted_iota(I32, (L, CONV_HALO + L), 1)
    shifts = {delay: jnp.where(cc == rr + (CONV_HALO - delay), 1.0, 0.0).astype(BF16)
              for delay in range(1, CONV_WIDTH)}
    ti = lax.broadcasted_iota(I32, (L, L), 0)
    si = lax.broadcasted_iota(I32, (L, L), 1)
    tri = si <= ti
    ones_blk = jnp.ones((L, D), BF16)

    def conv_silu(b, u):
        if u == 0:
            prev = qkp_ref[b]
            prev = jnp.where(c > 0, prev, jnp.zeros_like(prev))
        else:
            prev = qkc_ref[b, u * L - CONV_HALO:u * L, :]
        cur = qkc_ref[b, u * L:(u + 1) * L, :]
        ext = jnp.concatenate([prev, cur], axis=0)
        y = cb_ref[...] + cw_ref[CONV_WIDTH - 1:CONV_WIDTH, :] * cur.astype(F32)
        for delay in range(1, CONV_WIDTH):
            tap = CONV_WIDTH - 1 - delay
            y = y + cw_ref[tap:tap + 1, :] * jnp.dot(shifts[delay], ext, preferred_element_type=F32)
        return y * jax.nn.sigmoid(y)

    states = {(b, h): state_ref[b, h] for b in range(B) for h in range(H)}
    ms = {(b, h): m_ref[b, h:h + 1, 0:1] for b in range(B) for h in range(H)}

    n_sub = qkc_ref.shape[1] // L
    for u, b, h in [(u, b, h) for u in range(n_sub) for b in range(B) for h in range(H)]:
        rows = slice(u * L, (u + 1) * L)
        if h == 0:
            kvp = kvp_ref[b] if u == 0 else kvc_ref[b, (u - 1) * L:u * L, :]
            tiles = _swa_block(qa_ref[b, rows, :], kvc_ref[b, rows, :], kvp, bias0_ref if u == 0 else bias_ref,
                               sink_ref)
            for pair, tile in enumerate(tiles):
                o_ref[b, rows, pair * LANES:(pair + 1) * LANES] = tile
            qk = conv_silu(b, u)
            gcol = gc_ref[b, rows, :]
            grow = gr_ref[b, :, rows]
        qh = (qk[:, h * D:(h + 1) * D] * (D ** -0.5)).astype(BF16)
        k_t = qk[:, W_B + h * D:W_B + (h + 1) * D].T
        v1 = jnp.concatenate([vb_ref[b, rows, h * D:(h + 1) * D], ones_blk], axis=-1)
        b_r = grow[H + h:H + h + 1, :]
        g_r = grow[h:h + 1, :] - b_r
        b_c = gcol[:, H + h:H + h + 1]
        m_prev = ms[b, h]
        state = states[b, h]

        gmat = jnp.where(tri, g_r, NEG_INF)
        m_c = jnp.maximum(jnp.max(gmat, axis=-1, keepdims=True), m_prev)
        a_inter = jnp.exp(m_prev - m_c)
        sc = jnp.dot(qh, k_t.astype(BF16), preferred_element_type=F32) * jnp.exp(gmat - m_c)
        tot = (jnp.dot(sc.astype(BF16), v1, preferred_element_type=F32)
               + a_inter * jnp.dot(qh, state.astype(BF16), preferred_element_type=F32))
        num = tot[:, :D]
        den = tot[:, D:]
        hh = num / jnp.maximum(jnp.abs(den), jnp.exp(-(b_c + m_c)))

        b_last = b_r[:, L - 1:L]
        m_new = jnp.maximum(b_last + m_prev, b_last + jnp.max(g_r, axis=-1, keepdims=True))
        w_r = jnp.exp(g_r + (b_last - m_new))
        decay = jnp.exp(b_last + m_prev - m_new)
        upd = jnp.dot((k_t * w_r).astype(BF16), v1, preferred_element_type=F32)
        states[b, h] = decay * state + upd
        ms[b, h] = m_new

        og = jax.nn.sigmoid(ob_ref[b, rows, h * D:(h + 1) * D].astype(F32))
        hb = og * hh
        hb = hb * lax.rsqrt(jnp.mean(hb * hb, axis=-1, keepdims=True) + EPS)
        o_ref[b, rows, W_A_Q + h * D:W_A_Q + (h + 1) * D] = (hb * nrm_ref[:, h * D:(h + 1) * D]).astype(BF16)

    for b, h in states:
        state_ref[b, h] = states[b, h]
        m_ref[b, h:h + 1, :] = jnp.broadcast_to(ms[b, h], (1, LANES))


def _seqmix(qkb, vb, ob, gc, gr, qa, kva, conv_w, conv_b, nrm, bias, sinks):
    assert CHUNK == BLOCK
    B, S, _ = qkb.shape
    rows = min(SEQ_SUB * CHUNK, S)
    halo_per_step = rows // CONV_HALO
    blk = lambda w: pl.BlockSpec((B, rows, w), lambda c: (0, c, 0))
    full = lambda a: pl.BlockSpec(a.shape, lambda c: (0,) * a.ndim)
    bias_variant = lambda pick: pl.BlockSpec((None,) + bias.shape[1:], lambda c: (pick(c), 0, 0, 0))
    return pl.pallas_call(
        _seqmix_kernel,
        grid=(S // rows,),
        in_specs=[pl.BlockSpec(memory_space=pltpu.SMEM),
                  blk(2 * W_B),
                  pl.BlockSpec((B, CONV_HALO, 2 * W_B), lambda c: (0, jnp.maximum(c * halo_per_step - 1, 0), 0)),
                  blk(W_B), blk(W_B), blk(LANES),
                  pl.BlockSpec((B, GATE_ROWS, rows), lambda c: (0, 0, c)),
                  blk(W_A_Q), blk(2 * W_A_KV),
                  pl.BlockSpec((B, BLOCK, 2 * W_A_KV), lambda c: (0, jnp.maximum(c * (rows // BLOCK) - 1, 0), 0)),
                  full(conv_w), full(conv_b), full(nrm),
                  bias_variant(lambda c: jnp.minimum(c, 1)), bias_variant(lambda c: 1)],
        out_specs=blk(W_A_Q + W_B),
        out_shape=jax.ShapeDtypeStruct((B, S, W_A_Q + W_B), BF16),
        scratch_shapes=[pltpu.VMEM((B, N_HEADS_B, HEAD_DIM_B, 2 * HEAD_DIM_B), F32),
                        pltpu.VMEM((B, GATE_ROWS, LANES), F32)],
        compiler_params=pltpu.CompilerParams(dimension_semantics=("arbitrary",), vmem_limit_bytes=VMEM_LIMIT),
        name="seqmix",
    )(sinks, qkb, qkb, vb, ob, gc, gr, qa, kva, kva, conv_w, conv_b, nrm, bias, bias)


def _memkv_kernel(mem_ref, g_ref, wk_ref, wv_ref, k_ref, v_ref):
    hm = _rms(mem_ref[...], g_ref[...]).astype(BF16)
    k_ref[...] = jnp.dot(hm, wk_ref[...], preferred_element_type=F32).astype(BF16)
    v_ref[...] = jnp.dot(hm, wv_ref[...], preferred_element_type=F32).astype(BF16)


def _memkv(mem2d, g, wk, wv, B):
    full = lambda a: pl.BlockSpec(a.shape, lambda b: (0,) * a.ndim)
    blk = pl.BlockSpec((N_MEM, D_MODEL), lambda b: (b, 0))
    return pl.pallas_call(
        _memkv_kernel,
        grid=(B,),
        in_specs=[blk, full(g), full(wk), full(wv)],
        out_specs=[blk, blk],
        out_shape=[jax.ShapeDtypeStruct((B * N_MEM, D_MODEL), BF16)] * 2,
        compiler_params=pltpu.CompilerParams(dimension_semantics=("parallel",), vmem_limit_bytes=VMEM_LIMIT),
        name="memkv",
    )(mem2d, g, wk, wv)


ROUTE_E1, ROUTE_E2, ROUTE_G1, ROUTE_G2, ROUTE_R1, ROUTE_R2 = 0, 1, 2, 3, 4, 5
ROUTER_GROUP_COL = N_EXPERTS
ROUTER_ROWS = -(-(N_EXPERTS + N_GROUPS) // SUBLANES) * SUBLANES


def _mid_kernel(x_ref, mix_ref, wo_ref, gx_ref, wq_ref, ck_ref, cv_ref, wco_ref, gz_ref, wr_ref, br_ref,
                x2_ref, hz_ref, route_ref, rrows_ref, counts_ref, cnt_ref):
    @pl.when(pl.program_id(0) == 0)
    def _():
        cnt_ref[...] = jnp.zeros_like(cnt_ref)

    x1 = x_ref[...] + jnp.dot(mix_ref[...], wo_ref[...], preferred_element_type=F32)

    hc = _rms(x1, gx_ref[...]).astype(BF16)
    cq = jnp.dot(hc, wq_ref[...], preferred_element_type=F32).astype(BF16)
    scale = HEAD_DIM_X ** -0.5
    heads = []
    for h in range(N_HEADS_X):
        sl = slice(h * HEAD_DIM_X, (h + 1) * HEAD_DIM_X)
        s = lax.dot_general(cq[:, sl], ck_ref[:, sl], (((1,), (1,)), ((), ())), preferred_element_type=F32) * scale
        p = jnp.exp(s - jnp.max(s, axis=-1, keepdims=True))
        co = jnp.dot(p.astype(BF16), cv_ref[:, sl], preferred_element_type=F32) / jnp.sum(p, axis=-1, keepdims=True)
        heads.append(co.astype(BF16))
    x2 = x1 + jnp.dot(jnp.concatenate(heads, axis=-1), wco_ref[...], preferred_element_type=F32)
    x2_ref[...] = x2

    hz = _rms(x2, gz_ref[...])
    hz_ref[...] = _pack_halves(hz)
    lg = jnp.dot(hz.astype(BF16), wr_ref[...], preferred_element_type=F32) + br_ref[...]
    tm = lg.shape[0]
    lt = jnp.transpose(lg)[0:ROUTER_ROWS, :]
    row = lax.broadcasted_iota(I32, lt.shape, 0)
    big = jnp.int32(ROUTER_ROWS)
    is_g = (row >= ROUTER_GROUP_COL) & (row < ROUTER_GROUP_COL + N_GROUPS)
    gl = jnp.where(is_g, lt, NEG_INF)
    gmax = jnp.max(gl, axis=0, keepdims=True)
    gsum = jnp.sum(jnp.exp(gl - gmax), axis=0, keepdims=True)
    g_prob = 1.0 / gsum
    g_idx = jnp.min(jnp.where(gl == gmax, row - ROUTER_GROUP_COL, big), axis=0, keepdims=True)
    sel = (row < N_EXPERTS) & ((row // EXPERTS_PER_GROUP) == g_idx)
    el = jnp.where(sel, lt, NEG_INF)
    m1 = jnp.max(el, axis=0, keepdims=True)
    i1 = jnp.min(jnp.where(el == m1, row, big), axis=0, keepdims=True)
    el2 = jnp.where(row == i1, NEG_INF, el)
    m2 = jnp.max(el2, axis=0, keepdims=True)
    i2 = jnp.min(jnp.where(el2 == m2, row, big), axis=0, keepdims=True)
    z = jnp.sum(jnp.exp(el - m1), axis=0, keepdims=True)
    p1 = 1.0 / z
    p2 = jnp.exp(m2 - m1) / z
    g1 = g_prob * (p1 / (p1 + p2))
    g2 = g_prob * (p2 / (p1 + p2))

    used = jnp.where((row == i1) | (row == i2), 1.0, 0.0)
    t_from = lax.broadcasted_iota(I32, (tm, tm), 0)
    t_to = lax.broadcasted_iota(I32, (tm, tm), 1)
    earlier = jnp.where(t_from < t_to, 1.0, 0.0).astype(BF16)
    before = jnp.dot(used.astype(BF16), earlier, preferred_element_type=F32) + cnt_ref[:, 0:1]
    r1 = jnp.sum(jnp.where(row == i1, before, 0.0), axis=0, keepdims=True)
    r2 = jnp.sum(jnp.where(row == i2, before, 0.0), axis=0, keepdims=True)
    cnt_ref[...] = cnt_ref[...] + jnp.sum(used, axis=1, keepdims=True)
    counts_ref[...] = cnt_ref[...]

    rec_row = lax.broadcasted_iota(I32, (SUBLANES, tm), 0)
    rec = jnp.zeros((SUBLANES, tm), F32)
    for c, v in ((ROUTE_E1, i1.astype(F32)), (ROUTE_E2, i2.astype(F32)), (ROUTE_G1, g1), (ROUTE_G2, g2),
                 (ROUTE_R1, r1), (ROUTE_R2, r2)):
        rec = jnp.where(rec_row == c, v, rec)
    rrows_ref[...] = rec
    route_ref[...] = jnp.transpose(jnp.concatenate([rec, jnp.zeros((LANES - SUBLANES, tm), F32)], axis=0))


def _mid(x2d, mix, wo, gx, wq, ck, cv, wco, gz, wr, br, B, S):
    T = B * S
    tm = min(TM_MID, S)
    per_b = S // tm
    row = lambda w: pl.BlockSpec((tm, w), lambda i: (i, 0))
    full = lambda a: pl.BlockSpec(a.shape, lambda i: (0,) * a.ndim)
    kvspec = pl.BlockSpec((N_MEM, D_MODEL), lambda i: (i // per_b, 0))
    return pl.pallas_call(
        _mid_kernel,
        grid=(T // tm,),
        in_specs=[row(D_MODEL), row(W_A_Q + W_B), full(wo), full(gx), full(wq), kvspec, kvspec,
                  full(wco), full(gz), full(wr), full(br)],
        out_specs=[row(D_MODEL), row(HALF), row(LANES), pl.BlockSpec((SUBLANES, tm), lambda i: (0, i)),
                   pl.BlockSpec((ROUTER_ROWS, LANES), lambda i: (0, 0))],
        out_shape=[jax.ShapeDtypeStruct((T, D_MODEL), F32),
                   jax.ShapeDtypeStruct((T, HALF), U32),
                   jax.ShapeDtypeStruct((T, LANES), F32),
                   jax.ShapeDtypeStruct((SUBLANES, T), F32),
                   jax.ShapeDtypeStruct((ROUTER_ROWS, LANES), F32)],
        scratch_shapes=[pltpu.VMEM((ROUTER_ROWS, LANES), F32)],
        compiler_params=pltpu.CompilerParams(dimension_semantics=("arbitrary",), vmem_limit_bytes=VMEM_LIMIT),
        name="mid",
    )(x2d, mix, wo, gx, wq, ck, cv, wco, gz, wr, br)


def _dispatch_kernel(zf_ref, pos_ref, hz_ref, xs_hbm, zbuf, sem, zsem):
    i = pl.program_id(0)
    tm = hz_ref.shape[0]
    zrows = zbuf.shape[0]

    @pl.when(i == 0)
    def _():
        zbuf[...] = jnp.zeros_like(zbuf)

        def fill(t):
            return pltpu.make_async_copy(zbuf, xs_hbm.at[pl.ds(t * zrows, zrows)], zsem)

        def start(t, carry):
            @pl.when(zf_ref[t] != 0)
            def _():
                fill(t).start()
            return carry

        def wait(t, carry):
            @pl.when(zf_ref[t] != 0)
            def _():
                fill(t).wait()
            return carry

        lax.fori_loop(0, zf_ref.shape[0], start, 0)
        lax.fori_loop(0, zf_ref.shape[0], wait, 0)

    for r in range(tm):
        for k in range(TOP_K):
            pltpu.make_async_copy(hz_ref.at[pl.ds(r, 1)], xs_hbm.at[pl.ds(pos_ref[0, 0, k * tm + r], 1)],
                                  sem).start(priority=k % 2)
    for k in range(TOP_K):
        pltpu.make_async_copy(hz_ref, xs_hbm.at[pl.ds(0, tm)], sem).wait()


def _dispatch(hz_packed, pos, zfill, n_slots, tmx):
    T = hz_packed.shape[0]
    tm = pos.shape[2] // TOP_K
    grid_spec = pltpu.PrefetchScalarGridSpec(
        num_scalar_prefetch=1,
        grid=(T // tm,),
        in_specs=[pl.BlockSpec((1, 1, TOP_K * tm), lambda i, zf: (i, 0, 0), memory_space=pltpu.SMEM),
                  pl.BlockSpec((tm, HALF), lambda i, zf: (i, 0))],
        out_specs=pl.BlockSpec(memory_space=pl.ANY),
        scratch_shapes=[pltpu.VMEM((tmx, HALF), U32), pltpu.SemaphoreType.DMA(()), pltpu.SemaphoreType.DMA(())],
    )
    return pl.pallas_call(
        _dispatch_kernel,
        grid_spec=grid_spec,
        out_shape=jax.ShapeDtypeStruct((n_slots, HALF), U32),
        compiler_params=pltpu.CompilerParams(dimension_semantics=("arbitrary",), vmem_limit_bytes=VMEM_LIMIT),
        name="dispatch",
    )(zfill, pos, hz_packed)


def _expert_kernel(te_ref, nt_ref, first_ref, slot_ref, next_ref, xs_ref, wg_hbm, wu_hbm, wd_hbm, ys_ref,
                   wg32, wu32, wd32, wgb, wub, wdb, wsem):
    i = pl.program_id(0)
    nt = nt_ref[0]

    def fetch(e, s):
        return [pltpu.make_async_copy(src.at[e], dst.at[s], wsem.at[s])
                for src, dst in ((wg_hbm, wg32), (wu_hbm, wu32), (wd_hbm, wd32))]

    @pl.when(i < nt)
    def _():
        @pl.when(i == 0)
        def _():
            for cp in fetch(te_ref[0], 0):
                cp.start()

        @pl.when(first_ref[i] != 0)
        def _():
            s = slot_ref[i]
            for cp in fetch(te_ref[i], s):
                cp.wait()

            @pl.when(next_ref[i] >= 0)
            def _():
                for cp in fetch(next_ref[i], 1 - s):
                    cp.start()

            wgb[...] = wg32[s].astype(BF16)
            wub[...] = wu32[s].astype(BF16)
            wdb[...] = wd32[s].astype(BF16)

        x = _unpack_halves(xs_ref[...]).astype(BF16)
        hg = jnp.dot(x, wgb[...], preferred_element_type=F32)
        hu = jnp.dot(x, wub[...], preferred_element_type=F32)
        a = (hg * jax.nn.sigmoid(hg) * hu).astype(BF16)
        ys_ref[...] = _pack_halves(jnp.dot(a, wdb[...], preferred_element_type=F32))

    @pl.when(i >= nt)
    def _():
        ys_ref[...] = jnp.zeros_like(ys_ref)


def _experts(xs, w_gate, w_up, w_down, tile_expert, ntiles, run_first, run_slot, run_next, tmx):
    n_tiles_max = tile_expert.shape[0]
    hbm = pl.BlockSpec(memory_space=pl.ANY)
    grid_spec = pltpu.PrefetchScalarGridSpec(
        num_scalar_prefetch=5,
        grid=(n_tiles_max,),
        in_specs=[pl.BlockSpec((tmx, HALF), lambda i, te, nt, *_: (jnp.minimum(i, nt[0] - 1), 0)), hbm, hbm, hbm],
        out_specs=pl.BlockSpec((tmx, HALF), lambda i, *_: (i, 0)),
        scratch_shapes=[pltpu.VMEM((2, D_MODEL, D_EXPERT), F32),
                        pltpu.VMEM((2, D_MODEL, D_EXPERT), F32),
                        pltpu.VMEM((2, D_EXPERT, D_MODEL), F32),
                        pltpu.VMEM((D_MODEL, D_EXPERT), BF16),
                        pltpu.VMEM((D_MODEL, D_EXPERT), BF16),
                        pltpu.VMEM((D_EXPERT, D_MODEL), BF16),
                        pltpu.SemaphoreType.DMA((2,))],
    )
    return pl.pallas_call(
        _expert_kernel,
        grid_spec=grid_spec,
        out_shape=jax.ShapeDtypeStruct(xs.shape, U32),
        compiler_params=pltpu.CompilerParams(dimension_semantics=("arbitrary",), vmem_limit_bytes=VMEM_LIMIT),
        name="experts",
    )(tile_expert, ntiles, run_first, run_slot, run_next, xs, w_gate, w_up, w_down)


def _final_kernel(posc_ref, posn_ref, x2_ref, route_ref, g_ref, ys_hbm, o_ref, ybuf, sem):
    i = pl.program_id(0)
    n = pl.num_programs(0)
    tm = x2_ref.shape[0]
    slot = i % 2

    def issue(pos_ref, s):
        for r in range(tm):
            for k in range(TOP_K):
                pltpu.make_async_copy(ys_hbm.at[pl.ds(pos_ref[0, 0, k * tm + r], 1)],
                                      ybuf.at[s, k, pl.ds(r, 1)], sem.at[s]).start(priority=k % 2)

    def wait(s):
        for k in range(TOP_K):
            pltpu.make_async_copy(ys_hbm.at[pl.ds(0, tm)], ybuf.at[s, k], sem.at[s]).wait()

    @pl.when(i == 0)
    def _():
        issue(posc_ref, 0)

    wait(slot)

    for s in range(2):
        @pl.when(slot == s)
        def _():
            issue(posn_ref, 1 - s)

    r = route_ref[...]
    g1 = r[:, ROUTE_G1:ROUTE_G1 + 1]
    g2 = r[:, ROUTE_G2:ROUTE_G2 + 1]
    xo = x2_ref[...] + g1 * _unpack_halves(ybuf[slot, 0]) + g2 * _unpack_halves(ybuf[slot, 1])
    o_ref[...] = _rms(xo, g_ref[...])

    @pl.when(i == n - 1)
    def _():
        wait(1 - slot)


def _final(x2, ys, pos, route, g):
    T = x2.shape[0]
    nblk = pos.shape[0]
    tm = T // nblk
    row = lambda w: pl.BlockSpec((tm, w), lambda i: (i, 0))
    return pl.pallas_call(
        _final_kernel,
        grid=(nblk,),
        in_specs=[pl.BlockSpec((1, 1, TOP_K * tm), lambda i: (i, 0, 0), memory_space=pltpu.SMEM),
                  pl.BlockSpec((1, 1, TOP_K * tm), lambda i: (jnp.minimum(i + 1, nblk - 1), 0, 0),
                               memory_space=pltpu.SMEM),
                  row(D_MODEL), row(LANES), pl.BlockSpec(g.shape, lambda i: (0, 0)),
                  pl.BlockSpec(memory_space=pl.ANY)],
        out_specs=row(D_MODEL),
        out_shape=jax.ShapeDtypeStruct((T, D_MODEL), F32),
        scratch_shapes=[pltpu.VMEM((2, TOP_K, tm, HALF), U32), pltpu.SemaphoreType.DMA((2,))],
        compiler_params=pltpu.CompilerParams(dimension_semantics=("arbitrary",), vmem_limit_bytes=VMEM_LIMIT),
        name="final",
    )(pos, pos, x2, route, g, ys)


def _band_bias(table):
    assert WINDOW == BLOCK
    i = jnp.arange(BLOCK)[:, None]
    j = jnp.arange(2 * BLOCK)[None, :]
    n = jnp.maximum(i + BLOCK - j, 0)
    nf = jnp.maximum(n, 1).astype(F32)
    large = MAX_EXACT + (jnp.log(nf / MAX_EXACT) / math.log(MAX_DISTANCE / MAX_EXACT)
                         * (NUM_BUCKETS - MAX_EXACT)).astype(I32)
    large = jnp.minimum(large, NUM_BUCKETS - 1)
    bucket = jnp.where(n < MAX_EXACT, n, large)
    onehot = (bucket[:, :, None] == jnp.arange(NUM_BUCKETS)[None, None, :]).astype(F32)
    bias = jnp.einsum("ijb,bh->hij", onehot, table.astype(F32), precision=lax.Precision.HIGHEST)
    from_prev = (jnp.arange(BLOCK)[None, :] > i)[None]
    prev, cur = bias[:, :, :BLOCK], bias[:, :, BLOCK:]
    return jnp.stack([jnp.where(from_prev, NEG_INF, cur), jnp.where(from_prev, prev, cur)])


def _dispatch_plan(route_rows, counts_f, tmx, n_tiles_max, tm_rows):
    T = route_rows.shape[1]
    experts = jnp.arange(N_EXPERTS, dtype=I32)
    counts = counts_f[:N_EXPERTS, 0].astype(I32)
    ptiles = (counts + tmx - 1) // tmx
    tile_end = jnp.cumsum(ptiles)
    nt = tile_end[-1]
    row_off = (tile_end - ptiles) * tmx

    def slot(e_row, r_row):
        e = route_rows[e_row].astype(I32)
        off = jnp.sum(jnp.where(e[None, :] == experts[:, None], row_off[:, None], 0), axis=0)
        return (off + route_rows[r_row].astype(I32)).reshape(T // tm_rows, 1, tm_rows)

    pos = jnp.concatenate([slot(ROUTE_E1, ROUTE_R1), slot(ROUTE_E2, ROUTE_R2)], axis=2)

    tile_ids = jnp.arange(n_tiles_max, dtype=I32)
    expert_of = lambda t: jnp.sum((tile_end[None, :] <= t[:, None]).astype(I32), axis=1)
    te = expert_of(jnp.minimum(tile_ids, nt - 1))
    partial = jnp.any((tile_ids[:, None] == (tile_end - 1)[None, :]) & (counts % tmx != 0)[None, :], axis=1)
    zfill = (partial | (tile_ids >= nt)).astype(I32)

    used = ptiles > 0
    run_first = (jnp.any((tile_ids[:, None] == (tile_end - ptiles)[None, :]) & used[None, :], axis=1)
                 & (tile_ids < nt)).astype(I32)
    run_slot = (jnp.cumsum(run_first) - 1) % 2
    later_used = used[None, :] & (experts[None, :] > experts[:, None])
    next_of = jnp.min(jnp.where(later_used, experts[None, :], N_EXPERTS), axis=1)
    next_of = jnp.where(next_of < N_EXPERTS, next_of, -1)
    run_next = jnp.sum(jnp.where(te[:, None] == experts[None, :], next_of[None, :], 0), axis=1)
    return pos, te, nt.reshape(1), zfill, run_first, run_slot.astype(I32), run_next.astype(I32)


def kernel(x, mem, rel_bias_table, norm_mix, w_in, attn_sinks, conv_w, conv_b, gate_bias_i, gate_bias_f, mlstm_norm, w_out, norm_cross, norm_mem, w_cq, w_ck, w_cv, w_co, norm_moe, w_router_group, b_router_group, w_router_expert, b_router_expert, w_exp_gate, w_exp_up, w_exp_down, norm_final):
    B, S, _ = x.shape
    T = B * S
    depth = w_in.shape[0]
    x2d = x.reshape(T, D_MODEL)
    mem2d = mem.reshape(B * N_MEM, D_MODEL)
    bias = _band_bias(rel_bias_table)

    tmx = min(TM_EXPERT, T)
    n_tiles_max = (T * TOP_K) // tmx + N_EXPERTS
    tm_rows = min(TM_ROWDMA, T)

    assert depth == 1, "the final combine is fused with the final norm: single layer only"
    l = 0
    gb = jnp.concatenate([gate_bias_i[l], gate_bias_f[l]]).astype(F32)
    gbias_col = jnp.pad(gb, (0, LANES - GATE_ROWS))[None, :]
    qa, kva, qkb, vb, ob, gc, gr = _inproj(x2d, norm_mix[l][None, :], w_in[l], gbias_col, B, S)

    per_seq = lambda a: a.reshape(B, S, a.shape[-1])
    mix = _seqmix(per_seq(qkb), per_seq(vb), per_seq(ob), per_seq(gc), gr, per_seq(qa), per_seq(kva),
                  conv_w[l][:, 0, :].astype(F32), conv_b[l][None, :].astype(F32),
                  mlstm_norm[l][None, :].astype(F32), bias, attn_sinks[l].astype(F32)).reshape(T, W_A_Q + W_B)

    ck, cv = _memkv(mem2d, norm_mem[l][None, :], w_ck[l].astype(BF16), w_cv[l].astype(BF16), B)

    wr = jnp.pad(jnp.concatenate([w_router_expert[l], w_router_group[l]], axis=1),
                 ((0, 0), (0, LANES - N_EXPERTS - N_GROUPS))).astype(BF16)
    br = jnp.pad(jnp.concatenate([b_router_expert[l], b_router_group[l]]),
                 (0, LANES - N_EXPERTS - N_GROUPS)).astype(F32)[None, :]
    x2, hz_packed, route, route_rows, counts = _mid(
        x2d, mix, w_out[l].astype(BF16), norm_cross[l][None, :], w_cq[l].astype(BF16), ck, cv,
        w_co[l].astype(BF16), norm_moe[l][None, :], wr, br, B, S)

    pos, te, nt, zfill, run_first, run_slot, run_next = _dispatch_plan(route_rows, counts, tmx, n_tiles_max,
                                                                       tm_rows)
    xs = _dispatch(hz_packed, pos, zfill, n_tiles_max * tmx, tmx)
    ys = _experts(xs, w_exp_gate[l], w_exp_up[l], w_exp_down[l], te, nt, run_first, run_slot, run_next, tmx)
    out = _final(x2, ys, pos, route, norm_final[None, :])
    return out.reshape(B, S, D_MODEL)
```

```python
import math

import jax
import jax.numpy as jnp
from jax import lax
from jax.experimental import pallas as pl
from jax.experimental.pallas import tpu as pltpu

F32 = jnp.float32
BF16 = jnp.bfloat16
U32 = jnp.uint32
I32 = jnp.int32

D_MODEL = 1024
N_MEM = 256
N_HEADS_A = 8
N_KV_A = 2
HEAD_DIM_A = 64
BLOCK = 128
WINDOW = 128
NUM_BUCKETS = 32
MAX_EXACT = NUM_BUCKETS // 2
MAX_DISTANCE = 128
N_HEADS_B = 4
HEAD_DIM_B = 128
CHUNK = 128
CONV_WIDTH = 4
N_HEADS_X = 4
HEAD_DIM_X = D_MODEL // N_HEADS_X
N_GROUPS = 4
EXPERTS_PER_GROUP = 8
N_EXPERTS = N_GROUPS * EXPERTS_PER_GROUP
TOP_K = 2
D_EXPERT = 512
EPS = 1e-6
NEG_INF = -1e30

W_A_Q = N_HEADS_A * HEAD_DIM_A
W_A_KV = N_KV_A * HEAD_DIM_A
W_B = N_HEADS_B * HEAD_DIM_B
C_QA = 0
C_KVA = C_QA + W_A_Q
C_QKB = C_KVA + 2 * W_A_KV
C_VB = C_QKB + 2 * W_B
C_OB = C_VB + W_B
C_GATE = C_OB + W_B
D_IN = C_GATE + 2 * N_HEADS_B

LANES = 128
SUBLANES = 8
GATE_ROWS = 8
HALF = D_MODEL // 2

TM_INPROJ = 1024
TM_MID = 1024
TM_ROWDMA = 512
TM_EXPERT = 512
EXPERT_ROW_STEP = 128

V7X_VMEM_BYTES = 64 * 1024 * 1024
VMEM_LIMIT = V7X_VMEM_BYTES * 3 // 4


def _rms(xf, g):
    return xf * lax.rsqrt(jnp.mean(xf * xf, axis=-1, keepdims=True) + EPS) * g


def _pack_halves(v):
    b = pltpu.bitcast(v.astype(BF16).astype(F32), U32)
    return (b[:, :HALF] >> 16) | b[:, HALF:]


def _unpack_halves(p):
    lo = pltpu.bitcast(p << 16, F32)
    hi = pltpu.bitcast(p & jnp.uint32(0xFFFF0000), F32)
    return jnp.concatenate([lo, hi], axis=-1)


def _log_sigmoid(z):
    return jnp.minimum(z, 0.0) - jnp.log1p(jnp.exp(-jnp.abs(z)))


def _split3(v):
    hi = v.astype(BF16).astype(F32)
    rest = v - hi
    mid = rest.astype(BF16).astype(F32)
    return hi, mid, (rest - mid).astype(BF16).astype(F32)


def _inproj_kernel(x_ref, g_ref, w32_ref, gbc_ref, qa_ref, kva_ref, qkb_ref, vb_ref, ob_ref, gc_ref, gr_ref,
                   w_ref):
    @pl.when(pl.program_id(0) == 0)
    def _():
        w_ref[:, C_GATE:] = jnp.zeros((D_MODEL, LANES), BF16)
        w_ref[:, :D_IN] = w32_ref[...].astype(BF16)

    tm = x_ref.shape[0]
    h = _rms(x_ref[...], g_ref[...]).astype(BF16)

    def mm(lo, hi):
        return jnp.dot(h, w_ref[:, lo:hi], preferred_element_type=F32)

    qa_ref[...] = mm(C_QA, C_KVA).astype(BF16)
    kva_ref[...] = mm(C_KVA, C_QKB).astype(BF16)
    qkb_ref[...] = mm(C_QKB, C_VB).astype(BF16)
    vb_ref[...] = mm(C_VB, C_OB).astype(BF16)
    ob_ref[...] = mm(C_OB, C_GATE).astype(BF16)

    H, L = N_HEADS_B, CHUNK
    gcol = mm(C_GATE, C_GATE + LANES) + gbc_ref[...]
    grow = jnp.transpose(gcol)[0:GATE_ROWS, :]
    lane_c = lax.broadcasted_iota(I32, (L, LANES), 1)
    is_f_col = (lane_c >= H) & (lane_c < 2 * H)
    is_f_row = lax.broadcasted_iota(I32, (GATE_ROWS, L), 0) >= H
    ti = lax.broadcasted_iota(I32, (L, L), 0)
    si = lax.broadcasted_iota(I32, (L, L), 1)
    tril = jnp.where(si <= ti, 1.0, 0.0).astype(BF16)
    triu = jnp.where(si >= ti, 1.0, 0.0).astype(BF16)
    for c in range(tm // L):
        rows = slice(c * L, (c + 1) * L)
        gcol_c = gcol[rows, :]
        fcol = jnp.where(is_f_col, _log_sigmoid(gcol_c), 0.0)
        parts = jnp.dot(tril, jnp.concatenate(_split3(fcol), axis=1).astype(BF16), preferred_element_type=F32)
        bcol = parts[:, :LANES] + parts[:, LANES:2 * LANES] + parts[:, 2 * LANES:]
        gc_ref[rows, :] = jnp.where(is_f_col, bcol, gcol_c)
        grow_c = grow[:, rows]
        frow = jnp.where(is_f_row, _log_sigmoid(grow_c), 0.0)
        parts = jnp.dot(jnp.concatenate(_split3(frow), axis=0).astype(BF16), triu, preferred_element_type=F32)
        brow = parts[:GATE_ROWS] + parts[GATE_ROWS:2 * GATE_ROWS] + parts[2 * GATE_ROWS:]
        gr_ref[:, rows] = jnp.where(is_f_row, brow, grow_c)


def _inproj(x2d, g, w_in, gbias_col, B, S):
    T = x2d.shape[0]
    tm = min(TM_INPROJ, S)
    tiles_per_seq = S // tm
    row = lambda w: pl.BlockSpec((tm, w), lambda i: (i, 0))
    full = lambda a: pl.BlockSpec(a.shape, lambda i: (0,) * a.ndim)
    return pl.pallas_call(
        _inproj_kernel,
        grid=(T // tm,),
        in_specs=[row(D_MODEL), full(g),
                  pl.BlockSpec(w_in.shape, lambda i: (0, 0), pipeline_mode=pl.Buffered(1)), full(gbias_col)],
        out_specs=[row(W_A_Q), row(2 * W_A_KV), row(2 * W_B), row(W_B), row(W_B), row(LANES),
                   pl.BlockSpec((None, GATE_ROWS, tm), lambda i: (i // tiles_per_seq, 0, i % tiles_per_seq))],
        out_shape=[jax.ShapeDtypeStruct((T, W_A_Q), BF16),
                   jax.ShapeDtypeStruct((T, 2 * W_A_KV), BF16),
                   jax.ShapeDtypeStruct((T, 2 * W_B), BF16),
                   jax.ShapeDtypeStruct((T, W_B), BF16),
                   jax.ShapeDtypeStruct((T, W_B), BF16),
                   jax.ShapeDtypeStruct((T, LANES), F32),
                   jax.ShapeDtypeStruct((B, GATE_ROWS, S), F32)],
        scratch_shapes=[pltpu.VMEM((D_MODEL, C_GATE + LANES), BF16)],
        compiler_params=pltpu.CompilerParams(dimension_semantics=("arbitrary",), vmem_limit_bytes=VMEM_LIMIT),
        name="inproj",
    )(x2d, g, w_in, gbias_col)


def _swa_block(q, kvc, kvp, bias_ref, sink_ref):
    kvp = kvp.astype(F32)
    kvc = kvc.astype(F32)
    kband = jnp.concatenate([kvp[:, :W_A_KV], kvc[:, :W_A_KV]], axis=0)
    vband = jnp.concatenate([kvp[:, W_A_KV:], kvc[:, W_A_KV:]], axis=0)
    lane = lax.broadcasted_iota(I32, (2 * BLOCK, LANES), 1)
    lo = lane < HEAD_DIM_A

    def placements(band):
        swapped = pltpu.roll(band, HEAD_DIM_A, axis=1)
        z = jnp.zeros_like(band)
        return {(0, 0): jnp.where(lo, band, z).astype(BF16), (0, 1): jnp.where(lo, z, swapped).astype(BF16),
                (1, 0): jnp.where(lo, swapped, z).astype(BF16), (1, 1): jnp.where(lo, z, band).astype(BF16)}

    kpl = placements(kband)
    vpl = placements(vband)

    scale = HEAD_DIM_A ** -0.5
    group = N_HEADS_A // N_KV_A
    tiles = []
    from_prev = (lax.broadcasted_iota(I32, (BLOCK, BLOCK), 1) > lax.broadcasted_iota(I32, (BLOCK, BLOCK), 0))

    for pair in range(N_HEADS_A // 2):
        qt = q[:, pair * LANES:(pair + 1) * LANES]
        acc = None
        for half in range(2):
            h = 2 * pair + half
            g = h // group
            s2 = lax.dot_general(qt, kpl[(g, half)], (((1,), (1,)), ((), ())), preferred_element_type=F32)
            s = jnp.where(from_prev, s2[:, :BLOCK], s2[:, BLOCK:])
            s = s * scale + bias_ref[h]
            sink = sink_ref[h]
            m = jnp.maximum(jnp.max(s, axis=-1, keepdims=True), sink)
            p = jnp.exp(s - m)
            denom = jnp.sum(p, axis=-1, keepdims=True) + jnp.exp(sink - m)
            p2 = jnp.concatenate([jnp.where(from_prev, p, 0.0), jnp.where(from_prev, 0.0, p)], axis=1)
            o = jnp.dot(p2.astype(BF16), vpl[(g, half)], preferred_element_type=F32) / denom
            acc = o if acc is None else acc + o
        tiles.append(acc.astype(BF16))
    return tiles


CONV_HALO = 16
SEQ_SUB = 2


def _seqmix_kernel(sink_ref, qkc_ref, qkp_ref, vb_ref, ob_ref, gc_ref, gr_ref, qa_ref, kvc_ref, kvp_ref,
                   cw_ref, cb_ref, nrm_ref, bias0_ref, bias_ref, o_ref, state_ref, m_ref):
    c = pl.program_id(0)
    B = qkc_ref.shape[0]
    H, D, L = N_HEADS_B, HEAD_DIM_B, CHUNK

    @pl.when(c == 0)
    def _():
        state_ref[...] = jnp.zeros_like(state_ref)
        m_ref[...] = jnp.zeros_like(m_ref)

    rr = lax.broadcasted_iota(I32, (L, CONV_HALO + L), 0)
    cc = lax.broadcasted_iota(I32, (L, CONV_HALO + L), 1)
    shifts = {delay: jnp.where(cc == rr + (CONV_HALO - delay), 1.0, 0.0).astype(BF16)
              for delay in range(1, CONV_WIDTH)}
    ti = lax.broadcasted_iota(I32, (L, L), 0)
    si = lax.broadcasted_iota(I32, (L, L), 1)
    tri = si <= ti
    ones_blk = jnp.ones((L, D), BF16)

    def conv_silu(b, u):
        if u == 0:
            prev = qkp_ref[b]
            prev = jnp.where(c > 0, prev, jnp.zeros_like(prev))
        else:
            prev = qkc_ref[b, u * L - CONV_HALO:u * L, :]
        cur = qkc_ref[b, u * L:(u + 1) * L, :]
        ext = jnp.concatenate([prev, cur], axis=0)
        y = cb_ref[...] + cw_ref[CONV_WIDTH - 1:CONV_WIDTH, :] * cur.astype(F32)
        for delay in range(1, CONV_WIDTH):
            tap = CONV_WIDTH - 1 - delay
            y = y + cw_ref[tap:tap + 1, :] * jnp.dot(shifts[delay], ext, preferred_element_type=F32)
        return y * jax.nn.sigmoid(y)

    states = {(b, h): state_ref[b, h] for b in range(B) for h in range(H)}
    ms = {(b, h): m_ref[b, h:h + 1, 0:1] for b in range(B) for h in range(H)}

    n_sub = qkc_ref.shape[1] // L
    for u, b, h in [(u, b, h) for u in range(n_sub) for b in range(B) for h in range(H)]:
        rows = slice(u * L, (u + 1) * L)
        if h == 0:
            kvp = kvp_ref[b] if u == 0 else kvc_ref[b, (u - 1) * L:u * L, :]
            tiles = _swa_block(qa_ref[b, rows, :], kvc_ref[b, rows, :], kvp, bias0_ref if u == 0 else bias_ref,
                               sink_ref)
            for pair, tile in enumerate(tiles):
                o_ref[b, rows, pair * LANES:(pair + 1) * LANES] = tile
            qk = conv_silu(b, u)
            gcol = gc_ref[b, rows, :]
            grow = gr_ref[b, :, rows]
        qh = (qk[:, h * D:(h + 1) * D] * (D ** -0.5)).astype(BF16)
        k_t = qk[:, W_B + h * D:W_B + (h + 1) * D].T
        v1 = jnp.concatenate([vb_ref[b, rows, h * D:(h + 1) * D], ones_blk], axis=-1)
        b_r = grow[H + h:H + h + 1, :]
        g_r = grow[h:h + 1, :] - b_r
        b_c = gcol[:, H + h:H + h + 1]
        m_prev = ms[b, h]
        state = states[b, h]

        gmat = jnp.where(tri, g_r, NEG_INF)
        m_c = jnp.maximum(jnp.max(gmat, axis=-1, keepdims=True), m_prev)
        a_inter = jnp.exp(m_prev - m_c)
        sc = jnp.dot(qh, k_t.astype(BF16), preferred_element_type=F32) * jnp.exp(gmat - m_c)
        tot = (jnp.dot(sc.astype(BF16), v1, preferred_element_type=F32)
               + a_inter * jnp.dot(qh, state.astype(BF16), preferred_element_type=F32))
        num = tot[:, :D]
        den = tot[:, D:]
        hh = num / jnp.maximum(jnp.abs(den), jnp.exp(-(b_c + m_c)))

        b_last = b_r[:, L - 1:L]
        m_new = jnp.maximum(b_last + m_prev, b_last + jnp.max(g_r, axis=-1, keepdims=True))
        w_r = jnp.exp(g_r + (b_last - m_new))
        decay = jnp.exp(b_last + m_prev - m_new)
        upd = jnp.dot((k_t * w_r).astype(BF16), v1, preferred_element_type=F32)
        states[b, h] = decay * state + upd
        ms[b, h] = m_new

        og = jax.nn.sigmoid(ob_ref[b, rows, h * D:(h + 1) * D].astype(F32))
        hb = og * hh
        hb = hb * lax.rsqrt(jnp.mean(hb * hb, axis=-1, keepdims=True) + EPS)
        o_ref[b, rows, W_A_Q + h * D:W_A_Q + (h + 1) * D] = (hb * nrm_ref[:, h * D:(h + 1) * D]).astype(BF16)

    for b, h in states:
        state_ref[b, h] = states[b, h]
        m_ref[b, h:h + 1, :] = jnp.broadcast_to(ms[b, h], (1, LANES))


def _seqmix(qkb, vb, ob, gc, gr, qa, kva, conv_w, conv_b, nrm, bias, sinks):
    assert CHUNK == BLOCK
    B, S, _ = qkb.shape
    rows = min(SEQ_SUB * CHUNK, S)
    halo_per_step = rows // CONV_HALO
    blk = lambda w: pl.BlockSpec((B, rows, w), lambda c: (0, c, 0))
    full = lambda a: pl.BlockSpec(a.shape, lambda c: (0,) * a.ndim)
    bias_variant = lambda pick: pl.BlockSpec((None,) + bias.shape[1:], lambda c: (pick(c), 0, 0, 0))
    return pl.pallas_call(
        _seqmix_kernel,
        grid=(S // rows,),
        in_specs=[pl.BlockSpec(memory_space=pltpu.SMEM),
                  blk(2 * W_B),
                  pl.BlockSpec((B, CONV_HALO, 2 * W_B), lambda c: (0, jnp.maximum(c * halo_per_step - 1, 0), 0)),
                  blk(W_B), blk(W_B), blk(LANES),
                  pl.BlockSpec((B, GATE_ROWS, rows), lambda c: (0, 0, c)),
                  blk(W_A_Q), blk(2 * W_A_KV),
                  pl.BlockSpec((B, BLOCK, 2 * W_A_KV), lambda c: (0, jnp.maximum(c * (rows // BLOCK) - 1, 0), 0)),
                  full(conv_w), full(conv_b), full(nrm),
                  bias_variant(lambda c: jnp.minimum(c, 1)), bias_variant(lambda c: 1)],
        out_specs=blk(W_A_Q + W_B),
        out_shape=jax.ShapeDtypeStruct((B, S, W_A_Q + W_B), BF16),
        scratch_shapes=[pltpu.VMEM((B, N_HEADS_B, HEAD_DIM_B, 2 * HEAD_DIM_B), F32),
                        pltpu.VMEM((B, GATE_ROWS, LANES), F32)],
        compiler_params=pltpu.CompilerParams(dimension_semantics=("arbitrary",), vmem_limit_bytes=VMEM_LIMIT),
        name="seqmix",
    )(sinks, qkb, qkb, vb, ob, gc, gr, qa, kva, kva, conv_w, conv_b, nrm, bias, bias)


def _memkv_kernel(mem_ref, g_ref, wk_ref, wv_ref, k_ref, v_ref):
    hm = _rms(mem_ref[...], g_ref[...]).astype(BF16)
    k_ref[...] = jnp.dot(hm, wk_ref[...], preferred_element_type=F32).astype(BF16)
    v_ref[...] = jnp.dot(hm, wv_ref[...], preferred_element_type=F32).astype(BF16)


def _memkv(mem2d, g, wk, wv, B):
    full = lambda a: pl.BlockSpec(a.shape, lambda b: (0,) * a.ndim)
    blk = pl.BlockSpec((N_MEM, D_MODEL), lambda b: (b, 0))
    return pl.pallas_call(
        _memkv_kernel,
        grid=(B,),
        in_specs=[blk, full(g), full(wk), full(wv)],
        out_specs=[blk, blk],
        out_shape=[jax.ShapeDtypeStruct((B * N_MEM, D_MODEL), BF16)] * 2,
        compiler_params=pltpu.CompilerParams(dimension_semantics=("parallel",), vmem_limit_bytes=VMEM_LIMIT),
        name="memkv",
    )(mem2d, g, wk, wv)


ROUTE_E1, ROUTE_E2, ROUTE_G1, ROUTE_G2, ROUTE_R1, ROUTE_R2 = 0, 1, 2, 3, 4, 5
ROUTER_GROUP_COL = N_EXPERTS
ROUTER_ROWS = -(-(N_EXPERTS + N_GROUPS) // SUBLANES) * SUBLANES


def _mid_kernel(x_ref, mix_ref, wo_ref, gx_ref, wq_ref, ck_ref, cv_ref, wco_ref, gz_ref, wr_ref, br_ref,
                x2_ref, hz_ref, route_ref, rrows_ref, counts_ref, cnt_ref):
    @pl.when(pl.program_id(0) == 0)
    def _():
        cnt_ref[...] = jnp.zeros_like(cnt_ref)

    x1 = x_ref[...] + jnp.dot(mix_ref[...], wo_ref[...], preferred_element_type=F32)

    hc = _rms(x1, gx_ref[...]).astype(BF16)
    cq = jnp.dot(hc, wq_ref[...], preferred_element_type=F32).astype(BF16)
    scale = HEAD_DIM_X ** -0.5
    heads = []
    for h in range(N_HEADS_X):
        sl = slice(h * HEAD_DIM_X, (h + 1) * HEAD_DIM_X)
        s = lax.dot_general(cq[:, sl], ck_ref[:, sl], (((1,), (1,)), ((), ())), preferred_element_type=F32) * scale
        p = jnp.exp(s - jnp.max(s, axis=-1, keepdims=True))
        co = jnp.dot(p.astype(BF16), cv_ref[:, sl], preferred_element_type=F32) / jnp.sum(p, axis=-1, keepdims=True)
        heads.append(co.astype(BF16))
    x2 = x1 + jnp.dot(jnp.concatenate(heads, axis=-1), wco_ref[...], preferred_element_type=F32)
    x2_ref[...] = x2

    hz = _rms(x2, gz_ref[...])
    hz_ref[...] = _pack_halves(hz)
    lg = jnp.dot(hz.astype(BF16), wr_ref[...], preferred_element_type=F32) + br_ref[...]
    tm = lg.shape[0]
    lt = jnp.transpose(lg)[0:ROUTER_ROWS, :]
    row = lax.broadcasted_iota(I32, lt.shape, 0)
    big = jnp.int32(ROUTER_ROWS)
    is_g = (row >= ROUTER_GROUP_COL) & (row < ROUTER_GROUP_COL + N_GROUPS)
    gl = jnp.where(is_g, lt, NEG_INF)
    gmax = jnp.max(gl, axis=0, keepdims=True)
    gsum = jnp.sum(jnp.exp(gl - gmax), axis=0, keepdims=True)
    g_prob = 1.0 / gsum
    g_idx = jnp.min(jnp.where(gl == gmax, row - ROUTER_GROUP_COL, big), axis=0, keepdims=True)
    sel = (row < N_EXPERTS) & ((row // EXPERTS_PER_GROUP) == g_idx)
    el = jnp.where(sel, lt, NEG_INF)
    m1 = jnp.max(el, axis=0, keepdims=True)
    i1 = jnp.min(jnp.where(el == m1, row, big), axis=0, keepdims=True)
    el2 = jnp.where(row == i1, NEG_INF, el)
    m2 = jnp.max(el2, axis=0, keepdims=True)
    i2 = jnp.min(jnp.where(el2 == m2, row, big), axis=0, keepdims=True)
    z = jnp.sum(jnp.exp(el - m1), axis=0, keepdims=True)
    p1 = 1.0 / z
    p2 = jnp.exp(m2 - m1) / z
    g1 = g_prob * (p1 / (p1 + p2))
    g2 = g_prob * (p2 / (p1 + p2))

    used = jnp.where((row == i1) | (row == i2), 1.0, 0.0)
    t_from = lax.broadcasted_iota(I32, (tm, tm), 0)
    t_to = lax.broadcasted_iota(I32, (tm, tm), 1)
    earlier = jnp.where(t_from < t_to, 1.0, 0.0).astype(BF16)
    before = jnp.dot(used.astype(BF16), earlier, preferred_element_type=F32) + cnt_ref[:, 0:1]
    r1 = jnp.sum(jnp.where(row == i1, before, 0.0), axis=0, keepdims=True)
    r2 = jnp.sum(jnp.where(row == i2, before, 0.0), axis=0, keepdims=True)
    cnt_ref[...] = cnt_ref[...] + jnp.sum(used, axis=1, keepdims=True)
    counts_ref[...] = cnt_ref[...]

    rec_row = lax.broadcasted_iota(I32, (SUBLANES, tm), 0)
    rec = jnp.zeros((SUBLANES, tm), F32)
    for c, v in ((ROUTE_E1, i1.astype(F32)), (ROUTE_E2, i2.astype(F32)), (ROUTE_G1, g1), (ROUTE_G2, g2),
                 (ROUTE_R1, r1), (ROUTE_R2, r2)):
        rec = jnp.where(rec_row == c, v, rec)
    rrows_ref[...] = rec
    route_ref[...] = jnp.transpose(jnp.concatenate([rec, jnp.zeros((LANES - SUBLANES, tm), F32)], axis=0))


def _mid(x2d, mix, wo, gx, wq, ck, cv, wco, gz, wr, br, B, S):
    T = B * S
    tm = min(TM_MID, S)
    per_b = S // tm
    row = lambda w: pl.BlockSpec((tm, w), lambda i: (i, 0))
    full = lambda a: pl.BlockSpec(a.shape, lambda i: (0,) * a.ndim)
    kvspec = pl.BlockSpec((N_MEM, D_MODEL), lambda i: (i // per_b, 0))
    return pl.pallas_call(
        _mid_kernel,
        grid=(T // tm,),
        in_specs=[row(D_MODEL), row(W_A_Q + W_B), full(wo), full(gx), full(wq), kvspec, kvspec,
                  full(wco), full(gz), full(wr), full(br)],
        out_specs=[row(D_MODEL), row(HALF), row(LANES), pl.BlockSpec((SUBLANES, tm), lambda i: (0, i)),
                   pl.BlockSpec((ROUTER_ROWS, LANES), lambda i: (0, 0))],
        out_shape=[jax.ShapeDtypeStruct((T, D_MODEL), F32),
                   jax.ShapeDtypeStruct((T, HALF), U32),
                   jax.ShapeDtypeStruct((T, LANES), F32),
                   jax.ShapeDtypeStruct((SUBLANES, T), F32),
                   jax.ShapeDtypeStruct((ROUTER_ROWS, LANES), F32)],
        scratch_shapes=[pltpu.VMEM((ROUTER_ROWS, LANES), F32)],
        compiler_params=pltpu.CompilerParams(dimension_semantics=("arbitrary",), vmem_limit_bytes=VMEM_LIMIT),
        name="mid",
    )(x2d, mix, wo, gx, wq, ck, cv, wco, gz, wr, br)


def _dispatch_kernel(zf_ref, pos_ref, hz_ref, xs_hbm, zbuf, sem, zsem):
    i = pl.program_id(0)
    tm = hz_ref.shape[0]
    zrows = zbuf.shape[0]

    @pl.when(i == 0)
    def _():
        zbuf[...] = jnp.zeros_like(zbuf)

        def fill(t):
            return pltpu.make_async_copy(zbuf, xs_hbm.at[pl.ds(t * zrows, zrows)], zsem)

        def start(t, carry):
            @pl.when(zf_ref[t] != 0)
            def _():
                fill(t).start()
            return carry

        def wait(t, carry):
            @pl.when(zf_ref[t] != 0)
            def _():
                fill(t).wait()
            return carry

        lax.fori_loop(0, zf_ref.shape[0], start, 0)
        lax.fori_loop(0, zf_ref.shape[0], wait, 0)

    for r in range(tm):
        for k in range(TOP_K):
            pltpu.make_async_copy(hz_ref.at[pl.ds(r, 1)], xs_hbm.at[pl.ds(pos_ref[0, 0, k * tm + r], 1)],
                                  sem).start(priority=k % 2)
    for k in range(TOP_K):
        pltpu.make_async_copy(hz_ref, xs_hbm.at[pl.ds(0, tm)], sem).wait()


def _dispatch(hz_packed, pos, zfill, n_slots, tmx):
    T = hz_packed.shape[0]
    tm = pos.shape[2] // TOP_K
    grid_spec = pltpu.PrefetchScalarGridSpec(
        num_scalar_prefetch=1,
        grid=(T // tm,),
        in_specs=[pl.BlockSpec((1, 1, TOP_K * tm), lambda i, zf: (i, 0, 0), memory_space=pltpu.SMEM),
                  pl.BlockSpec((tm, HALF), lambda i, zf: (i, 0))],
        out_specs=pl.BlockSpec(memory_space=pl.ANY),
        scratch_shapes=[pltpu.VMEM((tmx, HALF), U32), pltpu.SemaphoreType.DMA(()), pltpu.SemaphoreType.DMA(())],
    )
    return pl.pallas_call(
        _dispatch_kernel,
        grid_spec=grid_spec,
        out_shape=jax.ShapeDtypeStruct((n_slots, HALF), U32),
        compiler_params=pltpu.CompilerParams(dimension_semantics=("arbitrary",), vmem_limit_bytes=VMEM_LIMIT),
        name="dispatch",
    )(zfill, pos, hz_packed)


def _expert_kernel(te_ref, nt_ref, first_ref, slot_ref, next_ref, rows_ref, xs_ref, wg_hbm, wu_hbm, wd_hbm, ys_ref,
                   wg32, wu32, wd32, wgb, wub, wdb, wsem):
    i = pl.program_id(0)
    nt = nt_ref[0]
    tmx = xs_ref.shape[0]

    def fetch(e, s):
        return [pltpu.make_async_copy(src.at[e], dst.at[s], wsem.at[s])
                for src, dst in ((wg_hbm, wg32), (wu_hbm, wu32), (wd_hbm, wd32))]

    @pl.when(i < nt)
    def _():
        @pl.when(i == 0)
        def _():
            for cp in fetch(te_ref[0], 0):
                cp.start()

        @pl.when(first_ref[i] != 0)
        def _():
            s = slot_ref[i]
            for cp in fetch(te_ref[i], s):
                cp.wait()

            @pl.when(next_ref[i] >= 0)
            def _():
                for cp in fetch(next_ref[i], 1 - s):
                    cp.start()

            wgb[...] = wg32[s].astype(BF16)
            wub[...] = wu32[s].astype(BF16)
            wdb[...] = wd32[s].astype(BF16)

        nrows = rows_ref[i]
        for m in range(EXPERT_ROW_STEP, tmx + 1, EXPERT_ROW_STEP):
            @pl.when((nrows > m - EXPERT_ROW_STEP) & (nrows <= m))
            def _():
                x = _unpack_halves(xs_ref[0:m, :]).astype(BF16)
                hg = jnp.dot(x, wgb[...], preferred_element_type=F32)
                hu = jnp.dot(x, wub[...], preferred_element_type=F32)
                a = (hg * jax.nn.sigmoid(hg) * hu).astype(BF16)
                ys_ref[0:m, :] = _pack_halves(jnp.dot(a, wdb[...], preferred_element_type=F32))
                if m < tmx:
                    ys_ref[m:, :] = jnp.zeros((tmx - m, HALF), U32)

    @pl.when(i >= nt)
    def _():
        ys_ref[...] = jnp.zeros_like(ys_ref)


def _experts(xs, w_gate, w_up, w_down, tile_expert, ntiles, run_first, run_slot, run_next, tile_rows, tmx):
    n_tiles_max = tile_expert.shape[0]
    hbm = pl.BlockSpec(memory_space=pl.ANY)
    grid_spec = pltpu.PrefetchScalarGridSpec(
        num_scalar_prefetch=6,
        grid=(n_tiles_max,),
        in_specs=[pl.BlockSpec((tmx, HALF), lambda i, te, nt, *_: (jnp.minimum(i, nt[0] - 1), 0)), hbm, hbm, hbm],
        out_specs=pl.BlockSpec((tmx, HALF), lambda i, *_: (i, 0)),
        scratch_shapes=[pltpu.VMEM((2, D_MODEL, D_EXPERT), F32),
                        pltpu.VMEM((2, D_MODEL, D_EXPERT), F32),
                        pltpu.VMEM((2, D_EXPERT, D_MODEL), F32),
                        pltpu.VMEM((D_MODEL, D_EXPERT), BF16),
                        pltpu.VMEM((D_MODEL, D_EXPERT), BF16),
                        pltpu.VMEM((D_EXPERT, D_MODEL), BF16),
                        pltpu.SemaphoreType.DMA((2,))],
    )
    return pl.pallas_call(
        _expert_kernel,
        grid_spec=grid_spec,
        out_shape=jax.ShapeDtypeStruct(xs.shape, U32),
        compiler_params=pltpu.CompilerParams(dimension_semantics=("arbitrary",), vmem_limit_bytes=VMEM_LIMIT),
        name="experts",
    )(tile_expert, ntiles, run_first, run_slot, run_next, tile_rows, xs, w_gate, w_up, w_down)


def _final_kernel(posc_ref, posn_ref, x2_ref, route_ref, g_ref, ys_hbm, o_ref, ybuf, sem):
    i = pl.program_id(0)
    n = pl.num_programs(0)
    tm = x2_ref.shape[0]
    slot = i % 2

    def issue(pos_ref, s):
        for r in range(tm):
            for k in range(TOP_K):
                pltpu.make_async_copy(ys_hbm.at[pl.ds(pos_ref[0, 0, k * tm + r], 1)],
                                      ybuf.at[s, k, pl.ds(r, 1)], sem.at[s]).start(priority=k % 2)

    def wait(s):
        for k in range(TOP_K):
            pltpu.make_async_copy(ys_hbm.at[pl.ds(0, tm)], ybuf.at[s, k], sem.at[s]).wait()

    @pl.when(i == 0)
    def _():
        issue(posc_ref, 0)

    wait(slot)

    for s in range(2):
        @pl.when(slot == s)
        def _():
            issue(posn_ref, 1 - s)

    r = route_ref[...]
    g1 = r[:, ROUTE_G1:ROUTE_G1 + 1]
    g2 = r[:, ROUTE_G2:ROUTE_G2 + 1]
    xo = x2_ref[...] + g1 * _unpack_halves(ybuf[slot, 0]) + g2 * _unpack_halves(ybuf[slot, 1])
    o_ref[...] = _rms(xo, g_ref[...])

    @pl.when(i == n - 1)
    def _():
        wait(1 - slot)


def _final(x2, ys, pos, route, g):
    T = x2.shape[0]
    nblk = pos.shape[0]
    tm = T // nblk
    row = lambda w: pl.BlockSpec((tm, w), lambda i: (i, 0))
    return pl.pallas_call(
        _final_kernel,
        grid=(nblk,),
        in_specs=[pl.BlockSpec((1, 1, TOP_K * tm), lambda i: (i, 0, 0), memory_space=pltpu.SMEM),
                  pl.BlockSpec((1, 1, TOP_K * tm), lambda i: (jnp.minimum(i + 1, nblk - 1), 0, 0),
                               memory_space=pltpu.SMEM),
                  row(D_MODEL), row(LANES), pl.BlockSpec(g.shape, lambda i: (0, 0)),
                  pl.BlockSpec(memory_space=pl.ANY)],
        out_specs=row(D_MODEL),
        out_shape=jax.ShapeDtypeStruct((T, D_MODEL), F32),
        scratch_shapes=[pltpu.VMEM((2, TOP_K, tm, HALF), U32), pltpu.SemaphoreType.DMA((2,))],
        compiler_params=pltpu.CompilerParams(dimension_semantics=("arbitrary",), vmem_limit_bytes=VMEM_LIMIT),
        name="final",
    )(pos, pos, x2, route, g, ys)


def _band_bias(table):
    assert WINDOW == BLOCK
    i = jnp.arange(BLOCK)[:, None]
    j = jnp.arange(2 * BLOCK)[None, :]
    n = jnp.maximum(i + BLOCK - j, 0)
    nf = jnp.maximum(n, 1).astype(F32)
    large = MAX_EXACT + (jnp.log(nf / MAX_EXACT) / math.log(MAX_DISTANCE / MAX_EXACT)
                         * (NUM_BUCKETS - MAX_EXACT)).astype(I32)
    large = jnp.minimum(large, NUM_BUCKETS - 1)
    bucket = jnp.where(n < MAX_EXACT, n, large)
    onehot = (bucket[:, :, None] == jnp.arange(NUM_BUCKETS)[None, None, :]).astype(F32)
    bias = jnp.einsum("ijb,bh->hij", onehot, table.astype(F32), precision=lax.Precision.HIGHEST)
    from_prev = (jnp.arange(BLOCK)[None, :] > i)[None]
    prev, cur = bias[:, :, :BLOCK], bias[:, :, BLOCK:]
    return jnp.stack([jnp.where(from_prev, NEG_INF, cur), jnp.where(from_prev, prev, cur)])


def _dispatch_plan(route_rows, counts_f, tmx, n_tiles_max, tm_rows):
    T = route_rows.shape[1]
    experts = jnp.arange(N_EXPERTS, dtype=I32)
    counts = counts_f[:N_EXPERTS, 0].astype(I32)
    ptiles = (counts + tmx - 1) // tmx
    tile_end = jnp.cumsum(ptiles)
    nt = tile_end[-1]
    row_off = (tile_end - ptiles) * tmx

    def slot(e_row, r_row):
        e = route_rows[e_row].astype(I32)
        off = jnp.sum(jnp.where(e[None, :] == experts[:, None], row_off[:, None], 0), axis=0)
        return (off + route_rows[r_row].astype(I32)).reshape(T // tm_rows, 1, tm_rows)

    pos = jnp.concatenate([slot(ROUTE_E1, ROUTE_R1), slot(ROUTE_E2, ROUTE_R2)], axis=2)

    tile_ids = jnp.arange(n_tiles_max, dtype=I32)
    expert_of = lambda t: jnp.sum((tile_end[None, :] <= t[:, None]).astype(I32), axis=1)
    te = expert_of(jnp.minimum(tile_ids, nt - 1))
    partial = jnp.any((tile_ids[:, None] == (tile_end - 1)[None, :]) & (counts % tmx != 0)[None, :], axis=1)
    zfill = (partial | (tile_ids >= nt)).astype(I32)

    used = ptiles > 0
    run_first = (jnp.any((tile_ids[:, None] == (tile_end - ptiles)[None, :]) & used[None, :], axis=1)
                 & (tile_ids < nt)).astype(I32)
    run_slot = (jnp.cumsum(run_first) - 1) % 2
    later_used = used[None, :] & (experts[None, :] > experts[:, None])
    next_of = jnp.min(jnp.where(later_used, experts[None, :], N_EXPERTS), axis=1)
    next_of = jnp.where(next_of < N_EXPERTS, next_of, -1)
    run_next = jnp.sum(jnp.where(te[:, None] == experts[None, :], next_of[None, :], 0), axis=1)

    of_tile = lambda v: jnp.sum(jnp.where(te[:, None] == experts[None, :], v[None, :], 0), axis=1)
    tile_rows = jnp.clip(of_tile(counts) - (tile_ids - of_tile(tile_end - ptiles)) * tmx, 0, tmx)
    tile_rows = jnp.where(tile_ids < nt, tile_rows, 0)
    return (pos, te, nt.reshape(1), zfill, run_first, run_slot.astype(I32), run_next.astype(I32),
            tile_rows.astype(I32))


def kernel(x, mem, rel_bias_table, norm_mix, w_in, attn_sinks, conv_w, conv_b, gate_bias_i, gate_bias_f, mlstm_norm, w_out, norm_cross, norm_mem, w_cq, w_ck, w_cv, w_co, norm_moe, w_router_group, b_router_group, w_router_expert, b_router_expert, w_exp_gate, w_exp_up, w_exp_down, norm_final):
    B, S, _ = x.shape
    T = B * S
    depth = w_in.shape[0]
    x2d = x.reshape(T, D_MODEL)
    mem2d = mem.reshape(B * N_MEM, D_MODEL)
    bias = _band_bias(rel_bias_table)

    tmx = min(TM_EXPERT, T)
    n_tiles_max = (T * TOP_K) // tmx + N_EXPERTS
    tm_rows = min(TM_ROWDMA, T)

    assert depth == 1, "the final combine is fused with the final norm: single layer only"
    l = 0
    gb = jnp.concatenate([gate_bias_i[l], gate_bias_f[l]]).astype(F32)
    gbias_col = jnp.pad(gb, (0, LANES - GATE_ROWS))[None, :]
    qa, kva, qkb, vb, ob, gc, gr = _inproj(x2d, norm_mix[l][None, :], w_in[l], gbias_col, B, S)

    per_seq = lambda a: a.reshape(B, S, a.shape[-1])
    mix = _seqmix(per_seq(qkb), per_seq(vb), per_seq(ob), per_seq(gc), gr, per_seq(qa), per_seq(kva),
                  conv_w[l][:, 0, :].astype(F32), conv_b[l][None, :].astype(F32),
                  mlstm_norm[l][None, :].astype(F32), bias, attn_sinks[l].astype(F32)).reshape(T, W_A_Q + W_B)

    ck, cv = _memkv(mem2d, norm_mem[l][None, :], w_ck[l].astype(BF16), w_cv[l].astype(BF16), B)

    wr = jnp.pad(jnp.concatenate([w_router_expert[l], w_router_group[l]], axis=1),
                 ((0, 0), (0, LANES - N_EXPERTS - N_GROUPS))).astype(BF16)
    br = jnp.pad(jnp.concatenate([b_router_expert[l], b_router_group[l]]),
                 (0, LANES - N_EXPERTS - N_GROUPS)).astype(F32)[None, :]
    x2, hz_packed, route, route_rows, counts = _mid(
        x2d, mix, w_out[l].astype(BF16), norm_cross[l][None, :], w_cq[l].astype(BF16), ck, cv,
        w_co[l].astype(BF16), norm_moe[l][None, :], wr, br, B, S)

    pos, te, nt, zfill, run_first, run_slot, run_next, tile_rows = _dispatch_plan(route_rows, counts, tmx,
                                                                                  n_tiles_max, tm_rows)
    xs = _dispatch(hz_packed, pos, zfill, n_tiles_max * tmx, tmx)
    ys = _experts(xs, w_exp_gate[l], w_exp_up[l], w_exp_down[l], te, nt, run_first, run_slot, run_next, tile_rows,
                  tmx)
    out = _final(x2, ys, pos, route, norm_final[None, :])
    return out.reshape(B, S, D_MODEL)
```

```python
import math

import jax
import jax.numpy as jnp
from jax import lax
from jax.experimental import pallas as pl
from jax.experimental.pallas import tpu as pltpu

F32 = jnp.float32
BF16 = jnp.bfloat16
U32 = jnp.uint32
I32 = jnp.int32

D_MODEL = 1024
N_MEM = 256
N_HEADS_A = 8
N_KV_A = 2
HEAD_DIM_A = 64
BLOCK = 128
WINDOW = 128
NUM_BUCKETS = 32
MAX_EXACT = NUM_BUCKETS // 2
MAX_DISTANCE = 128
N_HEADS_B = 4
HEAD_DIM_B = 128
CHUNK = 128
CONV_WIDTH = 4
N_HEADS_X = 4
HEAD_DIM_X = D_MODEL // N_HEADS_X
N_GROUPS = 4
EXPERTS_PER_GROUP = 8
N_EXPERTS = N_GROUPS * EXPERTS_PER_GROUP
TOP_K = 2
D_EXPERT = 512
EPS = 1e-6
NEG_INF = -1e30

W_A_Q = N_HEADS_A * HEAD_DIM_A
W_A_KV = N_KV_A * HEAD_DIM_A
W_B = N_HEADS_B * HEAD_DIM_B
C_QA = 0
C_KVA = C_QA + W_A_Q
C_QKB = C_KVA + 2 * W_A_KV
C_VB = C_QKB + 2 * W_B
C_OB = C_VB + W_B
C_GATE = C_OB + W_B
D_IN = C_GATE + 2 * N_HEADS_B

LANES = 128
SUBLANES = 8
GATE_ROWS = 8
HALF = D_MODEL // 2

TM_INPROJ = 1024
TM_MID = 1024
TM_ROWDMA = 512
TM_EXPERT = 512
EXPERT_ROW_STEP = 128

V7X_VMEM_BYTES = 64 * 1024 * 1024
VMEM_LIMIT = V7X_VMEM_BYTES * 3 // 4


def _rms(xf, g):
    return xf * lax.rsqrt(jnp.mean(xf * xf, axis=-1, keepdims=True) + EPS) * g


def _pack_halves(v):
    b = pltpu.bitcast(v.astype(BF16).astype(F32), U32)
    return (b[:, :HALF] >> 16) | b[:, HALF:]


def _unpack_halves(p):
    lo = pltpu.bitcast(p << 16, F32)
    hi = pltpu.bitcast(p & jnp.uint32(0xFFFF0000), F32)
    return jnp.concatenate([lo, hi], axis=-1)


def _log_sigmoid(z):
    return jnp.minimum(z, 0.0) - jnp.log1p(jnp.exp(-jnp.abs(z)))


def _split3(v):
    hi = v.astype(BF16).astype(F32)
    rest = v - hi
    mid = rest.astype(BF16).astype(F32)
    return hi, mid, (rest - mid).astype(BF16).astype(F32)


def _inproj_kernel(x_ref, g_ref, w32_ref, gbc_ref, qa_ref, kva_ref, qkb_ref, vb_ref, ob_ref, gc_ref, gr_ref,
                   xs_hbm, w_ref, zbuf, zsem):
    zero_fills = [pltpu.make_async_copy(zbuf, xs_hbm.at[pl.ds(t * zbuf.shape[0], zbuf.shape[0])], zsem)
                  for t in range(xs_hbm.shape[0] // zbuf.shape[0])]

    @pl.when(pl.program_id(0) == 0)
    def _():
        w_ref[:, C_GATE:] = jnp.zeros((D_MODEL, LANES), BF16)
        w_ref[:, :D_IN] = w32_ref[...].astype(BF16)
        zbuf[...] = jnp.zeros_like(zbuf)
        for fill in zero_fills:
            fill.start()

    tm = x_ref.shape[0]
    h = _rms(x_ref[...], g_ref[...]).astype(BF16)

    def mm(lo, hi):
        return jnp.dot(h, w_ref[:, lo:hi], preferred_element_type=F32)

    qa_ref[...] = mm(C_QA, C_KVA).astype(BF16)
    kva_ref[...] = mm(C_KVA, C_QKB).astype(BF16)
    qkb_ref[...] = mm(C_QKB, C_VB).astype(BF16)
    vb_ref[...] = mm(C_VB, C_OB).astype(BF16)
    ob_ref[...] = mm(C_OB, C_GATE).astype(BF16)

    H, L = N_HEADS_B, CHUNK
    gcol = mm(C_GATE, C_GATE + LANES) + gbc_ref[...]
    grow = jnp.transpose(gcol)[0:GATE_ROWS, :]
    lane_c = lax.broadcasted_iota(I32, (L, LANES), 1)
    is_f_col = (lane_c >= H) & (lane_c < 2 * H)
    is_f_row = lax.broadcasted_iota(I32, (GATE_ROWS, L), 0) >= H
    ti = lax.broadcasted_iota(I32, (L, L), 0)
    si = lax.broadcasted_iota(I32, (L, L), 1)
    tril = jnp.where(si <= ti, 1.0, 0.0).astype(BF16)
    triu = jnp.where(si >= ti, 1.0, 0.0).astype(BF16)
    for c in range(tm // L):
        rows = slice(c * L, (c + 1) * L)
        gcol_c = gcol[rows, :]
        fcol = jnp.where(is_f_col, _log_sigmoid(gcol_c), 0.0)
        parts = jnp.dot(tril, jnp.concatenate(_split3(fcol), axis=1).astype(BF16), preferred_element_type=F32)
        bcol = parts[:, :LANES] + parts[:, LANES:2 * LANES] + parts[:, 2 * LANES:]
        gc_ref[rows, :] = jnp.where(is_f_col, bcol, gcol_c)
        grow_c = grow[:, rows]
        frow = jnp.where(is_f_row, _log_sigmoid(grow_c), 0.0)
        parts = jnp.dot(jnp.concatenate(_split3(frow), axis=0).astype(BF16), triu, preferred_element_type=F32)
        brow = parts[:GATE_ROWS] + parts[GATE_ROWS:2 * GATE_ROWS] + parts[2 * GATE_ROWS:]
        gr_ref[:, rows] = jnp.where(is_f_row, brow, grow_c)

    @pl.when(pl.program_id(0) == pl.num_programs(0) - 1)
    def _():
        for fill in zero_fills:
            fill.wait()


def _inproj(x2d, g, w_in, gbias_col, B, S, n_slots, tmx):
    T = x2d.shape[0]
    tm = min(TM_INPROJ, S)
    tiles_per_seq = S // tm
    row = lambda w: pl.BlockSpec((tm, w), lambda i: (i, 0))
    full = lambda a: pl.BlockSpec(a.shape, lambda i: (0,) * a.ndim)
    return pl.pallas_call(
        _inproj_kernel,
        grid=(T // tm,),
        in_specs=[row(D_MODEL), full(g),
                  pl.BlockSpec(w_in.shape, lambda i: (0, 0), pipeline_mode=pl.Buffered(1)), full(gbias_col)],
        out_specs=[row(W_A_Q), row(2 * W_A_KV), row(2 * W_B), row(W_B), row(W_B), row(LANES),
                   pl.BlockSpec((None, GATE_ROWS, tm), lambda i: (i // tiles_per_seq, 0, i % tiles_per_seq)),
                   pl.BlockSpec(memory_space=pl.ANY)],
        out_shape=[jax.ShapeDtypeStruct((T, W_A_Q), BF16),
                   jax.ShapeDtypeStruct((T, 2 * W_A_KV), BF16),
                   jax.ShapeDtypeStruct((T, 2 * W_B), BF16),
                   jax.ShapeDtypeStruct((T, W_B), BF16),
                   jax.ShapeDtypeStruct((T, W_B), BF16),
                   jax.ShapeDtypeStruct((T, LANES), F32),
                   jax.ShapeDtypeStruct((B, GATE_ROWS, S), F32),
                   jax.ShapeDtypeStruct((n_slots, HALF), U32)],
        scratch_shapes=[pltpu.VMEM((D_MODEL, C_GATE + LANES), BF16), pltpu.VMEM((tmx, HALF), U32),
                        pltpu.SemaphoreType.DMA(())],
        compiler_params=pltpu.CompilerParams(dimension_semantics=("arbitrary",), vmem_limit_bytes=VMEM_LIMIT),
        name="inproj",
    )(x2d, g, w_in, gbias_col)


def _swa_block(q, kvc, kvp, bias_ref, sink_ref):
    kvp = kvp.astype(F32)
    kvc = kvc.astype(F32)
    kband = jnp.concatenate([kvp[:, :W_A_KV], kvc[:, :W_A_KV]], axis=0)
    vband = jnp.concatenate([kvp[:, W_A_KV:], kvc[:, W_A_KV:]], axis=0)
    lane = lax.broadcasted_iota(I32, (2 * BLOCK, LANES), 1)
    lo = lane < HEAD_DIM_A

    def placements(band):
        swapped = pltpu.roll(band, HEAD_DIM_A, axis=1)
        z = jnp.zeros_like(band)
        return {(0, 0): jnp.where(lo, band, z).astype(BF16), (0, 1): jnp.where(lo, z, swapped).astype(BF16),
                (1, 0): jnp.where(lo, swapped, z).astype(BF16), (1, 1): jnp.where(lo, z, band).astype(BF16)}

    kpl = placements(kband)
    vpl = placements(vband)

    scale = HEAD_DIM_A ** -0.5
    group = N_HEADS_A // N_KV_A
    tiles = []
    from_prev = (lax.broadcasted_iota(I32, (BLOCK, BLOCK), 1) > lax.broadcasted_iota(I32, (BLOCK, BLOCK), 0))

    for pair in range(N_HEADS_A // 2):
        qt = q[:, pair * LANES:(pair + 1) * LANES]
        acc = None
        for half in range(2):
            h = 2 * pair + half
            g = h // group
            s2 = lax.dot_general(qt, kpl[(g, half)], (((1,), (1,)), ((), ())), preferred_element_type=F32)
            s = jnp.where(from_prev, s2[:, :BLOCK], s2[:, BLOCK:])
            s = s * scale + bias_ref[h]
            sink = sink_ref[h]
            m = jnp.maximum(jnp.max(s, axis=-1, keepdims=True), sink)
            p = jnp.exp(s - m)
            denom = jnp.sum(p, axis=-1, keepdims=True) + jnp.exp(sink - m)
            p2 = jnp.concatenate([jnp.where(from_prev, p, 0.0), jnp.where(from_prev, 0.0, p)], axis=1)
            o = jnp.dot(p2.astype(BF16), vpl[(g, half)], preferred_element_type=F32) / denom
            acc = o if acc is None else acc + o
        tiles.append(acc.astype(BF16))
    return tiles


CONV_HALO = 16
SEQ_SUB = 2


def _seqmix_kernel(sink_ref, qkc_ref, qkp_ref, vb_ref, ob_ref, gc_ref, gr_ref, qa_ref, kvc_ref, kvp_ref,
                   cw_ref, cb_ref, nrm_ref, bias0_ref, bias_ref, o_ref, state_ref, m_ref):
    c = pl.program_id(0)
    B = qkc_ref.shape[0]
    H, D, L = N_HEADS_B, HEAD_DIM_B, CHUNK

    @pl.when(c == 0)
    def _():
        state_ref[...] = jnp.zeros_like(state_ref)
        m_ref[...] = jnp.zeros_like(m_ref)

    rr = lax.broadcasted_iota(I32, (L, CONV_HALO + L), 0)
    cc = lax.broadcasted_iota(I32, (L, CONV_HALO + L), 1)
    shifts = {delay: jnp.where(cc == rr + (CONV_HALO - delay), 1.0, 0.0).astype(BF16)
              for delay in range(1, CONV_WIDTH)}
    ti = lax.broadcasted_iota(I32, (L, L), 0)
    si = lax.broadcasted_iota(I32, (L, L), 1)
    tri = si <= ti
    ones_blk = jnp.ones((L, D), BF16)

    def conv_silu(b, u):
        if u == 0:
            prev = qkp_ref[b]
            prev = jnp.where(c > 0, prev, jnp.zeros_like(prev))
        else:
            prev = qkc_ref[b, u * L - CONV_HALO:u * L, :]
        cur = qkc_ref[b, u * L:(u + 1) * L, :]
        ext = jnp.concatenate([prev, cur], axis=0)
        y = cb_ref[...] + cw_ref[CONV_WIDTH - 1:CONV_WIDTH, :] * cur.astype(F32)
        for delay in range(1, CONV_WIDTH):
            tap = CONV_WIDTH - 1 - delay
            y = y + cw_ref[tap:tap + 1, :] * jnp.dot(shifts[delay], ext, preferred_element_type=F32)
        return y * jax.nn.sigmoid(y)

    states = {(b, h): state_ref[b, h] for b in range(B) for h in range(H)}
    ms = {(b, h): m_ref[b, h:h + 1, 0:1] for b in range(B) for h in range(H)}

    n_sub = qkc_ref.shape[1] // L
    for u, b, h in [(u, b, h) for u in range(n_sub) for b in range(B) for h in range(H)]:
        rows = slice(u * L, (u + 1) * L)
        if h == 0:
            kvp = kvp_ref[b] if u == 0 else kvc_ref[b, (u - 1) * L:u * L, :]
            tiles = _swa_block(qa_ref[b, rows, :], kvc_ref[b, rows, :], kvp, bias0_ref if u == 0 else bias_ref,
                               sink_ref)
            for pair, tile in enumerate(tiles):
                o_ref[b, rows, pair * LANES:(pair + 1) * LANES] = tile
            qk = conv_silu(b, u)
            gcol = gc_ref[b, rows, :]
            grow = gr_ref[b, :, rows]
        qh = (qk[:, h * D:(h + 1) * D] * (D ** -0.5)).astype(BF16)
        k_t = qk[:, W_B + h * D:W_B + (h + 1) * D].T
        v1 = jnp.concatenate([vb_ref[b, rows, h * D:(h + 1) * D], ones_blk], axis=-1)
        b_r = grow[H + h:H + h + 1, :]
        g_r = grow[h:h + 1, :] - b_r
        b_c = gcol[:, H + h:H + h + 1]
        m_prev = ms[b, h]
        state = states[b, h]

        gmat = jnp.where(tri, g_r, NEG_INF)
        m_c = jnp.maximum(jnp.max(gmat, axis=-1, keepdims=True), m_prev)
        a_inter = jnp.exp(m_prev - m_c)
        sc = jnp.dot(qh, k_t.astype(BF16), preferred_element_type=F32) * jnp.exp(gmat - m_c)
        tot = (jnp.dot(sc.astype(BF16), v1, preferred_element_type=F32)
               + a_inter * jnp.dot(qh, state.astype(BF16), preferred_element_type=F32))
        num = tot[:, :D]
        den = tot[:, D:]
        hh = num / jnp.maximum(jnp.abs(den), jnp.exp(-(b_c + m_c)))

        b_last = b_r[:, L - 1:L]
        m_new = jnp.maximum(b_last + m_prev, b_last + jnp.max(g_r, axis=-1, keepdims=True))
        w_r = jnp.exp(g_r + (b_last - m_new))
        decay = jnp.exp(b_last + m_prev - m_new)
        upd = jnp.dot((k_t * w_r).astype(BF16), v1, preferred_element_type=F32)
        states[b, h] = decay * state + upd
        ms[b, h] = m_new

        og = jax.nn.sigmoid(ob_ref[b, rows, h * D:(h + 1) * D].astype(F32))
        hb = og * hh
        hb = hb * lax.rsqrt(jnp.mean(hb * hb, axis=-1, keepdims=True) + EPS)
        o_ref[b, rows, W_A_Q + h * D:W_A_Q + (h + 1) * D] = (hb * nrm_ref[:, h * D:(h + 1) * D]).astype(BF16)

    for b, h in states:
        state_ref[b, h] = states[b, h]
        m_ref[b, h:h + 1, :] = jnp.broadcast_to(ms[b, h], (1, LANES))


def _seqmix(qkb, vb, ob, gc, gr, qa, kva, conv_w, conv_b, nrm, bias, sinks):
    assert CHUNK == BLOCK
    B, S, _ = qkb.shape
    rows = min(SEQ_SUB * CHUNK, S)
    halo_per_step = rows // CONV_HALO
    blk = lambda w: pl.BlockSpec((B, rows, w), lambda c: (0, c, 0))
    full = lambda a: pl.BlockSpec(a.shape, lambda c: (0,) * a.ndim)
    bias_variant = lambda pick: pl.BlockSpec((None,) + bias.shape[1:], lambda c: (pick(c), 0, 0, 0))
    return pl.pallas_call(
        _seqmix_kernel,
        grid=(S // rows,),
        in_specs=[pl.BlockSpec(memory_space=pltpu.SMEM),
                  blk(2 * W_B),
                  pl.BlockSpec((B, CONV_HALO, 2 * W_B), lambda c: (0, jnp.maximum(c * halo_per_step - 1, 0), 0)),
                  blk(W_B), blk(W_B), blk(LANES),
                  pl.BlockSpec((B, GATE_ROWS, rows), lambda c: (0, 0, c)),
                  blk(W_A_Q), blk(2 * W_A_KV),
                  pl.BlockSpec((B, BLOCK, 2 * W_A_KV), lambda c: (0, jnp.maximum(c * (rows // BLOCK) - 1, 0), 0)),
                  full(conv_w), full(conv_b), full(nrm),
                  bias_variant(lambda c: jnp.minimum(c, 1)), bias_variant(lambda c: 1)],
        out_specs=blk(W_A_Q + W_B),
        out_shape=jax.ShapeDtypeStruct((B, S, W_A_Q + W_B), BF16),
        scratch_shapes=[pltpu.VMEM((B, N_HEADS_B, HEAD_DIM_B, 2 * HEAD_DIM_B), F32),
                        pltpu.VMEM((B, GATE_ROWS, LANES), F32)],
        compiler_params=pltpu.CompilerParams(dimension_semantics=("arbitrary",), vmem_limit_bytes=VMEM_LIMIT),
        name="seqmix",
    )(sinks, qkb, qkb, vb, ob, gc, gr, qa, kva, kva, conv_w, conv_b, nrm, bias, bias)


def _memkv_kernel(mem_ref, g_ref, wk_ref, wv_ref, k_ref, v_ref):
    hm = _rms(mem_ref[...], g_ref[...]).astype(BF16)
    k_ref[...] = jnp.dot(hm, wk_ref[...], preferred_element_type=F32).astype(BF16)
    v_ref[...] = jnp.dot(hm, wv_ref[...], preferred_element_type=F32).astype(BF16)


def _memkv(mem2d, g, wk, wv, B):
    full = lambda a: pl.BlockSpec(a.shape, lambda b: (0,) * a.ndim)
    blk = pl.BlockSpec((N_MEM, D_MODEL), lambda b: (b, 0))
    return pl.pallas_call(
        _memkv_kernel,
        grid=(B,),
        in_specs=[blk, full(g), full(wk), full(wv)],
        out_specs=[blk, blk],
        out_shape=[jax.ShapeDtypeStruct((B * N_MEM, D_MODEL), BF16)] * 2,
        compiler_params=pltpu.CompilerParams(dimension_semantics=("parallel",), vmem_limit_bytes=VMEM_LIMIT),
        name="memkv",
    )(mem2d, g, wk, wv)


ROUTE_E1, ROUTE_E2, ROUTE_G1, ROUTE_G2, ROUTE_R1, ROUTE_R2 = 0, 1, 2, 3, 4, 5
ROUTER_GROUP_COL = N_EXPERTS
ROUTER_ROWS = -(-(N_EXPERTS + N_GROUPS) // SUBLANES) * SUBLANES


def _mid_kernel(x_ref, mix_ref, wo_ref, gx_ref, wq_ref, ck_ref, cv_ref, wco_ref, gz_ref, wr_ref, br_ref,
                x2_ref, hz_ref, route_ref, rrows_ref, counts_ref, cnt_ref):
    @pl.when(pl.program_id(0) == 0)
    def _():
        cnt_ref[...] = jnp.zeros_like(cnt_ref)

    x1 = x_ref[...] + jnp.dot(mix_ref[...], wo_ref[...], preferred_element_type=F32)

    hc = _rms(x1, gx_ref[...]).astype(BF16)
    cq = jnp.dot(hc, wq_ref[...], preferred_element_type=F32).astype(BF16)
    scale = HEAD_DIM_X ** -0.5
    heads = []
    for h in range(N_HEADS_X):
        sl = slice(h * HEAD_DIM_X, (h + 1) * HEAD_DIM_X)
        s = lax.dot_general(cq[:, sl], ck_ref[:, sl], (((1,), (1,)), ((), ())), preferred_element_type=F32) * scale
        p = jnp.exp(s - jnp.max(s, axis=-1, keepdims=True))
        co = jnp.dot(p.astype(BF16), cv_ref[:, sl], preferred_element_type=F32) / jnp.sum(p, axis=-1, keepdims=True)
        heads.append(co.astype(BF16))
    x2 = x1 + jnp.dot(jnp.concatenate(heads, axis=-1), wco_ref[...], preferred_element_type=F32)
    x2_ref[...] = x2

    hz = _rms(x2, gz_ref[...])
    hz_ref[...] = _pack_halves(hz)
    lg = jnp.dot(hz.astype(BF16), wr_ref[...], preferred_element_type=F32) + br_ref[...]
    tm = lg.shape[0]
    lt = jnp.transpose(lg)[0:ROUTER_ROWS, :]
    row = lax.broadcasted_iota(I32, lt.shape, 0)
    big = jnp.int32(ROUTER_ROWS)
    is_g = (row >= ROUTER_GROUP_COL) & (row < ROUTER_GROUP_COL + N_GROUPS)
    gl = jnp.where(is_g, lt, NEG_INF)
    gmax = jnp.max(gl, axis=0, keepdims=True)
    gsum = jnp.sum(jnp.exp(gl - gmax), axis=0, keepdims=True)
    g_prob = 1.0 / gsum
    g_idx = jnp.min(jnp.where(gl == gmax, row - ROUTER_GROUP_COL, big), axis=0, keepdims=True)
    sel = (row < N_EXPERTS) & ((row // EXPERTS_PER_GROUP) == g_idx)
    el = jnp.where(sel, lt, NEG_INF)
    m1 = jnp.max(el, axis=0, keepdims=True)
    i1 = jnp.min(jnp.where(el == m1, row, big), axis=0, keepdims=True)
    el2 = jnp.where(row == i1, NEG_INF, el)
    m2 = jnp.max(el2, axis=0, keepdims=True)
    i2 = jnp.min(jnp.where(el2 == m2, row, big), axis=0, keepdims=True)
    z = jnp.sum(jnp.exp(el - m1), axis=0, keepdims=True)
    p1 = 1.0 / z
    p2 = jnp.exp(m2 - m1) / z
    g1 = g_prob * (p1 / (p1 + p2))
    g2 = g_prob * (p2 / (p1 + p2))

    used = jnp.where((row == i1) | (row == i2), 1.0, 0.0)
    t_from = lax.broadcasted_iota(I32, (tm, tm), 0)
    t_to = lax.broadcasted_iota(I32, (tm, tm), 1)
    earlier = jnp.where(t_from < t_to, 1.0, 0.0).astype(BF16)
    before = jnp.dot(used.astype(BF16), earlier, preferred_element_type=F32) + cnt_ref[:, 0:1]
    r1 = jnp.sum(jnp.where(row == i1, before, 0.0), axis=0, keepdims=True)
    r2 = jnp.sum(jnp.where(row == i2, before, 0.0), axis=0, keepdims=True)
    cnt_ref[...] = cnt_ref[...] + jnp.sum(used, axis=1, keepdims=True)
    counts_ref[...] = cnt_ref[...]

    rec_row = lax.broadcasted_iota(I32, (SUBLANES, tm), 0)
    rec = jnp.zeros((SUBLANES, tm), F32)
    for c, v in ((ROUTE_E1, i1.astype(F32)), (ROUTE_E2, i2.astype(F32)), (ROUTE_G1, g1), (ROUTE_G2, g2),
                 (ROUTE_R1, r1), (ROUTE_R2, r2)):
        rec = jnp.where(rec_row == c, v, rec)
    rrows_ref[...] = rec
    route_ref[...] = jnp.transpose(jnp.concatenate([rec, jnp.zeros((LANES - SUBLANES, tm), F32)], axis=0))


def _mid(x2d, mix, wo, gx, wq, ck, cv, wco, gz, wr, br, B, S):
    T = B * S
    tm = min(TM_MID, S)
    per_b = S // tm
    row = lambda w: pl.BlockSpec((tm, w), lambda i: (i, 0))
    full = lambda a: pl.BlockSpec(a.shape, lambda i: (0,) * a.ndim)
    kvspec = pl.BlockSpec((N_MEM, D_MODEL), lambda i: (i // per_b, 0))
    return pl.pallas_call(
        _mid_kernel,
        grid=(T // tm,),
        in_specs=[row(D_MODEL), row(W_A_Q + W_B), full(wo), full(gx), full(wq), kvspec, kvspec,
                  full(wco), full(gz), full(wr), full(br)],
        out_specs=[row(D_MODEL), row(HALF), row(LANES), pl.BlockSpec((SUBLANES, tm), lambda i: (0, i)),
                   pl.BlockSpec((ROUTER_ROWS, LANES), lambda i: (0, 0))],
        out_shape=[jax.ShapeDtypeStruct((T, D_MODEL), F32),
                   jax.ShapeDtypeStruct((T, HALF), U32),
                   jax.ShapeDtypeStruct((T, LANES), F32),
                   jax.ShapeDtypeStruct((SUBLANES, T), F32),
                   jax.ShapeDtypeStruct((ROUTER_ROWS, LANES), F32)],
        scratch_shapes=[pltpu.VMEM((ROUTER_ROWS, LANES), F32)],
        compiler_params=pltpu.CompilerParams(dimension_semantics=("arbitrary",), vmem_limit_bytes=VMEM_LIMIT),
        name="mid",
    )(x2d, mix, wo, gx, wq, ck, cv, wco, gz, wr, br)


def _dispatch_kernel(pos_ref, hz_ref, xs_zeroed, xs_hbm, sem):
    del xs_zeroed
    tm = hz_ref.shape[0]

    for r in range(tm):
        for k in range(TOP_K):
            pltpu.make_async_copy(hz_ref.at[pl.ds(r, 1)], xs_hbm.at[pl.ds(pos_ref[0, 0, k * tm + r], 1)],
                                  sem).start(priority=k % 2)
    for k in range(TOP_K):
        pltpu.make_async_copy(hz_ref, xs_hbm.at[pl.ds(0, tm)], sem).wait()


def _dispatch(hz_packed, pos, xs_zeroed):
    T = hz_packed.shape[0]
    tm = pos.shape[2] // TOP_K
    return pl.pallas_call(
        _dispatch_kernel,
        grid=(T // tm,),
        in_specs=[pl.BlockSpec((1, 1, TOP_K * tm), lambda i: (i, 0, 0), memory_space=pltpu.SMEM),
                  pl.BlockSpec((tm, HALF), lambda i: (i, 0)),
                  pl.BlockSpec(memory_space=pl.ANY)],
        out_specs=pl.BlockSpec(memory_space=pl.ANY),
        out_shape=jax.ShapeDtypeStruct(xs_zeroed.shape, U32),
        scratch_shapes=[pltpu.SemaphoreType.DMA(())],
        input_output_aliases={2: 0},
        compiler_params=pltpu.CompilerParams(dimension_semantics=("arbitrary",), vmem_limit_bytes=VMEM_LIMIT),
        name="dispatch",
    )(pos, hz_packed, xs_zeroed)


def _expert_kernel(te_ref, nt_ref, first_ref, slot_ref, next_ref, rows_ref, xs_ref, wg_hbm, wu_hbm, wd_hbm, ys_ref,
                   wg32, wu32, wd32, wgb, wub, wdb, wsem):
    i = pl.program_id(0)
    nt = nt_ref[0]
    tmx = xs_ref.shape[0]

    def fetch(e, s):
        return [pltpu.make_async_copy(src.at[e], dst.at[s], wsem.at[s])
                for src, dst in ((wg_hbm, wg32), (wu_hbm, wu32), (wd_hbm, wd32))]

    @pl.when(i < nt)
    def _():
        @pl.when(i == 0)
        def _():
            for cp in fetch(te_ref[0], 0):
                cp.start()

        @pl.when(first_ref[i] != 0)
        def _():
            s = slot_ref[i]
            for cp in fetch(te_ref[i], s):
                cp.wait()

            @pl.when(next_ref[i] >= 0)
            def _():
                for cp in fetch(next_ref[i], 1 - s):
                    cp.start()

            wgb[...] = wg32[s].astype(BF16)
            wub[...] = wu32[s].astype(BF16)
            wdb[...] = wd32[s].astype(BF16)

        nrows = rows_ref[i]
        for m in range(EXPERT_ROW_STEP, tmx + 1, EXPERT_ROW_STEP):
            @pl.when((nrows > m - EXPERT_ROW_STEP) & (nrows <= m))
            def _():
                x = _unpack_halves(xs_ref[0:m, :]).astype(BF16)
                hg = jnp.dot(x, wgb[...], preferred_element_type=F32)
                hu = jnp.dot(x, wub[...], preferred_element_type=F32)
                a = (hg * jax.nn.sigmoid(hg) * hu).astype(BF16)
                ys_ref[0:m, :] = _pack_halves(jnp.dot(a, wdb[...], preferred_element_type=F32))
                if m < tmx:
                    ys_ref[m:, :] = jnp.zeros((tmx - m, HALF), U32)

    @pl.when(i >= nt)
    def _():
        ys_ref[...] = jnp.zeros_like(ys_ref)


def _experts(xs, w_gate, w_up, w_down, tile_expert, ntiles, run_first, run_slot, run_next, tile_rows, tmx):
    n_tiles_max = tile_expert.shape[0]
    hbm = pl.BlockSpec(memory_space=pl.ANY)
    grid_spec = pltpu.PrefetchScalarGridSpec(
        num_scalar_prefetch=6,
        grid=(n_tiles_max,),
        in_specs=[pl.BlockSpec((tmx, HALF), lambda i, te, nt, *_: (jnp.minimum(i, nt[0] - 1), 0)), hbm, hbm, hbm],
        out_specs=pl.BlockSpec((tmx, HALF), lambda i, *_: (i, 0)),
        scratch_shapes=[pltpu.VMEM((2, D_MODEL, D_EXPERT), F32),
                        pltpu.VMEM((2, D_MODEL, D_EXPERT), F32),
                        pltpu.VMEM((2, D_EXPERT, D_MODEL), F32),
                        pltpu.VMEM((D_MODEL, D_EXPERT), BF16),
                        pltpu.VMEM((D_MODEL, D_EXPERT), BF16),
                        pltpu.VMEM((D_EXPERT, D_MODEL), BF16),
                        pltpu.SemaphoreType.DMA((2,))],
    )
    return pl.pallas_call(
        _expert_kernel,
        grid_spec=grid_spec,
        out_shape=jax.ShapeDtypeStruct(xs.shape, U32),
        compiler_params=pltpu.CompilerParams(dimension_semantics=("arbitrary",), vmem_limit_bytes=VMEM_LIMIT),
        name="experts",
    )(tile_expert, ntiles, run_first, run_slot, run_next, tile_rows, xs, w_gate, w_up, w_down)


def _final_kernel(posc_ref, posn_ref, x2_ref, route_ref, g_ref, ys_hbm, o_ref, ybuf, sem):
    i = pl.program_id(0)
    n = pl.num_programs(0)
    tm = x2_ref.shape[0]
    slot = i % 2

    def issue(pos_ref, s):
        for r in range(tm):
            for k in range(TOP_K):
                pltpu.make_async_copy(ys_hbm.at[pl.ds(pos_ref[0, 0, k * tm + r], 1)],
                                      ybuf.at[s, k, pl.ds(r, 1)], sem.at[s]).start(priority=k % 2)

    def wait(s):
        for k in range(TOP_K):
            pltpu.make_async_copy(ys_hbm.at[pl.ds(0, tm)], ybuf.at[s, k], sem.at[s]).wait()

    @pl.when(i == 0)
    def _():
        issue(posc_ref, 0)

    wait(slot)

    for s in range(2):
        @pl.when(slot == s)
        def _():
            issue(posn_ref, 1 - s)

    r = route_ref[...]
    g1 = r[:, ROUTE_G1:ROUTE_G1 + 1]
    g2 = r[:, ROUTE_G2:ROUTE_G2 + 1]
    xo = x2_ref[...] + g1 * _unpack_halves(ybuf[slot, 0]) + g2 * _unpack_halves(ybuf[slot, 1])
    o_ref[...] = _rms(xo, g_ref[...])

    @pl.when(i == n - 1)
    def _():
        wait(1 - slot)


def _final(x2, ys, pos, route, g):
    T = x2.shape[0]
    nblk = pos.shape[0]
    tm = T // nblk
    row = lambda w: pl.BlockSpec((tm, w), lambda i: (i, 0))
    return pl.pallas_call(
        _final_kernel,
        grid=(nblk,),
        in_specs=[pl.BlockSpec((1, 1, TOP_K * tm), lambda i: (i, 0, 0), memory_space=pltpu.SMEM),
                  pl.BlockSpec((1, 1, TOP_K * tm), lambda i: (jnp.minimum(i + 1, nblk - 1), 0, 0),
                               memory_space=pltpu.SMEM),
                  row(D_MODEL), row(LANES), pl.BlockSpec(g.shape, lambda i: (0, 0)),
                  pl.BlockSpec(memory_space=pl.ANY)],
        out_specs=row(D_MODEL),
        out_shape=jax.ShapeDtypeStruct((T, D_MODEL), F32),
        scratch_shapes=[pltpu.VMEM((2, TOP_K, tm, HALF), U32), pltpu.SemaphoreType.DMA((2,))],
        compiler_params=pltpu.CompilerParams(dimension_semantics=("arbitrary",), vmem_limit_bytes=VMEM_LIMIT),
        name="final",
    )(pos, pos, x2, route, g, ys)


def _band_bias(table):
    assert WINDOW == BLOCK
    i = jnp.arange(BLOCK)[:, None]
    j = jnp.arange(2 * BLOCK)[None, :]
    n = jnp.maximum(i + BLOCK - j, 0)
    nf = jnp.maximum(n, 1).astype(F32)
    large = MAX_EXACT + (jnp.log(nf / MAX_EXACT) / math.log(MAX_DISTANCE / MAX_EXACT)
                         * (NUM_BUCKETS - MAX_EXACT)).astype(I32)
    large = jnp.minimum(large, NUM_BUCKETS - 1)
    bucket = jnp.where(n < MAX_EXACT, n, large)
    onehot = (bucket[:, :, None] == jnp.arange(NUM_BUCKETS)[None, None, :]).astype(F32)
    bias = jnp.einsum("ijb,bh->hij", onehot, table.astype(F32), precision=lax.Precision.HIGHEST)
    from_prev = (jnp.arange(BLOCK)[None, :] > i)[None]
    prev, cur = bias[:, :, :BLOCK], bias[:, :, BLOCK:]
    return jnp.stack([jnp.where(from_prev, NEG_INF, cur), jnp.where(from_prev, prev, cur)])


def _dispatch_plan(route_rows, counts_f, tmx, n_tiles_max, tm_rows):
    T = route_rows.shape[1]
    experts = jnp.arange(N_EXPERTS, dtype=I32)
    counts = counts_f[:N_EXPERTS, 0].astype(I32)
    ptiles = (counts + tmx - 1) // tmx
    tile_end = jnp.cumsum(ptiles)
    nt = tile_end[-1]
    row_off = (tile_end - ptiles) * tmx

    def slot(e_row, r_row):
        e = route_rows[e_row].astype(I32)
        off = jnp.sum(jnp.where(e[None, :] == experts[:, None], row_off[:, None], 0), axis=0)
        return (off + route_rows[r_row].astype(I32)).reshape(T // tm_rows, 1, tm_rows)

    pos = jnp.concatenate([slot(ROUTE_E1, ROUTE_R1), slot(ROUTE_E2, ROUTE_R2)], axis=2)

    tile_ids = jnp.arange(n_tiles_max, dtype=I32)
    expert_of = lambda t: jnp.sum((tile_end[None, :] <= t[:, None]).astype(I32), axis=1)
    te = expert_of(jnp.minimum(tile_ids, nt - 1))

    used = ptiles > 0
    run_first = (jnp.any((tile_ids[:, None] == (tile_end - ptiles)[None, :]) & used[None, :], axis=1)
                 & (tile_ids < nt)).astype(I32)
    run_slot = (jnp.cumsum(run_first) - 1) % 2
    later_used = used[None, :] & (experts[None, :] > experts[:, None])
    next_of = jnp.min(jnp.where(later_used, experts[None, :], N_EXPERTS), axis=1)
    next_of = jnp.where(next_of < N_EXPERTS, next_of, -1)
    run_next = jnp.sum(jnp.where(te[:, None] == experts[None, :], next_of[None, :], 0), axis=1)

    of_tile = lambda v: jnp.sum(jnp.where(te[:, None] == experts[None, :], v[None, :], 0), axis=1)
    tile_rows = jnp.clip(of_tile(counts) - (tile_ids - of_tile(tile_end - ptiles)) * tmx, 0, tmx)
    tile_rows = jnp.where(tile_ids < nt, tile_rows, 0)
    return pos, te, nt.reshape(1), run_first, run_slot.astype(I32), run_next.astype(I32), tile_rows.astype(I32)


def kernel(x, mem, rel_bias_table, norm_mix, w_in, attn_sinks, conv_w, conv_b, gate_bias_i, gate_bias_f, mlstm_norm, w_out, norm_cross, norm_mem, w_cq, w_ck, w_cv, w_co, norm_moe, w_router_group, b_router_group, w_router_expert, b_router_expert, w_exp_gate, w_exp_up, w_exp_down, norm_final):
    B, S, _ = x.shape
    T = B * S
    depth = w_in.shape[0]
    x2d = x.reshape(T, D_MODEL)
    mem2d = mem.reshape(B * N_MEM, D_MODEL)
    bias = _band_bias(rel_bias_table)

    tmx = min(TM_EXPERT, T)
    n_tiles_max = (T * TOP_K) // tmx + N_EXPERTS
    tm_rows = min(TM_ROWDMA, T)

    assert depth == 1, "the final combine is fused with the final norm: single layer only"
    l = 0
    gb = jnp.concatenate([gate_bias_i[l], gate_bias_f[l]]).astype(F32)
    gbias_col = jnp.pad(gb, (0, LANES - GATE_ROWS))[None, :]
    qa, kva, qkb, vb, ob, gc, gr, xs_zeroed = _inproj(x2d, norm_mix[l][None, :], w_in[l], gbias_col, B, S,
                                                      n_tiles_max * tmx, tmx)

    per_seq = lambda a: a.reshape(B, S, a.shape[-1])
    mix = _seqmix(per_seq(qkb), per_seq(vb), per_seq(ob), per_seq(gc), gr, per_seq(qa), per_seq(kva),
                  conv_w[l][:, 0, :].astype(F32), conv_b[l][None, :].astype(F32),
                  mlstm_norm[l][None, :].astype(F32), bias, attn_sinks[l].astype(F32)).reshape(T, W_A_Q + W_B)

    ck, cv = _memkv(mem2d, norm_mem[l][None, :], w_ck[l].astype(BF16), w_cv[l].astype(BF16), B)

    wr = jnp.pad(jnp.concatenate([w_router_expert[l], w_router_group[l]], axis=1),
                 ((0, 0), (0, LANES - N_EXPERTS - N_GROUPS))).astype(BF16)
    br = jnp.pad(jnp.concatenate([b_router_expert[l], b_router_group[l]]),
                 (0, LANES - N_EXPERTS - N_GROUPS)).astype(F32)[None, :]
    x2, hz_packed, route, route_rows, counts = _mid(
        x2d, mix, w_out[l].astype(BF16), norm_cross[l][None, :], w_cq[l].astype(BF16), ck, cv,
        w_co[l].astype(BF16), norm_moe[l][None, :], wr, br, B, S)

    pos, te, nt, run_first, run_slot, run_next, tile_rows = _dispatch_plan(route_rows, counts, tmx, n_tiles_max,
                                                                           tm_rows)
    xs = _dispatch(hz_packed, pos, xs_zeroed)
    ys = _experts(xs, w_exp_gate[l], w_exp_up[l], w_exp_down[l], te, nt, run_first, run_slot, run_next, tile_rows,
                  tmx)
    out = _final(x2, ys, pos, route, norm_final[None, :])
    return out.reshape(B, S, D_MODEL)
```

```python
import math

import jax
import jax.numpy as jnp
from jax import lax
from jax.experimental import pallas as pl
from jax.experimental.pallas import tpu as pltpu

F32 = jnp.float32
BF16 = jnp.bfloat16
U32 = jnp.uint32
I32 = jnp.int32

D_MODEL = 1024
N_MEM = 256
N_HEADS_A = 8
N_KV_A = 2
HEAD_DIM_A = 64
BLOCK = 128
WINDOW = 128
NUM_BUCKETS = 32
MAX_EXACT = NUM_BUCKETS // 2
MAX_DISTANCE = 128
N_HEADS_B = 4
HEAD_DIM_B = 128
CHUNK = 128
CONV_WIDTH = 4
N_HEADS_X = 4
HEAD_DIM_X = D_MODEL // N_HEADS_X
N_GROUPS = 4
EXPERTS_PER_GROUP = 8
N_EXPERTS = N_GROUPS * EXPERTS_PER_GROUP
TOP_K = 2
D_EXPERT = 512
EPS = 1e-6
NEG_INF = -1e30

W_A_Q = N_HEADS_A * HEAD_DIM_A
W_A_KV = N_KV_A * HEAD_DIM_A
W_B = N_HEADS_B * HEAD_DIM_B
C_QA = 0
C_KVA = C_QA + W_A_Q
C_QKB = C_KVA + 2 * W_A_KV
C_VB = C_QKB + 2 * W_B
C_OB = C_VB + W_B
C_GATE = C_OB + W_B
D_IN = C_GATE + 2 * N_HEADS_B

LANES = 128
SUBLANES = 8
GATE_ROWS = 8
HALF = D_MODEL // 2

TM_INPROJ = 1024
TM_MID = 1024
TM_ROWDMA = 512
TM_EXPERT = 512
EXPERT_ROW_STEP = 128

V7X_VMEM_BYTES = 64 * 1024 * 1024
VMEM_LIMIT = V7X_VMEM_BYTES * 3 // 4


def _rms(xf, g):
    return xf * lax.rsqrt(jnp.mean(xf * xf, axis=-1, keepdims=True) + EPS) * g


def _pack_halves(v):
    b = pltpu.bitcast(v.astype(BF16).astype(F32), U32)
    return (b[:, :HALF] >> 16) | b[:, HALF:]


def _unpack_halves(p):
    lo = pltpu.bitcast(p << 16, F32)
    hi = pltpu.bitcast(p & jnp.uint32(0xFFFF0000), F32)
    return jnp.concatenate([lo, hi], axis=-1)


def _log_sigmoid(z):
    return jnp.minimum(z, 0.0) - jnp.log1p(jnp.exp(-jnp.abs(z)))


def _split3(v):
    hi = v.astype(BF16).astype(F32)
    rest = v - hi
    mid = rest.astype(BF16).astype(F32)
    return hi, mid, (rest - mid).astype(BF16).astype(F32)


def _inproj_kernel(x_ref, g_ref, w32_ref, gbc_ref, qa_ref, kva_ref, qkb_ref, vb_ref, ob_ref, gc_ref, gr_ref,
                   xs_hbm, w_ref, zbuf, zsem):
    zero_fills = [pltpu.make_async_copy(zbuf, xs_hbm.at[pl.ds(t * zbuf.shape[0], zbuf.shape[0])], zsem)
                  for t in range(xs_hbm.shape[0] // zbuf.shape[0])]

    @pl.when(pl.program_id(0) == 0)
    def _():
        w_ref[:, C_GATE:] = jnp.zeros((D_MODEL, LANES), BF16)
        w_ref[:, :D_IN] = w32_ref[...].astype(BF16)
        zbuf[...] = jnp.zeros_like(zbuf)
        for fill in zero_fills:
            fill.start(priority=1)

    tm = x_ref.shape[0]
    h = _rms(x_ref[...], g_ref[...]).astype(BF16)

    def mm(lo, hi):
        return jnp.dot(h, w_ref[:, lo:hi], preferred_element_type=F32)

    qa_ref[...] = mm(C_QA, C_KVA).astype(BF16)
    kva_ref[...] = mm(C_KVA, C_QKB).astype(BF16)
    qkb_ref[...] = mm(C_QKB, C_VB).astype(BF16)
    vb_ref[...] = mm(C_VB, C_OB).astype(BF16)
    ob_ref[...] = mm(C_OB, C_GATE).astype(BF16)

    H, L = N_HEADS_B, CHUNK
    gcol = mm(C_GATE, C_GATE + LANES) + gbc_ref[...]
    grow = jnp.transpose(gcol)[0:GATE_ROWS, :]
    lane_c = lax.broadcasted_iota(I32, (L, LANES), 1)
    is_f_col = (lane_c >= H) & (lane_c < 2 * H)
    is_f_row = lax.broadcasted_iota(I32, (GATE_ROWS, L), 0) >= H
    ti = lax.broadcasted_iota(I32, (L, L), 0)
    si = lax.broadcasted_iota(I32, (L, L), 1)
    tril = jnp.where(si <= ti, 1.0, 0.0).astype(BF16)
    triu = jnp.where(si >= ti, 1.0, 0.0).astype(BF16)
    for c in range(tm // L):
        rows = slice(c * L, (c + 1) * L)
        gcol_c = gcol[rows, :]
        fcol = jnp.where(is_f_col, _log_sigmoid(gcol_c), 0.0)
        parts = jnp.dot(tril, jnp.concatenate(_split3(fcol), axis=1).astype(BF16), preferred_element_type=F32)
        bcol = parts[:, :LANES] + parts[:, LANES:2 * LANES] + parts[:, 2 * LANES:]
        gc_ref[rows, :] = jnp.where(is_f_col, bcol, gcol_c)
        grow_c = grow[:, rows]
        frow = jnp.where(is_f_row, _log_sigmoid(grow_c), 0.0)
        parts = jnp.dot(jnp.concatenate(_split3(frow), axis=0).astype(BF16), triu, preferred_element_type=F32)
        brow = parts[:GATE_ROWS] + parts[GATE_ROWS:2 * GATE_ROWS] + parts[2 * GATE_ROWS:]
        gr_ref[:, rows] = jnp.where(is_f_row, brow, grow_c)

    @pl.when(pl.program_id(0) == pl.num_programs(0) - 1)
    def _():
        for fill in zero_fills:
            fill.wait()


def _inproj(x2d, g, w_in, gbias_col, B, S, n_slots, tmx):
    T = x2d.shape[0]
    tm = min(TM_INPROJ, S)
    tiles_per_seq = S // tm
    row = lambda w: pl.BlockSpec((tm, w), lambda i: (i, 0))
    full = lambda a: pl.BlockSpec(a.shape, lambda i: (0,) * a.ndim)
    return pl.pallas_call(
        _inproj_kernel,
        grid=(T // tm,),
        in_specs=[row(D_MODEL), full(g),
                  pl.BlockSpec(w_in.shape, lambda i: (0, 0), pipeline_mode=pl.Buffered(1)), full(gbias_col)],
        out_specs=[row(W_A_Q), row(2 * W_A_KV), row(2 * W_B), row(W_B), row(W_B), row(LANES),
                   pl.BlockSpec((None, GATE_ROWS, tm), lambda i: (i // tiles_per_seq, 0, i % tiles_per_seq)),
                   pl.BlockSpec(memory_space=pl.ANY)],
        out_shape=[jax.ShapeDtypeStruct((T, W_A_Q), BF16),
                   jax.ShapeDtypeStruct((T, 2 * W_A_KV), BF16),
                   jax.ShapeDtypeStruct((T, 2 * W_B), BF16),
                   jax.ShapeDtypeStruct((T, W_B), BF16),
                   jax.ShapeDtypeStruct((T, W_B), BF16),
                   jax.ShapeDtypeStruct((T, LANES), F32),
                   jax.ShapeDtypeStruct((B, GATE_ROWS, S), F32),
                   jax.ShapeDtypeStruct((n_slots, HALF), U32)],
        scratch_shapes=[pltpu.VMEM((D_MODEL, C_GATE + LANES), BF16), pltpu.VMEM((tmx, HALF), U32),
                        pltpu.SemaphoreType.DMA(())],
        compiler_params=pltpu.CompilerParams(dimension_semantics=("arbitrary",), vmem_limit_bytes=VMEM_LIMIT),
        name="inproj",
    )(x2d, g, w_in, gbias_col)


def _swa_block(q, kvc, kvp, bias_ref, sink_ref):
    kvp = kvp.astype(F32)
    kvc = kvc.astype(F32)
    kband = jnp.concatenate([kvp[:, :W_A_KV], kvc[:, :W_A_KV]], axis=0)
    vband = jnp.concatenate([kvp[:, W_A_KV:], kvc[:, W_A_KV:]], axis=0)
    lane = lax.broadcasted_iota(I32, (2 * BLOCK, LANES), 1)
    lo = lane < HEAD_DIM_A

    def placements(band):
        swapped = pltpu.roll(band, HEAD_DIM_A, axis=1)
        z = jnp.zeros_like(band)
        return {(0, 0): jnp.where(lo, band, z).astype(BF16), (0, 1): jnp.where(lo, z, swapped).astype(BF16),
                (1, 0): jnp.where(lo, swapped, z).astype(BF16), (1, 1): jnp.where(lo, z, band).astype(BF16)}

    kpl = placements(kband)
    vpl = placements(vband)

    scale = HEAD_DIM_A ** -0.5
    group = N_HEADS_A // N_KV_A
    tiles = []
    from_prev = (lax.broadcasted_iota(I32, (BLOCK, BLOCK), 1) > lax.broadcasted_iota(I32, (BLOCK, BLOCK), 0))

    for pair in range(N_HEADS_A // 2):
        qt = q[:, pair * LANES:(pair + 1) * LANES]
        acc = None
        for half in range(2):
            h = 2 * pair + half
            g = h // group
            s2 = lax.dot_general(qt, kpl[(g, half)], (((1,), (1,)), ((), ())), preferred_element_type=F32)
            s = jnp.where(from_prev, s2[:, :BLOCK], s2[:, BLOCK:])
            s = s * scale + bias_ref[h]
            sink = sink_ref[h]
            m = jnp.maximum(jnp.max(s, axis=-1, keepdims=True), sink)
            p = jnp.exp(s - m)
            denom = jnp.sum(p, axis=-1, keepdims=True) + jnp.exp(sink - m)
            p2 = jnp.concatenate([jnp.where(from_prev, p, 0.0), jnp.where(from_prev, 0.0, p)], axis=1)
            o = jnp.dot(p2.astype(BF16), vpl[(g, half)], preferred_element_type=F32) / denom
            acc = o if acc is None else acc + o
        tiles.append(acc.astype(BF16))
    return tiles


CONV_HALO = 16
SEQ_SUB = 2


def _seqmix_kernel(sink_ref, qkc_ref, qkp_ref, vb_ref, ob_ref, gc_ref, gr_ref, qa_ref, kvc_ref, kvp_ref,
                   cw_ref, cb_ref, nrm_ref, bias0_ref, bias_ref, o_ref, state_ref, m_ref):
    c = pl.program_id(0)
    B = qkc_ref.shape[0]
    H, D, L = N_HEADS_B, HEAD_DIM_B, CHUNK

    @pl.when(c == 0)
    def _():
        state_ref[...] = jnp.zeros_like(state_ref)
        m_ref[...] = jnp.zeros_like(m_ref)

    rr = lax.broadcasted_iota(I32, (L, CONV_HALO + L), 0)
    cc = lax.broadcasted_iota(I32, (L, CONV_HALO + L), 1)
    shifts = {delay: jnp.where(cc == rr + (CONV_HALO - delay), 1.0, 0.0).astype(BF16)
              for delay in range(1, CONV_WIDTH)}
    ti = lax.broadcasted_iota(I32, (L, L), 0)
    si = lax.broadcasted_iota(I32, (L, L), 1)
    tri = si <= ti
    ones_blk = jnp.ones((L, D), BF16)

    def conv_silu(b, u):
        if u == 0:
            prev = qkp_ref[b]
            prev = jnp.where(c > 0, prev, jnp.zeros_like(prev))
        else:
            prev = qkc_ref[b, u * L - CONV_HALO:u * L, :]
        cur = qkc_ref[b, u * L:(u + 1) * L, :]
        ext = jnp.concatenate([prev, cur], axis=0)
        y = cb_ref[...] + cw_ref[CONV_WIDTH - 1:CONV_WIDTH, :] * cur.astype(F32)
        for delay in range(1, CONV_WIDTH):
            tap = CONV_WIDTH - 1 - delay
            y = y + cw_ref[tap:tap + 1, :] * jnp.dot(shifts[delay], ext, preferred_element_type=F32)
        return y * jax.nn.sigmoid(y)

    states = {(b, h): state_ref[b, h] for b in range(B) for h in range(H)}
    ms = {(b, h): m_ref[b, h:h + 1, 0:1] for b in range(B) for h in range(H)}

    n_sub = qkc_ref.shape[1] // L
    for u, b, h in [(u, b, h) for u in range(n_sub) for b in range(B) for h in range(H)]:
        rows = slice(u * L, (u + 1) * L)
        if h == 0:
            kvp = kvp_ref[b] if u == 0 else kvc_ref[b, (u - 1) * L:u * L, :]
            tiles = _swa_block(qa_ref[b, rows, :], kvc_ref[b, rows, :], kvp, bias0_ref if u == 0 else bias_ref,
                               sink_ref)
            for pair, tile in enumerate(tiles):
                o_ref[b, rows, pair * LANES:(pair + 1) * LANES] = tile
            qk = conv_silu(b, u)
            gcol = gc_ref[b, rows, :]
            grow = gr_ref[b, :, rows]
        qh = (qk[:, h * D:(h + 1) * D] * (D ** -0.5)).astype(BF16)
        k_t = qk[:, W_B + h * D:W_B + (h + 1) * D].T
        v1 = jnp.concatenate([vb_ref[b, rows, h * D:(h + 1) * D], ones_blk], axis=-1)
        b_r = grow[H + h:H + h + 1, :]
        g_r = grow[h:h + 1, :] - b_r
        b_c = gcol[:, H + h:H + h + 1]
        m_prev = ms[b, h]
        state = states[b, h]

        gmat = jnp.where(tri, g_r, NEG_INF)
        m_c = jnp.maximum(jnp.max(gmat, axis=-1, keepdims=True), m_prev)
        a_inter = jnp.exp(m_prev - m_c)
        sc = jnp.dot(qh, k_t.astype(BF16), preferred_element_type=F32) * jnp.exp(gmat - m_c)
        tot = (jnp.dot(sc.astype(BF16), v1, preferred_element_type=F32)
               + a_inter * jnp.dot(qh, state.astype(BF16), preferred_element_type=F32))
        num = tot[:, :D]
        den = tot[:, D:]
        hh = num / jnp.maximum(jnp.abs(den), jnp.exp(-(b_c + m_c)))

        b_last = b_r[:, L - 1:L]
        m_new = jnp.maximum(b_last + m_prev, b_last + jnp.max(g_r, axis=-1, keepdims=True))
        w_r = jnp.exp(g_r + (b_last - m_new))
        decay = jnp.exp(b_last + m_prev - m_new)
        upd = jnp.dot((k_t * w_r).astype(BF16), v1, preferred_element_type=F32)
        states[b, h] = decay * state + upd
        ms[b, h] = m_new

        og = jax.nn.sigmoid(ob_ref[b, rows, h * D:(h + 1) * D].astype(F32))
        hb = og * hh
        hb = hb * lax.rsqrt(jnp.mean(hb * hb, axis=-1, keepdims=True) + EPS)
        o_ref[b, rows, W_A_Q + h * D:W_A_Q + (h + 1) * D] = (hb * nrm_ref[:, h * D:(h + 1) * D]).astype(BF16)

    for b, h in states:
        state_ref[b, h] = states[b, h]
        m_ref[b, h:h + 1, :] = jnp.broadcast_to(ms[b, h], (1, LANES))


def _seqmix(qkb, vb, ob, gc, gr, qa, kva, conv_w, conv_b, nrm, bias, sinks):
    assert CHUNK == BLOCK
    B, S, _ = qkb.shape
    rows = min(SEQ_SUB * CHUNK, S)
    halo_per_step = rows // CONV_HALO
    blk = lambda w: pl.BlockSpec((B, rows, w), lambda c: (0, c, 0))
    full = lambda a: pl.BlockSpec(a.shape, lambda c: (0,) * a.ndim)
    bias_variant = lambda pick: pl.BlockSpec((None,) + bias.shape[1:], lambda c: (pick(c), 0, 0, 0))
    return pl.pallas_call(
        _seqmix_kernel,
        grid=(S // rows,),
        in_specs=[pl.BlockSpec(memory_space=pltpu.SMEM),
                  blk(2 * W_B),
                  pl.BlockSpec((B, CONV_HALO, 2 * W_B), lambda c: (0, jnp.maximum(c * halo_per_step - 1, 0), 0)),
                  blk(W_B), blk(W_B), blk(LANES),
                  pl.BlockSpec((B, GATE_ROWS, rows), lambda c: (0, 0, c)),
                  blk(W_A_Q), blk(2 * W_A_KV),
                  pl.BlockSpec((B, BLOCK, 2 * W_A_KV), lambda c: (0, jnp.maximum(c * (rows // BLOCK) - 1, 0), 0)),
                  full(conv_w), full(conv_b), full(nrm),
                  bias_variant(lambda c: jnp.minimum(c, 1)), bias_variant(lambda c: 1)],
        out_specs=blk(W_A_Q + W_B),
        out_shape=jax.ShapeDtypeStruct((B, S, W_A_Q + W_B), BF16),
        scratch_shapes=[pltpu.VMEM((B, N_HEADS_B, HEAD_DIM_B, 2 * HEAD_DIM_B), F32),
                        pltpu.VMEM((B, GATE_ROWS, LANES), F32)],
        compiler_params=pltpu.CompilerParams(dimension_semantics=("arbitrary",), vmem_limit_bytes=VMEM_LIMIT),
        name="seqmix",
    )(sinks, qkb, qkb, vb, ob, gc, gr, qa, kva, kva, conv_w, conv_b, nrm, bias, bias)


def _memkv_kernel(mem_ref, g_ref, wk_ref, wv_ref, k_ref, v_ref):
    hm = _rms(mem_ref[...], g_ref[...]).astype(BF16)
    k_ref[...] = jnp.dot(hm, wk_ref[...], preferred_element_type=F32).astype(BF16)
    v_ref[...] = jnp.dot(hm, wv_ref[...], preferred_element_type=F32).astype(BF16)


def _memkv(mem2d, g, wk, wv, B):
    full = lambda a: pl.BlockSpec(a.shape, lambda b: (0,) * a.ndim)
    blk = pl.BlockSpec((N_MEM, D_MODEL), lambda b: (b, 0))
    return pl.pallas_call(
        _memkv_kernel,
        grid=(B,),
        in_specs=[blk, full(g), full(wk), full(wv)],
        out_specs=[blk, blk],
        out_shape=[jax.ShapeDtypeStruct((B * N_MEM, D_MODEL), BF16)] * 2,
        compiler_params=pltpu.CompilerParams(dimension_semantics=("parallel",), vmem_limit_bytes=VMEM_LIMIT),
        name="memkv",
    )(mem2d, g, wk, wv)


ROUTE_E1, ROUTE_E2, ROUTE_G1, ROUTE_G2, ROUTE_R1, ROUTE_R2 = 0, 1, 2, 3, 4, 5
ROUTER_GROUP_COL = N_EXPERTS
ROUTER_ROWS = -(-(N_EXPERTS + N_GROUPS) // SUBLANES) * SUBLANES


def _mid_kernel(x_ref, mix_ref, wo_ref, gx_ref, wq_ref, ck_ref, cv_ref, wco_ref, gz_ref, wr_ref, br_ref,
                x2_ref, hz_ref, route_ref, rrows_ref, counts_ref, cnt_ref):
    @pl.when(pl.program_id(0) == 0)
    def _():
        cnt_ref[...] = jnp.zeros_like(cnt_ref)

    x1 = x_ref[...] + jnp.dot(mix_ref[...], wo_ref[...], preferred_element_type=F32)

    hc = _rms(x1, gx_ref[...]).astype(BF16)
    cq = jnp.dot(hc, wq_ref[...], preferred_element_type=F32).astype(BF16)
    scale = HEAD_DIM_X ** -0.5
    heads = []
    for h in range(N_HEADS_X):
        sl = slice(h * HEAD_DIM_X, (h + 1) * HEAD_DIM_X)
        s = lax.dot_general(cq[:, sl], ck_ref[:, sl], (((1,), (1,)), ((), ())), preferred_element_type=F32) * scale
        p = jnp.exp(s - jnp.max(s, axis=-1, keepdims=True))
        co = jnp.dot(p.astype(BF16), cv_ref[:, sl], preferred_element_type=F32) / jnp.sum(p, axis=-1, keepdims=True)
        heads.append(co.astype(BF16))
    x2 = x1 + jnp.dot(jnp.concatenate(heads, axis=-1), wco_ref[...], preferred_element_type=F32)
    x2_ref[...] = x2

    hz = _rms(x2, gz_ref[...])
    hz_ref[...] = _pack_halves(hz)
    lg = jnp.dot(hz.astype(BF16), wr_ref[...], preferred_element_type=F32) + br_ref[...]
    tm = lg.shape[0]
    lt = jnp.transpose(lg)[0:ROUTER_ROWS, :]
    row = lax.broadcasted_iota(I32, lt.shape, 0)
    big = jnp.int32(ROUTER_ROWS)
    is_g = (row >= ROUTER_GROUP_COL) & (row < ROUTER_GROUP_COL + N_GROUPS)
    gl = jnp.where(is_g, lt, NEG_INF)
    gmax = jnp.max(gl, axis=0, keepdims=True)
    gsum = jnp.sum(jnp.exp(gl - gmax), axis=0, keepdims=True)
    g_prob = 1.0 / gsum
    g_idx = jnp.min(jnp.where(gl == gmax, row - ROUTER_GROUP_COL, big), axis=0, keepdims=True)
    sel = (row < N_EXPERTS) & ((row // EXPERTS_PER_GROUP) == g_idx)
    el = jnp.where(sel, lt, NEG_INF)
    m1 = jnp.max(el, axis=0, keepdims=True)
    i1 = jnp.min(jnp.where(el == m1, row, big), axis=0, keepdims=True)
    el2 = jnp.where(row == i1, NEG_INF, el)
    m2 = jnp.max(el2, axis=0, keepdims=True)
    i2 = jnp.min(jnp.where(el2 == m2, row, big), axis=0, keepdims=True)
    z = jnp.sum(jnp.exp(el - m1), axis=0, keepdims=True)
    p1 = 1.0 / z
    p2 = jnp.exp(m2 - m1) / z
    g1 = g_prob * (p1 / (p1 + p2))
    g2 = g_prob * (p2 / (p1 + p2))

    used = jnp.where((row == i1) | (row == i2), 1.0, 0.0)
    t_from = lax.broadcasted_iota(I32, (tm, tm), 0)
    t_to = lax.broadcasted_iota(I32, (tm, tm), 1)
    earlier = jnp.where(t_from < t_to, 1.0, 0.0).astype(BF16)
    before = jnp.dot(used.astype(BF16), earlier, preferred_element_type=F32) + cnt_ref[:, 0:1]
    r1 = jnp.sum(jnp.where(row == i1, before, 0.0), axis=0, keepdims=True)
    r2 = jnp.sum(jnp.where(row == i2, before, 0.0), axis=0, keepdims=True)
    cnt_ref[...] = cnt_ref[...] + jnp.sum(used, axis=1, keepdims=True)
    counts_ref[...] = cnt_ref[...]

    rec_row = lax.broadcasted_iota(I32, (SUBLANES, tm), 0)
    rec = jnp.zeros((SUBLANES, tm), F32)
    for c, v in ((ROUTE_E1, i1.astype(F32)), (ROUTE_E2, i2.astype(F32)), (ROUTE_G1, g1), (ROUTE_G2, g2),
                 (ROUTE_R1, r1), (ROUTE_R2, r2)):
        rec = jnp.where(rec_row == c, v, rec)
    rrows_ref[...] = rec
    route_ref[...] = jnp.transpose(jnp.concatenate([rec, jnp.zeros((LANES - SUBLANES, tm), F32)], axis=0))


def _mid(x2d, mix, wo, gx, wq, ck, cv, wco, gz, wr, br, B, S):
    T = B * S
    tm = min(TM_MID, S)
    per_b = S // tm
    row = lambda w: pl.BlockSpec((tm, w), lambda i: (i, 0))
    full = lambda a: pl.BlockSpec(a.shape, lambda i: (0,) * a.ndim)
    kvspec = pl.BlockSpec((N_MEM, D_MODEL), lambda i: (i // per_b, 0))
    return pl.pallas_call(
        _mid_kernel,
        grid=(T // tm,),
        in_specs=[row(D_MODEL), row(W_A_Q + W_B), full(wo), full(gx), full(wq), kvspec, kvspec,
                  full(wco), full(gz), full(wr), full(br)],
        out_specs=[row(D_MODEL), row(HALF), row(LANES), pl.BlockSpec((SUBLANES, tm), lambda i: (0, i)),
                   pl.BlockSpec((ROUTER_ROWS, LANES), lambda i: (0, 0))],
        out_shape=[jax.ShapeDtypeStruct((T, D_MODEL), F32),
                   jax.ShapeDtypeStruct((T, HALF), U32),
                   jax.ShapeDtypeStruct((T, LANES), F32),
                   jax.ShapeDtypeStruct((SUBLANES, T), F32),
                   jax.ShapeDtypeStruct((ROUTER_ROWS, LANES), F32)],
        scratch_shapes=[pltpu.VMEM((ROUTER_ROWS, LANES), F32)],
        compiler_params=pltpu.CompilerParams(dimension_semantics=("arbitrary",), vmem_limit_bytes=VMEM_LIMIT),
        name="mid",
    )(x2d, mix, wo, gx, wq, ck, cv, wco, gz, wr, br)


def _dispatch_kernel(pos_ref, hz_ref, xs_zeroed, xs_hbm, sem):
    del xs_zeroed
    tm = hz_ref.shape[0]

    for r in range(tm):
        for k in range(TOP_K):
            pltpu.make_async_copy(hz_ref.at[pl.ds(r, 1)], xs_hbm.at[pl.ds(pos_ref[0, 0, k * tm + r], 1)],
                                  sem).start(priority=k % 2)
    for k in range(TOP_K):
        pltpu.make_async_copy(hz_ref, xs_hbm.at[pl.ds(0, tm)], sem).wait()


def _dispatch(hz_packed, pos, xs_zeroed):
    T = hz_packed.shape[0]
    tm = pos.shape[2] // TOP_K
    return pl.pallas_call(
        _dispatch_kernel,
        grid=(T // tm,),
        in_specs=[pl.BlockSpec((1, 1, TOP_K * tm), lambda i: (i, 0, 0), memory_space=pltpu.SMEM),
                  pl.BlockSpec((tm, HALF), lambda i: (i, 0)),
                  pl.BlockSpec(memory_space=pl.ANY)],
        out_specs=pl.BlockSpec(memory_space=pl.ANY),
        out_shape=jax.ShapeDtypeStruct(xs_zeroed.shape, U32),
        scratch_shapes=[pltpu.SemaphoreType.DMA(())],
        input_output_aliases={2: 0},
        compiler_params=pltpu.CompilerParams(dimension_semantics=("arbitrary",), vmem_limit_bytes=VMEM_LIMIT),
        name="dispatch",
    )(pos, hz_packed, xs_zeroed)


def _expert_kernel(te_ref, nt_ref, first_ref, slot_ref, next_ref, rows_ref, xs_ref, wg_hbm, wu_hbm, wd_hbm, ys_ref,
                   wg32, wu32, wd32, wgb, wub, wdb, wsem):
    i = pl.program_id(0)
    nt = nt_ref[0]
    tmx = xs_ref.shape[0]

    def fetch(e, s):
        return [pltpu.make_async_copy(src.at[e], dst.at[s], wsem.at[s])
                for src, dst in ((wg_hbm, wg32), (wu_hbm, wu32), (wd_hbm, wd32))]

    @pl.when(i < nt)
    def _():
        @pl.when(i == 0)
        def _():
            for cp in fetch(te_ref[0], 0):
                cp.start()

        @pl.when(first_ref[i] != 0)
        def _():
            s = slot_ref[i]
            for cp in fetch(te_ref[i], s):
                cp.wait()

            @pl.when(next_ref[i] >= 0)
            def _():
                for cp in fetch(next_ref[i], 1 - s):
                    cp.start()

            wgb[...] = wg32[s].astype(BF16)
            wub[...] = wu32[s].astype(BF16)
            wdb[...] = wd32[s].astype(BF16)

        nrows = rows_ref[i]
        for m in range(EXPERT_ROW_STEP, tmx + 1, EXPERT_ROW_STEP):
            @pl.when((nrows > m - EXPERT_ROW_STEP) & (nrows <= m))
            def _():
                x = _unpack_halves(xs_ref[0:m, :]).astype(BF16)
                hg = jnp.dot(x, wgb[...], preferred_element_type=F32)
                hu = jnp.dot(x, wub[...], preferred_element_type=F32)
                a = (hg * jax.nn.sigmoid(hg) * hu).astype(BF16)
                ys_ref[0:m, :] = _pack_halves(jnp.dot(a, wdb[...], preferred_element_type=F32))
                if m < tmx:
                    ys_ref[m:, :] = jnp.zeros((tmx - m, HALF), U32)

    @pl.when(i >= nt)
    def _():
        ys_ref[...] = jnp.zeros_like(ys_ref)


def _experts(xs, w_gate, w_up, w_down, tile_expert, ntiles, run_first, run_slot, run_next, tile_rows, tmx):
    n_tiles_max = tile_expert.shape[0]
    hbm = pl.BlockSpec(memory_space=pl.ANY)
    grid_spec = pltpu.PrefetchScalarGridSpec(
        num_scalar_prefetch=6,
        grid=(n_tiles_max,),
        in_specs=[pl.BlockSpec((tmx, HALF), lambda i, te, nt, *_: (jnp.minimum(i, nt[0] - 1), 0)), hbm, hbm, hbm],
        out_specs=pl.BlockSpec((tmx, HALF), lambda i, *_: (i, 0)),
        scratch_shapes=[pltpu.VMEM((2, D_MODEL, D_EXPERT), F32),
                        pltpu.VMEM((2, D_MODEL, D_EXPERT), F32),
                        pltpu.VMEM((2, D_EXPERT, D_MODEL), F32),
                        pltpu.VMEM((D_MODEL, D_EXPERT), BF16),
                        pltpu.VMEM((D_MODEL, D_EXPERT), BF16),
                        pltpu.VMEM((D_EXPERT, D_MODEL), BF16),
                        pltpu.SemaphoreType.DMA((2,))],
    )
    return pl.pallas_call(
        _expert_kernel,
        grid_spec=grid_spec,
        out_shape=jax.ShapeDtypeStruct(xs.shape, U32),
        compiler_params=pltpu.CompilerParams(dimension_semantics=("arbitrary",), vmem_limit_bytes=VMEM_LIMIT),
        name="experts",
    )(tile_expert, ntiles, run_first, run_slot, run_next, tile_rows, xs, w_gate, w_up, w_down)


def _final_kernel(posc_ref, posn_ref, x2_ref, route_ref, g_ref, ys_hbm, o_ref, ybuf, sem):
    i = pl.program_id(0)
    n = pl.num_programs(0)
    tm = x2_ref.shape[0]
    slot = i % 2

    def issue(pos_ref, s):
        for r in range(tm):
            for k in range(TOP_K):
                pltpu.make_async_copy(ys_hbm.at[pl.ds(pos_ref[0, 0, k * tm + r], 1)],
                                      ybuf.at[s, k, pl.ds(r, 1)], sem.at[s]).start(priority=k % 2)

    def wait(s):
        for k in range(TOP_K):
            pltpu.make_async_copy(ys_hbm.at[pl.ds(0, tm)], ybuf.at[s, k], sem.at[s]).wait()

    @pl.when(i == 0)
    def _():
        issue(posc_ref, 0)

    wait(slot)

    for s in range(2):
        @pl.when(slot == s)
        def _():
            issue(posn_ref, 1 - s)

    r = route_ref[...]
    g1 = r[:, ROUTE_G1:ROUTE_G1 + 1]
    g2 = r[:, ROUTE_G2:ROUTE_G2 + 1]
    xo = x2_ref[...] + g1 * _unpack_halves(ybuf[slot, 0]) + g2 * _unpack_halves(ybuf[slot, 1])
    o_ref[...] = _rms(xo, g_ref[...])

    @pl.when(i == n - 1)
    def _():
        wait(1 - slot)


def _final(x2, ys, pos, route, g):
    T = x2.shape[0]
    nblk = pos.shape[0]
    tm = T // nblk
    row = lambda w: pl.BlockSpec((tm, w), lambda i: (i, 0))
    return pl.pallas_call(
        _final_kernel,
        grid=(nblk,),
        in_specs=[pl.BlockSpec((1, 1, TOP_K * tm), lambda i: (i, 0, 0), memory_space=pltpu.SMEM),
                  pl.BlockSpec((1, 1, TOP_K * tm), lambda i: (jnp.minimum(i + 1, nblk - 1), 0, 0),
                               memory_space=pltpu.SMEM),
                  row(D_MODEL), row(LANES), pl.BlockSpec(g.shape, lambda i: (0, 0)),
                  pl.BlockSpec(memory_space=pl.ANY)],
        out_specs=row(D_MODEL),
        out_shape=jax.ShapeDtypeStruct((T, D_MODEL), F32),
        scratch_shapes=[pltpu.VMEM((2, TOP_K, tm, HALF), U32), pltpu.SemaphoreType.DMA((2,))],
        compiler_params=pltpu.CompilerParams(dimension_semantics=("arbitrary",), vmem_limit_bytes=VMEM_LIMIT),
        name="final",
    )(pos, pos, x2, route, g, ys)


def _band_bias(table):
    assert WINDOW == BLOCK
    i = jnp.arange(BLOCK)[:, None]
    j = jnp.arange(2 * BLOCK)[None, :]
    n = jnp.maximum(i + BLOCK - j, 0)
    nf = jnp.maximum(n, 1).astype(F32)
    large = MAX_EXACT + (jnp.log(nf / MAX_EXACT) / math.log(MAX_DISTANCE / MAX_EXACT)
                         * (NUM_BUCKETS - MAX_EXACT)).astype(I32)
    large = jnp.minimum(large, NUM_BUCKETS - 1)
    bucket = jnp.where(n < MAX_EXACT, n, large)
    onehot = (bucket[:, :, None] == jnp.arange(NUM_BUCKETS)[None, None, :]).astype(F32)
    bias = jnp.einsum("ijb,bh->hij", onehot, table.astype(F32), precision=lax.Precision.HIGHEST)
    from_prev = (jnp.arange(BLOCK)[None, :] > i)[None]
    prev, cur = bias[:, :, :BLOCK], bias[:, :, BLOCK:]
    return jnp.stack([jnp.where(from_prev, NEG_INF, cur), jnp.where(from_prev, prev, cur)])


def _dispatch_plan(route_rows, counts_f, tmx, n_tiles_max, tm_rows):
    T = route_rows.shape[1]
    experts = jnp.arange(N_EXPERTS, dtype=I32)
    counts = counts_f[:N_EXPERTS, 0].astype(I32)
    ptiles = (counts + tmx - 1) // tmx
    tile_end = jnp.cumsum(ptiles)
    nt = tile_end[-1]
    row_off = (tile_end - ptiles) * tmx

    def slot(e_row, r_row):
        e = route_rows[e_row].astype(I32)
        off = jnp.sum(jnp.where(e[None, :] == experts[:, None], row_off[:, None], 0), axis=0)
        return (off + route_rows[r_row].astype(I32)).reshape(T // tm_rows, 1, tm_rows)

    pos = jnp.concatenate([slot(ROUTE_E1, ROUTE_R1), slot(ROUTE_E2, ROUTE_R2)], axis=2)

    tile_ids = jnp.arange(n_tiles_max, dtype=I32)
    expert_of = lambda t: jnp.sum((tile_end[None, :] <= t[:, None]).astype(I32), axis=1)
    te = expert_of(jnp.minimum(tile_ids, nt - 1))

    used = ptiles > 0
    run_first = (jnp.any((tile_ids[:, None] == (tile_end - ptiles)[None, :]) & used[None, :], axis=1)
                 & (tile_ids < nt)).astype(I32)
    run_slot = (jnp.cumsum(run_first) - 1) % 2
    later_used = used[None, :] & (experts[None, :] > experts[:, None])
    next_of = jnp.min(jnp.where(later_used, experts[None, :], N_EXPERTS), axis=1)
    next_of = jnp.where(next_of < N_EXPERTS, next_of, -1)
    run_next = jnp.sum(jnp.where(te[:, None] == experts[None, :], next_of[None, :], 0), axis=1)

    of_tile = lambda v: jnp.sum(jnp.where(te[:, None] == experts[None, :], v[None, :], 0), axis=1)
    tile_rows = jnp.clip(of_tile(counts) - (tile_ids - of_tile(tile_end - ptiles)) * tmx, 0, tmx)
    tile_rows = jnp.where(tile_ids < nt, tile_rows, 0)
    return pos, te, nt.reshape(1), run_first, run_slot.astype(I32), run_next.astype(I32), tile_rows.astype(I32)


def kernel(x, mem, rel_bias_table, norm_mix, w_in, attn_sinks, conv_w, conv_b, gate_bias_i, gate_bias_f, mlstm_norm, w_out, norm_cross, norm_mem, w_cq, w_ck, w_cv, w_co, norm_moe, w_router_group, b_router_group, w_router_expert, b_router_expert, w_exp_gate, w_exp_up, w_exp_down, norm_final):
    B, S, _ = x.shape
    T = B * S
    depth = w_in.shape[0]
    x2d = x.reshape(T, D_MODEL)
    mem2d = mem.reshape(B * N_MEM, D_MODEL)
    bias = _band_bias(rel_bias_table)

    tmx = min(TM_EXPERT, T)
    n_tiles_max = (T * TOP_K) // tmx + N_EXPERTS
    tm_rows = min(TM_ROWDMA, T)

    assert depth == 1, "the final combine is fused with the final norm: single layer only"
    l = 0
    gb = jnp.concatenate([gate_bias_i[l], gate_bias_f[l]]).astype(F32)
    gbias_col = jnp.pad(gb, (0, LANES - GATE_ROWS))[None, :]
    qa, kva, qkb, vb, ob, gc, gr, xs_zeroed = _inproj(x2d, norm_mix[l][None, :], w_in[l], gbias_col, B, S,
                                                      n_tiles_max * tmx, tmx)

    per_seq = lambda a: a.reshape(B, S, a.shape[-1])
    mix = _seqmix(per_seq(qkb), per_seq(vb), per_seq(ob), per_seq(gc), gr, per_seq(qa), per_seq(kva),
                  conv_w[l][:, 0, :].astype(F32), conv_b[l][None, :].astype(F32),
                  mlstm_norm[l][None, :].astype(F32), bias, attn_sinks[l].astype(F32)).reshape(T, W_A_Q + W_B)

    ck, cv = _memkv(mem2d, norm_mem[l][None, :], w_ck[l].astype(BF16), w_cv[l].astype(BF16), B)

    wr = jnp.pad(jnp.concatenate([w_router_expert[l], w_router_group[l]], axis=1),
                 ((0, 0), (0, LANES - N_EXPERTS - N_GROUPS))).astype(BF16)
    br = jnp.pad(jnp.concatenate([b_router_expert[l], b_router_group[l]]),
                 (0, LANES - N_EXPERTS - N_GROUPS)).astype(F32)[None, :]
    x2, hz_packed, route, route_rows, counts = _mid(
        x2d, mix, w_out[l].astype(BF16), norm_cross[l][None, :], w_cq[l].astype(BF16), ck, cv,
        w_co[l].astype(BF16), norm_moe[l][None, :], wr, br, B, S)

    pos, te, nt, run_first, run_slot, run_next, tile_rows = _dispatch_plan(route_rows, counts, tmx, n_tiles_max,
                                                                           tm_rows)
    xs = _dispatch(hz_packed, pos, xs_zeroed)
    ys = _experts(xs, w_exp_gate[l], w_exp_up[l], w_exp_down[l], te, nt, run_first, run_slot, run_next, tile_rows,
                  tmx)
    out = _final(x2, ys, pos, route, norm_final[None, :])
    return out.reshape(B, S, D_MODEL)
```

```python
import math

import jax
import jax.numpy as jnp
from jax import lax
from jax.experimental import pallas as pl
from jax.experimental.pallas import tpu as pltpu

F32 = jnp.float32
BF16 = jnp.bfloat16
U32 = jnp.uint32
I32 = jnp.int32

D_MODEL = 1024
N_MEM = 256
N_HEADS_A = 8
N_KV_A = 2
HEAD_DIM_A = 64
BLOCK = 128
WINDOW = 128
NUM_BUCKETS = 32
MAX_EXACT = NUM_BUCKETS // 2
MAX_DISTANCE = 128
N_HEADS_B = 4
HEAD_DIM_B = 128
CHUNK = 128
CONV_WIDTH = 4
N_HEADS_X = 4
HEAD_DIM_X = D_MODEL // N_HEADS_X
N_GROUPS = 4
EXPERTS_PER_GROUP = 8
N_EXPERTS = N_GROUPS * EXPERTS_PER_GROUP
TOP_K = 2
D_EXPERT = 512
EPS = 1e-6
NEG_INF = -1e30

W_A_Q = N_HEADS_A * HEAD_DIM_A
W_A_KV = N_KV_A * HEAD_DIM_A
W_B = N_HEADS_B * HEAD_DIM_B
C_QA = 0
C_KVA = C_QA + W_A_Q
C_QKB = C_KVA + 2 * W_A_KV
C_VB = C_QKB + 2 * W_B
C_OB = C_VB + W_B
C_GATE = C_OB + W_B
D_IN = C_GATE + 2 * N_HEADS_B

LANES = 128
SUBLANES = 8
GATE_ROWS = 8
HALF = D_MODEL // 2

TM_INPROJ = 1024
TM_MID = 1024
TM_ROWDMA = 512
TM_EXPERT = 512
EXPERT_ROW_STEP = 128

V7X_VMEM_BYTES = 64 * 1024 * 1024
VMEM_LIMIT = V7X_VMEM_BYTES * 3 // 4


def _rms(xf, g):
    return xf * lax.rsqrt(jnp.mean(xf * xf, axis=-1, keepdims=True) + EPS) * g


def _pack_halves(v):
    b = pltpu.bitcast(v.astype(BF16).astype(F32), U32)
    return (b[:, :HALF] >> 16) | b[:, HALF:]


def _unpack_halves(p):
    lo = pltpu.bitcast(p << 16, F32)
    hi = pltpu.bitcast(p & jnp.uint32(0xFFFF0000), F32)
    return jnp.concatenate([lo, hi], axis=-1)


def _log_sigmoid(z):
    return jnp.minimum(z, 0.0) - jnp.log1p(jnp.exp(-jnp.abs(z)))


def _split3(v):
    hi = v.astype(BF16).astype(F32)
    rest = v - hi
    mid = rest.astype(BF16).astype(F32)
    return hi, mid, (rest - mid).astype(BF16).astype(F32)


def _inproj_kernel(x_ref, g_ref, w32_ref, gbc_ref, qa_ref, kva_ref, qkb_ref, vb_ref, ob_ref, gc_ref, gr_ref,
                   xs_hbm, w_ref, zbuf, zsem):
    zero_fills = [pltpu.make_async_copy(zbuf, xs_hbm.at[pl.ds(t * zbuf.shape[0], zbuf.shape[0])], zsem)
                  for t in range(xs_hbm.shape[0] // zbuf.shape[0])]

    @pl.when(pl.program_id(0) == 0)
    def _():
        w_ref[:, C_GATE:] = jnp.zeros((D_MODEL, LANES), BF16)
        w_ref[:, :D_IN] = w32_ref[...].astype(BF16)
        zbuf[...] = jnp.zeros_like(zbuf)
        for fill in zero_fills:
            fill.start(priority=1)

    tm = x_ref.shape[0]
    h = _rms(x_ref[...], g_ref[...]).astype(BF16)

    def mm(lo, hi):
        return jnp.dot(h, w_ref[:, lo:hi], preferred_element_type=F32)

    qa_ref[...] = mm(C_QA, C_KVA).astype(BF16)
    kva_ref[...] = mm(C_KVA, C_QKB).astype(BF16)
    qkb_ref[...] = mm(C_QKB, C_VB).astype(BF16)
    vb_ref[...] = mm(C_VB, C_OB).astype(BF16)
    ob_ref[...] = mm(C_OB, C_GATE).astype(BF16)

    H, L = N_HEADS_B, CHUNK
    gcol = mm(C_GATE, C_GATE + LANES) + gbc_ref[...]
    grow = jnp.transpose(gcol)[0:GATE_ROWS, :]
    lane_c = lax.broadcasted_iota(I32, (L, LANES), 1)
    is_f_col = (lane_c >= H) & (lane_c < 2 * H)
    is_f_row = lax.broadcasted_iota(I32, (GATE_ROWS, L), 0) >= H
    ti = lax.broadcasted_iota(I32, (L, L), 0)
    si = lax.broadcasted_iota(I32, (L, L), 1)
    tril = jnp.where(si <= ti, 1.0, 0.0).astype(BF16)
    triu = jnp.where(si >= ti, 1.0, 0.0).astype(BF16)
    for c in range(tm // L):
        rows = slice(c * L, (c + 1) * L)
        gcol_c = gcol[rows, :]
        fcol = jnp.where(is_f_col, _log_sigmoid(gcol_c), 0.0)
        parts = jnp.dot(tril, jnp.concatenate(_split3(fcol), axis=1).astype(BF16), preferred_element_type=F32)
        bcol = parts[:, :LANES] + parts[:, LANES:2 * LANES] + parts[:, 2 * LANES:]
        gc_ref[rows, :] = jnp.where(is_f_col, bcol, gcol_c)
        grow_c = grow[:, rows]
        frow = jnp.where(is_f_row, _log_sigmoid(grow_c), 0.0)
        parts = jnp.dot(jnp.concatenate(_split3(frow), axis=0).astype(BF16), triu, preferred_element_type=F32)
        brow = parts[:GATE_ROWS] + parts[GATE_ROWS:2 * GATE_ROWS] + parts[2 * GATE_ROWS:]
        gr_ref[:, rows] = jnp.where(is_f_row, brow, grow_c)

    @pl.when(pl.program_id(0) == pl.num_programs(0) - 1)
    def _():
        for fill in zero_fills:
            fill.wait()


def _inproj(x2d, g, w_in, gbias_col, B, S, n_slots, tmx):
    T = x2d.shape[0]
    tm = min(TM_INPROJ, S)
    tiles_per_seq = S // tm
    row = lambda w: pl.BlockSpec((tm, w), lambda i: (i, 0))
    full = lambda a: pl.BlockSpec(a.shape, lambda i: (0,) * a.ndim)
    return pl.pallas_call(
        _inproj_kernel,
        grid=(T // tm,),
        in_specs=[row(D_MODEL), full(g),
                  pl.BlockSpec(w_in.shape, lambda i: (0, 0), pipeline_mode=pl.Buffered(1)), full(gbias_col)],
        out_specs=[row(W_A_Q), row(2 * W_A_KV), row(2 * W_B), row(W_B), row(W_B), row(LANES),
                   pl.BlockSpec((None, GATE_ROWS, tm), lambda i: (i // tiles_per_seq, 0, i % tiles_per_seq)),
                   pl.BlockSpec(memory_space=pl.ANY)],
        out_shape=[jax.ShapeDtypeStruct((T, W_A_Q), BF16),
                   jax.ShapeDtypeStruct((T, 2 * W_A_KV), BF16),
                   jax.ShapeDtypeStruct((T, 2 * W_B), BF16),
                   jax.ShapeDtypeStruct((T, W_B), BF16),
                   jax.ShapeDtypeStruct((T, W_B), BF16),
                   jax.ShapeDtypeStruct((T, LANES), F32),
                   jax.ShapeDtypeStruct((B, GATE_ROWS, S), F32),
                   jax.ShapeDtypeStruct((n_slots, HALF), U32)],
        scratch_shapes=[pltpu.VMEM((D_MODEL, C_GATE + LANES), BF16), pltpu.VMEM((tmx, HALF), U32),
                        pltpu.SemaphoreType.DMA(())],
        compiler_params=pltpu.CompilerParams(dimension_semantics=("arbitrary",), vmem_limit_bytes=VMEM_LIMIT),
        name="inproj",
    )(x2d, g, w_in, gbias_col)


def _swa_block(q, kvc, kvp, bias_ref, sink_ref):
    kvp = kvp.astype(F32)
    kvc = kvc.astype(F32)
    kband = jnp.concatenate([kvp[:, :W_A_KV], kvc[:, :W_A_KV]], axis=0)
    vband = jnp.concatenate([kvp[:, W_A_KV:], kvc[:, W_A_KV:]], axis=0)
    lane = lax.broadcasted_iota(I32, (2 * BLOCK, LANES), 1)
    lo = lane < HEAD_DIM_A

    def placements(band):
        swapped = pltpu.roll(band, HEAD_DIM_A, axis=1)
        z = jnp.zeros_like(band)
        return {(0, 0): jnp.where(lo, band, z).astype(BF16), (0, 1): jnp.where(lo, z, swapped).astype(BF16),
                (1, 0): jnp.where(lo, swapped, z).astype(BF16), (1, 1): jnp.where(lo, z, band).astype(BF16)}

    kpl = placements(kband)
    vpl = placements(vband)

    scale = HEAD_DIM_A ** -0.5
    group = N_HEADS_A // N_KV_A
    tiles = []
    from_prev = (lax.broadcasted_iota(I32, (BLOCK, BLOCK), 1) > lax.broadcasted_iota(I32, (BLOCK, BLOCK), 0))

    for pair in range(N_HEADS_A // 2):
        qt = q[:, pair * LANES:(pair + 1) * LANES]
        acc = None
        for half in range(2):
            h = 2 * pair + half
            g = h // group
            s2 = lax.dot_general(qt, kpl[(g, half)], (((1,), (1,)), ((), ())), preferred_element_type=F32)
            s = jnp.where(from_prev, s2[:, :BLOCK], s2[:, BLOCK:])
            s = s * scale + bias_ref[h]
            sink = sink_ref[h]
            m = jnp.maximum(jnp.max(s, axis=-1, keepdims=True), sink)
            p = jnp.exp(s - m)
            denom = jnp.sum(p, axis=-1, keepdims=True) + jnp.exp(sink - m)
            p2 = jnp.concatenate([jnp.where(from_prev, p, 0.0), jnp.where(from_prev, 0.0, p)], axis=1)
            o = jnp.dot(p2.astype(BF16), vpl[(g, half)], preferred_element_type=F32) / denom
            acc = o if acc is None else acc + o
        tiles.append(acc.astype(BF16))
    return tiles


CONV_HALO = 16
SEQ_SUB = 2


def _seqmix_kernel(sink_ref, qkc_ref, qkp_ref, vb_ref, ob_ref, gc_ref, gr_ref, qa_ref, kvc_ref, kvp_ref,
                   cw_ref, cb_ref, nrm_ref, bias0_ref, bias_ref, o_ref, state_ref, m_ref):
    c = pl.program_id(0)
    B = qkc_ref.shape[0]
    H, D, L = N_HEADS_B, HEAD_DIM_B, CHUNK

    @pl.when(c == 0)
    def _():
        state_ref[...] = jnp.zeros_like(state_ref)
        m_ref[...] = jnp.zeros_like(m_ref)

    rr = lax.broadcasted_iota(I32, (L, CONV_HALO + L), 0)
    cc = lax.broadcasted_iota(I32, (L, CONV_HALO + L), 1)
    shifts = {delay: jnp.where(cc == rr + (CONV_HALO - delay), 1.0, 0.0).astype(BF16)
              for delay in range(1, CONV_WIDTH)}
    ti = lax.broadcasted_iota(I32, (L, L), 0)
    si = lax.broadcasted_iota(I32, (L, L), 1)
    tri = si <= ti
    ones_blk = jnp.ones((L, D), BF16)

    def conv_silu(b, u):
        if u == 0:
            prev = qkp_ref[b]
            prev = jnp.where(c > 0, prev, jnp.zeros_like(prev))
        else:
            prev = qkc_ref[b, u * L - CONV_HALO:u * L, :]
        cur = qkc_ref[b, u * L:(u + 1) * L, :]
        ext = jnp.concatenate([prev, cur], axis=0)
        y = cb_ref[...] + cw_ref[CONV_WIDTH - 1:CONV_WIDTH, :] * cur.astype(F32)
        for delay in range(1, CONV_WIDTH):
            tap = CONV_WIDTH - 1 - delay
            y = y + cw_ref[tap:tap + 1, :] * jnp.dot(shifts[delay], ext, preferred_element_type=F32)
        return y * jax.nn.sigmoid(y)

    states = {(b, h): state_ref[b, h] for b in range(B) for h in range(H)}
    ms = {(b, h): m_ref[b, h:h + 1, 0:1] for b in range(B) for h in range(H)}

    n_sub = qkc_ref.shape[1] // L
    for u, b, h in [(u, b, h) for u in range(n_sub) for b in range(B) for h in range(H)]:
        rows = slice(u * L, (u + 1) * L)
        if h == 0:
            kvp = kvp_ref[b] if u == 0 else kvc_ref[b, (u - 1) * L:u * L, :]
            tiles = _swa_block(qa_ref[b, rows, :], kvc_ref[b, rows, :], kvp, bias0_ref if u == 0 else bias_ref,
                               sink_ref)
            for pair, tile in enumerate(tiles):
                o_ref[b, rows, pair * LANES:(pair + 1) * LANES] = tile
            qk = conv_silu(b, u)
            gcol = gc_ref[b, rows, :]
            grow = gr_ref[b, :, rows]
        qh = (qk[:, h * D:(h + 1) * D] * (D ** -0.5)).astype(BF16)
        k_t = qk[:, W_B + h * D:W_B + (h + 1) * D].T
        v1 = jnp.concatenate([vb_ref[b, rows, h * D:(h + 1) * D], ones_blk], axis=-1)
        b_r = grow[H + h:H + h + 1, :]
        g_r = grow[h:h + 1, :] - b_r
        b_c = gcol[:, H + h:H + h + 1]
        m_prev = ms[b, h]
        state = states[b, h]

        gmat = jnp.where(tri, g_r, NEG_INF)
        m_c = jnp.maximum(jnp.max(gmat, axis=-1, keepdims=True), m_prev)
        a_inter = jnp.exp(m_prev - m_c)
        sc = jnp.dot(qh, k_t.astype(BF16), preferred_element_type=F32) * jnp.exp(gmat - m_c)
        tot = (jnp.dot(sc.astype(BF16), v1, preferred_element_type=F32)
               + a_inter * jnp.dot(qh, state.astype(BF16), preferred_element_type=F32))
        num = tot[:, :D]
        den = tot[:, D:]
        hh = num / jnp.maximum(jnp.abs(den), jnp.exp(-(b_c + m_c)))

        b_last = b_r[:, L - 1:L]
        m_new = jnp.maximum(b_last + m_prev, b_last + jnp.max(g_r, axis=-1, keepdims=True))
        w_r = jnp.exp(g_r + (b_last - m_new))
        decay = jnp.exp(b_last + m_prev - m_new)
        upd = jnp.dot((k_t * w_r).astype(BF16), v1, preferred_element_type=F32)
        states[b, h] = decay * state + upd
        ms[b, h] = m_new

        og = jax.nn.sigmoid(ob_ref[b, rows, h * D:(h + 1) * D].astype(F32))
        hb = og * hh
        hb = hb * lax.rsqrt(jnp.mean(hb * hb, axis=-1, keepdims=True) + EPS)
        o_ref[b, rows, W_A_Q + h * D:W_A_Q + (h + 1) * D] = (hb * nrm_ref[:, h * D:(h + 1) * D]).astype(BF16)

    for b, h in states:
        state_ref[b, h] = states[b, h]
        m_ref[b, h:h + 1, :] = jnp.broadcast_to(ms[b, h], (1, LANES))


def _seqmix(qkb, vb, ob, gc, gr, qa, kva, conv_w, conv_b, nrm, bias, sinks):
    assert CHUNK == BLOCK
    B, S, _ = qkb.shape
    rows = min(SEQ_SUB * CHUNK, S)
    halo_per_step = rows // CONV_HALO
    blk = lambda w: pl.BlockSpec((B, rows, w), lambda c: (0, c, 0))
    full = lambda a: pl.BlockSpec(a.shape, lambda c: (0,) * a.ndim)
    bias_variant = lambda pick: pl.BlockSpec((None,) + bias.shape[1:], lambda c: (pick(c), 0, 0, 0))
    return pl.pallas_call(
        _seqmix_kernel,
        grid=(S // rows,),
        in_specs=[pl.BlockSpec(memory_space=pltpu.SMEM),
                  blk(2 * W_B),
                  pl.BlockSpec((B, CONV_HALO, 2 * W_B), lambda c: (0, jnp.maximum(c * halo_per_step - 1, 0), 0)),
                  blk(W_B), blk(W_B), blk(LANES),
                  pl.BlockSpec((B, GATE_ROWS, rows), lambda c: (0, 0, c)),
                  blk(W_A_Q), blk(2 * W_A_KV),
                  pl.BlockSpec((B, BLOCK, 2 * W_A_KV), lambda c: (0, jnp.maximum(c * (rows // BLOCK) - 1, 0), 0)),
                  full(conv_w), full(conv_b), full(nrm),
                  bias_variant(lambda c: jnp.minimum(c, 1)), bias_variant(lambda c: 1)],
        out_specs=blk(W_A_Q + W_B),
        out_shape=jax.ShapeDtypeStruct((B, S, W_A_Q + W_B), BF16),
        scratch_shapes=[pltpu.VMEM((B, N_HEADS_B, HEAD_DIM_B, 2 * HEAD_DIM_B), F32),
                        pltpu.VMEM((B, GATE_ROWS, LANES), F32)],
        compiler_params=pltpu.CompilerParams(dimension_semantics=("arbitrary",), vmem_limit_bytes=VMEM_LIMIT),
        name="seqmix",
    )(sinks, qkb, qkb, vb, ob, gc, gr, qa, kva, kva, conv_w, conv_b, nrm, bias, bias)


def _memkv_kernel(mem_ref, g_ref, wk_ref, wv_ref, k_ref, v_ref):
    hm = _rms(mem_ref[...], g_ref[...]).astype(BF16)
    k_ref[...] = jnp.dot(hm, wk_ref[...].astype(BF16), preferred_element_type=F32).astype(BF16)
    v_ref[...] = jnp.dot(hm, wv_ref[...].astype(BF16), preferred_element_type=F32).astype(BF16)


def _memkv(mem2d, g, wk, wv):
    full = lambda a: pl.BlockSpec(a.shape, lambda i: (0,) * a.ndim)
    return pl.pallas_call(
        _memkv_kernel,
        grid=(1,),
        in_specs=[full(mem2d), full(g), full(wk), full(wv)],
        out_specs=[full(mem2d), full(mem2d)],
        out_shape=[jax.ShapeDtypeStruct(mem2d.shape, BF16)] * 2,
        compiler_params=pltpu.CompilerParams(dimension_semantics=("arbitrary",), vmem_limit_bytes=VMEM_LIMIT),
        name="memkv",
    )(mem2d, g, wk, wv)


ROUTE_E1, ROUTE_E2, ROUTE_G1, ROUTE_G2, ROUTE_R1, ROUTE_R2 = 0, 1, 2, 3, 4, 5
ROUTER_GROUP_COL = N_EXPERTS
ROUTER_ROWS = -(-(N_EXPERTS + N_GROUPS) // SUBLANES) * SUBLANES


def _mid_kernel(x_ref, mix_ref, wo_ref, gx_ref, wq_ref, ck_ref, cv_ref, wco_ref, gz_ref, wr_ref, br_ref,
                x2_ref, hz_ref, route_ref, rrows_ref, counts_ref, cnt_ref):
    @pl.when(pl.program_id(0) == 0)
    def _():
        cnt_ref[...] = jnp.zeros_like(cnt_ref)

    x1 = x_ref[...] + jnp.dot(mix_ref[...], wo_ref[...], preferred_element_type=F32)

    hc = _rms(x1, gx_ref[...]).astype(BF16)
    cq = jnp.dot(hc, wq_ref[...], preferred_element_type=F32).astype(BF16)
    scale = HEAD_DIM_X ** -0.5
    heads = []
    for h in range(N_HEADS_X):
        sl = slice(h * HEAD_DIM_X, (h + 1) * HEAD_DIM_X)
        s = lax.dot_general(cq[:, sl], ck_ref[:, sl], (((1,), (1,)), ((), ())), preferred_element_type=F32) * scale
        p = jnp.exp(s - jnp.max(s, axis=-1, keepdims=True))
        co = jnp.dot(p.astype(BF16), cv_ref[:, sl], preferred_element_type=F32) / jnp.sum(p, axis=-1, keepdims=True)
        heads.append(co.astype(BF16))
    x2 = x1 + jnp.dot(jnp.concatenate(heads, axis=-1), wco_ref[...], preferred_element_type=F32)
    x2_ref[...] = x2

    hz = _rms(x2, gz_ref[...])
    hz_ref[...] = _pack_halves(hz)
    lg = jnp.dot(hz.astype(BF16), wr_ref[...], preferred_element_type=F32) + br_ref[...]
    tm = lg.shape[0]
    lt = jnp.transpose(lg)[0:ROUTER_ROWS, :]
    row = lax.broadcasted_iota(I32, lt.shape, 0)
    big = jnp.int32(ROUTER_ROWS)
    is_g = (row >= ROUTER_GROUP_COL) & (row < ROUTER_GROUP_COL + N_GROUPS)
    gl = jnp.where(is_g, lt, NEG_INF)
    gmax = jnp.max(gl, axis=0, keepdims=True)
    gsum = jnp.sum(jnp.exp(gl - gmax), axis=0, keepdims=True)
    g_prob = 1.0 / gsum
    g_idx = jnp.min(jnp.where(gl == gmax, row - ROUTER_GROUP_COL, big), axis=0, keepdims=True)
    sel = (row < N_EXPERTS) & ((row // EXPERTS_PER_GROUP) == g_idx)
    el = jnp.where(sel, lt, NEG_INF)
    m1 = jnp.max(el, axis=0, keepdims=True)
    i1 = jnp.min(jnp.where(el == m1, row, big), axis=0, keepdims=True)
    el2 = jnp.where(row == i1, NEG_INF, el)
    m2 = jnp.max(el2, axis=0, keepdims=True)
    i2 = jnp.min(jnp.where(el2 == m2, row, big), axis=0, keepdims=True)
    z = jnp.sum(jnp.exp(el - m1), axis=0, keepdims=True)
    p1 = 1.0 / z
    p2 = jnp.exp(m2 - m1) / z
    g1 = g_prob * (p1 / (p1 + p2))
    g2 = g_prob * (p2 / (p1 + p2))

    used = jnp.where((row == i1) | (row == i2), 1.0, 0.0)
    t_from = lax.broadcasted_iota(I32, (tm, tm), 0)
    t_to = lax.broadcasted_iota(I32, (tm, tm), 1)
    earlier = jnp.where(t_from < t_to, 1.0, 0.0).astype(BF16)
    before = jnp.dot(used.astype(BF16), earlier, preferred_element_type=F32) + cnt_ref[:, 0:1]
    r1 = jnp.sum(jnp.where(row == i1, before, 0.0), axis=0, keepdims=True)
    r2 = jnp.sum(jnp.where(row == i2, before, 0.0), axis=0, keepdims=True)
    cnt_ref[...] = cnt_ref[...] + jnp.sum(used, axis=1, keepdims=True)
    counts_ref[...] = cnt_ref[...]

    rec_row = lax.broadcasted_iota(I32, (SUBLANES, tm), 0)
    rec = jnp.zeros((SUBLANES, tm), F32)
    for c, v in ((ROUTE_E1, i1.astype(F32)), (ROUTE_E2, i2.astype(F32)), (ROUTE_G1, g1), (ROUTE_G2, g2),
                 (ROUTE_R1, r1), (ROUTE_R2, r2)):
        rec = jnp.where(rec_row == c, v, rec)
    rrows_ref[...] = rec
    route_ref[...] = jnp.transpose(jnp.concatenate([rec, jnp.zeros((LANES - SUBLANES, tm), F32)], axis=0))


def _mid(x2d, mix, wo, gx, wq, ck, cv, wco, gz, wr, br, B, S):
    T = B * S
    tm = min(TM_MID, S)
    per_b = S // tm
    row = lambda w: pl.BlockSpec((tm, w), lambda i: (i, 0))
    full = lambda a: pl.BlockSpec(a.shape, lambda i: (0,) * a.ndim)
    kvspec = pl.BlockSpec((N_MEM, D_MODEL), lambda i: (i // per_b, 0))
    return pl.pallas_call(
        _mid_kernel,
        grid=(T // tm,),
        in_specs=[row(D_MODEL), row(W_A_Q + W_B), full(wo), full(gx), full(wq), kvspec, kvspec,
                  full(wco), full(gz), full(wr), full(br)],
        out_specs=[row(D_MODEL), row(HALF), row(LANES), pl.BlockSpec((SUBLANES, tm), lambda i: (0, i)),
                   pl.BlockSpec((ROUTER_ROWS, LANES), lambda i: (0, 0))],
        out_shape=[jax.ShapeDtypeStruct((T, D_MODEL), F32),
                   jax.ShapeDtypeStruct((T, HALF), U32),
                   jax.ShapeDtypeStruct((T, LANES), F32),
                   jax.ShapeDtypeStruct((SUBLANES, T), F32),
                   jax.ShapeDtypeStruct((ROUTER_ROWS, LANES), F32)],
        scratch_shapes=[pltpu.VMEM((ROUTER_ROWS, LANES), F32)],
        compiler_params=pltpu.CompilerParams(dimension_semantics=("arbitrary",), vmem_limit_bytes=VMEM_LIMIT),
        name="mid",
    )(x2d, mix, wo, gx, wq, ck, cv, wco, gz, wr, br)


def _dispatch_kernel(pos_ref, hz_ref, xs_zeroed, xs_hbm, sem):
    del xs_zeroed
    tm = hz_ref.shape[0]

    for r in range(tm):
        for k in range(TOP_K):
            pltpu.make_async_copy(hz_ref.at[pl.ds(r, 1)], xs_hbm.at[pl.ds(pos_ref[0, 0, k * tm + r], 1)],
                                  sem).start(priority=k % 2)
    for k in range(TOP_K):
        pltpu.make_async_copy(hz_ref, xs_hbm.at[pl.ds(0, tm)], sem).wait()


def _dispatch(hz_packed, pos, xs_zeroed):
    T = hz_packed.shape[0]
    tm = pos.shape[2] // TOP_K
    return pl.pallas_call(
        _dispatch_kernel,
        grid=(T // tm,),
        in_specs=[pl.BlockSpec((1, 1, TOP_K * tm), lambda i: (i, 0, 0), memory_space=pltpu.SMEM),
                  pl.BlockSpec((tm, HALF), lambda i: (i, 0)),
                  pl.BlockSpec(memory_space=pl.ANY)],
        out_specs=pl.BlockSpec(memory_space=pl.ANY),
        out_shape=jax.ShapeDtypeStruct(xs_zeroed.shape, U32),
        scratch_shapes=[pltpu.SemaphoreType.DMA(())],
        input_output_aliases={2: 0},
        compiler_params=pltpu.CompilerParams(dimension_semantics=("arbitrary",), vmem_limit_bytes=VMEM_LIMIT),
        name="dispatch",
    )(pos, hz_packed, xs_zeroed)


def _expert_kernel(te_ref, nt_ref, first_ref, slot_ref, next_ref, rows_ref, xs_ref, wg_hbm, wu_hbm, wd_hbm, ys_ref,
                   wg32, wu32, wd32, wgb, wub, wdb, wsem):
    i = pl.program_id(0)
    nt = nt_ref[0]
    tmx = xs_ref.shape[0]

    def fetch(e, s):
        return [pltpu.make_async_copy(src.at[e], dst.at[s], wsem.at[s])
                for src, dst in ((wg_hbm, wg32), (wu_hbm, wu32), (wd_hbm, wd32))]

    @pl.when(i < nt)
    def _():
        @pl.when(i == 0)
        def _():
            for cp in fetch(te_ref[0], 0):
                cp.start()

        @pl.when(first_ref[i] != 0)
        def _():
            s = slot_ref[i]
            for cp in fetch(te_ref[i], s):
                cp.wait()

            @pl.when(next_ref[i] >= 0)
            def _():
                for cp in fetch(next_ref[i], 1 - s):
                    cp.start()

            wgb[...] = wg32[s].astype(BF16)
            wub[...] = wu32[s].astype(BF16)
            wdb[...] = wd32[s].astype(BF16)

        nrows = rows_ref[i]
        for m in range(EXPERT_ROW_STEP, tmx + 1, EXPERT_ROW_STEP):
            @pl.when((nrows > m - EXPERT_ROW_STEP) & (nrows <= m))
            def _():
                x = _unpack_halves(xs_ref[0:m, :]).astype(BF16)
                hg = jnp.dot(x, wgb[...], preferred_element_type=F32)
                hu = jnp.dot(x, wub[...], preferred_element_type=F32)
                a = (hg * jax.nn.sigmoid(hg) * hu).astype(BF16)
                ys_ref[0:m, :] = _pack_halves(jnp.dot(a, wdb[...], preferred_element_type=F32))
                if m < tmx:
                    ys_ref[m:, :] = jnp.zeros((tmx - m, HALF), U32)

    @pl.when(i >= nt)
    def _():
        ys_ref[...] = jnp.zeros_like(ys_ref)


def _experts(xs, w_gate, w_up, w_down, tile_expert, ntiles, run_first, run_slot, run_next, tile_rows, tmx):
    n_tiles_max = tile_expert.shape[0]
    hbm = pl.BlockSpec(memory_space=pl.ANY)
    grid_spec = pltpu.PrefetchScalarGridSpec(
        num_scalar_prefetch=6,
        grid=(n_tiles_max,),
        in_specs=[pl.BlockSpec((tmx, HALF), lambda i, te, nt, *_: (jnp.minimum(i, nt[0] - 1), 0)), hbm, hbm, hbm],
        out_specs=pl.BlockSpec((tmx, HALF), lambda i, *_: (i, 0)),
        scratch_shapes=[pltpu.VMEM((2, D_MODEL, D_EXPERT), F32),
                        pltpu.VMEM((2, D_MODEL, D_EXPERT), F32),
                        pltpu.VMEM((2, D_EXPERT, D_MODEL), F32),
                        pltpu.VMEM((D_MODEL, D_EXPERT), BF16),
                        pltpu.VMEM((D_MODEL, D_EXPERT), BF16),
                        pltpu.VMEM((D_EXPERT, D_MODEL), BF16),
                        pltpu.SemaphoreType.DMA((2,))],
    )
    return pl.pallas_call(
        _expert_kernel,
        grid_spec=grid_spec,
        out_shape=jax.ShapeDtypeStruct(xs.shape, U32),
        compiler_params=pltpu.CompilerParams(dimension_semantics=("arbitrary",), vmem_limit_bytes=VMEM_LIMIT),
        name="experts",
    )(tile_expert, ntiles, run_first, run_slot, run_next, tile_rows, xs, w_gate, w_up, w_down)


def _final_kernel(posc_ref, posn_ref, x2_ref, route_ref, g_ref, ys_hbm, o_ref, ybuf, sem):
    i = pl.program_id(0)
    n = pl.num_programs(0)
    tm = x2_ref.shape[0]
    slot = i % 2

    def issue(pos_ref, s):
        for r in range(tm):
            for k in range(TOP_K):
                pltpu.make_async_copy(ys_hbm.at[pl.ds(pos_ref[0, 0, k * tm + r], 1)],
                                      ybuf.at[s, k, pl.ds(r, 1)], sem.at[s]).start(priority=k % 2)

    def wait(s):
        for k in range(TOP_K):
            pltpu.make_async_copy(ys_hbm.at[pl.ds(0, tm)], ybuf.at[s, k], sem.at[s]).wait()

    @pl.when(i == 0)
    def _():
        issue(posc_ref, 0)

    wait(slot)

    for s in range(2):
        @pl.when(slot == s)
        def _():
            issue(posn_ref, 1 - s)

    r = route_ref[...]
    g1 = r[:, ROUTE_G1:ROUTE_G1 + 1]
    g2 = r[:, ROUTE_G2:ROUTE_G2 + 1]
    xo = x2_ref[...] + g1 * _unpack_halves(ybuf[slot, 0]) + g2 * _unpack_halves(ybuf[slot, 1])
    o_ref[...] = _rms(xo, g_ref[...])

    @pl.when(i == n - 1)
    def _():
        wait(1 - slot)


def _final(x2, ys, pos, route, g):
    T = x2.shape[0]
    nblk = pos.shape[0]
    tm = T // nblk
    row = lambda w: pl.BlockSpec((tm, w), lambda i: (i, 0))
    return pl.pallas_call(
        _final_kernel,
        grid=(nblk,),
        in_specs=[pl.BlockSpec((1, 1, TOP_K * tm), lambda i: (i, 0, 0), memory_space=pltpu.SMEM),
                  pl.BlockSpec((1, 1, TOP_K * tm), lambda i: (jnp.minimum(i + 1, nblk - 1), 0, 0),
                               memory_space=pltpu.SMEM),
                  row(D_MODEL), row(LANES), pl.BlockSpec(g.shape, lambda i: (0, 0)),
                  pl.BlockSpec(memory_space=pl.ANY)],
        out_specs=row(D_MODEL),
        out_shape=jax.ShapeDtypeStruct((T, D_MODEL), F32),
        scratch_shapes=[pltpu.VMEM((2, TOP_K, tm, HALF), U32), pltpu.SemaphoreType.DMA((2,))],
        compiler_params=pltpu.CompilerParams(dimension_semantics=("arbitrary",), vmem_limit_bytes=VMEM_LIMIT),
        name="final",
    )(pos, pos, x2, route, g, ys)


def _band_bias(table):
    assert WINDOW == BLOCK
    i = jnp.arange(BLOCK)[:, None]
    j = jnp.arange(2 * BLOCK)[None, :]
    n = jnp.maximum(i + BLOCK - j, 0)
    nf = jnp.maximum(n, 1).astype(F32)
    large = MAX_EXACT + (jnp.log(nf / MAX_EXACT) / math.log(MAX_DISTANCE / MAX_EXACT)
                         * (NUM_BUCKETS - MAX_EXACT)).astype(I32)
    large = jnp.minimum(large, NUM_BUCKETS - 1)
    bucket = jnp.where(n < MAX_EXACT, n, large)
    onehot = (bucket[:, :, None] == jnp.arange(NUM_BUCKETS)[None, None, :]).astype(F32)
    bias = jnp.einsum("ijb,bh->hij", onehot, table.astype(F32), precision=lax.Precision.HIGHEST)
    from_prev = (jnp.arange(BLOCK)[None, :] > i)[None]
    prev, cur = bias[:, :, :BLOCK], bias[:, :, BLOCK:]
    return jnp.stack([jnp.where(from_prev, NEG_INF, cur), jnp.where(from_prev, prev, cur)])


def _dispatch_plan(route_rows, counts_f, tmx, n_tiles_max, tm_rows):
    T = route_rows.shape[1]
    experts = jnp.arange(N_EXPERTS, dtype=I32)
    counts = counts_f[:N_EXPERTS, 0].astype(I32)
    ptiles = (counts + tmx - 1) // tmx
    tile_end = jnp.cumsum(ptiles)
    nt = tile_end[-1]
    row_off = (tile_end - ptiles) * tmx

    def slot(e_row, r_row):
        e = route_rows[e_row].astype(I32)
        off = jnp.sum(jnp.where(e[None, :] == experts[:, None], row_off[:, None], 0), axis=0)
        return (off + route_rows[r_row].astype(I32)).reshape(T // tm_rows, 1, tm_rows)

    pos = jnp.concatenate([slot(ROUTE_E1, ROUTE_R1), slot(ROUTE_E2, ROUTE_R2)], axis=2)

    tile_ids = jnp.arange(n_tiles_max, dtype=I32)
    expert_of = lambda t: jnp.sum((tile_end[None, :] <= t[:, None]).astype(I32), axis=1)
    te = expert_of(jnp.minimum(tile_ids, nt - 1))

    used = ptiles > 0
    run_first = (jnp.any((tile_ids[:, None] == (tile_end - ptiles)[None, :]) & used[None, :], axis=1)
                 & (tile_ids < nt)).astype(I32)
    run_slot = (jnp.cumsum(run_first) - 1) % 2
    later_used = used[None, :] & (experts[None, :] > experts[:, None])
    next_of = jnp.min(jnp.where(later_used, experts[None, :], N_EXPERTS), axis=1)
    next_of = jnp.where(next_of < N_EXPERTS, next_of, -1)
    run_next = jnp.sum(jnp.where(te[:, None] == experts[None, :], next_of[None, :], 0), axis=1)

    of_tile = lambda v: jnp.sum(jnp.where(te[:, None] == experts[None, :], v[None, :], 0), axis=1)
    tile_rows = jnp.clip(of_tile(counts) - (tile_ids - of_tile(tile_end - ptiles)) * tmx, 0, tmx)
    tile_rows = jnp.where(tile_ids < nt, tile_rows, 0)
    return pos, te, nt.reshape(1), run_first, run_slot.astype(I32), run_next.astype(I32), tile_rows.astype(I32)


def kernel(x, mem, rel_bias_table, norm_mix, w_in, attn_sinks, conv_w, conv_b, gate_bias_i, gate_bias_f, mlstm_norm, w_out, norm_cross, norm_mem, w_cq, w_ck, w_cv, w_co, norm_moe, w_router_group, b_router_group, w_router_expert, b_router_expert, w_exp_gate, w_exp_up, w_exp_down, norm_final):
    B, S, _ = x.shape
    T = B * S
    depth = w_in.shape[0]
    x2d = x.reshape(T, D_MODEL)
    mem2d = mem.reshape(B * N_MEM, D_MODEL)
    bias = _band_bias(rel_bias_table)

    tmx = min(TM_EXPERT, T)
    n_tiles_max = (T * TOP_K) // tmx + N_EXPERTS
    tm_rows = min(TM_ROWDMA, T)

    assert depth == 1, "the final combine is fused with the final norm: single layer only"
    l = 0
    gb = jnp.concatenate([gate_bias_i[l], gate_bias_f[l]]).astype(F32)
    gbias_col = jnp.pad(gb, (0, LANES - GATE_ROWS))[None, :]
    qa, kva, qkb, vb, ob, gc, gr, xs_zeroed = _inproj(x2d, norm_mix[l][None, :], w_in[l], gbias_col, B, S,
                                                      n_tiles_max * tmx, tmx)

    per_seq = lambda a: a.reshape(B, S, a.shape[-1])
    mix = _seqmix(per_seq(qkb), per_seq(vb), per_seq(ob), per_seq(gc), gr, per_seq(qa), per_seq(kva),
                  conv_w[l][:, 0, :].astype(F32), conv_b[l][None, :].astype(F32),
                  mlstm_norm[l][None, :].astype(F32), bias, attn_sinks[l].astype(F32)).reshape(T, W_A_Q + W_B)

    ck, cv = _memkv(mem2d, norm_mem[l][None, :], w_ck[l], w_cv[l])

    wr = jnp.pad(jnp.concatenate([w_router_expert[l], w_router_group[l]], axis=1),
                 ((0, 0), (0, LANES - N_EXPERTS - N_GROUPS))).astype(BF16)
    br = jnp.pad(jnp.concatenate([b_router_expert[l], b_router_group[l]]),
                 (0, LANES - N_EXPERTS - N_GROUPS)).astype(F32)[None, :]
    x2, hz_packed, route, route_rows, counts = _mid(
        x2d, mix, w_out[l].astype(BF16), norm_cross[l][None, :], w_cq[l].astype(BF16), ck, cv,
        w_co[l].astype(BF16), norm_moe[l][None, :], wr, br, B, S)

    pos, te, nt, run_first, run_slot, run_next, tile_rows = _dispatch_plan(route_rows, counts, tmx, n_tiles_max,
                                                                           tm_rows)
    xs = _dispatch(hz_packed, pos, xs_zeroed)
    ys = _experts(xs, w_exp_gate[l], w_exp_up[l], w_exp_down[l], te, nt, run_first, run_slot, run_next, tile_rows,
                  tmx)
    out = _final(x2, ys, pos, route, norm_final[None, :])
    return out.reshape(B, S, D_MODEL)
```

```python
import math

import jax
import jax.numpy as jnp
from jax import lax
from jax.experimental import pallas as pl
from jax.experimental.pallas import tpu as pltpu

F32 = jnp.float32
BF16 = jnp.bfloat16
U32 = jnp.uint32
I32 = jnp.int32

D_MODEL = 1024
N_MEM = 256
N_HEADS_A = 8
N_KV_A = 2
HEAD_DIM_A = 64
BLOCK = 128
WINDOW = 128
NUM_BUCKETS = 32
MAX_EXACT = NUM_BUCKETS // 2
MAX_DISTANCE = 128
N_HEADS_B = 4
HEAD_DIM_B = 128
CHUNK = 128
CONV_WIDTH = 4
N_HEADS_X = 4
HEAD_DIM_X = D_MODEL // N_HEADS_X
N_GROUPS = 4
EXPERTS_PER_GROUP = 8
N_EXPERTS = N_GROUPS * EXPERTS_PER_GROUP
TOP_K = 2
D_EXPERT = 512
EPS = 1e-6
NEG_INF = -1e30

W_A_Q = N_HEADS_A * HEAD_DIM_A
W_A_KV = N_KV_A * HEAD_DIM_A
W_B = N_HEADS_B * HEAD_DIM_B
C_QA = 0
C_KVA = C_QA + W_A_Q
C_QKB = C_KVA + 2 * W_A_KV
C_VB = C_QKB + 2 * W_B
C_OB = C_VB + W_B
C_GATE = C_OB + W_B
D_IN = C_GATE + 2 * N_HEADS_B

LANES = 128
SUBLANES = 8
GATE_ROWS = 8
HALF = D_MODEL // 2

TM_INPROJ = 1024
TM_MID = 1024
TM_ROWDMA = 512
TM_EXPERT = 512
EXPERT_ROW_STEP = 128

V7X_VMEM_BYTES = 64 * 1024 * 1024
VMEM_LIMIT = V7X_VMEM_BYTES * 3 // 4


def _rms(xf, g):
    return xf * lax.rsqrt(jnp.mean(xf * xf, axis=-1, keepdims=True) + EPS) * g


def _pack_halves(v):
    b = pltpu.bitcast(v.astype(BF16).astype(F32), U32)
    return (b[:, :HALF] >> 16) | b[:, HALF:]


def _unpack_halves(p):
    lo = pltpu.bitcast(p << 16, F32)
    hi = pltpu.bitcast(p & jnp.uint32(0xFFFF0000), F32)
    return jnp.concatenate([lo, hi], axis=-1)


def _log_sigmoid(z):
    return jnp.minimum(z, 0.0) - jnp.log1p(jnp.exp(-jnp.abs(z)))


def _split3(v):
    hi = v.astype(BF16).astype(F32)
    rest = v - hi
    mid = rest.astype(BF16).astype(F32)
    return hi, mid, (rest - mid).astype(BF16).astype(F32)


def _inproj_kernel(x_ref, g_ref, w32_ref, gbc_ref, qa_ref, kva_ref, qkb_ref, vb_ref, ob_ref, gc_ref, gr_ref,
                   xs_hbm, w_ref, zbuf, zsem):
    zero_fills = [pltpu.make_async_copy(zbuf, xs_hbm.at[pl.ds(t * zbuf.shape[0], zbuf.shape[0])], zsem)
                  for t in range(xs_hbm.shape[0] // zbuf.shape[0])]

    @pl.when(pl.program_id(0) == 0)
    def _():
        w_ref[:, C_GATE:] = jnp.zeros((D_MODEL, LANES), BF16)
        w_ref[:, :D_IN] = w32_ref[...].astype(BF16)
        zbuf[...] = jnp.zeros_like(zbuf)
        for fill in zero_fills:
            fill.start(priority=1)

    tm = x_ref.shape[0]
    h = _rms(x_ref[...], g_ref[...]).astype(BF16)

    def mm(lo, hi):
        return jnp.dot(h, w_ref[:, lo:hi], preferred_element_type=F32)

    qa_ref[...] = mm(C_QA, C_KVA).astype(BF16)
    kva_ref[...] = mm(C_KVA, C_QKB).astype(BF16)
    qkb_ref[...] = mm(C_QKB, C_VB).astype(BF16)
    vb_ref[...] = mm(C_VB, C_OB).astype(BF16)
    ob_ref[...] = mm(C_OB, C_GATE).astype(BF16)

    H, L = N_HEADS_B, CHUNK
    gcol = mm(C_GATE, C_GATE + LANES) + gbc_ref[...]
    grow = jnp.transpose(gcol)[0:GATE_ROWS, :]
    lane_c = lax.broadcasted_iota(I32, (L, LANES), 1)
    is_f_col = (lane_c >= H) & (lane_c < 2 * H)
    is_f_row = lax.broadcasted_iota(I32, (GATE_ROWS, L), 0) >= H
    ti = lax.broadcasted_iota(I32, (L, L), 0)
    si = lax.broadcasted_iota(I32, (L, L), 1)
    tril = jnp.where(si <= ti, 1.0, 0.0).astype(BF16)
    triu = jnp.where(si >= ti, 1.0, 0.0).astype(BF16)
    for c in range(tm // L):
        rows = slice(c * L, (c + 1) * L)
        gcol_c = gcol[rows, :]
        fcol = jnp.where(is_f_col, _log_sigmoid(gcol_c), 0.0)
        parts = jnp.dot(tril, jnp.concatenate(_split3(fcol), axis=1).astype(BF16), preferred_element_type=F32)
        bcol = parts[:, :LANES] + parts[:, LANES:2 * LANES] + parts[:, 2 * LANES:]
        gc_ref[rows, :] = jnp.where(is_f_col, bcol, gcol_c)
        grow_c = grow[:, rows]
        frow = jnp.where(is_f_row, _log_sigmoid(grow_c), 0.0)
        parts = jnp.dot(jnp.concatenate(_split3(frow), axis=0).astype(BF16), triu, preferred_element_type=F32)
        brow = parts[:GATE_ROWS] + parts[GATE_ROWS:2 * GATE_ROWS] + parts[2 * GATE_ROWS:]
        gr_ref[:, rows] = jnp.where(is_f_row, brow, grow_c)

    @pl.when(pl.program_id(0) == pl.num_programs(0) - 1)
    def _():
        for fill in zero_fills:
            fill.wait()


def _inproj(x2d, g, w_in, gbias_col, B, S, n_slots, tmx):
    T = x2d.shape[0]
    tm = min(TM_INPROJ, S)
    tiles_per_seq = S // tm
    row = lambda w: pl.BlockSpec((tm, w), lambda i: (i, 0))
    full = lambda a: pl.BlockSpec(a.shape, lambda i: (0,) * a.ndim)
    return pl.pallas_call(
        _inproj_kernel,
        grid=(T // tm,),
        in_specs=[row(D_MODEL), full(g),
                  pl.BlockSpec(w_in.shape, lambda i: (0, 0), pipeline_mode=pl.Buffered(1)), full(gbias_col)],
        out_specs=[row(W_A_Q), row(2 * W_A_KV), row(2 * W_B), row(W_B), row(W_B), row(LANES),
                   pl.BlockSpec((None, GATE_ROWS, tm), lambda i: (i // tiles_per_seq, 0, i % tiles_per_seq)),
                   pl.BlockSpec(memory_space=pl.ANY)],
        out_shape=[jax.ShapeDtypeStruct((T, W_A_Q), BF16),
                   jax.ShapeDtypeStruct((T, 2 * W_A_KV), BF16),
                   jax.ShapeDtypeStruct((T, 2 * W_B), BF16),
                   jax.ShapeDtypeStruct((T, W_B), BF16),
                   jax.ShapeDtypeStruct((T, W_B), BF16),
                   jax.ShapeDtypeStruct((T, LANES), F32),
                   jax.ShapeDtypeStruct((B, GATE_ROWS, S), F32),
                   jax.ShapeDtypeStruct((n_slots, HALF), U32)],
        scratch_shapes=[pltpu.VMEM((D_MODEL, C_GATE + LANES), BF16), pltpu.VMEM((tmx, HALF), U32),
                        pltpu.SemaphoreType.DMA(())],
        compiler_params=pltpu.CompilerParams(dimension_semantics=("arbitrary",), vmem_limit_bytes=VMEM_LIMIT),
        name="inproj",
    )(x2d, g, w_in, gbias_col)


def _swa_block(q, kvc, kvp, bias_ref, sink_ref):
    kvp = kvp.astype(F32)
    kvc = kvc.astype(F32)
    kband = jnp.concatenate([kvp[:, :W_A_KV], kvc[:, :W_A_KV]], axis=0)
    vband = jnp.concatenate([kvp[:, W_A_KV:], kvc[:, W_A_KV:]], axis=0)
    lane = lax.broadcasted_iota(I32, (2 * BLOCK, LANES), 1)
    lo = lane < HEAD_DIM_A

    def placements(band):
        swapped = pltpu.roll(band, HEAD_DIM_A, axis=1)
        z = jnp.zeros_like(band)
        return {(0, 0): jnp.where(lo, band, z).astype(BF16), (0, 1): jnp.where(lo, z, swapped).astype(BF16),
                (1, 0): jnp.where(lo, swapped, z).astype(BF16), (1, 1): jnp.where(lo, z, band).astype(BF16)}

    kpl = placements(kband)
    vpl = placements(vband)

    scale = HEAD_DIM_A ** -0.5
    group = N_HEADS_A // N_KV_A
    tiles = []
    from_prev = (lax.broadcasted_iota(I32, (BLOCK, BLOCK), 1) > lax.broadcasted_iota(I32, (BLOCK, BLOCK), 0))

    for pair in range(N_HEADS_A // 2):
        qt = q[:, pair * LANES:(pair + 1) * LANES]
        acc = None
        for half in range(2):
            h = 2 * pair + half
            g = h // group
            s2 = lax.dot_general(qt, kpl[(g, half)], (((1,), (1,)), ((), ())), preferred_element_type=F32)
            s = jnp.where(from_prev, s2[:, :BLOCK], s2[:, BLOCK:])
            s = s * scale + bias_ref[h]
            sink = sink_ref[h]
            m = jnp.maximum(jnp.max(s, axis=-1, keepdims=True), sink)
            p = jnp.exp(s - m)
            denom = jnp.sum(p, axis=-1, keepdims=True) + jnp.exp(sink - m)
            p2 = jnp.concatenate([jnp.where(from_prev, p, 0.0), jnp.where(from_prev, 0.0, p)], axis=1)
            o = jnp.dot(p2.astype(BF16), vpl[(g, half)], preferred_element_type=F32) / denom
            acc = o if acc is None else acc + o
        tiles.append(acc.astype(BF16))
    return tiles


CONV_HALO = 16
SEQ_SUB = 2


def _seqmix_kernel(sink_ref, qkc_ref, qkp_ref, vb_ref, ob_ref, gc_ref, gr_ref, qa_ref, kvc_ref, kvp_ref,
                   cw_ref, cb_ref, nrm_ref, bias0_ref, bias_ref, o_ref, state_ref, m_ref):
    c = pl.program_id(0)
    B = qkc_ref.shape[0]
    H, D, L = N_HEADS_B, HEAD_DIM_B, CHUNK

    @pl.when(c == 0)
    def _():
        state_ref[...] = jnp.zeros_like(state_ref)
        m_ref[...] = jnp.zeros_like(m_ref)

    rr = lax.broadcasted_iota(I32, (L, CONV_HALO + L), 0)
    cc = lax.broadcasted_iota(I32, (L, CONV_HALO + L), 1)
    shifts = {delay: jnp.where(cc == rr + (CONV_HALO - delay), 1.0, 0.0).astype(BF16)
              for delay in range(1, CONV_WIDTH)}
    ti = lax.broadcasted_iota(I32, (L, L), 0)
    si = lax.broadcasted_iota(I32, (L, L), 1)
    tri = si <= ti
    ones_blk = jnp.ones((L, D), BF16)

    def conv_silu(b, u):
        if u == 0:
            prev = qkp_ref[b]
            prev = jnp.where(c > 0, prev, jnp.zeros_like(prev))
        else:
            prev = qkc_ref[b, u * L - CONV_HALO:u * L, :]
        cur = qkc_ref[b, u * L:(u + 1) * L, :]
        ext = jnp.concatenate([prev, cur], axis=0)
        y = cb_ref[...] + cw_ref[CONV_WIDTH - 1:CONV_WIDTH, :] * cur.astype(F32)
        for delay in range(1, CONV_WIDTH):
            tap = CONV_WIDTH - 1 - delay
            y = y + cw_ref[tap:tap + 1, :] * jnp.dot(shifts[delay], ext, preferred_element_type=F32)
        return y * jax.nn.sigmoid(y)

    states = {(b, h): state_ref[b, h] for b in range(B) for h in range(H)}
    ms = {(b, h): m_ref[b, h:h + 1, 0:1] for b in range(B) for h in range(H)}

    n_sub = qkc_ref.shape[1] // L
    for u, b, h in [(u, b, h) for u in range(n_sub) for b in range(B) for h in range(H)]:
        rows = slice(u * L, (u + 1) * L)
        if h == 0:
            kvp = kvp_ref[b] if u == 0 else kvc_ref[b, (u - 1) * L:u * L, :]
            tiles = _swa_block(qa_ref[b, rows, :], kvc_ref[b, rows, :], kvp, bias0_ref if u == 0 else bias_ref,
                               sink_ref)
            for pair, tile in enumerate(tiles):
                o_ref[b, rows, pair * LANES:(pair + 1) * LANES] = tile
            qk = conv_silu(b, u)
            gcol = gc_ref[b, rows, :]
            grow = gr_ref[b, :, rows]
        qh = (qk[:, h * D:(h + 1) * D] * (D ** -0.5)).astype(BF16)
        k_t = qk[:, W_B + h * D:W_B + (h + 1) * D].T
        v1 = jnp.concatenate([vb_ref[b, rows, h * D:(h + 1) * D], ones_blk], axis=-1)
        b_r = grow[H + h:H + h + 1, :]
        g_r = grow[h:h + 1, :] - b_r
        b_c = gcol[:, H + h:H + h + 1]
        m_prev = ms[b, h]
        state = states[b, h]

        gmat = jnp.where(tri, g_r, NEG_INF)
        m_c = jnp.maximum(jnp.max(gmat, axis=-1, keepdims=True), m_prev)
        a_inter = jnp.exp(m_prev - m_c)
        sc = jnp.dot(qh, k_t.astype(BF16), preferred_element_type=F32) * jnp.exp(gmat - m_c)
        tot = (jnp.dot(sc.astype(BF16), v1, preferred_element_type=F32)
               + a_inter * jnp.dot(qh, state.astype(BF16), preferred_element_type=F32))
        num = tot[:, :D]
        den = tot[:, D:]
        hh = num / jnp.maximum(jnp.abs(den), jnp.exp(-(b_c + m_c)))

        b_last = b_r[:, L - 1:L]
        m_new = jnp.maximum(b_last + m_prev, b_last + jnp.max(g_r, axis=-1, keepdims=True))
        w_r = jnp.exp(g_r + (b_last - m_new))
        decay = jnp.exp(b_last + m_prev - m_new)
        upd = jnp.dot((k_t * w_r).astype(BF16), v1, preferred_element_type=F32)
        states[b, h] = decay * state + upd
        ms[b, h] = m_new

        og = jax.nn.sigmoid(ob_ref[b, rows, h * D:(h + 1) * D].astype(F32))
        hb = og * hh
        hb = hb * lax.rsqrt(jnp.mean(hb * hb, axis=-1, keepdims=True) + EPS)
        o_ref[b, rows, W_A_Q + h * D:W_A_Q + (h + 1) * D] = (hb * nrm_ref[:, h * D:(h + 1) * D]).astype(BF16)

    for b, h in states:
        state_ref[b, h] = states[b, h]
        m_ref[b, h:h + 1, :] = jnp.broadcast_to(ms[b, h], (1, LANES))


def _seqmix(qkb, vb, ob, gc, gr, qa, kva, conv_w, conv_b, nrm, bias, sinks):
    assert CHUNK == BLOCK
    B, S, _ = qkb.shape
    rows = min(SEQ_SUB * CHUNK, S)
    halo_per_step = rows // CONV_HALO
    blk = lambda w: pl.BlockSpec((B, rows, w), lambda c: (0, c, 0))
    full = lambda a: pl.BlockSpec(a.shape, lambda c: (0,) * a.ndim)
    bias_variant = lambda pick: pl.BlockSpec((None,) + bias.shape[1:], lambda c: (pick(c), 0, 0, 0))
    return pl.pallas_call(
        _seqmix_kernel,
        grid=(S // rows,),
        in_specs=[pl.BlockSpec(memory_space=pltpu.SMEM),
                  blk(2 * W_B),
                  pl.BlockSpec((B, CONV_HALO, 2 * W_B), lambda c: (0, jnp.maximum(c * halo_per_step - 1, 0), 0)),
                  blk(W_B), blk(W_B), blk(LANES),
                  pl.BlockSpec((B, GATE_ROWS, rows), lambda c: (0, 0, c)),
                  blk(W_A_Q), blk(2 * W_A_KV),
                  pl.BlockSpec((B, BLOCK, 2 * W_A_KV), lambda c: (0, jnp.maximum(c * (rows // BLOCK) - 1, 0), 0)),
                  full(conv_w), full(conv_b), full(nrm),
                  bias_variant(lambda c: jnp.minimum(c, 1)), bias_variant(lambda c: 1)],
        out_specs=blk(W_A_Q + W_B),
        out_shape=jax.ShapeDtypeStruct((B, S, W_A_Q + W_B), BF16),
        scratch_shapes=[pltpu.VMEM((B, N_HEADS_B, HEAD_DIM_B, 2 * HEAD_DIM_B), F32),
                        pltpu.VMEM((B, GATE_ROWS, LANES), F32)],
        compiler_params=pltpu.CompilerParams(dimension_semantics=("arbitrary",), vmem_limit_bytes=VMEM_LIMIT),
        name="seqmix",
    )(sinks, qkb, qkb, vb, ob, gc, gr, qa, kva, kva, conv_w, conv_b, nrm, bias, bias)


def _memkv_kernel(mem_ref, g_ref, wk_ref, wv_ref, k_ref, v_ref):
    hm = _rms(mem_ref[...], g_ref[...]).astype(BF16)
    k_ref[...] = jnp.dot(hm, wk_ref[...], preferred_element_type=F32).astype(BF16)
    v_ref[...] = jnp.dot(hm, wv_ref[...], preferred_element_type=F32).astype(BF16)


def _memkv(mem2d, g, wk, wv, B):
    full = lambda a: pl.BlockSpec(a.shape, lambda b: (0,) * a.ndim)
    blk = pl.BlockSpec((N_MEM, D_MODEL), lambda b: (b, 0))
    return pl.pallas_call(
        _memkv_kernel,
        grid=(B,),
        in_specs=[blk, full(g), full(wk), full(wv)],
        out_specs=[blk, blk],
        out_shape=[jax.ShapeDtypeStruct((B * N_MEM, D_MODEL), BF16)] * 2,
        compiler_params=pltpu.CompilerParams(dimension_semantics=("parallel",), vmem_limit_bytes=VMEM_LIMIT),
        name="memkv",
    )(mem2d, g, wk, wv)


ROUTE_E1, ROUTE_E2, ROUTE_G1, ROUTE_G2, ROUTE_R1, ROUTE_R2 = 0, 1, 2, 3, 4, 5
ROUTER_GROUP_COL = N_EXPERTS
ROUTER_ROWS = -(-(N_EXPERTS + N_GROUPS) // SUBLANES) * SUBLANES


def _mid_kernel(x_ref, mix_ref, wo_ref, gx_ref, wq_ref, ck_ref, cv_ref, wco_ref, gz_ref, wr_ref, br_ref,
                x2_ref, hz_ref, route_ref, rrows_ref, counts_ref, cnt_ref):
    @pl.when(pl.program_id(0) == 0)
    def _():
        cnt_ref[...] = jnp.zeros_like(cnt_ref)

    x1 = x_ref[...] + jnp.dot(mix_ref[...], wo_ref[...], preferred_element_type=F32)

    hc = _rms(x1, gx_ref[...]).astype(BF16)
    cq = jnp.dot(hc, wq_ref[...], preferred_element_type=F32).astype(BF16)
    scale = HEAD_DIM_X ** -0.5
    heads = []
    for h in range(N_HEADS_X):
        sl = slice(h * HEAD_DIM_X, (h + 1) * HEAD_DIM_X)
        s = lax.dot_general(cq[:, sl], ck_ref[:, sl], (((1,), (1,)), ((), ())), preferred_element_type=F32) * scale
        p = jnp.exp(s - jnp.max(s, axis=-1, keepdims=True))
        co = jnp.dot(p.astype(BF16), cv_ref[:, sl], preferred_element_type=F32) / jnp.sum(p, axis=-1, keepdims=True)
        heads.append(co.astype(BF16))
    x2 = x1 + jnp.dot(jnp.concatenate(heads, axis=-1), wco_ref[...], preferred_element_type=F32)
    x2_ref[...] = x2

    hz = _rms(x2, gz_ref[...])
    hz_ref[...] = _pack_halves(hz)
    lg = jnp.dot(hz.astype(BF16), wr_ref[...], preferred_element_type=F32) + br_ref[...]
    tm = lg.shape[0]
    lt = jnp.transpose(lg)[0:ROUTER_ROWS, :]
    row = lax.broadcasted_iota(I32, lt.shape, 0)
    big = jnp.int32(ROUTER_ROWS)
    is_g = (row >= ROUTER_GROUP_COL) & (row < ROUTER_GROUP_COL + N_GROUPS)
    gl = jnp.where(is_g, lt, NEG_INF)
    gmax = jnp.max(gl, axis=0, keepdims=True)
    gsum = jnp.sum(jnp.exp(gl - gmax), axis=0, keepdims=True)
    g_prob = 1.0 / gsum
    g_idx = jnp.min(jnp.where(gl == gmax, row - ROUTER_GROUP_COL, big), axis=0, keepdims=True)
    sel = (row < N_EXPERTS) & ((row // EXPERTS_PER_GROUP) == g_idx)
    el = jnp.where(sel, lt, NEG_INF)
    m1 = jnp.max(el, axis=0, keepdims=True)
    i1 = jnp.min(jnp.where(el == m1, row, big), axis=0, keepdims=True)
    el2 = jnp.where(row == i1, NEG_INF, el)
    m2 = jnp.max(el2, axis=0, keepdims=True)
    i2 = jnp.min(jnp.where(el2 == m2, row, big), axis=0, keepdims=True)
    z = jnp.sum(jnp.exp(el - m1), axis=0, keepdims=True)
    p1 = 1.0 / z
    p2 = jnp.exp(m2 - m1) / z
    g1 = g_prob * (p1 / (p1 + p2))
    g2 = g_prob * (p2 / (p1 + p2))

    used = jnp.where((row == i1) | (row == i2), 1.0, 0.0)
    t_from = lax.broadcasted_iota(I32, (tm, tm), 0)
    t_to = lax.broadcasted_iota(I32, (tm, tm), 1)
    earlier = jnp.where(t_from < t_to, 1.0, 0.0).astype(BF16)
    before = jnp.dot(used.astype(BF16), earlier, preferred_element_type=F32) + cnt_ref[:, 0:1]
    r1 = jnp.sum(jnp.where(row == i1, before, 0.0), axis=0, keepdims=True)
    r2 = jnp.sum(jnp.where(row == i2, before, 0.0), axis=0, keepdims=True)
    cnt_ref[...] = cnt_ref[...] + jnp.sum(used, axis=1, keepdims=True)
    counts_ref[...] = cnt_ref[...]

    rec_row = lax.broadcasted_iota(I32, (SUBLANES, tm), 0)
    rec = jnp.zeros((SUBLANES, tm), F32)
    for c, v in ((ROUTE_E1, i1.astype(F32)), (ROUTE_E2, i2.astype(F32)), (ROUTE_G1, g1), (ROUTE_G2, g2),
                 (ROUTE_R1, r1), (ROUTE_R2, r2)):
        rec = jnp.where(rec_row == c, v, rec)
    rrows_ref[...] = rec
    route_ref[...] = jnp.transpose(jnp.concatenate([rec, jnp.zeros((LANES - SUBLANES, tm), F32)], axis=0))


def _mid(x2d, mix, wo, gx, wq, ck, cv, wco, gz, wr, br, B, S):
    T = B * S
    tm = min(TM_MID, S)
    per_b = S // tm
    row = lambda w: pl.BlockSpec((tm, w), lambda i: (i, 0))
    full = lambda a: pl.BlockSpec(a.shape, lambda i: (0,) * a.ndim)
    kvspec = pl.BlockSpec((N_MEM, D_MODEL), lambda i: (i // per_b, 0))
    return pl.pallas_call(
        _mid_kernel,
        grid=(T // tm,),
        in_specs=[row(D_MODEL), row(W_A_Q + W_B), full(wo), full(gx), full(wq), kvspec, kvspec,
                  full(wco), full(gz), full(wr), full(br)],
        out_specs=[row(D_MODEL), row(HALF), row(LANES), pl.BlockSpec((SUBLANES, tm), lambda i: (0, i)),
                   pl.BlockSpec((ROUTER_ROWS, LANES), lambda i: (0, 0))],
        out_shape=[jax.ShapeDtypeStruct((T, D_MODEL), F32),
                   jax.ShapeDtypeStruct((T, HALF), U32),
                   jax.ShapeDtypeStruct((T, LANES), F32),
                   jax.ShapeDtypeStruct((SUBLANES, T), F32),
                   jax.ShapeDtypeStruct((ROUTER_ROWS, LANES), F32)],
        scratch_shapes=[pltpu.VMEM((ROUTER_ROWS, LANES), F32)],
        compiler_params=pltpu.CompilerParams(dimension_semantics=("arbitrary",), vmem_limit_bytes=VMEM_LIMIT),
        name="mid",
    )(x2d, mix, wo, gx, wq, ck, cv, wco, gz, wr, br)


def _dispatch_kernel(pos_ref, hz_ref, xs_zeroed, xs_hbm, sem):
    del xs_zeroed
    tm = hz_ref.shape[0]

    for r in range(tm):
        for k in range(TOP_K):
            pltpu.make_async_copy(hz_ref.at[pl.ds(r, 1)], xs_hbm.at[pl.ds(pos_ref[0, 0, k * tm + r], 1)],
                                  sem).start(priority=k % 2)
    for k in range(TOP_K):
        pltpu.make_async_copy(hz_ref, xs_hbm.at[pl.ds(0, tm)], sem).wait()


def _dispatch(hz_packed, pos, xs_zeroed):
    T = hz_packed.shape[0]
    tm = pos.shape[2] // TOP_K
    return pl.pallas_call(
        _dispatch_kernel,
        grid=(T // tm,),
        in_specs=[pl.BlockSpec((1, 1, TOP_K * tm), lambda i: (i, 0, 0), memory_space=pltpu.SMEM),
                  pl.BlockSpec((tm, HALF), lambda i: (i, 0)),
                  pl.BlockSpec(memory_space=pl.ANY)],
        out_specs=pl.BlockSpec(memory_space=pl.ANY),
        out_shape=jax.ShapeDtypeStruct(xs_zeroed.shape, U32),
        scratch_shapes=[pltpu.SemaphoreType.DMA(())],
        input_output_aliases={2: 0},
        compiler_params=pltpu.CompilerParams(dimension_semantics=("arbitrary",), vmem_limit_bytes=VMEM_LIMIT),
        name="dispatch",
    )(pos, hz_packed, xs_zeroed)


def _expert_kernel(te_ref, nt_ref, first_ref, slot_ref, next_ref, rows_ref, xs_ref, wg_hbm, wu_hbm, wd_hbm, ys_ref,
                   wg32, wu32, wd32, wgb, wub, wdb, wsem):
    i = pl.program_id(0)
    nt = nt_ref[0]
    tmx = xs_ref.shape[0]

    def fetch(e, s):
        return [pltpu.make_async_copy(src.at[e], dst.at[s], wsem.at[s])
                for src, dst in ((wg_hbm, wg32), (wu_hbm, wu32), (wd_hbm, wd32))]

    @pl.when(i < nt)
    def _():
        @pl.when(i == 0)
        def _():
            for cp in fetch(te_ref[0], 0):
                cp.start()

        @pl.when(first_ref[i] != 0)
        def _():
            s = slot_ref[i]
            for cp in fetch(te_ref[i], s):
                cp.wait()

            @pl.when(next_ref[i] >= 0)
            def _():
                for cp in fetch(next_ref[i], 1 - s):
                    cp.start(priority=1)

            wgb[...] = wg32[s].astype(BF16)
            wub[...] = wu32[s].astype(BF16)
            wdb[...] = wd32[s].astype(BF16)

        nrows = rows_ref[i]
        for m in range(EXPERT_ROW_STEP, tmx + 1, EXPERT_ROW_STEP):
            @pl.when((nrows > m - EXPERT_ROW_STEP) & (nrows <= m))
            def _():
                x = _unpack_halves(xs_ref[0:m, :]).astype(BF16)
                hg = jnp.dot(x, wgb[...], preferred_element_type=F32)
                hu = jnp.dot(x, wub[...], preferred_element_type=F32)
                a = (hg * jax.nn.sigmoid(hg) * hu).astype(BF16)
                ys_ref[0:m, :] = _pack_halves(jnp.dot(a, wdb[...], preferred_element_type=F32))
                if m < tmx:
                    ys_ref[m:, :] = jnp.zeros((tmx - m, HALF), U32)

    @pl.when(i >= nt)
    def _():
        ys_ref[...] = jnp.zeros_like(ys_ref)


def _experts(xs, w_gate, w_up, w_down, tile_expert, ntiles, run_first, run_slot, run_next, tile_rows, tmx):
    n_tiles_max = tile_expert.shape[0]
    hbm = pl.BlockSpec(memory_space=pl.ANY)
    grid_spec = pltpu.PrefetchScalarGridSpec(
        num_scalar_prefetch=6,
        grid=(n_tiles_max,),
        in_specs=[pl.BlockSpec((tmx, HALF), lambda i, te, nt, *_: (jnp.minimum(i, nt[0] - 1), 0)), hbm, hbm, hbm],
        out_specs=pl.BlockSpec((tmx, HALF), lambda i, *_: (i, 0)),
        scratch_shapes=[pltpu.VMEM((2, D_MODEL, D_EXPERT), F32),
                        pltpu.VMEM((2, D_MODEL, D_EXPERT), F32),
                        pltpu.VMEM((2, D_EXPERT, D_MODEL), F32),
                        pltpu.VMEM((D_MODEL, D_EXPERT), BF16),
                        pltpu.VMEM((D_MODEL, D_EXPERT), BF16),
                        pltpu.VMEM((D_EXPERT, D_MODEL), BF16),
                        pltpu.SemaphoreType.DMA((2,))],
    )
    return pl.pallas_call(
        _expert_kernel,
        grid_spec=grid_spec,
        out_shape=jax.ShapeDtypeStruct(xs.shape, U32),
        compiler_params=pltpu.CompilerParams(dimension_semantics=("arbitrary",), vmem_limit_bytes=VMEM_LIMIT),
        name="experts",
    )(tile_expert, ntiles, run_first, run_slot, run_next, tile_rows, xs, w_gate, w_up, w_down)


def _final_kernel(posc_ref, posn_ref, x2_ref, route_ref, g_ref, ys_hbm, o_ref, ybuf, sem):
    i = pl.program_id(0)
    n = pl.num_programs(0)
    tm = x2_ref.shape[0]
    slot = i % 2

    def issue(pos_ref, s):
        for r in range(tm):
            for k in range(TOP_K):
                pltpu.make_async_copy(ys_hbm.at[pl.ds(pos_ref[0, 0, k * tm + r], 1)],
                                      ybuf.at[s, k, pl.ds(r, 1)], sem.at[s]).start(priority=k % 2)

    def wait(s):
        for k in range(TOP_K):
            pltpu.make_async_copy(ys_hbm.at[pl.ds(0, tm)], ybuf.at[s, k], sem.at[s]).wait()

    @pl.when(i == 0)
    def _():
        issue(posc_ref, 0)

    wait(slot)

    for s in range(2):
        @pl.when(slot == s)
        def _():
            issue(posn_ref, 1 - s)

    r = route_ref[...]
    g1 = r[:, ROUTE_G1:ROUTE_G1 + 1]
    g2 = r[:, ROUTE_G2:ROUTE_G2 + 1]
    xo = x2_ref[...] + g1 * _unpack_halves(ybuf[slot, 0]) + g2 * _unpack_halves(ybuf[slot, 1])
    o_ref[...] = _rms(xo, g_ref[...])

    @pl.when(i == n - 1)
    def _():
        wait(1 - slot)


def _final(x2, ys, pos, route, g):
    T = x2.shape[0]
    nblk = pos.shape[0]
    tm = T // nblk
    row = lambda w: pl.BlockSpec((tm, w), lambda i: (i, 0))
    return pl.pallas_call(
        _final_kernel,
        grid=(nblk,),
        in_specs=[pl.BlockSpec((1, 1, TOP_K * tm), lambda i: (i, 0, 0), memory_space=pltpu.SMEM),
                  pl.BlockSpec((1, 1, TOP_K * tm), lambda i: (jnp.minimum(i + 1, nblk - 1), 0, 0),
                               memory_space=pltpu.SMEM),
                  row(D_MODEL), row(LANES), pl.BlockSpec(g.shape, lambda i: (0, 0)),
                  pl.BlockSpec(memory_space=pl.ANY)],
        out_specs=row(D_MODEL),
        out_shape=jax.ShapeDtypeStruct((T, D_MODEL), F32),
        scratch_shapes=[pltpu.VMEM((2, TOP_K, tm, HALF), U32), pltpu.SemaphoreType.DMA((2,))],
        compiler_params=pltpu.CompilerParams(dimension_semantics=("arbitrary",), vmem_limit_bytes=VMEM_LIMIT),
        name="final",
    )(pos, pos, x2, route, g, ys)


def _band_bias(table):
    assert WINDOW == BLOCK
    i = jnp.arange(BLOCK)[:, None]
    j = jnp.arange(2 * BLOCK)[None, :]
    n = jnp.maximum(i + BLOCK - j, 0)
    nf = jnp.maximum(n, 1).astype(F32)
    large = MAX_EXACT + (jnp.log(nf / MAX_EXACT) / math.log(MAX_DISTANCE / MAX_EXACT)
                         * (NUM_BUCKETS - MAX_EXACT)).astype(I32)
    large = jnp.minimum(large, NUM_BUCKETS - 1)
    bucket = jnp.where(n < MAX_EXACT, n, large)
    onehot = (bucket[:, :, None] == jnp.arange(NUM_BUCKETS)[None, None, :]).astype(F32)
    bias = jnp.einsum("ijb,bh->hij", onehot, table.astype(F32), precision=lax.Precision.HIGHEST)
    from_prev = (jnp.arange(BLOCK)[None, :] > i)[None]
    prev, cur = bias[:, :, :BLOCK], bias[:, :, BLOCK:]
    return jnp.stack([jnp.where(from_prev, NEG_INF, cur), jnp.where(from_prev, prev, cur)])


def _dispatch_plan(route_rows, counts_f, tmx, n_tiles_max, tm_rows):
    T = route_rows.shape[1]
    experts = jnp.arange(N_EXPERTS, dtype=I32)
    counts = counts_f[:N_EXPERTS, 0].astype(I32)
    ptiles = (counts + tmx - 1) // tmx
    tile_end = jnp.cumsum(ptiles)
    nt = tile_end[-1]
    row_off = (tile_end - ptiles) * tmx

    def slot(e_row, r_row):
        e = route_rows[e_row].astype(I32)
        off = jnp.sum(jnp.where(e[None, :] == experts[:, None], row_off[:, None], 0), axis=0)
        return (off + route_rows[r_row].astype(I32)).reshape(T // tm_rows, 1, tm_rows)

    pos = jnp.concatenate([slot(ROUTE_E1, ROUTE_R1), slot(ROUTE_E2, ROUTE_R2)], axis=2)

    tile_ids = jnp.arange(n_tiles_max, dtype=I32)
    expert_of = lambda t: jnp.sum((tile_end[None, :] <= t[:, None]).astype(I32), axis=1)
    te = expert_of(jnp.minimum(tile_ids, nt - 1))

    used = ptiles > 0
    run_first = (jnp.any((tile_ids[:, None] == (tile_end - ptiles)[None, :]) & used[None, :], axis=1)
                 & (tile_ids < nt)).astype(I32)
    run_slot = (jnp.cumsum(run_first) - 1) % 2
    later_used = used[None, :] & (experts[None, :] > experts[:, None])
    next_of = jnp.min(jnp.where(later_used, experts[None, :], N_EXPERTS), axis=1)
    next_of = jnp.where(next_of < N_EXPERTS, next_of, -1)
    run_next = jnp.sum(jnp.where(te[:, None] == experts[None, :], next_of[None, :], 0), axis=1)

    of_tile = lambda v: jnp.sum(jnp.where(te[:, None] == experts[None, :], v[None, :], 0), axis=1)
    tile_rows = jnp.clip(of_tile(counts) - (tile_ids - of_tile(tile_end - ptiles)) * tmx, 0, tmx)
    tile_rows = jnp.where(tile_ids < nt, tile_rows, 0)
    return pos, te, nt.reshape(1), run_first, run_slot.astype(I32), run_next.astype(I32), tile_rows.astype(I32)


def kernel(x, mem, rel_bias_table, norm_mix, w_in, attn_sinks, conv_w, conv_b, gate_bias_i, gate_bias_f, mlstm_norm, w_out, norm_cross, norm_mem, w_cq, w_ck, w_cv, w_co, norm_moe, w_router_group, b_router_group, w_router_expert, b_router_expert, w_exp_gate, w_exp_up, w_exp_down, norm_final):
    B, S, _ = x.shape
    T = B * S
    depth = w_in.shape[0]
    x2d = x.reshape(T, D_MODEL)
    mem2d = mem.reshape(B * N_MEM, D_MODEL)
    bias = _band_bias(rel_bias_table)

    tmx = min(TM_EXPERT, T)
    n_tiles_max = (T * TOP_K) // tmx + N_EXPERTS
    tm_rows = min(TM_ROWDMA, T)

    assert depth == 1, "the final combine is fused with the final norm: single layer only"
    l = 0
    gb = jnp.concatenate([gate_bias_i[l], gate_bias_f[l]]).astype(F32)
    gbias_col = jnp.pad(gb, (0, LANES - GATE_ROWS))[None, :]
    qa, kva, qkb, vb, ob, gc, gr, xs_zeroed = _inproj(x2d, norm_mix[l][None, :], w_in[l], gbias_col, B, S,
                                                      n_tiles_max * tmx, tmx)

    per_seq = lambda a: a.reshape(B, S, a.shape[-1])
    mix = _seqmix(per_seq(qkb), per_seq(vb), per_seq(ob), per_seq(gc), gr, per_seq(qa), per_seq(kva),
                  conv_w[l][:, 0, :].astype(F32), conv_b[l][None, :].astype(F32),
                  mlstm_norm[l][None, :].astype(F32), bias, attn_sinks[l].astype(F32)).reshape(T, W_A_Q + W_B)

    ck, cv = _memkv(mem2d, norm_mem[l][None, :], w_ck[l].astype(BF16), w_cv[l].astype(BF16), B)

    wr = jnp.pad(jnp.concatenate([w_router_expert[l], w_router_group[l]], axis=1),
                 ((0, 0), (0, LANES - N_EXPERTS - N_GROUPS))).astype(BF16)
    br = jnp.pad(jnp.concatenate([b_router_expert[l], b_router_group[l]]),
                 (0, LANES - N_EXPERTS - N_GROUPS)).astype(F32)[None, :]
    x2, hz_packed, route, route_rows, counts = _mid(
        x2d, mix, w_out[l].astype(BF16), norm_cross[l][None, :], w_cq[l].astype(BF16), ck, cv,
        w_co[l].astype(BF16), norm_moe[l][None, :], wr, br, B, S)

    pos, te, nt, run_first, run_slot, run_next, tile_rows = _dispatch_plan(route_rows, counts, tmx, n_tiles_max,
                                                                           tm_rows)
    xs = _dispatch(hz_packed, pos, xs_zeroed)
    ys = _experts(xs, w_exp_gate[l], w_exp_up[l], w_exp_down[l], te, nt, run_first, run_slot, run_next, tile_rows,
                  tmx)
    out = _final(x2, ys, pos, route, norm_final[None, :])
    return out.reshape(B, S, D_MODEL)
```

```python
import math

import jax
import jax.numpy as jnp
from jax import lax
from jax.experimental import pallas as pl
from jax.experimental.pallas import tpu as pltpu

F32 = jnp.float32
BF16 = jnp.bfloat16
U32 = jnp.uint32
I32 = jnp.int32

D_MODEL = 1024
N_MEM = 256
N_HEADS_A = 8
N_KV_A = 2
HEAD_DIM_A = 64
BLOCK = 128
WINDOW = 128
NUM_BUCKETS = 32
MAX_EXACT = NUM_BUCKETS // 2
MAX_DISTANCE = 128
N_HEADS_B = 4
HEAD_DIM_B = 128
CHUNK = 128
CONV_WIDTH = 4
N_HEADS_X = 4
HEAD_DIM_X = D_MODEL // N_HEADS_X
N_GROUPS = 4
EXPERTS_PER_GROUP = 8
N_EXPERTS = N_GROUPS * EXPERTS_PER_GROUP
TOP_K = 2
D_EXPERT = 512
EPS = 1e-6
NEG_INF = -1e30

W_A_Q = N_HEADS_A * HEAD_DIM_A
W_A_KV = N_KV_A * HEAD_DIM_A
W_B = N_HEADS_B * HEAD_DIM_B
C_QA = 0
C_KVA = C_QA + W_A_Q
C_QKB = C_KVA + 2 * W_A_KV
C_VB = C_QKB + 2 * W_B
C_OB = C_VB + W_B
C_GATE = C_OB + W_B
D_IN = C_GATE + 2 * N_HEADS_B

LANES = 128
SUBLANES = 8
GATE_ROWS = 8
HALF = D_MODEL // 2

TM_INPROJ = 1024
TM_MID = 1024
TM_ROWDMA = 512
TM_EXPERT = 1024
EXPERT_ROW_STEP = 128

V7X_VMEM_BYTES = 64 * 1024 * 1024
VMEM_LIMIT = V7X_VMEM_BYTES * 3 // 4


def _rms(xf, g):
    return xf * lax.rsqrt(jnp.mean(xf * xf, axis=-1, keepdims=True) + EPS) * g


def _pack_halves(v):
    b = pltpu.bitcast(v.astype(BF16).astype(F32), U32)
    return (b[:, :HALF] >> 16) | b[:, HALF:]


def _unpack_halves(p):
    lo = pltpu.bitcast(p << 16, F32)
    hi = pltpu.bitcast(p & jnp.uint32(0xFFFF0000), F32)
    return jnp.concatenate([lo, hi], axis=-1)


def _log_sigmoid(z):
    return jnp.minimum(z, 0.0) - jnp.log1p(jnp.exp(-jnp.abs(z)))


def _split3(v):
    hi = v.astype(BF16).astype(F32)
    rest = v - hi
    mid = rest.astype(BF16).astype(F32)
    return hi, mid, (rest - mid).astype(BF16).astype(F32)


def _inproj_kernel(x_ref, g_ref, w32_ref, gbc_ref, qa_ref, kva_ref, qkb_ref, vb_ref, ob_ref, gc_ref, gr_ref,
                   xs_hbm, w_ref, zbuf, zsem):
    zero_fills = [pltpu.make_async_copy(zbuf, xs_hbm.at[pl.ds(t * zbuf.shape[0], zbuf.shape[0])], zsem)
                  for t in range(xs_hbm.shape[0] // zbuf.shape[0])]

    @pl.when(pl.program_id(0) == 0)
    def _():
        w_ref[:, C_GATE:] = jnp.zeros((D_MODEL, LANES), BF16)
        w_ref[:, :D_IN] = w32_ref[...].astype(BF16)
        zbuf[...] = jnp.zeros_like(zbuf)
        for fill in zero_fills:
            fill.start(priority=1)

    tm = x_ref.shape[0]
    h = _rms(x_ref[...], g_ref[...]).astype(BF16)

    def mm(lo, hi):
        return jnp.dot(h, w_ref[:, lo:hi], preferred_element_type=F32)

    qa_ref[...] = mm(C_QA, C_KVA).astype(BF16)
    kva_ref[...] = mm(C_KVA, C_QKB).astype(BF16)
    qkb_ref[...] = mm(C_QKB, C_VB).astype(BF16)
    vb_ref[...] = mm(C_VB, C_OB).astype(BF16)
    ob_ref[...] = mm(C_OB, C_GATE).astype(BF16)

    H, L = N_HEADS_B, CHUNK
    gcol = mm(C_GATE, C_GATE + LANES) + gbc_ref[...]
    grow = jnp.transpose(gcol)[0:GATE_ROWS, :]
    lane_c = lax.broadcasted_iota(I32, (L, LANES), 1)
    is_f_col = (lane_c >= H) & (lane_c < 2 * H)
    is_f_row = lax.broadcasted_iota(I32, (GATE_ROWS, L), 0) >= H
    ti = lax.broadcasted_iota(I32, (L, L), 0)
    si = lax.broadcasted_iota(I32, (L, L), 1)
    tril = jnp.where(si <= ti, 1.0, 0.0).astype(BF16)
    triu = jnp.where(si >= ti, 1.0, 0.0).astype(BF16)
    for c in range(tm // L):
        rows = slice(c * L, (c + 1) * L)
        gcol_c = gcol[rows, :]
        fcol = jnp.where(is_f_col, _log_sigmoid(gcol_c), 0.0)
        parts = jnp.dot(tril, jnp.concatenate(_split3(fcol), axis=1).astype(BF16), preferred_element_type=F32)
        bcol = parts[:, :LANES] + parts[:, LANES:2 * LANES] + parts[:, 2 * LANES:]
        gc_ref[rows, :] = jnp.where(is_f_col, bcol, gcol_c)
        grow_c = grow[:, rows]
        frow = jnp.where(is_f_row, _log_sigmoid(grow_c), 0.0)
        parts = jnp.dot(jnp.concatenate(_split3(frow), axis=0).astype(BF16), triu, preferred_element_type=F32)
        brow = parts[:GATE_ROWS] + parts[GATE_ROWS:2 * GATE_ROWS] + parts[2 * GATE_ROWS:]
        gr_ref[:, rows] = jnp.where(is_f_row, brow, grow_c)

    @pl.when(pl.program_id(0) == pl.num_programs(0) - 1)
    def _():
        for fill in zero_fills:
            fill.wait()


def _inproj(x2d, g, w_in, gbias_col, B, S, n_slots, tmx):
    T = x2d.shape[0]
    tm = min(TM_INPROJ, S)
    tiles_per_seq = S // tm
    row = lambda w: pl.BlockSpec((tm, w), lambda i: (i, 0))
    full = lambda a: pl.BlockSpec(a.shape, lambda i: (0,) * a.ndim)
    return pl.pallas_call(
        _inproj_kernel,
        grid=(T // tm,),
        in_specs=[row(D_MODEL), full(g),
                  pl.BlockSpec(w_in.shape, lambda i: (0, 0), pipeline_mode=pl.Buffered(1)), full(gbias_col)],
        out_specs=[row(W_A_Q), row(2 * W_A_KV), row(2 * W_B), row(W_B), row(W_B), row(LANES),
                   pl.BlockSpec((None, GATE_ROWS, tm), lambda i: (i // tiles_per_seq, 0, i % tiles_per_seq)),
                   pl.BlockSpec(memory_space=pl.ANY)],
        out_shape=[jax.ShapeDtypeStruct((T, W_A_Q), BF16),
                   jax.ShapeDtypeStruct((T, 2 * W_A_KV), BF16),
                   jax.ShapeDtypeStruct((T, 2 * W_B), BF16),
                   jax.ShapeDtypeStruct((T, W_B), BF16),
                   jax.ShapeDtypeStruct((T, W_B), BF16),
                   jax.ShapeDtypeStruct((T, LANES), F32),
                   jax.ShapeDtypeStruct((B, GATE_ROWS, S), F32),
                   jax.ShapeDtypeStruct((n_slots, HALF), U32)],
        scratch_shapes=[pltpu.VMEM((D_MODEL, C_GATE + LANES), BF16), pltpu.VMEM((tmx, HALF), U32),
                        pltpu.SemaphoreType.DMA(())],
        compiler_params=pltpu.CompilerParams(dimension_semantics=("arbitrary",), vmem_limit_bytes=VMEM_LIMIT),
        name="inproj",
    )(x2d, g, w_in, gbias_col)


def _swa_block(q, kvc, kvp, bias_ref, sink_ref):
    kvp = kvp.astype(F32)
    kvc = kvc.astype(F32)
    kband = jnp.concatenate([kvp[:, :W_A_KV], kvc[:, :W_A_KV]], axis=0)
    vband = jnp.concatenate([kvp[:, W_A_KV:], kvc[:, W_A_KV:]], axis=0)
    lane = lax.broadcasted_iota(I32, (2 * BLOCK, LANES), 1)
    lo = lane < HEAD_DIM_A

    def placements(band):
        swapped = pltpu.roll(band, HEAD_DIM_A, axis=1)
        z = jnp.zeros_like(band)
        return {(0, 0): jnp.where(lo, band, z).astype(BF16), (0, 1): jnp.where(lo, z, swapped).astype(BF16),
                (1, 0): jnp.where(lo, swapped, z).astype(BF16), (1, 1): jnp.where(lo, z, band).astype(BF16)}

    kpl = placements(kband)
    vpl = placements(vband)

    scale = HEAD_DIM_A ** -0.5
    group = N_HEADS_A // N_KV_A
    tiles = []
    from_prev = (lax.broadcasted_iota(I32, (BLOCK, BLOCK), 1) > lax.broadcasted_iota(I32, (BLOCK, BLOCK), 0))

    for pair in range(N_HEADS_A // 2):
        qt = q[:, pair * LANES:(pair + 1) * LANES]
        acc = None
        for half in range(2):
            h = 2 * pair + half
            g = h // group
            s2 = lax.dot_general(qt, kpl[(g, half)], (((1,), (1,)), ((), ())), preferred_element_type=F32)
            s = jnp.where(from_prev, s2[:, :BLOCK], s2[:, BLOCK:])
            s = s * scale + bias_ref[h]
            sink = sink_ref[h]
            m = jnp.maximum(jnp.max(s, axis=-1, keepdims=True), sink)
            p = jnp.exp(s - m)
            denom = jnp.sum(p, axis=-1, keepdims=True) + jnp.exp(sink - m)
            p2 = jnp.concatenate([jnp.where(from_prev, p, 0.0), jnp.where(from_prev, 0.0, p)], axis=1)
            o = jnp.dot(p2.astype(BF16), vpl[(g, half)], preferred_element_type=F32) / denom
            acc = o if acc is None else acc + o
        tiles.append(acc.astype(BF16))
    return tiles


CONV_HALO = 16
SEQ_SUB = 2


def _seqmix_kernel(sink_ref, qkc_ref, qkp_ref, vb_ref, ob_ref, gc_ref, gr_ref, qa_ref, kvc_ref, kvp_ref,
                   cw_ref, cb_ref, nrm_ref, bias0_ref, bias_ref, o_ref, state_ref, m_ref):
    c = pl.program_id(0)
    B = qkc_ref.shape[0]
    H, D, L = N_HEADS_B, HEAD_DIM_B, CHUNK

    @pl.when(c == 0)
    def _():
        state_ref[...] = jnp.zeros_like(state_ref)
        m_ref[...] = jnp.zeros_like(m_ref)

    rr = lax.broadcasted_iota(I32, (L, CONV_HALO + L), 0)
    cc = lax.broadcasted_iota(I32, (L, CONV_HALO + L), 1)
    shifts = {delay: jnp.where(cc == rr + (CONV_HALO - delay), 1.0, 0.0).astype(BF16)
              for delay in range(1, CONV_WIDTH)}
    ti = lax.broadcasted_iota(I32, (L, L), 0)
    si = lax.broadcasted_iota(I32, (L, L), 1)
    tri = si <= ti
    ones_blk = jnp.ones((L, D), BF16)

    def conv_silu(b, u):
        if u == 0:
            prev = qkp_ref[b]
            prev = jnp.where(c > 0, prev, jnp.zeros_like(prev))
        else:
            prev = qkc_ref[b, u * L - CONV_HALO:u * L, :]
        cur = qkc_ref[b, u * L:(u + 1) * L, :]
        ext = jnp.concatenate([prev, cur], axis=0)
        y = cb_ref[...] + cw_ref[CONV_WIDTH - 1:CONV_WIDTH, :] * cur.astype(F32)
        for delay in range(1, CONV_WIDTH):
            tap = CONV_WIDTH - 1 - delay
            y = y + cw_ref[tap:tap + 1, :] * jnp.dot(shifts[delay], ext, preferred_element_type=F32)
        return y * jax.nn.sigmoid(y)

    states = {(b, h): state_ref[b, h] for b in range(B) for h in range(H)}
    ms = {(b, h): m_ref[b, h:h + 1, 0:1] for b in range(B) for h in range(H)}

    n_sub = qkc_ref.shape[1] // L
    for u, b, h in [(u, b, h) for u in range(n_sub) for b in range(B) for h in range(H)]:
        rows = slice(u * L, (u + 1) * L)
        if h == 0:
            kvp = kvp_ref[b] if u == 0 else kvc_ref[b, (u - 1) * L:u * L, :]
            tiles = _swa_block(qa_ref[b, rows, :], kvc_ref[b, rows, :], kvp, bias0_ref if u == 0 else bias_ref,
                               sink_ref)
            for pair, tile in enumerate(tiles):
                o_ref[b, rows, pair * LANES:(pair + 1) * LANES] = tile
            qk = conv_silu(b, u)
            gcol = gc_ref[b, rows, :]
            grow = gr_ref[b, :, rows]
        qh = (qk[:, h * D:(h + 1) * D] * (D ** -0.5)).astype(BF16)
        k_t = qk[:, W_B + h * D:W_B + (h + 1) * D].T
        v1 = jnp.concatenate([vb_ref[b, rows, h * D:(h + 1) * D], ones_blk], axis=-1)
        b_r = grow[H + h:H + h + 1, :]
        g_r = grow[h:h + 1, :] - b_r
        b_c = gcol[:, H + h:H + h + 1]
        m_prev = ms[b, h]
        state = states[b, h]

        gmat = jnp.where(tri, g_r, NEG_INF)
        m_c = jnp.maximum(jnp.max(gmat, axis=-1, keepdims=True), m_prev)
        a_inter = jnp.exp(m_prev - m_c)
        sc = jnp.dot(qh, k_t.astype(BF16), preferred_element_type=F32) * jnp.exp(gmat - m_c)
        tot = (jnp.dot(sc.astype(BF16), v1, preferred_element_type=F32)
               + a_inter * jnp.dot(qh, state.astype(BF16), preferred_element_type=F32))
        num = tot[:, :D]
        den = tot[:, D:]
        hh = num / jnp.maximum(jnp.abs(den), jnp.exp(-(b_c + m_c)))

        b_last = b_r[:, L - 1:L]
        m_new = jnp.maximum(b_last + m_prev, b_last + jnp.max(g_r, axis=-1, keepdims=True))
        w_r = jnp.exp(g_r + (b_last - m_new))
        decay = jnp.exp(b_last + m_prev - m_new)
        upd = jnp.dot((k_t * w_r).astype(BF16), v1, preferred_element_type=F32)
        states[b, h] = decay * state + upd
        ms[b, h] = m_new

        og = jax.nn.sigmoid(ob_ref[b, rows, h * D:(h + 1) * D].astype(F32))
        hb = og * hh
        hb = hb * lax.rsqrt(jnp.mean(hb * hb, axis=-1, keepdims=True) + EPS)
        o_ref[b, rows, W_A_Q + h * D:W_A_Q + (h + 1) * D] = (hb * nrm_ref[:, h * D:(h + 1) * D]).astype(BF16)

    for b, h in states:
        state_ref[b, h] = states[b, h]
        m_ref[b, h:h + 1, :] = jnp.broadcast_to(ms[b, h], (1, LANES))


def _seqmix(qkb, vb, ob, gc, gr, qa, kva, conv_w, conv_b, nrm, bias, sinks):
    assert CHUNK == BLOCK
    B, S, _ = qkb.shape
    rows = min(SEQ_SUB * CHUNK, S)
    halo_per_step = rows // CONV_HALO
    blk = lambda w: pl.BlockSpec((B, rows, w), lambda c: (0, c, 0))
    full = lambda a: pl.BlockSpec(a.shape, lambda c: (0,) * a.ndim)
    bias_variant = lambda pick: pl.BlockSpec((None,) + bias.shape[1:], lambda c: (pick(c), 0, 0, 0))
    return pl.pallas_call(
        _seqmix_kernel,
        grid=(S // rows,),
        in_specs=[pl.BlockSpec(memory_space=pltpu.SMEM),
                  blk(2 * W_B),
                  pl.BlockSpec((B, CONV_HALO, 2 * W_B), lambda c: (0, jnp.maximum(c * halo_per_step - 1, 0), 0)),
                  blk(W_B), blk(W_B), blk(LANES),
                  pl.BlockSpec((B, GATE_ROWS, rows), lambda c: (0, 0, c)),
                  blk(W_A_Q), blk(2 * W_A_KV),
                  pl.BlockSpec((B, BLOCK, 2 * W_A_KV), lambda c: (0, jnp.maximum(c * (rows // BLOCK) - 1, 0), 0)),
                  full(conv_w), full(conv_b), full(nrm),
                  bias_variant(lambda c: jnp.minimum(c, 1)), bias_variant(lambda c: 1)],
        out_specs=blk(W_A_Q + W_B),
        out_shape=jax.ShapeDtypeStruct((B, S, W_A_Q + W_B), BF16),
        scratch_shapes=[pltpu.VMEM((B, N_HEADS_B, HEAD_DIM_B, 2 * HEAD_DIM_B), F32),
                        pltpu.VMEM((B, GATE_ROWS, LANES), F32)],
        compiler_params=pltpu.CompilerParams(dimension_semantics=("arbitrary",), vmem_limit_bytes=VMEM_LIMIT),
        name="seqmix",
    )(sinks, qkb, qkb, vb, ob, gc, gr, qa, kva, kva, conv_w, conv_b, nrm, bias, bias)


def _memkv_kernel(mem_ref, g_ref, wk_ref, wv_ref, k_ref, v_ref):
    hm = _rms(mem_ref[...], g_ref[...]).astype(BF16)
    k_ref[...] = jnp.dot(hm, wk_ref[...], preferred_element_type=F32).astype(BF16)
    v_ref[...] = jnp.dot(hm, wv_ref[...], preferred_element_type=F32).astype(BF16)


def _memkv(mem2d, g, wk, wv, B):
    full = lambda a: pl.BlockSpec(a.shape, lambda b: (0,) * a.ndim)
    blk = pl.BlockSpec((N_MEM, D_MODEL), lambda b: (b, 0))
    return pl.pallas_call(
        _memkv_kernel,
        grid=(B,),
        in_specs=[blk, full(g), full(wk), full(wv)],
        out_specs=[blk, blk],
        out_shape=[jax.ShapeDtypeStruct((B * N_MEM, D_MODEL), BF16)] * 2,
        compiler_params=pltpu.CompilerParams(dimension_semantics=("parallel",), vmem_limit_bytes=VMEM_LIMIT),
        name="memkv",
    )(mem2d, g, wk, wv)


ROUTE_E1, ROUTE_E2, ROUTE_G1, ROUTE_G2, ROUTE_R1, ROUTE_R2 = 0, 1, 2, 3, 4, 5
ROUTER_GROUP_COL = N_EXPERTS
ROUTER_ROWS = -(-(N_EXPERTS + N_GROUPS) // SUBLANES) * SUBLANES


def _mid_kernel(x_ref, mix_ref, wo_ref, gx_ref, wq_ref, ck_ref, cv_ref, wco_ref, gz_ref, wr_ref, br_ref,
                x2_ref, hz_ref, route_ref, rrows_ref, counts_ref, cnt_ref):
    @pl.when(pl.program_id(0) == 0)
    def _():
        cnt_ref[...] = jnp.zeros_like(cnt_ref)

    x1 = x_ref[...] + jnp.dot(mix_ref[...], wo_ref[...], preferred_element_type=F32)

    hc = _rms(x1, gx_ref[...]).astype(BF16)
    cq = jnp.dot(hc, wq_ref[...], preferred_element_type=F32).astype(BF16)
    scale = HEAD_DIM_X ** -0.5
    heads = []
    for h in range(N_HEADS_X):
        sl = slice(h * HEAD_DIM_X, (h + 1) * HEAD_DIM_X)
        s = lax.dot_general(cq[:, sl], ck_ref[:, sl], (((1,), (1,)), ((), ())), preferred_element_type=F32) * scale
        p = jnp.exp(s - jnp.max(s, axis=-1, keepdims=True))
        co = jnp.dot(p.astype(BF16), cv_ref[:, sl], preferred_element_type=F32) / jnp.sum(p, axis=-1, keepdims=True)
        heads.append(co.astype(BF16))
    x2 = x1 + jnp.dot(jnp.concatenate(heads, axis=-1), wco_ref[...], preferred_element_type=F32)
    x2_ref[...] = x2

    hz = _rms(x2, gz_ref[...])
    hz_ref[...] = _pack_halves(hz)
    lg = jnp.dot(hz.astype(BF16), wr_ref[...], preferred_element_type=F32) + br_ref[...]
    tm = lg.shape[0]
    lt = jnp.transpose(lg)[0:ROUTER_ROWS, :]
    row = lax.broadcasted_iota(I32, lt.shape, 0)
    big = jnp.int32(ROUTER_ROWS)
    is_g = (row >= ROUTER_GROUP_COL) & (row < ROUTER_GROUP_COL + N_GROUPS)
    gl = jnp.where(is_g, lt, NEG_INF)
    gmax = jnp.max(gl, axis=0, keepdims=True)
    gsum = jnp.sum(jnp.exp(gl - gmax), axis=0, keepdims=True)
    g_prob = 1.0 / gsum
    g_idx = jnp.min(jnp.where(gl == gmax, row - ROUTER_GROUP_COL, big), axis=0, keepdims=True)
    sel = (row < N_EXPERTS) & ((row // EXPERTS_PER_GROUP) == g_idx)
    el = jnp.where(sel, lt, NEG_INF)
    m1 = jnp.max(el, axis=0, keepdims=True)
    i1 = jnp.min(jnp.where(el == m1, row, big), axis=0, keepdims=True)
    el2 = jnp.where(row == i1, NEG_INF, el)
    m2 = jnp.max(el2, axis=0, keepdims=True)
    i2 = jnp.min(jnp.where(el2 == m2, row, big), axis=0, keepdims=True)
    z = jnp.sum(jnp.exp(el - m1), axis=0, keepdims=True)
    p1 = 1.0 / z
    p2 = jnp.exp(m2 - m1) / z
    g1 = g_prob * (p1 / (p1 + p2))
    g2 = g_prob * (p2 / (p1 + p2))

    used = jnp.where((row == i1) | (row == i2), 1.0, 0.0)
    t_from = lax.broadcasted_iota(I32, (tm, tm), 0)
    t_to = lax.broadcasted_iota(I32, (tm, tm), 1)
    earlier = jnp.where(t_from < t_to, 1.0, 0.0).astype(BF16)
    before = jnp.dot(used.astype(BF16), earlier, preferred_element_type=F32) + cnt_ref[:, 0:1]
    r1 = jnp.sum(jnp.where(row == i1, before, 0.0), axis=0, keepdims=True)
    r2 = jnp.sum(jnp.where(row == i2, before, 0.0), axis=0, keepdims=True)
    cnt_ref[...] = cnt_ref[...] + jnp.sum(used, axis=1, keepdims=True)
    counts_ref[...] = cnt_ref[...]

    rec_row = lax.broadcasted_iota(I32, (SUBLANES, tm), 0)
    rec = jnp.zeros((SUBLANES, tm), F32)
    for c, v in ((ROUTE_E1, i1.astype(F32)), (ROUTE_E2, i2.astype(F32)), (ROUTE_G1, g1), (ROUTE_G2, g2),
                 (ROUTE_R1, r1), (ROUTE_R2, r2)):
        rec = jnp.where(rec_row == c, v, rec)
    rrows_ref[...] = rec
    route_ref[...] = jnp.transpose(jnp.concatenate([rec, jnp.zeros((LANES - SUBLANES, tm), F32)], axis=0))


def _mid(x2d, mix, wo, gx, wq, ck, cv, wco, gz, wr, br, B, S):
    T = B * S
    tm = min(TM_MID, S)
    per_b = S // tm
    row = lambda w: pl.BlockSpec((tm, w), lambda i: (i, 0))
    full = lambda a: pl.BlockSpec(a.shape, lambda i: (0,) * a.ndim)
    kvspec = pl.BlockSpec((N_MEM, D_MODEL), lambda i: (i // per_b, 0))
    return pl.pallas_call(
        _mid_kernel,
        grid=(T // tm,),
        in_specs=[row(D_MODEL), row(W_A_Q + W_B), full(wo), full(gx), full(wq), kvspec, kvspec,
                  full(wco), full(gz), full(wr), full(br)],
        out_specs=[row(D_MODEL), row(HALF), row(LANES), pl.BlockSpec((SUBLANES, tm), lambda i: (0, i)),
                   pl.BlockSpec((ROUTER_ROWS, LANES), lambda i: (0, 0))],
        out_shape=[jax.ShapeDtypeStruct((T, D_MODEL), F32),
                   jax.ShapeDtypeStruct((T, HALF), U32),
                   jax.ShapeDtypeStruct((T, LANES), F32),
                   jax.ShapeDtypeStruct((SUBLANES, T), F32),
                   jax.ShapeDtypeStruct((ROUTER_ROWS, LANES), F32)],
        scratch_shapes=[pltpu.VMEM((ROUTER_ROWS, LANES), F32)],
        compiler_params=pltpu.CompilerParams(dimension_semantics=("arbitrary",), vmem_limit_bytes=VMEM_LIMIT),
        name="mid",
    )(x2d, mix, wo, gx, wq, ck, cv, wco, gz, wr, br)


def _dispatch_kernel(pos_ref, hz_ref, xs_zeroed, xs_hbm, sem):
    del xs_zeroed
    tm = hz_ref.shape[0]

    for r in range(tm):
        for k in range(TOP_K):
            pltpu.make_async_copy(hz_ref.at[pl.ds(r, 1)], xs_hbm.at[pl.ds(pos_ref[0, 0, k * tm + r], 1)],
                                  sem).start(priority=k % 2)
    for k in range(TOP_K):
        pltpu.make_async_copy(hz_ref, xs_hbm.at[pl.ds(0, tm)], sem).wait()


def _dispatch(hz_packed, pos, xs_zeroed):
    T = hz_packed.shape[0]
    tm = pos.shape[2] // TOP_K
    return pl.pallas_call(
        _dispatch_kernel,
        grid=(T // tm,),
        in_specs=[pl.BlockSpec((1, 1, TOP_K * tm), lambda i: (i, 0, 0), memory_space=pltpu.SMEM),
                  pl.BlockSpec((tm, HALF), lambda i: (i, 0)),
                  pl.BlockSpec(memory_space=pl.ANY)],
        out_specs=pl.BlockSpec(memory_space=pl.ANY),
        out_shape=jax.ShapeDtypeStruct(xs_zeroed.shape, U32),
        scratch_shapes=[pltpu.SemaphoreType.DMA(())],
        input_output_aliases={2: 0},
        compiler_params=pltpu.CompilerParams(dimension_semantics=("arbitrary",), vmem_limit_bytes=VMEM_LIMIT),
        name="dispatch",
    )(pos, hz_packed, xs_zeroed)


def _expert_kernel(te_ref, nt_ref, first_ref, slot_ref, next_ref, rows_ref, xs_ref, wg_hbm, wu_hbm, wd_hbm, ys_ref,
                   wg32, wu32, wd32, wgb, wub, wdb, wsem):
    i = pl.program_id(0)
    nt = nt_ref[0]
    tmx = xs_ref.shape[0]

    def fetch(e, s):
        return [pltpu.make_async_copy(src.at[e], dst.at[s], wsem.at[s])
                for src, dst in ((wg_hbm, wg32), (wu_hbm, wu32), (wd_hbm, wd32))]

    @pl.when(i < nt)
    def _():
        @pl.when(i == 0)
        def _():
            for cp in fetch(te_ref[0], 0):
                cp.start()

        @pl.when(first_ref[i] != 0)
        def _():
            s = slot_ref[i]
            for cp in fetch(te_ref[i], s):
                cp.wait()

            @pl.when(next_ref[i] >= 0)
            def _():
                for cp in fetch(next_ref[i], 1 - s):
                    cp.start()

            wgb[...] = wg32[s].astype(BF16)
            wub[...] = wu32[s].astype(BF16)
            wdb[...] = wd32[s].astype(BF16)

        nrows = rows_ref[i]
        for m in range(EXPERT_ROW_STEP, tmx + 1, EXPERT_ROW_STEP):
            @pl.when((nrows > m - EXPERT_ROW_STEP) & (nrows <= m))
            def _():
                x = _unpack_halves(xs_ref[0:m, :]).astype(BF16)
                hg = jnp.dot(x, wgb[...], preferred_element_type=F32)
                hu = jnp.dot(x, wub[...], preferred_element_type=F32)
                a = (hg * jax.nn.sigmoid(hg) * hu).astype(BF16)
                ys_ref[0:m, :] = _pack_halves(jnp.dot(a, wdb[...], preferred_element_type=F32))
                if m < tmx:
                    ys_ref[m:, :] = jnp.zeros((tmx - m, HALF), U32)

    @pl.when(i >= nt)
    def _():
        ys_ref[...] = jnp.zeros_like(ys_ref)


def _experts(xs, w_gate, w_up, w_down, tile_expert, ntiles, run_first, run_slot, run_next, tile_rows, tmx):
    n_tiles_max = tile_expert.shape[0]
    hbm = pl.BlockSpec(memory_space=pl.ANY)
    grid_spec = pltpu.PrefetchScalarGridSpec(
        num_scalar_prefetch=6,
        grid=(n_tiles_max,),
        in_specs=[pl.BlockSpec((tmx, HALF), lambda i, te, nt, *_: (jnp.minimum(i, nt[0] - 1), 0)), hbm, hbm, hbm],
        out_specs=pl.BlockSpec((tmx, HALF), lambda i, *_: (i, 0)),
        scratch_shapes=[pltpu.VMEM((2, D_MODEL, D_EXPERT), F32),
                        pltpu.VMEM((2, D_MODEL, D_EXPERT), F32),
                        pltpu.VMEM((2, D_EXPERT, D_MODEL), F32),
                        pltpu.VMEM((D_MODEL, D_EXPERT), BF16),
                        pltpu.VMEM((D_MODEL, D_EXPERT), BF16),
                        pltpu.VMEM((D_EXPERT, D_MODEL), BF16),
                        pltpu.SemaphoreType.DMA((2,))],
    )
    return pl.pallas_call(
        _expert_kernel,
        grid_spec=grid_spec,
        out_shape=jax.ShapeDtypeStruct(xs.shape, U32),
        compiler_params=pltpu.CompilerParams(dimension_semantics=("arbitrary",), vmem_limit_bytes=VMEM_LIMIT),
        name="experts",
    )(tile_expert, ntiles, run_first, run_slot, run_next, tile_rows, xs, w_gate, w_up, w_down)


def _final_kernel(posc_ref, posn_ref, x2_ref, route_ref, g_ref, ys_hbm, o_ref, ybuf, sem):
    i = pl.program_id(0)
    n = pl.num_programs(0)
    tm = x2_ref.shape[0]
    slot = i % 2

    def issue(pos_ref, s):
        for r in range(tm):
            for k in range(TOP_K):
                pltpu.make_async_copy(ys_hbm.at[pl.ds(pos_ref[0, 0, k * tm + r], 1)],
                                      ybuf.at[s, k, pl.ds(r, 1)], sem.at[s]).start(priority=k % 2)

    def wait(s):
        for k in range(TOP_K):
            pltpu.make_async_copy(ys_hbm.at[pl.ds(0, tm)], ybuf.at[s, k], sem.at[s]).wait()

    @pl.when(i == 0)
    def _():
        issue(posc_ref, 0)

    wait(slot)

    for s in range(2):
        @pl.when(slot == s)
        def _():
            issue(posn_ref, 1 - s)

    r = route_ref[...]
    g1 = r[:, ROUTE_G1:ROUTE_G1 + 1]
    g2 = r[:, ROUTE_G2:ROUTE_G2 + 1]
    xo = x2_ref[...] + g1 * _unpack_halves(ybuf[slot, 0]) + g2 * _unpack_halves(ybuf[slot, 1])
    o_ref[...] = _rms(xo, g_ref[...])

    @pl.when(i == n - 1)
    def _():
        wait(1 - slot)


def _final(x2, ys, pos, route, g):
    T = x2.shape[0]
    nblk = pos.shape[0]
    tm = T // nblk
    row = lambda w: pl.BlockSpec((tm, w), lambda i: (i, 0))
    return pl.pallas_call(
        _final_kernel,
        grid=(nblk,),
        in_specs=[pl.BlockSpec((1, 1, TOP_K * tm), lambda i: (i, 0, 0), memory_space=pltpu.SMEM),
                  pl.BlockSpec((1, 1, TOP_K * tm), lambda i: (jnp.minimum(i + 1, nblk - 1), 0, 0),
                               memory_space=pltpu.SMEM),
                  row(D_MODEL), row(LANES), pl.BlockSpec(g.shape, lambda i: (0, 0)),
                  pl.BlockSpec(memory_space=pl.ANY)],
        out_specs=row(D_MODEL),
        out_shape=jax.ShapeDtypeStruct((T, D_MODEL), F32),
        scratch_shapes=[pltpu.VMEM((2, TOP_K, tm, HALF), U32), pltpu.SemaphoreType.DMA((2,))],
        compiler_params=pltpu.CompilerParams(dimension_semantics=("arbitrary",), vmem_limit_bytes=VMEM_LIMIT),
        name="final",
    )(pos, pos, x2, route, g, ys)


def _band_bias(table):
    assert WINDOW == BLOCK
    i = jnp.arange(BLOCK)[:, None]
    j = jnp.arange(2 * BLOCK)[None, :]
    n = jnp.maximum(i + BLOCK - j, 0)
    nf = jnp.maximum(n, 1).astype(F32)
    large = MAX_EXACT + (jnp.log(nf / MAX_EXACT) / math.log(MAX_DISTANCE / MAX_EXACT)
                         * (NUM_BUCKETS - MAX_EXACT)).astype(I32)
    large = jnp.minimum(large, NUM_BUCKETS - 1)
    bucket = jnp.where(n < MAX_EXACT, n, large)
    onehot = (bucket[:, :, None] == jnp.arange(NUM_BUCKETS)[None, None, :]).astype(F32)
    bias = jnp.einsum("ijb,bh->hij", onehot, table.astype(F32), precision=lax.Precision.HIGHEST)
    from_prev = (jnp.arange(BLOCK)[None, :] > i)[None]
    prev, cur = bias[:, :, :BLOCK], bias[:, :, BLOCK:]
    return jnp.stack([jnp.where(from_prev, NEG_INF, cur), jnp.where(from_prev, prev, cur)])


def _dispatch_plan(route_rows, counts_f, tmx, n_tiles_max, tm_rows):
    T = route_rows.shape[1]
    experts = jnp.arange(N_EXPERTS, dtype=I32)
    counts = counts_f[:N_EXPERTS, 0].astype(I32)
    ptiles = (counts + tmx - 1) // tmx
    tile_end = jnp.cumsum(ptiles)
    nt = tile_end[-1]
    row_off = (tile_end - ptiles) * tmx

    def slot(e_row, r_row):
        e = route_rows[e_row].astype(I32)
        off = jnp.sum(jnp.where(e[None, :] == experts[:, None], row_off[:, None], 0), axis=0)
        return (off + route_rows[r_row].astype(I32)).reshape(T // tm_rows, 1, tm_rows)

    pos = jnp.concatenate([slot(ROUTE_E1, ROUTE_R1), slot(ROUTE_E2, ROUTE_R2)], axis=2)

    tile_ids = jnp.arange(n_tiles_max, dtype=I32)
    expert_of = lambda t: jnp.sum((tile_end[None, :] <= t[:, None]).astype(I32), axis=1)
    te = expert_of(jnp.minimum(tile_ids, nt - 1))

    used = ptiles > 0
    run_first = (jnp.any((tile_ids[:, None] == (tile_end - ptiles)[None, :]) & used[None, :], axis=1)
                 & (tile_ids < nt)).astype(I32)
    run_slot = (jnp.cumsum(run_first) - 1) % 2
    later_used = used[None, :] & (experts[None, :] > experts[:, None])
    next_of = jnp.min(jnp.where(later_used, experts[None, :], N_EXPERTS), axis=1)
    next_of = jnp.where(next_of < N_EXPERTS, next_of, -1)
    run_next = jnp.sum(jnp.where(te[:, None] == experts[None, :], next_of[None, :], 0), axis=1)

    of_tile = lambda v: jnp.sum(jnp.where(te[:, None] == experts[None, :], v[None, :], 0), axis=1)
    tile_rows = jnp.clip(of_tile(counts) - (tile_ids - of_tile(tile_end - ptiles)) * tmx, 0, tmx)
    tile_rows = jnp.where(tile_ids < nt, tile_rows, 0)
    return pos, te, nt.reshape(1), run_first, run_slot.astype(I32), run_next.astype(I32), tile_rows.astype(I32)


def kernel(x, mem, rel_bias_table, norm_mix, w_in, attn_sinks, conv_w, conv_b, gate_bias_i, gate_bias_f, mlstm_norm, w_out, norm_cross, norm_mem, w_cq, w_ck, w_cv, w_co, norm_moe, w_router_group, b_router_group, w_router_expert, b_router_expert, w_exp_gate, w_exp_up, w_exp_down, norm_final):
    B, S, _ = x.shape
    T = B * S
    depth = w_in.shape[0]
    x2d = x.reshape(T, D_MODEL)
    mem2d = mem.reshape(B * N_MEM, D_MODEL)
    bias = _band_bias(rel_bias_table)

    tmx = min(TM_EXPERT, T)
    n_tiles_max = (T * TOP_K) // tmx + N_EXPERTS
    tm_rows = min(TM_ROWDMA, T)

    assert depth == 1, "the final combine is fused with the final norm: single layer only"
    l = 0
    gb = jnp.concatenate([gate_bias_i[l], gate_bias_f[l]]).astype(F32)
    gbias_col = jnp.pad(gb, (0, LANES - GATE_ROWS))[None, :]
    qa, kva, qkb, vb, ob, gc, gr, xs_zeroed = _inproj(x2d, norm_mix[l][None, :], w_in[l], gbias_col, B, S,
                                                      n_tiles_max * tmx, tmx)

    per_seq = lambda a: a.reshape(B, S, a.shape[-1])
    mix = _seqmix(per_seq(qkb), per_seq(vb), per_seq(ob), per_seq(gc), gr, per_seq(qa), per_seq(kva),
                  conv_w[l][:, 0, :].astype(F32), conv_b[l][None, :].astype(F32),
                  mlstm_norm[l][None, :].astype(F32), bias, attn_sinks[l].astype(F32)).reshape(T, W_A_Q + W_B)

    ck, cv = _memkv(mem2d, norm_mem[l][None, :], w_ck[l].astype(BF16), w_cv[l].astype(BF16), B)

    wr = jnp.pad(jnp.concatenate([w_router_expert[l], w_router_group[l]], axis=1),
                 ((0, 0), (0, LANES - N_EXPERTS - N_GROUPS))).astype(BF16)
    br = jnp.pad(jnp.concatenate([b_router_expert[l], b_router_group[l]]),
                 (0, LANES - N_EXPERTS - N_GROUPS)).astype(F32)[None, :]
    x2, hz_packed, route, route_rows, counts = _mid(
        x2d, mix, w_out[l].astype(BF16), norm_cross[l][None, :], w_cq[l].astype(BF16), ck, cv,
        w_co[l].astype(BF16), norm_moe[l][None, :], wr, br, B, S)

    pos, te, nt, run_first, run_slot, run_next, tile_rows = _dispatch_plan(route_rows, counts, tmx, n_tiles_max,
                                                                           tm_rows)
    xs = _dispatch(hz_packed, pos, xs_zeroed)
    ys = _experts(xs, w_exp_gate[l], w_exp_up[l], w_exp_down[l], te, nt, run_first, run_slot, run_next, tile_rows,
                  tmx)
    out = _final(x2, ys, pos, route, norm_final[None, :])
    return out.reshape(B, S, D_MODEL)
```

```python
import math

import jax
import jax.numpy as jnp
from jax import lax
from jax.experimental import pallas as pl
from jax.experimental.pallas import tpu as pltpu

F32 = jnp.float32
BF16 = jnp.bfloat16
U32 = jnp.uint32
I32 = jnp.int32

D_MODEL = 1024
N_MEM = 256
N_HEADS_A = 8
N_KV_A = 2
HEAD_DIM_A = 64
BLOCK = 128
WINDOW = 128
NUM_BUCKETS = 32
MAX_EXACT = NUM_BUCKETS // 2
MAX_DISTANCE = 128
N_HEADS_B = 4
HEAD_DIM_B = 128
CHUNK = 128
CONV_WIDTH = 4
N_HEADS_X = 4
HEAD_DIM_X = D_MODEL // N_HEADS_X
N_GROUPS = 4
EXPERTS_PER_GROUP = 8
N_EXPERTS = N_GROUPS * EXPERTS_PER_GROUP
TOP_K = 2
D_EXPERT = 512
EPS = 1e-6
NEG_INF = -1e30

W_A_Q = N_HEADS_A * HEAD_DIM_A
W_A_KV = N_KV_A * HEAD_DIM_A
W_B = N_HEADS_B * HEAD_DIM_B
C_QA = 0
C_KVA = C_QA + W_A_Q
C_QKB = C_KVA + 2 * W_A_KV
C_VB = C_QKB + 2 * W_B
C_OB = C_VB + W_B
C_GATE = C_OB + W_B
D_IN = C_GATE + 2 * N_HEADS_B

LANES = 128
SUBLANES = 8
GATE_ROWS = 8
HALF = D_MODEL // 2

TM_INPROJ = 1024
TM_MID = 1024
TM_ROWDMA = 512
TM_EXPERT = 512
EXPERT_ROW_STEP = 128

V7X_VMEM_BYTES = 64 * 1024 * 1024
VMEM_LIMIT = V7X_VMEM_BYTES * 3 // 4


def _rms(xf, g):
    return xf * lax.rsqrt(jnp.mean(xf * xf, axis=-1, keepdims=True) + EPS) * g


def _pack_halves(v):
    b = pltpu.bitcast(v.astype(BF16).astype(F32), U32)
    return (b[:, :HALF] >> 16) | b[:, HALF:]


def _unpack_halves(p):
    lo = pltpu.bitcast(p << 16, F32)
    hi = pltpu.bitcast(p & jnp.uint32(0xFFFF0000), F32)
    return jnp.concatenate([lo, hi], axis=-1)


def _log_sigmoid(z):
    return jnp.minimum(z, 0.0) - jnp.log1p(jnp.exp(-jnp.abs(z)))


def _split3(v):
    hi = v.astype(BF16).astype(F32)
    rest = v - hi
    mid = rest.astype(BF16).astype(F32)
    return hi, mid, (rest - mid).astype(BF16).astype(F32)


def _inproj_kernel(x_ref, g_ref, w32_ref, gbc_ref, qa_ref, kva_ref, qkb_ref, vb_ref, ob_ref, gc_ref, gr_ref,
                   xs_hbm, w_ref, zbuf, zsem):
    zero_fills = [pltpu.make_async_copy(zbuf, xs_hbm.at[pl.ds(t * zbuf.shape[0], zbuf.shape[0])], zsem)
                  for t in range(xs_hbm.shape[0] // zbuf.shape[0])]

    @pl.when(pl.program_id(0) == 0)
    def _():
        w_ref[:, C_GATE:] = jnp.zeros((D_MODEL, LANES), BF16)
        w_ref[:, :D_IN] = w32_ref[...].astype(BF16)
        zbuf[...] = jnp.zeros_like(zbuf)
        for fill in zero_fills:
            fill.start(priority=1)

    tm = x_ref.shape[0]
    h = _rms(x_ref[...], g_ref[...]).astype(BF16)

    def mm(lo, hi):
        return jnp.dot(h, w_ref[:, lo:hi], preferred_element_type=F32)

    qa_ref[...] = mm(C_QA, C_KVA).astype(BF16)
    kva_ref[...] = mm(C_KVA, C_QKB).astype(BF16)
    qkb_ref[...] = mm(C_QKB, C_VB).astype(BF16)
    vb_ref[...] = mm(C_VB, C_OB).astype(BF16)
    ob_ref[...] = mm(C_OB, C_GATE).astype(BF16)

    H, L = N_HEADS_B, CHUNK
    gcol = mm(C_GATE, C_GATE + LANES) + gbc_ref[...]
    grow = jnp.transpose(gcol)[0:GATE_ROWS, :]
    lane_c = lax.broadcasted_iota(I32, (L, LANES), 1)
    is_f_col = (lane_c >= H) & (lane_c < 2 * H)
    is_f_row = lax.broadcasted_iota(I32, (GATE_ROWS, L), 0) >= H
    ti = lax.broadcasted_iota(I32, (L, L), 0)
    si = lax.broadcasted_iota(I32, (L, L), 1)
    tril = jnp.where(si <= ti, 1.0, 0.0).astype(BF16)
    triu = jnp.where(si >= ti, 1.0, 0.0).astype(BF16)
    for c in range(tm // L):
        rows = slice(c * L, (c + 1) * L)
        gcol_c = gcol[rows, :]
        fcol = jnp.where(is_f_col, _log_sigmoid(gcol_c), 0.0)
        parts = jnp.dot(tril, jnp.concatenate(_split3(fcol), axis=1).astype(BF16), preferred_element_type=F32)
        bcol = parts[:, :LANES] + parts[:, LANES:2 * LANES] + parts[:, 2 * LANES:]
        gc_ref[rows, :] = jnp.where(is_f_col, bcol, gcol_c)
        grow_c = grow[:, rows]
        frow = jnp.where(is_f_row, _log_sigmoid(grow_c), 0.0)
        parts = jnp.dot(jnp.concatenate(_split3(frow), axis=0).astype(BF16), triu, preferred_element_type=F32)
        brow = parts[:GATE_ROWS] + parts[GATE_ROWS:2 * GATE_ROWS] + parts[2 * GATE_ROWS:]
        gr_ref[:, rows] = jnp.where(is_f_row, brow, grow_c)

    @pl.when(pl.program_id(0) == pl.num_programs(0) - 1)
    def _():
        for fill in zero_fills:
            fill.wait()


def _inproj(x2d, g, w_in, gbias_col, B, S, n_slots, tmx):
    T = x2d.shape[0]
    tm = min(TM_INPROJ, S)
    tiles_per_seq = S // tm
    row = lambda w: pl.BlockSpec((tm, w), lambda i: (i, 0))
    full = lambda a: pl.BlockSpec(a.shape, lambda i: (0,) * a.ndim)
    return pl.pallas_call(
        _inproj_kernel,
        grid=(T // tm,),
        in_specs=[row(D_MODEL), full(g),
                  pl.BlockSpec(w_in.shape, lambda i: (0, 0), pipeline_mode=pl.Buffered(1)), full(gbias_col)],
        out_specs=[row(W_A_Q), row(2 * W_A_KV), row(2 * W_B), row(W_B), row(W_B), row(LANES),
                   pl.BlockSpec((None, GATE_ROWS, tm), lambda i: (i // tiles_per_seq, 0, i % tiles_per_seq)),
                   pl.BlockSpec(memory_space=pl.ANY)],
        out_shape=[jax.ShapeDtypeStruct((T, W_A_Q), BF16),
                   jax.ShapeDtypeStruct((T, 2 * W_A_KV), BF16),
                   jax.ShapeDtypeStruct((T, 2 * W_B), BF16),
                   jax.ShapeDtypeStruct((T, W_B), BF16),
                   jax.ShapeDtypeStruct((T, W_B), BF16),
                   jax.ShapeDtypeStruct((T, LANES), F32),
                   jax.ShapeDtypeStruct((B, GATE_ROWS, S), F32),
                   jax.ShapeDtypeStruct((n_slots, HALF), U32)],
        scratch_shapes=[pltpu.VMEM((D_MODEL, C_GATE + LANES), BF16), pltpu.VMEM((tmx, HALF), U32),
                        pltpu.SemaphoreType.DMA(())],
        compiler_params=pltpu.CompilerParams(dimension_semantics=("arbitrary",), vmem_limit_bytes=VMEM_LIMIT),
        name="inproj",
    )(x2d, g, w_in, gbias_col)


def _swa_block(q, kvc, kvp, bias_ref, sink_ref):
    kvp = kvp.astype(F32)
    kvc = kvc.astype(F32)
    kband = jnp.concatenate([kvp[:, :W_A_KV], kvc[:, :W_A_KV]], axis=0)
    vband = jnp.concatenate([kvp[:, W_A_KV:], kvc[:, W_A_KV:]], axis=0)
    lane = lax.broadcasted_iota(I32, (2 * BLOCK, LANES), 1)
    lo = lane < HEAD_DIM_A

    def placements(band):
        swapped = pltpu.roll(band, HEAD_DIM_A, axis=1)
        z = jnp.zeros_like(band)
        return {(0, 0): jnp.where(lo, band, z).astype(BF16), (0, 1): jnp.where(lo, z, swapped).astype(BF16),
                (1, 0): jnp.where(lo, swapped, z).astype(BF16), (1, 1): jnp.where(lo, z, band).astype(BF16)}

    kpl = placements(kband)
    vpl = placements(vband)

    scale = HEAD_DIM_A ** -0.5
    group = N_HEADS_A // N_KV_A
    tiles = []
    from_prev = (lax.broadcasted_iota(I32, (BLOCK, BLOCK), 1) > lax.broadcasted_iota(I32, (BLOCK, BLOCK), 0))

    for pair in range(N_HEADS_A // 2):
        qt = q[:, pair * LANES:(pair + 1) * LANES]
        acc = None
        for half in range(2):
            h = 2 * pair + half
            g = h // group
            s2 = lax.dot_general(qt, kpl[(g, half)], (((1,), (1,)), ((), ())), preferred_element_type=F32)
            s = jnp.where(from_prev, s2[:, :BLOCK], s2[:, BLOCK:])
            s = s * scale + bias_ref[h]
            sink = sink_ref[h]
            m = jnp.maximum(jnp.max(s, axis=-1, keepdims=True), sink)
            p = jnp.exp(s - m)
            denom = jnp.sum(p, axis=-1, keepdims=True) + jnp.exp(sink - m)
            p2 = jnp.concatenate([jnp.where(from_prev, p, 0.0), jnp.where(from_prev, 0.0, p)], axis=1)
            o = jnp.dot(p2.astype(BF16), vpl[(g, half)], preferred_element_type=F32) / denom
            acc = o if acc is None else acc + o
        tiles.append(acc.astype(BF16))
    return tiles


CONV_HALO = 16
SEQ_SUB = 2


def _seqmix_kernel(sink_ref, qkc_ref, qkp_ref, vb_ref, ob_ref, gc_ref, gr_ref, qa_ref, kvc_ref, kvp_ref,
                   cw_ref, cb_ref, nrm_ref, bias0_ref, bias_ref, o_ref, state_ref, m_ref):
    c = pl.program_id(0)
    B = qkc_ref.shape[0]
    H, D, L = N_HEADS_B, HEAD_DIM_B, CHUNK

    @pl.when(c == 0)
    def _():
        state_ref[...] = jnp.zeros_like(state_ref)
        m_ref[...] = jnp.zeros_like(m_ref)

    rr = lax.broadcasted_iota(I32, (L, CONV_HALO + L), 0)
    cc = lax.broadcasted_iota(I32, (L, CONV_HALO + L), 1)
    shifts = {delay: jnp.where(cc == rr + (CONV_HALO - delay), 1.0, 0.0).astype(BF16)
              for delay in range(1, CONV_WIDTH)}
    ti = lax.broadcasted_iota(I32, (L, L), 0)
    si = lax.broadcasted_iota(I32, (L, L), 1)
    tri = si <= ti
    ones_blk = jnp.ones((L, D), BF16)

    def conv_silu(b, u):
        if u == 0:
            prev = qkp_ref[b]
            prev = jnp.where(c > 0, prev, jnp.zeros_like(prev))
        else:
            prev = qkc_ref[b, u * L - CONV_HALO:u * L, :]
        cur = qkc_ref[b, u * L:(u + 1) * L, :]
        ext = jnp.concatenate([prev, cur], axis=0)
        y = cb_ref[...] + cw_ref[CONV_WIDTH - 1:CONV_WIDTH, :] * cur.astype(F32)
        for delay in range(1, CONV_WIDTH):
            tap = CONV_WIDTH - 1 - delay
            y = y + cw_ref[tap:tap + 1, :] * jnp.dot(shifts[delay], ext, preferred_element_type=F32)
        return y * jax.nn.sigmoid(y)

    states = {(b, h): state_ref[b, h] for b in range(B) for h in range(H)}
    ms = {(b, h): m_ref[b, h:h + 1, 0:1] for b in range(B) for h in range(H)}

    n_sub = qkc_ref.shape[1] // L
    for u, b, h in [(u, b, h) for u in range(n_sub) for b in range(B) for h in range(H)]:
        rows = slice(u * L, (u + 1) * L)
        if h == 0:
            kvp = kvp_ref[b] if u == 0 else kvc_ref[b, (u - 1) * L:u * L, :]
            tiles = _swa_block(qa_ref[b, rows, :], kvc_ref[b, rows, :], kvp, bias0_ref if u == 0 else bias_ref,
                               sink_ref)
            for pair, tile in enumerate(tiles):
                o_ref[b, rows, pair * LANES:(pair + 1) * LANES] = tile
            qk = conv_silu(b, u)
            gcol = gc_ref[b, rows, :]
            grow = gr_ref[b, :, rows]
        qh = (qk[:, h * D:(h + 1) * D] * (D ** -0.5)).astype(BF16)
        k_t = qk[:, W_B + h * D:W_B + (h + 1) * D].T
        v1 = jnp.concatenate([vb_ref[b, rows, h * D:(h + 1) * D], ones_blk], axis=-1)
        b_r = grow[H + h:H + h + 1, :]
        g_r = grow[h:h + 1, :] - b_r
        b_c = gcol[:, H + h:H + h + 1]
        m_prev = ms[b, h]
        state = states[b, h]

        gmat = jnp.where(tri, g_r, NEG_INF)
        m_c = jnp.maximum(jnp.max(gmat, axis=-1, keepdims=True), m_prev)
        a_inter = jnp.exp(m_prev - m_c)
        sc = jnp.dot(qh, k_t.astype(BF16), preferred_element_type=F32) * jnp.exp(gmat - m_c)
        tot = (jnp.dot(sc.astype(BF16), v1, preferred_element_type=F32)
               + a_inter * jnp.dot(qh, state.astype(BF16), preferred_element_type=F32))
        num = tot[:, :D]
        den = tot[:, D:]
        hh = num / jnp.maximum(jnp.abs(den), jnp.exp(-(b_c + m_c)))

        b_last = b_r[:, L - 1:L]
        m_new = jnp.maximum(b_last + m_prev, b_last + jnp.max(g_r, axis=-1, keepdims=True))
        w_r = jnp.exp(g_r + (b_last - m_new))
        decay = jnp.exp(b_last + m_prev - m_new)
        upd = jnp.dot((k_t * w_r).astype(BF16), v1, preferred_element_type=F32)
        states[b, h] = decay * state + upd
        ms[b, h] = m_new

        og = jax.nn.sigmoid(ob_ref[b, rows, h * D:(h + 1) * D].astype(F32))
        hb = og * hh
        hb = hb * lax.rsqrt(jnp.mean(hb * hb, axis=-1, keepdims=True) + EPS)
        o_ref[b, rows, W_A_Q + h * D:W_A_Q + (h + 1) * D] = (hb * nrm_ref[:, h * D:(h + 1) * D]).astype(BF16)

    for b, h in states:
        state_ref[b, h] = states[b, h]
        m_ref[b, h:h + 1, :] = jnp.broadcast_to(ms[b, h], (1, LANES))


def _seqmix(qkb, vb, ob, gc, gr, qa, kva, conv_w, conv_b, nrm, bias, sinks):
    assert CHUNK == BLOCK
    B, S, _ = qkb.shape
    rows = min(SEQ_SUB * CHUNK, S)
    halo_per_step = rows // CONV_HALO
    blk = lambda w: pl.BlockSpec((B, rows, w), lambda c: (0, c, 0))
    full = lambda a: pl.BlockSpec(a.shape, lambda c: (0,) * a.ndim)
    bias_variant = lambda pick: pl.BlockSpec((None,) + bias.shape[1:], lambda c: (pick(c), 0, 0, 0))
    return pl.pallas_call(
        _seqmix_kernel,
        grid=(S // rows,),
        in_specs=[pl.BlockSpec(memory_space=pltpu.SMEM),
                  blk(2 * W_B),
                  pl.BlockSpec((B, CONV_HALO, 2 * W_B), lambda c: (0, jnp.maximum(c * halo_per_step - 1, 0), 0)),
                  blk(W_B), blk(W_B), blk(LANES),
                  pl.BlockSpec((B, GATE_ROWS, rows), lambda c: (0, 0, c)),
                  blk(W_A_Q), blk(2 * W_A_KV),
                  pl.BlockSpec((B, BLOCK, 2 * W_A_KV), lambda c: (0, jnp.maximum(c * (rows // BLOCK) - 1, 0), 0)),
                  full(conv_w), full(conv_b), full(nrm),
                  bias_variant(lambda c: jnp.minimum(c, 1)), bias_variant(lambda c: 1)],
        out_specs=blk(W_A_Q + W_B),
        out_shape=jax.ShapeDtypeStruct((B, S, W_A_Q + W_B), BF16),
        scratch_shapes=[pltpu.VMEM((B, N_HEADS_B, HEAD_DIM_B, 2 * HEAD_DIM_B), F32),
                        pltpu.VMEM((B, GATE_ROWS, LANES), F32)],
        compiler_params=pltpu.CompilerParams(dimension_semantics=("arbitrary",), vmem_limit_bytes=VMEM_LIMIT),
        name="seqmix",
    )(sinks, qkb, qkb, vb, ob, gc, gr, qa, kva, kva, conv_w, conv_b, nrm, bias, bias)


def _memkv_kernel(mem_ref, g_ref, wk_ref, wv_ref, k_ref, v_ref):
    hm = _rms(mem_ref[...], g_ref[...]).astype(BF16)
    k_ref[...] = jnp.dot(hm, wk_ref[...], preferred_element_type=F32).astype(BF16)
    v_ref[...] = jnp.dot(hm, wv_ref[...], preferred_element_type=F32).astype(BF16)


def _memkv(mem2d, g, wk, wv, B):
    full = lambda a: pl.BlockSpec(a.shape, lambda b: (0,) * a.ndim)
    blk = pl.BlockSpec((N_MEM, D_MODEL), lambda b: (b, 0))
    return pl.pallas_call(
        _memkv_kernel,
        grid=(B,),
        in_specs=[blk, full(g), full(wk), full(wv)],
        out_specs=[blk, blk],
        out_shape=[jax.ShapeDtypeStruct((B * N_MEM, D_MODEL), BF16)] * 2,
        compiler_params=pltpu.CompilerParams(dimension_semantics=("parallel",), vmem_limit_bytes=VMEM_LIMIT),
        name="memkv",
    )(mem2d, g, wk, wv)


ROUTE_E1, ROUTE_E2, ROUTE_G1, ROUTE_G2, ROUTE_R1, ROUTE_R2 = 0, 1, 2, 3, 4, 5
ROUTER_GROUP_COL = N_EXPERTS
ROUTER_ROWS = -(-(N_EXPERTS + N_GROUPS) // SUBLANES) * SUBLANES


def _mid_kernel(x_ref, mix_ref, wo_ref, gx_ref, wq_ref, ck_ref, cv_ref, wco_ref, gz_ref, wr_ref, br_ref,
                x2_ref, hz_ref, route_ref, rrows_ref, counts_ref, cnt_ref):
    @pl.when(pl.program_id(0) == 0)
    def _():
        cnt_ref[...] = jnp.zeros_like(cnt_ref)

    x1 = x_ref[...] + jnp.dot(mix_ref[...], wo_ref[...], preferred_element_type=F32)

    hc = _rms(x1, gx_ref[...]).astype(BF16)
    cq = jnp.dot(hc, wq_ref[...], preferred_element_type=F32).astype(BF16)
    scale = HEAD_DIM_X ** -0.5
    heads = []
    for h in range(N_HEADS_X):
        sl = slice(h * HEAD_DIM_X, (h + 1) * HEAD_DIM_X)
        s = lax.dot_general(cq[:, sl], ck_ref[:, sl], (((1,), (1,)), ((), ())), preferred_element_type=F32) * scale
        p = jnp.exp(s - jnp.max(s, axis=-1, keepdims=True))
        co = jnp.dot(p.astype(BF16), cv_ref[:, sl], preferred_element_type=F32) / jnp.sum(p, axis=-1, keepdims=True)
        heads.append(co.astype(BF16))
    x2 = x1 + jnp.dot(jnp.concatenate(heads, axis=-1), wco_ref[...], preferred_element_type=F32)
    x2_ref[...] = x2

    hz = _rms(x2, gz_ref[...])
    hz_ref[...] = _pack_halves(hz)
    lg = jnp.dot(hz.astype(BF16), wr_ref[...], preferred_element_type=F32) + br_ref[...]
    tm = lg.shape[0]
    lt = jnp.transpose(lg)[0:ROUTER_ROWS, :]
    row = lax.broadcasted_iota(I32, lt.shape, 0)
    big = jnp.int32(ROUTER_ROWS)
    is_g = (row >= ROUTER_GROUP_COL) & (row < ROUTER_GROUP_COL + N_GROUPS)
    gl = jnp.where(is_g, lt, NEG_INF)
    gmax = jnp.max(gl, axis=0, keepdims=True)
    gsum = jnp.sum(jnp.exp(gl - gmax), axis=0, keepdims=True)
    g_prob = 1.0 / gsum
    g_idx = jnp.min(jnp.where(gl == gmax, row - ROUTER_GROUP_COL, big), axis=0, keepdims=True)
    sel = (row < N_EXPERTS) & ((row // EXPERTS_PER_GROUP) == g_idx)
    el = jnp.where(sel, lt, NEG_INF)
    m1 = jnp.max(el, axis=0, keepdims=True)
    i1 = jnp.min(jnp.where(el == m1, row, big), axis=0, keepdims=True)
    el2 = jnp.where(row == i1, NEG_INF, el)
    m2 = jnp.max(el2, axis=0, keepdims=True)
    i2 = jnp.min(jnp.where(el2 == m2, row, big), axis=0, keepdims=True)
    z = jnp.sum(jnp.exp(el - m1), axis=0, keepdims=True)
    p1 = 1.0 / z
    p2 = jnp.exp(m2 - m1) / z
    g1 = g_prob * (p1 / (p1 + p2))
    g2 = g_prob * (p2 / (p1 + p2))

    used = jnp.where((row == i1) | (row == i2), 1.0, 0.0)
    t_from = lax.broadcasted_iota(I32, (tm, tm), 0)
    t_to = lax.broadcasted_iota(I32, (tm, tm), 1)
    earlier = jnp.where(t_from < t_to, 1.0, 0.0).astype(BF16)
    before = jnp.dot(used.astype(BF16), earlier, preferred_element_type=F32) + cnt_ref[:, 0:1]
    r1 = jnp.sum(jnp.where(row == i1, before, 0.0), axis=0, keepdims=True)
    r2 = jnp.sum(jnp.where(row == i2, before, 0.0), axis=0, keepdims=True)
    cnt_ref[...] = cnt_ref[...] + jnp.sum(used, axis=1, keepdims=True)
    counts_ref[...] = cnt_ref[...]

    rec_row = lax.broadcasted_iota(I32, (SUBLANES, tm), 0)
    rec = jnp.zeros((SUBLANES, tm), F32)
    for c, v in ((ROUTE_E1, i1.astype(F32)), (ROUTE_E2, i2.astype(F32)), (ROUTE_G1, g1), (ROUTE_G2, g2),
                 (ROUTE_R1, r1), (ROUTE_R2, r2)):
        rec = jnp.where(rec_row == c, v, rec)
    rrows_ref[...] = rec
    route_ref[...] = jnp.transpose(jnp.concatenate([rec, jnp.zeros((LANES - SUBLANES, tm), F32)], axis=0))


def _mid(x2d, mix, wo, gx, wq, ck, cv, wco, gz, wr, br, B, S):
    T = B * S
    tm = min(TM_MID, S)
    per_b = S // tm
    row = lambda w: pl.BlockSpec((tm, w), lambda i: (i, 0))
    full = lambda a: pl.BlockSpec(a.shape, lambda i: (0,) * a.ndim)
    kvspec = pl.BlockSpec((N_MEM, D_MODEL), lambda i: (i // per_b, 0))
    return pl.pallas_call(
        _mid_kernel,
        grid=(T // tm,),
        in_specs=[row(D_MODEL), row(W_A_Q + W_B), full(wo), full(gx), full(wq), kvspec, kvspec,
                  full(wco), full(gz), full(wr), full(br)],
        out_specs=[row(D_MODEL), row(HALF), row(LANES), pl.BlockSpec((SUBLANES, tm), lambda i: (0, i)),
                   pl.BlockSpec((ROUTER_ROWS, LANES), lambda i: (0, 0))],
        out_shape=[jax.ShapeDtypeStruct((T, D_MODEL), F32),
                   jax.ShapeDtypeStruct((T, HALF), U32),
                   jax.ShapeDtypeStruct((T, LANES), F32),
                   jax.ShapeDtypeStruct((SUBLANES, T), F32),
                   jax.ShapeDtypeStruct((ROUTER_ROWS, LANES), F32)],
        scratch_shapes=[pltpu.VMEM((ROUTER_ROWS, LANES), F32)],
        compiler_params=pltpu.CompilerParams(dimension_semantics=("arbitrary",), vmem_limit_bytes=VMEM_LIMIT),
        name="mid",
    )(x2d, mix, wo, gx, wq, ck, cv, wco, gz, wr, br)


def _dispatch_kernel(pos_ref, hz_ref, xs_zeroed, xs_hbm, stage, sem):
    del xs_zeroed
    i = pl.program_id(0)
    n = pl.num_programs(0)
    tm = hz_ref.shape[0]
    par = i % 2

    def wait(p):
        for k in range(TOP_K):
            pltpu.make_async_copy(hz_ref, xs_hbm.at[pl.ds(0, tm)], sem.at[p]).wait()

    for p in range(2):
        @pl.when(par == p)
        def _():
            stage[p] = hz_ref[...]
            for r in range(tm):
                for k in range(TOP_K):
                    pltpu.make_async_copy(stage.at[p, pl.ds(r, 1)],
                                          xs_hbm.at[pl.ds(pos_ref[0, 0, k * tm + r], 1)],
                                          sem.at[p]).start(priority=k % 2)

    @pl.when(i > 0)
    def _():
        wait(1 - par)

    @pl.when(i == n - 1)
    def _():
        wait(par)


def _dispatch(hz_packed, pos, xs_zeroed):
    T = hz_packed.shape[0]
    tm = pos.shape[2] // TOP_K
    return pl.pallas_call(
        _dispatch_kernel,
        grid=(T // tm,),
        in_specs=[pl.BlockSpec((1, 1, TOP_K * tm), lambda i: (i, 0, 0), memory_space=pltpu.SMEM),
                  pl.BlockSpec((tm, HALF), lambda i: (i, 0)),
                  pl.BlockSpec(memory_space=pl.ANY)],
        out_specs=pl.BlockSpec(memory_space=pl.ANY),
        out_shape=jax.ShapeDtypeStruct(xs_zeroed.shape, U32),
        scratch_shapes=[pltpu.VMEM((2, tm, HALF), U32), pltpu.SemaphoreType.DMA((2,))],
        input_output_aliases={2: 0},
        compiler_params=pltpu.CompilerParams(dimension_semantics=("arbitrary",), vmem_limit_bytes=VMEM_LIMIT),
        name="dispatch",
    )(pos, hz_packed, xs_zeroed)


def _expert_kernel(te_ref, nt_ref, first_ref, slot_ref, next_ref, rows_ref, xs_ref, wg_hbm, wu_hbm, wd_hbm, ys_ref,
                   wg32, wu32, wd32, wgb, wub, wdb, wsem):
    i = pl.program_id(0)
    nt = nt_ref[0]
    tmx = xs_ref.shape[0]

    def fetch(e, s):
        return [pltpu.make_async_copy(src.at[e], dst.at[s], wsem.at[s])
                for src, dst in ((wg_hbm, wg32), (wu_hbm, wu32), (wd_hbm, wd32))]

    @pl.when(i < nt)
    def _():
        @pl.when(i == 0)
        def _():
            for cp in fetch(te_ref[0], 0):
                cp.start()

        @pl.when(first_ref[i] != 0)
        def _():
            s = slot_ref[i]
            for cp in fetch(te_ref[i], s):
                cp.wait()

            @pl.when(next_ref[i] >= 0)
            def _():
                for cp in fetch(next_ref[i], 1 - s):
                    cp.start()

            wgb[...] = wg32[s].astype(BF16)
            wub[...] = wu32[s].astype(BF16)
            wdb[...] = wd32[s].astype(BF16)

        nrows = rows_ref[i]
        for m in range(EXPERT_ROW_STEP, tmx + 1, EXPERT_ROW_STEP):
            @pl.when((nrows > m - EXPERT_ROW_STEP) & (nrows <= m))
            def _():
                x = _unpack_halves(xs_ref[0:m, :]).astype(BF16)
                hg = jnp.dot(x, wgb[...], preferred_element_type=F32)
                hu = jnp.dot(x, wub[...], preferred_element_type=F32)
                a = (hg * jax.nn.sigmoid(hg) * hu).astype(BF16)
                ys_ref[0:m, :] = _pack_halves(jnp.dot(a, wdb[...], preferred_element_type=F32))
                if m < tmx:
                    ys_ref[m:, :] = jnp.zeros((tmx - m, HALF), U32)

    @pl.when(i >= nt)
    def _():
        ys_ref[...] = jnp.zeros_like(ys_ref)


def _experts(xs, w_gate, w_up, w_down, tile_expert, ntiles, run_first, run_slot, run_next, tile_rows, tmx):
    n_tiles_max = tile_expert.shape[0]
    hbm = pl.BlockSpec(memory_space=pl.ANY)
    grid_spec = pltpu.PrefetchScalarGridSpec(
        num_scalar_prefetch=6,
        grid=(n_tiles_max,),
        in_specs=[pl.BlockSpec((tmx, HALF), lambda i, te, nt, *_: (jnp.minimum(i, nt[0] - 1), 0)), hbm, hbm, hbm],
        out_specs=pl.BlockSpec((tmx, HALF), lambda i, *_: (i, 0)),
        scratch_shapes=[pltpu.VMEM((2, D_MODEL, D_EXPERT), F32),
                        pltpu.VMEM((2, D_MODEL, D_EXPERT), F32),
                        pltpu.VMEM((2, D_EXPERT, D_MODEL), F32),
                        pltpu.VMEM((D_MODEL, D_EXPERT), BF16),
                        pltpu.VMEM((D_MODEL, D_EXPERT), BF16),
                        pltpu.VMEM((D_EXPERT, D_MODEL), BF16),
                        pltpu.SemaphoreType.DMA((2,))],
    )
    return pl.pallas_call(
        _expert_kernel,
        grid_spec=grid_spec,
        out_shape=jax.ShapeDtypeStruct(xs.shape, U32),
        compiler_params=pltpu.CompilerParams(dimension_semantics=("arbitrary",), vmem_limit_bytes=VMEM_LIMIT),
        name="experts",
    )(tile_expert, ntiles, run_first, run_slot, run_next, tile_rows, xs, w_gate, w_up, w_down)


def _final_kernel(posc_ref, posn_ref, x2_ref, route_ref, g_ref, ys_hbm, o_ref, ybuf, sem):
    i = pl.program_id(0)
    n = pl.num_programs(0)
    tm = x2_ref.shape[0]
    slot = i % 2

    def issue(pos_ref, s):
        for r in range(tm):
            for k in range(TOP_K):
                pltpu.make_async_copy(ys_hbm.at[pl.ds(pos_ref[0, 0, k * tm + r], 1)],
                                      ybuf.at[s, k, pl.ds(r, 1)], sem.at[s]).start(priority=k % 2)

    def wait(s):
        for k in range(TOP_K):
            pltpu.make_async_copy(ys_hbm.at[pl.ds(0, tm)], ybuf.at[s, k], sem.at[s]).wait()

    @pl.when(i == 0)
    def _():
        issue(posc_ref, 0)

    wait(slot)

    for s in range(2):
        @pl.when(slot == s)
        def _():
            issue(posn_ref, 1 - s)

    r = route_ref[...]
    g1 = r[:, ROUTE_G1:ROUTE_G1 + 1]
    g2 = r[:, ROUTE_G2:ROUTE_G2 + 1]
    xo = x2_ref[...] + g1 * _unpack_halves(ybuf[slot, 0]) + g2 * _unpack_halves(ybuf[slot, 1])
    o_ref[...] = _rms(xo, g_ref[...])

    @pl.when(i == n - 1)
    def _():
        wait(1 - slot)


def _final(x2, ys, pos, route, g):
    T = x2.shape[0]
    nblk = pos.shape[0]
    tm = T // nblk
    row = lambda w: pl.BlockSpec((tm, w), lambda i: (i, 0))
    return pl.pallas_call(
        _final_kernel,
        grid=(nblk,),
        in_specs=[pl.BlockSpec((1, 1, TOP_K * tm), lambda i: (i, 0, 0), memory_space=pltpu.SMEM),
                  pl.BlockSpec((1, 1, TOP_K * tm), lambda i: (jnp.minimum(i + 1, nblk - 1), 0, 0),
                               memory_space=pltpu.SMEM),
                  row(D_MODEL), row(LANES), pl.BlockSpec(g.shape, lambda i: (0, 0)),
                  pl.BlockSpec(memory_space=pl.ANY)],
        out_specs=row(D_MODEL),
        out_shape=jax.ShapeDtypeStruct((T, D_MODEL), F32),
        scratch_shapes=[pltpu.VMEM((2, TOP_K, tm, HALF), U32), pltpu.SemaphoreType.DMA((2,))],
        compiler_params=pltpu.CompilerParams(dimension_semantics=("arbitrary",), vmem_limit_bytes=VMEM_LIMIT),
        name="final",
    )(pos, pos, x2, route, g, ys)


def _band_bias(table):
    assert WINDOW == BLOCK
    i = jnp.arange(BLOCK)[:, None]
    j = jnp.arange(2 * BLOCK)[None, :]
    n = jnp.maximum(i + BLOCK - j, 0)
    nf = jnp.maximum(n, 1).astype(F32)
    large = MAX_EXACT + (jnp.log(nf / MAX_EXACT) / math.log(MAX_DISTANCE / MAX_EXACT)
                         * (NUM_BUCKETS - MAX_EXACT)).astype(I32)
    large = jnp.minimum(large, NUM_BUCKETS - 1)
    bucket = jnp.where(n < MAX_EXACT, n, large)
    onehot = (bucket[:, :, None] == jnp.arange(NUM_BUCKETS)[None, None, :]).astype(F32)
    bias = jnp.einsum("ijb,bh->hij", onehot, table.astype(F32), precision=lax.Precision.HIGHEST)
    from_prev = (jnp.arange(BLOCK)[None, :] > i)[None]
    prev, cur = bias[:, :, :BLOCK], bias[:, :, BLOCK:]
    return jnp.stack([jnp.where(from_prev, NEG_INF, cur), jnp.where(from_prev, prev, cur)])


def _dispatch_plan(route_rows, counts_f, tmx, n_tiles_max, tm_rows):
    T = route_rows.shape[1]
    experts = jnp.arange(N_EXPERTS, dtype=I32)
    counts = counts_f[:N_EXPERTS, 0].astype(I32)
    ptiles = (counts + tmx - 1) // tmx
    tile_end = jnp.cumsum(ptiles)
    nt = tile_end[-1]
    row_off = (tile_end - ptiles) * tmx

    def slot(e_row, r_row):
        e = route_rows[e_row].astype(I32)
        off = jnp.sum(jnp.where(e[None, :] == experts[:, None], row_off[:, None], 0), axis=0)
        return (off + route_rows[r_row].astype(I32)).reshape(T // tm_rows, 1, tm_rows)

    pos = jnp.concatenate([slot(ROUTE_E1, ROUTE_R1), slot(ROUTE_E2, ROUTE_R2)], axis=2)

    tile_ids = jnp.arange(n_tiles_max, dtype=I32)
    expert_of = lambda t: jnp.sum((tile_end[None, :] <= t[:, None]).astype(I32), axis=1)
    te = expert_of(jnp.minimum(tile_ids, nt - 1))

    used = ptiles > 0
    run_first = (jnp.any((tile_ids[:, None] == (tile_end - ptiles)[None, :]) & used[None, :], axis=1)
                 & (tile_ids < nt)).astype(I32)
    run_slot = (jnp.cumsum(run_first) - 1) % 2
    later_used = used[None, :] & (experts[None, :] > experts[:, None])
    next_of = jnp.min(jnp.where(later_used, experts[None, :], N_EXPERTS), axis=1)
    next_of = jnp.where(next_of < N_EXPERTS, next_of, -1)
    run_next = jnp.sum(jnp.where(te[:, None] == experts[None, :], next_of[None, :], 0), axis=1)

    of_tile = lambda v: jnp.sum(jnp.where(te[:, None] == experts[None, :], v[None, :], 0), axis=1)
    tile_rows = jnp.clip(of_tile(counts) - (tile_ids - of_tile(tile_end - ptiles)) * tmx, 0, tmx)
    tile_rows = jnp.where(tile_ids < nt, tile_rows, 0)
    return pos, te, nt.reshape(1), run_first, run_slot.astype(I32), run_next.astype(I32), tile_rows.astype(I32)


def kernel(x, mem, rel_bias_table, norm_mix, w_in, attn_sinks, conv_w, conv_b, gate_bias_i, gate_bias_f, mlstm_norm, w_out, norm_cross, norm_mem, w_cq, w_ck, w_cv, w_co, norm_moe, w_router_group, b_router_group, w_router_expert, b_router_expert, w_exp_gate, w_exp_up, w_exp_down, norm_final):
    B, S, _ = x.shape
    T = B * S
    depth = w_in.shape[0]
    x2d = x.reshape(T, D_MODEL)
    mem2d = mem.reshape(B * N_MEM, D_MODEL)
    bias = _band_bias(rel_bias_table)

    tmx = min(TM_EXPERT, T)
    n_tiles_max = (T * TOP_K) // tmx + N_EXPERTS
    tm_rows = min(TM_ROWDMA, T)

    assert depth == 1, "the final combine is fused with the final norm: single layer only"
    l = 0
    gb = jnp.concatenate([gate_bias_i[l], gate_bias_f[l]]).astype(F32)
    gbias_col = jnp.pad(gb, (0, LANES - GATE_ROWS))[None, :]
    qa, kva, qkb, vb, ob, gc, gr, xs_zeroed = _inproj(x2d, norm_mix[l][None, :], w_in[l], gbias_col, B, S,
                                                      n_tiles_max * tmx, tmx)

    per_seq = lambda a: a.reshape(B, S, a.shape[-1])
    mix = _seqmix(per_seq(qkb), per_seq(vb), per_seq(ob), per_seq(gc), gr, per_seq(qa), per_seq(kva),
                  conv_w[l][:, 0, :].astype(F32), conv_b[l][None, :].astype(F32),
                  mlstm_norm[l][None, :].astype(F32), bias, attn_sinks[l].astype(F32)).reshape(T, W_A_Q + W_B)

    ck, cv = _memkv(mem2d, norm_mem[l][None, :], w_ck[l].astype(BF16), w_cv[l].astype(BF16), B)

    wr = jnp.pad(jnp.concatenate([w_router_expert[l], w_router_group[l]], axis=1),
                 ((0, 0), (0, LANES - N_EXPERTS - N_GROUPS))).astype(BF16)
    br = jnp.pad(jnp.concatenate([b_router_expert[l], b_router_group[l]]),
                 (0, LANES - N_EXPERTS - N_GROUPS)).astype(F32)[None, :]
    x2, hz_packed, route, route_rows, counts = _mid(
        x2d, mix, w_out[l].astype(BF16), norm_cross[l][None, :], w_cq[l].astype(BF16), ck, cv,
        w_co[l].astype(BF16), norm_moe[l][None, :], wr, br, B, S)

    pos, te, nt, run_first, run_slot, run_next, tile_rows = _dispatch_plan(route_rows, counts, tmx, n_tiles_max,
                                                                           tm_rows)
    xs = _dispatch(hz_packed, pos, xs_zeroed)
    ys = _experts(xs, w_exp_gate[l], w_exp_up[l], w_exp_down[l], te, nt, run_first, run_slot, run_next, tile_rows,
                  tmx)
    out = _final(x2, ys, pos, route, norm_final[None, :])
    return out.reshape(B, S, D_MODEL)
```
